```python
import math
import jax, jax.numpy as jnp
from jax import lax
import numpy as np

D_MODEL = 1024
BATCH = 16
SEQ = 4096
DEPTH = 4

N_MIXERS = 4
D_FF = 4 * D_MODEL
EPS = 1e-6

SSM_GROUP = 16
SSM_GROUPS = D_MODEL // SSM_GROUP
SSM_STATE = 64
DT_MIN = 1e-3
DT_MAX = 1e-1

CONV_WIDTH = 31

GMLP_CHUNK = 128
GMLP_HEADS = 4
GMLP_WIDTH = D_MODEL

ATT_CONFIGS = ((128, 1), (512, 4), (2048, 16))
ATT_GROUPS = len(ATT_CONFIGS)
ATT_HEADS = 8
HEAD_DIM = 64
ATT_GROUP_WIDTH = ATT_HEADS * HEAD_DIM

kernel_name = "interleaved_s5_conv_gmlp_dilated_attn_trunk"


def _rmsnorm(x, g):
    xf = x.astype(jnp.float32)
    y = xf * lax.rsqrt(jnp.mean(xf * xf, axis=-1, keepdims=True) + EPS)
    return (y * g.astype(jnp.float32)).astype(x.dtype)


def _layernorm(x, g, b):
    xf = x.astype(jnp.float32)
    mu = jnp.mean(xf, axis=-1, keepdims=True)
    var = jnp.mean(jnp.square(xf - mu), axis=-1, keepdims=True)
    y = (xf - mu) * lax.rsqrt(var + EPS)
    return (y * g.astype(jnp.float32) + b.astype(jnp.float32)).astype(x.dtype)


def _s5_mixer(h, a_re, a_im, b_re, b_im, c_re, c_im, d_skip, log_dt, w_glu):
    bsz, s, _ = h.shape
    f32 = jnp.float32
    u = h.astype(f32).reshape(bsz, s, SSM_GROUPS, SSM_GROUP)
    a = lax.complex(a_re.astype(f32), a_im.astype(f32))
    dt = jnp.exp(log_dt.astype(f32))[:, None]
    a_bar = jnp.exp(a * dt)
    b_mat = lax.complex(b_re.astype(f32), b_im.astype(f32))
    b_bar = ((a_bar - 1.0) / a)[..., None] * b_mat
    bu = jnp.einsum('bsgp,gnp->sbgn', u.astype(jnp.complex64), b_bar)
    a_elems = jnp.broadcast_to(a_bar[None, None], (s, 1, SSM_GROUPS, SSM_STATE))

    def combine(left, right):
        a_l, b_l = left
        a_r, b_r = right
        return a_r * a_l, a_r * b_l + b_r

    _, states = lax.associative_scan(combine, (a_elems, bu), axis=0)
    c_mat = lax.complex(c_re.astype(f32), c_im.astype(f32))
    y = jnp.real(jnp.einsum('sbgn,gpn->bsgp', states, c_mat))
    y = y + d_skip.astype(f32).reshape(SSM_GROUPS, SSM_GROUP) * u
    y = jax.nn.gelu(y.reshape(bsz, s, D_MODEL)).astype(h.dtype)
    z = y @ w_glu
    return z[..., :D_MODEL] * jax.nn.sigmoid(z[..., D_MODEL:])


def _conv_mixer(h, w_pw1, b_pw1, w_dw, b_dw, ln_g, ln_b, w_pw2, b_pw2):
    z = h @ w_pw1 + b_pw1
    z = z[..., :D_MODEL] * jax.nn.sigmoid(z[..., D_MODEL:])
    y = lax.conv_general_dilated(
        z, w_dw[:, None, :].astype(z.dtype), window_strides=(1,),
        padding=((CONV_WIDTH - 1, 0),),
        dimension_numbers=('NWC', 'WIO', 'NWC'),
        feature_group_count=D_MODEL) + b_dw
    y = jax.nn.silu(_layernorm(y, ln_g, ln_b))
    return y @ w_pw2 + b_pw2


def _gmlp_mixer(h, w_in, ln_g, ln_b, w_s, b_s, w_out):
    bsz, s, _ = h.shape
    z = jax.nn.gelu(h @ w_in)
    u, v = z[..., :GMLP_WIDTH], z[..., GMLP_WIDTH:]
    v = _layernorm(v, ln_g, ln_b)
    n_chunks = s // GMLP_CHUNK
    v = v.reshape(bsz, n_chunks, GMLP_CHUNK, GMLP_HEADS, GMLP_WIDTH // GMLP_HEADS)
    causal = jnp.tril(jnp.ones((GMLP_CHUNK, GMLP_CHUNK), dtype=bool))
    ws = jnp.where(causal[None], w_s, 0.0)
    v = jnp.einsum('hts,bcshe->bcthe', ws, v) + b_s.T[None, None, :, :, None]
    v = v.reshape(bsz, s, GMLP_WIDTH)
    return (u * v) @ w_out


def _dilated_window_attention(q, k, v, window, dil):
    bsz, s, nh, hd = q.shape
    steps = window // dil
    blk = steps
    span = dil * blk
    s_pad = -(-s // span) * span
    pad = ((0, 0), (0, s_pad - s), (0, 0), (0, 0))
    nb = s_pad // span

    def split(t):
        t = jnp.pad(t, pad)
        return t.reshape(bsz, nb, blk, dil, nh, hd).transpose(0, 3, 1, 2, 4, 5)

    def with_prev(t):
        prev = jnp.pad(t, ((0, 0), (0, 0), (1, 0), (0, 0), (0, 0), (0, 0)))[:, :, :-1]
        return jnp.concatenate([prev, t], axis=3)

    qb = split(q)
    kk = with_prev(split(k))
    vv = with_prev(split(v))
    scores = jnp.einsum('brnihd,brnjhd->brnhij', qb, kk,
                        preferred_element_type=jnp.float32) * (HEAD_DIM ** -0.5)
    i_idx = jnp.arange(blk)[:, None]
    j_idx = jnp.arange(2 * blk)[None, :]
    dist = i_idx + blk - j_idx
    band = (dist >= 0) & (dist <= steps)
    has_prev = (jnp.arange(nb) > 0)[:, None, None]
    valid = band[None] & (has_prev | (j_idx >= blk)[None])
    scores = jnp.where(valid[None, None, :, None], scores, -jnp.inf)
    lse = jax.nn.logsumexp(scores, axis=-1)
    probs = jnp.exp(scores - lse[..., None])
    out = jnp.einsum('brnhij,brnjhd->brnihd', probs, vv.astype(jnp.float32))
    out = out.transpose(0, 2, 3, 1, 4, 5).reshape(bsz, s_pad, nh, hd)[:, :s]
    lse = lse.transpose(0, 2, 4, 1, 3).reshape(bsz, s_pad, nh)[:, :s]
    return out, lse


def _attention_mixer(h, w_qkv, w_o):
    bsz, s, _ = h.shape
    qkv = (h @ w_qkv).reshape(bsz, s, 3, ATT_GROUPS, ATT_HEADS, HEAD_DIM)
    outs, lses = [], []
    for g, (window, dil) in enumerate(ATT_CONFIGS):
        o, l = _dilated_window_attention(qkv[:, :, 0, g], qkv[:, :, 1, g], qkv[:, :, 2, g], window, dil)
        outs.append(o)
        lses.append(l)
    outs = jnp.stack(outs, axis=0)
    weights = jax.nn.softmax(jnp.stack(lses, axis=0), axis=0)
    merged = jnp.sum(weights[..., None] * outs, axis=0)
    return merged.reshape(bsz, s, ATT_GROUP_WIDTH).astype(h.dtype) @ w_o


def _mlp(h, w_in, w_out):
    return jnp.square(jax.nn.relu(h @ w_in)) @ w_out


def _n_layers_of(m):
    return len(range(m, DEPTH, N_MIXERS))


def _fwd_setup_inputs(seed: int = 0) -> dict:
    key = jax.random.key(seed)
    ks = jax.random.split(key, 32)
    nrm = jax.random.normal
    f32 = jnp.float32
    D, G, N, P = D_MODEL, SSM_GROUPS, SSM_STATE, SSM_GROUP
    nA, nB, nC, nD = (_n_layers_of(m) for m in range(N_MIXERS))
    E, T = GMLP_WIDTH, GMLP_CHUNK
    qkv_width = 3 * ATT_GROUPS * ATT_GROUP_WIDTH
    n_idx = jnp.arange(N, dtype=f32)
    return {
        "x": nrm(ks[0], (BATCH, SEQ, D), f32),
        "norm_mix": 1.0 + 0.02 * nrm(ks[1], (DEPTH, D), f32),
        "norm_mlp": 1.0 + 0.02 * nrm(ks[2], (DEPTH, D), f32),
        "norm_final": 1.0 + 0.02 * nrm(ks[3], (D,), f32),
        "ssm_a_re": -0.5 + 0.01 * nrm(ks[4], (nA, G, N), f32),
        "ssm_a_im": math.pi * n_idx + 0.01 * nrm(ks[5], (nA, G, N), f32),
        "ssm_b_re": nrm(ks[6], (nA, G, N, P), f32) * (2 * P) ** -0.5,
        "ssm_b_im": nrm(ks[7], (nA, G, N, P), f32) * (2 * P) ** -0.5,
        "ssm_c_re": nrm(ks[8], (nA, G, P, N), f32) * (2 * N) ** -0.5,
        "ssm_c_im": nrm(ks[9], (nA, G, P, N), f32) * (2 * N) ** -0.5,
        "ssm_d": nrm(ks[10], (nA, D), f32),
        "ssm_log_dt": jax.random.uniform(ks[11], (nA, G), f32, math.log(DT_MIN), math.log(DT_MAX)),
        "ssm_w_glu": nrm(ks[12], (nA, D, 2 * D), f32) * D ** -0.5,
        "conv_w_pw1": nrm(ks[13], (nB, D, 2 * D), f32) * D ** -0.5,
        "conv_b_pw1": 0.01 * nrm(ks[14], (nB, 2 * D), f32),
        "conv_w_dw": nrm(ks[15], (nB, CONV_WIDTH, D), f32) * CONV_WIDTH ** -0.5,
        "conv_b_dw": 0.01 * nrm(ks[16], (nB, D), f32),
        "conv_ln_g": 1.0 + 0.02 * nrm(ks[17], (nB, D), f32),
        "conv_ln_b": 0.01 * nrm(ks[18], (nB, D), f32),
        "conv_w_pw2": nrm(ks[19], (nB, D, D), f32) * D ** -0.5,
        "conv_b_pw2": 0.01 * nrm(ks[20], (nB, D), f32),
        "gmlp_w_in": nrm(ks[21], (nC, D, 2 * E), f32) * D ** -0.5,
        "gmlp_ln_g": 1.0 + 0.02 * nrm(ks[22], (nC, E), f32),
        "gmlp_ln_b": 0.01 * nrm(ks[23], (nC, E), f32),
        "gmlp_w_s": nrm(ks[24], (nC, GMLP_HEADS, T, T), f32) * T ** -0.5,
        "gmlp_b_s": 1.0 + 0.02 * nrm(ks[25], (nC, GMLP_HEADS, T), f32),
        "gmlp_w_out": nrm(ks[26], (nC, E, D), f32) * E ** -0.5,
        "attn_w_qkv": nrm(ks[27], (nD, D, qkv_width), f32) * D ** -0.5,
        "attn_w_o": nrm(ks[28], (nD, ATT_GROUP_WIDTH, D), f32) * ATT_GROUP_WIDTH ** -0.5,
        "mlp_w_in": nrm(ks[29], (DEPTH, D, D_FF), f32) * D ** -0.5,
        "mlp_w_out": nrm(ks[30], (DEPTH, D_FF, D), f32) * D_FF ** -0.5,
    }


def _fwd_reference(x, norm_mix, norm_mlp, norm_final,
              ssm_a_re, ssm_a_im, ssm_b_re, ssm_b_im, ssm_c_re, ssm_c_im, ssm_d, ssm_log_dt, ssm_w_glu,
              conv_w_pw1, conv_b_pw1, conv_w_dw, conv_b_dw, conv_ln_g, conv_ln_b, conv_w_pw2, conv_b_pw2,
              gmlp_w_in, gmlp_ln_g, gmlp_ln_b, gmlp_w_s, gmlp_b_s, gmlp_w_out,
              attn_w_qkv, attn_w_o, mlp_w_in, mlp_w_out):
    for i in range(DEPTH):
        m, j = i % N_MIXERS, i // N_MIXERS
        h = _rmsnorm(x, norm_mix[i])
        if m == 0:
            y = _s5_mixer(h, ssm_a_re[j], ssm_a_im[j], ssm_b_re[j], ssm_b_im[j], ssm_c_re[j],
                          ssm_c_im[j], ssm_d[j], ssm_log_dt[j], ssm_w_glu[j])
        elif m == 1:
            y = _conv_mixer(h, conv_w_pw1[j], conv_b_pw1[j], conv_w_dw[j], conv_b_dw[j],
                            conv_ln_g[j], conv_ln_b[j], conv_w_pw2[j], conv_b_pw2[j])
        elif m == 2:
            y = _gmlp_mixer(h, gmlp_w_in[j], gmlp_ln_g[j], gmlp_ln_b[j], gmlp_w_s[j],
                            gmlp_b_s[j], gmlp_w_out[j])
        else:
            y = _attention_mixer(h, attn_w_qkv[j], attn_w_o[j])
        x = x + y.astype(x.dtype)
        x = x + _mlp(_rmsnorm(x, norm_mlp[i]), mlp_w_in[i], mlp_w_out[i]).astype(x.dtype)
    return _rmsnorm(x, norm_final)


import jax as _jax
import jax.numpy as _jnp

TWIN_FORMAT = 'train_step'
FWD_PARAMS = ['x', 'norm_mix', 'norm_mlp', 'norm_final', 'ssm_a_re', 'ssm_a_im', 'ssm_b_re', 'ssm_b_im', 'ssm_c_re', 'ssm_c_im', 'ssm_d', 'ssm_log_dt', 'ssm_w_glu', 'conv_w_pw1', 'conv_b_pw1', 'conv_w_dw', 'conv_b_dw', 'conv_ln_g', 'conv_ln_b', 'conv_w_pw2', 'conv_b_pw2', 'gmlp_w_in', 'gmlp_ln_g', 'gmlp_ln_b', 'gmlp_w_s', 'gmlp_b_s', 'gmlp_w_out', 'attn_w_qkv', 'attn_w_o', 'mlp_w_in', 'mlp_w_out']
TWIN_WEIGHTS = ['norm_mix', 'norm_mlp', 'norm_final', 'ssm_a_re', 'ssm_a_im', 'ssm_b_re', 'ssm_b_im', 'ssm_c_re', 'ssm_c_im', 'ssm_d', 'ssm_log_dt', 'ssm_w_glu', 'conv_w_pw1', 'conv_b_pw1', 'conv_w_dw', 'conv_b_dw', 'conv_ln_g', 'conv_ln_b', 'conv_w_pw2', 'conv_b_pw2', 'gmlp_w_in', 'gmlp_ln_g', 'gmlp_ln_b', 'gmlp_w_s', 'gmlp_b_s', 'gmlp_w_out', 'attn_w_qkv', 'attn_w_o', 'mlp_w_in', 'mlp_w_out']
TWIN_DIFF_INPUT = 'x'
TWIN_INPUTS = ['x', 'norm_mix', 'norm_mlp', 'norm_final', 'ssm_a_re', 'ssm_a_im', 'ssm_b_re', 'ssm_b_im', 'ssm_c_re', 'ssm_c_im', 'ssm_d', 'ssm_log_dt', 'ssm_w_glu', 'conv_w_pw1', 'conv_b_pw1', 'conv_w_dw', 'conv_b_dw', 'conv_ln_g', 'conv_ln_b', 'conv_w_pw2', 'conv_b_pw2', 'gmlp_w_in', 'gmlp_ln_g', 'gmlp_ln_b', 'gmlp_w_s', 'gmlp_b_s', 'gmlp_w_out', 'attn_w_qkv', 'attn_w_o', 'mlp_w_in', 'mlp_w_out', 'loss_target', 'm_norm_mix', 'm_norm_mlp', 'm_norm_final', 'm_ssm_a_re', 'm_ssm_a_im', 'm_ssm_b_re', 'm_ssm_b_im', 'm_ssm_c_re', 'm_ssm_c_im', 'm_ssm_d', 'm_ssm_log_dt', 'm_ssm_w_glu', 'm_conv_w_pw1', 'm_conv_b_pw1', 'm_conv_w_dw', 'm_conv_b_dw', 'm_conv_ln_g', 'm_conv_ln_b', 'm_conv_w_pw2', 'm_conv_b_pw2', 'm_gmlp_w_in', 'm_gmlp_ln_g', 'm_gmlp_ln_b', 'm_gmlp_w_s', 'm_gmlp_b_s', 'm_gmlp_w_out', 'm_attn_w_qkv', 'm_attn_w_o', 'm_mlp_w_in', 'm_mlp_w_out', 'v_norm_mix', 'v_norm_mlp', 'v_norm_final', 'v_ssm_a_re', 'v_ssm_a_im', 'v_ssm_b_re', 'v_ssm_b_im', 'v_ssm_c_re', 'v_ssm_c_im', 'v_ssm_d', 'v_ssm_log_dt', 'v_ssm_w_glu', 'v_conv_w_pw1', 'v_conv_b_pw1', 'v_conv_w_dw', 'v_conv_b_dw', 'v_conv_ln_g', 'v_conv_ln_b', 'v_conv_w_pw2', 'v_conv_b_pw2', 'v_gmlp_w_in', 'v_gmlp_ln_g', 'v_gmlp_ln_b', 'v_gmlp_w_s', 'v_gmlp_b_s', 'v_gmlp_w_out', 'v_attn_w_qkv', 'v_attn_w_o', 'v_mlp_w_in', 'v_mlp_w_out']
TWIN_OUTPUTS = ['loss', 'grad_x', 'grad_norm_mix', 'grad_norm_mlp', 'grad_norm_final', 'grad_ssm_a_re', 'grad_ssm_a_im', 'grad_ssm_b_re', 'grad_ssm_b_im', 'grad_ssm_c_re', 'grad_ssm_c_im', 'grad_ssm_d', 'grad_ssm_log_dt', 'grad_ssm_w_glu', 'grad_conv_w_pw1', 'grad_conv_b_pw1', 'grad_conv_w_dw', 'grad_conv_b_dw', 'grad_conv_ln_g', 'grad_conv_ln_b', 'grad_conv_w_pw2', 'grad_conv_b_pw2', 'grad_gmlp_w_in', 'grad_gmlp_ln_g', 'grad_gmlp_ln_b', 'grad_gmlp_w_s', 'grad_gmlp_b_s', 'grad_gmlp_w_out', 'grad_attn_w_qkv', 'grad_attn_w_o', 'grad_mlp_w_in', 'grad_mlp_w_out', 'delta_norm_mix', 'delta_norm_mlp', 'delta_norm_final', 'delta_ssm_a_re', 'delta_ssm_a_im', 'delta_ssm_b_re', 'delta_ssm_b_im', 'delta_ssm_c_re', 'delta_ssm_c_im', 'delta_ssm_d', 'delta_ssm_log_dt', 'delta_ssm_w_glu', 'delta_conv_w_pw1', 'delta_conv_b_pw1', 'delta_conv_w_dw', 'delta_conv_b_dw', 'delta_conv_ln_g', 'delta_conv_ln_b', 'delta_conv_w_pw2', 'delta_conv_b_pw2', 'delta_gmlp_w_in', 'delta_gmlp_ln_g', 'delta_gmlp_ln_b', 'delta_gmlp_w_s', 'delta_gmlp_b_s', 'delta_gmlp_w_out', 'delta_attn_w_qkv', 'delta_attn_w_o', 'delta_mlp_w_in', 'delta_mlp_w_out', 'new_m_norm_mix', 'new_m_norm_mlp', 'new_m_norm_final', 'new_m_ssm_a_re', 'new_m_ssm_a_im', 'new_m_ssm_b_re', 'new_m_ssm_b_im', 'new_m_ssm_c_re', 'new_m_ssm_c_im', 'new_m_ssm_d', 'new_m_ssm_log_dt', 'new_m_ssm_w_glu', 'new_m_conv_w_pw1', 'new_m_conv_b_pw1', 'new_m_conv_w_dw', 'new_m_conv_b_dw', 'new_m_conv_ln_g', 'new_m_conv_ln_b', 'new_m_conv_w_pw2', 'new_m_conv_b_pw2', 'new_m_gmlp_w_in', 'new_m_gmlp_ln_g', 'new_m_gmlp_ln_b', 'new_m_gmlp_w_s', 'new_m_gmlp_b_s', 'new_m_gmlp_w_out', 'new_m_attn_w_qkv', 'new_m_attn_w_o', 'new_m_mlp_w_in', 'new_m_mlp_w_out', 'new_v_norm_mix', 'new_v_norm_mlp', 'new_v_norm_final', 'new_v_ssm_a_re', 'new_v_ssm_a_im', 'new_v_ssm_b_re', 'new_v_ssm_b_im', 'new_v_ssm_c_re', 'new_v_ssm_c_im', 'new_v_ssm_d', 'new_v_ssm_log_dt', 'new_v_ssm_w_glu', 'new_v_conv_w_pw1', 'new_v_conv_b_pw1', 'new_v_conv_w_dw', 'new_v_conv_b_dw', 'new_v_conv_ln_g', 'new_v_conv_ln_b', 'new_v_conv_w_pw2', 'new_v_conv_b_pw2', 'new_v_gmlp_w_in', 'new_v_gmlp_ln_g', 'new_v_gmlp_ln_b', 'new_v_gmlp_w_s', 'new_v_gmlp_b_s', 'new_v_gmlp_w_out', 'new_v_attn_w_qkv', 'new_v_attn_w_o', 'new_v_mlp_w_in', 'new_v_mlp_w_out']
TWIN_LEAF_KINDS = {'loss': 'loss', 'grad_x': 'grad_x', 'grad_norm_mix': 'grad_w', 'grad_norm_mlp': 'grad_w', 'grad_norm_final': 'grad_w', 'grad_ssm_a_re': 'grad_w', 'grad_ssm_a_im': 'grad_w', 'grad_ssm_b_re': 'grad_w', 'grad_ssm_b_im': 'grad_w', 'grad_ssm_c_re': 'grad_w', 'grad_ssm_c_im': 'grad_w', 'grad_ssm_d': 'grad_w', 'grad_ssm_log_dt': 'grad_w', 'grad_ssm_w_glu': 'grad_w', 'grad_conv_w_pw1': 'grad_w', 'grad_conv_b_pw1': 'grad_w', 'grad_conv_w_dw': 'grad_w', 'grad_conv_b_dw': 'grad_w', 'grad_conv_ln_g': 'grad_w', 'grad_conv_ln_b': 'grad_w', 'grad_conv_w_pw2': 'grad_w', 'grad_conv_b_pw2': 'grad_w', 'grad_gmlp_w_in': 'grad_w', 'grad_gmlp_ln_g': 'grad_w', 'grad_gmlp_ln_b': 'grad_w', 'grad_gmlp_w_s': 'grad_w', 'grad_gmlp_b_s': 'grad_w', 'grad_gmlp_w_out': 'grad_w', 'grad_attn_w_qkv': 'grad_w', 'grad_attn_w_o': 'grad_w', 'grad_mlp_w_in': 'grad_w', 'grad_mlp_w_out': 'grad_w', 'delta_norm_mix': 'delta_w', 'delta_norm_mlp': 'delta_w', 'delta_norm_final': 'delta_w', 'delta_ssm_a_re': 'delta_w', 'delta_ssm_a_im': 'delta_w', 'delta_ssm_b_re': 'delta_w', 'delta_ssm_b_im': 'delta_w', 'delta_ssm_c_re': 'delta_w', 'delta_ssm_c_im': 'delta_w', 'delta_ssm_d': 'delta_w', 'delta_ssm_log_dt': 'delta_w', 'delta_ssm_w_glu': 'delta_w', 'delta_conv_w_pw1': 'delta_w', 'delta_conv_b_pw1': 'delta_w', 'delta_conv_w_dw': 'delta_w', 'delta_conv_b_dw': 'delta_w', 'delta_conv_ln_g': 'delta_w', 'delta_conv_ln_b': 'delta_w', 'delta_conv_w_pw2': 'delta_w', 'delta_conv_b_pw2': 'delta_w', 'delta_gmlp_w_in': 'delta_w', 'delta_gmlp_ln_g': 'delta_w', 'delta_gmlp_ln_b': 'delta_w', 'delta_gmlp_w_s': 'delta_w', 'delta_gmlp_b_s': 'delta_w', 'delta_gmlp_w_out': 'delta_w', 'delta_attn_w_qkv': 'delta_w', 'delta_attn_w_o': 'delta_w', 'delta_mlp_w_in': 'delta_w', 'delta_mlp_w_out': 'delta_w', 'new_m_norm_mix': 'new_m', 'new_m_norm_mlp': 'new_m', 'new_m_norm_final': 'new_m', 'new_m_ssm_a_re': 'new_m', 'new_m_ssm_a_im': 'new_m', 'new_m_ssm_b_re': 'new_m', 'new_m_ssm_b_im': 'new_m', 'new_m_ssm_c_re': 'new_m', 'new_m_ssm_c_im': 'new_m', 'new_m_ssm_d': 'new_m', 'new_m_ssm_log_dt': 'new_m', 'new_m_ssm_w_glu': 'new_m', 'new_m_conv_w_pw1': 'new_m', 'new_m_conv_b_pw1': 'new_m', 'new_m_conv_w_dw': 'new_m', 'new_m_conv_b_dw': 'new_m', 'new_m_conv_ln_g': 'new_m', 'new_m_conv_ln_b': 'new_m', 'new_m_conv_w_pw2': 'new_m', 'new_m_conv_b_pw2': 'new_m', 'new_m_gmlp_w_in': 'new_m', 'new_m_gmlp_ln_g': 'new_m', 'new_m_gmlp_ln_b': 'new_m', 'new_m_gmlp_w_s': 'new_m', 'new_m_gmlp_b_s': 'new_m', 'new_m_gmlp_w_out': 'new_m', 'new_m_attn_w_qkv': 'new_m', 'new_m_attn_w_o': 'new_m', 'new_m_mlp_w_in': 'new_m', 'new_m_mlp_w_out': 'new_m', 'new_v_norm_mix': 'new_v', 'new_v_norm_mlp': 'new_v', 'new_v_norm_final': 'new_v', 'new_v_ssm_a_re': 'new_v', 'new_v_ssm_a_im': 'new_v', 'new_v_ssm_b_re': 'new_v', 'new_v_ssm_b_im': 'new_v', 'new_v_ssm_c_re': 'new_v', 'new_v_ssm_c_im': 'new_v', 'new_v_ssm_d': 'new_v', 'new_v_ssm_log_dt': 'new_v', 'new_v_ssm_w_glu': 'new_v', 'new_v_conv_w_pw1': 'new_v', 'new_v_conv_b_pw1': 'new_v', 'new_v_conv_w_dw': 'new_v', 'new_v_conv_b_dw': 'new_v', 'new_v_conv_ln_g': 'new_v', 'new_v_conv_ln_b': 'new_v', 'new_v_conv_w_pw2': 'new_v', 'new_v_conv_b_pw2': 'new_v', 'new_v_gmlp_w_in': 'new_v', 'new_v_gmlp_ln_g': 'new_v', 'new_v_gmlp_ln_b': 'new_v', 'new_v_gmlp_w_s': 'new_v', 'new_v_gmlp_b_s': 'new_v', 'new_v_gmlp_w_out': 'new_v', 'new_v_attn_w_qkv': 'new_v', 'new_v_attn_w_o': 'new_v', 'new_v_mlp_w_in': 'new_v', 'new_v_mlp_w_out': 'new_v'}


def _forward(args):
    return _fwd_reference(*[args[k] for k in FWD_PARAMS])


def _output_shape():
    out = _jax.eval_shape(lambda: _forward(_fwd_setup_inputs(0)))
    return out.shape, out.dtype

N_MICROBATCH = 1
ADAM_LR = 0.001
ADAM_B1 = 0.9
ADAM_B2 = 0.999
ADAM_EPS = 1e-08
ADAM_WD = 0.01
ADAM_STEP = 10
PER_EXAMPLE_BATCH_AXIS = {'x': 0, 'loss_target': 0}
SHARED_INPUTS = []
_WEIGHT_DTYPES = {'norm_mix': _jnp.float32, 'norm_mlp': _jnp.float32, 'norm_final': _jnp.float32, 'ssm_a_re': _jnp.float32, 'ssm_a_im': _jnp.float32, 'ssm_b_re': _jnp.float32, 'ssm_b_im': _jnp.float32, 'ssm_c_re': _jnp.float32, 'ssm_c_im': _jnp.float32, 'ssm_d': _jnp.float32, 'ssm_log_dt': _jnp.float32, 'ssm_w_glu': _jnp.float32, 'conv_w_pw1': _jnp.float32, 'conv_b_pw1': _jnp.float32, 'conv_w_dw': _jnp.float32, 'conv_b_dw': _jnp.float32, 'conv_ln_g': _jnp.float32, 'conv_ln_b': _jnp.float32, 'conv_w_pw2': _jnp.float32, 'conv_b_pw2': _jnp.float32, 'gmlp_w_in': _jnp.float32, 'gmlp_ln_g': _jnp.float32, 'gmlp_ln_b': _jnp.float32, 'gmlp_w_s': _jnp.float32, 'gmlp_b_s': _jnp.float32, 'gmlp_w_out': _jnp.float32, 'attn_w_qkv': _jnp.float32, 'attn_w_o': _jnp.float32, 'mlp_w_in': _jnp.float32, 'mlp_w_out': _jnp.float32}
MOMENT_SCALE = {'norm_mix': 1.272906e-01, 'norm_mlp': 1.852125e-01, 'norm_final': 6.605171e+01, 'ssm_a_re': 7.468558e-03, 'ssm_a_im': 7.564631e-03, 'ssm_b_re': 3.958143e-03, 'ssm_b_im': 3.930943e-03, 'ssm_c_re': 7.789463e-03, 'ssm_c_im': 7.672637e-03, 'ssm_d': 1.306292e-01, 'ssm_log_dt': 5.072862e+00, 'ssm_w_glu': 8.537206e-02, 'conv_w_pw1': 8.365691e-02, 'conv_b_pw1': 1.152837e-01, 'conv_w_dw': 1.120682e-01, 'conv_b_dw': 2.459268e-01, 'conv_ln_g': 1.453633e-01, 'conv_ln_b': 1.615369e-01, 'conv_w_pw2': 1.135008e-01, 'conv_b_pw2': 2.788087e-01, 'gmlp_w_in': 8.686374e-02, 'gmlp_ln_g': 5.765163e-02, 'gmlp_ln_b': 5.677108e-02, 'gmlp_w_s': 7.744856e-02, 'gmlp_b_s': 1.165478e-01, 'gmlp_w_out': 1.189684e-01, 'attn_w_qkv': 2.422647e-02, 'attn_w_o': 4.524828e-02, 'mlp_w_in': 9.538579e-02, 'mlp_w_out': 1.927644e-01}


def _to_microbatches(a, axis):
    t = _jnp.moveaxis(a, axis, 0)
    t = t.reshape((N_MICROBATCH, t.shape[0] // N_MICROBATCH) + t.shape[1:])
    return _jnp.moveaxis(t, 1, axis + 1)


def setup_inputs(seed: int = 0) -> dict:
    inp = _fwd_setup_inputs(seed)
    key = _jax.random.fold_in(_jax.random.key(seed), 7919)
    shape, _ = _output_shape()
    out = dict(inp)
    out["loss_target"] = _jax.random.normal(_jax.random.fold_in(key, 0), shape, _jnp.float32)
    for i, name in enumerate(TWIN_WEIGHTS):
        w = inp[name].astype(_jnp.float32)
        if MOMENT_SCALE is None:
            s = _jnp.sqrt(_jnp.mean(_jnp.square(w)) + 1e-30)
        else:
            s = MOMENT_SCALE[name]
        km, kv = _jax.random.split(_jax.random.fold_in(key, i + 1))
        out[name] = w
        out["m_" + name] = s * _jax.random.normal(km, w.shape, _jnp.float32)
        out["v_" + name] = (s * s) * _jax.random.uniform(kv, w.shape, _jnp.float32, 0.5, 1.5)
    if N_MICROBATCH > 1:
        for name, axis in PER_EXAMPLE_BATCH_AXIS.items():
            out[name] = _to_microbatches(out[name], axis)
    return {'x': out['x'], 'norm_mix': out['norm_mix'], 'norm_mlp': out['norm_mlp'], 'norm_final': out['norm_final'], 'ssm_a_re': out['ssm_a_re'], 'ssm_a_im': out['ssm_a_im'], 'ssm_b_re': out['ssm_b_re'], 'ssm_b_im': out['ssm_b_im'], 'ssm_c_re': out['ssm_c_re'], 'ssm_c_im': out['ssm_c_im'], 'ssm_d': out['ssm_d'], 'ssm_log_dt': out['ssm_log_dt'], 'ssm_w_glu': out['ssm_w_glu'], 'conv_w_pw1': out['conv_w_pw1'], 'conv_b_pw1': out['conv_b_pw1'], 'conv_w_dw': out['conv_w_dw'], 'conv_b_dw': out['conv_b_dw'], 'conv_ln_g': out['conv_ln_g'], 'conv_ln_b': out['conv_ln_b'], 'conv_w_pw2': out['conv_w_pw2'], 'conv_b_pw2': out['conv_b_pw2'], 'gmlp_w_in': out['gmlp_w_in'], 'gmlp_ln_g': out['gmlp_ln_g'], 'gmlp_ln_b': out['gmlp_ln_b'], 'gmlp_w_s': out['gmlp_w_s'], 'gmlp_b_s': out['gmlp_b_s'], 'gmlp_w_out': out['gmlp_w_out'], 'attn_w_qkv': out['attn_w_qkv'], 'attn_w_o': out['attn_w_o'], 'mlp_w_in': out['mlp_w_in'], 'mlp_w_out': out['mlp_w_out'], 'loss_target': out['loss_target'], 'm_norm_mix': out['m_norm_mix'], 'm_norm_mlp': out['m_norm_mlp'], 'm_norm_final': out['m_norm_final'], 'm_ssm_a_re': out['m_ssm_a_re'], 'm_ssm_a_im': out['m_ssm_a_im'], 'm_ssm_b_re': out['m_ssm_b_re'], 'm_ssm_b_im': out['m_ssm_b_im'], 'm_ssm_c_re': out['m_ssm_c_re'], 'm_ssm_c_im': out['m_ssm_c_im'], 'm_ssm_d': out['m_ssm_d'], 'm_ssm_log_dt': out['m_ssm_log_dt'], 'm_ssm_w_glu': out['m_ssm_w_glu'], 'm_conv_w_pw1': out['m_conv_w_pw1'], 'm_conv_b_pw1': out['m_conv_b_pw1'], 'm_conv_w_dw': out['m_conv_w_dw'], 'm_conv_b_dw': out['m_conv_b_dw'], 'm_conv_ln_g': out['m_conv_ln_g'], 'm_conv_ln_b': out['m_conv_ln_b'], 'm_conv_w_pw2': out['m_conv_w_pw2'], 'm_conv_b_pw2': out['m_conv_b_pw2'], 'm_gmlp_w_in': out['m_gmlp_w_in'], 'm_gmlp_ln_g': out['m_gmlp_ln_g'], 'm_gmlp_ln_b': out['m_gmlp_ln_b'], 'm_gmlp_w_s': out['m_gmlp_w_s'], 'm_gmlp_b_s': out['m_gmlp_b_s'], 'm_gmlp_w_out': out['m_gmlp_w_out'], 'm_attn_w_qkv': out['m_attn_w_qkv'], 'm_attn_w_o': out['m_attn_w_o'], 'm_mlp_w_in': out['m_mlp_w_in'], 'm_mlp_w_out': out['m_mlp_w_out'], 'v_norm_mix': out['v_norm_mix'], 'v_norm_mlp': out['v_norm_mlp'], 'v_norm_final': out['v_norm_final'], 'v_ssm_a_re': out['v_ssm_a_re'], 'v_ssm_a_im': out['v_ssm_a_im'], 'v_ssm_b_re': out['v_ssm_b_re'], 'v_ssm_b_im': out['v_ssm_b_im'], 'v_ssm_c_re': out['v_ssm_c_re'], 'v_ssm_c_im': out['v_ssm_c_im'], 'v_ssm_d': out['v_ssm_d'], 'v_ssm_log_dt': out['v_ssm_log_dt'], 'v_ssm_w_glu': out['v_ssm_w_glu'], 'v_conv_w_pw1': out['v_conv_w_pw1'], 'v_conv_b_pw1': out['v_conv_b_pw1'], 'v_conv_w_dw': out['v_conv_w_dw'], 'v_conv_b_dw': out['v_conv_b_dw'], 'v_conv_ln_g': out['v_conv_ln_g'], 'v_conv_ln_b': out['v_conv_ln_b'], 'v_conv_w_pw2': out['v_conv_w_pw2'], 'v_conv_b_pw2': out['v_conv_b_pw2'], 'v_gmlp_w_in': out['v_gmlp_w_in'], 'v_gmlp_ln_g': out['v_gmlp_ln_g'], 'v_gmlp_ln_b': out['v_gmlp_ln_b'], 'v_gmlp_w_s': out['v_gmlp_w_s'], 'v_gmlp_b_s': out['v_gmlp_b_s'], 'v_gmlp_w_out': out['v_gmlp_w_out'], 'v_attn_w_qkv': out['v_attn_w_qkv'], 'v_attn_w_o': out['v_attn_w_o'], 'v_mlp_w_in': out['v_mlp_w_in'], 'v_mlp_w_out': out['v_mlp_w_out']}


def _loss(weights, diff, rest, loss_target):
    with _jax.named_scope("forward"):
        args = {**rest, TWIN_DIFF_INPUT: diff, **{k: w.astype(_WEIGHT_DTYPES[k]) for k, w in weights.items()}}
        y = _forward(args)
    with _jax.named_scope("loss_head"):
        err = _jnp.square(y.astype(_jnp.float32) - loss_target)
        return 0.5 * _jnp.sum(_jnp.mean(err, axis=-1)) if err.ndim else 0.5 * err


def _adamw(w, g, m, v):
    m = ADAM_B1 * m + (1.0 - ADAM_B1) * g
    v = ADAM_B2 * v + (1.0 - ADAM_B2) * _jnp.square(g)
    m_hat = m / (1.0 - ADAM_B1 ** ADAM_STEP)
    v_hat = v / (1.0 - ADAM_B2 ** ADAM_STEP)
    delta = -ADAM_LR * (m_hat / (_jnp.sqrt(v_hat) + ADAM_EPS) + ADAM_WD * w)
    return delta, m, v


def reference(x, norm_mix, norm_mlp, norm_final, ssm_a_re, ssm_a_im, ssm_b_re, ssm_b_im, ssm_c_re, ssm_c_im, ssm_d, ssm_log_dt, ssm_w_glu, conv_w_pw1, conv_b_pw1, conv_w_dw, conv_b_dw, conv_ln_g, conv_ln_b, conv_w_pw2, conv_b_pw2, gmlp_w_in, gmlp_ln_g, gmlp_ln_b, gmlp_w_s, gmlp_b_s, gmlp_w_out, attn_w_qkv, attn_w_o, mlp_w_in, mlp_w_out, loss_target, m_norm_mix, m_norm_mlp, m_norm_final, m_ssm_a_re, m_ssm_a_im, m_ssm_b_re, m_ssm_b_im, m_ssm_c_re, m_ssm_c_im, m_ssm_d, m_ssm_log_dt, m_ssm_w_glu, m_conv_w_pw1, m_conv_b_pw1, m_conv_w_dw, m_conv_b_dw, m_conv_ln_g, m_conv_ln_b, m_conv_w_pw2, m_conv_b_pw2, m_gmlp_w_in, m_gmlp_ln_g, m_gmlp_ln_b, m_gmlp_w_s, m_gmlp_b_s, m_gmlp_w_out, m_attn_w_qkv, m_attn_w_o, m_mlp_w_in, m_mlp_w_out, v_norm_mix, v_norm_mlp, v_norm_final, v_ssm_a_re, v_ssm_a_im, v_ssm_b_re, v_ssm_b_im, v_ssm_c_re, v_ssm_c_im, v_ssm_d, v_ssm_log_dt, v_ssm_w_glu, v_conv_w_pw1, v_conv_b_pw1, v_conv_w_dw, v_conv_b_dw, v_conv_ln_g, v_conv_ln_b, v_conv_w_pw2, v_conv_b_pw2, v_gmlp_w_in, v_gmlp_ln_g, v_gmlp_ln_b, v_gmlp_w_s, v_gmlp_b_s, v_gmlp_w_out, v_attn_w_qkv, v_attn_w_o, v_mlp_w_in, v_mlp_w_out):
    given = dict(x=x, norm_mix=norm_mix, norm_mlp=norm_mlp, norm_final=norm_final, ssm_a_re=ssm_a_re, ssm_a_im=ssm_a_im, ssm_b_re=ssm_b_re, ssm_b_im=ssm_b_im, ssm_c_re=ssm_c_re, ssm_c_im=ssm_c_im, ssm_d=ssm_d, ssm_log_dt=ssm_log_dt, ssm_w_glu=ssm_w_glu, conv_w_pw1=conv_w_pw1, conv_b_pw1=conv_b_pw1, conv_w_dw=conv_w_dw, conv_b_dw=conv_b_dw, conv_ln_g=conv_ln_g, conv_ln_b=conv_ln_b, conv_w_pw2=conv_w_pw2, conv_b_pw2=conv_b_pw2, gmlp_w_in=gmlp_w_in, gmlp_ln_g=gmlp_ln_g, gmlp_ln_b=gmlp_ln_b, gmlp_w_s=gmlp_w_s, gmlp_b_s=gmlp_b_s, gmlp_w_out=gmlp_w_out, attn_w_qkv=attn_w_qkv, attn_w_o=attn_w_o, mlp_w_in=mlp_w_in, mlp_w_out=mlp_w_out, loss_target=loss_target, m_norm_mix=m_norm_mix, m_norm_mlp=m_norm_mlp, m_norm_final=m_norm_final, m_ssm_a_re=m_ssm_a_re, m_ssm_a_im=m_ssm_a_im, m_ssm_b_re=m_ssm_b_re, m_ssm_b_im=m_ssm_b_im, m_ssm_c_re=m_ssm_c_re, m_ssm_c_im=m_ssm_c_im, m_ssm_d=m_ssm_d, m_ssm_log_dt=m_ssm_log_dt, m_ssm_w_glu=m_ssm_w_glu, m_conv_w_pw1=m_conv_w_pw1, m_conv_b_pw1=m_conv_b_pw1, m_conv_w_dw=m_conv_w_dw, m_conv_b_dw=m_conv_b_dw, m_conv_ln_g=m_conv_ln_g, m_conv_ln_b=m_conv_ln_b, m_conv_w_pw2=m_conv_w_pw2, m_conv_b_pw2=m_conv_b_pw2, m_gmlp_w_in=m_gmlp_w_in, m_gmlp_ln_g=m_gmlp_ln_g, m_gmlp_ln_b=m_gmlp_ln_b, m_gmlp_w_s=m_gmlp_w_s, m_gmlp_b_s=m_gmlp_b_s, m_gmlp_w_out=m_gmlp_w_out, m_attn_w_qkv=m_attn_w_qkv, m_attn_w_o=m_attn_w_o, m_mlp_w_in=m_mlp_w_in, m_mlp_w_out=m_mlp_w_out, v_norm_mix=v_norm_mix, v_norm_mlp=v_norm_mlp, v_norm_final=v_norm_final, v_ssm_a_re=v_ssm_a_re, v_ssm_a_im=v_ssm_a_im, v_ssm_b_re=v_ssm_b_re, v_ssm_b_im=v_ssm_b_im, v_ssm_c_re=v_ssm_c_re, v_ssm_c_im=v_ssm_c_im, v_ssm_d=v_ssm_d, v_ssm_log_dt=v_ssm_log_dt, v_ssm_w_glu=v_ssm_w_glu, v_conv_w_pw1=v_conv_w_pw1, v_conv_b_pw1=v_conv_b_pw1, v_conv_w_dw=v_conv_w_dw, v_conv_b_dw=v_conv_b_dw, v_conv_ln_g=v_conv_ln_g, v_conv_ln_b=v_conv_ln_b, v_conv_w_pw2=v_conv_w_pw2, v_conv_b_pw2=v_conv_b_pw2, v_gmlp_w_in=v_gmlp_w_in, v_gmlp_ln_g=v_gmlp_ln_g, v_gmlp_ln_b=v_gmlp_ln_b, v_gmlp_w_s=v_gmlp_w_s, v_gmlp_b_s=v_gmlp_b_s, v_gmlp_w_out=v_gmlp_w_out, v_attn_w_qkv=v_attn_w_qkv, v_attn_w_o=v_attn_w_o, v_mlp_w_in=v_mlp_w_in, v_mlp_w_out=v_mlp_w_out)
    weights = {n: given[n] for n in TWIN_WEIGHTS}
    shared = {n: given[n] for n in SHARED_INPUTS}
    per_example = {n: given[n] for n in ['x']}
    grad_fn = _jax.value_and_grad(_loss, argnums=(0, 1))

    def one_microbatch(ex, loss_target):
        ex = dict(ex)
        diff = ex.pop(TWIN_DIFF_INPUT)
        return grad_fn(weights, diff, {**shared, **ex}, loss_target)

    if N_MICROBATCH == 1:
        loss, (grad_w, grad_x) = one_microbatch(per_example, given["loss_target"])
    else:
        def body(carry, xs):
            loss_sum, grad_sum = carry
            l_k, (gw_k, gx_k) = one_microbatch(xs[0], xs[1])
            with _jax.named_scope("update"):
                return (loss_sum + l_k, _jax.tree.map(_jnp.add, grad_sum, gw_k)), gx_k

        init = (_jnp.zeros((), _jnp.float32), _jax.tree.map(_jnp.zeros_like, weights))
        (loss, grad_w), grad_x = _jax.lax.scan(body, init, (per_example, given["loss_target"]))
    with _jax.named_scope("update"):
        delta_w, new_m, new_v = {}, {}, {}
        for n in TWIN_WEIGHTS:
            delta_w[n], new_m[n], new_v[n] = _adamw(weights[n], grad_w[n], given["m_" + n], given["v_" + n])
    return (loss, grad_x, *[grad_w[n] for n in TWIN_WEIGHTS], *[delta_w[n] for n in TWIN_WEIGHTS],
            *[new_m[n] for n in TWIN_WEIGHTS], *[new_v[n] for n in TWIN_WEIGHTS])
```

```python
import functools
import math

import jax
import jax.numpy as jnp
import numpy as np
from jax import lax
from jax.experimental import pallas as pl
from jax.experimental.pallas import tpu as pltpu

F32 = jnp.float32
BF16 = jnp.bfloat16
MESH = pl.DeviceIdType.MESH

EPS = 1e-6
SSM_GROUP = 16
SSM_STATE = 64
CONV_WIDTH = 31
CONV_HALO = 32
GMLP_CHUNK = 128
GMLP_HEADS = 4
ATT_DILS = (1, 4, 16)
ATT_BLK = 128
ATT_HEADS = 8
HEAD_DIM = 64
ATT_W = ATT_HEADS * HEAD_DIM
N_CHIPS = 4
ADAM_LR, ADAM_B1, ADAM_B2, ADAM_EPS, ADAM_WD, ADAM_STEP = 1e-3, 0.9, 0.999, 1e-8, 0.01, 10

VMEM_BYTES_V7X = 64 * 1024 * 1024
VMEM_LIMIT = VMEM_BYTES_V7X - 8 * 1024 * 1024
MASK_VALUE = -1e30
LANE_TILE = 128


def _cp(sem=None):
    return pltpu.CompilerParams(dimension_semantics=sem, vmem_limit_bytes=VMEM_LIMIT)


def _pick_tile(total, target):
    for cand in range(min(target, total) // LANE_TILE * LANE_TILE, 0, -LANE_TILE):
        if total % cand == 0:
            return cand
    return total


def _mm(a, b, *, ta=False, tb=False, out_dtypes=(F32,), tm=512, tn=1024, tk=1024,
        pro_a=None, pro_b=None, epi=None, extras=(), name):
    if ta:
        K, M = a.shape
    else:
        M, K = a.shape
    if tb:
        N, Kb = b.shape
    else:
        Kb, N = b.shape
    assert K == Kb, (a.shape, b.shape, ta, tb)
    tm, tn, tk = _pick_tile(M, tm), _pick_tile(N, tn), _pick_tile(K, tk)
    nk = K // tk
    a_spec = (pl.BlockSpec((tk, tm), lambda i, j, k: (k, i)) if ta
              else pl.BlockSpec((tm, tk), lambda i, j, k: (i, k)))
    b_spec = (pl.BlockSpec((tn, tk), lambda i, j, k: (j, k)) if tb
              else pl.BlockSpec((tk, tn), lambda i, j, k: (k, j)))
    ex_specs = []
    for e in extras:
        if e.shape[0] == 1:
            ex_specs.append(pl.BlockSpec((1, tn), lambda i, j, k: (0, j)))
        else:
            assert e.shape == (M, N), (e.shape, M, N)
            ex_specs.append(pl.BlockSpec((tm, tn), lambda i, j, k: (i, j)))
    dims = (((0 if ta else 1,), (1 if tb else 0,)), ((), ()))
    n_ex, n_out = len(extras), len(out_dtypes)

    def finish(r, ex, outs):
        res = epi(r, *[e[...] for e in ex]) if epi is not None else (r,)
        for o, v in zip(outs, res):
            o[...] = v.astype(o.dtype)

    def body(*refs):
        a_ref, b_ref = refs[:2]
        ex = refs[2:2 + n_ex]
        outs = refs[2 + n_ex:2 + n_ex + n_out]
        at, bt = a_ref[...], b_ref[...]
        if pro_a is not None:
            at = pro_a(at)
        if pro_b is not None:
            bt = pro_b(bt)
        part = lax.dot_general(at, bt, dims, preferred_element_type=F32)
        if nk == 1:
            finish(part, ex, outs)
            return
        acc = refs[-1]
        k = pl.program_id(2)

        @pl.when(k == 0)
        def _():
            acc[...] = part

        @pl.when(k > 0)
        def _():
            acc[...] += part

        @pl.when(k == nk - 1)
        def _():
            finish(acc[...], ex, outs)

    out = pl.pallas_call(
        body, grid=(M // tm, N // tn, nk),
        in_specs=[a_spec, b_spec] + ex_specs,
        out_specs=[pl.BlockSpec((tm, tn), lambda i, j, k: (i, j)) for _ in out_dtypes],
        out_shape=[jax.ShapeDtypeStruct((M, N), dt) for dt in out_dtypes],
        scratch_shapes=[pltpu.VMEM((tm, tn), F32)] if nk > 1 else [],
        compiler_params=_cp(("parallel", "parallel", "arbitrary")), name=name,
    )(a, b, *extras)
    return out[0] if n_out == 1 else out


def _to_bf16(t):
    return t.astype(BF16)


def _pick_rows(total, target):
    for cand in range(min(target, total) // 8 * 8, 0, -8):
        if total % cand == 0:
            return cand
    return total


def _rowwise(f, rows, params, out_dtypes, *, tr, name):
    T = rows[0].shape[0]
    tr = _pick_rows(T, tr)
    nr, npar = len(rows), len(params)
    blk = [jax.ShapeDtypeStruct((tr, r.shape[1]), F32) for r in rows]
    blk += [jax.ShapeDtypeStruct(p.shape, F32) for p in params]
    out_avals = jax.eval_shape(f, *blk)

    def body(*refs):
        res = f(*[r[...].astype(F32) for r in refs[:nr + npar]])
        for o, v in zip(refs[nr + npar:], res):
            o[...] = v.astype(o.dtype)

    out = pl.pallas_call(
        body, grid=(T // tr,),
        in_specs=[pl.BlockSpec((tr, r.shape[1]), lambda i: (i, 0)) for r in rows]
        + [pl.BlockSpec(p.shape, lambda i, nd=p.ndim: (0,) * nd) for p in params],
        out_specs=[pl.BlockSpec((tr, o.shape[1]), lambda i: (i, 0)) for o in out_avals],
        out_shape=[jax.ShapeDtypeStruct((T, o.shape[1]), dt) for o, dt in zip(out_avals, out_dtypes)],
        compiler_params=_cp(("parallel",)), name=name,
    )(*rows, *params)
    return out


def _rowwise_vjp(f, rows, params, cots, drow_dtypes, *, adds=None, tr, name):
    adds = adds or {}
    T = rows[0].shape[0]
    tr = _pick_rows(T, tr)
    nr, npar, nc = len(rows), len(params), len(cots)
    want = [i for i, dt in enumerate(drow_dtypes) if dt is not None]
    add_idx = [i for i in want if i in adds]
    add_arrays = [adds[i] for i in add_idx]
    na = len(add_arrays)

    def body(*refs):
        ins = [r[...].astype(F32) for r in refs[:nr + npar]]
        cvals = [r[...].astype(F32) for r in refs[nr + npar:nr + npar + nc]]
        avals = refs[nr + npar + nc:nr + npar + nc + na]
        outs = refs[nr + npar + nc + na:]
        _, vjp = jax.vjp(f, *ins)
        grads = vjp(tuple(cvals))
        for o, i in zip(outs[:len(want)], want):
            g = grads[i]
            if i in adds:
                g = g + avals[add_idx.index(i)][...].astype(F32)
            o[...] = g.astype(o.dtype)
        step = pl.program_id(0)
        for o, g in zip(outs[len(want):], grads[nr:]):
            @pl.when(step == 0)
            def _(o=o):
                o[...] = jnp.zeros_like(o)
            o[...] += g

    rspec = lambda r: pl.BlockSpec((tr, r.shape[1]), lambda i: (i, 0))
    pspec = lambda p: pl.BlockSpec(p.shape, lambda i, nd=p.ndim: (0,) * nd)
    out = pl.pallas_call(
        body, grid=(T // tr,),
        in_specs=[rspec(r) for r in rows] + [pspec(p) for p in params] + [rspec(c) for c in cots]
        + [rspec(a) for a in add_arrays],
        out_specs=[rspec(rows[i]) for i in want] + [pspec(p) for p in params],
        out_shape=[jax.ShapeDtypeStruct(rows[i].shape, drow_dtypes[i]) for i in want]
        + [jax.ShapeDtypeStruct(p.shape, F32) for p in params],
        compiler_params=_cp(("arbitrary",)), name=name,
    )(*rows, *params, *cots, *add_arrays)
    return out[:len(want)], out[len(want):]


def _f_rms(x, g):
    return (x * lax.rsqrt(jnp.mean(x * x, axis=-1, keepdims=True) + EPS) * g,)


def _ln(x, g, b):
    mu = jnp.mean(x, axis=-1, keepdims=True)
    var = jnp.mean(jnp.square(x - mu), axis=-1, keepdims=True)
    return (x - mu) * lax.rsqrt(var + EPS) * g + b


def _f_glu(z):
    d = z.shape[1] // 2
    return (z[:, :d] * jax.nn.sigmoid(z[:, d:]),)


def _f_bias_glu(z, b):
    return _f_glu(z + b)


def _f_ln_silu(y, b_dw, g, b):
    return (jax.nn.silu(_ln(y + b_dw, g, b)),)


def _f_gelu_ln(z, g, b):
    d = z.shape[1] // 2
    zz = jax.nn.gelu(z)
    return zz[:, :d], _ln(zz[:, d:], g, b)


def _f_gelu(y):
    return (jax.nn.gelu(y),)


def _f_merge(o0, o1, o2, l0, l1, l2):
    m = jnp.maximum(jnp.maximum(l0, l1), l2)
    e0, e1, e2 = jnp.exp(l0 - m), jnp.exp(l1 - m), jnp.exp(l2 - m)
    s = e0 + e1 + e2
    return ((e0 * o0 + e1 * o1 + e2 * o2) / s,)


def _f_add(a, b):
    return (a + b,)


def _loss_head(x, tgt, g, *, tr, name):
    T, D = x.shape
    tr = min(tr, T)

    def f(xv, gv, tv):
        y = _f_rms(xv, gv)[0]
        return 0.5 * jnp.mean(jnp.square(y - tv), axis=-1, keepdims=True)

    def body(x_ref, t_ref, g_ref, loss_ref, dx_ref, dg_ref):
        tv = t_ref[...]
        l, vjp = jax.vjp(lambda xv, gv: f(xv, gv, tv), x_ref[...], g_ref[...])
        dx, dg = vjp(jnp.ones_like(l))
        dx_ref[...] = dx

        @pl.when(pl.program_id(0) == 0)
        def _():
            loss_ref[...] = jnp.zeros_like(loss_ref)
            dg_ref[...] = jnp.zeros_like(dg_ref)

        loss_ref[...] += jnp.sum(l)
        dg_ref[...] += dg

    return pl.pallas_call(
        body, grid=(T // tr,),
        in_specs=[pl.BlockSpec((tr, D), lambda i: (i, 0)), pl.BlockSpec((tr, D), lambda i: (i, 0)),
                  pl.BlockSpec((1, D), lambda i: (0, 0))],
        out_specs=[pl.BlockSpec((1, 128), lambda i: (0, 0)), pl.BlockSpec((tr, D), lambda i: (i, 0)),
                   pl.BlockSpec((1, D), lambda i: (0, 0))],
        out_shape=[jax.ShapeDtypeStruct((1, 128), F32), jax.ShapeDtypeStruct((T, D), F32),
                   jax.ShapeDtypeStruct((1, D), F32)],
        compiler_params=_cp(("arbitrary",)), name=name,
    )(x, tgt, g)


def _adamw(w, g, m, v, *, name):
    R, C = w.shape
    tr = R
    for cand in (512, 256, 128, 64, 32, 16, 8):
        if R % cand == 0 and cand * C * 4 <= 2 * 1024 * 1024:
            tr = cand
            break
    c1 = 1.0 - ADAM_B1 ** ADAM_STEP
    c2 = 1.0 - ADAM_B2 ** ADAM_STEP

    def body(w_ref, g_ref, m_ref, v_ref, d_ref, nm_ref, nv_ref):
        gv = g_ref[...]
        nm = ADAM_B1 * m_ref[...] + (1.0 - ADAM_B1) * gv
        nv = ADAM_B2 * v_ref[...] + (1.0 - ADAM_B2) * jnp.square(gv)
        nm_ref[...] = nm
        nv_ref[...] = nv
        d_ref[...] = -ADAM_LR * ((nm / c1) / (jnp.sqrt(nv / c2) + ADAM_EPS) + ADAM_WD * w_ref[...])

    spec = pl.BlockSpec((tr, C), lambda i: (i, 0))
    return pl.pallas_call(
        body, grid=(R // tr,), in_specs=[spec] * 4, out_specs=[spec] * 3,
        out_shape=[jax.ShapeDtypeStruct((R, C), F32)] * 3,
        compiler_params=_cp(("parallel",)), name=name,
    )(w, g, m, v)


def _s5_prep(a_re, a_im, b_re, b_im, c_re, c_im, log_dt):
    G, N = a_re.shape
    P = b_re.shape[2]
    gpb = 128 // P
    nblk = G // gpb
    dt = jnp.exp(log_dt)[:, None]
    mag = jnp.exp(a_re * dt)
    abr, abi = mag * jnp.cos(a_im * dt), mag * jnp.sin(a_im * dt)
    den = a_re * a_re + a_im * a_im
    nr, ni = abr - 1.0, abi
    qr, qi = (nr * a_re + ni * a_im) / den, (ni * a_re - nr * a_im) / den
    bbr = qr[..., None] * b_re - qi[..., None] * b_im
    bbi = qr[..., None] * b_im + qi[..., None] * b_re
    eye = jnp.eye(gpb, dtype=F32)

    def expand_b(t):
        t = t.reshape(nblk, gpb, N, P).transpose(0, 1, 3, 2)
        return (t[:, :, :, None, :] * eye[None, :, None, :, None]).reshape(nblk, gpb * P, gpb * N)

    def expand_c(t):
        t = t.reshape(nblk, gpb, P, N).transpose(0, 1, 3, 2)
        return (t[:, :, :, None, :] * eye[None, :, None, :, None]).reshape(nblk, gpb * N, gpb * P)

    return (abr.reshape(1, G * N), abi.reshape(1, G * N), expand_b(bbr), expand_b(bbi),
            expand_c(c_re), expand_c(c_im))


def _s5_fwd(h, abr, abi, bre, bim, cre, cim, d, *, B, L, name):
    T, D = h.shape
    S = T // B
    L = min(L, S)
    nc = S // L
    nblk, cb, sb = bre.shape
    GN = abr.shape[1]

    def body(h_ref, ar_ref, ai_ref, bre_ref, bim_ref, cre_ref, cim_ref, d_ref,
             y_ref, xr_ref, xi_ref, er_ref, ei_ref, sr, si, car, cai):
        ci = pl.program_id(1)

        @pl.when(ci == 0)
        def _():
            car[...] = jnp.zeros_like(car)
            cai[...] = jnp.zeros_like(cai)

        for j in range(nblk):
            u = h_ref[:, j * cb:(j + 1) * cb]
            sr[:, j * sb:(j + 1) * sb] = jnp.dot(u, bre_ref[j], preferred_element_type=F32)
            si[:, j * sb:(j + 1) * sb] = jnp.dot(u, bim_ref[j], preferred_element_type=F32)
        ar, ai = ar_ref[...], ai_ref[...]

        def step(t, carry):
            pr, pi = carry
            nr = ar * pr - ai * pi + sr[pl.ds(t, 1), :]
            ni = ar * pi + ai * pr + si[pl.ds(t, 1), :]
            sr[pl.ds(t, 1), :] = nr
            si[pl.ds(t, 1), :] = ni
            return nr, ni

        pr, pi = lax.fori_loop(0, L, step, (car[...], cai[...]), unroll=4)
        car[...] = pr
        cai[...] = pi
        er_ref[0] = pr
        ei_ref[0] = pi
        for j in range(nblk):
            xr = sr[:, j * sb:(j + 1) * sb].astype(BF16)
            xi = si[:, j * sb:(j + 1) * sb].astype(BF16)
            xr_ref[:, j * sb:(j + 1) * sb] = xr
            xi_ref[:, j * sb:(j + 1) * sb] = xi
            y = (jnp.dot(xr, cre_ref[j], preferred_element_type=F32)
                 - jnp.dot(xi, cim_ref[j], preferred_element_type=F32))
            u = h_ref[:, j * cb:(j + 1) * cb].astype(F32)
            y_ref[:, j * cb:(j + 1) * cb] = y + d_ref[:, j * cb:(j + 1) * cb] * u

    tok = lambda w: pl.BlockSpec((L, w), lambda b, c: (b * nc + c, 0))
    whole = lambda p: pl.BlockSpec(p.shape, lambda b, c, nd=p.ndim: (0,) * nd)
    end = pl.BlockSpec((1, 1, GN), lambda b, c: (b * nc + c, 0, 0))
    return pl.pallas_call(
        body, grid=(B, nc),
        in_specs=[tok(D)] + [whole(p) for p in (abr, abi, bre, bim, cre, cim, d)],
        out_specs=[tok(D), tok(GN), tok(GN), end, end],
        out_shape=[jax.ShapeDtypeStruct((T, D), F32), jax.ShapeDtypeStruct((T, GN), BF16),
                   jax.ShapeDtypeStruct((T, GN), BF16), jax.ShapeDtypeStruct((B * nc, 1, GN), F32),
                   jax.ShapeDtypeStruct((B * nc, 1, GN), F32)],
        scratch_shapes=[pltpu.VMEM((L, GN), F32), pltpu.VMEM((L, GN), F32),
                        pltpu.VMEM((1, GN), F32), pltpu.VMEM((1, GN), F32)],
        compiler_params=_cp(("arbitrary", "arbitrary")), name=name,
    )(h, abr, abi, bre, bim, cre, cim, d)


def _s5_bwd(dy, h, xr, xi, er, ei, abr, abi, bre, bim, cre, cim, d, *, B, L, name):
    T, D = h.shape
    S = T // B
    L = min(L, S)
    nc = S // L
    nblk, cb, sb = bre.shape
    GN = abr.shape[1]
    dims_nt = (((1,), (1,)), ((), ()))
    dims_tn = (((0,), (0,)), ((), ()))

    def body(dy_ref, h_ref, xr_ref, xi_ref, er_ref, ei_ref, ar_ref, ai_ref, bre_ref, bim_ref,
             cre_ref, cim_ref, d_ref,
             dh_ref, dbre_ref, dbim_ref, dcre_ref, dcim_ref, dar_ref, dai_ref, dd_ref,
             lr, li, car, cai):
        b, cstep = pl.program_id(0), pl.program_id(1)
        ci = nc - 1 - cstep

        @pl.when((b == 0) & (cstep == 0))
        def _():
            for r in (dbre_ref, dbim_ref, dcre_ref, dcim_ref, dar_ref, dai_ref, dd_ref):
                r[...] = jnp.zeros_like(r)

        @pl.when(cstep == 0)
        def _():
            car[...] = jnp.zeros_like(car)
            cai[...] = jnp.zeros_like(cai)

        for j in range(nblk):
            dyj = dy_ref[:, j * cb:(j + 1) * cb].astype(BF16)
            lr[:, j * sb:(j + 1) * sb] = lax.dot_general(dyj, cre_ref[j], dims_nt, preferred_element_type=F32)
            li[:, j * sb:(j + 1) * sb] = -lax.dot_general(dyj, cim_ref[j], dims_nt, preferred_element_type=F32)
        ar, ai = ar_ref[...], ai_ref[...]

        def step(s, carry):
            t = L - 1 - s
            pr, pi = carry
            nr = lr[pl.ds(t, 1), :] + ar * pr + ai * pi
            ni = li[pl.ds(t, 1), :] - ai * pr + ar * pi
            lr[pl.ds(t, 1), :] = nr
            li[pl.ds(t, 1), :] = ni
            return nr, ni

        pr, pi = lax.fori_loop(0, L, step, (car[...], cai[...]), unroll=4)
        car[...] = pr
        cai[...] = pi
        has_prev = (ci > 0).astype(F32)
        first_row = lax.broadcasted_iota(jnp.int32, (L, sb), 0) == 0
        for j in range(nblk):
            cs = slice(j * cb, (j + 1) * cb)
            ss = slice(j * sb, (j + 1) * sb)
            lrj, lij = lr[:, ss], li[:, ss]
            xrj, xij = xr_ref[:, ss], xi_ref[:, ss]
            pr_j = jnp.where(first_row, er_ref[0][:, ss] * has_prev, pltpu.roll(xrj.astype(F32), 1, 0))
            pi_j = jnp.where(first_row, ei_ref[0][:, ss] * has_prev, pltpu.roll(xij.astype(F32), 1, 0))
            dar_ref[:, ss] += jnp.sum(lrj * pr_j + lij * pi_j, axis=0, keepdims=True)
            dai_ref[:, ss] += jnp.sum(lij * pr_j - lrj * pi_j, axis=0, keepdims=True)
            lrb, lib = lrj.astype(BF16), lij.astype(BF16)
            hj = h_ref[:, cs]
            dyf = dy_ref[:, cs]
            dyj = dyf.astype(BF16)
            dbre_ref[j] += lax.dot_general(hj, lrb, dims_tn, preferred_element_type=F32)
            dbim_ref[j] += lax.dot_general(hj, lib, dims_tn, preferred_element_type=F32)
            dcre_ref[j] += lax.dot_general(xrj, dyj, dims_tn, preferred_element_type=F32)
            dcim_ref[j] -= lax.dot_general(xij, dyj, dims_tn, preferred_element_type=F32)
            du = (lax.dot_general(lrb, bre_ref[j], dims_nt, preferred_element_type=F32)
                  + lax.dot_general(lib, bim_ref[j], dims_nt, preferred_element_type=F32))
            dh_ref[:, cs] = du + d_ref[:, cs] * dyf
            dd_ref[:, cs] += jnp.sum(dyf * hj.astype(F32), axis=0, keepdims=True)

    tok = lambda w: pl.BlockSpec((L, w), lambda b, c: (b * nc + nc - 1 - c, 0))
    whole = lambda p: pl.BlockSpec(p.shape, lambda b, c, nd=p.ndim: (0,) * nd)
    prev_end = pl.BlockSpec((1, 1, GN), lambda b, c: (b * nc + jnp.maximum(nc - 2 - c, 0), 0, 0))
    params = (abr, abi, bre, bim, cre, cim, d)
    acc_shapes = [bre.shape, bim.shape, cre.shape, cim.shape, abr.shape, abi.shape, d.shape]
    out = pl.pallas_call(
        body, grid=(B, nc),
        in_specs=[tok(D), tok(D), tok(GN), tok(GN), prev_end, prev_end] + [whole(p) for p in params],
        out_specs=[tok(D)] + [pl.BlockSpec(s, lambda b, c, nd=len(s): (0,) * nd) for s in acc_shapes],
        out_shape=[jax.ShapeDtypeStruct((T, D), F32)] + [jax.ShapeDtypeStruct(s, F32) for s in acc_shapes],
        scratch_shapes=[pltpu.VMEM((L, GN), F32), pltpu.VMEM((L, GN), F32),
                        pltpu.VMEM((1, GN), F32), pltpu.VMEM((1, GN), F32)],
        compiler_params=_cp(("arbitrary", "arbitrary")), name=name,
    )(dy, h, xr, xi, er, ei, *params)
    return out


def _conv_fwd(zp, w, *, R, tc, name):
    B, SP, C = zp.shape
    S = SP - CONV_HALO
    R, tc = min(R, S), min(tc, C)

    def body(z_ref, w_ref, y_ref):
        def chunk(ci, _):
            start = pl.multiple_of(ci * R, 8)
            ze = z_ref[pl.ds(start, R + CONV_HALO), :]
            acc = jnp.zeros((R, tc), F32)
            for m in range(CONV_WIDTH):
                k = CONV_WIDTH - 1 - m
                sh = ze if m == 0 else pltpu.roll(ze, m, 0)
                acc = acc + w_ref[k:k + 1, :] * sh[CONV_HALO:, :]
            y_ref[pl.ds(start, R), :] = acc
            return 0

        lax.fori_loop(0, S // R, chunk, 0)

    return pl.pallas_call(
        body, grid=(B, C // tc),
        in_specs=[pl.BlockSpec((None, SP, tc), lambda b, c: (b, 0, c)),
                  pl.BlockSpec((32, tc), lambda b, c: (0, c))],
        out_specs=pl.BlockSpec((None, S, tc), lambda b, c: (b, 0, c)),
        out_shape=jax.ShapeDtypeStruct((B, S, C), F32),
        compiler_params=_cp(("parallel", "parallel")), name=name,
    )(zp, w)


def _conv_bwd(zp, dyp, w, *, R, tc, name):
    B, SP, C = zp.shape
    S = SP - CONV_HALO
    R, tc = min(R, S), min(tc, C)

    def body(z_ref, dy_ref, w_ref, dz_ref, dw_ref):
        @pl.when(pl.program_id(1) == 0)
        def _():
            dw_ref[...] = jnp.zeros_like(dw_ref)

        def chunk(ci, _):
            start = pl.multiple_of(ci * R, 8)
            ze = z_ref[pl.ds(start, R + CONV_HALO), :]
            de = dy_ref[pl.ds(start, R + CONV_HALO), :]
            dy = de[:R, :]
            acc = jnp.zeros((R, tc), F32)
            for m in range(CONV_WIDTH):
                k = CONV_WIDTH - 1 - m
                zs = ze if m == 0 else pltpu.roll(ze, m, 0)
                ds_ = de if m == 0 else pltpu.roll(de, R + CONV_HALO - m, 0)
                acc = acc + w_ref[k:k + 1, :] * ds_[:R, :]
                dw_ref[k:k + 1, :] += jnp.sum(dy * zs[CONV_HALO:, :], axis=0, keepdims=True)
            dz_ref[pl.ds(start, R), :] = acc
            return 0

        lax.fori_loop(0, S // R, chunk, 0)

    return pl.pallas_call(
        body, grid=(C // tc, B),
        in_specs=[pl.BlockSpec((None, SP, tc), lambda c, b: (b, 0, c)),
                  pl.BlockSpec((None, SP, tc), lambda c, b: (b, 0, c)),
                  pl.BlockSpec((32, tc), lambda c, b: (0, c))],
        out_specs=[pl.BlockSpec((None, S, tc), lambda c, b: (b, 0, c)),
                   pl.BlockSpec((32, tc), lambda c, b: (0, c))],
        out_shape=[jax.ShapeDtypeStruct((B, S, C), F32), jax.ShapeDtypeStruct((32, C), F32)],
        compiler_params=_cp(("parallel", "arbitrary")), name=name,
    )(zp, dyp, w)


def _gmlp_fwd(u, vn, ws, bcol, *, nck, name):
    T, E = u.shape
    H = ws.shape[0]
    he = E // H
    rows = nck * GMLP_CHUNK
    rows = min(rows, T)
    n_in = rows // GMLP_CHUNK

    def body(u_ref, v_ref, ws_ref, b_ref, o_ref):
        for c in range(n_in):
            rs = slice(c * GMLP_CHUNK, (c + 1) * GMLP_CHUNK)
            for hh in range(H):
                cs = slice(hh * he, (hh + 1) * he)
                v2 = jnp.dot(ws_ref[hh], v_ref[rs, cs].astype(BF16), preferred_element_type=F32)
                v2 = v2 + b_ref[:, hh:hh + 1]
                o_ref[rs, cs] = (u_ref[rs, cs] * v2).astype(o_ref.dtype)

    tok = pl.BlockSpec((rows, E), lambda i: (i, 0))
    return pl.pallas_call(
        body, grid=(T // rows,),
        in_specs=[tok, tok, pl.BlockSpec(ws.shape, lambda i: (0, 0, 0)), pl.BlockSpec(bcol.shape, lambda i: (0, 0))],
        out_specs=tok, out_shape=jax.ShapeDtypeStruct((T, E), BF16),
        compiler_params=_cp(("parallel",)), name=name,
    )(u, vn, ws, bcol)


def _gmlp_bwd(duv, u, vn, ws, bcol, *, nck, name):
    T, E = u.shape
    H = ws.shape[0]
    he = E // H
    rows = min(nck * GMLP_CHUNK, T)
    n_in = rows // GMLP_CHUNK
    dims_nt = (((1,), (1,)), ((), ()))
    dims_tn = (((0,), (0,)), ((), ()))

    def body(g_ref, u_ref, v_ref, ws_ref, b_ref, du_ref, dv_ref, dws_ref, db_ref):
        @pl.when(pl.program_id(0) == 0)
        def _():
            dws_ref[...] = jnp.zeros_like(dws_ref)
            db_ref[...] = jnp.zeros_like(db_ref)

        for c in range(n_in):
            rs = slice(c * GMLP_CHUNK, (c + 1) * GMLP_CHUNK)
            for hh in range(H):
                cs = slice(hh * he, (hh + 1) * he)
                vb = v_ref[rs, cs].astype(BF16)
                v2 = jnp.dot(ws_ref[hh], vb, preferred_element_type=F32) + b_ref[:, hh:hh + 1]
                g = g_ref[rs, cs]
                du_ref[rs, cs] = g * v2
                dv2 = g * u_ref[rs, cs]
                dv2b = dv2.astype(BF16)
                dv_ref[rs, cs] = lax.dot_general(ws_ref[hh], dv2b, dims_tn, preferred_element_type=F32)
                dws_ref[hh] += lax.dot_general(dv2b, vb, dims_nt, preferred_element_type=F32)
                db_ref[:, hh:hh + 1] += jnp.sum(dv2, axis=1, keepdims=True)

    tok = pl.BlockSpec((rows, E), lambda i: (i, 0))
    return pl.pallas_call(
        body, grid=(T // rows,),
        in_specs=[tok, tok, tok, pl.BlockSpec(ws.shape, lambda i: (0, 0, 0)), pl.BlockSpec(bcol.shape, lambda i: (0, 0))],
        out_specs=[tok, tok, pl.BlockSpec(ws.shape, lambda i: (0, 0, 0)), pl.BlockSpec(bcol.shape, lambda i: (0, 0))],
        out_shape=[jax.ShapeDtypeStruct((T, E), F32), jax.ShapeDtypeStruct((T, E), F32),
                   jax.ShapeDtypeStruct(ws.shape, F32), jax.ShapeDtypeStruct(bcol.shape, F32)],
        compiler_params=_cp(("arbitrary",)), name=name,
    )(duv, u, vn, ws, bcol)


def _att_masks(i, nbs, blocks_per_group, nbk):
    g = (i * nbk) // blocks_per_group
    nbm1 = jnp.where(g == 0, nbs[0] - 1, jnp.where(g == 1, nbs[1] - 1, nbs[2] - 1))
    ji = lax.broadcasted_iota(jnp.int32, (2 * ATT_BLK, ATT_BLK), 0)
    ii = lax.broadcasted_iota(jnp.int32, (2 * ATT_BLK, ATT_BLK), 1)
    dist = ii + ATT_BLK - ji
    band = (dist >= 0) & (dist <= ATT_BLK)
    cur = ji >= ATT_BLK
    return nbm1, band, cur


def _att_fwd(q, k, v, *, nbs, nbk, name):
    NB = q.shape[0]
    bpg = NB // 3
    nbk = min(nbk, bpg)
    scale = HEAD_DIM ** -0.5
    dims_nt = (((1,), (1,)), ((), ()))
    dims_tn = (((0,), (0,)), ((), ()))

    def body(q_ref, k_ref, kp_ref, v_ref, vp_ref, o_ref, lse_ref):
        i = pl.program_id(0)
        nbm1, band, cur = _att_masks(i, nbs, bpg, nbk)
        for jj in range(nbk):
            hp = ((i * nbk + jj) & nbm1) != 0
            kk = jnp.concatenate([kp_ref[0] if jj == 0 else k_ref[jj - 1], k_ref[jj]], axis=0)
            vv = jnp.concatenate([vp_ref[0] if jj == 0 else v_ref[jj - 1], v_ref[jj]], axis=0)
            st = lax.dot_general(kk, q_ref[jj], dims_nt, preferred_element_type=F32) * scale
            st = jnp.where(band & (cur | hp), st, MASK_VALUE)
            m = jnp.max(st, axis=0, keepdims=True)
            p = jnp.exp(st - m)
            l = jnp.sum(p, axis=0, keepdims=True)
            lse_ref[jj:jj + 1, :] = m + jnp.log(l)
            pn = (p / l).astype(BF16)
            o_ref[jj] = lax.dot_general(pn, vv, dims_tn, preferred_element_type=F32)

    blk = pl.BlockSpec((nbk, ATT_BLK, HEAD_DIM), lambda i: (i, 0, 0))
    prev = pl.BlockSpec((1, ATT_BLK, HEAD_DIM), lambda i: (jnp.maximum(i * nbk - 1, 0), 0, 0))
    return pl.pallas_call(
        body, grid=(NB // nbk,),
        in_specs=[blk, blk, prev, blk, prev],
        out_specs=[blk, pl.BlockSpec((nbk, ATT_BLK), lambda i: (i, 0))],
        out_shape=[jax.ShapeDtypeStruct((NB, ATT_BLK, HEAD_DIM), F32), jax.ShapeDtypeStruct((NB, ATT_BLK), F32)],
        compiler_params=_cp(("parallel",)), name=name,
    )(q, k, k, v, v)


def _att_bwd(q, k, v, do, lse, dlse, *, nbs, nbk, name):
    NB = q.shape[0]
    bpg = NB // 3
    nbk = min(nbk, bpg)
    scale = HEAD_DIM ** -0.5
    dims_nt = (((1,), (1,)), ((), ()))
    dims_tn = (((0,), (0,)), ((), ()))

    def body(q_ref, k_ref, kp_ref, v_ref, vp_ref, do_ref, lse_ref, dlse_ref,
             dq_ref, dkc_ref, dkp_ref, dvc_ref, dvp_ref):
        i = pl.program_id(0)
        nbm1, band, cur = _att_masks(i, nbs, bpg, nbk)
        for jj in range(nbk):
            hp = ((i * nbk + jj) & nbm1) != 0
            kk = jnp.concatenate([kp_ref[0] if jj == 0 else k_ref[jj - 1], k_ref[jj]], axis=0)
            vv = jnp.concatenate([vp_ref[0] if jj == 0 else v_ref[jj - 1], v_ref[jj]], axis=0)
            qj = q_ref[jj]
            doj = do_ref[jj].astype(BF16)
            st = lax.dot_general(kk, qj, dims_nt, preferred_element_type=F32) * scale
            st = jnp.where(band & (cur | hp), st, MASK_VALUE)
            p = jnp.exp(st - lse_ref[jj:jj + 1, :])
            dp = lax.dot_general(vv, doj, dims_nt, preferred_element_type=F32)
            delta = jnp.sum(p * dp, axis=0, keepdims=True)
            ds_ = p * (dp - delta + dlse_ref[jj:jj + 1, :]) * scale
            dsb = ds_.astype(BF16)
            dq_ref[jj] = lax.dot_general(dsb, kk, dims_tn, preferred_element_type=F32)
            dkk = jnp.dot(dsb, qj, preferred_element_type=F32)
            dvv = jnp.dot(p.astype(BF16), doj, preferred_element_type=F32)
            dkp_ref[jj] = dkk[:ATT_BLK]
            dkc_ref[jj] = dkk[ATT_BLK:]
            dvp_ref[jj] = dvv[:ATT_BLK]
            dvc_ref[jj] = dvv[ATT_BLK:]

    blk = pl.BlockSpec((nbk, ATT_BLK, HEAD_DIM), lambda i: (i, 0, 0))
    prev = pl.BlockSpec((1, ATT_BLK, HEAD_DIM), lambda i: (jnp.maximum(i * nbk - 1, 0), 0, 0))
    row = pl.BlockSpec((nbk, ATT_BLK), lambda i: (i, 0))
    return pl.pallas_call(
        body, grid=(NB // nbk,),
        in_specs=[blk, blk, prev, blk, prev, blk, row, row],
        out_specs=[blk] * 5,
        out_shape=[jax.ShapeDtypeStruct((NB, ATT_BLK, HEAD_DIM), F32)] * 5,
        compiler_params=_cp(("parallel",)), name=name,
    )(q, k, k, v, v, do, lse, dlse)


def _to_blocks(t, B, S, dil):
    H, w = t.shape[1], t.shape[2]
    t = t.reshape(B, S // dil, dil, H, w).transpose(0, 2, 3, 1, 4)
    return t.reshape(B * dil * H * (S // dil // ATT_BLK), ATT_BLK, w)


def _from_blocks(t, B, S, dil, H):
    w = t.shape[2]
    t = t.reshape(B, dil, H, S // dil, w).transpose(0, 3, 1, 2, 4)
    return t.reshape(B * S, H, w)


def _mesh_pos():
    return lax.axis_index("x"), lax.axis_index("y"), lax.axis_index("c")


def _allgather8(xs, *, name):
    m_per, n = xs.shape

    def body(x_ref, out_ref, send_sems, recv_sems, local_sem):
        x, y, c = _mesh_pos()
        me, sibling = (x, y, c), (x, y, 1 - c)
        chips = [(1 - x, y), (x, 1 - y), (1 - x, 1 - y)]

        def rows(px, py, pc):
            return out_ref.at[pl.ds((4 * px + 2 * py + pc) * m_per, m_per), :]

        def copy(k, block, to, src=None):
            return pltpu.make_async_remote_copy(
                src_ref=rows(*block) if src is None else src, dst_ref=rows(*block),
                send_sem=send_sems.at[k], recv_sem=recv_sems.at[k], device_id=to, device_id_type=MESH)

        mine = pltpu.make_async_copy(x_ref, rows(*me), local_sem)
        mine.start()
        first = [copy(0, me, sibling, src=x_ref)]
        first += [copy(1 + j, me, (*chip, c), src=x_ref) for j, chip in enumerate(chips)]
        for cp in first:
            cp.start()
        passed = [copy(4 + j, (*chip, c), sibling) for j, chip in enumerate(chips)]
        for j, chip in enumerate(chips):
            copy(1 + j, (*chip, c), me).wait_recv()
            passed[j].start()
        copy(0, sibling, me).wait_recv()
        for j, chip in enumerate(chips):
            copy(4 + j, (*chip, 1 - c), me).wait_recv()
        for cp in first + passed:
            cp.wait_send()
        mine.wait()

    return pl.pallas_call(
        body, out_shape=jax.ShapeDtypeStruct((8 * m_per, n), xs.dtype),
        in_specs=[pl.BlockSpec(memory_space=pltpu.VMEM)], out_specs=pl.BlockSpec(memory_space=pltpu.VMEM),
        scratch_shapes=[pltpu.SemaphoreType.DMA((7,)), pltpu.SemaphoreType.DMA((7,)), pltpu.SemaphoreType.DMA],
        compiler_params=pltpu.CompilerParams(vmem_limit_bytes=VMEM_LIMIT), name=name,
    )(xs)


def _allgather_chips(w, *, name):
    _, R, N = w.shape

    def body(w_ref, out_ref, send_sems, recv_sems, local_sem):
        x, y, c = _mesh_pos()
        sibling = (x, y, 1 - c)
        chips = [(1 - x, y), (x, 1 - y), (1 - x, 1 - y)]

        def slot(px, py, half):
            return out_ref.at[2 * px + py, half]

        def copy(k, px, py, half, to, src=None):
            return pltpu.make_async_remote_copy(
                src_ref=slot(px, py, half) if src is None else src, dst_ref=slot(px, py, half),
                send_sem=send_sems.at[k], recv_sem=recv_sems.at[k], device_id=to, device_id_type=MESH)

        mine = pltpu.make_async_copy(w_ref, out_ref.at[2 * x + y], local_sem)
        mine.start()
        first = [copy(j, x, y, c, (*chip, c), src=w_ref.at[c]) for j, chip in enumerate(chips)]
        for cp in first:
            cp.start()
        passed = [copy(3 + j, *chip, c, sibling) for j, chip in enumerate(chips)]
        for j, chip in enumerate(chips):
            copy(j, *chip, c, (x, y, c)).wait_recv()
            passed[j].start()
        for j, chip in enumerate(chips):
            copy(3 + j, *chip, 1 - c, (x, y, c)).wait_recv()
        for cp in first + passed:
            cp.wait_send()
        mine.wait()

    return pl.pallas_call(
        body, out_shape=jax.ShapeDtypeStruct((N_CHIPS, 2, R, N), w.dtype),
        in_specs=[pl.BlockSpec(memory_space=pl.ANY)], out_specs=pl.BlockSpec(memory_space=pl.ANY),
        scratch_shapes=[pltpu.SemaphoreType.DMA((6,)), pltpu.SemaphoreType.DMA((6,)), pltpu.SemaphoreType.DMA],
        name=name,
    )(w)


def _swap_half(g, *, name):
    def body(g_ref, out_ref, send_sem, recv_sem):
        x, y, c = _mesh_pos()
        cp = pltpu.make_async_remote_copy(
            src_ref=g_ref.at[1 - c], dst_ref=out_ref, send_sem=send_sem, recv_sem=recv_sem,
            device_id=(x, y, 1 - c), device_id_type=MESH)
        cp.start()
        cp.wait()

    return pl.pallas_call(
        body, out_shape=jax.ShapeDtypeStruct(g.shape[1:], g.dtype),
        in_specs=[pl.BlockSpec(memory_space=pl.ANY)], out_specs=pl.BlockSpec(memory_space=pl.ANY),
        scratch_shapes=[pltpu.SemaphoreType.DMA, pltpu.SemaphoreType.DMA], name=name,
    )(g)


def _scatter_chips(s, *, name):
    def body(s_ref, out_ref, send_sems, recv_sems, local_sem):
        x, y, c = _mesh_pos()
        me = 2 * x + y
        chips = [(1 - x, y), (x, 1 - y), (1 - x, 1 - y)]
        mine = pltpu.make_async_copy(s_ref.at[me], out_ref.at[me], local_sem)
        mine.start()
        cps = [pltpu.make_async_remote_copy(
            src_ref=s_ref.at[2 * px + py], dst_ref=out_ref.at[me],
            send_sem=send_sems.at[j], recv_sem=recv_sems.at[j],
            device_id=(px, py, c), device_id_type=MESH) for j, (px, py) in enumerate(chips)]
        for cp in cps:
            cp.start()
        for j, (px, py) in enumerate(chips):
            pltpu.make_async_remote_copy(
                src_ref=s_ref.at[me], dst_ref=out_ref.at[2 * px + py],
                send_sem=send_sems.at[j], recv_sem=recv_sems.at[j],
                device_id=(px, py, c), device_id_type=MESH).wait_recv()
        for cp in cps:
            cp.wait_send()
        mine.wait()

    return pl.pallas_call(
        body, out_shape=jax.ShapeDtypeStruct(s.shape, s.dtype),
        in_specs=[pl.BlockSpec(memory_space=pl.ANY)], out_specs=pl.BlockSpec(memory_space=pl.ANY),
        scratch_shapes=[pltpu.SemaphoreType.DMA((3,)), pltpu.SemaphoreType.DMA((3,)), pltpu.SemaphoreType.DMA],
        name=name,
    )(s)


def _share_half(t, *, name):
    def body(t_ref, out_ref, send_sem, recv_sem, local_sem):
        x, y, c = _mesh_pos()
        mine = pltpu.make_async_copy(t_ref, out_ref.at[c], local_sem)
        mine.start()
        cp = pltpu.make_async_remote_copy(
            src_ref=t_ref, dst_ref=out_ref.at[c], send_sem=send_sem, recv_sem=recv_sem,
            device_id=(x, y, 1 - c), device_id_type=MESH)
        cp.start()
        pltpu.make_async_remote_copy(
            src_ref=t_ref, dst_ref=out_ref.at[1 - c], send_sem=send_sem, recv_sem=recv_sem,
            device_id=(x, y, 1 - c), device_id_type=MESH).wait_recv()
        cp.wait_send()
        mine.wait()

    return pl.pallas_call(
        body, out_shape=jax.ShapeDtypeStruct((2,) + t.shape, t.dtype),
        in_specs=[pl.BlockSpec(memory_space=pl.ANY)], out_specs=pl.BlockSpec(memory_space=pl.ANY),
        scratch_shapes=[pltpu.SemaphoreType.DMA, pltpu.SemaphoreType.DMA, pltpu.SemaphoreType.DMA],
        name=name,
    )(t)


TR = 256
S5_CHUNK = 256


def _rms_fwd(x, g, name):
    return _rowwise(_f_rms, [x], [g], (BF16,), tr=TR, name=name)[0]


def _rms_bwd(x, g, dh, gx, name):
    (dx,), (dg,) = _rowwise_vjp(_f_rms, [x], [g], [dh], [F32], adds={0: gx}, tr=TR, name=name)
    return dx, dg


def _mlp_fwd(x, g, w_in, w_out, li):
    h2 = _rms_fwd(x, g, f"mlp_rms_{li}")
    r = _mm(h2, w_in, out_dtypes=(BF16,), epi=lambda acc: (jnp.maximum(acc, 0.0),), name=f"mlp_in_{li}")
    x_out = _mm(r, w_out, pro_a=lambda t: t * t, epi=lambda acc, res: (acc + res,), extras=(x,),
                name=f"mlp_out_{li}")
    return x_out, (h2, r)


def _mlp_bwd(gx, x, g, w_in, w_out, saved, li):
    h2, r = saved
    da = _mm(gx, w_out, tb=True, pro_a=_to_bf16, out_dtypes=(BF16,),
             epi=lambda acc, rt: (acc * 2.0 * rt.astype(F32),), extras=(r,), name=f"mlp_dact_{li}")
    d_w_out = _mm(r, gx, ta=True, pro_a=lambda t: t * t, pro_b=_to_bf16, tm=512, tn=1024, tk=512,
                  name=f"mlp_dwout_{li}")
    d_w_in = _mm(h2, da, ta=True, tm=512, tn=2048, tk=512, name=f"mlp_dwin_{li}")
    dh2 = _mm(da, w_in, tb=True, name=f"mlp_dh_{li}")
    gx_mid, dg = _rms_bwd(x, g, dh2, gx, f"mlp_rms_bwd_{li}")
    return gx_mid, dg, d_w_in, d_w_out


def _local_step(x3, tgt3, p):
    B, S, D = x3.shape
    T = B * S
    x = x3.reshape(T, D)
    grads = {}
    row = lambda v: v.reshape(1, -1)

    xs = [x]
    g0 = row(p["norm_mix"][0])
    h0 = _rms_fwd(x, g0, "rms_mix_0")
    s5_args = (p["ssm_a_re"][0], p["ssm_a_im"][0], p["ssm_b_re"][0], p["ssm_b_im"][0],
               p["ssm_c_re"][0], p["ssm_c_im"][0], p["ssm_log_dt"][0])
    s5_exp, s5_vjp = jax.vjp(_s5_prep, *s5_args)
    abr, abi, bre, bim, cre, cim = s5_exp
    bre_b, bim_b, cre_b, cim_b = (t.astype(BF16) for t in (bre, bim, cre, cim))
    d_skip = p["ssm_d"]
    ypre, sxr, sxi, ser, sei = _s5_fwd(h0, abr, abi, bre_b, bim_b, cre_b, cim_b, d_skip, B=B, L=S5_CHUNK, name="s5_fwd")
    w_glu = p["ssm_w_glu"]
    z0 = _mm(ypre, w_glu, pro_a=lambda t: jax.nn.gelu(t).astype(BF16), name="s5_glu_mm")
    x_mid0 = _rowwise(lambda z, xr: (_f_glu(z)[0] + xr,), [z0, x], [], (F32,), tr=TR, name="s5_glu")[0]
    x1, mlp_saved0 = _mlp_fwd(x_mid0, row(p["norm_mlp"][0]), p["mlp_w_in"][0], p["mlp_w_out"][0], 0)

    g1 = row(p["norm_mix"][1])
    h1 = _rms_fwd(x1, g1, "rms_mix_1")
    z1 = _mm(h1, p["conv_w_pw1"], name="conv_pw1")
    zg = _rowwise(_f_bias_glu, [z1], [p["conv_b_pw1"]], (F32,), tr=TR, name="conv_glu")[0]
    zp = jnp.pad(zg.reshape(B, S, D), ((0, 0), (CONV_HALO, 0), (0, 0)))
    w_dw = jnp.pad(p["conv_w_dw"], ((0, 32 - CONV_WIDTH), (0, 0)))
    yc = _conv_fwd(zp, w_dw, R=256, tc=128, name="conv_dw").reshape(T, D)
    ln_par = [p["conv_b_dw"], p["conv_ln_g"], p["conv_ln_b"]]
    qc = _rowwise(_f_ln_silu, [yc], ln_par, (BF16,), tr=TR, name="conv_ln_silu")[0]
    x_mid1 = _mm(qc, p["conv_w_pw2"], epi=lambda acc, bias, res: (acc + bias + res,),
                 extras=(p["conv_b_pw2"], x1), name="conv_pw2")
    x2, mlp_saved1 = _mlp_fwd(x_mid1, row(p["norm_mlp"][1]), p["mlp_w_in"][1], p["mlp_w_out"][1], 1)

    g2 = row(p["norm_mix"][2])
    h2 = _rms_fwd(x2, g2, "rms_mix_2")
    z2 = _mm(h2, p["gmlp_w_in"], name="gmlp_in")
    gl_par = [p["gmlp_ln_g"], p["gmlp_ln_b"]]
    gu, gvn = _rowwise(_f_gelu_ln, [z2], gl_par, (F32, F32), tr=TR, name="gmlp_gelu_ln")
    causal = jnp.tril(jnp.ones((GMLP_CHUNK, GMLP_CHUNK), dtype=bool))
    ws_b = jnp.where(causal[None], p["gmlp_w_s"][0], 0.0).astype(BF16)
    bcol = jnp.pad(p["gmlp_b_s"][0].T, ((0, 0), (0, 128 - GMLP_HEADS)))
    uv = _gmlp_fwd(gu, gvn, ws_b, bcol, nck=4, name="gmlp_spatial")
    x_mid2 = _mm(uv, p["gmlp_w_out"], epi=lambda acc, res: (acc + res,), extras=(x2,), name="gmlp_out")
    x3_, mlp_saved2 = _mlp_fwd(x_mid2, row(p["norm_mlp"][2]), p["mlp_w_in"][2], p["mlp_w_out"][2], 2)

    g3 = row(p["norm_mix"][3])
    h3 = _rms_fwd(x3_, g3, "rms_mix_3")
    qkv = _mm(h3, p["attn_w_qkv"], out_dtypes=(BF16,), tn=1152, name="attn_qkv")
    qkv6 = qkv.reshape(T, 3, len(ATT_DILS), ATT_HEADS, HEAD_DIM)
    nbs = tuple(S // dil // ATT_BLK for dil in ATT_DILS)
    qb, kb, vb = (jnp.concatenate([_to_blocks(qkv6[:, i, gi], B, S, dil) for gi, dil in enumerate(ATT_DILS)], axis=0)
                  for i in range(3))
    ob, lseb = _att_fwd(qb, kb, vb, nbs=nbs, nbk=8, name="attn_fwd")
    bpg = qb.shape[0] // 3
    o_tok, l_tok = [], []
    for gi, dil in enumerate(ATT_DILS):
        o_tok.append(_from_blocks(ob[gi * bpg:(gi + 1) * bpg], B, S, dil, ATT_HEADS).reshape(T * ATT_HEADS, HEAD_DIM))
        l_tok.append(_from_blocks(lseb[gi * bpg:(gi + 1) * bpg][:, :, None], B, S, dil, ATT_HEADS).reshape(T * ATT_HEADS, 1))
    merged = _rowwise(_f_merge, o_tok + l_tok, [], (BF16,), tr=2048, name="attn_merge")[0]
    merged2 = merged.reshape(T, ATT_W)
    x_mid3 = _mm(merged2, p["attn_w_o"], epi=lambda acc, res: (acc + res,), extras=(x3_,), name="attn_out")
    x4, mlp_saved3 = _mlp_fwd(x_mid3, row(p["norm_mlp"][3]), p["mlp_w_in"][3], p["mlp_w_out"][3], 3)

    loss_part, gx, dgf = _loss_head(x4, tgt3.reshape(T, D), row(p["norm_final"]), tr=TR, name="loss_head")
    grads["norm_final"] = dgf.reshape(-1)
    d_norm_mix, d_norm_mlp = [None] * 4, [None] * 4
    d_mlp_in, d_mlp_out = [None] * 4, [None] * 4

    gx, d_norm_mlp[3], d_mlp_in[3], d_mlp_out[3] = _mlp_bwd(
        gx, x_mid3, row(p["norm_mlp"][3]), p["mlp_w_in"][3], p["mlp_w_out"][3], mlp_saved3, 3)
    dmerged = _mm(gx, p["attn_w_o"], tb=True, pro_a=_to_bf16, name="attn_dmerged")
    grads["attn_w_o"] = _mm(merged2, gx, ta=True, pro_b=_to_bf16, tk=512, name="attn_dwo")[None]
    dml, _ = _rowwise_vjp(_f_merge, o_tok + l_tok, [], [dmerged.reshape(T * ATT_HEADS, HEAD_DIM)], [F32] * 6,
                          tr=2048, name="attn_merge_bwd")
    dob = jnp.concatenate([_to_blocks(dml[gi].reshape(T, ATT_HEADS, HEAD_DIM), B, S, dil)
                           for gi, dil in enumerate(ATT_DILS)], axis=0)
    dlb = jnp.concatenate([_to_blocks(dml[3 + gi].reshape(T, ATT_HEADS, 1), B, S, dil)[:, :, 0]
                           for gi, dil in enumerate(ATT_DILS)], axis=0)
    dqb, dkc, dkp, dvc, dvp = _att_bwd(qb, kb, vb, dob, lseb, dlb, nbs=nbs, nbk=8, name="attn_bwd")
    nblk_all = qb.shape[0]
    flat = lambda t: t.reshape(nblk_all * ATT_BLK, HEAD_DIM)
    shift = lambda t: jnp.concatenate([t[1:], jnp.zeros_like(t[:1])], axis=0)
    dkb = _rowwise(_f_add, [flat(dkc), flat(shift(dkp))], [], (BF16,), tr=2048, name="attn_dk_add")[0]
    dvb = _rowwise(_f_add, [flat(dvc), flat(shift(dvp))], [], (BF16,), tr=2048, name="attn_dv_add")[0]
    parts = []
    for t in (dqb.astype(BF16), dkb.reshape(nblk_all, ATT_BLK, HEAD_DIM), dvb.reshape(nblk_all, ATT_BLK, HEAD_DIM)):
        parts.append(jnp.stack([_from_blocks(t[gi * bpg:(gi + 1) * bpg], B, S, dil, ATT_HEADS)
                                for gi, dil in enumerate(ATT_DILS)], axis=1))
    dqkv = jnp.stack(parts, axis=1).reshape(T, 3 * len(ATT_DILS) * ATT_W)
    grads["attn_w_qkv"] = _mm(h3, dqkv, ta=True, tn=1152, tk=512, name="attn_dwqkv")[None]
    dh3 = _mm(dqkv, p["attn_w_qkv"], tb=True, name="attn_dh")
    gx, d_norm_mix[3] = _rms_bwd(x3_, g3, dh3, gx, "rms_mix_bwd_3")

    gx, d_norm_mlp[2], d_mlp_in[2], d_mlp_out[2] = _mlp_bwd(
        gx, x_mid2, row(p["norm_mlp"][2]), p["mlp_w_in"][2], p["mlp_w_out"][2], mlp_saved2, 2)
    duv = _mm(gx, p["gmlp_w_out"], tb=True, pro_a=_to_bf16, name="gmlp_duv")
    grads["gmlp_w_out"] = _mm(uv, gx, ta=True, pro_b=_to_bf16, tk=512, name="gmlp_dwout")[None]
    du, dvn, dws, dbcol = _gmlp_bwd(duv, gu, gvn, ws_b, bcol, nck=4, name="gmlp_spatial_bwd")
    grads["gmlp_w_s"] = jnp.where(causal[None], dws, 0.0)[None]
    grads["gmlp_b_s"] = dbcol[:, :GMLP_HEADS].T[None]
    (dz2,), (dlg, dlb_) = _rowwise_vjp(_f_gelu_ln, [z2], gl_par, [du, dvn], [BF16], tr=TR, name="gmlp_gelu_ln_bwd")
    grads["gmlp_ln_g"], grads["gmlp_ln_b"] = dlg, dlb_
    grads["gmlp_w_in"] = _mm(h2, dz2, ta=True, tn=2048, tk=512, name="gmlp_dwin")[None]
    dh2 = _mm(dz2, p["gmlp_w_in"], tb=True, name="gmlp_dh")
    gx, d_norm_mix[2] = _rms_bwd(x2, g2, dh2, gx, "rms_mix_bwd_2")

    gx, d_norm_mlp[1], d_mlp_in[1], d_mlp_out[1] = _mlp_bwd(
        gx, x_mid1, row(p["norm_mlp"][1]), p["mlp_w_in"][1], p["mlp_w_out"][1], mlp_saved1, 1)
    dqc = _mm(gx, p["conv_w_pw2"], tb=True, pro_a=_to_bf16, name="conv_dq")
    grads["conv_w_pw2"] = _mm(qc, gx, ta=True, pro_b=_to_bf16, tk=512, name="conv_dwpw2")[None]
    _, (db2,) = _rowwise_vjp(lambda t, b: (t + b,), [gx], [p["conv_b_pw2"]], [gx], [None], tr=TR, name="conv_db2")
    grads["conv_b_pw2"] = db2
    (dyc,), (dbdw, dcg, dcb) = _rowwise_vjp(_f_ln_silu, [yc], ln_par, [dqc], [F32], tr=TR, name="conv_ln_silu_bwd")
    grads["conv_b_dw"], grads["conv_ln_g"], grads["conv_ln_b"] = dbdw, dcg, dcb
    dyp = jnp.pad(dyc.reshape(B, S, D), ((0, 0), (0, CONV_HALO), (0, 0)))
    dzg, dwdw = _conv_bwd(zp, dyp, w_dw, R=256, tc=128, name="conv_dw_bwd")
    grads["conv_w_dw"] = dwdw[:CONV_WIDTH][None]
    (dz1,), (db1,) = _rowwise_vjp(_f_bias_glu, [z1], [p["conv_b_pw1"]], [dzg.reshape(T, D)], [BF16], tr=TR,
                                  name="conv_glu_bwd")
    grads["conv_b_pw1"] = db1
    grads["conv_w_pw1"] = _mm(h1, dz1, ta=True, tn=2048, tk=512, name="conv_dwpw1")[None]
    dh1 = _mm(dz1, p["conv_w_pw1"], tb=True, name="conv_dh")
    gx, d_norm_mix[1] = _rms_bwd(x1, g1, dh1, gx, "rms_mix_bwd_1")

    gx, d_norm_mlp[0], d_mlp_in[0], d_mlp_out[0] = _mlp_bwd(
        gx, x_mid0, row(p["norm_mlp"][0]), p["mlp_w_in"][0], p["mlp_w_out"][0], mlp_saved0, 0)
    (dz0,), _ = _rowwise_vjp(_f_glu, [z0], [], [gx], [BF16], tr=TR, name="s5_glu_bwd")
    grads["ssm_w_glu"] = _mm(ypre, dz0, ta=True, pro_a=lambda t: jax.nn.gelu(t).astype(BF16), tn=2048, tk=512,
                             name="s5_dwglu")[None]
    dypre = _mm(dz0, w_glu, tb=True, epi=lambda acc, yp: (jax.vjp(lambda t: jax.nn.gelu(t), yp)[1](acc)[0],),
                extras=(ypre,), name="s5_dypre")
    dh0, dbre, dbim, dcre, dcim, dabr, dabi, dd = _s5_bwd(
        dypre, h0, sxr, sxi, ser, sei, abr, abi, bre_b, bim_b, cre_b, cim_b, d_skip, B=B, L=S5_CHUNK, name="s5_bwd")
    s5_grads = s5_vjp((dabr, dabi, dbre, dbim, dcre, dcim))
    for nm, gv in zip(("ssm_a_re", "ssm_a_im", "ssm_b_re", "ssm_b_im", "ssm_c_re", "ssm_c_im", "ssm_log_dt"), s5_grads):
        grads[nm] = gv[None]
    grads["ssm_d"] = dd
    gx, d_norm_mix[0] = _rms_bwd(x, g0, dh0, gx, "rms_mix_bwd_0")

    grads["norm_mix"] = jnp.concatenate(d_norm_mix, axis=0)
    grads["norm_mlp"] = jnp.concatenate(d_norm_mlp, axis=0)
    grads["mlp_w_in"] = jnp.stack(d_mlp_in, axis=0)
    grads["mlp_w_out"] = jnp.stack(d_mlp_out, axis=0)
    return loss_part, gx.reshape(B, S, D), grads


WEIGHTS = ['norm_mix', 'norm_mlp', 'norm_final', 'ssm_a_re', 'ssm_a_im', 'ssm_b_re', 'ssm_b_im', 'ssm_c_re',
           'ssm_c_im', 'ssm_d', 'ssm_log_dt', 'ssm_w_glu', 'conv_w_pw1', 'conv_b_pw1', 'conv_w_dw', 'conv_b_dw',
           'conv_ln_g', 'conv_ln_b', 'conv_w_pw2', 'conv_b_pw2', 'gmlp_w_in', 'gmlp_ln_g', 'gmlp_ln_b', 'gmlp_w_s',
           'gmlp_b_s', 'gmlp_w_out', 'attn_w_qkv', 'attn_w_o', 'mlp_w_in', 'mlp_w_out']
BIG_AXIS = {'ssm_w_glu': -1, 'conv_w_pw1': -1, 'conv_w_pw2': -2, 'gmlp_w_in': -1, 'gmlp_w_out': -2,
            'attn_w_qkv': -1, 'attn_w_o': -1, 'mlp_w_in': -1, 'mlp_w_out': -2}
BIG = list(BIG_AXIS)
SMALL_SHARDED = ['conv_b_pw1', 'conv_w_dw', 'conv_b_dw', 'conv_ln_g', 'conv_ln_b', 'conv_b_pw2', 'gmlp_ln_g', 'gmlp_ln_b']
SMALL_REPL = [n for n in WEIGHTS if n not in BIG_AXIS and n not in SMALL_SHARDED]
SMALL = SMALL_REPL + SMALL_SHARDED
LANES = 128
FLAT_COLS = 1024


def _pack(arrs, cols, row_mult):
    flat = jnp.concatenate([a.reshape(-1) for a in arrs])
    per = cols * row_mult
    n = -(-flat.shape[0] // per) * per
    return jnp.pad(flat, (0, n - flat.shape[0])).reshape(n // cols, cols)


def _unpack(flat2d, shapes):
    flat = flat2d.reshape(-1)
    out, off = [], 0
    for s in shapes:
        n = int(np.prod(s))
        out.append(flat[off:off + n].reshape(s))
        off += n
    return out


def _chip_shards(a, axis):
    return jnp.split(a, N_CHIPS, axis=axis)


def kernel(x, norm_mix, norm_mlp, norm_final, ssm_a_re, ssm_a_im, ssm_b_re, ssm_b_im, ssm_c_re, ssm_c_im, ssm_d, ssm_log_dt, ssm_w_glu, conv_w_pw1, conv_b_pw1, conv_w_dw, conv_b_dw, conv_ln_g, conv_ln_b, conv_w_pw2, conv_b_pw2, gmlp_w_in, gmlp_ln_g, gmlp_ln_b, gmlp_w_s, gmlp_b_s, gmlp_w_out, attn_w_qkv, attn_w_o, mlp_w_in, mlp_w_out, loss_target, m_norm_mix, m_norm_mlp, m_norm_final, m_ssm_a_re, m_ssm_a_im, m_ssm_b_re, m_ssm_b_im, m_ssm_c_re, m_ssm_c_im, m_ssm_d, m_ssm_log_dt, m_ssm_w_glu, m_conv_w_pw1, m_conv_b_pw1, m_conv_w_dw, m_conv_b_dw, m_conv_ln_g, m_conv_ln_b, m_conv_w_pw2, m_conv_b_pw2, m_gmlp_w_in, m_gmlp_ln_g, m_gmlp_ln_b, m_gmlp_w_s, m_gmlp_b_s, m_gmlp_w_out, m_attn_w_qkv, m_attn_w_o, m_mlp_w_in, m_mlp_w_out, v_norm_mix, v_norm_mlp, v_norm_final, v_ssm_a_re, v_ssm_a_im, v_ssm_b_re, v_ssm_b_im, v_ssm_c_re, v_ssm_c_im, v_ssm_d, v_ssm_log_dt, v_ssm_w_glu, v_conv_w_pw1, v_conv_b_pw1, v_conv_w_dw, v_conv_b_dw, v_conv_ln_g, v_conv_ln_b, v_conv_w_pw2, v_conv_b_pw2, v_gmlp_w_in, v_gmlp_ln_g, v_gmlp_ln_b, v_gmlp_w_s, v_gmlp_b_s, v_gmlp_w_out, v_attn_w_qkv, v_attn_w_o, v_mlp_w_in, v_mlp_w_out):
    args = dict(locals())
    w = {n: args[n] for n in WEIGHTS}
    m = {n: args["m_" + n] for n in WEIGHTS}
    v = {n: args["v_" + n] for n in WEIGHTS}
    chip = 2 * lax.axis_index("x") + lax.axis_index("y")
    core = lax.axis_index("c")

    big_shapes = [w[n].shape for n in BIG]
    wflat = _pack([w[n].astype(BF16) for n in BIG], FLAT_COLS, 32)
    R = wflat.shape[0]
    wall = _allgather_chips(wflat.reshape(2, R // 2, FLAT_COLS), name="gather_weights")
    wall = wall.reshape(N_CHIPS, R, FLAT_COLS)
    p = {}
    per_chip = [_unpack(wall[k], big_shapes) for k in range(N_CHIPS)]
    for i, n in enumerate(BIG):
        p[n] = jnp.concatenate([per_chip[k][i] for k in range(N_CHIPS)], axis=BIG_AXIS[n])
    sm_shapes = [w[n].shape for n in SMALL_SHARDED]
    sflat = _pack([w[n] for n in SMALL_SHARDED], LANES, 8)
    rs = sflat.shape[0]
    sall = _allgather8(sflat, name="gather_small").reshape(8, rs, LANES)
    per_chip = [_unpack(sall[2 * k], sm_shapes) for k in range(N_CHIPS)]
    for i, n in enumerate(SMALL_SHARDED):
        p[n] = jnp.concatenate([per_chip[k][i] for k in range(N_CHIPS)], axis=-1)
    for n in SMALL_REPL:
        p[n] = w[n]
    for n in ('ssm_w_glu', 'conv_w_pw1', 'conv_w_pw2', 'gmlp_w_in', 'gmlp_w_out', 'attn_w_qkv', 'attn_w_o', 'conv_w_dw'):
        p[n] = p[n][0]

    loss_part, grad_x, g = _local_step(x, loss_target, p)
    loss = lax.psum(loss_part[0, 0], ("x", "y", "c"))

    owner_flat = []
    for k in range(N_CHIPS):
        owner_flat.append(_pack([_chip_shards(g[n].reshape(p_shape_full(w[n], BIG_AXIS[n])), BIG_AXIS[n])[k]
                                 for n in BIG], FLAT_COLS, 32))
    rh = R // 2
    g2 = jnp.stack(owner_flat, axis=0).reshape(N_CHIPS, 2, rh, FLAT_COLS).transpose(1, 0, 2, 3)
    from_sibling = _swap_half(g2, name="grads_swap_half")
    own_half = lax.dynamic_index_in_dim(g2, core, axis=0, keepdims=False)
    chip_sum = _rowwise(_f_add, [own_half.reshape(N_CHIPS * rh, FLAT_COLS), from_sibling.reshape(N_CHIPS * rh, FLAT_COLS)],
                        [], (BF16,), tr=512, name="grads_chip_sum")[0]
    gathered = _scatter_chips(chip_sum.reshape(N_CHIPS, rh, FLAT_COLS), name="grads_to_owner")
    total_half = _rowwise(lambda a, b, c, d: (((a + b) + c) + d,), [gathered[k] for k in range(N_CHIPS)], [], (F32,),
                          tr=512, name="grads_owner_sum")[0]
    total = _share_half(total_half, name="grads_share_half").reshape(R, FLAT_COLS)
    big_grads = dict(zip(BIG, _unpack(total, big_shapes)))

    small_full_shapes = [g[n].shape for n in SMALL]
    gs = _pack([g[n] for n in SMALL], LANES, 8)
    rg = gs.shape[0]
    gs_all = _allgather8(gs, name="gather_small_grads").reshape(8, rg, LANES)
    gs_sum = _rowwise(lambda *a: (functools.reduce(lambda s, t: s + t, a),), [gs_all[k] for k in range(8)], [], (F32,),
                      tr=rg, name="small_grads_sum")[0]
    small_grads = dict(zip(SMALL, _unpack(gs_sum, small_full_shapes)))
    for n in SMALL:
        small_grads[n] = small_grads[n].reshape(p_shape_full(w[n], -1 if n in SMALL_SHARDED else None))
    for n in SMALL_SHARDED:
        width = w[n].shape[-1]
        small_grads[n] = lax.dynamic_slice_in_dim(small_grads[n], chip * width, width, axis=-1)

    grad, delta, new_m, new_v = {}, {}, {}, {}
    for n in BIG:
        shape = w[n].shape
        two_d = lambda t: t.reshape(-1, shape[-1])
        grad[n] = big_grads[n]
        d_, m_, v_ = _adamw(two_d(w[n]), two_d(grad[n]), two_d(m[n]), two_d(v[n]), name="adamw_" + n)
        delta[n], new_m[n], new_v[n] = d_.reshape(shape), m_.reshape(shape), v_.reshape(shape)
    sm_own_shapes = [w[n].shape for n in SMALL]
    packed = [_pack([src[n] for n in SMALL], LANES, 8) for src in (w, small_grads, m, v)]
    outs = _adamw(*packed, name="adamw_small")
    for dst, flat in zip((delta, new_m, new_v), outs):
        dst.update(dict(zip(SMALL, _unpack(flat, sm_own_shapes))))
    for n in SMALL:
        grad[n] = small_grads[n]

    return (loss, grad_x, *[grad[n] for n in WEIGHTS], *[delta[n] for n in WEIGHTS],
            *[new_m[n] for n in WEIGHTS], *[new_v[n] for n in WEIGHTS])


def p_shape_full(shard, axis):
    s = list(shard.shape)
    if axis is not None:
        s[axis] *= N_CHIPS
    return tuple(s)
```

```python
import functools
import math

import jax
import jax.numpy as jnp
import numpy as np
from jax import lax
from jax.experimental import pallas as pl
from jax.experimental.pallas import tpu as pltpu

F32 = jnp.float32
BF16 = jnp.bfloat16
MESH = pl.DeviceIdType.MESH

EPS = 1e-6
SSM_GROUP = 16
SSM_STATE = 64
CONV_WIDTH = 31
CONV_HALO = 32
GMLP_CHUNK = 128
GMLP_HEADS = 4
ATT_DILS = (1, 4, 16)
ATT_BLK = 128
ATT_HEADS = 8
HEAD_DIM = 64
ATT_W = ATT_HEADS * HEAD_DIM
N_CHIPS = 4
ADAM_LR, ADAM_B1, ADAM_B2, ADAM_EPS, ADAM_WD, ADAM_STEP = 1e-3, 0.9, 0.999, 1e-8, 0.01, 10

VMEM_BYTES_V7X = 64 * 1024 * 1024
VMEM_LIMIT = VMEM_BYTES_V7X - 8 * 1024 * 1024
MASK_VALUE = -1e30
LANE_TILE = 128


def _cp(sem=None):
    return pltpu.CompilerParams(dimension_semantics=sem, vmem_limit_bytes=VMEM_LIMIT)


def _pick_tile(total, target):
    for cand in range(min(target, total) // LANE_TILE * LANE_TILE, 0, -LANE_TILE):
        if total % cand == 0:
            return cand
    return total


class _Stored:
    def __init__(self, arr, kind="plain", lead=()):
        self.arr, self.kind, self.lead = arr, kind, tuple(lead)
        r, c = arr.shape[-2:]
        self.shape = (r, c * N_CHIPS) if kind == "cols" else (r * N_CHIPS, c) if kind == "rows" else (r, c)

    def spec(self, br, bc, rc_of):
        lead, nl = self.lead, len(self.lead)
        if self.kind == "plain":
            return pl.BlockSpec((None,) * nl + (br, bc), lambda i, j, k: (*lead, *rc_of(i, j, k)))
        if self.kind == "cols":
            per = self.arr.shape[-1] // bc
            assert per * bc == self.arr.shape[-1]

            def imap(i, j, k):
                r, c = rc_of(i, j, k)
                return (c // per, *lead, r, c % per)
        else:
            per = self.arr.shape[-2] // br
            assert per * br == self.arr.shape[-2]

            def imap(i, j, k):
                r, c = rc_of(i, j, k)
                return (r // per, *lead, r % per, c)
        return pl.BlockSpec((None,) * (nl + 1) + (br, bc), imap)


def _mm(a, b, *, ta=False, tb=False, out_dtypes=(F32,), tm=1024, tn=1024, tk=1024,
        pro_a=None, pro_b=None, epi=None, extras=(), out=None, name):
    if ta:
        K, M = a.shape
    else:
        M, K = a.shape
    if not isinstance(b, _Stored):
        b = _Stored(b)
    N, Kb = b.shape if tb else b.shape[::-1]
    assert K == Kb, (a.shape, b.shape, ta, tb)
    n_unit = b.arr.shape[-1] if (b.kind == "cols" and not tb) else b.arr.shape[-2] if (b.kind == "rows" and tb) else N
    k_unit = b.arr.shape[-1] if (b.kind == "cols" and tb) else b.arr.shape[-2] if (b.kind == "rows" and not tb) else K
    m_unit = M
    if out is not None:
        m_unit, n_unit = out[4], math.gcd(n_unit, out[5])
    tm, tn, tk = _pick_tile(m_unit, tm), _pick_tile(n_unit, tn), _pick_tile(k_unit, tk)
    nk = K // tk
    a_spec = (pl.BlockSpec((tk, tm), lambda i, j, k: (k, i)) if ta
              else pl.BlockSpec((tm, tk), lambda i, j, k: (i, k)))
    b_spec = b.spec(tn, tk, lambda i, j, k: (j, k)) if tb else b.spec(tk, tn, lambda i, j, k: (k, j))
    ex_specs = []
    for e in extras:
        if e.shape[0] == 1:
            ex_specs.append(pl.BlockSpec((1, tn), lambda i, j, k: (0, j)))
        else:
            assert e.shape == (M, N), (e.shape, M, N)
            ex_specs.append(pl.BlockSpec((tm, tn), lambda i, j, k: (i, j)))
    dims = (((0 if ta else 1,), (1 if tb else 0,)), ((), ()))
    n_ex, n_out = len(extras), len(out_dtypes)
    direct = epi is None and n_out == 1 and out_dtypes[0] == F32
    use_acc = nk > 1 and not direct
    operands, aliases, alias_specs = [a, b.arr, *extras], {}, []
    if out is None:
        out_specs = [pl.BlockSpec((tm, tn), lambda i, j, k: (i, j)) for _ in out_dtypes]
        out_shape = [jax.ShapeDtypeStruct((M, N), dt) for dt in out_dtypes]
    else:
        shape, block_fn, imap_fn, alias = out[:4]
        assert n_out == 1
        out_specs = [pl.BlockSpec(block_fn(tm, tn), imap_fn(tm, tn))]
        out_shape = [jax.ShapeDtypeStruct(shape, out_dtypes[0])]
        if alias is not None:
            operands.append(alias)
            aliases = {len(operands) - 1: 0}
            alias_specs = [pl.BlockSpec(memory_space=pl.ANY)]
    n_in = len(operands)

    def finish(r, ex, outs):
        res = epi(r, *[e[...] for e in ex]) if epi is not None else (r,)
        for o, v in zip(outs, res):
            o[...] = v.astype(o.dtype)

    def body(*refs):
        a_ref, b_ref = refs[:2]
        ex = refs[2:2 + n_ex]
        outs = refs[n_in:n_in + n_out]
        at, bt = a_ref[...], b_ref[...]
        if pro_a is not None:
            at = pro_a(at)
        if pro_b is not None:
            bt = pro_b(bt)
        part = lax.dot_general(at, bt, dims, preferred_element_type=F32)
        if nk == 1:
            finish(part, ex, outs)
            return
        acc = refs[-1] if use_acc else outs[0]
        k = pl.program_id(2)

        @pl.when(k == 0)
        def _():
            acc[...] = part

        @pl.when(k > 0)
        def _():
            acc[...] += part

        if use_acc:
            @pl.when(k == nk - 1)
            def _():
                finish(acc[...], ex, outs)

    res = pl.pallas_call(
        body, grid=(M // tm, N // tn, nk),
        in_specs=[a_spec, b_spec] + ex_specs + alias_specs,
        out_specs=out_specs, out_shape=out_shape,
        scratch_shapes=[pltpu.VMEM((tm, tn), F32)] if use_acc else [],
        input_output_aliases=aliases,
        compiler_params=_cp(("parallel", "parallel", "arbitrary")), name=name,
    )(*operands)
    return res[0] if n_out == 1 else res


def _to_bf16(t):
    return t.astype(BF16)


def _pick_rows(total, target):
    for cand in range(min(target, total) // 8 * 8, 0, -8):
        if total % cand == 0:
            return cand
    return total


def _rowwise(f, rows, params, out_dtypes, *, tr, name):
    T = rows[0].shape[0]
    tr = _pick_rows(T, tr)
    nr, npar = len(rows), len(params)
    blk = [jax.ShapeDtypeStruct((tr, r.shape[1]), F32) for r in rows]
    blk += [jax.ShapeDtypeStruct(p.shape, F32) for p in params]
    out_avals = jax.eval_shape(f, *blk)

    def body(*refs):
        res = f(*[r[...].astype(F32) for r in refs[:nr + npar]])
        for o, v in zip(refs[nr + npar:], res):
            o[...] = v.astype(o.dtype)

    out = pl.pallas_call(
        body, grid=(T // tr,),
        in_specs=[pl.BlockSpec((tr, r.shape[1]), lambda i: (i, 0)) for r in rows]
        + [pl.BlockSpec(p.shape, lambda i, nd=p.ndim: (0,) * nd) for p in params],
        out_specs=[pl.BlockSpec((tr, o.shape[1]), lambda i: (i, 0)) for o in out_avals],
        out_shape=[jax.ShapeDtypeStruct((T, o.shape[1]), dt) for o, dt in zip(out_avals, out_dtypes)],
        compiler_params=_cp(("parallel",)), name=name,
    )(*rows, *params)
    return out


def _rowwise_vjp(f, rows, params, cots, drow_dtypes, *, adds=None, tr, name):
    adds = adds or {}
    T = rows[0].shape[0]
    tr = _pick_rows(T, tr)
    nr, npar, nc = len(rows), len(params), len(cots)
    want, want_dt = [], []
    for i, dt in enumerate(drow_dtypes):
        for one in (dt if isinstance(dt, tuple) else (dt,)):
            if one is not None:
                want.append(i)
                want_dt.append(one)
    add_idx = sorted(set(i for i in want if i in adds))
    add_arrays = [adds[i] for i in add_idx]
    na = len(add_arrays)

    def body(*refs):
        ins = [r[...].astype(F32) for r in refs[:nr + npar]]
        cvals = [r[...].astype(F32) for r in refs[nr + npar:nr + npar + nc]]
        avals = refs[nr + npar + nc:nr + npar + nc + na]
        outs = refs[nr + npar + nc + na:]
        _, vjp = jax.vjp(f, *ins)
        grads = vjp(tuple(cvals))
        for o, i in zip(outs[:len(want)], want):
            g = grads[i]
            if i in adds:
                g = g + avals[add_idx.index(i)][...].astype(F32)
            o[...] = g.astype(o.dtype)
        step = pl.program_id(0)
        for o, g in zip(outs[len(want):], grads[nr:]):
            @pl.when(step == 0)
            def _(o=o):
                o[...] = jnp.zeros_like(o)
            o[...] += g

    rspec = lambda r: pl.BlockSpec((tr, r.shape[1]), lambda i: (i, 0))
    pspec = lambda p: pl.BlockSpec(p.shape, lambda i, nd=p.ndim: (0,) * nd)
    out = pl.pallas_call(
        body, grid=(T // tr,),
        in_specs=[rspec(r) for r in rows] + [pspec(p) for p in params] + [rspec(c) for c in cots]
        + [rspec(a) for a in add_arrays],
        out_specs=[rspec(rows[i]) for i in want] + [pspec(p) for p in params],
        out_shape=[jax.ShapeDtypeStruct(rows[i].shape, dt) for i, dt in zip(want, want_dt)]
        + [jax.ShapeDtypeStruct(p.shape, F32) for p in params],
        compiler_params=_cp(("arbitrary",)), name=name,
    )(*rows, *params, *cots, *add_arrays)
    return out[:len(want)], out[len(want):]


def _f_rms(x, g):
    return (x * lax.rsqrt(jnp.mean(x * x, axis=-1, keepdims=True) + EPS) * g,)


def _ln(x, g, b):
    mu = jnp.mean(x, axis=-1, keepdims=True)
    var = jnp.mean(jnp.square(x - mu), axis=-1, keepdims=True)
    return (x - mu) * lax.rsqrt(var + EPS) * g + b


def _f_glu(z):
    d = z.shape[1] // 2
    return (z[:, :d] * jax.nn.sigmoid(z[:, d:]),)


def _f_bias_glu(z, b):
    return _f_glu(z + b)


def _f_ln_silu(y, b_dw, g, b):
    return (jax.nn.silu(_ln(y + b_dw, g, b)),)


def _f_gelu_ln(z, g, b):
    d = z.shape[1] // 2
    zz = jax.nn.gelu(z)
    return zz[:, :d], _ln(zz[:, d:], g, b)


def _f_gelu(y):
    return (jax.nn.gelu(y),)


def _f_merge(o0, o1, o2, l0, l1, l2):
    m = jnp.maximum(jnp.maximum(l0, l1), l2)
    e0, e1, e2 = jnp.exp(l0 - m), jnp.exp(l1 - m), jnp.exp(l2 - m)
    s = e0 + e1 + e2
    return ((e0 * o0 + e1 * o1 + e2 * o2) / s,)


def _f_add(a, b):
    return (a + b,)


def _loss_head(x, tgt, g, *, tr, name):
    T, D = x.shape
    tr = min(tr, T)

    def f(xv, gv, tv):
        y = _f_rms(xv, gv)[0]
        return 0.5 * jnp.mean(jnp.square(y - tv), axis=-1, keepdims=True)

    def body(x_ref, t_ref, g_ref, loss_ref, dx_ref, dxb_ref, dg_ref):
        tv = t_ref[...]
        l, vjp = jax.vjp(lambda xv, gv: f(xv, gv, tv), x_ref[...], g_ref[...])
        dx, dg = vjp(jnp.ones_like(l))
        dx_ref[...] = dx
        dxb_ref[...] = dx.astype(BF16)

        @pl.when(pl.program_id(0) == 0)
        def _():
            loss_ref[...] = jnp.zeros_like(loss_ref)
            dg_ref[...] = jnp.zeros_like(dg_ref)

        loss_ref[...] += jnp.sum(l)
        dg_ref[...] += dg

    return pl.pallas_call(
        body, grid=(T // tr,),
        in_specs=[pl.BlockSpec((tr, D), lambda i: (i, 0)), pl.BlockSpec((tr, D), lambda i: (i, 0)),
                  pl.BlockSpec((1, D), lambda i: (0, 0))],
        out_specs=[pl.BlockSpec((1, 128), lambda i: (0, 0)), pl.BlockSpec((tr, D), lambda i: (i, 0)),
                   pl.BlockSpec((tr, D), lambda i: (i, 0)), pl.BlockSpec((1, D), lambda i: (0, 0))],
        out_shape=[jax.ShapeDtypeStruct((1, 128), F32), jax.ShapeDtypeStruct((T, D), F32),
                   jax.ShapeDtypeStruct((T, D), BF16), jax.ShapeDtypeStruct((1, D), F32)],
        compiler_params=_cp(("arbitrary",)), name=name,
    )(x, tgt, g)


def _adamw(w, g, m, v, *, name):
    R, C = w.shape
    tr = _pick_rows(R, max(8, 2 * 1024 * 1024 // (4 * C)))
    c1 = 1.0 - ADAM_B1 ** ADAM_STEP
    c2 = 1.0 - ADAM_B2 ** ADAM_STEP

    def body(w_ref, g_ref, m_ref, v_ref, d_ref, nm_ref, nv_ref):
        gv = g_ref[...]
        nm = ADAM_B1 * m_ref[...] + (1.0 - ADAM_B1) * gv
        nv = ADAM_B2 * v_ref[...] + (1.0 - ADAM_B2) * jnp.square(gv)
        nm_ref[...] = nm
        nv_ref[...] = nv
        d_ref[...] = -ADAM_LR * ((nm / c1) / (jnp.sqrt(nv / c2) + ADAM_EPS) + ADAM_WD * w_ref[...])

    spec = pl.BlockSpec((tr, C), lambda i: (i, 0))
    return pl.pallas_call(
        body, grid=(R // tr,), in_specs=[spec] * 4, out_specs=[spec] * 3,
        out_shape=[jax.ShapeDtypeStruct((R, C), F32)] * 3,
        compiler_params=_cp(("parallel",)), name=name,
    )(w, g, m, v)


def _s5_prep(a_re, a_im, b_re, b_im, c_re, c_im, log_dt):
    G, N = a_re.shape
    P = b_re.shape[2]
    gpb = 128 // P
    nblk = G // gpb
    dt = jnp.exp(log_dt)[:, None]
    mag = jnp.exp(a_re * dt)
    abr, abi = mag * jnp.cos(a_im * dt), mag * jnp.sin(a_im * dt)
    den = a_re * a_re + a_im * a_im
    nr, ni = abr - 1.0, abi
    qr, qi = (nr * a_re + ni * a_im) / den, (ni * a_re - nr * a_im) / den
    bbr = qr[..., None] * b_re - qi[..., None] * b_im
    bbi = qr[..., None] * b_im + qi[..., None] * b_re
    eye = jnp.eye(gpb, dtype=F32)

    def expand_b(t):
        t = t.reshape(nblk, gpb, N, P).transpose(0, 1, 3, 2)
        return (t[:, :, :, None, :] * eye[None, :, None, :, None]).reshape(nblk, gpb * P, gpb * N)

    def expand_c(t):
        t = t.reshape(nblk, gpb, P, N).transpose(0, 1, 3, 2)
        return (t[:, :, :, None, :] * eye[None, :, None, :, None]).reshape(nblk, gpb * N, gpb * P)

    return (abr.reshape(1, G * N), abi.reshape(1, G * N), expand_b(bbr), expand_b(bbi),
            expand_c(c_re), expand_c(c_im))


def _s5_fwd(h, abr, abi, bre, bim, cre, cim, d, *, B, L, name):
    T, D = h.shape
    S = T // B
    L = min(L, S)
    nc = S // L
    nblk, cb, sb = bre.shape
    GN = abr.shape[1]

    def body(h_ref, ar_ref, ai_ref, bre_ref, bim_ref, cre_ref, cim_ref, d_ref,
             y_ref, yb_ref, xr_ref, xi_ref, er_ref, ei_ref, sr, si, car, cai):
        ci = pl.program_id(1)

        @pl.when(ci == 0)
        def _():
            car[...] = jnp.zeros_like(car)
            cai[...] = jnp.zeros_like(cai)

        for j in range(nblk):
            u = h_ref[:, j * cb:(j + 1) * cb]
            sr[:, j * sb:(j + 1) * sb] = jnp.dot(u, bre_ref[j], preferred_element_type=F32)
            si[:, j * sb:(j + 1) * sb] = jnp.dot(u, bim_ref[j], preferred_element_type=F32)
        ar, ai = ar_ref[...], ai_ref[...]

        def step(t, carry):
            pr, pi = carry
            nr = ar * pr - ai * pi + sr[pl.ds(t, 1), :]
            ni = ar * pi + ai * pr + si[pl.ds(t, 1), :]
            sr[pl.ds(t, 1), :] = nr
            si[pl.ds(t, 1), :] = ni
            return nr, ni

        pr, pi = lax.fori_loop(0, L, step, (car[...], cai[...]), unroll=4)
        car[...] = pr
        cai[...] = pi
        er_ref[0] = pr
        ei_ref[0] = pi
        for j in range(nblk):
            xr = sr[:, j * sb:(j + 1) * sb].astype(BF16)
            xi = si[:, j * sb:(j + 1) * sb].astype(BF16)
            xr_ref[:, j * sb:(j + 1) * sb] = xr
            xi_ref[:, j * sb:(j + 1) * sb] = xi
            y = (jnp.dot(xr, cre_ref[j], preferred_element_type=F32)
                 - jnp.dot(xi, cim_ref[j], preferred_element_type=F32))
            u = h_ref[:, j * cb:(j + 1) * cb].astype(F32)
            y = y + d_ref[:, j * cb:(j + 1) * cb] * u
            y_ref[:, j * cb:(j + 1) * cb] = y
            yb_ref[:, j * cb:(j + 1) * cb] = jax.nn.gelu(y).astype(BF16)

    tok = lambda w: pl.BlockSpec((L, w), lambda b, c: (b * nc + c, 0))
    whole = lambda p: pl.BlockSpec(p.shape, lambda b, c, nd=p.ndim: (0,) * nd)
    end = pl.BlockSpec((1, 1, GN), lambda b, c: (b * nc + c, 0, 0))
    return pl.pallas_call(
        body, grid=(B, nc),
        in_specs=[tok(D)] + [whole(p) for p in (abr, abi, bre, bim, cre, cim, d)],
        out_specs=[tok(D), tok(D), tok(GN), tok(GN), end, end],
        out_shape=[jax.ShapeDtypeStruct((T, D), F32), jax.ShapeDtypeStruct((T, D), BF16),
                   jax.ShapeDtypeStruct((T, GN), BF16),
                   jax.ShapeDtypeStruct((T, GN), BF16), jax.ShapeDtypeStruct((B * nc, 1, GN), F32),
                   jax.ShapeDtypeStruct((B * nc, 1, GN), F32)],
        scratch_shapes=[pltpu.VMEM((L, GN), F32), pltpu.VMEM((L, GN), F32),
                        pltpu.VMEM((1, GN), F32), pltpu.VMEM((1, GN), F32)],
        compiler_params=_cp(("arbitrary", "arbitrary")), name=name,
    )(h, abr, abi, bre, bim, cre, cim, d)


def _s5_bwd(dy, h, xr, xi, er, ei, abr, abi, bre, bim, cre, cim, d, *, B, L, name):
    T, D = h.shape
    S = T // B
    L = min(L, S)
    nc = S // L
    nblk, cb, sb = bre.shape
    GN = abr.shape[1]
    dims_nt = (((1,), (1,)), ((), ()))
    dims_tn = (((0,), (0,)), ((), ()))

    def body(dy_ref, h_ref, xr_ref, xi_ref, er_ref, ei_ref, ar_ref, ai_ref, bre_ref, bim_ref,
             cre_ref, cim_ref, d_ref,
             dh_ref, dbre_ref, dbim_ref, dcre_ref, dcim_ref, dar_ref, dai_ref, dd_ref,
             lr, li, car, cai):
        b, cstep = pl.program_id(0), pl.program_id(1)
        ci = nc - 1 - cstep

        @pl.when((b == 0) & (cstep == 0))
        def _():
            for r in (dbre_ref, dbim_ref, dcre_ref, dcim_ref, dar_ref, dai_ref, dd_ref):
                r[...] = jnp.zeros_like(r)

        @pl.when(cstep == 0)
        def _():
            car[...] = jnp.zeros_like(car)
            cai[...] = jnp.zeros_like(cai)

        for j in range(nblk):
            dyj = dy_ref[:, j * cb:(j + 1) * cb].astype(BF16)
            lr[:, j * sb:(j + 1) * sb] = lax.dot_general(dyj, cre_ref[j], dims_nt, preferred_element_type=F32)
            li[:, j * sb:(j + 1) * sb] = -lax.dot_general(dyj, cim_ref[j], dims_nt, preferred_element_type=F32)
        ar, ai = ar_ref[...], ai_ref[...]

        def step(s, carry):
            t = L - 1 - s
            pr, pi = carry
            nr = lr[pl.ds(t, 1), :] + ar * pr + ai * pi
            ni = li[pl.ds(t, 1), :] - ai * pr + ar * pi
            lr[pl.ds(t, 1), :] = nr
            li[pl.ds(t, 1), :] = ni
            return nr, ni

        pr, pi = lax.fori_loop(0, L, step, (car[...], cai[...]), unroll=4)
        car[...] = pr
        cai[...] = pi
        has_prev = (ci > 0).astype(F32)
        first_row = lax.broadcasted_iota(jnp.int32, (L, sb), 0) == 0
        for j in range(nblk):
            cs = slice(j * cb, (j + 1) * cb)
            ss = slice(j * sb, (j + 1) * sb)
            lrj, lij = lr[:, ss], li[:, ss]
            xrj, xij = xr_ref[:, ss], xi_ref[:, ss]
            pr_j = jnp.where(first_row, er_ref[0][:, ss] * has_prev, pltpu.roll(xrj.astype(F32), 1, 0))
            pi_j = jnp.where(first_row, ei_ref[0][:, ss] * has_prev, pltpu.roll(xij.astype(F32), 1, 0))
            dar_ref[:, ss] += jnp.sum(lrj * pr_j + lij * pi_j, axis=0, keepdims=True)
            dai_ref[:, ss] += jnp.sum(lij * pr_j - lrj * pi_j, axis=0, keepdims=True)
            lrb, lib = lrj.astype(BF16), lij.astype(BF16)
            hj = h_ref[:, cs]
            dyf = dy_ref[:, cs]
            dyj = dyf.astype(BF16)
            dbre_ref[j] += lax.dot_general(hj, lrb, dims_tn, preferred_element_type=F32)
            dbim_ref[j] += lax.dot_general(hj, lib, dims_tn, preferred_element_type=F32)
            dcre_ref[j] += lax.dot_general(xrj, dyj, dims_tn, preferred_element_type=F32)
            dcim_ref[j] -= lax.dot_general(xij, dyj, dims_tn, preferred_element_type=F32)
            du = (lax.dot_general(lrb, bre_ref[j], dims_nt, preferred_element_type=F32)
                  + lax.dot_general(lib, bim_ref[j], dims_nt, preferred_element_type=F32))
            dh_ref[:, cs] = du + d_ref[:, cs] * dyf
            dd_ref[:, cs] += jnp.sum(dyf * hj.astype(F32), axis=0, keepdims=True)

    tok = lambda w: pl.BlockSpec((L, w), lambda b, c: (b * nc + nc - 1 - c, 0))
    whole = lambda p: pl.BlockSpec(p.shape, lambda b, c, nd=p.ndim: (0,) * nd)
    prev_end = pl.BlockSpec((1, 1, GN), lambda b, c: (b * nc + jnp.maximum(nc - 2 - c, 0), 0, 0))
    params = (abr, abi, bre, bim, cre, cim, d)
    acc_shapes = [bre.shape, bim.shape, cre.shape, cim.shape, abr.shape, abi.shape, d.shape]
    out = pl.pallas_call(
        body, grid=(B, nc),
        in_specs=[tok(D), tok(D), tok(GN), tok(GN), prev_end, prev_end] + [whole(p) for p in params],
        out_specs=[tok(D)] + [pl.BlockSpec(s, lambda b, c, nd=len(s): (0,) * nd) for s in acc_shapes],
        out_shape=[jax.ShapeDtypeStruct((T, D), F32)] + [jax.ShapeDtypeStruct(s, F32) for s in acc_shapes],
        scratch_shapes=[pltpu.VMEM((L, GN), F32), pltpu.VMEM((L, GN), F32),
                        pltpu.VMEM((1, GN), F32), pltpu.VMEM((1, GN), F32)],
        compiler_params=_cp(("arbitrary", "arbitrary")), name=name,
    )(dy, h, xr, xi, er, ei, *params)
    return out


def _conv_fwd(zp, w, *, R, tc, name):
    B, SP, C = zp.shape
    S = SP - CONV_HALO
    R, tc = min(R, S), min(tc, C)

    def body(z_ref, w_ref, y_ref):
        def chunk(ci, _):
            start = pl.multiple_of(ci * R, 8)
            ze = z_ref[pl.ds(start, R + CONV_HALO), :]
            acc = jnp.zeros((R, tc), F32)
            for m in range(CONV_WIDTH):
                k = CONV_WIDTH - 1 - m
                sh = ze if m == 0 else pltpu.roll(ze, m, 0)
                acc = acc + w_ref[k:k + 1, :] * sh[CONV_HALO:, :]
            y_ref[pl.ds(start, R), :] = acc
            return 0

        lax.fori_loop(0, S // R, chunk, 0)

    return pl.pallas_call(
        body, grid=(B, C // tc),
        in_specs=[pl.BlockSpec((None, SP, tc), lambda b, c: (b, 0, c)),
                  pl.BlockSpec((32, tc), lambda b, c: (0, c))],
        out_specs=pl.BlockSpec((None, S, tc), lambda b, c: (b, 0, c)),
        out_shape=jax.ShapeDtypeStruct((B, S, C), F32),
        compiler_params=_cp(("parallel", "parallel")), name=name,
    )(zp, w)


def _conv_bwd(zp, dyp, w, *, R, tc, name):
    B, SP, C = zp.shape
    S = SP - CONV_HALO
    R, tc = min(R, S), min(tc, C)

    def body(z_ref, dy_ref, w_ref, dz_ref, dw_ref):
        @pl.when(pl.program_id(1) == 0)
        def _():
            dw_ref[...] = jnp.zeros_like(dw_ref)

        def chunk(ci, _):
            start = pl.multiple_of(ci * R, 8)
            ze = z_ref[pl.ds(start, R + CONV_HALO), :]
            de = dy_ref[pl.ds(start, R + CONV_HALO), :]
            dy = de[:R, :]
            acc = jnp.zeros((R, tc), F32)
            for m in range(CONV_WIDTH):
                k = CONV_WIDTH - 1 - m
                zs = ze if m == 0 else pltpu.roll(ze, m, 0)
                ds_ = de if m == 0 else pltpu.roll(de, R + CONV_HALO - m, 0)
                acc = acc + w_ref[k:k + 1, :] * ds_[:R, :]
                dw_ref[k:k + 1, :] += jnp.sum(dy * zs[CONV_HALO:, :], axis=0, keepdims=True)
            dz_ref[pl.ds(start, R), :] = acc
            return 0

        lax.fori_loop(0, S // R, chunk, 0)

    return pl.pallas_call(
        body, grid=(C // tc, B),
        in_specs=[pl.BlockSpec((None, SP, tc), lambda c, b: (b, 0, c)),
                  pl.BlockSpec((None, SP, tc), lambda c, b: (b, 0, c)),
                  pl.BlockSpec((32, tc), lambda c, b: (0, c))],
        out_specs=[pl.BlockSpec((None, S, tc), lambda c, b: (b, 0, c)),
                   pl.BlockSpec((32, tc), lambda c, b: (0, c))],
        out_shape=[jax.ShapeDtypeStruct((B, S, C), F32), jax.ShapeDtypeStruct((32, C), F32)],
        compiler_params=_cp(("parallel", "arbitrary")), name=name,
    )(zp, dyp, w)


def _gmlp_fwd(u, vn, ws, bcol, *, nck, name):
    T, E = u.shape
    H = ws.shape[0]
    he = E // H
    rows = nck * GMLP_CHUNK
    rows = min(rows, T)
    n_in = rows // GMLP_CHUNK

    def body(u_ref, v_ref, ws_ref, b_ref, o_ref):
        for c in range(n_in):
            rs = slice(c * GMLP_CHUNK, (c + 1) * GMLP_CHUNK)
            for hh in range(H):
                cs = slice(hh * he, (hh + 1) * he)
                v2 = jnp.dot(ws_ref[hh], v_ref[rs, cs].astype(BF16), preferred_element_type=F32)
                v2 = v2 + b_ref[:, hh:hh + 1]
                o_ref[rs, cs] = (u_ref[rs, cs] * v2).astype(o_ref.dtype)

    tok = pl.BlockSpec((rows, E), lambda i: (i, 0))
    return pl.pallas_call(
        body, grid=(T // rows,),
        in_specs=[tok, tok, pl.BlockSpec(ws.shape, lambda i: (0, 0, 0)), pl.BlockSpec(bcol.shape, lambda i: (0, 0))],
        out_specs=tok, out_shape=jax.ShapeDtypeStruct((T, E), BF16),
        compiler_params=_cp(("parallel",)), name=name,
    )(u, vn, ws, bcol)


def _gmlp_bwd(duv, u, vn, ws, bcol, *, nck, name):
    T, E = u.shape
    H = ws.shape[0]
    he = E // H
    rows = min(nck * GMLP_CHUNK, T)
    n_in = rows // GMLP_CHUNK
    dims_nt = (((1,), (1,)), ((), ()))
    dims_tn = (((0,), (0,)), ((), ()))

    def body(g_ref, u_ref, v_ref, ws_ref, b_ref, du_ref, dv_ref, dws_ref, db_ref):
        @pl.when(pl.program_id(0) == 0)
        def _():
            dws_ref[...] = jnp.zeros_like(dws_ref)
            db_ref[...] = jnp.zeros_like(db_ref)

        for c in range(n_in):
            rs = slice(c * GMLP_CHUNK, (c + 1) * GMLP_CHUNK)
            for hh in range(H):
                cs = slice(hh * he, (hh + 1) * he)
                vb = v_ref[rs, cs].astype(BF16)
                v2 = jnp.dot(ws_ref[hh], vb, preferred_element_type=F32) + b_ref[:, hh:hh + 1]
                g = g_ref[rs, cs]
                du_ref[rs, cs] = g * v2
                dv2 = g * u_ref[rs, cs]
                dv2b = dv2.astype(BF16)
                dv_ref[rs, cs] = lax.dot_general(ws_ref[hh], dv2b, dims_tn, preferred_element_type=F32)
                dws_ref[hh] += lax.dot_general(dv2b, vb, dims_nt, preferred_element_type=F32)
                db_ref[:, hh:hh + 1] += jnp.sum(dv2, axis=1, keepdims=True)

    tok = pl.BlockSpec((rows, E), lambda i: (i, 0))
    return pl.pallas_call(
        body, grid=(T // rows,),
        in_specs=[tok, tok, tok, pl.BlockSpec(ws.shape, lambda i: (0, 0, 0)), pl.BlockSpec(bcol.shape, lambda i: (0, 0))],
        out_specs=[tok, tok, pl.BlockSpec(ws.shape, lambda i: (0, 0, 0)), pl.BlockSpec(bcol.shape, lambda i: (0, 0))],
        out_shape=[jax.ShapeDtypeStruct((T, E), F32), jax.ShapeDtypeStruct((T, E), F32),
                   jax.ShapeDtypeStruct(ws.shape, F32), jax.ShapeDtypeStruct(bcol.shape, F32)],
        compiler_params=_cp(("arbitrary",)), name=name,
    )(duv, u, vn, ws, bcol)


def _att_masks(i, nbs, blocks_per_group, nbk):
    g = (i * nbk) // blocks_per_group
    nbm1 = jnp.where(g == 0, nbs[0] - 1, jnp.where(g == 1, nbs[1] - 1, nbs[2] - 1))
    ji = lax.broadcasted_iota(jnp.int32, (2 * ATT_BLK, ATT_BLK), 0)
    ii = lax.broadcasted_iota(jnp.int32, (2 * ATT_BLK, ATT_BLK), 1)
    dist = ii + ATT_BLK - ji
    band = (dist >= 0) & (dist <= ATT_BLK)
    cur = ji >= ATT_BLK
    return nbm1, band, cur


def _att_fwd(q, k, v, *, nbs, nbk, name):
    NB = q.shape[0]
    bpg = NB // 3
    nbk = min(nbk, bpg)
    scale = HEAD_DIM ** -0.5
    dims_nt = (((1,), (1,)), ((), ()))
    dims_tn = (((0,), (0,)), ((), ()))

    def body(q_ref, k_ref, kp_ref, v_ref, vp_ref, o_ref, lse_ref):
        i = pl.program_id(0)
        nbm1, band, cur = _att_masks(i, nbs, bpg, nbk)
        for jj in range(nbk):
            hp = ((i * nbk + jj) & nbm1) != 0
            kk = jnp.concatenate([kp_ref[0] if jj == 0 else k_ref[jj - 1], k_ref[jj]], axis=0)
            vv = jnp.concatenate([vp_ref[0] if jj == 0 else v_ref[jj - 1], v_ref[jj]], axis=0)
            st = lax.dot_general(kk, q_ref[jj], dims_nt, preferred_element_type=F32) * scale
            st = jnp.where(band & (cur | hp), st, MASK_VALUE)
            m = jnp.max(st, axis=0, keepdims=True)
            p = jnp.exp(st - m)
            l = jnp.sum(p, axis=0, keepdims=True)
            lse_ref[jj:jj + 1, :] = m + jnp.log(l)
            pn = (p / l).astype(BF16)
            o_ref[jj] = lax.dot_general(pn, vv, dims_tn, preferred_element_type=F32)

    blk = pl.BlockSpec((nbk, ATT_BLK, HEAD_DIM), lambda i: (i, 0, 0))
    prev = pl.BlockSpec((1, ATT_BLK, HEAD_DIM), lambda i: (jnp.maximum(i * nbk - 1, 0), 0, 0))
    return pl.pallas_call(
        body, grid=(NB // nbk,),
        in_specs=[blk, blk, prev, blk, prev],
        out_specs=[blk, pl.BlockSpec((nbk, ATT_BLK), lambda i: (i, 0))],
        out_shape=[jax.ShapeDtypeStruct((NB, ATT_BLK, HEAD_DIM), F32), jax.ShapeDtypeStruct((NB, ATT_BLK), F32)],
        compiler_params=_cp(("parallel",)), name=name,
    )(q, k, k, v, v)


def _att_bwd(q, k, v, do, lse, dlse, *, nbs, nbk, name):
    NB = q.shape[0]
    bpg = NB // 3
    nbk = min(nbk, bpg)
    scale = HEAD_DIM ** -0.5
    dims_nt = (((1,), (1,)), ((), ()))
    dims_tn = (((0,), (0,)), ((), ()))

    def body(q_ref, k_ref, kp_ref, v_ref, vp_ref, do_ref, lse_ref, dlse_ref,
             dq_ref, dkc_ref, dkp_ref, dvc_ref, dvp_ref):
        i = pl.program_id(0)
        nbm1, band, cur = _att_masks(i, nbs, bpg, nbk)
        for jj in range(nbk):
            hp = ((i * nbk + jj) & nbm1) != 0
            kk = jnp.concatenate([kp_ref[0] if jj == 0 else k_ref[jj - 1], k_ref[jj]], axis=0)
            vv = jnp.concatenate([vp_ref[0] if jj == 0 else v_ref[jj - 1], v_ref[jj]], axis=0)
            qj = q_ref[jj]
            doj = do_ref[jj].astype(BF16)
            st = lax.dot_general(kk, qj, dims_nt, preferred_element_type=F32) * scale
            st = jnp.where(band & (cur | hp), st, MASK_VALUE)
            p = jnp.exp(st - lse_ref[jj:jj + 1, :])
            dp = lax.dot_general(vv, doj, dims_nt, preferred_element_type=F32)
            delta = jnp.sum(p * dp, axis=0, keepdims=True)
            ds_ = p * (dp - delta + dlse_ref[jj:jj + 1, :]) * scale
            dsb = ds_.astype(BF16)
            dq_ref[jj] = lax.dot_general(dsb, kk, dims_tn, preferred_element_type=F32)
            dkk = jnp.dot(dsb, qj, preferred_element_type=F32)
            dvv = jnp.dot(p.astype(BF16), doj, preferred_element_type=F32)
            dkp_ref[jj] = dkk[:ATT_BLK]
            dkc_ref[jj] = dkk[ATT_BLK:]
            dvp_ref[jj] = dvv[:ATT_BLK]
            dvc_ref[jj] = dvv[ATT_BLK:]

    blk = pl.BlockSpec((nbk, ATT_BLK, HEAD_DIM), lambda i: (i, 0, 0))
    prev = pl.BlockSpec((1, ATT_BLK, HEAD_DIM), lambda i: (jnp.maximum(i * nbk - 1, 0), 0, 0))
    row = pl.BlockSpec((nbk, ATT_BLK), lambda i: (i, 0))
    return pl.pallas_call(
        body, grid=(NB // nbk,),
        in_specs=[blk, blk, prev, blk, prev, blk, row, row],
        out_specs=[blk] * 5,
        out_shape=[jax.ShapeDtypeStruct((NB, ATT_BLK, HEAD_DIM), F32)] * 5,
        compiler_params=_cp(("parallel",)), name=name,
    )(q, k, k, v, v, do, lse, dlse)


def _to_blocks(t, B, S, dil):
    H, w = t.shape[1], t.shape[2]
    t = t.reshape(B, S // dil, dil, H, w).transpose(0, 2, 3, 1, 4)
    return t.reshape(B * dil * H * (S // dil // ATT_BLK), ATT_BLK, w)


def _from_blocks(t, B, S, dil, H):
    w = t.shape[2]
    t = t.reshape(B, dil, H, S // dil, w).transpose(0, 3, 1, 2, 4)
    return t.reshape(B * S, H, w)


def _mesh_pos():
    return lax.axis_index("x"), lax.axis_index("y"), lax.axis_index("c")


def _allgather8(xs, *, name):
    m_per, n = xs.shape

    def body(x_ref, out_ref, send_sems, recv_sems, local_sem):
        x, y, c = _mesh_pos()
        me, sibling = (x, y, c), (x, y, 1 - c)
        chips = [(1 - x, y), (x, 1 - y), (1 - x, 1 - y)]

        def rows(px, py, pc):
            return out_ref.at[pl.ds((4 * px + 2 * py + pc) * m_per, m_per), :]

        def copy(k, block, to, src=None):
            return pltpu.make_async_remote_copy(
                src_ref=rows(*block) if src is None else src, dst_ref=rows(*block),
                send_sem=send_sems.at[k], recv_sem=recv_sems.at[k], device_id=to, device_id_type=MESH)

        mine = pltpu.make_async_copy(x_ref, rows(*me), local_sem)
        mine.start()
        first = [copy(0, me, sibling, src=x_ref)]
        first += [copy(1 + j, me, (*chip, c), src=x_ref) for j, chip in enumerate(chips)]
        for cp in first:
            cp.start()
        passed = [copy(4 + j, (*chip, c), sibling) for j, chip in enumerate(chips)]
        for j, chip in enumerate(chips):
            copy(1 + j, (*chip, c), me).wait_recv()
            passed[j].start()
        copy(0, sibling, me).wait_recv()
        for j, chip in enumerate(chips):
            copy(4 + j, (*chip, 1 - c), me).wait_recv()
        for cp in first + passed:
            cp.wait_send()
        mine.wait()

    return pl.pallas_call(
        body, out_shape=jax.ShapeDtypeStruct((8 * m_per, n), xs.dtype),
        in_specs=[pl.BlockSpec(memory_space=pltpu.VMEM)], out_specs=pl.BlockSpec(memory_space=pltpu.VMEM),
        scratch_shapes=[pltpu.SemaphoreType.DMA((7,)), pltpu.SemaphoreType.DMA((7,)), pltpu.SemaphoreType.DMA],
        compiler_params=pltpu.CompilerParams(vmem_limit_bytes=VMEM_LIMIT), name=name,
    )(xs)


def _hbm_call(body, arrays, out_shapes, n_sems, *, name):
    any_spec = pl.BlockSpec(memory_space=pl.ANY)
    return pl.pallas_call(
        body, out_shape=out_shapes, in_specs=[any_spec] * len(arrays), out_specs=[any_spec] * len(out_shapes),
        scratch_shapes=[pltpu.SemaphoreType.DMA((n_sems,)), pltpu.SemaphoreType.DMA((n_sems,))], name=name,
    )(*arrays)


def _other_chips(x, y):
    return [(1 - x, y), (x, 1 - y), (1 - x, 1 - y)]


def _allgather_chips(ws, *, name):
    n = len(ws)

    def body(*refs):
        ins, outs, (send_sems, recv_sems) = refs[:n], refs[n:2 * n], refs[2 * n:]
        x, y, c = _mesh_pos()
        chips = _other_chips(x, y)

        def copy(a, k, px, py, half, to, src=None):
            slot = outs[a].at[2 * px + py, half]
            return pltpu.make_async_remote_copy(
                src_ref=slot if src is None else src, dst_ref=slot,
                send_sem=send_sems.at[6 * a + k], recv_sem=recv_sems.at[6 * a + k], device_id=to, device_id_type=MESH)

        first = [copy(a, j, x, y, c, (*chip, c), src=ins[a].at[c]) for a in range(n) for j, chip in enumerate(chips)]
        for cp in first:
            cp.start()
        passed = []
        for j, chip in enumerate(chips):
            for a in range(n):
                copy(a, j, *chip, c, (x, y, c)).wait_recv()
                passed.append(copy(a, 3 + j, *chip, c, (x, y, 1 - c)))
                passed[-1].start()
        for j, chip in enumerate(chips):
            for a in range(n):
                copy(a, 3 + j, *chip, 1 - c, (x, y, c)).wait_recv()
        for cp in first + passed:
            cp.wait_send()

    return _hbm_call(body, ws, [jax.ShapeDtypeStruct((N_CHIPS,) + w.shape, w.dtype) for w in ws], 6 * n, name=name)


def _swap_halves(gs, *, name):
    n = len(gs)

    def body(*refs):
        ins, outs, (send_sems, recv_sems) = refs[:n], refs[n:2 * n], refs[2 * n:]
        x, y, c = _mesh_pos()
        cps = [pltpu.make_async_remote_copy(
            src_ref=ins[a].at[1 - c], dst_ref=outs[a], send_sem=send_sems.at[a], recv_sem=recv_sems.at[a],
            device_id=(x, y, 1 - c), device_id_type=MESH) for a in range(n)]
        for cp in cps:
            cp.start()
        for cp in cps:
            cp.wait()

    return _hbm_call(body, gs, [jax.ShapeDtypeStruct(g.shape[1:], g.dtype) for g in gs], n, name=name)


def _scatter_chips(ss, *, name):
    n = len(ss)

    def body(*refs):
        ins, outs, (send_sems, recv_sems) = refs[:n], refs[n:2 * n], refs[2 * n:]
        x, y, c = _mesh_pos()
        me = 2 * x + y
        chips = _other_chips(x, y)

        def copy(a, j, px, py):
            return pltpu.make_async_remote_copy(
                src_ref=ins[a].at[2 * px + py], dst_ref=outs[a].at[me],
                send_sem=send_sems.at[3 * a + j], recv_sem=recv_sems.at[3 * a + j],
                device_id=(px, py, c), device_id_type=MESH)

        def arrival(a, j, px, py):
            return pltpu.make_async_remote_copy(
                src_ref=ins[a].at[me], dst_ref=outs[a].at[2 * px + py],
                send_sem=send_sems.at[3 * a + j], recv_sem=recv_sems.at[3 * a + j],
                device_id=(px, py, c), device_id_type=MESH)

        cps = [copy(a, j, *chip) for a in range(n) for j, chip in enumerate(chips)]
        for cp in cps:
            cp.start()
        for a in range(n):
            for j, chip in enumerate(chips):
                arrival(a, j, *chip).wait_recv()
        for cp in cps:
            cp.wait_send()

    return _hbm_call(body, ss, [jax.ShapeDtypeStruct(s.shape, s.dtype) for s in ss], 3 * n, name=name)


def _share_halves(ts, *, name):
    n = len(ts)

    def body(*refs):
        ins, outs, (send_sems, recv_sems) = refs[:n], refs[n:2 * n], refs[2 * n:]
        x, y, c = _mesh_pos()
        cps = [pltpu.make_async_remote_copy(
            src_ref=ins[a], dst_ref=outs[a].at[c], send_sem=send_sems.at[a], recv_sem=recv_sems.at[a],
            device_id=(x, y, 1 - c), device_id_type=MESH) for a in range(n)]
        for cp in cps:
            cp.start()
        for a in range(n):
            pltpu.make_async_remote_copy(
                src_ref=ins[a], dst_ref=outs[a].at[1 - c], send_sem=send_sems.at[a], recv_sem=recv_sems.at[a],
                device_id=(x, y, 1 - c), device_id_type=MESH).wait_recv()
        for cp in cps:
            cp.wait_send()

    return _hbm_call(body, ts, [jax.ShapeDtypeStruct((2,) + t.shape, t.dtype) for t in ts], n, name=name)


def _half_add(g, ra, core, *, name):
    _, R, C = g.shape
    tr = _pick_rows(R, max(8, 2 * 1024 * 1024 // (4 * C)))

    def body(core_ref, g_ref, ra_ref, o_ref):
        o_ref[...] = (g_ref[...] + ra_ref[...]).astype(o_ref.dtype)

    return pl.pallas_call(
        body, out_shape=jax.ShapeDtypeStruct((R, C), BF16),
        grid_spec=pltpu.PrefetchScalarGridSpec(
            num_scalar_prefetch=1, grid=(R // tr,),
            in_specs=[pl.BlockSpec((None, tr, C), lambda i, cr: (cr[0], i, 0)),
                      pl.BlockSpec((tr, C), lambda i, cr: (i, 0))],
            out_specs=pl.BlockSpec((tr, C), lambda i, cr: (i, 0))),
        compiler_params=_cp(("parallel",)), name=name,
    )(core.reshape(1).astype(jnp.int32), g, ra)


def _sum4(rb, *, name):
    _, R, C = rb.shape
    tr = _pick_rows(R, max(8, 2 * 1024 * 1024 // (4 * C)))

    def body(r0, r1, r2, r3, o_ref):
        f = lambda r: r[...].astype(F32)
        o_ref[...] = ((f(r0) + f(r1)) + f(r2)) + f(r3)

    return pl.pallas_call(
        body, out_shape=jax.ShapeDtypeStruct((R, C), F32), grid=(R // tr,),
        in_specs=[pl.BlockSpec((None, tr, C), lambda i, k=k: (k, i, 0)) for k in range(N_CHIPS)],
        out_specs=pl.BlockSpec((tr, C), lambda i: (i, 0)),
        compiler_params=_cp(("parallel",)), name=name,
    )(rb, rb, rb, rb)


TR = 256
S5_CHUNK = 256


def _rms_fwd(x, g, name):
    return _rowwise(_f_rms, [x], [g], (BF16,), tr=TR, name=name)[0]


def _rms_bwd(x, g, dh, gx, name):
    (dx, dxb), (dg,) = _rowwise_vjp(_f_rms, [x], [g], [dh], [(F32, BF16)], adds={0: gx}, tr=TR, name=name)
    return dx, dxb, dg


def _grad_cols(M, Nq):
    def imap(tm, tn):
        hp, per = (M // 2) // tm, Nq // tn
        assert hp * tm * 2 == M and per * tn == Nq, (M, Nq, tm, tn)
        return lambda i, j, k: (i // hp, j // per, i % hp, j % per)
    return (2, N_CHIPS, M // 2, Nq), lambda tm, tn: (None, None, tm, tn), imap, None, M // 2, Nq


def _grad_rows(Mq, N):
    def imap(tm, tn):
        po, hp = Mq // tm, (Mq // 2) // tm
        assert hp * tm * 2 == Mq, (Mq, tm)
        return lambda i, j, k: ((i % po) // hp, i // po, (i % po) % hp, j)
    return (2, N_CHIPS, Mq // 2, N), lambda tm, tn: (None, None, tm, tn), imap, None, Mq // 2, N


def _grad_layer_cols(li, nl, M, Nq, buf):
    lh = nl // 2

    def imap(tm, tn):
        per = Nq // tn
        return lambda i, j, k: (li // lh, j // per, li % lh, i, j % per)
    return (2, N_CHIPS, lh, M, Nq), lambda tm, tn: (None, None, None, tm, tn), imap, buf, M, Nq


def _grad_layer_rows(li, nl, Mq, N, buf):
    lh = nl // 2

    def imap(tm, tn):
        po = Mq // tm
        return lambda i, j, k: (li // lh, i // po, li % lh, i % po, j)
    return (2, N_CHIPS, lh, Mq, N), lambda tm, tn: (None, None, None, tm, tn), imap, buf, Mq, N


def _mlp_fwd(x, g, w_in, w_out, li):
    h2 = _rms_fwd(x, g, f"mlp_rms_{li}")
    r = _mm(h2, w_in, out_dtypes=(BF16,), epi=lambda acc: (jnp.maximum(acc, 0.0),), name=f"mlp_in_{li}")
    x_out = _mm(r, w_out, pro_a=lambda t: t * t, epi=lambda acc, res: (acc + res,), extras=(x,),
                name=f"mlp_out_{li}")
    return x_out, (h2, r)


def _mlp_bwd(gx, gxb, x, g, w_in, w_out, saved, li, nl, bufs):
    h2, r = saved
    D, F = w_in.shape
    da = _mm(gxb, w_out, tb=True, out_dtypes=(BF16,),
             epi=lambda acc, rt: (acc * 2.0 * rt.astype(F32),), extras=(r,), name=f"mlp_dact_{li}")
    buf_in, buf_out = bufs if bufs is not None else (None, None)
    d_w_out = _mm(r, gxb, ta=True, pro_a=lambda t: t * t, tm=512, tn=1024, tk=2048,
                  out=_grad_layer_rows(li, nl, F // N_CHIPS, D, buf_out), name=f"mlp_dwout_{li}")
    d_w_in = _mm(h2, da, ta=True, tm=1024, tn=1024, tk=2048,
                 out=_grad_layer_cols(li, nl, D, F // N_CHIPS, buf_in), name=f"mlp_dwin_{li}")
    dh2 = _mm(da, w_in, tb=True, name=f"mlp_dh_{li}")
    gx_mid, gxb_mid, dg = _rms_bwd(x, g, dh2, gx, f"mlp_rms_bwd_{li}")
    return gx_mid, gxb_mid, dg, (d_w_in, d_w_out)


def _local_step(x3, tgt3, p):
    B, S, D = x3.shape
    T = B * S
    x = x3.reshape(T, D)
    grads = {}
    row = lambda v: v.reshape(1, -1)

    xs = [x]
    g0 = row(p["norm_mix"][0])
    h0 = _rms_fwd(x, g0, "rms_mix_0")
    s5_args = (p["ssm_a_re"][0], p["ssm_a_im"][0], p["ssm_b_re"][0], p["ssm_b_im"][0],
               p["ssm_c_re"][0], p["ssm_c_im"][0], p["ssm_log_dt"][0])
    s5_exp, s5_vjp = jax.vjp(_s5_prep, *s5_args)
    abr, abi, bre, bim, cre, cim = s5_exp
    bre_b, bim_b, cre_b, cim_b = (t.astype(BF16) for t in (bre, bim, cre, cim))
    d_skip = p["ssm_d"]
    ypre, yb, sxr, sxi, ser, sei = _s5_fwd(h0, abr, abi, bre_b, bim_b, cre_b, cim_b, d_skip, B=B, L=S5_CHUNK,
                                           name="s5_fwd")
    w_glu = p["ssm_w_glu"]
    z0 = _mm(yb, w_glu, name="s5_glu_mm")
    x_mid0 = _rowwise(lambda z, xr: (_f_glu(z)[0] + xr,), [z0, x], [], (F32,), tr=TR, name="s5_glu")[0]
    nl = len(p["mlp_w_in"])
    x1, mlp_saved0 = _mlp_fwd(x_mid0, row(p["norm_mlp"][0]), p["mlp_w_in"][0], p["mlp_w_out"][0], 0)

    g1 = row(p["norm_mix"][1])
    h1 = _rms_fwd(x1, g1, "rms_mix_1")
    z1 = _mm(h1, p["conv_w_pw1"], name="conv_pw1")
    zg = _rowwise(_f_bias_glu, [z1], [p["conv_b_pw1"]], (F32,), tr=TR, name="conv_glu")[0]
    zp = jnp.pad(zg.reshape(B, S, D), ((0, 0), (CONV_HALO, 0), (0, 0)))
    w_dw = jnp.pad(p["conv_w_dw"], ((0, 32 - CONV_WIDTH), (0, 0)))
    yc = _conv_fwd(zp, w_dw, R=256, tc=128, name="conv_dw").reshape(T, D)
    ln_par = [p["conv_b_dw"], p["conv_ln_g"], p["conv_ln_b"]]
    qc = _rowwise(_f_ln_silu, [yc], ln_par, (BF16,), tr=TR, name="conv_ln_silu")[0]
    x_mid1 = _mm(qc, p["conv_w_pw2"], epi=lambda acc, bias, res: (acc + bias + res,),
                 extras=(p["conv_b_pw2"], x1), name="conv_pw2")
    x2, mlp_saved1 = _mlp_fwd(x_mid1, row(p["norm_mlp"][1]), p["mlp_w_in"][1], p["mlp_w_out"][1], 1)

    g2 = row(p["norm_mix"][2])
    h2 = _rms_fwd(x2, g2, "rms_mix_2")
    z2 = _mm(h2, p["gmlp_w_in"], name="gmlp_in")
    gl_par = [p["gmlp_ln_g"], p["gmlp_ln_b"]]
    gu, gvn = _rowwise(_f_gelu_ln, [z2], gl_par, (F32, F32), tr=TR, name="gmlp_gelu_ln")
    causal = jnp.tril(jnp.ones((GMLP_CHUNK, GMLP_CHUNK), dtype=bool))
    ws_b = jnp.where(causal[None], p["gmlp_w_s"][0], 0.0).astype(BF16)
    bcol = jnp.pad(p["gmlp_b_s"][0].T, ((0, 0), (0, 128 - GMLP_HEADS)))
    uv = _gmlp_fwd(gu, gvn, ws_b, bcol, nck=4, name="gmlp_spatial")
    x_mid2 = _mm(uv, p["gmlp_w_out"], epi=lambda acc, res: (acc + res,), extras=(x2,), name="gmlp_out")
    x3_, mlp_saved2 = _mlp_fwd(x_mid2, row(p["norm_mlp"][2]), p["mlp_w_in"][2], p["mlp_w_out"][2], 2)

    g3 = row(p["norm_mix"][3])
    h3 = _rms_fwd(x3_, g3, "rms_mix_3")
    qkv = _mm(h3, p["attn_w_qkv"], out_dtypes=(BF16,), tn=1152, name="attn_qkv")
    qkv6 = qkv.reshape(T, 3, len(ATT_DILS), ATT_HEADS, HEAD_DIM)
    nbs = tuple(S // dil // ATT_BLK for dil in ATT_DILS)
    qb, kb, vb = (jnp.concatenate([_to_blocks(qkv6[:, i, gi], B, S, dil) for gi, dil in enumerate(ATT_DILS)], axis=0)
                  for i in range(3))
    ob, lseb = _att_fwd(qb, kb, vb, nbs=nbs, nbk=8, name="attn_fwd")
    bpg = qb.shape[0] // 3
    o_tok, l_tok = [], []
    for gi, dil in enumerate(ATT_DILS):
        o_tok.append(_from_blocks(ob[gi * bpg:(gi + 1) * bpg], B, S, dil, ATT_HEADS).reshape(T * ATT_HEADS, HEAD_DIM))
        l_tok.append(_from_blocks(lseb[gi * bpg:(gi + 1) * bpg][:, :, None], B, S, dil, ATT_HEADS).reshape(T * ATT_HEADS, 1))
    merged = _rowwise(_f_merge, o_tok + l_tok, [], (BF16,), tr=2048, name="attn_merge")[0]
    merged2 = merged.reshape(T, ATT_W)
    x_mid3 = _mm(merged2, p["attn_w_o"], epi=lambda acc, res: (acc + res,), extras=(x3_,), name="attn_out")
    x4, mlp_saved3 = _mlp_fwd(x_mid3, row(p["norm_mlp"][3]), p["mlp_w_in"][3], p["mlp_w_out"][3], 3)

    loss_part, gx, gxb, dgf = _loss_head(x4, tgt3.reshape(T, D), row(p["norm_final"]), tr=TR, name="loss_head")
    grads["norm_final"] = dgf.reshape(-1)
    d_norm_mix, d_norm_mlp = [None] * 4, [None] * 4
    Dq = D // N_CHIPS

    gx, gxb, d_norm_mlp[3], mlp_bufs = _mlp_bwd(
        gx, gxb, x_mid3, row(p["norm_mlp"][3]), p["mlp_w_in"][3], p["mlp_w_out"][3], mlp_saved3, 3, nl, None)
    dmerged = _mm(gxb, p["attn_w_o"], tb=True, name="attn_dmerged")
    grads["attn_w_o"] = _mm(merged2, gxb, ta=True, tm=256, tn=256, tk=2048, out=_grad_cols(ATT_W, Dq), name="attn_dwo")
    dml, _ = _rowwise_vjp(_f_merge, o_tok + l_tok, [], [dmerged.reshape(T * ATT_HEADS, HEAD_DIM)], [F32] * 6,
                          tr=2048, name="attn_merge_bwd")
    dob = jnp.concatenate([_to_blocks(dml[gi].reshape(T, ATT_HEADS, HEAD_DIM), B, S, dil)
                           for gi, dil in enumerate(ATT_DILS)], axis=0)
    dlb = jnp.concatenate([_to_blocks(dml[3 + gi].reshape(T, ATT_HEADS, 1), B, S, dil)[:, :, 0]
                           for gi, dil in enumerate(ATT_DILS)], axis=0)
    dqb, dkc, dkp, dvc, dvp = _att_bwd(qb, kb, vb, dob, lseb, dlb, nbs=nbs, nbk=8, name="attn_bwd")
    nblk_all = qb.shape[0]
    flat = lambda t: t.reshape(nblk_all * ATT_BLK, HEAD_DIM)
    shift = lambda t: jnp.concatenate([t[1:], jnp.zeros_like(t[:1])], axis=0)
    dkb = _rowwise(_f_add, [flat(dkc), flat(shift(dkp))], [], (BF16,), tr=2048, name="attn_dk_add")[0]
    dvb = _rowwise(_f_add, [flat(dvc), flat(shift(dvp))], [], (BF16,), tr=2048, name="attn_dv_add")[0]
    parts = []
    for t in (dqb.astype(BF16), dkb.reshape(nblk_all, ATT_BLK, HEAD_DIM), dvb.reshape(nblk_all, ATT_BLK, HEAD_DIM)):
        parts.append(jnp.stack([_from_blocks(t[gi * bpg:(gi + 1) * bpg], B, S, dil, ATT_HEADS)
                                for gi, dil in enumerate(ATT_DILS)], axis=1))
    dqkv = jnp.stack(parts, axis=1).reshape(T, 3 * len(ATT_DILS) * ATT_W)
    qkv_w = 3 * len(ATT_DILS) * ATT_W
    grads["attn_w_qkv"] = _mm(h3, dqkv, ta=True, tm=512, tn=1152, tk=2048, out=_grad_cols(D, qkv_w // N_CHIPS),
                              name="attn_dwqkv")
    dh3 = _mm(dqkv, p["attn_w_qkv"], tb=True, tk=1152, name="attn_dh")
    gx, gxb, d_norm_mix[3] = _rms_bwd(x3_, g3, dh3, gx, "rms_mix_bwd_3")

    gx, gxb, d_norm_mlp[2], mlp_bufs = _mlp_bwd(
        gx, gxb, x_mid2, row(p["norm_mlp"][2]), p["mlp_w_in"][2], p["mlp_w_out"][2], mlp_saved2, 2, nl, mlp_bufs)
    duv = _mm(gxb, p["gmlp_w_out"], tb=True, name="gmlp_duv")
    grads["gmlp_w_out"] = _mm(uv, gxb, ta=True, tm=128, tn=1024, tk=2048, out=_grad_rows(Dq, D), name="gmlp_dwout")
    du, dvn, dws, dbcol = _gmlp_bwd(duv, gu, gvn, ws_b, bcol, nck=4, name="gmlp_spatial_bwd")
    grads["gmlp_w_s"] = jnp.where(causal[None], dws, 0.0)[None]
    grads["gmlp_b_s"] = dbcol[:, :GMLP_HEADS].T[None]
    (dz2,), (dlg, dlb_) = _rowwise_vjp(_f_gelu_ln, [z2], gl_par, [du, dvn], [BF16], tr=TR, name="gmlp_gelu_ln_bwd")
    grads["gmlp_ln_g"], grads["gmlp_ln_b"] = dlg, dlb_
    grads["gmlp_w_in"] = _mm(h2, dz2, ta=True, tm=512, tn=512, tk=2048, out=_grad_cols(D, 2 * Dq), name="gmlp_dwin")
    dh2 = _mm(dz2, p["gmlp_w_in"], tb=True, name="gmlp_dh")
    gx, gxb, d_norm_mix[2] = _rms_bwd(x2, g2, dh2, gx, "rms_mix_bwd_2")

    gx, gxb, d_norm_mlp[1], mlp_bufs = _mlp_bwd(
        gx, gxb, x_mid1, row(p["norm_mlp"][1]), p["mlp_w_in"][1], p["mlp_w_out"][1], mlp_saved1, 1, nl, mlp_bufs)
    dqc = _mm(gxb, p["conv_w_pw2"], tb=True, name="conv_dq")
    grads["conv_w_pw2"] = _mm(qc, gxb, ta=True, tm=128, tn=1024, tk=2048, out=_grad_rows(Dq, D), name="conv_dwpw2")
    _, (db2,) = _rowwise_vjp(lambda t, b: (t + b,), [gx], [p["conv_b_pw2"]], [gx], [None], tr=TR, name="conv_db2")
    grads["conv_b_pw2"] = db2
    (dyc,), (dbdw, dcg, dcb) = _rowwise_vjp(_f_ln_silu, [yc], ln_par, [dqc], [F32], tr=TR, name="conv_ln_silu_bwd")
    grads["conv_b_dw"], grads["conv_ln_g"], grads["conv_ln_b"] = dbdw, dcg, dcb
    dyp = jnp.pad(dyc.reshape(B, S, D), ((0, 0), (0, CONV_HALO), (0, 0)))
    dzg, dwdw = _conv_bwd(zp, dyp, w_dw, R=256, tc=128, name="conv_dw_bwd")
    grads["conv_w_dw"] = dwdw[:CONV_WIDTH][None]
    (dz1,), (db1,) = _rowwise_vjp(_f_bias_glu, [z1], [p["conv_b_pw1"]], [dzg.reshape(T, D)], [BF16], tr=TR,
                                  name="conv_glu_bwd")
    grads["conv_b_pw1"] = db1
    grads["conv_w_pw1"] = _mm(h1, dz1, ta=True, tm=512, tn=512, tk=2048, out=_grad_cols(D, 2 * Dq), name="conv_dwpw1")
    dh1 = _mm(dz1, p["conv_w_pw1"], tb=True, name="conv_dh")
    gx, gxb, d_norm_mix[1] = _rms_bwd(x1, g1, dh1, gx, "rms_mix_bwd_1")

    gx, gxb, d_norm_mlp[0], mlp_bufs = _mlp_bwd(
        gx, gxb, x_mid0, row(p["norm_mlp"][0]), p["mlp_w_in"][0], p["mlp_w_out"][0], mlp_saved0, 0, nl, mlp_bufs)
    (dz0,), _ = _rowwise_vjp(_f_glu, [z0], [], [gx], [BF16], tr=TR, name="s5_glu_bwd")
    grads["ssm_w_glu"] = _mm(yb, dz0, ta=True, tm=512, tn=512, tk=2048, out=_grad_cols(D, 2 * Dq), name="s5_dwglu")
    dypre = _mm(dz0, w_glu, tb=True, epi=lambda acc, yp: (jax.vjp(lambda t: jax.nn.gelu(t), yp)[1](acc)[0],),
                extras=(ypre,), name="s5_dypre")
    dh0, dbre, dbim, dcre, dcim, dabr, dabi, dd = _s5_bwd(
        dypre, h0, sxr, sxi, ser, sei, abr, abi, bre_b, bim_b, cre_b, cim_b, d_skip, B=B, L=S5_CHUNK, name="s5_bwd")
    s5_grads = s5_vjp((dabr, dabi, dbre, dbim, dcre, dcim))
    for nm, gv in zip(("ssm_a_re", "ssm_a_im", "ssm_b_re", "ssm_b_im", "ssm_c_re", "ssm_c_im", "ssm_log_dt"), s5_grads):
        grads[nm] = gv[None]
    grads["ssm_d"] = dd
    gx, _, d_norm_mix[0] = _rms_bwd(x, g0, dh0, gx, "rms_mix_bwd_0")

    grads["norm_mix"] = jnp.concatenate(d_norm_mix, axis=0)
    grads["norm_mlp"] = jnp.concatenate(d_norm_mlp, axis=0)
    grads["mlp_w_in"], grads["mlp_w_out"] = mlp_bufs
    return loss_part, gx.reshape(B, S, D), grads


WEIGHTS = ['norm_mix', 'norm_mlp', 'norm_final', 'ssm_a_re', 'ssm_a_im', 'ssm_b_re', 'ssm_b_im', 'ssm_c_re',
           'ssm_c_im', 'ssm_d', 'ssm_log_dt', 'ssm_w_glu', 'conv_w_pw1', 'conv_b_pw1', 'conv_w_dw', 'conv_b_dw',
           'conv_ln_g', 'conv_ln_b', 'conv_w_pw2', 'conv_b_pw2', 'gmlp_w_in', 'gmlp_ln_g', 'gmlp_ln_b', 'gmlp_w_s',
           'gmlp_b_s', 'gmlp_w_out', 'attn_w_qkv', 'attn_w_o', 'mlp_w_in', 'mlp_w_out']
BIG_AXIS = {'ssm_w_glu': -1, 'conv_w_pw1': -1, 'conv_w_pw2': -2, 'gmlp_w_in': -1, 'gmlp_w_out': -2,
            'attn_w_qkv': -1, 'attn_w_o': -1, 'mlp_w_in': -1, 'mlp_w_out': -2}
BIG = list(BIG_AXIS)
SMALL_SHARDED = ['conv_b_pw1', 'conv_w_dw', 'conv_b_dw', 'conv_ln_g', 'conv_ln_b', 'conv_b_pw2', 'gmlp_ln_g', 'gmlp_ln_b']
SMALL_REPL = [n for n in WEIGHTS if n not in BIG_AXIS and n not in SMALL_SHARDED]
SMALL = SMALL_REPL + SMALL_SHARDED
LANES = 128
FLAT_COLS = 1024


def _pack(arrs, cols, row_mult):
    flat = jnp.concatenate([a.reshape(-1) for a in arrs])
    per = cols * row_mult
    n = -(-flat.shape[0] // per) * per
    return jnp.pad(flat, (0, n - flat.shape[0])).reshape(n // cols, cols)


def _unpack(flat2d, shapes):
    flat = flat2d.reshape(-1)
    out, off = [], 0
    for s in shapes:
        n = int(np.prod(s))
        out.append(flat[off:off + n].reshape(s))
        off += n
    return out


def _as_halves(shard):
    if shard.shape[0] == 1:
        shard = shard[0]
    return shard.reshape((2, shard.shape[0] // 2) + shard.shape[1:])


def _stored_weight(name, arr):
    kind = "cols" if BIG_AXIS[name] == -1 else "rows"
    if arr.shape[1] > 1:
        return [_Stored(arr, kind, lead=(li,)) for li in range(arr.shape[1])]
    arr = arr[:, 0]
    if kind == "rows":
        return arr.reshape(-1, arr.shape[-1])
    return _Stored(arr, kind)


def kernel(x, norm_mix, norm_mlp, norm_final, ssm_a_re, ssm_a_im, ssm_b_re, ssm_b_im, ssm_c_re, ssm_c_im, ssm_d, ssm_log_dt, ssm_w_glu, conv_w_pw1, conv_b_pw1, conv_w_dw, conv_b_dw, conv_ln_g, conv_ln_b, conv_w_pw2, conv_b_pw2, gmlp_w_in, gmlp_ln_g, gmlp_ln_b, gmlp_w_s, gmlp_b_s, gmlp_w_out, attn_w_qkv, attn_w_o, mlp_w_in, mlp_w_out, loss_target, m_norm_mix, m_norm_mlp, m_norm_final, m_ssm_a_re, m_ssm_a_im, m_ssm_b_re, m_ssm_b_im, m_ssm_c_re, m_ssm_c_im, m_ssm_d, m_ssm_log_dt, m_ssm_w_glu, m_conv_w_pw1, m_conv_b_pw1, m_conv_w_dw, m_conv_b_dw, m_conv_ln_g, m_conv_ln_b, m_conv_w_pw2, m_conv_b_pw2, m_gmlp_w_in, m_gmlp_ln_g, m_gmlp_ln_b, m_gmlp_w_s, m_gmlp_b_s, m_gmlp_w_out, m_attn_w_qkv, m_attn_w_o, m_mlp_w_in, m_mlp_w_out, v_norm_mix, v_norm_mlp, v_norm_final, v_ssm_a_re, v_ssm_a_im, v_ssm_b_re, v_ssm_b_im, v_ssm_c_re, v_ssm_c_im, v_ssm_d, v_ssm_log_dt, v_ssm_w_glu, v_conv_w_pw1, v_conv_b_pw1, v_conv_w_dw, v_conv_b_dw, v_conv_ln_g, v_conv_ln_b, v_conv_w_pw2, v_conv_b_pw2, v_gmlp_w_in, v_gmlp_ln_g, v_gmlp_ln_b, v_gmlp_w_s, v_gmlp_b_s, v_gmlp_w_out, v_attn_w_qkv, v_attn_w_o, v_mlp_w_in, v_mlp_w_out):
    args = dict(locals())
    w = {n: args[n] for n in WEIGHTS}
    m = {n: args["m_" + n] for n in WEIGHTS}
    v = {n: args["v_" + n] for n in WEIGHTS}
    chip = 2 * lax.axis_index("x") + lax.axis_index("y")
    core = lax.axis_index("c")

    big_shapes = [w[n].shape for n in BIG]
    halves = {n: _as_halves(w[n].astype(BF16)) for n in BIG}
    gathered_w = _allgather_chips([halves[n] for n in BIG], name="gather_weights")
    p = {}
    for n, arr in zip(BIG, gathered_w):
        arr = lax.dynamic_update_index_in_dim(arr, halves[n], chip, axis=0)
        p[n] = _stored_weight(n, arr.reshape((N_CHIPS,) + w[n].shape))
    sm_shapes = [w[n].shape for n in SMALL_SHARDED]
    sflat = _pack([w[n] for n in SMALL_SHARDED], LANES, 8)
    rs = sflat.shape[0]
    sall = _allgather8(sflat, name="gather_small").reshape(8, rs, LANES)
    per_chip = [_unpack(sall[2 * k], sm_shapes) for k in range(N_CHIPS)]
    for i, n in enumerate(SMALL_SHARDED):
        p[n] = jnp.concatenate([per_chip[k][i] for k in range(N_CHIPS)], axis=-1)
    for n in SMALL_REPL:
        p[n] = w[n]
    p['conv_w_dw'] = p['conv_w_dw'][0]

    loss_part, grad_x, g = _local_step(x, loss_target, p)
    loss = lax.psum(loss_part[0, 0], ("x", "y", "c"))

    g3 = [g[n].reshape(2, N_CHIPS, -1, g[n].shape[-1]) for n in BIG]
    from_sibling = _swap_halves(g3, name="grads_swap_halves")
    chip_sums = []
    for n, own, sib in zip(BIG, g3, from_sibling):
        _, _, r, cols = own.shape
        s = _half_add(own.reshape(2, N_CHIPS * r, cols), sib.reshape(N_CHIPS * r, cols), core, name="chip_sum_" + n)
        chip_sums.append(s.reshape(N_CHIPS, r, cols))
    arrived = _scatter_chips(chip_sums, name="grads_to_owner")
    totals = []
    for n, arr, s in zip(BIG, arrived, chip_sums):
        mine = lax.dynamic_index_in_dim(s, chip, axis=0, keepdims=True)
        arr = lax.dynamic_update_index_in_dim(arr, mine, chip, axis=0)
        totals.append(_sum4(arr, name="owner_sum_" + n))
    shared = _share_halves(totals, name="grads_share_halves")
    big_grads = {}
    for n, arr, t in zip(BIG, shared, totals):
        arr = lax.dynamic_update_index_in_dim(arr, t[None], core, axis=0)
        big_grads[n] = arr.reshape(w[n].shape)

    small_full_shapes = [g[n].shape for n in SMALL]
    gs = _pack([g[n] for n in SMALL], LANES, 8)
    rg = gs.shape[0]
    gs_all = _allgather8(gs, name="gather_small_grads").reshape(8, rg, LANES)
    gs_sum = _rowwise(lambda *a: (functools.reduce(lambda s, t: s + t, a),), [gs_all[k] for k in range(8)], [], (F32,),
                      tr=rg, name="small_grads_sum")[0]
    small_grads = dict(zip(SMALL, _unpack(gs_sum, small_full_shapes)))
    for n in SMALL:
        small_grads[n] = small_grads[n].reshape(p_shape_full(w[n], -1 if n in SMALL_SHARDED else None))
    for n in SMALL_SHARDED:
        width = w[n].shape[-1]
        small_grads[n] = lax.dynamic_slice_in_dim(small_grads[n], chip * width, width, axis=-1)

    grad, delta, new_m, new_v = {}, {}, {}, {}
    for n in BIG:
        shape = w[n].shape
        two_d = lambda t: t.reshape(-1, shape[-1])
        grad[n] = big_grads[n]
        d_, m_, v_ = _adamw(two_d(w[n]), two_d(grad[n]), two_d(m[n]), two_d(v[n]), name="adamw_" + n)
        delta[n], new_m[n], new_v[n] = d_.reshape(shape), m_.reshape(shape), v_.reshape(shape)
    sm_own_shapes = [w[n].shape for n in SMALL]
    packed = [_pack([src[n] for n in SMALL], LANES, 8) for src in (w, small_grads, m, v)]
    outs = _adamw(*packed, name="adamw_small")
    for dst, flat in zip((delta, new_m, new_v), outs):
        dst.update(dict(zip(SMALL, _unpack(flat, sm_own_shapes))))
    for n in SMALL:
        grad[n] = small_grads[n]

    return (loss, grad_x, *[grad[n] for n in WEIGHTS], *[delta[n] for n in WEIGHTS],
            *[new_m[n] for n in WEIGHTS], *[new_v[n] for n in WEIGHTS])


def p_shape_full(shard, axis):
    s = list(shard.shape)
    if axis is not None:
        s[axis] *= N_CHIPS
    return tuple(s)
```

```python
import functools
import math

import jax
import jax.numpy as jnp
import numpy as np
from jax import lax
from jax.experimental import pallas as pl
from jax.experimental.pallas import tpu as pltpu

F32 = jnp.float32
BF16 = jnp.bfloat16
MESH = pl.DeviceIdType.MESH

EPS = 1e-6
SSM_GROUP = 16
SSM_STATE = 64
CONV_WIDTH = 31
CONV_HALO = 32
GMLP_CHUNK = 128
GMLP_HEADS = 4
ATT_DILS = (1, 4, 16)
ATT_BLK = 128
ATT_HEADS = 8
HEAD_DIM = 64
ATT_W = ATT_HEADS * HEAD_DIM
N_CHIPS = 4
ADAM_LR, ADAM_B1, ADAM_B2, ADAM_EPS, ADAM_WD, ADAM_STEP = 1e-3, 0.9, 0.999, 1e-8, 0.01, 10

VMEM_BYTES_V7X = 64 * 1024 * 1024
VMEM_LIMIT = VMEM_BYTES_V7X - 8 * 1024 * 1024
MASK_VALUE = -1e30
LANE_TILE = 128


def _cp(sem=None):
    return pltpu.CompilerParams(dimension_semantics=sem, vmem_limit_bytes=VMEM_LIMIT)


def _pick_tile(total, target):
    for cand in range(min(target, total) // LANE_TILE * LANE_TILE, 0, -LANE_TILE):
        if total % cand == 0:
            return cand
    return total


class _Stored:
    def __init__(self, arr, kind="plain", lead=()):
        self.arr, self.kind, self.lead = arr, kind, tuple(lead)
        r, c = arr.shape[-2:]
        self.shape = (r, c * N_CHIPS) if kind == "cols" else (r * N_CHIPS, c) if kind == "rows" else (r, c)

    def spec(self, br, bc, rc_of):
        lead, nl = self.lead, len(self.lead)
        if self.kind == "plain":
            return pl.BlockSpec((None,) * nl + (br, bc), lambda i, j, k: (*lead, *rc_of(i, j, k)))
        if self.kind == "cols":
            per = self.arr.shape[-1] // bc
            assert per * bc == self.arr.shape[-1]

            def imap(i, j, k):
                r, c = rc_of(i, j, k)
                return (c // per, *lead, r, c % per)
        else:
            per = self.arr.shape[-2] // br
            assert per * br == self.arr.shape[-2]

            def imap(i, j, k):
                r, c = rc_of(i, j, k)
                return (r // per, *lead, r % per, c)
        return pl.BlockSpec((None,) * (nl + 1) + (br, bc), imap)


def _mm(a, b, *, ta=False, tb=False, out_dtypes=(F32,), tm=1024, tn=1024, tk=1024,
        pro_a=None, pro_b=None, epi=None, extras=(), out=None, name):
    if ta:
        K, M = a.shape
    else:
        M, K = a.shape
    if not isinstance(b, _Stored):
        b = _Stored(b)
    N, Kb = b.shape if tb else b.shape[::-1]
    assert K == Kb, (a.shape, b.shape, ta, tb)
    n_unit = b.arr.shape[-1] if (b.kind == "cols" and not tb) else b.arr.shape[-2] if (b.kind == "rows" and tb) else N
    k_unit = b.arr.shape[-1] if (b.kind == "cols" and tb) else b.arr.shape[-2] if (b.kind == "rows" and not tb) else K
    m_unit = M
    if out is not None:
        m_unit, n_unit = out[4], math.gcd(n_unit, out[5])
    tm, tn, tk = _pick_tile(m_unit, tm), _pick_tile(n_unit, tn), _pick_tile(k_unit, tk)
    nk = K // tk
    a_spec = (pl.BlockSpec((tk, tm), lambda i, j, k: (k, i)) if ta
              else pl.BlockSpec((tm, tk), lambda i, j, k: (i, k)))
    b_spec = b.spec(tn, tk, lambda i, j, k: (j, k)) if tb else b.spec(tk, tn, lambda i, j, k: (k, j))
    ex_specs = []
    for e in extras:
        if e.shape[0] == 1:
            ex_specs.append(pl.BlockSpec((1, tn), lambda i, j, k: (0, j)))
        else:
            assert e.shape == (M, N), (e.shape, M, N)
            ex_specs.append(pl.BlockSpec((tm, tn), lambda i, j, k: (i, j)))
    dims = (((0 if ta else 1,), (1 if tb else 0,)), ((), ()))
    n_ex, n_out = len(extras), len(out_dtypes)
    direct = epi is None and n_out == 1 and out_dtypes[0] == F32
    use_acc = nk > 1 and not direct
    operands, aliases, alias_specs = [a, b.arr, *extras], {}, []
    if out is None:
        out_specs = [pl.BlockSpec((tm, tn), lambda i, j, k: (i, j)) for _ in out_dtypes]
        out_shape = [jax.ShapeDtypeStruct((M, N), dt) for dt in out_dtypes]
    else:
        shape, block_fn, imap_fn, alias = out[:4]
        assert n_out == 1
        out_specs = [pl.BlockSpec(block_fn(tm, tn), imap_fn(tm, tn))]
        out_shape = [jax.ShapeDtypeStruct(shape, out_dtypes[0])]
        if alias is not None:
            operands.append(alias)
            aliases = {len(operands) - 1: 0}
            alias_specs = [pl.BlockSpec(memory_space=pl.ANY)]
    n_in = len(operands)

    def finish(r, ex, outs):
        res = epi(r, *[e[...] for e in ex]) if epi is not None else (r,)
        for o, v in zip(outs, res):
            o[...] = v.astype(o.dtype)

    def body(*refs):
        a_ref, b_ref = refs[:2]
        ex = refs[2:2 + n_ex]
        outs = refs[n_in:n_in + n_out]
        at, bt = a_ref[...], b_ref[...]
        if pro_a is not None:
            at = pro_a(at)
        if pro_b is not None:
            bt = pro_b(bt)
        part = lax.dot_general(at, bt, dims, preferred_element_type=F32)
        if nk == 1:
            finish(part, ex, outs)
            return
        acc = refs[-1] if use_acc else outs[0]
        k = pl.program_id(2)

        @pl.when(k == 0)
        def _():
            acc[...] = part

        @pl.when(k > 0)
        def _():
            acc[...] += part

        if use_acc:
            @pl.when(k == nk - 1)
            def _():
                finish(acc[...], ex, outs)

    res = pl.pallas_call(
        body, grid=(M // tm, N // tn, nk),
        in_specs=[a_spec, b_spec] + ex_specs + alias_specs,
        out_specs=out_specs, out_shape=out_shape,
        scratch_shapes=[pltpu.VMEM((tm, tn), F32)] if use_acc else [],
        input_output_aliases=aliases,
        compiler_params=_cp(("parallel", "parallel", "arbitrary")), name=name,
    )(*operands)
    return res[0] if n_out == 1 else res


def _to_bf16(t):
    return t.astype(BF16)


def _pick_rows(total, target):
    for cand in range(min(target, total) // 8 * 8, 0, -8):
        if total % cand == 0:
            return cand
    return total


def _rowwise(f, rows, params, out_dtypes, *, tr, name):
    T = rows[0].shape[0]
    tr = _pick_rows(T, tr)
    nr, npar = len(rows), len(params)
    blk = [jax.ShapeDtypeStruct((tr, r.shape[1]), F32) for r in rows]
    blk += [jax.ShapeDtypeStruct(p.shape, F32) for p in params]
    out_avals = jax.eval_shape(f, *blk)

    def body(*refs):
        res = f(*[r[...].astype(F32) for r in refs[:nr + npar]])
        for o, v in zip(refs[nr + npar:], res):
            o[...] = v.astype(o.dtype)

    out = pl.pallas_call(
        body, grid=(T // tr,),
        in_specs=[pl.BlockSpec((tr, r.shape[1]), lambda i: (i, 0)) for r in rows]
        + [pl.BlockSpec(p.shape, lambda i, nd=p.ndim: (0,) * nd) for p in params],
        out_specs=[pl.BlockSpec((tr, o.shape[1]), lambda i: (i, 0)) for o in out_avals],
        out_shape=[jax.ShapeDtypeStruct((T, o.shape[1]), dt) for o, dt in zip(out_avals, out_dtypes)],
        compiler_params=_cp(("parallel",)), name=name,
    )(*rows, *params)
    return out


def _rowwise_vjp(f, rows, params, cots, drow_dtypes, *, adds=None, tr, name):
    adds = adds or {}
    T = rows[0].shape[0]
    tr = _pick_rows(T, tr)
    nr, npar, nc = len(rows), len(params), len(cots)
    want, want_dt = [], []
    for i, dt in enumerate(drow_dtypes):
        for one in (dt if isinstance(dt, tuple) else (dt,)):
            if one is not None:
                want.append(i)
                want_dt.append(one)
    add_idx = sorted(set(i for i in want if i in adds))
    add_arrays = [adds[i] for i in add_idx]
    na = len(add_arrays)

    def body(*refs):
        ins = [r[...].astype(F32) for r in refs[:nr + npar]]
        cvals = [r[...].astype(F32) for r in refs[nr + npar:nr + npar + nc]]
        avals = refs[nr + npar + nc:nr + npar + nc + na]
        outs = refs[nr + npar + nc + na:]
        _, vjp = jax.vjp(f, *ins)
        grads = vjp(tuple(cvals))
        for o, i in zip(outs[:len(want)], want):
            g = grads[i]
            if i in adds:
                g = g + avals[add_idx.index(i)][...].astype(F32)
            o[...] = g.astype(o.dtype)
        step = pl.program_id(0)
        for o, g in zip(outs[len(want):], grads[nr:]):
            @pl.when(step == 0)
            def _(o=o):
                o[...] = jnp.zeros_like(o)
            o[...] += g

    rspec = lambda r: pl.BlockSpec((tr, r.shape[1]), lambda i: (i, 0))
    pspec = lambda p: pl.BlockSpec(p.shape, lambda i, nd=p.ndim: (0,) * nd)
    out = pl.pallas_call(
        body, grid=(T // tr,),
        in_specs=[rspec(r) for r in rows] + [pspec(p) for p in params] + [rspec(c) for c in cots]
        + [rspec(a) for a in add_arrays],
        out_specs=[rspec(rows[i]) for i in want] + [pspec(p) for p in params],
        out_shape=[jax.ShapeDtypeStruct(rows[i].shape, dt) for i, dt in zip(want, want_dt)]
        + [jax.ShapeDtypeStruct(p.shape, F32) for p in params],
        compiler_params=_cp(("arbitrary",)), name=name,
    )(*rows, *params, *cots, *add_arrays)
    return out[:len(want)], out[len(want):]


def _f_rms(x, g):
    return (x * lax.rsqrt(jnp.mean(x * x, axis=-1, keepdims=True) + EPS) * g,)


def _ln(x, g, b):
    mu = jnp.mean(x, axis=-1, keepdims=True)
    var = jnp.mean(jnp.square(x - mu), axis=-1, keepdims=True)
    return (x - mu) * lax.rsqrt(var + EPS) * g + b


def _f_glu(z):
    d = z.shape[1] // 2
    return (z[:, :d] * jax.nn.sigmoid(z[:, d:]),)


def _f_bias_glu(z, b):
    return _f_glu(z + b)


def _f_ln_silu(y, b_dw, g, b):
    return (jax.nn.silu(_ln(y + b_dw, g, b)),)


def _f_gelu_ln(z, g, b):
    d = z.shape[1] // 2
    zz = jax.nn.gelu(z)
    return zz[:, :d], _ln(zz[:, d:], g, b)


def _f_gelu(y):
    return (jax.nn.gelu(y),)


def _f_merge(o0, o1, o2, l0, l1, l2):
    m = jnp.maximum(jnp.maximum(l0, l1), l2)
    e0, e1, e2 = jnp.exp(l0 - m), jnp.exp(l1 - m), jnp.exp(l2 - m)
    s = e0 + e1 + e2
    pair = 2 * HEAD_DIM
    first_head = lax.broadcasted_iota(jnp.int32, (o0.shape[0], pair), 1) < HEAD_DIM
    cols = []
    for hp in range(o0.shape[1] // pair):
        acc = None
        for o, e in ((o0, e0), (o1, e1), (o2, e2)):
            wgt = e / s
            wp = jnp.where(first_head, wgt[:, 2 * hp:2 * hp + 1], wgt[:, 2 * hp + 1:2 * hp + 2])
            term = wp * o[:, hp * pair:(hp + 1) * pair]
            acc = term if acc is None else acc + term
        cols.append(acc)
    return (jnp.concatenate(cols, axis=1),)


def _f_add(a, b):
    return (a + b,)


def _loss_head(x, tgt, g, *, tr, name):
    T, D = x.shape
    tr = min(tr, T)

    def f(xv, gv, tv):
        y = _f_rms(xv, gv)[0]
        return 0.5 * jnp.mean(jnp.square(y - tv), axis=-1, keepdims=True)

    def body(x_ref, t_ref, g_ref, loss_ref, dx_ref, dxb_ref, dg_ref):
        tv = t_ref[...]
        l, vjp = jax.vjp(lambda xv, gv: f(xv, gv, tv), x_ref[...], g_ref[...])
        dx, dg = vjp(jnp.ones_like(l))
        dx_ref[...] = dx
        dxb_ref[...] = dx.astype(BF16)

        @pl.when(pl.program_id(0) == 0)
        def _():
            loss_ref[...] = jnp.zeros_like(loss_ref)
            dg_ref[...] = jnp.zeros_like(dg_ref)

        loss_ref[...] += jnp.sum(l)
        dg_ref[...] += dg

    return pl.pallas_call(
        body, grid=(T // tr,),
        in_specs=[pl.BlockSpec((tr, D), lambda i: (i, 0)), pl.BlockSpec((tr, D), lambda i: (i, 0)),
                  pl.BlockSpec((1, D), lambda i: (0, 0))],
        out_specs=[pl.BlockSpec((1, 128), lambda i: (0, 0)), pl.BlockSpec((tr, D), lambda i: (i, 0)),
                   pl.BlockSpec((tr, D), lambda i: (i, 0)), pl.BlockSpec((1, D), lambda i: (0, 0))],
        out_shape=[jax.ShapeDtypeStruct((1, 128), F32), jax.ShapeDtypeStruct((T, D), F32),
                   jax.ShapeDtypeStruct((T, D), BF16), jax.ShapeDtypeStruct((1, D), F32)],
        compiler_params=_cp(("arbitrary",)), name=name,
    )(x, tgt, g)


def _adamw(w, g, m, v, *, name):
    R, C = w.shape
    tr = _pick_rows(R, max(8, 2 * 1024 * 1024 // (4 * C)))
    c1 = 1.0 - ADAM_B1 ** ADAM_STEP
    c2 = 1.0 - ADAM_B2 ** ADAM_STEP

    def body(w_ref, g_ref, m_ref, v_ref, d_ref, nm_ref, nv_ref):
        gv = g_ref[...]
        nm = ADAM_B1 * m_ref[...] + (1.0 - ADAM_B1) * gv
        nv = ADAM_B2 * v_ref[...] + (1.0 - ADAM_B2) * jnp.square(gv)
        nm_ref[...] = nm
        nv_ref[...] = nv
        d_ref[...] = -ADAM_LR * ((nm / c1) / (jnp.sqrt(nv / c2) + ADAM_EPS) + ADAM_WD * w_ref[...])

    spec = pl.BlockSpec((tr, C), lambda i: (i, 0))
    return pl.pallas_call(
        body, grid=(R // tr,), in_specs=[spec] * 4, out_specs=[spec] * 3,
        out_shape=[jax.ShapeDtypeStruct((R, C), F32)] * 3,
        compiler_params=_cp(("parallel",)), name=name,
    )(w, g, m, v)


def _s5_prep(a_re, a_im, b_re, b_im, c_re, c_im, log_dt):
    G, N = a_re.shape
    P = b_re.shape[2]
    gpb = 128 // P
    nblk = G // gpb
    dt = jnp.exp(log_dt)[:, None]
    mag = jnp.exp(a_re * dt)
    abr, abi = mag * jnp.cos(a_im * dt), mag * jnp.sin(a_im * dt)
    den = a_re * a_re + a_im * a_im
    nr, ni = abr - 1.0, abi
    qr, qi = (nr * a_re + ni * a_im) / den, (ni * a_re - nr * a_im) / den
    bbr = qr[..., None] * b_re - qi[..., None] * b_im
    bbi = qr[..., None] * b_im + qi[..., None] * b_re
    eye = jnp.eye(gpb, dtype=F32)

    def expand_b(t):
        t = t.reshape(nblk, gpb, N, P).transpose(0, 1, 3, 2)
        return (t[:, :, :, None, :] * eye[None, :, None, :, None]).reshape(nblk, gpb * P, gpb * N)

    def expand_c(t):
        t = t.reshape(nblk, gpb, P, N).transpose(0, 1, 3, 2)
        return (t[:, :, :, None, :] * eye[None, :, None, :, None]).reshape(nblk, gpb * N, gpb * P)

    return (abr.reshape(1, G * N), abi.reshape(1, G * N), expand_b(bbr), expand_b(bbi),
            expand_c(c_re), expand_c(c_im))


def _s5_fwd(h, abr, abi, bre, bim, cre, cim, d, *, B, L, name):
    T, D = h.shape
    S = T // B
    L = min(L, S)
    nc = S // L
    nblk, cb, sb = bre.shape
    GN = abr.shape[1]

    def body(h_ref, ar_ref, ai_ref, bre_ref, bim_ref, cre_ref, cim_ref, d_ref,
             y_ref, yb_ref, xr_ref, xi_ref, er_ref, ei_ref, sr, si, car, cai):
        ci = pl.program_id(1)

        @pl.when(ci == 0)
        def _():
            car[...] = jnp.zeros_like(car)
            cai[...] = jnp.zeros_like(cai)

        for j in range(nblk):
            u = h_ref[:, j * cb:(j + 1) * cb]
            sr[:, j * sb:(j + 1) * sb] = jnp.dot(u, bre_ref[j], preferred_element_type=F32)
            si[:, j * sb:(j + 1) * sb] = jnp.dot(u, bim_ref[j], preferred_element_type=F32)
        ar, ai = ar_ref[...], ai_ref[...]

        def step(t, carry):
            pr, pi = carry
            nr = ar * pr - ai * pi + sr[pl.ds(t, 1), :]
            ni = ar * pi + ai * pr + si[pl.ds(t, 1), :]
            sr[pl.ds(t, 1), :] = nr
            si[pl.ds(t, 1), :] = ni
            return nr, ni

        pr, pi = lax.fori_loop(0, L, step, (car[...], cai[...]), unroll=4)
        car[...] = pr
        cai[...] = pi
        er_ref[0] = pr
        ei_ref[0] = pi
        for j in range(nblk):
            xr = sr[:, j * sb:(j + 1) * sb].astype(BF16)
            xi = si[:, j * sb:(j + 1) * sb].astype(BF16)
            xr_ref[:, j * sb:(j + 1) * sb] = xr
            xi_ref[:, j * sb:(j + 1) * sb] = xi
            y = (jnp.dot(xr, cre_ref[j], preferred_element_type=F32)
                 - jnp.dot(xi, cim_ref[j], preferred_element_type=F32))
            u = h_ref[:, j * cb:(j + 1) * cb].astype(F32)
            y = y + d_ref[:, j * cb:(j + 1) * cb] * u
            y_ref[:, j * cb:(j + 1) * cb] = y
            yb_ref[:, j * cb:(j + 1) * cb] = jax.nn.gelu(y).astype(BF16)

    tok = lambda w: pl.BlockSpec((L, w), lambda b, c: (b * nc + c, 0))
    whole = lambda p: pl.BlockSpec(p.shape, lambda b, c, nd=p.ndim: (0,) * nd)
    end = pl.BlockSpec((1, 1, GN), lambda b, c: (b * nc + c, 0, 0))
    return pl.pallas_call(
        body, grid=(B, nc),
        in_specs=[tok(D)] + [whole(p) for p in (abr, abi, bre, bim, cre, cim, d)],
        out_specs=[tok(D), tok(D), tok(GN), tok(GN), end, end],
        out_shape=[jax.ShapeDtypeStruct((T, D), F32), jax.ShapeDtypeStruct((T, D), BF16),
                   jax.ShapeDtypeStruct((T, GN), BF16),
                   jax.ShapeDtypeStruct((T, GN), BF16), jax.ShapeDtypeStruct((B * nc, 1, GN), F32),
                   jax.ShapeDtypeStruct((B * nc, 1, GN), F32)],
        scratch_shapes=[pltpu.VMEM((L, GN), F32), pltpu.VMEM((L, GN), F32),
                        pltpu.VMEM((1, GN), F32), pltpu.VMEM((1, GN), F32)],
        compiler_params=_cp(("arbitrary", "arbitrary")), name=name,
    )(h, abr, abi, bre, bim, cre, cim, d)


def _s5_bwd(dy, h, xr, xi, er, ei, abr, abi, bre, bim, cre, cim, d, *, B, L, name):
    T, D = h.shape
    S = T // B
    L = min(L, S)
    nc = S // L
    nblk, cb, sb = bre.shape
    GN = abr.shape[1]
    dims_nt = (((1,), (1,)), ((), ()))
    dims_tn = (((0,), (0,)), ((), ()))

    def body(dy_ref, h_ref, xr_ref, xi_ref, er_ref, ei_ref, ar_ref, ai_ref, bre_ref, bim_ref,
             cre_ref, cim_ref, d_ref,
             dh_ref, dbre_ref, dbim_ref, dcre_ref, dcim_ref, dar_ref, dai_ref, dd_ref,
             lr, li, car, cai):
        b, cstep = pl.program_id(0), pl.program_id(1)
        ci = nc - 1 - cstep

        @pl.when((b == 0) & (cstep == 0))
        def _():
            for r in (dbre_ref, dbim_ref, dcre_ref, dcim_ref, dar_ref, dai_ref, dd_ref):
                r[...] = jnp.zeros_like(r)

        @pl.when(cstep == 0)
        def _():
            car[...] = jnp.zeros_like(car)
            cai[...] = jnp.zeros_like(cai)

        for j in range(nblk):
            dyj = dy_ref[:, j * cb:(j + 1) * cb].astype(BF16)
            lr[:, j * sb:(j + 1) * sb] = lax.dot_general(dyj, cre_ref[j], dims_nt, preferred_element_type=F32)
            li[:, j * sb:(j + 1) * sb] = -lax.dot_general(dyj, cim_ref[j], dims_nt, preferred_element_type=F32)
        ar, ai = ar_ref[...], ai_ref[...]

        def step(s, carry):
            t = L - 1 - s
            pr, pi = carry
            nr = lr[pl.ds(t, 1), :] + ar * pr + ai * pi
            ni = li[pl.ds(t, 1), :] - ai * pr + ar * pi
            lr[pl.ds(t, 1), :] = nr
            li[pl.ds(t, 1), :] = ni
            return nr, ni

        pr, pi = lax.fori_loop(0, L, step, (car[...], cai[...]), unroll=4)
        car[...] = pr
        cai[...] = pi
        has_prev = (ci > 0).astype(F32)
        first_row = lax.broadcasted_iota(jnp.int32, (L, sb), 0) == 0
        for j in range(nblk):
            cs = slice(j * cb, (j + 1) * cb)
            ss = slice(j * sb, (j + 1) * sb)
            lrj, lij = lr[:, ss], li[:, ss]
            xrj, xij = xr_ref[:, ss], xi_ref[:, ss]
            pr_j = jnp.where(first_row, er_ref[0][:, ss] * has_prev, pltpu.roll(xrj.astype(F32), 1, 0))
            pi_j = jnp.where(first_row, ei_ref[0][:, ss] * has_prev, pltpu.roll(xij.astype(F32), 1, 0))
            dar_ref[:, ss] += jnp.sum(lrj * pr_j + lij * pi_j, axis=0, keepdims=True)
            dai_ref[:, ss] += jnp.sum(lij * pr_j - lrj * pi_j, axis=0, keepdims=True)
            lrb, lib = lrj.astype(BF16), lij.astype(BF16)
            hj = h_ref[:, cs]
            dyf = dy_ref[:, cs]
            dyj = dyf.astype(BF16)
            dbre_ref[j] += lax.dot_general(hj, lrb, dims_tn, preferred_element_type=F32)
            dbim_ref[j] += lax.dot_general(hj, lib, dims_tn, preferred_element_type=F32)
            dcre_ref[j] += lax.dot_general(xrj, dyj, dims_tn, preferred_element_type=F32)
            dcim_ref[j] -= lax.dot_general(xij, dyj, dims_tn, preferred_element_type=F32)
            du = (lax.dot_general(lrb, bre_ref[j], dims_nt, preferred_element_type=F32)
                  + lax.dot_general(lib, bim_ref[j], dims_nt, preferred_element_type=F32))
            dh_ref[:, cs] = du + d_ref[:, cs] * dyf
            dd_ref[:, cs] += jnp.sum(dyf * hj.astype(F32), axis=0, keepdims=True)

    tok = lambda w: pl.BlockSpec((L, w), lambda b, c: (b * nc + nc - 1 - c, 0))
    whole = lambda p: pl.BlockSpec(p.shape, lambda b, c, nd=p.ndim: (0,) * nd)
    prev_end = pl.BlockSpec((1, 1, GN), lambda b, c: (b * nc + jnp.maximum(nc - 2 - c, 0), 0, 0))
    params = (abr, abi, bre, bim, cre, cim, d)
    acc_shapes = [bre.shape, bim.shape, cre.shape, cim.shape, abr.shape, abi.shape, d.shape]
    out = pl.pallas_call(
        body, grid=(B, nc),
        in_specs=[tok(D), tok(D), tok(GN), tok(GN), prev_end, prev_end] + [whole(p) for p in params],
        out_specs=[tok(D)] + [pl.BlockSpec(s, lambda b, c, nd=len(s): (0,) * nd) for s in acc_shapes],
        out_shape=[jax.ShapeDtypeStruct((T, D), F32)] + [jax.ShapeDtypeStruct(s, F32) for s in acc_shapes],
        scratch_shapes=[pltpu.VMEM((L, GN), F32), pltpu.VMEM((L, GN), F32),
                        pltpu.VMEM((1, GN), F32), pltpu.VMEM((1, GN), F32)],
        compiler_params=_cp(("arbitrary", "arbitrary")), name=name,
    )(dy, h, xr, xi, er, ei, *params)
    return out


def _conv_fwd(zp, w, *, R, tc, name):
    B, SP, C = zp.shape
    S = SP - CONV_HALO
    R, tc = min(R, S), min(tc, C)

    def body(z_ref, w_ref, y_ref):
        def chunk(ci, _):
            start = pl.multiple_of(ci * R, 8)
            ze = z_ref[pl.ds(start, R + CONV_HALO), :]
            acc = jnp.zeros((R, tc), F32)
            for m in range(CONV_WIDTH):
                k = CONV_WIDTH - 1 - m
                sh = ze if m == 0 else pltpu.roll(ze, m, 0)
                acc = acc + w_ref[k:k + 1, :] * sh[CONV_HALO:, :]
            y_ref[pl.ds(start, R), :] = acc
            return 0

        lax.fori_loop(0, S // R, chunk, 0)

    return pl.pallas_call(
        body, grid=(B, C // tc),
        in_specs=[pl.BlockSpec((None, SP, tc), lambda b, c: (b, 0, c)),
                  pl.BlockSpec((32, tc), lambda b, c: (0, c))],
        out_specs=pl.BlockSpec((None, S, tc), lambda b, c: (b, 0, c)),
        out_shape=jax.ShapeDtypeStruct((B, S, C), F32),
        compiler_params=_cp(("parallel", "parallel")), name=name,
    )(zp, w)


def _conv_bwd(zp, dyp, w, *, R, tc, name):
    B, SP, C = zp.shape
    S = SP - CONV_HALO
    R, tc = min(R, S), min(tc, C)

    def body(z_ref, dy_ref, w_ref, dz_ref, dw_ref):
        @pl.when(pl.program_id(1) == 0)
        def _():
            dw_ref[...] = jnp.zeros_like(dw_ref)

        def chunk(ci, _):
            start = pl.multiple_of(ci * R, 8)
            ze = z_ref[pl.ds(start, R + CONV_HALO), :]
            de = dy_ref[pl.ds(start, R + CONV_HALO), :]
            dy = de[:R, :]
            acc = jnp.zeros((R, tc), F32)
            for m in range(CONV_WIDTH):
                k = CONV_WIDTH - 1 - m
                zs = ze if m == 0 else pltpu.roll(ze, m, 0)
                ds_ = de if m == 0 else pltpu.roll(de, R + CONV_HALO - m, 0)
                acc = acc + w_ref[k:k + 1, :] * ds_[:R, :]
                dw_ref[k:k + 1, :] += jnp.sum(dy * zs[CONV_HALO:, :], axis=0, keepdims=True)
            dz_ref[pl.ds(start, R), :] = acc
            return 0

        lax.fori_loop(0, S // R, chunk, 0)

    return pl.pallas_call(
        body, grid=(C // tc, B),
        in_specs=[pl.BlockSpec((None, SP, tc), lambda c, b: (b, 0, c)),
                  pl.BlockSpec((None, SP, tc), lambda c, b: (b, 0, c)),
                  pl.BlockSpec((32, tc), lambda c, b: (0, c))],
        out_specs=[pl.BlockSpec((None, S, tc), lambda c, b: (b, 0, c)),
                   pl.BlockSpec((32, tc), lambda c, b: (0, c))],
        out_shape=[jax.ShapeDtypeStruct((B, S, C), F32), jax.ShapeDtypeStruct((32, C), F32)],
        compiler_params=_cp(("parallel", "arbitrary")), name=name,
    )(zp, dyp, w)


def _gmlp_fwd(u, vn, ws, bcol, *, nck, name):
    T, E = u.shape
    H = ws.shape[0]
    he = E // H
    rows = nck * GMLP_CHUNK
    rows = min(rows, T)
    n_in = rows // GMLP_CHUNK

    def body(u_ref, v_ref, ws_ref, b_ref, o_ref):
        for c in range(n_in):
            rs = slice(c * GMLP_CHUNK, (c + 1) * GMLP_CHUNK)
            for hh in range(H):
                cs = slice(hh * he, (hh + 1) * he)
                v2 = jnp.dot(ws_ref[hh], v_ref[rs, cs].astype(BF16), preferred_element_type=F32)
                v2 = v2 + b_ref[:, hh:hh + 1]
                o_ref[rs, cs] = (u_ref[rs, cs] * v2).astype(o_ref.dtype)

    tok = pl.BlockSpec((rows, E), lambda i: (i, 0))
    return pl.pallas_call(
        body, grid=(T // rows,),
        in_specs=[tok, tok, pl.BlockSpec(ws.shape, lambda i: (0, 0, 0)), pl.BlockSpec(bcol.shape, lambda i: (0, 0))],
        out_specs=tok, out_shape=jax.ShapeDtypeStruct((T, E), BF16),
        compiler_params=_cp(("parallel",)), name=name,
    )(u, vn, ws, bcol)


def _gmlp_bwd(duv, u, vn, ws, bcol, *, nck, name):
    T, E = u.shape
    H = ws.shape[0]
    he = E // H
    rows = min(nck * GMLP_CHUNK, T)
    n_in = rows // GMLP_CHUNK
    dims_nt = (((1,), (1,)), ((), ()))
    dims_tn = (((0,), (0,)), ((), ()))

    def body(g_ref, u_ref, v_ref, ws_ref, b_ref, du_ref, dv_ref, dws_ref, db_ref):
        @pl.when(pl.program_id(0) == 0)
        def _():
            dws_ref[...] = jnp.zeros_like(dws_ref)
            db_ref[...] = jnp.zeros_like(db_ref)

        for c in range(n_in):
            rs = slice(c * GMLP_CHUNK, (c + 1) * GMLP_CHUNK)
            for hh in range(H):
                cs = slice(hh * he, (hh + 1) * he)
                vb = v_ref[rs, cs].astype(BF16)
                v2 = jnp.dot(ws_ref[hh], vb, preferred_element_type=F32) + b_ref[:, hh:hh + 1]
                g = g_ref[rs, cs]
                du_ref[rs, cs] = g * v2
                dv2 = g * u_ref[rs, cs]
                dv2b = dv2.astype(BF16)
                dv_ref[rs, cs] = lax.dot_general(ws_ref[hh], dv2b, dims_tn, preferred_element_type=F32)
                dws_ref[hh] += lax.dot_general(dv2b, vb, dims_nt, preferred_element_type=F32)
                db_ref[:, hh:hh + 1] += jnp.sum(dv2, axis=1, keepdims=True)

    tok = pl.BlockSpec((rows, E), lambda i: (i, 0))
    return pl.pallas_call(
        body, grid=(T // rows,),
        in_specs=[tok, tok, tok, pl.BlockSpec(ws.shape, lambda i: (0, 0, 0)), pl.BlockSpec(bcol.shape, lambda i: (0, 0))],
        out_specs=[tok, tok, pl.BlockSpec(ws.shape, lambda i: (0, 0, 0)), pl.BlockSpec(bcol.shape, lambda i: (0, 0))],
        out_shape=[jax.ShapeDtypeStruct((T, E), F32), jax.ShapeDtypeStruct((T, E), F32),
                   jax.ShapeDtypeStruct(ws.shape, F32), jax.ShapeDtypeStruct(bcol.shape, F32)],
        compiler_params=_cp(("arbitrary",)), name=name,
    )(duv, u, vn, ws, bcol)


PAIRS = ATT_HEADS // 2


def _att_consts():
    ji = lax.broadcasted_iota(jnp.int32, (2 * ATT_BLK, ATT_BLK), 0)
    ii = lax.broadcasted_iota(jnp.int32, (2 * ATT_BLK, ATT_BLK), 1)
    dist = ii + ATT_BLK - ji
    band = (dist >= 0) & (dist <= ATT_BLK)
    cur = ji >= ATT_BLK
    first_head = lax.broadcasted_iota(jnp.int32, (ATT_BLK, 2 * HEAD_DIM), 1) < HEAD_DIM
    return band, cur, first_head


def _att_specs(nbk, offs, nsteps, rev):
    rows = nbk * ATT_BLK
    step = (lambda i: nsteps - 1 - i) if rev else (lambda i: i)
    qoff, koff, voff = offs
    blk = lambda off: pl.BlockSpec((rows, 2 * HEAD_DIM), lambda hp, i: (step(i), off + hp))
    prev = lambda off: pl.BlockSpec((ATT_BLK, 2 * HEAD_DIM), lambda hp, i: (jnp.maximum(step(i) * nbk - 1, 0), off + hp))
    out = pl.BlockSpec((rows, 2 * HEAD_DIM), lambda hp, i: (step(i), hp))
    stat = pl.BlockSpec((2, nbk, ATT_BLK), lambda hp, i: (hp, step(i), 0))
    return [blk(qoff), blk(koff), prev(koff), blk(voff), prev(voff)], out, stat


def _att_fwd(arr, offs, *, nb, nbk, name):
    T = arr.shape[0]
    nbk = min(nbk, T // ATT_BLK)
    nsteps = T // (nbk * ATT_BLK)
    scale = HEAD_DIM ** -0.5
    dims_nt = (((1,), (1,)), ((), ()))
    dims_tn = (((0,), (0,)), ((), ()))

    def body(q_ref, k_ref, kp_ref, v_ref, vp_ref, o_ref, lse_ref):
        i = pl.program_id(1)
        band, cur, first_head = _att_consts()
        for jj in range(nbk):
            rs = slice(jj * ATT_BLK, (jj + 1) * ATT_BLK)
            ps = slice((jj - 1) * ATT_BLK, jj * ATT_BLK)
            has_prev = ((i * nbk + jj) & (nb - 1)) != 0
            valid = band & (cur | has_prev)
            kk = jnp.concatenate([kp_ref[...] if jj == 0 else k_ref[ps, :], k_ref[rs, :]], axis=0)
            vv = jnp.concatenate([vp_ref[...] if jj == 0 else v_ref[ps, :], v_ref[rs, :]], axis=0)
            q = q_ref[rs, :]
            outs = []
            for hd in range(2):
                qh = jnp.where(first_head if hd == 0 else ~first_head, q, jnp.zeros_like(q))
                st = lax.dot_general(kk, qh, dims_nt, preferred_element_type=F32) * scale
                st = jnp.where(valid, st, MASK_VALUE)
                m = jnp.max(st, axis=0, keepdims=True)
                p = jnp.exp(st - m)
                l = jnp.sum(p, axis=0, keepdims=True)
                lse_ref[hd, jj:jj + 1, :] = m + jnp.log(l)
                pn = (p / l).astype(BF16)
                outs.append(lax.dot_general(pn, vv, dims_tn, preferred_element_type=F32))
            o_ref[rs, :] = jnp.where(first_head, outs[0], outs[1])

    ins, out, stat = _att_specs(nbk, offs, nsteps, False)
    return pl.pallas_call(
        body, grid=(PAIRS, nsteps), in_specs=ins, out_specs=[out, stat],
        out_shape=[jax.ShapeDtypeStruct((T, ATT_W), F32), jax.ShapeDtypeStruct((ATT_HEADS, T // ATT_BLK, ATT_BLK), F32)],
        compiler_params=_cp(("parallel", "parallel")), name=name,
    )(arr, arr, arr, arr, arr)


def _att_bwd(arr, offs, do, lse, dlse, *, nb, nbk, name):
    T = arr.shape[0]
    nbk = min(nbk, T // ATT_BLK)
    nsteps = T // (nbk * ATT_BLK)
    scale = HEAD_DIM ** -0.5
    dims_nt = (((1,), (1,)), ((), ()))
    dims_tn = (((0,), (0,)), ((), ()))

    def body(q_ref, k_ref, kp_ref, v_ref, vp_ref, do_ref, lse_ref, dlse_ref, dq_ref, dk_ref, dv_ref, ck, cv):
        step = pl.program_id(1)
        i = nsteps - 1 - step
        band, cur, first_head = _att_consts()

        @pl.when(step == 0)
        def _():
            ck[...] = jnp.zeros_like(ck)
            cv[...] = jnp.zeros_like(cv)

        carry_k, carry_v = ck[...], cv[...]
        for jj in reversed(range(nbk)):
            rs = slice(jj * ATT_BLK, (jj + 1) * ATT_BLK)
            ps = slice((jj - 1) * ATT_BLK, jj * ATT_BLK)
            has_prev = ((i * nbk + jj) & (nb - 1)) != 0
            valid = band & (cur | has_prev)
            kk = jnp.concatenate([kp_ref[...] if jj == 0 else k_ref[ps, :], k_ref[rs, :]], axis=0)
            vv = jnp.concatenate([vp_ref[...] if jj == 0 else v_ref[ps, :], v_ref[rs, :]], axis=0)
            q = q_ref[rs, :]
            dob = do_ref[rs, :].astype(BF16)
            dqs, dkk, dvv = [], None, None
            for hd in range(2):
                sel = first_head if hd == 0 else ~first_head
                qh = jnp.where(sel, q, jnp.zeros_like(q))
                doh = jnp.where(sel, dob, jnp.zeros_like(dob))
                st = lax.dot_general(kk, qh, dims_nt, preferred_element_type=F32) * scale
                st = jnp.where(valid, st, MASK_VALUE)
                p = jnp.exp(st - lse_ref[hd, jj:jj + 1, :])
                dp = lax.dot_general(vv, doh, dims_nt, preferred_element_type=F32)
                delta = jnp.sum(p * dp, axis=0, keepdims=True)
                dsb = (p * (dp - delta + dlse_ref[hd, jj:jj + 1, :]) * scale).astype(BF16)
                dqs.append(lax.dot_general(dsb, kk, dims_tn, preferred_element_type=F32))
                dk_h = jnp.dot(dsb, qh, preferred_element_type=F32)
                dv_h = jnp.dot(p.astype(BF16), doh, preferred_element_type=F32)
                dkk = dk_h if dkk is None else dkk + dk_h
                dvv = dv_h if dvv is None else dvv + dv_h
            dq_ref[rs, :] = jnp.where(first_head, dqs[0], dqs[1]).astype(dq_ref.dtype)
            dk_ref[rs, :] = (dkk[ATT_BLK:] + carry_k).astype(dk_ref.dtype)
            dv_ref[rs, :] = (dvv[ATT_BLK:] + carry_v).astype(dv_ref.dtype)
            carry_k, carry_v = dkk[:ATT_BLK], dvv[:ATT_BLK]
        ck[...] = carry_k
        cv[...] = carry_v

    ins, out, stat = _att_specs(nbk, offs, nsteps, True)
    return pl.pallas_call(
        body, grid=(PAIRS, nsteps), in_specs=ins + [out, stat, stat], out_specs=[out] * 3,
        out_shape=[jax.ShapeDtypeStruct((T, ATT_W), BF16)] * 3,
        scratch_shapes=[pltpu.VMEM((ATT_BLK, 2 * HEAD_DIM), F32), pltpu.VMEM((ATT_BLK, 2 * HEAD_DIM), F32)],
        compiler_params=_cp(("arbitrary", "arbitrary")), name=name,
    )(arr, arr, arr, arr, arr, do, lse, dlse)


def _deinterleave(t, B, S, dil):
    if dil == 1:
        return t
    return t.reshape((B, S // dil, dil) + t.shape[1:]).swapaxes(1, 2).reshape(t.shape)


def _interleave(t, B, S, dil):
    if dil == 1:
        return t
    return t.reshape((B, dil, S // dil) + t.shape[1:]).swapaxes(1, 2).reshape(t.shape)


def _stats_to_tokens(lse, B, S, dil):
    return _interleave(lse.reshape(lse.shape[0], -1).T, B, S, dil)


def _stats_from_tokens(dl, B, S, dil):
    return _deinterleave(dl, B, S, dil).T.reshape(dl.shape[1], -1, ATT_BLK)


def _mesh_pos():
    return lax.axis_index("x"), lax.axis_index("y"), lax.axis_index("c")


def _allgather8(xs, *, name):
    m_per, n = xs.shape

    def body(x_ref, out_ref, send_sems, recv_sems, local_sem):
        x, y, c = _mesh_pos()
        me, sibling = (x, y, c), (x, y, 1 - c)
        chips = [(1 - x, y), (x, 1 - y), (1 - x, 1 - y)]

        def rows(px, py, pc):
            return out_ref.at[pl.ds((4 * px + 2 * py + pc) * m_per, m_per), :]

        def copy(k, block, to, src=None):
            return pltpu.make_async_remote_copy(
                src_ref=rows(*block) if src is None else src, dst_ref=rows(*block),
                send_sem=send_sems.at[k], recv_sem=recv_sems.at[k], device_id=to, device_id_type=MESH)

        mine = pltpu.make_async_copy(x_ref, rows(*me), local_sem)
        mine.start()
        first = [copy(0, me, sibling, src=x_ref)]
        first += [copy(1 + j, me, (*chip, c), src=x_ref) for j, chip in enumerate(chips)]
        for cp in first:
            cp.start()
        passed = [copy(4 + j, (*chip, c), sibling) for j, chip in enumerate(chips)]
        for j, chip in enumerate(chips):
            copy(1 + j, (*chip, c), me).wait_recv()
            passed[j].start()
        copy(0, sibling, me).wait_recv()
        for j, chip in enumerate(chips):
            copy(4 + j, (*chip, 1 - c), me).wait_recv()
        for cp in first + passed:
            cp.wait_send()
        mine.wait()

    return pl.pallas_call(
        body, out_shape=jax.ShapeDtypeStruct((8 * m_per, n), xs.dtype),
        in_specs=[pl.BlockSpec(memory_space=pltpu.VMEM)], out_specs=pl.BlockSpec(memory_space=pltpu.VMEM),
        scratch_shapes=[pltpu.SemaphoreType.DMA((7,)), pltpu.SemaphoreType.DMA((7,)), pltpu.SemaphoreType.DMA],
        compiler_params=pltpu.CompilerParams(vmem_limit_bytes=VMEM_LIMIT), name=name,
    )(xs)


def _hbm_call(body, arrays, out_shapes, n_sems, *, name):
    any_spec = pl.BlockSpec(memory_space=pl.ANY)
    return pl.pallas_call(
        body, out_shape=out_shapes, in_specs=[any_spec] * len(arrays), out_specs=[any_spec] * len(out_shapes),
        scratch_shapes=[pltpu.SemaphoreType.DMA((n_sems,)), pltpu.SemaphoreType.DMA((n_sems,))], name=name,
    )(*arrays)


def _other_chips(x, y):
    return [(1 - x, y), (x, 1 - y), (1 - x, 1 - y)]


def _allgather_chips(ws, *, name):
    n = len(ws)

    def body(*refs):
        ins, outs, (send_sems, recv_sems) = refs[:n], refs[n:2 * n], refs[2 * n:]
        x, y, c = _mesh_pos()
        chips = _other_chips(x, y)

        def copy(a, k, px, py, half, to, src=None):
            slot = outs[a].at[2 * px + py, half]
            return pltpu.make_async_remote_copy(
                src_ref=slot if src is None else src, dst_ref=slot,
                send_sem=send_sems.at[6 * a + k], recv_sem=recv_sems.at[6 * a + k], device_id=to, device_id_type=MESH)

        first = [copy(a, j, x, y, c, (*chip, c), src=ins[a].at[c]) for a in range(n) for j, chip in enumerate(chips)]
        for cp in first:
            cp.start()
        passed = []
        for j, chip in enumerate(chips):
            for a in range(n):
                copy(a, j, *chip, c, (x, y, c)).wait_recv()
                passed.append(copy(a, 3 + j, *chip, c, (x, y, 1 - c)))
                passed[-1].start()
        for j, chip in enumerate(chips):
            for a in range(n):
                copy(a, 3 + j, *chip, 1 - c, (x, y, c)).wait_recv()
        for cp in first + passed:
            cp.wait_send()

    return _hbm_call(body, ws, [jax.ShapeDtypeStruct((N_CHIPS,) + w.shape, w.dtype) for w in ws], 6 * n, name=name)


def _swap_halves(gs, *, name):
    n = len(gs)

    def body(*refs):
        ins, outs, (send_sems, recv_sems) = refs[:n], refs[n:2 * n], refs[2 * n:]
        x, y, c = _mesh_pos()
        cps = [pltpu.make_async_remote_copy(
            src_ref=ins[a].at[1 - c], dst_ref=outs[a], send_sem=send_sems.at[a], recv_sem=recv_sems.at[a],
            device_id=(x, y, 1 - c), device_id_type=MESH) for a in range(n)]
        for cp in cps:
            cp.start()
        for cp in cps:
            cp.wait()

    return _hbm_call(body, gs, [jax.ShapeDtypeStruct(g.shape[1:], g.dtype) for g in gs], n, name=name)


def _scatter_chips(ss, *, name):
    n = len(ss)

    def body(*refs):
        ins, outs, (send_sems, recv_sems) = refs[:n], refs[n:2 * n], refs[2 * n:]
        x, y, c = _mesh_pos()
        me = 2 * x + y
        chips = _other_chips(x, y)

        def copy(a, j, px, py):
            return pltpu.make_async_remote_copy(
                src_ref=ins[a].at[2 * px + py], dst_ref=outs[a].at[me],
                send_sem=send_sems.at[3 * a + j], recv_sem=recv_sems.at[3 * a + j],
                device_id=(px, py, c), device_id_type=MESH)

        def arrival(a, j, px, py):
            return pltpu.make_async_remote_copy(
                src_ref=ins[a].at[me], dst_ref=outs[a].at[2 * px + py],
                send_sem=send_sems.at[3 * a + j], recv_sem=recv_sems.at[3 * a + j],
                device_id=(px, py, c), device_id_type=MESH)

        cps = [copy(a, j, *chip) for a in range(n) for j, chip in enumerate(chips)]
        for cp in cps:
            cp.start()
        for a in range(n):
            for j, chip in enumerate(chips):
                arrival(a, j, *chip).wait_recv()
        for cp in cps:
            cp.wait_send()

    return _hbm_call(body, ss, [jax.ShapeDtypeStruct(s.shape, s.dtype) for s in ss], 3 * n, name=name)


def _share_halves(ts, *, name):
    n = len(ts)

    def body(*refs):
        ins, outs, (send_sems, recv_sems) = refs[:n], refs[n:2 * n], refs[2 * n:]
        x, y, c = _mesh_pos()
        cps = [pltpu.make_async_remote_copy(
            src_ref=ins[a], dst_ref=outs[a].at[c], send_sem=send_sems.at[a], recv_sem=recv_sems.at[a],
            device_id=(x, y, 1 - c), device_id_type=MESH) for a in range(n)]
        for cp in cps:
            cp.start()
        for a in range(n):
            pltpu.make_async_remote_copy(
                src_ref=ins[a], dst_ref=outs[a].at[1 - c], send_sem=send_sems.at[a], recv_sem=recv_sems.at[a],
                device_id=(x, y, 1 - c), device_id_type=MESH).wait_recv()
        for cp in cps:
            cp.wait_send()

    return _hbm_call(body, ts, [jax.ShapeDtypeStruct((2,) + t.shape, t.dtype) for t in ts], n, name=name)


def _half_add(g, ra, core, *, name):
    _, R, C = g.shape
    tr = _pick_rows(R, max(8, 2 * 1024 * 1024 // (4 * C)))

    def body(core_ref, g_ref, ra_ref, o_ref):
        o_ref[...] = (g_ref[...] + ra_ref[...]).astype(o_ref.dtype)

    return pl.pallas_call(
        body, out_shape=jax.ShapeDtypeStruct((R, C), BF16),
        grid_spec=pltpu.PrefetchScalarGridSpec(
            num_scalar_prefetch=1, grid=(R // tr,),
            in_specs=[pl.BlockSpec((None, tr, C), lambda i, cr: (cr[0], i, 0)),
                      pl.BlockSpec((tr, C), lambda i, cr: (i, 0))],
            out_specs=pl.BlockSpec((tr, C), lambda i, cr: (i, 0))),
        compiler_params=_cp(("parallel",)), name=name,
    )(core.reshape(1).astype(jnp.int32), g, ra)


def _sum4(rb, *, name):
    _, R, C = rb.shape
    tr = _pick_rows(R, max(8, 2 * 1024 * 1024 // (4 * C)))

    def body(r0, r1, r2, r3, o_ref):
        f = lambda r: r[...].astype(F32)
        o_ref[...] = ((f(r0) + f(r1)) + f(r2)) + f(r3)

    return pl.pallas_call(
        body, out_shape=jax.ShapeDtypeStruct((R, C), F32), grid=(R // tr,),
        in_specs=[pl.BlockSpec((None, tr, C), lambda i, k=k: (k, i, 0)) for k in range(N_CHIPS)],
        out_specs=pl.BlockSpec((tr, C), lambda i: (i, 0)),
        compiler_params=_cp(("parallel",)), name=name,
    )(rb, rb, rb, rb)


TR = 256
S5_CHUNK = 256


def _rms_fwd(x, g, name):
    return _rowwise(_f_rms, [x], [g], (BF16,), tr=TR, name=name)[0]


def _rms_bwd(x, g, dh, gx, name):
    (dx, dxb), (dg,) = _rowwise_vjp(_f_rms, [x], [g], [dh], [(F32, BF16)], adds={0: gx}, tr=TR, name=name)
    return dx, dxb, dg


def _grad_cols(M, Nq):
    def imap(tm, tn):
        hp, per = (M // 2) // tm, Nq // tn
        assert hp * tm * 2 == M and per * tn == Nq, (M, Nq, tm, tn)
        return lambda i, j, k: (i // hp, j // per, i % hp, j % per)
    return (2, N_CHIPS, M // 2, Nq), lambda tm, tn: (None, None, tm, tn), imap, None, M // 2, Nq


def _grad_rows(Mq, N):
    def imap(tm, tn):
        po, hp = Mq // tm, (Mq // 2) // tm
        assert hp * tm * 2 == Mq, (Mq, tm)
        return lambda i, j, k: ((i % po) // hp, i // po, (i % po) % hp, j)
    return (2, N_CHIPS, Mq // 2, N), lambda tm, tn: (None, None, tm, tn), imap, None, Mq // 2, N


def _grad_layer_cols(li, nl, M, Nq, buf):
    lh = nl // 2

    def imap(tm, tn):
        per = Nq // tn
        return lambda i, j, k: (li // lh, j // per, li % lh, i, j % per)
    return (2, N_CHIPS, lh, M, Nq), lambda tm, tn: (None, None, None, tm, tn), imap, buf, M, Nq


def _grad_layer_rows(li, nl, Mq, N, buf):
    lh = nl // 2

    def imap(tm, tn):
        po = Mq // tm
        return lambda i, j, k: (li // lh, i // po, li % lh, i % po, j)
    return (2, N_CHIPS, lh, Mq, N), lambda tm, tn: (None, None, None, tm, tn), imap, buf, Mq, N


def _mlp_fwd(x, g, w_in, w_out, li):
    h2 = _rms_fwd(x, g, f"mlp_rms_{li}")
    r = _mm(h2, w_in, out_dtypes=(BF16,), epi=lambda acc: (jnp.maximum(acc, 0.0),), name=f"mlp_in_{li}")
    x_out = _mm(r, w_out, pro_a=lambda t: t * t, epi=lambda acc, res: (acc + res,), extras=(x,),
                name=f"mlp_out_{li}")
    return x_out, (h2, r)


def _mlp_bwd(gx, gxb, x, g, w_in, w_out, saved, li, nl, bufs):
    h2, r = saved
    D, F = w_in.shape
    da = _mm(gxb, w_out, tb=True, out_dtypes=(BF16,),
             epi=lambda acc, rt: (acc * 2.0 * rt.astype(F32),), extras=(r,), name=f"mlp_dact_{li}")
    buf_in, buf_out = bufs if bufs is not None else (None, None)
    d_w_out = _mm(r, gxb, ta=True, pro_a=lambda t: t * t, tm=512, tn=1024, tk=2048,
                  out=_grad_layer_rows(li, nl, F // N_CHIPS, D, buf_out), name=f"mlp_dwout_{li}")
    d_w_in = _mm(h2, da, ta=True, tm=1024, tn=1024, tk=2048,
                 out=_grad_layer_cols(li, nl, D, F // N_CHIPS, buf_in), name=f"mlp_dwin_{li}")
    dh2 = _mm(da, w_in, tb=True, name=f"mlp_dh_{li}")
    gx_mid, gxb_mid, dg = _rms_bwd(x, g, dh2, gx, f"mlp_rms_bwd_{li}")
    return gx_mid, gxb_mid, dg, (d_w_in, d_w_out)


def _local_step(x3, tgt3, p):
    B, S, D = x3.shape
    T = B * S
    x = x3.reshape(T, D)
    grads = {}
    row = lambda v: v.reshape(1, -1)

    xs = [x]
    g0 = row(p["norm_mix"][0])
    h0 = _rms_fwd(x, g0, "rms_mix_0")
    s5_args = (p["ssm_a_re"][0], p["ssm_a_im"][0], p["ssm_b_re"][0], p["ssm_b_im"][0],
               p["ssm_c_re"][0], p["ssm_c_im"][0], p["ssm_log_dt"][0])
    s5_exp, s5_vjp = jax.vjp(_s5_prep, *s5_args)
    abr, abi, bre, bim, cre, cim = s5_exp
    bre_b, bim_b, cre_b, cim_b = (t.astype(BF16) for t in (bre, bim, cre, cim))
    d_skip = p["ssm_d"]
    ypre, yb, sxr, sxi, ser, sei = _s5_fwd(h0, abr, abi, bre_b, bim_b, cre_b, cim_b, d_skip, B=B, L=S5_CHUNK,
                                           name="s5_fwd")
    w_glu = p["ssm_w_glu"]
    z0 = _mm(yb, w_glu, name="s5_glu_mm")
    x_mid0 = _rowwise(lambda z, xr: (_f_glu(z)[0] + xr,), [z0, x], [], (F32,), tr=TR, name="s5_glu")[0]
    nl = len(p["mlp_w_in"])
    x1, mlp_saved0 = _mlp_fwd(x_mid0, row(p["norm_mlp"][0]), p["mlp_w_in"][0], p["mlp_w_out"][0], 0)

    g1 = row(p["norm_mix"][1])
    h1 = _rms_fwd(x1, g1, "rms_mix_1")
    z1 = _mm(h1, p["conv_w_pw1"], name="conv_pw1")
    zg = _rowwise(_f_bias_glu, [z1], [p["conv_b_pw1"]], (F32,), tr=TR, name="conv_glu")[0]
    zp = jnp.pad(zg.reshape(B, S, D), ((0, 0), (CONV_HALO, 0), (0, 0)))
    w_dw = jnp.pad(p["conv_w_dw"], ((0, 32 - CONV_WIDTH), (0, 0)))
    yc = _conv_fwd(zp, w_dw, R=256, tc=128, name="conv_dw").reshape(T, D)
    ln_par = [p["conv_b_dw"], p["conv_ln_g"], p["conv_ln_b"]]
    qc = _rowwise(_f_ln_silu, [yc], ln_par, (BF16,), tr=TR, name="conv_ln_silu")[0]
    x_mid1 = _mm(qc, p["conv_w_pw2"], epi=lambda acc, bias, res: (acc + bias + res,),
                 extras=(p["conv_b_pw2"], x1), name="conv_pw2")
    x2, mlp_saved1 = _mlp_fwd(x_mid1, row(p["norm_mlp"][1]), p["mlp_w_in"][1], p["mlp_w_out"][1], 1)

    g2 = row(p["norm_mix"][2])
    h2 = _rms_fwd(x2, g2, "rms_mix_2")
    z2 = _mm(h2, p["gmlp_w_in"], name="gmlp_in")
    gl_par = [p["gmlp_ln_g"], p["gmlp_ln_b"]]
    gu, gvn = _rowwise(_f_gelu_ln, [z2], gl_par, (F32, F32), tr=TR, name="gmlp_gelu_ln")
    causal = jnp.tril(jnp.ones((GMLP_CHUNK, GMLP_CHUNK), dtype=bool))
    ws_b = jnp.where(causal[None], p["gmlp_w_s"][0], 0.0).astype(BF16)
    bcol = jnp.pad(p["gmlp_b_s"][0].T, ((0, 0), (0, 128 - GMLP_HEADS)))
    uv = _gmlp_fwd(gu, gvn, ws_b, bcol, nck=4, name="gmlp_spatial")
    x_mid2 = _mm(uv, p["gmlp_w_out"], epi=lambda acc, res: (acc + res,), extras=(x2,), name="gmlp_out")
    x3_, mlp_saved2 = _mlp_fwd(x_mid2, row(p["norm_mlp"][2]), p["mlp_w_in"][2], p["mlp_w_out"][2], 2)

    g3 = row(p["norm_mix"][3])
    h3 = _rms_fwd(x3_, g3, "rms_mix_3")
    qkv = _mm(h3, p["attn_w_qkv"], out_dtypes=(BF16,), tn=1152, name="attn_qkv")
    ng = len(ATT_DILS)
    qkv4 = qkv.reshape(T, 3, ng, ATT_W)
    att_in, o_tok, l_tok, lses = [], [], [], []
    for gi, dil in enumerate(ATT_DILS):
        if dil == 1:
            arr, offs = qkv, tuple((i * ng + gi) * PAIRS for i in range(3))
        else:
            arr, offs = _deinterleave(qkv4[:, :, gi].reshape(T, 3 * ATT_W), B, S, dil), (0, PAIRS, 2 * PAIRS)
        att_in.append((arr, offs))
        og, lg = _att_fwd(arr, offs, nb=S // dil // ATT_BLK, nbk=8, name=f"attn_fwd_{gi}")
        lses.append(lg)
        o_tok.append(_interleave(og, B, S, dil))
        l_tok.append(_stats_to_tokens(lg, B, S, dil))
    merged2 = _rowwise(_f_merge, o_tok + l_tok, [], (BF16,), tr=TR, name="attn_merge")[0]
    x_mid3 = _mm(merged2, p["attn_w_o"], epi=lambda acc, res: (acc + res,), extras=(x3_,), name="attn_out")
    x4, mlp_saved3 = _mlp_fwd(x_mid3, row(p["norm_mlp"][3]), p["mlp_w_in"][3], p["mlp_w_out"][3], 3)

    loss_part, gx, gxb, dgf = _loss_head(x4, tgt3.reshape(T, D), row(p["norm_final"]), tr=TR, name="loss_head")
    grads["norm_final"] = dgf.reshape(-1)
    d_norm_mix, d_norm_mlp = [None] * 4, [None] * 4
    Dq = D // N_CHIPS

    gx, gxb, d_norm_mlp[3], mlp_bufs = _mlp_bwd(
        gx, gxb, x_mid3, row(p["norm_mlp"][3]), p["mlp_w_in"][3], p["mlp_w_out"][3], mlp_saved3, 3, nl, None)
    dmerged = _mm(gxb, p["attn_w_o"], tb=True, name="attn_dmerged")
    grads["attn_w_o"] = _mm(merged2, gxb, ta=True, tm=256, tn=256, tk=2048, out=_grad_cols(ATT_W, Dq), name="attn_dwo")
    dml, _ = _rowwise_vjp(_f_merge, o_tok + l_tok, [], [dmerged], [F32] * 6, tr=TR, name="attn_merge_bwd")
    pieces = [[None] * ng for _ in range(3)]
    for gi, dil in enumerate(ATT_DILS):
        arr, offs = att_in[gi]
        dqkv_g = _att_bwd(arr, offs, _deinterleave(dml[gi], B, S, dil), lses[gi],
                          _stats_from_tokens(dml[ng + gi], B, S, dil), nb=S // dil // ATT_BLK, nbk=8,
                          name=f"attn_bwd_{gi}")
        for i in range(3):
            pieces[i][gi] = _interleave(dqkv_g[i], B, S, dil)
    dqkv = jnp.concatenate([pieces[i][gi] for i in range(3) for gi in range(ng)], axis=1)
    qkv_w = 3 * ng * ATT_W
    grads["attn_w_qkv"] = _mm(h3, dqkv, ta=True, tm=512, tn=1152, tk=2048, out=_grad_cols(D, qkv_w // N_CHIPS),
                              name="attn_dwqkv")
    dh3 = _mm(dqkv, p["attn_w_qkv"], tb=True, tk=1152, name="attn_dh")
    gx, gxb, d_norm_mix[3] = _rms_bwd(x3_, g3, dh3, gx, "rms_mix_bwd_3")

    gx, gxb, d_norm_mlp[2], mlp_bufs = _mlp_bwd(
        gx, gxb, x_mid2, row(p["norm_mlp"][2]), p["mlp_w_in"][2], p["mlp_w_out"][2], mlp_saved2, 2, nl, mlp_bufs)
    duv = _mm(gxb, p["gmlp_w_out"], tb=True, name="gmlp_duv")
    grads["gmlp_w_out"] = _mm(uv, gxb, ta=True, tm=128, tn=1024, tk=2048, out=_grad_rows(Dq, D), name="gmlp_dwout")
    du, dvn, dws, dbcol = _gmlp_bwd(duv, gu, gvn, ws_b, bcol, nck=4, name="gmlp_spatial_bwd")
    grads["gmlp_w_s"] = jnp.where(causal[None], dws, 0.0)[None]
    grads["gmlp_b_s"] = dbcol[:, :GMLP_HEADS].T[None]
    (dz2,), (dlg, dlb_) = _rowwise_vjp(_f_gelu_ln, [z2], gl_par, [du, dvn], [BF16], tr=TR, name="gmlp_gelu_ln_bwd")
    grads["gmlp_ln_g"], grads["gmlp_ln_b"] = dlg, dlb_
    grads["gmlp_w_in"] = _mm(h2, dz2, ta=True, tm=512, tn=512, tk=2048, out=_grad_cols(D, 2 * Dq), name="gmlp_dwin")
    dh2 = _mm(dz2, p["gmlp_w_in"], tb=True, name="gmlp_dh")
    gx, gxb, d_norm_mix[2] = _rms_bwd(x2, g2, dh2, gx, "rms_mix_bwd_2")

    gx, gxb, d_norm_mlp[1], mlp_bufs = _mlp_bwd(
        gx, gxb, x_mid1, row(p["norm_mlp"][1]), p["mlp_w_in"][1], p["mlp_w_out"][1], mlp_saved1, 1, nl, mlp_bufs)
    dqc = _mm(gxb, p["conv_w_pw2"], tb=True, name="conv_dq")
    grads["conv_w_pw2"] = _mm(qc, gxb, ta=True, tm=128, tn=1024, tk=2048, out=_grad_rows(Dq, D), name="conv_dwpw2")
    _, (db2,) = _rowwise_vjp(lambda t, b: (t + b,), [gx], [p["conv_b_pw2"]], [gx], [None], tr=TR, name="conv_db2")
    grads["conv_b_pw2"] = db2
    (dyc,), (dbdw, dcg, dcb) = _rowwise_vjp(_f_ln_silu, [yc], ln_par, [dqc], [F32], tr=TR, name="conv_ln_silu_bwd")
    grads["conv_b_dw"], grads["conv_ln_g"], grads["conv_ln_b"] = dbdw, dcg, dcb
    dyp = jnp.pad(dyc.reshape(B, S, D), ((0, 0), (0, CONV_HALO), (0, 0)))
    dzg, dwdw = _conv_bwd(zp, dyp, w_dw, R=256, tc=128, name="conv_dw_bwd")
    grads["conv_w_dw"] = dwdw[:CONV_WIDTH][None]
    (dz1,), (db1,) = _rowwise_vjp(_f_bias_glu, [z1], [p["conv_b_pw1"]], [dzg.reshape(T, D)], [BF16], tr=TR,
                                  name="conv_glu_bwd")
    grads["conv_b_pw1"] = db1
    grads["conv_w_pw1"] = _mm(h1, dz1, ta=True, tm=512, tn=512, tk=2048, out=_grad_cols(D, 2 * Dq), name="conv_dwpw1")
    dh1 = _mm(dz1, p["conv_w_pw1"], tb=True, name="conv_dh")
    gx, gxb, d_norm_mix[1] = _rms_bwd(x1, g1, dh1, gx, "rms_mix_bwd_1")

    gx, gxb, d_norm_mlp[0], mlp_bufs = _mlp_bwd(
        gx, gxb, x_mid0, row(p["norm_mlp"][0]), p["mlp_w_in"][0], p["mlp_w_out"][0], mlp_saved0, 0, nl, mlp_bufs)
    (dz0,), _ = _rowwise_vjp(_f_glu, [z0], [], [gx], [BF16], tr=TR, name="s5_glu_bwd")
    grads["ssm_w_glu"] = _mm(yb, dz0, ta=True, tm=512, tn=512, tk=2048, out=_grad_cols(D, 2 * Dq), name="s5_dwglu")
    dypre = _mm(dz0, w_glu, tb=True, epi=lambda acc, yp: (jax.vjp(lambda t: jax.nn.gelu(t), yp)[1](acc)[0],),
                extras=(ypre,), name="s5_dypre")
    dh0, dbre, dbim, dcre, dcim, dabr, dabi, dd = _s5_bwd(
        dypre, h0, sxr, sxi, ser, sei, abr, abi, bre_b, bim_b, cre_b, cim_b, d_skip, B=B, L=S5_CHUNK, name="s5_bwd")
    s5_grads = s5_vjp((dabr, dabi, dbre, dbim, dcre, dcim))
    for nm, gv in zip(("ssm_a_re", "ssm_a_im", "ssm_b_re", "ssm_b_im", "ssm_c_re", "ssm_c_im", "ssm_log_dt"), s5_grads):
        grads[nm] = gv[None]
    grads["ssm_d"] = dd
    gx, _, d_norm_mix[0] = _rms_bwd(x, g0, dh0, gx, "rms_mix_bwd_0")

    grads["norm_mix"] = jnp.concatenate(d_norm_mix, axis=0)
    grads["norm_mlp"] = jnp.concatenate(d_norm_mlp, axis=0)
    grads["mlp_w_in"], grads["mlp_w_out"] = mlp_bufs
    return loss_part, gx.reshape(B, S, D), grads


WEIGHTS = ['norm_mix', 'norm_mlp', 'norm_final', 'ssm_a_re', 'ssm_a_im', 'ssm_b_re', 'ssm_b_im', 'ssm_c_re',
           'ssm_c_im', 'ssm_d', 'ssm_log_dt', 'ssm_w_glu', 'conv_w_pw1', 'conv_b_pw1', 'conv_w_dw', 'conv_b_dw',
           'conv_ln_g', 'conv_ln_b', 'conv_w_pw2', 'conv_b_pw2', 'gmlp_w_in', 'gmlp_ln_g', 'gmlp_ln_b', 'gmlp_w_s',
           'gmlp_b_s', 'gmlp_w_out', 'attn_w_qkv', 'attn_w_o', 'mlp_w_in', 'mlp_w_out']
BIG_AXIS = {'ssm_w_glu': -1, 'conv_w_pw1': -1, 'conv_w_pw2': -2, 'gmlp_w_in': -1, 'gmlp_w_out': -2,
            'attn_w_qkv': -1, 'attn_w_o': -1, 'mlp_w_in': -1, 'mlp_w_out': -2}
BIG = list(BIG_AXIS)
SMALL_SHARDED = ['conv_b_pw1', 'conv_w_dw', 'conv_b_dw', 'conv_ln_g', 'conv_ln_b', 'conv_b_pw2', 'gmlp_ln_g', 'gmlp_ln_b']
SMALL_REPL = [n for n in WEIGHTS if n not in BIG_AXIS and n not in SMALL_SHARDED]
SMALL = SMALL_REPL + SMALL_SHARDED
LANES = 128
FLAT_COLS = 1024


def _pack(arrs, cols, row_mult):
    flat = jnp.concatenate([a.reshape(-1) for a in arrs])
    per = cols * row_mult
    n = -(-flat.shape[0] // per) * per
    return jnp.pad(flat, (0, n - flat.shape[0])).reshape(n // cols, cols)


def _unpack(flat2d, shapes):
    flat = flat2d.reshape(-1)
    out, off = [], 0
    for s in shapes:
        n = int(np.prod(s))
        out.append(flat[off:off + n].reshape(s))
        off += n
    return out


def _as_halves(shard):
    if shard.shape[0] == 1:
        shard = shard[0]
    return shard.reshape((2, shard.shape[0] // 2) + shard.shape[1:])


def _stored_weight(name, arr):
    kind = "cols" if BIG_AXIS[name] == -1 else "rows"
    if arr.shape[1] > 1:
        return [_Stored(arr, kind, lead=(li,)) for li in range(arr.shape[1])]
    arr = arr[:, 0]
    if kind == "rows":
        return arr.reshape(-1, arr.shape[-1])
    return _Stored(arr, kind)


def kernel(x, norm_mix, norm_mlp, norm_final, ssm_a_re, ssm_a_im, ssm_b_re, ssm_b_im, ssm_c_re, ssm_c_im, ssm_d, ssm_log_dt, ssm_w_glu, conv_w_pw1, conv_b_pw1, conv_w_dw, conv_b_dw, conv_ln_g, conv_ln_b, conv_w_pw2, conv_b_pw2, gmlp_w_in, gmlp_ln_g, gmlp_ln_b, gmlp_w_s, gmlp_b_s, gmlp_w_out, attn_w_qkv, attn_w_o, mlp_w_in, mlp_w_out, loss_target, m_norm_mix, m_norm_mlp, m_norm_final, m_ssm_a_re, m_ssm_a_im, m_ssm_b_re, m_ssm_b_im, m_ssm_c_re, m_ssm_c_im, m_ssm_d, m_ssm_log_dt, m_ssm_w_glu, m_conv_w_pw1, m_conv_b_pw1, m_conv_w_dw, m_conv_b_dw, m_conv_ln_g, m_conv_ln_b, m_conv_w_pw2, m_conv_b_pw2, m_gmlp_w_in, m_gmlp_ln_g, m_gmlp_ln_b, m_gmlp_w_s, m_gmlp_b_s, m_gmlp_w_out, m_attn_w_qkv, m_attn_w_o, m_mlp_w_in, m_mlp_w_out, v_norm_mix, v_norm_mlp, v_norm_final, v_ssm_a_re, v_ssm_a_im, v_ssm_b_re, v_ssm_b_im, v_ssm_c_re, v_ssm_c_im, v_ssm_d, v_ssm_log_dt, v_ssm_w_glu, v_conv_w_pw1, v_conv_b_pw1, v_conv_w_dw, v_conv_b_dw, v_conv_ln_g, v_conv_ln_b, v_conv_w_pw2, v_conv_b_pw2, v_gmlp_w_in, v_gmlp_ln_g, v_gmlp_ln_b, v_gmlp_w_s, v_gmlp_b_s, v_gmlp_w_out, v_attn_w_qkv, v_attn_w_o, v_mlp_w_in, v_mlp_w_out):
    args = dict(locals())
    w = {n: args[n] for n in WEIGHTS}
    m = {n: args["m_" + n] for n in WEIGHTS}
    v = {n: args["v_" + n] for n in WEIGHTS}
    chip = 2 * lax.axis_index("x") + lax.axis_index("y")
    core = lax.axis_index("c")

    big_shapes = [w[n].shape for n in BIG]
    halves = {n: _as_halves(w[n].astype(BF16)) for n in BIG}
    gathered_w = _allgather_chips([halves[n] for n in BIG], name="gather_weights")
    p = {}
    for n, arr in zip(BIG, gathered_w):
        arr = lax.dynamic_update_index_in_dim(arr, halves[n], chip, axis=0)
        p[n] = _stored_weight(n, arr.reshape((N_CHIPS,) + w[n].shape))
    sm_shapes = [w[n].shape for n in SMALL_SHARDED]
    sflat = _pack([w[n] for n in SMALL_SHARDED], LANES, 8)
    rs = sflat.shape[0]
    sall = _allgather8(sflat, name="gather_small").reshape(8, rs, LANES)
    per_chip = [_unpack(sall[2 * k], sm_shapes) for k in range(N_CHIPS)]
    for i, n in enumerate(SMALL_SHARDED):
        p[n] = jnp.concatenate([per_chip[k][i] for k in range(N_CHIPS)], axis=-1)
    for n in SMALL_REPL:
        p[n] = w[n]
    p['conv_w_dw'] = p['conv_w_dw'][0]

    loss_part, grad_x, g = _local_step(x, loss_target, p)
    loss = lax.psum(loss_part[0, 0], ("x", "y", "c"))

    g3 = [g[n].reshape(2, N_CHIPS, -1, g[n].shape[-1]) for n in BIG]
    from_sibling = _swap_halves(g3, name="grads_swap_halves")
    chip_sums = []
    for n, own, sib in zip(BIG, g3, from_sibling):
        _, _, r, cols = own.shape
        s = _half_add(own.reshape(2, N_CHIPS * r, cols), sib.reshape(N_CHIPS * r, cols), core, name="chip_sum_" + n)
        chip_sums.append(s.reshape(N_CHIPS, r, cols))
    arrived = _scatter_chips(chip_sums, name="grads_to_owner")
    totals = []
    for n, arr, s in zip(BIG, arrived, chip_sums):
        mine = lax.dynamic_index_in_dim(s, chip, axis=0, keepdims=True)
        arr = lax.dynamic_update_index_in_dim(arr, mine, chip, axis=0)
        totals.append(_sum4(arr, name="owner_sum_" + n))
    shared = _share_halves(totals, name="grads_share_halves")
    big_grads = {}
    for n, arr, t in zip(BIG, shared, totals):
        arr = lax.dynamic_update_index_in_dim(arr, t[None], core, axis=0)
        big_grads[n] = arr.reshape(w[n].shape)

    small_full_shapes = [g[n].shape for n in SMALL]
    gs = _pack([g[n] for n in SMALL], LANES, 8)
    rg = gs.shape[0]
    gs_all = _allgather8(gs, name="gather_small_grads").reshape(8, rg, LANES)
    gs_sum = _rowwise(lambda *a: (functools.reduce(lambda s, t: s + t, a),), [gs_all[k] for k in range(8)], [], (F32,),
                      tr=rg, name="small_grads_sum")[0]
    small_grads = dict(zip(SMALL, _unpack(gs_sum, small_full_shapes)))
    for n in SMALL:
        small_grads[n] = small_grads[n].reshape(p_shape_full(w[n], -1 if n in SMALL_SHARDED else None))
    for n in SMALL_SHARDED:
        width = w[n].shape[-1]
        small_grads[n] = lax.dynamic_slice_in_dim(small_grads[n], chip * width, width, axis=-1)

    grad, delta, new_m, new_v = {}, {}, {}, {}
    for n in BIG:
        shape = w[n].shape
        two_d = lambda t: t.reshape(-1, shape[-1])
        grad[n] = big_grads[n]
        d_, m_, v_ = _adamw(two_d(w[n]), two_d(grad[n]), two_d(m[n]), two_d(v[n]), name="adamw_" + n)
        delta[n], new_m[n], new_v[n] = d_.reshape(shape), m_.reshape(shape), v_.reshape(shape)
    sm_own_shapes = [w[n].shape for n in SMALL]
    packed = [_pack([src[n] for n in SMALL], LANES, 8) for src in (w, small_grads, m, v)]
    outs = _adamw(*packed, name="adamw_small")
    for dst, flat in zip((delta, new_m, new_v), outs):
        dst.update(dict(zip(SMALL, _unpack(flat, sm_own_shapes))))
    for n in SMALL:
        grad[n] = small_grads[n]

    return (loss, grad_x, *[grad[n] for n in WEIGHTS], *[delta[n] for n in WEIGHTS],
            *[new_m[n] for n in WEIGHTS], *[new_v[n] for n in WEIGHTS])


def p_shape_full(shard, axis):
    s = list(shard.shape)
    if axis is not None:
        s[axis] *= N_CHIPS
    return tuple(s)
```

```python
import functools
import math

import jax
import jax.numpy as jnp
import numpy as np
from jax import lax
from jax.experimental import pallas as pl
from jax.experimental.pallas import tpu as pltpu

F32 = jnp.float32
BF16 = jnp.bfloat16
MESH = pl.DeviceIdType.MESH

EPS = 1e-6
SSM_GROUP = 16
SSM_STATE = 64
CONV_WIDTH = 31
CONV_HALO = 32
GMLP_CHUNK = 128
GMLP_HEADS = 4
ATT_DILS = (1, 4, 16)
ATT_BLK = 128
ATT_HEADS = 8
HEAD_DIM = 64
ATT_W = ATT_HEADS * HEAD_DIM
N_CHIPS = 4
ADAM_LR, ADAM_B1, ADAM_B2, ADAM_EPS, ADAM_WD, ADAM_STEP = 1e-3, 0.9, 0.999, 1e-8, 0.01, 10

VMEM_BYTES_V7X = 64 * 1024 * 1024
VMEM_LIMIT = VMEM_BYTES_V7X - 8 * 1024 * 1024
MASK_VALUE = -1e30
LANE_TILE = 128


def _cp(sem=None):
    return pltpu.CompilerParams(dimension_semantics=sem, vmem_limit_bytes=VMEM_LIMIT)


def _pick_tile(total, target):
    for cand in range(min(target, total) // LANE_TILE * LANE_TILE, 0, -LANE_TILE):
        if total % cand == 0:
            return cand
    return total


class _Stored:
    def __init__(self, arr, kind="plain", lead=()):
        self.arr, self.kind, self.lead = arr, kind, tuple(lead)
        r, c = arr.shape[-2:]
        self.shape = (r, c * N_CHIPS) if kind == "cols" else (r * N_CHIPS, c) if kind == "rows" else (r, c)

    def spec(self, br, bc, rc_of):
        lead, nl = self.lead, len(self.lead)
        if self.kind == "plain":
            return pl.BlockSpec((None,) * nl + (br, bc), lambda i, j, k: (*lead, *rc_of(i, j, k)))
        if self.kind == "cols":
            per = self.arr.shape[-1] // bc
            assert per * bc == self.arr.shape[-1]

            def imap(i, j, k):
                r, c = rc_of(i, j, k)
                return (c // per, *lead, r, c % per)
        else:
            per = self.arr.shape[-2] // br
            assert per * br == self.arr.shape[-2]

            def imap(i, j, k):
                r, c = rc_of(i, j, k)
                return (r // per, *lead, r % per, c)
        return pl.BlockSpec((None,) * (nl + 1) + (br, bc), imap)


def _mm(a, b, *, ta=False, tb=False, out_dtypes=(F32,), tm=1024, tn=1024, tk=1024,
        pro_a=None, pro_b=None, epi=None, extras=(), out=None, name):
    if ta:
        K, M = a.shape
    else:
        M, K = a.shape
    if not isinstance(b, _Stored):
        b = _Stored(b)
    N, Kb = b.shape if tb else b.shape[::-1]
    assert K == Kb, (a.shape, b.shape, ta, tb)
    n_unit = b.arr.shape[-1] if (b.kind == "cols" and not tb) else b.arr.shape[-2] if (b.kind == "rows" and tb) else N
    k_unit = b.arr.shape[-1] if (b.kind == "cols" and tb) else b.arr.shape[-2] if (b.kind == "rows" and not tb) else K
    m_unit = M
    if out is not None:
        m_unit, n_unit = out[4], math.gcd(n_unit, out[5])
    tm, tn, tk = _pick_tile(m_unit, tm), _pick_tile(n_unit, tn), _pick_tile(k_unit, tk)
    nk = K // tk
    a_spec = (pl.BlockSpec((tk, tm), lambda i, j, k: (k, i)) if ta
              else pl.BlockSpec((tm, tk), lambda i, j, k: (i, k)))
    b_spec = b.spec(tn, tk, lambda i, j, k: (j, k)) if tb else b.spec(tk, tn, lambda i, j, k: (k, j))
    ex_specs = []
    for e in extras:
        if e.shape[0] == 1:
            ex_specs.append(pl.BlockSpec((1, tn), lambda i, j, k: (0, j)))
        else:
            assert e.shape == (M, N), (e.shape, M, N)
            ex_specs.append(pl.BlockSpec((tm, tn), lambda i, j, k: (i, j)))
    dims = (((0 if ta else 1,), (1 if tb else 0,)), ((), ()))
    n_ex, n_out = len(extras), len(out_dtypes)
    direct = epi is None and n_out == 1 and out_dtypes[0] == F32
    use_acc = nk > 1 and not direct
    operands, aliases, alias_specs = [a, b.arr, *extras], {}, []
    if out is None:
        out_specs = [pl.BlockSpec((tm, tn), lambda i, j, k: (i, j)) for _ in out_dtypes]
        out_shape = [jax.ShapeDtypeStruct((M, N), dt) for dt in out_dtypes]
    else:
        shape, block_fn, imap_fn, alias = out[:4]
        assert n_out == 1
        out_specs = [pl.BlockSpec(block_fn(tm, tn), imap_fn(tm, tn))]
        out_shape = [jax.ShapeDtypeStruct(shape, out_dtypes[0])]
        if alias is not None:
            operands.append(alias)
            aliases = {len(operands) - 1: 0}
            alias_specs = [pl.BlockSpec(memory_space=pl.ANY)]
    n_in = len(operands)

    def finish(r, ex, outs):
        res = epi(r, *[e[...] for e in ex]) if epi is not None else (r,)
        for o, v in zip(outs, res):
            o[...] = v.astype(o.dtype)

    def body(*refs):
        a_ref, b_ref = refs[:2]
        ex = refs[2:2 + n_ex]
        outs = refs[n_in:n_in + n_out]
        at, bt = a_ref[...], b_ref[...]
        if pro_a is not None:
            at = pro_a(at)
        if pro_b is not None:
            bt = pro_b(bt)
        part = lax.dot_general(at, bt, dims, preferred_element_type=F32)
        if nk == 1:
            finish(part, ex, outs)
            return
        acc = refs[-1] if use_acc else outs[0]
        k = pl.program_id(2)

        @pl.when(k == 0)
        def _():
            acc[...] = part

        @pl.when(k > 0)
        def _():
            acc[...] += part

        if use_acc:
            @pl.when(k == nk - 1)
            def _():
                finish(acc[...], ex, outs)

    res = pl.pallas_call(
        body, grid=(M // tm, N // tn, nk),
        in_specs=[a_spec, b_spec] + ex_specs + alias_specs,
        out_specs=out_specs, out_shape=out_shape,
        scratch_shapes=[pltpu.VMEM((tm, tn), F32)] if use_acc else [],
        input_output_aliases=aliases,
        compiler_params=_cp(("parallel", "parallel", "arbitrary")), name=name,
    )(*operands)
    return res[0] if n_out == 1 else res


def _to_bf16(t):
    return t.astype(BF16)


def _pick_rows(total, target):
    for cand in range(min(target, total) // 8 * 8, 0, -8):
        if total % cand == 0:
            return cand
    return total


def _rowwise(f, rows, params, out_dtypes, *, tr, name):
    T = rows[0].shape[0]
    tr = _pick_rows(T, tr)
    nr, npar = len(rows), len(params)
    blk = [jax.ShapeDtypeStruct((tr, r.shape[1]), F32) for r in rows]
    blk += [jax.ShapeDtypeStruct(p.shape, F32) for p in params]
    out_avals = jax.eval_shape(f, *blk)

    def body(*refs):
        res = f(*[r[...].astype(F32) for r in refs[:nr + npar]])
        for o, v in zip(refs[nr + npar:], res):
            o[...] = v.astype(o.dtype)

    out = pl.pallas_call(
        body, grid=(T // tr,),
        in_specs=[pl.BlockSpec((tr, r.shape[1]), lambda i: (i, 0)) for r in rows]
        + [pl.BlockSpec(p.shape, lambda i, nd=p.ndim: (0,) * nd) for p in params],
        out_specs=[pl.BlockSpec((tr, o.shape[1]), lambda i: (i, 0)) for o in out_avals],
        out_shape=[jax.ShapeDtypeStruct((T, o.shape[1]), dt) for o, dt in zip(out_avals, out_dtypes)],
        compiler_params=_cp(("parallel",)), name=name,
    )(*rows, *params)
    return out


def _rowwise_vjp(f, rows, params, cots, drow_dtypes, *, adds=None, tr, name):
    adds = adds or {}
    T = rows[0].shape[0]
    tr = _pick_rows(T, tr)
    nr, npar, nc = len(rows), len(params), len(cots)
    want, want_dt = [], []
    for i, dt in enumerate(drow_dtypes):
        for one in (dt if isinstance(dt, tuple) else (dt,)):
            if one is not None:
                want.append(i)
                want_dt.append(one)
    add_idx = sorted(set(i for i in want if i in adds))
    add_arrays = [adds[i] for i in add_idx]
    na = len(add_arrays)

    def body(*refs):
        ins = [r[...].astype(F32) for r in refs[:nr + npar]]
        cvals = [r[...].astype(F32) for r in refs[nr + npar:nr + npar + nc]]
        avals = refs[nr + npar + nc:nr + npar + nc + na]
        outs = refs[nr + npar + nc + na:]
        _, vjp = jax.vjp(f, *ins)
        grads = vjp(tuple(cvals))
        for o, i in zip(outs[:len(want)], want):
            g = grads[i]
            if i in adds:
                g = g + avals[add_idx.index(i)][...].astype(F32)
            o[...] = g.astype(o.dtype)
        step = pl.program_id(0)
        for o, g in zip(outs[len(want):], grads[nr:]):
            @pl.when(step == 0)
            def _(o=o):
                o[...] = jnp.zeros_like(o)
            o[...] += g

    rspec = lambda r: pl.BlockSpec((tr, r.shape[1]), lambda i: (i, 0))
    pspec = lambda p: pl.BlockSpec(p.shape, lambda i, nd=p.ndim: (0,) * nd)
    out = pl.pallas_call(
        body, grid=(T // tr,),
        in_specs=[rspec(r) for r in rows] + [pspec(p) for p in params] + [rspec(c) for c in cots]
        + [rspec(a) for a in add_arrays],
        out_specs=[rspec(rows[i]) for i in want] + [pspec(p) for p in params],
        out_shape=[jax.ShapeDtypeStruct(rows[i].shape, dt) for i, dt in zip(want, want_dt)]
        + [jax.ShapeDtypeStruct(p.shape, F32) for p in params],
        compiler_params=_cp(("arbitrary",)), name=name,
    )(*rows, *params, *cots, *add_arrays)
    return out[:len(want)], out[len(want):]


def _f_rms(x, g):
    return (x * lax.rsqrt(jnp.mean(x * x, axis=-1, keepdims=True) + EPS) * g,)


def _ln(x, g, b):
    mu = jnp.mean(x, axis=-1, keepdims=True)
    var = jnp.mean(jnp.square(x - mu), axis=-1, keepdims=True)
    return (x - mu) * lax.rsqrt(var + EPS) * g + b


def _f_glu(z):
    d = z.shape[1] // 2
    return (z[:, :d] * jax.nn.sigmoid(z[:, d:]),)


def _f_bias_glu(z, b):
    return _f_glu(z + b)


def _f_ln_silu(y, b_dw, g, b):
    return (jax.nn.silu(_ln(y + b_dw, g, b)),)


def _f_gelu_ln(z, g, b):
    d = z.shape[1] // 2
    zz = jax.nn.gelu(z)
    return zz[:, :d], _ln(zz[:, d:], g, b)


def _f_gelu(y):
    return (jax.nn.gelu(y),)


def _f_merge(o0, o1, o2, l0, l1, l2):
    m = jnp.maximum(jnp.maximum(l0, l1), l2)
    e0, e1, e2 = jnp.exp(l0 - m), jnp.exp(l1 - m), jnp.exp(l2 - m)
    s = e0 + e1 + e2
    pair = 2 * HEAD_DIM
    first_head = lax.broadcasted_iota(jnp.int32, (o0.shape[0], pair), 1) < HEAD_DIM
    cols = []
    for hp in range(o0.shape[1] // pair):
        acc = None
        for o, e in ((o0, e0), (o1, e1), (o2, e2)):
            wgt = e / s
            wp = jnp.where(first_head, wgt[:, 2 * hp:2 * hp + 1], wgt[:, 2 * hp + 1:2 * hp + 2])
            term = wp * o[:, hp * pair:(hp + 1) * pair]
            acc = term if acc is None else acc + term
        cols.append(acc)
    return (jnp.concatenate(cols, axis=1),)


def _f_add(a, b):
    return (a + b,)


def _loss_head(x, tgt, g, *, tr, name):
    T, D = x.shape
    tr = min(tr, T)

    def f(xv, gv, tv):
        y = _f_rms(xv, gv)[0]
        return 0.5 * jnp.mean(jnp.square(y - tv), axis=-1, keepdims=True)

    def body(x_ref, t_ref, g_ref, loss_ref, dx_ref, dxb_ref, dg_ref):
        tv = t_ref[...]
        l, vjp = jax.vjp(lambda xv, gv: f(xv, gv, tv), x_ref[...], g_ref[...])
        dx, dg = vjp(jnp.ones_like(l))
        dx_ref[...] = dx
        dxb_ref[...] = dx.astype(BF16)

        @pl.when(pl.program_id(0) == 0)
        def _():
            loss_ref[...] = jnp.zeros_like(loss_ref)
            dg_ref[...] = jnp.zeros_like(dg_ref)

        loss_ref[...] += jnp.sum(l)
        dg_ref[...] += dg

    return pl.pallas_call(
        body, grid=(T // tr,),
        in_specs=[pl.BlockSpec((tr, D), lambda i: (i, 0)), pl.BlockSpec((tr, D), lambda i: (i, 0)),
                  pl.BlockSpec((1, D), lambda i: (0, 0))],
        out_specs=[pl.BlockSpec((1, 128), lambda i: (0, 0)), pl.BlockSpec((tr, D), lambda i: (i, 0)),
                   pl.BlockSpec((tr, D), lambda i: (i, 0)), pl.BlockSpec((1, D), lambda i: (0, 0))],
        out_shape=[jax.ShapeDtypeStruct((1, 128), F32), jax.ShapeDtypeStruct((T, D), F32),
                   jax.ShapeDtypeStruct((T, D), BF16), jax.ShapeDtypeStruct((1, D), F32)],
        compiler_params=_cp(("arbitrary",)), name=name,
    )(x, tgt, g)


def _adamw(w, g, m, v, *, name):
    R, C = w.shape
    tr = _pick_rows(R, max(8, 2 * 1024 * 1024 // (4 * C)))
    c1 = 1.0 - ADAM_B1 ** ADAM_STEP
    c2 = 1.0 - ADAM_B2 ** ADAM_STEP

    def body(w_ref, g_ref, m_ref, v_ref, d_ref, nm_ref, nv_ref):
        gv = g_ref[...]
        nm = ADAM_B1 * m_ref[...] + (1.0 - ADAM_B1) * gv
        nv = ADAM_B2 * v_ref[...] + (1.0 - ADAM_B2) * jnp.square(gv)
        nm_ref[...] = nm
        nv_ref[...] = nv
        d_ref[...] = -ADAM_LR * ((nm / c1) / (jnp.sqrt(nv / c2) + ADAM_EPS) + ADAM_WD * w_ref[...])

    spec = pl.BlockSpec((tr, C), lambda i: (i, 0))
    return pl.pallas_call(
        body, grid=(R // tr,), in_specs=[spec] * 4, out_specs=[spec] * 3,
        out_shape=[jax.ShapeDtypeStruct((R, C), F32)] * 3,
        compiler_params=_cp(("parallel",)), name=name,
    )(w, g, m, v)


def _s5_prep(a_re, a_im, b_re, b_im, c_re, c_im, log_dt):
    G, N = a_re.shape
    P = b_re.shape[2]
    gpb = 128 // P
    nblk = G // gpb
    dt = jnp.exp(log_dt)[:, None]
    mag = jnp.exp(a_re * dt)
    abr, abi = mag * jnp.cos(a_im * dt), mag * jnp.sin(a_im * dt)
    den = a_re * a_re + a_im * a_im
    nr, ni = abr - 1.0, abi
    qr, qi = (nr * a_re + ni * a_im) / den, (ni * a_re - nr * a_im) / den
    bbr = qr[..., None] * b_re - qi[..., None] * b_im
    bbi = qr[..., None] * b_im + qi[..., None] * b_re
    eye = jnp.eye(gpb, dtype=F32)

    def expand_b(t):
        t = t.reshape(nblk, gpb, N, P).transpose(0, 1, 3, 2)
        return (t[:, :, :, None, :] * eye[None, :, None, :, None]).reshape(nblk, gpb * P, gpb * N)

    def expand_c(t):
        t = t.reshape(nblk, gpb, P, N).transpose(0, 1, 3, 2)
        return (t[:, :, :, None, :] * eye[None, :, None, :, None]).reshape(nblk, gpb * N, gpb * P)

    return (abr.reshape(1, G * N), abi.reshape(1, G * N), expand_b(bbr), expand_b(bbi),
            expand_c(c_re), expand_c(c_im))


def _s5_fwd(h, abr, abi, bre, bim, cre, cim, d, *, B, L, name):
    T, D = h.shape
    S = T // B
    L = min(L, S)
    nc = S // L
    nblk, cb, sb = bre.shape
    GN = abr.shape[1]

    def body(h_ref, ar_ref, ai_ref, bre_ref, bim_ref, cre_ref, cim_ref, d_ref,
             y_ref, yb_ref, xr_ref, xi_ref, er_ref, ei_ref, sr, si, car, cai):
        ci = pl.program_id(1)

        @pl.when(ci == 0)
        def _():
            car[...] = jnp.zeros_like(car)
            cai[...] = jnp.zeros_like(cai)

        for j in range(nblk):
            u = h_ref[:, j * cb:(j + 1) * cb]
            sr[:, j * sb:(j + 1) * sb] = jnp.dot(u, bre_ref[j], preferred_element_type=F32)
            si[:, j * sb:(j + 1) * sb] = jnp.dot(u, bim_ref[j], preferred_element_type=F32)
        ar, ai = ar_ref[...], ai_ref[...]

        def step(t, carry):
            pr, pi = carry
            nr = ar * pr - ai * pi + sr[pl.ds(t, 1), :]
            ni = ar * pi + ai * pr + si[pl.ds(t, 1), :]
            sr[pl.ds(t, 1), :] = nr
            si[pl.ds(t, 1), :] = ni
            return nr, ni

        pr, pi = lax.fori_loop(0, L, step, (car[...], cai[...]), unroll=4)
        car[...] = pr
        cai[...] = pi
        er_ref[0] = pr
        ei_ref[0] = pi
        for j in range(nblk):
            xr = sr[:, j * sb:(j + 1) * sb].astype(BF16)
            xi = si[:, j * sb:(j + 1) * sb].astype(BF16)
            xr_ref[:, j * sb:(j + 1) * sb] = xr
            xi_ref[:, j * sb:(j + 1) * sb] = xi
            y = (jnp.dot(xr, cre_ref[j], preferred_element_type=F32)
                 - jnp.dot(xi, cim_ref[j], preferred_element_type=F32))
            u = h_ref[:, j * cb:(j + 1) * cb].astype(F32)
            y = y + d_ref[:, j * cb:(j + 1) * cb] * u
            y_ref[:, j * cb:(j + 1) * cb] = y
            yb_ref[:, j * cb:(j + 1) * cb] = jax.nn.gelu(y).astype(BF16)

    tok = lambda w: pl.BlockSpec((L, w), lambda b, c: (b * nc + c, 0))
    whole = lambda p: pl.BlockSpec(p.shape, lambda b, c, nd=p.ndim: (0,) * nd)
    end = pl.BlockSpec((1, 1, GN), lambda b, c: (b * nc + c, 0, 0))
    return pl.pallas_call(
        body, grid=(B, nc),
        in_specs=[tok(D)] + [whole(p) for p in (abr, abi, bre, bim, cre, cim, d)],
        out_specs=[tok(D), tok(D), tok(GN), tok(GN), end, end],
        out_shape=[jax.ShapeDtypeStruct((T, D), F32), jax.ShapeDtypeStruct((T, D), BF16),
                   jax.ShapeDtypeStruct((T, GN), BF16),
                   jax.ShapeDtypeStruct((T, GN), BF16), jax.ShapeDtypeStruct((B * nc, 1, GN), F32),
                   jax.ShapeDtypeStruct((B * nc, 1, GN), F32)],
        scratch_shapes=[pltpu.VMEM((L, GN), F32), pltpu.VMEM((L, GN), F32),
                        pltpu.VMEM((1, GN), F32), pltpu.VMEM((1, GN), F32)],
        compiler_params=_cp(("arbitrary", "arbitrary")), name=name,
    )(h, abr, abi, bre, bim, cre, cim, d)


def _s5_bwd(dy, h, xr, xi, er, ei, abr, abi, bre, bim, cre, cim, d, *, B, L, name):
    T, D = h.shape
    S = T // B
    L = min(L, S)
    nc = S // L
    nblk, cb, sb = bre.shape
    GN = abr.shape[1]
    dims_nt = (((1,), (1,)), ((), ()))
    dims_tn = (((0,), (0,)), ((), ()))

    def body(dy_ref, h_ref, xr_ref, xi_ref, er_ref, ei_ref, ar_ref, ai_ref, bre_ref, bim_ref,
             cre_ref, cim_ref, d_ref,
             dh_ref, dbre_ref, dbim_ref, dcre_ref, dcim_ref, dar_ref, dai_ref, dd_ref,
             lr, li, car, cai):
        b, cstep = pl.program_id(0), pl.program_id(1)
        ci = nc - 1 - cstep

        @pl.when((b == 0) & (cstep == 0))
        def _():
            for r in (dbre_ref, dbim_ref, dcre_ref, dcim_ref, dar_ref, dai_ref, dd_ref):
                r[...] = jnp.zeros_like(r)

        @pl.when(cstep == 0)
        def _():
            car[...] = jnp.zeros_like(car)
            cai[...] = jnp.zeros_like(cai)

        for j in range(nblk):
            dyj = dy_ref[:, j * cb:(j + 1) * cb].astype(BF16)
            lr[:, j * sb:(j + 1) * sb] = lax.dot_general(dyj, cre_ref[j], dims_nt, preferred_element_type=F32)
            li[:, j * sb:(j + 1) * sb] = -lax.dot_general(dyj, cim_ref[j], dims_nt, preferred_element_type=F32)
        ar, ai = ar_ref[...], ai_ref[...]

        def step(s, carry):
            t = L - 1 - s
            pr, pi = carry
            nr = lr[pl.ds(t, 1), :] + ar * pr + ai * pi
            ni = li[pl.ds(t, 1), :] - ai * pr + ar * pi
            lr[pl.ds(t, 1), :] = nr
            li[pl.ds(t, 1), :] = ni
            return nr, ni

        pr, pi = lax.fori_loop(0, L, step, (car[...], cai[...]), unroll=4)
        car[...] = pr
        cai[...] = pi
        has_prev = (ci > 0).astype(F32)
        first_row = lax.broadcasted_iota(jnp.int32, (L, sb), 0) == 0
        for j in range(nblk):
            cs = slice(j * cb, (j + 1) * cb)
            ss = slice(j * sb, (j + 1) * sb)
            lrj, lij = lr[:, ss], li[:, ss]
            xrj, xij = xr_ref[:, ss], xi_ref[:, ss]
            pr_j = jnp.where(first_row, er_ref[0][:, ss] * has_prev, pltpu.roll(xrj.astype(F32), 1, 0))
            pi_j = jnp.where(first_row, ei_ref[0][:, ss] * has_prev, pltpu.roll(xij.astype(F32), 1, 0))
            dar_ref[:, ss] += jnp.sum(lrj * pr_j + lij * pi_j, axis=0, keepdims=True)
            dai_ref[:, ss] += jnp.sum(lij * pr_j - lrj * pi_j, axis=0, keepdims=True)
            lrb, lib = lrj.astype(BF16), lij.astype(BF16)
            hj = h_ref[:, cs]
            dyf = dy_ref[:, cs]
            dyj = dyf.astype(BF16)
            dbre_ref[j] += lax.dot_general(hj, lrb, dims_tn, preferred_element_type=F32)
            dbim_ref[j] += lax.dot_general(hj, lib, dims_tn, preferred_element_type=F32)
            dcre_ref[j] += lax.dot_general(xrj, dyj, dims_tn, preferred_element_type=F32)
            dcim_ref[j] -= lax.dot_general(xij, dyj, dims_tn, preferred_element_type=F32)
            du = (lax.dot_general(lrb, bre_ref[j], dims_nt, preferred_element_type=F32)
                  + lax.dot_general(lib, bim_ref[j], dims_nt, preferred_element_type=F32))
            dh_ref[:, cs] = du + d_ref[:, cs] * dyf
            dd_ref[:, cs] += jnp.sum(dyf * hj.astype(F32), axis=0, keepdims=True)

    tok = lambda w: pl.BlockSpec((L, w), lambda b, c: (b * nc + nc - 1 - c, 0))
    whole = lambda p: pl.BlockSpec(p.shape, lambda b, c, nd=p.ndim: (0,) * nd)
    prev_end = pl.BlockSpec((1, 1, GN), lambda b, c: (b * nc + jnp.maximum(nc - 2 - c, 0), 0, 0))
    params = (abr, abi, bre, bim, cre, cim, d)
    acc_shapes = [bre.shape, bim.shape, cre.shape, cim.shape, abr.shape, abi.shape, d.shape]
    out = pl.pallas_call(
        body, grid=(B, nc),
        in_specs=[tok(D), tok(D), tok(GN), tok(GN), prev_end, prev_end] + [whole(p) for p in params],
        out_specs=[tok(D)] + [pl.BlockSpec(s, lambda b, c, nd=len(s): (0,) * nd) for s in acc_shapes],
        out_shape=[jax.ShapeDtypeStruct((T, D), F32)] + [jax.ShapeDtypeStruct(s, F32) for s in acc_shapes],
        scratch_shapes=[pltpu.VMEM((L, GN), F32), pltpu.VMEM((L, GN), F32),
                        pltpu.VMEM((1, GN), F32), pltpu.VMEM((1, GN), F32)],
        compiler_params=_cp(("arbitrary", "arbitrary")), name=name,
    )(dy, h, xr, xi, er, ei, *params)
    return out


def _conv_fwd(zp, w, *, R, tc, name):
    B, SP, C = zp.shape
    S = SP - CONV_HALO
    R, tc = min(R, S), min(tc, C)

    def body(z_ref, w_ref, y_ref):
        def chunk(ci, _):
            start = pl.multiple_of(ci * R, 8)
            ze = z_ref[pl.ds(start, R + CONV_HALO), :]
            acc = jnp.zeros((R, tc), F32)
            for m in range(CONV_WIDTH):
                k = CONV_WIDTH - 1 - m
                sh = ze if m == 0 else pltpu.roll(ze, m, 0)
                acc = acc + w_ref[k:k + 1, :] * sh[CONV_HALO:, :]
            y_ref[pl.ds(start, R), :] = acc
            return 0

        lax.fori_loop(0, S // R, chunk, 0)

    return pl.pallas_call(
        body, grid=(B, C // tc),
        in_specs=[pl.BlockSpec((None, SP, tc), lambda b, c: (b, 0, c)),
                  pl.BlockSpec((32, tc), lambda b, c: (0, c))],
        out_specs=pl.BlockSpec((None, S, tc), lambda b, c: (b, 0, c)),
        out_shape=jax.ShapeDtypeStruct((B, S, C), F32),
        compiler_params=_cp(("parallel", "parallel")), name=name,
    )(zp, w)


def _conv_bwd(zp, dyp, w, *, R, tc, name):
    B, SP, C = zp.shape
    S = SP - CONV_HALO
    R, tc = min(R, S), min(tc, C)

    def body(z_ref, dy_ref, w_ref, dz_ref, dw_ref):
        @pl.when(pl.program_id(1) == 0)
        def _():
            dw_ref[...] = jnp.zeros_like(dw_ref)

        def chunk(ci, _):
            start = pl.multiple_of(ci * R, 8)
            ze = z_ref[pl.ds(start, R + CONV_HALO), :]
            de = dy_ref[pl.ds(start, R + CONV_HALO), :]
            dy = de[:R, :]
            acc = jnp.zeros((R, tc), F32)
            for m in range(CONV_WIDTH):
                k = CONV_WIDTH - 1 - m
                zs = ze if m == 0 else pltpu.roll(ze, m, 0)
                ds_ = de if m == 0 else pltpu.roll(de, R + CONV_HALO - m, 0)
                acc = acc + w_ref[k:k + 1, :] * ds_[:R, :]
                dw_ref[k:k + 1, :] += jnp.sum(dy * zs[CONV_HALO:, :], axis=0, keepdims=True)
            dz_ref[pl.ds(start, R), :] = acc
            return 0

        lax.fori_loop(0, S // R, chunk, 0)

    return pl.pallas_call(
        body, grid=(C // tc, B),
        in_specs=[pl.BlockSpec((None, SP, tc), lambda c, b: (b, 0, c)),
                  pl.BlockSpec((None, SP, tc), lambda c, b: (b, 0, c)),
                  pl.BlockSpec((32, tc), lambda c, b: (0, c))],
        out_specs=[pl.BlockSpec((None, S, tc), lambda c, b: (b, 0, c)),
                   pl.BlockSpec((32, tc), lambda c, b: (0, c))],
        out_shape=[jax.ShapeDtypeStruct((B, S, C), F32), jax.ShapeDtypeStruct((32, C), F32)],
        compiler_params=_cp(("parallel", "arbitrary")), name=name,
    )(zp, dyp, w)


def _gmlp_fwd(u, vn, ws, bcol, *, nck, name):
    T, E = u.shape
    H = ws.shape[0]
    he = E // H
    rows = nck * GMLP_CHUNK
    rows = min(rows, T)
    n_in = rows // GMLP_CHUNK

    def body(u_ref, v_ref, ws_ref, b_ref, o_ref):
        for c in range(n_in):
            rs = slice(c * GMLP_CHUNK, (c + 1) * GMLP_CHUNK)
            for hh in range(H):
                cs = slice(hh * he, (hh + 1) * he)
                v2 = jnp.dot(ws_ref[hh], v_ref[rs, cs].astype(BF16), preferred_element_type=F32)
                v2 = v2 + b_ref[:, hh:hh + 1]
                o_ref[rs, cs] = (u_ref[rs, cs] * v2).astype(o_ref.dtype)

    tok = pl.BlockSpec((rows, E), lambda i: (i, 0))
    return pl.pallas_call(
        body, grid=(T // rows,),
        in_specs=[tok, tok, pl.BlockSpec(ws.shape, lambda i: (0, 0, 0)), pl.BlockSpec(bcol.shape, lambda i: (0, 0))],
        out_specs=tok, out_shape=jax.ShapeDtypeStruct((T, E), BF16),
        compiler_params=_cp(("parallel",)), name=name,
    )(u, vn, ws, bcol)


def _gmlp_bwd(duv, u, vn, ws, bcol, *, nck, name):
    T, E = u.shape
    H = ws.shape[0]
    he = E // H
    rows = min(nck * GMLP_CHUNK, T)
    n_in = rows // GMLP_CHUNK
    dims_nt = (((1,), (1,)), ((), ()))
    dims_tn = (((0,), (0,)), ((), ()))

    def body(g_ref, u_ref, v_ref, ws_ref, b_ref, du_ref, dv_ref, dws_ref, db_ref):
        @pl.when(pl.program_id(0) == 0)
        def _():
            dws_ref[...] = jnp.zeros_like(dws_ref)
            db_ref[...] = jnp.zeros_like(db_ref)

        for c in range(n_in):
            rs = slice(c * GMLP_CHUNK, (c + 1) * GMLP_CHUNK)
            for hh in range(H):
                cs = slice(hh * he, (hh + 1) * he)
                vb = v_ref[rs, cs].astype(BF16)
                v2 = jnp.dot(ws_ref[hh], vb, preferred_element_type=F32) + b_ref[:, hh:hh + 1]
                g = g_ref[rs, cs]
                du_ref[rs, cs] = g * v2
                dv2 = g * u_ref[rs, cs]
                dv2b = dv2.astype(BF16)
                dv_ref[rs, cs] = lax.dot_general(ws_ref[hh], dv2b, dims_tn, preferred_element_type=F32)
                dws_ref[hh] += lax.dot_general(dv2b, vb, dims_nt, preferred_element_type=F32)
                db_ref[:, hh:hh + 1] += jnp.sum(dv2, axis=1, keepdims=True)

    tok = pl.BlockSpec((rows, E), lambda i: (i, 0))
    return pl.pallas_call(
        body, grid=(T // rows,),
        in_specs=[tok, tok, tok, pl.BlockSpec(ws.shape, lambda i: (0, 0, 0)), pl.BlockSpec(bcol.shape, lambda i: (0, 0))],
        out_specs=[tok, tok, pl.BlockSpec(ws.shape, lambda i: (0, 0, 0)), pl.BlockSpec(bcol.shape, lambda i: (0, 0))],
        out_shape=[jax.ShapeDtypeStruct((T, E), F32), jax.ShapeDtypeStruct((T, E), F32),
                   jax.ShapeDtypeStruct(ws.shape, F32), jax.ShapeDtypeStruct(bcol.shape, F32)],
        compiler_params=_cp(("arbitrary",)), name=name,
    )(duv, u, vn, ws, bcol)


PAIRS = ATT_HEADS // 2


def _att_consts():
    ji = lax.broadcasted_iota(jnp.int32, (2 * ATT_BLK, ATT_BLK), 0)
    ii = lax.broadcasted_iota(jnp.int32, (2 * ATT_BLK, ATT_BLK), 1)
    dist = ii + ATT_BLK - ji
    band = (dist >= 0) & (dist <= ATT_BLK)
    cur = ji >= ATT_BLK
    first_head = lax.broadcasted_iota(jnp.int32, (ATT_BLK, 2 * HEAD_DIM), 1) < HEAD_DIM
    return band, cur, first_head


def _att_specs(nbk, offs, nsteps, rev):
    rows = nbk * ATT_BLK
    step = (lambda i: nsteps - 1 - i) if rev else (lambda i: i)
    qoff, koff, voff = offs
    blk = lambda off: pl.BlockSpec((rows, 2 * HEAD_DIM), lambda hp, i: (step(i), off + hp))
    prev = lambda off: pl.BlockSpec((ATT_BLK, 2 * HEAD_DIM), lambda hp, i: (jnp.maximum(step(i) * nbk - 1, 0), off + hp))
    out = pl.BlockSpec((rows, 2 * HEAD_DIM), lambda hp, i: (step(i), hp))
    stat = pl.BlockSpec((2, nbk, ATT_BLK), lambda hp, i: (hp, step(i), 0))
    return [blk(qoff), blk(koff), prev(koff), blk(voff), prev(voff)], out, stat


def _att_fwd(arr, offs, *, nb, nbk, name):
    T = arr.shape[0]
    nbk = min(nbk, T // ATT_BLK)
    nsteps = T // (nbk * ATT_BLK)
    scale = HEAD_DIM ** -0.5
    dims_nt = (((1,), (1,)), ((), ()))
    dims_tn = (((0,), (0,)), ((), ()))

    def body(q_ref, k_ref, kp_ref, v_ref, vp_ref, o_ref, lse_ref):
        i = pl.program_id(1)
        band, cur, first_head = _att_consts()
        for jj in range(nbk):
            rs = slice(jj * ATT_BLK, (jj + 1) * ATT_BLK)
            ps = slice((jj - 1) * ATT_BLK, jj * ATT_BLK)
            has_prev = ((i * nbk + jj) & (nb - 1)) != 0
            valid = band & (cur | has_prev)
            kk = jnp.concatenate([kp_ref[...] if jj == 0 else k_ref[ps, :], k_ref[rs, :]], axis=0)
            vv = jnp.concatenate([vp_ref[...] if jj == 0 else v_ref[ps, :], v_ref[rs, :]], axis=0)
            q = q_ref[rs, :]
            outs = []
            for hd in range(2):
                qh = jnp.where(first_head if hd == 0 else ~first_head, q, jnp.zeros_like(q))
                st = lax.dot_general(kk, qh, dims_nt, preferred_element_type=F32) * scale
                st = jnp.where(valid, st, MASK_VALUE)
                m = jnp.max(st, axis=0, keepdims=True)
                p = jnp.exp(st - m)
                l = jnp.sum(p, axis=0, keepdims=True)
                lse_ref[hd, jj:jj + 1, :] = m + jnp.log(l)
                pn = (p / l).astype(BF16)
                outs.append(lax.dot_general(pn, vv, dims_tn, preferred_element_type=F32))
            o_ref[rs, :] = jnp.where(first_head, outs[0], outs[1])

    ins, out, stat = _att_specs(nbk, offs, nsteps, False)
    return pl.pallas_call(
        body, grid=(PAIRS, nsteps), in_specs=ins, out_specs=[out, stat],
        out_shape=[jax.ShapeDtypeStruct((T, ATT_W), F32), jax.ShapeDtypeStruct((ATT_HEADS, T // ATT_BLK, ATT_BLK), F32)],
        compiler_params=_cp(("parallel", "parallel")), name=name,
    )(arr, arr, arr, arr, arr)


def _att_bwd(arr, offs, do, lse, dlse, *, nb, nbk, name):
    T = arr.shape[0]
    nbk = min(nbk, T // ATT_BLK)
    nsteps = T // (nbk * ATT_BLK)
    scale = HEAD_DIM ** -0.5
    dims_nt = (((1,), (1,)), ((), ()))
    dims_tn = (((0,), (0,)), ((), ()))

    def body(q_ref, k_ref, kp_ref, v_ref, vp_ref, do_ref, lse_ref, dlse_ref, dq_ref, dk_ref, dv_ref, ck, cv):
        step = pl.program_id(1)
        i = nsteps - 1 - step
        band, cur, first_head = _att_consts()

        @pl.when(step == 0)
        def _():
            ck[...] = jnp.zeros_like(ck)
            cv[...] = jnp.zeros_like(cv)

        carry_k, carry_v = ck[...], cv[...]
        for jj in reversed(range(nbk)):
            rs = slice(jj * ATT_BLK, (jj + 1) * ATT_BLK)
            ps = slice((jj - 1) * ATT_BLK, jj * ATT_BLK)
            has_prev = ((i * nbk + jj) & (nb - 1)) != 0
            valid = band & (cur | has_prev)
            kk = jnp.concatenate([kp_ref[...] if jj == 0 else k_ref[ps, :], k_ref[rs, :]], axis=0)
            vv = jnp.concatenate([vp_ref[...] if jj == 0 else v_ref[ps, :], v_ref[rs, :]], axis=0)
            q = q_ref[rs, :]
            dob = do_ref[rs, :].astype(BF16)
            dqs, dkk, dvv = [], None, None
            for hd in range(2):
                sel = first_head if hd == 0 else ~first_head
                qh = jnp.where(sel, q, jnp.zeros_like(q))
                doh = jnp.where(sel, dob, jnp.zeros_like(dob))
                st = lax.dot_general(kk, qh, dims_nt, preferred_element_type=F32) * scale
                st = jnp.where(valid, st, MASK_VALUE)
                p = jnp.exp(st - lse_ref[hd, jj:jj + 1, :])
                dp = lax.dot_general(vv, doh, dims_nt, preferred_element_type=F32)
                delta = jnp.sum(p * dp, axis=0, keepdims=True)
                dsb = (p * (dp - delta + dlse_ref[hd, jj:jj + 1, :]) * scale).astype(BF16)
                dqs.append(lax.dot_general(dsb, kk, dims_tn, preferred_element_type=F32))
                dk_h = jnp.dot(dsb, qh, preferred_element_type=F32)
                dv_h = jnp.dot(p.astype(BF16), doh, preferred_element_type=F32)
                dkk = dk_h if dkk is None else dkk + dk_h
                dvv = dv_h if dvv is None else dvv + dv_h
            dq_ref[rs, :] = jnp.where(first_head, dqs[0], dqs[1]).astype(dq_ref.dtype)
            dk_ref[rs, :] = (dkk[ATT_BLK:] + carry_k).astype(dk_ref.dtype)
            dv_ref[rs, :] = (dvv[ATT_BLK:] + carry_v).astype(dv_ref.dtype)
            carry_k, carry_v = dkk[:ATT_BLK], dvv[:ATT_BLK]
        ck[...] = carry_k
        cv[...] = carry_v

    ins, out, stat = _att_specs(nbk, offs, nsteps, True)
    return pl.pallas_call(
        body, grid=(PAIRS, nsteps), in_specs=ins + [out, stat, stat], out_specs=[out] * 3,
        out_shape=[jax.ShapeDtypeStruct((T, ATT_W), BF16)] * 3,
        scratch_shapes=[pltpu.VMEM((ATT_BLK, 2 * HEAD_DIM), F32), pltpu.VMEM((ATT_BLK, 2 * HEAD_DIM), F32)],
        compiler_params=_cp(("arbitrary", "arbitrary")), name=name,
    )(arr, arr, arr, arr, arr, do, lse, dlse)


def _deinterleave(t, B, S, dil):
    if dil == 1:
        return t
    return t.reshape((B, S // dil, dil) + t.shape[1:]).swapaxes(1, 2).reshape(t.shape)


def _interleave(t, B, S, dil):
    if dil == 1:
        return t
    return t.reshape((B, dil, S // dil) + t.shape[1:]).swapaxes(1, 2).reshape(t.shape)


def _stats_to_tokens(lse, B, S, dil):
    return _interleave(lse.reshape(lse.shape[0], -1).T, B, S, dil)


def _stats_from_tokens(dl, B, S, dil):
    return _deinterleave(dl, B, S, dil).T.reshape(dl.shape[1], -1, ATT_BLK)


def _mesh_pos():
    return lax.axis_index("x"), lax.axis_index("y"), lax.axis_index("c")


def _allgather8(xs, *, name):
    m_per, n = xs.shape

    def body(x_ref, out_ref, send_sems, recv_sems, local_sem):
        x, y, c = _mesh_pos()
        me, sibling = (x, y, c), (x, y, 1 - c)
        chips = [(1 - x, y), (x, 1 - y), (1 - x, 1 - y)]

        def rows(px, py, pc):
            return out_ref.at[pl.ds((4 * px + 2 * py + pc) * m_per, m_per), :]

        def copy(k, block, to, src=None):
            return pltpu.make_async_remote_copy(
                src_ref=rows(*block) if src is None else src, dst_ref=rows(*block),
                send_sem=send_sems.at[k], recv_sem=recv_sems.at[k], device_id=to, device_id_type=MESH)

        mine = pltpu.make_async_copy(x_ref, rows(*me), local_sem)
        mine.start()
        first = [copy(0, me, sibling, src=x_ref)]
        first += [copy(1 + j, me, (*chip, c), src=x_ref) for j, chip in enumerate(chips)]
        for cp in first:
            cp.start()
        passed = [copy(4 + j, (*chip, c), sibling) for j, chip in enumerate(chips)]
        for j, chip in enumerate(chips):
            copy(1 + j, (*chip, c), me).wait_recv()
            passed[j].start()
        copy(0, sibling, me).wait_recv()
        for j, chip in enumerate(chips):
            copy(4 + j, (*chip, 1 - c), me).wait_recv()
        for cp in first + passed:
            cp.wait_send()
        mine.wait()

    return pl.pallas_call(
        body, out_shape=jax.ShapeDtypeStruct((8 * m_per, n), xs.dtype),
        in_specs=[pl.BlockSpec(memory_space=pltpu.VMEM)], out_specs=pl.BlockSpec(memory_space=pltpu.VMEM),
        scratch_shapes=[pltpu.SemaphoreType.DMA((7,)), pltpu.SemaphoreType.DMA((7,)), pltpu.SemaphoreType.DMA],
        compiler_params=pltpu.CompilerParams(vmem_limit_bytes=VMEM_LIMIT), name=name,
    )(xs)


def _hbm_call(body, arrays, out_shapes, n_sems, *, name):
    any_spec = pl.BlockSpec(memory_space=pl.ANY)
    return pl.pallas_call(
        body, out_shape=out_shapes, in_specs=[any_spec] * len(arrays), out_specs=[any_spec] * len(out_shapes),
        scratch_shapes=[pltpu.SemaphoreType.DMA((n_sems,)), pltpu.SemaphoreType.DMA((n_sems,))], name=name,
    )(*arrays)


def _other_chips(x, y):
    return [(1 - x, y), (x, 1 - y), (1 - x, 1 - y)]


def _allgather_chips(ws, *, name):
    n = len(ws)

    def body(*refs):
        ins, outs, (send_sems, recv_sems) = refs[:n], refs[n:2 * n], refs[2 * n:]
        x, y, c = _mesh_pos()
        chips = _other_chips(x, y)

        def copy(a, k, px, py, half, to, src=None):
            slot = outs[a].at[2 * px + py, half]
            return pltpu.make_async_remote_copy(
                src_ref=slot if src is None else src, dst_ref=slot,
                send_sem=send_sems.at[6 * a + k], recv_sem=recv_sems.at[6 * a + k], device_id=to, device_id_type=MESH)

        first = [copy(a, j, x, y, c, (*chip, c), src=ins[a].at[c]) for a in range(n) for j, chip in enumerate(chips)]
        for cp in first:
            cp.start()
        passed = []
        for j, chip in enumerate(chips):
            for a in range(n):
                copy(a, j, *chip, c, (x, y, c)).wait_recv()
                passed.append(copy(a, 3 + j, *chip, c, (x, y, 1 - c)))
                passed[-1].start()
        for j, chip in enumerate(chips):
            for a in range(n):
                copy(a, 3 + j, *chip, 1 - c, (x, y, c)).wait_recv()
        for cp in first + passed:
            cp.wait_send()

    return _hbm_call(body, ws, [jax.ShapeDtypeStruct((N_CHIPS,) + w.shape, w.dtype) for w in ws], 6 * n, name=name)


def _split_start(srcs, lands, after, issue, n_sems, *, name):
    ns, nl = len(srcs), len(lands)
    hbm, sem = pl.BlockSpec(memory_space=pltpu.HBM), pl.BlockSpec(memory_space=pltpu.SEMAPHORE)
    extra = [] if after is None else [after]

    def body(*refs):
        n_in = ns + nl + len(extra)
        send_sems, recv_sems = refs[n_in], refs[n_in + 1]
        issue(refs[:ns], refs[ns:ns + nl], send_sems, recv_sems)
        refs[-1][...] = jnp.zeros_like(refs[-1])

    arrays = [pltpu.with_memory_space_constraint(a, pltpu.HBM) for a in list(srcs) + list(lands)]
    out = pl.pallas_call(
        body, name=name,
        out_shape=(pltpu.SemaphoreType.DMA((n_sems,)), pltpu.SemaphoreType.DMA((n_sems,)),
                   *[pltpu.HBM(a.shape, a.dtype) for a in arrays], jax.ShapeDtypeStruct((8, 128), F32)),
        in_specs=[hbm] * (ns + nl) + [pl.BlockSpec(memory_space=pl.ANY)] * len(extra),
        out_specs=(sem, sem, *[hbm] * (ns + nl), pl.BlockSpec(memory_space=pltpu.VMEM)),
        input_output_aliases={i: 2 + i for i in range(ns + nl)},
        compiler_params=pltpu.CompilerParams(has_side_effects=pltpu.SideEffectType.DATAFLOW_SIDE_EFFECTING),
    )(*arrays, *extra)
    return out[0], out[1], list(out[2:2 + ns]), list(out[2 + ns:2 + ns + nl]), out[-1]


def _split_wait(send_sems, recv_sems, srcs, lands, after, waits, *, name):
    ns, nl = len(srcs), len(lands)
    hbm, sem = pl.BlockSpec(memory_space=pltpu.HBM), pl.BlockSpec(memory_space=pltpu.SEMAPHORE)

    def body(*refs):
        waits(refs[:ns], refs[ns:ns + nl], refs[ns + nl], refs[ns + nl + 1])

    out = pl.pallas_call(
        body, name=name,
        out_shape=tuple(pltpu.HBM(a.shape, a.dtype) for a in list(srcs) + list(lands)),
        in_specs=[hbm] * (ns + nl) + [sem, sem, pl.BlockSpec(memory_space=pl.ANY)],
        out_specs=tuple([hbm] * (ns + nl)),
        input_output_aliases={i: i for i in range(ns + nl)},
        compiler_params=pltpu.CompilerParams(has_side_effects=pltpu.SideEffectType.DATAFLOW_SIDE_EFFECTING),
    )(*srcs, *lands, send_sems, recv_sems, after)
    return list(out[:ns]), list(out[ns:])


def _gather_start(halves, after, *, name):
    n = len(halves)
    lands = [lax.empty((N_CHIPS,) + h.shape, h.dtype) for h in halves]

    def issue(srcs, dsts, send_sems, recv_sems):
        x, y, c = _mesh_pos()
        me = 2 * x + y
        for a in range(n):
            for j, (px, py) in enumerate(_other_chips(x, y)):
                for cc in range(2):
                    pltpu.make_async_remote_copy(
                        src_ref=srcs[a].at[c], dst_ref=dsts[a].at[me, c],
                        send_sem=send_sems.at[6 * a + 2 * j + cc], recv_sem=recv_sems.at[6 * a + 2 * j + c],
                        device_id=(px, py, cc), device_id_type=MESH).start()

    return _split_start(halves, lands, after, issue, 6 * n, name=name)


def _gather_wait(started, after, *, name):
    send_sems, recv_sems, halves, lands = started
    n = len(halves)

    def waits(srcs, dsts, send_sems, recv_sems):
        x, y, c = _mesh_pos()
        me = 2 * x + y
        for a in range(n):
            for j, (px, py) in enumerate(_other_chips(x, y)):
                for cc in range(2):
                    pltpu.make_async_remote_copy(
                        src_ref=srcs[a].at[cc], dst_ref=dsts[a].at[2 * px + py, cc],
                        send_sem=send_sems.at[6 * a + 2 * j + cc], recv_sem=recv_sems.at[6 * a + 2 * j + cc],
                        device_id=(px, py, cc), device_id_type=MESH).wait_recv()
        for a in range(n):
            for j, (px, py) in enumerate(_other_chips(x, y)):
                for cc in range(2):
                    pltpu.make_async_remote_copy(
                        src_ref=srcs[a].at[c], dst_ref=dsts[a].at[me, c],
                        send_sem=send_sems.at[6 * a + 2 * j + cc], recv_sem=recv_sems.at[6 * a + 2 * j + c],
                        device_id=(px, py, cc), device_id_type=MESH).wait_send()

    return _split_wait(send_sems, recv_sems, halves, lands, after, waits, name=name)


def _swap_halves(gs, *, name):
    n = len(gs)

    def body(*refs):
        ins, outs, (send_sems, recv_sems) = refs[:n], refs[n:2 * n], refs[2 * n:]
        x, y, c = _mesh_pos()
        cps = [pltpu.make_async_remote_copy(
            src_ref=ins[a].at[1 - c], dst_ref=outs[a], send_sem=send_sems.at[a], recv_sem=recv_sems.at[a],
            device_id=(x, y, 1 - c), device_id_type=MESH) for a in range(n)]
        for cp in cps:
            cp.start()
        for cp in cps:
            cp.wait()

    return _hbm_call(body, gs, [jax.ShapeDtypeStruct(g.shape[1:], g.dtype) for g in gs], n, name=name)


def _scatter_chips(ss, *, name):
    n = len(ss)

    def body(*refs):
        ins, outs, (send_sems, recv_sems) = refs[:n], refs[n:2 * n], refs[2 * n:]
        x, y, c = _mesh_pos()
        me = 2 * x + y
        chips = _other_chips(x, y)

        def copy(a, j, px, py):
            return pltpu.make_async_remote_copy(
                src_ref=ins[a].at[2 * px + py], dst_ref=outs[a].at[me],
                send_sem=send_sems.at[3 * a + j], recv_sem=recv_sems.at[3 * a + j],
                device_id=(px, py, c), device_id_type=MESH)

        def arrival(a, j, px, py):
            return pltpu.make_async_remote_copy(
                src_ref=ins[a].at[me], dst_ref=outs[a].at[2 * px + py],
                send_sem=send_sems.at[3 * a + j], recv_sem=recv_sems.at[3 * a + j],
                device_id=(px, py, c), device_id_type=MESH)

        cps = [copy(a, j, *chip) for a in range(n) for j, chip in enumerate(chips)]
        for cp in cps:
            cp.start()
        for a in range(n):
            for j, chip in enumerate(chips):
                arrival(a, j, *chip).wait_recv()
        for cp in cps:
            cp.wait_send()

    return _hbm_call(body, ss, [jax.ShapeDtypeStruct(s.shape, s.dtype) for s in ss], 3 * n, name=name)


def _share_halves(ts, *, name):
    n = len(ts)

    def body(*refs):
        ins, outs, (send_sems, recv_sems) = refs[:n], refs[n:2 * n], refs[2 * n:]
        x, y, c = _mesh_pos()
        cps = [pltpu.make_async_remote_copy(
            src_ref=ins[a], dst_ref=outs[a].at[c], send_sem=send_sems.at[a], recv_sem=recv_sems.at[a],
            device_id=(x, y, 1 - c), device_id_type=MESH) for a in range(n)]
        for cp in cps:
            cp.start()
        for a in range(n):
            pltpu.make_async_remote_copy(
                src_ref=ins[a], dst_ref=outs[a].at[1 - c], send_sem=send_sems.at[a], recv_sem=recv_sems.at[a],
                device_id=(x, y, 1 - c), device_id_type=MESH).wait_recv()
        for cp in cps:
            cp.wait_send()

    return _hbm_call(body, ts, [jax.ShapeDtypeStruct((2,) + t.shape, t.dtype) for t in ts], n, name=name)


def _half_add(g, ra, core, *, name):
    _, R, C = g.shape
    tr = _pick_rows(R, max(8, 2 * 1024 * 1024 // (4 * C)))

    def body(core_ref, g_ref, ra_ref, o_ref):
        o_ref[...] = (g_ref[...] + ra_ref[...]).astype(o_ref.dtype)

    return pl.pallas_call(
        body, out_shape=jax.ShapeDtypeStruct((R, C), BF16),
        grid_spec=pltpu.PrefetchScalarGridSpec(
            num_scalar_prefetch=1, grid=(R // tr,),
            in_specs=[pl.BlockSpec((None, tr, C), lambda i, cr: (cr[0], i, 0)),
                      pl.BlockSpec((tr, C), lambda i, cr: (i, 0))],
            out_specs=pl.BlockSpec((tr, C), lambda i, cr: (i, 0))),
        compiler_params=_cp(("parallel",)), name=name,
    )(core.reshape(1).astype(jnp.int32), g, ra)


def _sum4(rb, *, name):
    _, R, C = rb.shape
    tr = _pick_rows(R, max(8, 2 * 1024 * 1024 // (4 * C)))

    def body(r0, r1, r2, r3, o_ref):
        f = lambda r: r[...].astype(F32)
        o_ref[...] = ((f(r0) + f(r1)) + f(r2)) + f(r3)

    return pl.pallas_call(
        body, out_shape=jax.ShapeDtypeStruct((R, C), F32), grid=(R // tr,),
        in_specs=[pl.BlockSpec((None, tr, C), lambda i, k=k: (k, i, 0)) for k in range(N_CHIPS)],
        out_specs=pl.BlockSpec((tr, C), lambda i: (i, 0)),
        compiler_params=_cp(("parallel",)), name=name,
    )(rb, rb, rb, rb)


TR = 256
S5_CHUNK = 256


def _rms_fwd(x, g, name):
    return _rowwise(_f_rms, [x], [g], (BF16,), tr=TR, name=name)[0]


def _rms_bwd(x, g, dh, gx, name):
    (dx, dxb), (dg,) = _rowwise_vjp(_f_rms, [x], [g], [dh], [(F32, BF16)], adds={0: gx}, tr=TR, name=name)
    return dx, dxb, dg


def _grad_cols(M, Nq):
    def imap(tm, tn):
        hp, per = (M // 2) // tm, Nq // tn
        assert hp * tm * 2 == M and per * tn == Nq, (M, Nq, tm, tn)
        return lambda i, j, k: (i // hp, j // per, i % hp, j % per)
    return (2, N_CHIPS, M // 2, Nq), lambda tm, tn: (None, None, tm, tn), imap, None, M // 2, Nq


def _grad_rows(Mq, N):
    def imap(tm, tn):
        po, hp = Mq // tm, (Mq // 2) // tm
        assert hp * tm * 2 == Mq, (Mq, tm)
        return lambda i, j, k: ((i % po) // hp, i // po, (i % po) % hp, j)
    return (2, N_CHIPS, Mq // 2, N), lambda tm, tn: (None, None, tm, tn), imap, None, Mq // 2, N


def _grad_layer_cols(li, nl, M, Nq, buf):
    lh = nl // 2

    def imap(tm, tn):
        per = Nq // tn
        return lambda i, j, k: (li // lh, j // per, li % lh, i, j % per)
    return (2, N_CHIPS, lh, M, Nq), lambda tm, tn: (None, None, None, tm, tn), imap, buf, M, Nq


def _grad_layer_rows(li, nl, Mq, N, buf):
    lh = nl // 2

    def imap(tm, tn):
        po = Mq // tm
        return lambda i, j, k: (li // lh, i // po, li % lh, i % po, j)
    return (2, N_CHIPS, lh, Mq, N), lambda tm, tn: (None, None, None, tm, tn), imap, buf, Mq, N


def _mlp_fwd(x, g, w_in, w_out, li):
    h2 = _rms_fwd(x, g, f"mlp_rms_{li}")
    r = _mm(h2, w_in, out_dtypes=(BF16,), epi=lambda acc: (jnp.maximum(acc, 0.0),), name=f"mlp_in_{li}")
    x_out = _mm(r, w_out, pro_a=lambda t: t * t, epi=lambda acc, res: (acc + res,), extras=(x,),
                name=f"mlp_out_{li}")
    return x_out, (h2, r)


def _mlp_bwd(gx, gxb, x, g, w_in, w_out, saved, li, nl, bufs):
    h2, r = saved
    D, F = w_in.shape
    da = _mm(gxb, w_out, tb=True, out_dtypes=(BF16,),
             epi=lambda acc, rt: (acc * 2.0 * rt.astype(F32),), extras=(r,), name=f"mlp_dact_{li}")
    buf_in, buf_out = bufs if bufs is not None else (None, None)
    d_w_out = _mm(r, gxb, ta=True, pro_a=lambda t: t * t, tm=512, tn=1024, tk=2048,
                  out=_grad_layer_rows(li, nl, F // N_CHIPS, D, buf_out), name=f"mlp_dwout_{li}")
    d_w_in = _mm(h2, da, ta=True, tm=1024, tn=1024, tk=2048,
                 out=_grad_layer_cols(li, nl, D, F // N_CHIPS, buf_in), name=f"mlp_dwin_{li}")
    dh2 = _mm(da, w_in, tb=True, name=f"mlp_dh_{li}")
    gx_mid, gxb_mid, dg = _rms_bwd(x, g, dh2, gx, f"mlp_rms_bwd_{li}")
    return gx_mid, gxb_mid, dg, (d_w_in, d_w_out)


def _local_step(x3, tgt3, p, layer_weights, token=None):
    B, S, D = x3.shape
    T = B * S
    x = x3.reshape(T, D)
    grads = {}
    row = lambda v: v.reshape(1, -1)
    p = dict(p)
    nl = p["norm_mlp"].shape[0]
    mlp_in, mlp_out = [None] * nl, [None] * nl

    def fetch(li, after):
        wl = dict(layer_weights(li, after))
        mlp_in[li], mlp_out[li] = wl.pop("mlp_w_in"), wl.pop("mlp_w_out")
        p.update(wl)

    g0 = row(p["norm_mix"][0])
    if token is not None:
        g0 = g0 + token[:1, :1]
    h0 = _rms_fwd(x, g0, "rms_mix_0")
    s5_args = (p["ssm_a_re"][0], p["ssm_a_im"][0], p["ssm_b_re"][0], p["ssm_b_im"][0],
               p["ssm_c_re"][0], p["ssm_c_im"][0], p["ssm_log_dt"][0])
    s5_exp, s5_vjp = jax.vjp(_s5_prep, *s5_args)
    abr, abi, bre, bim, cre, cim = s5_exp
    bre_b, bim_b, cre_b, cim_b = (t.astype(BF16) for t in (bre, bim, cre, cim))
    d_skip = p["ssm_d"]
    ypre, yb, sxr, sxi, ser, sei = _s5_fwd(h0, abr, abi, bre_b, bim_b, cre_b, cim_b, d_skip, B=B, L=S5_CHUNK,
                                           name="s5_fwd")
    fetch(0, yb)
    w_glu = p["ssm_w_glu"]
    z0 = _mm(yb, w_glu, name="s5_glu_mm")
    x_mid0 = _rowwise(lambda z, xr: (_f_glu(z)[0] + xr,), [z0, x], [], (F32,), tr=TR, name="s5_glu")[0]
    x1, mlp_saved0 = _mlp_fwd(x_mid0, row(p["norm_mlp"][0]), mlp_in[0], mlp_out[0], 0)

    g1 = row(p["norm_mix"][1])
    h1 = _rms_fwd(x1, g1, "rms_mix_1")
    fetch(1, h1)
    z1 = _mm(h1, p["conv_w_pw1"], name="conv_pw1")
    zg = _rowwise(_f_bias_glu, [z1], [p["conv_b_pw1"]], (F32,), tr=TR, name="conv_glu")[0]
    zp = jnp.pad(zg.reshape(B, S, D), ((0, 0), (CONV_HALO, 0), (0, 0)))
    w_dw = jnp.pad(p["conv_w_dw"], ((0, 32 - CONV_WIDTH), (0, 0)))
    yc = _conv_fwd(zp, w_dw, R=256, tc=128, name="conv_dw").reshape(T, D)
    ln_par = [p["conv_b_dw"], p["conv_ln_g"], p["conv_ln_b"]]
    qc = _rowwise(_f_ln_silu, [yc], ln_par, (BF16,), tr=TR, name="conv_ln_silu")[0]
    x_mid1 = _mm(qc, p["conv_w_pw2"], epi=lambda acc, bias, res: (acc + bias + res,),
                 extras=(p["conv_b_pw2"], x1), name="conv_pw2")
    x2, mlp_saved1 = _mlp_fwd(x_mid1, row(p["norm_mlp"][1]), mlp_in[1], mlp_out[1], 1)

    g2 = row(p["norm_mix"][2])
    h2 = _rms_fwd(x2, g2, "rms_mix_2")
    fetch(2, h2)
    z2 = _mm(h2, p["gmlp_w_in"], name="gmlp_in")
    gl_par = [p["gmlp_ln_g"], p["gmlp_ln_b"]]
    gu, gvn = _rowwise(_f_gelu_ln, [z2], gl_par, (F32, F32), tr=TR, name="gmlp_gelu_ln")
    causal = jnp.tril(jnp.ones((GMLP_CHUNK, GMLP_CHUNK), dtype=bool))
    ws_b = jnp.where(causal[None], p["gmlp_w_s"][0], 0.0).astype(BF16)
    bcol = jnp.pad(p["gmlp_b_s"][0].T, ((0, 0), (0, 128 - GMLP_HEADS)))
    uv = _gmlp_fwd(gu, gvn, ws_b, bcol, nck=4, name="gmlp_spatial")
    x_mid2 = _mm(uv, p["gmlp_w_out"], epi=lambda acc, res: (acc + res,), extras=(x2,), name="gmlp_out")
    x3_, mlp_saved2 = _mlp_fwd(x_mid2, row(p["norm_mlp"][2]), mlp_in[2], mlp_out[2], 2)

    g3 = row(p["norm_mix"][3])
    h3 = _rms_fwd(x3_, g3, "rms_mix_3")
    fetch(3, h3)
    qkv = _mm(h3, p["attn_w_qkv"], out_dtypes=(BF16,), tn=1152, name="attn_qkv")
    ng = len(ATT_DILS)
    qkv4 = qkv.reshape(T, 3, ng, ATT_W)
    att_in, o_tok, l_tok, lses = [], [], [], []
    for gi, dil in enumerate(ATT_DILS):
        if dil == 1:
            arr, offs = qkv, tuple((i * ng + gi) * PAIRS for i in range(3))
        else:
            arr, offs = _deinterleave(qkv4[:, :, gi].reshape(T, 3 * ATT_W), B, S, dil), (0, PAIRS, 2 * PAIRS)
        att_in.append((arr, offs))
        og, lg = _att_fwd(arr, offs, nb=S // dil // ATT_BLK, nbk=8, name=f"attn_fwd_{gi}")
        lses.append(lg)
        o_tok.append(_interleave(og, B, S, dil))
        l_tok.append(_stats_to_tokens(lg, B, S, dil))
    merged2 = _rowwise(_f_merge, o_tok + l_tok, [], (BF16,), tr=TR, name="attn_merge")[0]
    x_mid3 = _mm(merged2, p["attn_w_o"], epi=lambda acc, res: (acc + res,), extras=(x3_,), name="attn_out")
    x4, mlp_saved3 = _mlp_fwd(x_mid3, row(p["norm_mlp"][3]), mlp_in[3], mlp_out[3], 3)

    loss_part, gx, gxb, dgf = _loss_head(x4, tgt3.reshape(T, D), row(p["norm_final"]), tr=TR, name="loss_head")
    grads["norm_final"] = dgf.reshape(-1)
    d_norm_mix, d_norm_mlp = [None] * 4, [None] * 4
    Dq = D // N_CHIPS

    gx, gxb, d_norm_mlp[3], mlp_bufs = _mlp_bwd(
        gx, gxb, x_mid3, row(p["norm_mlp"][3]), mlp_in[3], mlp_out[3], mlp_saved3, 3, nl, None)
    dmerged = _mm(gxb, p["attn_w_o"], tb=True, name="attn_dmerged")
    grads["attn_w_o"] = _mm(merged2, gxb, ta=True, tm=256, tn=256, tk=2048, out=_grad_cols(ATT_W, Dq), name="attn_dwo")
    dml, _ = _rowwise_vjp(_f_merge, o_tok + l_tok, [], [dmerged], [F32] * 6, tr=TR, name="attn_merge_bwd")
    pieces = [[None] * ng for _ in range(3)]
    for gi, dil in enumerate(ATT_DILS):
        arr, offs = att_in[gi]
        dqkv_g = _att_bwd(arr, offs, _deinterleave(dml[gi], B, S, dil), lses[gi],
                          _stats_from_tokens(dml[ng + gi], B, S, dil), nb=S // dil // ATT_BLK, nbk=8,
                          name=f"attn_bwd_{gi}")
        for i in range(3):
            pieces[i][gi] = _interleave(dqkv_g[i], B, S, dil)
    dqkv = jnp.concatenate([pieces[i][gi] for i in range(3) for gi in range(ng)], axis=1)
    qkv_w = 3 * ng * ATT_W
    grads["attn_w_qkv"] = _mm(h3, dqkv, ta=True, tm=512, tn=1152, tk=2048, out=_grad_cols(D, qkv_w // N_CHIPS),
                              name="attn_dwqkv")
    dh3 = _mm(dqkv, p["attn_w_qkv"], tb=True, tk=1152, name="attn_dh")
    gx, gxb, d_norm_mix[3] = _rms_bwd(x3_, g3, dh3, gx, "rms_mix_bwd_3")

    gx, gxb, d_norm_mlp[2], mlp_bufs = _mlp_bwd(
        gx, gxb, x_mid2, row(p["norm_mlp"][2]), mlp_in[2], mlp_out[2], mlp_saved2, 2, nl, mlp_bufs)
    duv = _mm(gxb, p["gmlp_w_out"], tb=True, name="gmlp_duv")
    grads["gmlp_w_out"] = _mm(uv, gxb, ta=True, tm=128, tn=1024, tk=2048, out=_grad_rows(Dq, D), name="gmlp_dwout")
    du, dvn, dws, dbcol = _gmlp_bwd(duv, gu, gvn, ws_b, bcol, nck=4, name="gmlp_spatial_bwd")
    grads["gmlp_w_s"] = jnp.where(causal[None], dws, 0.0)[None]
    grads["gmlp_b_s"] = dbcol[:, :GMLP_HEADS].T[None]
    (dz2,), (dlg, dlb_) = _rowwise_vjp(_f_gelu_ln, [z2], gl_par, [du, dvn], [BF16], tr=TR, name="gmlp_gelu_ln_bwd")
    grads["gmlp_ln_g"], grads["gmlp_ln_b"] = dlg, dlb_
    grads["gmlp_w_in"] = _mm(h2, dz2, ta=True, tm=512, tn=512, tk=2048, out=_grad_cols(D, 2 * Dq), name="gmlp_dwin")
    dh2 = _mm(dz2, p["gmlp_w_in"], tb=True, name="gmlp_dh")
    gx, gxb, d_norm_mix[2] = _rms_bwd(x2, g2, dh2, gx, "rms_mix_bwd_2")

    gx, gxb, d_norm_mlp[1], mlp_bufs = _mlp_bwd(
        gx, gxb, x_mid1, row(p["norm_mlp"][1]), mlp_in[1], mlp_out[1], mlp_saved1, 1, nl, mlp_bufs)
    dqc = _mm(gxb, p["conv_w_pw2"], tb=True, name="conv_dq")
    grads["conv_w_pw2"] = _mm(qc, gxb, ta=True, tm=128, tn=1024, tk=2048, out=_grad_rows(Dq, D), name="conv_dwpw2")
    _, (db2,) = _rowwise_vjp(lambda t, b: (t + b,), [gx], [p["conv_b_pw2"]], [gx], [None], tr=TR, name="conv_db2")
    grads["conv_b_pw2"] = db2
    (dyc,), (dbdw, dcg, dcb) = _rowwise_vjp(_f_ln_silu, [yc], ln_par, [dqc], [F32], tr=TR, name="conv_ln_silu_bwd")
    grads["conv_b_dw"], grads["conv_ln_g"], grads["conv_ln_b"] = dbdw, dcg, dcb
    dyp = jnp.pad(dyc.reshape(B, S, D), ((0, 0), (0, CONV_HALO), (0, 0)))
    dzg, dwdw = _conv_bwd(zp, dyp, w_dw, R=256, tc=128, name="conv_dw_bwd")
    grads["conv_w_dw"] = dwdw[:CONV_WIDTH][None]
    (dz1,), (db1,) = _rowwise_vjp(_f_bias_glu, [z1], [p["conv_b_pw1"]], [dzg.reshape(T, D)], [BF16], tr=TR,
                                  name="conv_glu_bwd")
    grads["conv_b_pw1"] = db1
    grads["conv_w_pw1"] = _mm(h1, dz1, ta=True, tm=512, tn=512, tk=2048, out=_grad_cols(D, 2 * Dq), name="conv_dwpw1")
    dh1 = _mm(dz1, p["conv_w_pw1"], tb=True, name="conv_dh")
    gx, gxb, d_norm_mix[1] = _rms_bwd(x1, g1, dh1, gx, "rms_mix_bwd_1")

    gx, gxb, d_norm_mlp[0], mlp_bufs = _mlp_bwd(
        gx, gxb, x_mid0, row(p["norm_mlp"][0]), mlp_in[0], mlp_out[0], mlp_saved0, 0, nl, mlp_bufs)
    (dz0,), _ = _rowwise_vjp(_f_glu, [z0], [], [gx], [BF16], tr=TR, name="s5_glu_bwd")
    grads["ssm_w_glu"] = _mm(yb, dz0, ta=True, tm=512, tn=512, tk=2048, out=_grad_cols(D, 2 * Dq), name="s5_dwglu")
    dypre = _mm(dz0, w_glu, tb=True, epi=lambda acc, yp: (jax.vjp(lambda t: jax.nn.gelu(t), yp)[1](acc)[0],),
                extras=(ypre,), name="s5_dypre")
    dh0, dbre, dbim, dcre, dcim, dabr, dabi, dd = _s5_bwd(
        dypre, h0, sxr, sxi, ser, sei, abr, abi, bre_b, bim_b, cre_b, cim_b, d_skip, B=B, L=S5_CHUNK, name="s5_bwd")
    s5_grads = s5_vjp((dabr, dabi, dbre, dbim, dcre, dcim))
    for nm, gv in zip(("ssm_a_re", "ssm_a_im", "ssm_b_re", "ssm_b_im", "ssm_c_re", "ssm_c_im", "ssm_log_dt"), s5_grads):
        grads[nm] = gv[None]
    grads["ssm_d"] = dd
    gx, _, d_norm_mix[0] = _rms_bwd(x, g0, dh0, gx, "rms_mix_bwd_0")

    grads["norm_mix"] = jnp.concatenate(d_norm_mix, axis=0)
    grads["norm_mlp"] = jnp.concatenate(d_norm_mlp, axis=0)
    grads["mlp_w_in"], grads["mlp_w_out"] = mlp_bufs
    return loss_part, gx.reshape(B, S, D), grads


WEIGHTS = ['norm_mix', 'norm_mlp', 'norm_final', 'ssm_a_re', 'ssm_a_im', 'ssm_b_re', 'ssm_b_im', 'ssm_c_re',
           'ssm_c_im', 'ssm_d', 'ssm_log_dt', 'ssm_w_glu', 'conv_w_pw1', 'conv_b_pw1', 'conv_w_dw', 'conv_b_dw',
           'conv_ln_g', 'conv_ln_b', 'conv_w_pw2', 'conv_b_pw2', 'gmlp_w_in', 'gmlp_ln_g', 'gmlp_ln_b', 'gmlp_w_s',
           'gmlp_b_s', 'gmlp_w_out', 'attn_w_qkv', 'attn_w_o', 'mlp_w_in', 'mlp_w_out']
BIG_AXIS = {'ssm_w_glu': -1, 'conv_w_pw1': -1, 'conv_w_pw2': -2, 'gmlp_w_in': -1, 'gmlp_w_out': -2,
            'attn_w_qkv': -1, 'attn_w_o': -1, 'mlp_w_in': -1, 'mlp_w_out': -2}
BIG = list(BIG_AXIS)
LAYER_MIXER_WEIGHTS = (('ssm_w_glu',), ('conv_w_pw1', 'conv_w_pw2'), ('gmlp_w_in', 'gmlp_w_out'), ('attn_w_qkv', 'attn_w_o'))
SMALL_SHARDED = ['conv_b_pw1', 'conv_w_dw', 'conv_b_dw', 'conv_ln_g', 'conv_ln_b', 'conv_b_pw2', 'gmlp_ln_g', 'gmlp_ln_b']
SMALL_REPL = [n for n in WEIGHTS if n not in BIG_AXIS and n not in SMALL_SHARDED]
SMALL = SMALL_REPL + SMALL_SHARDED
LANES = 128
FLAT_COLS = 1024


def _pack(arrs, cols, row_mult):
    flat = jnp.concatenate([a.reshape(-1) for a in arrs])
    per = cols * row_mult
    n = -(-flat.shape[0] // per) * per
    return jnp.pad(flat, (0, n - flat.shape[0])).reshape(n // cols, cols)


def _unpack(flat2d, shapes):
    flat = flat2d.reshape(-1)
    out, off = [], 0
    for s in shapes:
        n = int(np.prod(s))
        out.append(flat[off:off + n].reshape(s))
        off += n
    return out


def _as_halves(shard):
    if shard.shape[0] == 1:
        shard = shard[0]
    return shard.reshape((2, shard.shape[0] // 2) + shard.shape[1:])


def _stored_weight(name, arr):
    kind = "cols" if BIG_AXIS[name] == -1 else "rows"
    if arr.shape[1] > 1:
        return [_Stored(arr, kind, lead=(li,)) for li in range(arr.shape[1])]
    arr = arr[:, 0]
    if kind == "rows":
        return arr.reshape(-1, arr.shape[-1])
    return _Stored(arr, kind)


def kernel(x, norm_mix, norm_mlp, norm_final, ssm_a_re, ssm_a_im, ssm_b_re, ssm_b_im, ssm_c_re, ssm_c_im, ssm_d, ssm_log_dt, ssm_w_glu, conv_w_pw1, conv_b_pw1, conv_w_dw, conv_b_dw, conv_ln_g, conv_ln_b, conv_w_pw2, conv_b_pw2, gmlp_w_in, gmlp_ln_g, gmlp_ln_b, gmlp_w_s, gmlp_b_s, gmlp_w_out, attn_w_qkv, attn_w_o, mlp_w_in, mlp_w_out, loss_target, m_norm_mix, m_norm_mlp, m_norm_final, m_ssm_a_re, m_ssm_a_im, m_ssm_b_re, m_ssm_b_im, m_ssm_c_re, m_ssm_c_im, m_ssm_d, m_ssm_log_dt, m_ssm_w_glu, m_conv_w_pw1, m_conv_b_pw1, m_conv_w_dw, m_conv_b_dw, m_conv_ln_g, m_conv_ln_b, m_conv_w_pw2, m_conv_b_pw2, m_gmlp_w_in, m_gmlp_ln_g, m_gmlp_ln_b, m_gmlp_w_s, m_gmlp_b_s, m_gmlp_w_out, m_attn_w_qkv, m_attn_w_o, m_mlp_w_in, m_mlp_w_out, v_norm_mix, v_norm_mlp, v_norm_final, v_ssm_a_re, v_ssm_a_im, v_ssm_b_re, v_ssm_b_im, v_ssm_c_re, v_ssm_c_im, v_ssm_d, v_ssm_log_dt, v_ssm_w_glu, v_conv_w_pw1, v_conv_b_pw1, v_conv_w_dw, v_conv_b_dw, v_conv_ln_g, v_conv_ln_b, v_conv_w_pw2, v_conv_b_pw2, v_gmlp_w_in, v_gmlp_ln_g, v_gmlp_ln_b, v_gmlp_w_s, v_gmlp_b_s, v_gmlp_w_out, v_attn_w_qkv, v_attn_w_o, v_mlp_w_in, v_mlp_w_out):
    args = dict(locals())
    w = {n: args[n] for n in WEIGHTS}
    m = {n: args["m_" + n] for n in WEIGHTS}
    v = {n: args["v_" + n] for n in WEIGHTS}
    chip = 2 * lax.axis_index("x") + lax.axis_index("y")
    core = lax.axis_index("c")

    big_shapes = [w[n].shape for n in BIG]
    started, token = [], None
    for li, mixer in enumerate(LAYER_MIXER_WEIGHTS):
        names = list(mixer) + ["mlp_w_in", "mlp_w_out"]
        shards = [w[n][0] for n in mixer] + [w["mlp_w_in"][li], w["mlp_w_out"][li]]
        halves = [s.astype(BF16).reshape((2, s.shape[0] // 2) + s.shape[1:]) for s in shards]
        send_sems, recv_sems, halves, lands, token = _gather_start(halves, token, name=f"gather_start_{li}")
        started.append((names, (send_sems, recv_sems, halves, lands)))

    def layer_weights(li, after):
        names, st = started[li]
        halves, lands = _gather_wait(st, after, name=f"gather_wait_{li}")
        out = {}
        for n, h, arr in zip(names, halves, lands):
            arr = lax.dynamic_update_index_in_dim(arr, h, chip, axis=0)
            arr = arr.reshape((N_CHIPS, arr.shape[1] * arr.shape[2]) + arr.shape[3:])
            if BIG_AXIS[n] == -1:
                out[n] = _Stored(arr, "cols")
            else:
                out[n] = arr.reshape(-1, arr.shape[-1])
        return out

    p = {}
    sm_shapes = [w[n].shape for n in SMALL_SHARDED]
    sflat = _pack([w[n] for n in SMALL_SHARDED], LANES, 8)
    rs = sflat.shape[0]
    sall = _allgather8(sflat, name="gather_small").reshape(8, rs, LANES)
    per_chip = [_unpack(sall[2 * k], sm_shapes) for k in range(N_CHIPS)]
    for i, n in enumerate(SMALL_SHARDED):
        p[n] = jnp.concatenate([per_chip[k][i] for k in range(N_CHIPS)], axis=-1)
    for n in SMALL_REPL:
        p[n] = w[n]
    p['conv_w_dw'] = p['conv_w_dw'][0]

    loss_part, grad_x, g = _local_step(x, loss_target, p, layer_weights, token)
    loss = lax.psum(loss_part[0, 0], ("x", "y", "c"))

    g3 = [g[n].reshape(2, N_CHIPS, -1, g[n].shape[-1]) for n in BIG]
    from_sibling = _swap_halves(g3, name="grads_swap_halves")
    chip_sums = []
    for n, own, sib in zip(BIG, g3, from_sibling):
        _, _, r, cols = own.shape
        s = _half_add(own.reshape(2, N_CHIPS * r, cols), sib.reshape(N_CHIPS * r, cols), core, name="chip_sum_" + n)
        chip_sums.append(s.reshape(N_CHIPS, r, cols))
    arrived = _scatter_chips(chip_sums, name="grads_to_owner")
    totals = []
    for n, arr, s in zip(BIG, arrived, chip_sums):
        mine = lax.dynamic_index_in_dim(s, chip, axis=0, keepdims=True)
        arr = lax.dynamic_update_index_in_dim(arr, mine, chip, axis=0)
        totals.append(_sum4(arr, name="owner_sum_" + n))
    shared = _share_halves(totals, name="grads_share_halves")
    big_grads = {}
    for n, arr, t in zip(BIG, shared, totals):
        arr = lax.dynamic_update_index_in_dim(arr, t[None], core, axis=0)
        big_grads[n] = arr.reshape(w[n].shape)

    small_full_shapes = [g[n].shape for n in SMALL]
    gs = _pack([g[n] for n in SMALL], LANES, 8)
    rg = gs.shape[0]
    gs_all = _allgather8(gs, name="gather_small_grads").reshape(8, rg, LANES)
    gs_sum = _rowwise(lambda *a: (functools.reduce(lambda s, t: s + t, a),), [gs_all[k] for k in range(8)], [], (F32,),
                      tr=rg, name="small_grads_sum")[0]
    small_grads = dict(zip(SMALL, _unpack(gs_sum, small_full_shapes)))
    for n in SMALL:
        small_grads[n] = small_grads[n].reshape(p_shape_full(w[n], -1 if n in SMALL_SHARDED else None))
    for n in SMALL_SHARDED:
        width = w[n].shape[-1]
        small_grads[n] = lax.dynamic_slice_in_dim(small_grads[n], chip * width, width, axis=-1)

    grad, delta, new_m, new_v = {}, {}, {}, {}
    for n in BIG:
        shape = w[n].shape
        two_d = lambda t: t.reshape(-1, shape[-1])
        grad[n] = big_grads[n]
        d_, m_, v_ = _adamw(two_d(w[n]), two_d(grad[n]), two_d(m[n]), two_d(v[n]), name="adamw_" + n)
        delta[n], new_m[n], new_v[n] = d_.reshape(shape), m_.reshape(shape), v_.reshape(shape)
    sm_own_shapes = [w[n].shape for n in SMALL]
    packed = [_pack([src[n] for n in SMALL], LANES, 8) for src in (w, small_grads, m, v)]
    outs = _adamw(*packed, name="adamw_small")
    for dst, flat in zip((delta, new_m, new_v), outs):
        dst.update(dict(zip(SMALL, _unpack(flat, sm_own_shapes))))
    for n in SMALL:
        grad[n] = small_grads[n]

    return (loss, grad_x, *[grad[n] for n in WEIGHTS], *[delta[n] for n in WEIGHTS],
            *[new_m[n] for n in WEIGHTS], *[new_v[n] for n in WEIGHTS])


def p_shape_full(shard, axis):
    s = list(shard.shape)
    if axis is not None:
        s[axis] *= N_CHIPS
    return tuple(s)
```

```python
import functools
import math

import jax
import jax.numpy as jnp
import numpy as np
from jax import lax
from jax.experimental import pallas as pl
from jax.experimental.pallas import tpu as pltpu

F32 = jnp.float32
BF16 = jnp.bfloat16
MESH = pl.DeviceIdType.MESH

EPS = 1e-6
SSM_GROUP = 16
SSM_STATE = 64
CONV_WIDTH = 31
CONV_HALO = 32
GMLP_CHUNK = 128
GMLP_HEADS = 4
ATT_DILS = (1, 4, 16)
ATT_BLK = 128
ATT_HEADS = 8
HEAD_DIM = 64
ATT_W = ATT_HEADS * HEAD_DIM
N_CHIPS = 4
ADAM_LR, ADAM_B1, ADAM_B2, ADAM_EPS, ADAM_WD, ADAM_STEP = 1e-3, 0.9, 0.999, 1e-8, 0.01, 10

VMEM_BYTES_V7X = 64 * 1024 * 1024
VMEM_LIMIT = VMEM_BYTES_V7X - 8 * 1024 * 1024
MASK_VALUE = -1e30
LANE_TILE = 128


def _cp(sem=None):
    return pltpu.CompilerParams(dimension_semantics=sem, vmem_limit_bytes=VMEM_LIMIT)


def _pick_tile(total, target):
    for cand in range(min(target, total) // LANE_TILE * LANE_TILE, 0, -LANE_TILE):
        if total % cand == 0:
            return cand
    return total


class _Stored:
    def __init__(self, arr, kind="plain", lead=()):
        self.arr, self.kind, self.lead = arr, kind, tuple(lead)
        r, c = arr.shape[-2:]
        self.shape = (r, c * N_CHIPS) if kind == "cols" else (r * N_CHIPS, c) if kind == "rows" else (r, c)

    def spec(self, br, bc, rc_of):
        lead, nl = self.lead, len(self.lead)
        if self.kind == "plain":
            return pl.BlockSpec((None,) * nl + (br, bc), lambda i, j, k: (*lead, *rc_of(i, j, k)))
        if self.kind == "cols":
            per = self.arr.shape[-1] // bc
            assert per * bc == self.arr.shape[-1]

            def imap(i, j, k):
                r, c = rc_of(i, j, k)
                return (c // per, *lead, r, c % per)
        else:
            per = self.arr.shape[-2] // br
            assert per * br == self.arr.shape[-2]

            def imap(i, j, k):
                r, c = rc_of(i, j, k)
                return (r // per, *lead, r % per, c)
        return pl.BlockSpec((None,) * (nl + 1) + (br, bc), imap)


def _mm(a, b, *, ta=False, tb=False, out_dtypes=(F32,), tm=1024, tn=1024, tk=1024,
        pro_a=None, pro_b=None, epi=None, extras=(), out=None, name):
    if ta:
        K, M = a.shape
    else:
        M, K = a.shape
    if not isinstance(b, _Stored):
        b = _Stored(b)
    N, Kb = b.shape if tb else b.shape[::-1]
    assert K == Kb, (a.shape, b.shape, ta, tb)
    n_unit = b.arr.shape[-1] if (b.kind == "cols" and not tb) else b.arr.shape[-2] if (b.kind == "rows" and tb) else N
    k_unit = b.arr.shape[-1] if (b.kind == "cols" and tb) else b.arr.shape[-2] if (b.kind == "rows" and not tb) else K
    m_unit = M
    if out is not None:
        m_unit, n_unit = out[4], math.gcd(n_unit, out[5])
    tm, tn, tk = _pick_tile(m_unit, tm), _pick_tile(n_unit, tn), _pick_tile(k_unit, tk)
    nk = K // tk
    a_spec = (pl.BlockSpec((tk, tm), lambda i, j, k: (k, i)) if ta
              else pl.BlockSpec((tm, tk), lambda i, j, k: (i, k)))
    b_spec = b.spec(tn, tk, lambda i, j, k: (j, k)) if tb else b.spec(tk, tn, lambda i, j, k: (k, j))
    ex_specs = []
    for e in extras:
        if e.shape[0] == 1:
            ex_specs.append(pl.BlockSpec((1, tn), lambda i, j, k: (0, j)))
        else:
            assert e.shape == (M, N), (e.shape, M, N)
            ex_specs.append(pl.BlockSpec((tm, tn), lambda i, j, k: (i, j)))
    dims = (((0 if ta else 1,), (1 if tb else 0,)), ((), ()))
    n_ex, n_out = len(extras), len(out_dtypes)
    direct = epi is None and n_out == 1 and out_dtypes[0] == F32
    use_acc = nk > 1 and not direct
    operands, aliases, alias_specs = [a, b.arr, *extras], {}, []
    if out is None:
        out_specs = [pl.BlockSpec((tm, tn), lambda i, j, k: (i, j)) for _ in out_dtypes]
        out_shape = [jax.ShapeDtypeStruct((M, N), dt) for dt in out_dtypes]
    else:
        shape, block_fn, imap_fn, alias = out[:4]
        assert n_out == 1
        out_specs = [pl.BlockSpec(block_fn(tm, tn), imap_fn(tm, tn))]
        out_shape = [jax.ShapeDtypeStruct(shape, out_dtypes[0])]
        if alias is not None:
            operands.append(alias)
            aliases = {len(operands) - 1: 0}
            alias_specs = [pl.BlockSpec(memory_space=pl.ANY)]
    n_in = len(operands)

    def finish(r, ex, outs):
        res = epi(r, *[e[...] for e in ex]) if epi is not None else (r,)
        for o, v in zip(outs, res):
            o[...] = v.astype(o.dtype)

    def body(*refs):
        a_ref, b_ref = refs[:2]
        ex = refs[2:2 + n_ex]
        outs = refs[n_in:n_in + n_out]
        at, bt = a_ref[...], b_ref[...]
        if pro_a is not None:
            at = pro_a(at)
        if pro_b is not None:
            bt = pro_b(bt)
        part = lax.dot_general(at, bt, dims, preferred_element_type=F32)
        if nk == 1:
            finish(part, ex, outs)
            return
        acc = refs[-1] if use_acc else outs[0]
        k = pl.program_id(2)

        @pl.when(k == 0)
        def _():
            acc[...] = part

        @pl.when(k > 0)
        def _():
            acc[...] += part

        if use_acc:
            @pl.when(k == nk - 1)
            def _():
                finish(acc[...], ex, outs)

    res = pl.pallas_call(
        body, grid=(M // tm, N // tn, nk),
        in_specs=[a_spec, b_spec] + ex_specs + alias_specs,
        out_specs=out_specs, out_shape=out_shape,
        scratch_shapes=[pltpu.VMEM((tm, tn), F32)] if use_acc else [],
        input_output_aliases=aliases,
        compiler_params=_cp(("parallel", "parallel", "arbitrary")), name=name,
    )(*operands)
    return res[0] if n_out == 1 else res


def _to_bf16(t):
    return t.astype(BF16)


def _pick_rows(total, target):
    for cand in range(min(target, total) // 8 * 8, 0, -8):
        if total % cand == 0:
            return cand
    return total


def _rowwise(f, rows, params, out_dtypes, *, tr, name):
    T = rows[0].shape[0]
    tr = _pick_rows(T, tr)
    nr, npar = len(rows), len(params)
    blk = [jax.ShapeDtypeStruct((tr, r.shape[1]), F32) for r in rows]
    blk += [jax.ShapeDtypeStruct(p.shape, F32) for p in params]
    out_avals = jax.eval_shape(f, *blk)

    def body(*refs):
        res = f(*[r[...].astype(F32) for r in refs[:nr + npar]])
        for o, v in zip(refs[nr + npar:], res):
            o[...] = v.astype(o.dtype)

    out = pl.pallas_call(
        body, grid=(T // tr,),
        in_specs=[pl.BlockSpec((tr, r.shape[1]), lambda i: (i, 0)) for r in rows]
        + [pl.BlockSpec(p.shape, lambda i, nd=p.ndim: (0,) * nd) for p in params],
        out_specs=[pl.BlockSpec((tr, o.shape[1]), lambda i: (i, 0)) for o in out_avals],
        out_shape=[jax.ShapeDtypeStruct((T, o.shape[1]), dt) for o, dt in zip(out_avals, out_dtypes)],
        compiler_params=_cp(("parallel",)), name=name,
    )(*rows, *params)
    return out


def _rowwise_vjp(f, rows, params, cots, drow_dtypes, *, adds=None, tr, name):
    adds = adds or {}
    T = rows[0].shape[0]
    tr = _pick_rows(T, tr)
    nr, npar, nc = len(rows), len(params), len(cots)
    want, want_dt = [], []
    for i, dt in enumerate(drow_dtypes):
        for one in (dt if isinstance(dt, tuple) else (dt,)):
            if one is not None:
                want.append(i)
                want_dt.append(one)
    add_idx = sorted(set(i for i in want if i in adds))
    add_arrays = [adds[i] for i in add_idx]
    na = len(add_arrays)

    def body(*refs):
        ins = [r[...].astype(F32) for r in refs[:nr + npar]]
        cvals = [r[...].astype(F32) for r in refs[nr + npar:nr + npar + nc]]
        avals = refs[nr + npar + nc:nr + npar + nc + na]
        outs = refs[nr + npar + nc + na:]
        _, vjp = jax.vjp(f, *ins)
        grads = vjp(tuple(cvals))
        for o, i in zip(outs[:len(want)], want):
            g = grads[i]
            if i in adds:
                g = g + avals[add_idx.index(i)][...].astype(F32)
            o[...] = g.astype(o.dtype)
        step = pl.program_id(0)
        for o, g in zip(outs[len(want):], grads[nr:]):
            @pl.when(step == 0)
            def _(o=o):
                o[...] = jnp.zeros_like(o)
            o[...] += g

    rspec = lambda r: pl.BlockSpec((tr, r.shape[1]), lambda i: (i, 0))
    pspec = lambda p: pl.BlockSpec(p.shape, lambda i, nd=p.ndim: (0,) * nd)
    out = pl.pallas_call(
        body, grid=(T // tr,),
        in_specs=[rspec(r) for r in rows] + [pspec(p) for p in params] + [rspec(c) for c in cots]
        + [rspec(a) for a in add_arrays],
        out_specs=[rspec(rows[i]) for i in want] + [pspec(p) for p in params],
        out_shape=[jax.ShapeDtypeStruct(rows[i].shape, dt) for i, dt in zip(want, want_dt)]
        + [jax.ShapeDtypeStruct(p.shape, F32) for p in params],
        compiler_params=_cp(("arbitrary",)), name=name,
    )(*rows, *params, *cots, *add_arrays)
    return out[:len(want)], out[len(want):]


def _f_rms(x, g):
    return (x * lax.rsqrt(jnp.mean(x * x, axis=-1, keepdims=True) + EPS) * g,)


def _ln(x, g, b):
    mu = jnp.mean(x, axis=-1, keepdims=True)
    var = jnp.mean(jnp.square(x - mu), axis=-1, keepdims=True)
    return (x - mu) * lax.rsqrt(var + EPS) * g + b


def _f_glu(z):
    d = z.shape[1] // 2
    return (z[:, :d] * jax.nn.sigmoid(z[:, d:]),)


def _f_bias_glu(z, b):
    return _f_glu(z + b)


def _f_ln_silu(y, b_dw, g, b):
    return (jax.nn.silu(_ln(y + b_dw, g, b)),)


def _f_gelu_ln(z, g, b):
    d = z.shape[1] // 2
    zz = jax.nn.gelu(z)
    return zz[:, :d], _ln(zz[:, d:], g, b)


def _f_gelu(y):
    return (jax.nn.gelu(y),)


def _f_merge(o0, o1, o2, l0, l1, l2):
    m = jnp.maximum(jnp.maximum(l0, l1), l2)
    e0, e1, e2 = jnp.exp(l0 - m), jnp.exp(l1 - m), jnp.exp(l2 - m)
    s = e0 + e1 + e2
    pair = 2 * HEAD_DIM
    first_head = lax.broadcasted_iota(jnp.int32, (o0.shape[0], pair), 1) < HEAD_DIM
    cols = []
    for hp in range(o0.shape[1] // pair):
        acc = None
        for o, e in ((o0, e0), (o1, e1), (o2, e2)):
            wgt = e / s
            wp = jnp.where(first_head, wgt[:, 2 * hp:2 * hp + 1], wgt[:, 2 * hp + 1:2 * hp + 2])
            term = wp * o[:, hp * pair:(hp + 1) * pair]
            acc = term if acc is None else acc + term
        cols.append(acc)
    return (jnp.concatenate(cols, axis=1),)


def _f_add(a, b):
    return (a + b,)


def _loss_head(x, tgt, g, *, tr, name):
    T, D = x.shape
    tr = min(tr, T)

    def f(xv, gv, tv):
        y = _f_rms(xv, gv)[0]
        return 0.5 * jnp.mean(jnp.square(y - tv), axis=-1, keepdims=True)

    def body(x_ref, t_ref, g_ref, loss_ref, dx_ref, dxb_ref, dg_ref):
        tv = t_ref[...]
        l, vjp = jax.vjp(lambda xv, gv: f(xv, gv, tv), x_ref[...], g_ref[...])
        dx, dg = vjp(jnp.ones_like(l))
        dx_ref[...] = dx
        dxb_ref[...] = dx.astype(BF16)

        @pl.when(pl.program_id(0) == 0)
        def _():
            loss_ref[...] = jnp.zeros_like(loss_ref)
            dg_ref[...] = jnp.zeros_like(dg_ref)

        loss_ref[...] += jnp.sum(l)
        dg_ref[...] += dg

    return pl.pallas_call(
        body, grid=(T // tr,),
        in_specs=[pl.BlockSpec((tr, D), lambda i: (i, 0)), pl.BlockSpec((tr, D), lambda i: (i, 0)),
                  pl.BlockSpec((1, D), lambda i: (0, 0))],
        out_specs=[pl.BlockSpec((1, 128), lambda i: (0, 0)), pl.BlockSpec((tr, D), lambda i: (i, 0)),
                   pl.BlockSpec((tr, D), lambda i: (i, 0)), pl.BlockSpec((1, D), lambda i: (0, 0))],
        out_shape=[jax.ShapeDtypeStruct((1, 128), F32), jax.ShapeDtypeStruct((T, D), F32),
                   jax.ShapeDtypeStruct((T, D), BF16), jax.ShapeDtypeStruct((1, D), F32)],
        compiler_params=_cp(("arbitrary",)), name=name,
    )(x, tgt, g)


def _adamw(w, g, m, v, *, name):
    R, C = w.shape
    tr = _pick_rows(R, max(8, 2 * 1024 * 1024 // (4 * C)))
    c1 = 1.0 - ADAM_B1 ** ADAM_STEP
    c2 = 1.0 - ADAM_B2 ** ADAM_STEP

    def body(w_ref, g_ref, m_ref, v_ref, d_ref, nm_ref, nv_ref):
        gv = g_ref[...]
        nm = ADAM_B1 * m_ref[...] + (1.0 - ADAM_B1) * gv
        nv = ADAM_B2 * v_ref[...] + (1.0 - ADAM_B2) * jnp.square(gv)
        nm_ref[...] = nm
        nv_ref[...] = nv
        d_ref[...] = -ADAM_LR * ((nm / c1) / (jnp.sqrt(nv / c2) + ADAM_EPS) + ADAM_WD * w_ref[...])

    spec = pl.BlockSpec((tr, C), lambda i: (i, 0))
    return pl.pallas_call(
        body, grid=(R // tr,), in_specs=[spec] * 4, out_specs=[spec] * 3,
        out_shape=[jax.ShapeDtypeStruct((R, C), F32)] * 3,
        compiler_params=_cp(("parallel",)), name=name,
    )(w, g, m, v)


def _s5_prep(a_re, a_im, b_re, b_im, c_re, c_im, log_dt):
    G, N = a_re.shape
    P = b_re.shape[2]
    gpb = 128 // P
    nblk = G // gpb
    dt = jnp.exp(log_dt)[:, None]
    mag = jnp.exp(a_re * dt)
    abr, abi = mag * jnp.cos(a_im * dt), mag * jnp.sin(a_im * dt)
    den = a_re * a_re + a_im * a_im
    nr, ni = abr - 1.0, abi
    qr, qi = (nr * a_re + ni * a_im) / den, (ni * a_re - nr * a_im) / den
    bbr = qr[..., None] * b_re - qi[..., None] * b_im
    bbi = qr[..., None] * b_im + qi[..., None] * b_re
    eye = jnp.eye(gpb, dtype=F32)

    def expand_b(t):
        t = t.reshape(nblk, gpb, N, P).transpose(0, 1, 3, 2)
        return (t[:, :, :, None, :] * eye[None, :, None, :, None]).reshape(nblk, gpb * P, gpb * N)

    def expand_c(t):
        t = t.reshape(nblk, gpb, P, N).transpose(0, 1, 3, 2)
        return (t[:, :, :, None, :] * eye[None, :, None, :, None]).reshape(nblk, gpb * N, gpb * P)

    return (abr.reshape(1, G * N), abi.reshape(1, G * N), expand_b(bbr), expand_b(bbi),
            expand_c(c_re), expand_c(c_im))


def _s5_fwd(h, abr, abi, bre, bim, cre, cim, d, *, B, L, name):
    T, D = h.shape
    S = T // B
    L = min(L, S)
    nc = S // L
    nblk, cb, sb = bre.shape
    GN = abr.shape[1]

    def body(h_ref, ar_ref, ai_ref, bre_ref, bim_ref, cre_ref, cim_ref, d_ref,
             y_ref, yb_ref, xr_ref, xi_ref, er_ref, ei_ref, sr, si, car, cai):
        ci = pl.program_id(1)

        @pl.when(ci == 0)
        def _():
            car[...] = jnp.zeros_like(car)
            cai[...] = jnp.zeros_like(cai)

        for j in range(nblk):
            u = h_ref[:, j * cb:(j + 1) * cb]
            sr[:, j * sb:(j + 1) * sb] = jnp.dot(u, bre_ref[j], preferred_element_type=F32)
            si[:, j * sb:(j + 1) * sb] = jnp.dot(u, bim_ref[j], preferred_element_type=F32)
        ar, ai = ar_ref[...], ai_ref[...]

        def step(t, carry):
            pr, pi = carry
            nr = ar * pr - ai * pi + sr[pl.ds(t, 1), :]
            ni = ar * pi + ai * pr + si[pl.ds(t, 1), :]
            sr[pl.ds(t, 1), :] = nr
            si[pl.ds(t, 1), :] = ni
            return nr, ni

        pr, pi = lax.fori_loop(0, L, step, (car[...], cai[...]), unroll=4)
        car[...] = pr
        cai[...] = pi
        er_ref[0] = pr
        ei_ref[0] = pi
        for j in range(nblk):
            xr = sr[:, j * sb:(j + 1) * sb].astype(BF16)
            xi = si[:, j * sb:(j + 1) * sb].astype(BF16)
            xr_ref[:, j * sb:(j + 1) * sb] = xr
            xi_ref[:, j * sb:(j + 1) * sb] = xi
            y = (jnp.dot(xr, cre_ref[j], preferred_element_type=F32)
                 - jnp.dot(xi, cim_ref[j], preferred_element_type=F32))
            u = h_ref[:, j * cb:(j + 1) * cb].astype(F32)
            y = y + d_ref[:, j * cb:(j + 1) * cb] * u
            y_ref[:, j * cb:(j + 1) * cb] = y
            yb_ref[:, j * cb:(j + 1) * cb] = jax.nn.gelu(y).astype(BF16)

    tok = lambda w: pl.BlockSpec((L, w), lambda b, c: (b * nc + c, 0))
    whole = lambda p: pl.BlockSpec(p.shape, lambda b, c, nd=p.ndim: (0,) * nd)
    end = pl.BlockSpec((1, 1, GN), lambda b, c: (b * nc + c, 0, 0))
    return pl.pallas_call(
        body, grid=(B, nc),
        in_specs=[tok(D)] + [whole(p) for p in (abr, abi, bre, bim, cre, cim, d)],
        out_specs=[tok(D), tok(D), tok(GN), tok(GN), end, end],
        out_shape=[jax.ShapeDtypeStruct((T, D), F32), jax.ShapeDtypeStruct((T, D), BF16),
                   jax.ShapeDtypeStruct((T, GN), BF16),
                   jax.ShapeDtypeStruct((T, GN), BF16), jax.ShapeDtypeStruct((B * nc, 1, GN), F32),
                   jax.ShapeDtypeStruct((B * nc, 1, GN), F32)],
        scratch_shapes=[pltpu.VMEM((L, GN), F32), pltpu.VMEM((L, GN), F32),
                        pltpu.VMEM((1, GN), F32), pltpu.VMEM((1, GN), F32)],
        compiler_params=_cp(("arbitrary", "arbitrary")), name=name,
    )(h, abr, abi, bre, bim, cre, cim, d)


def _s5_bwd(dy, h, xr, xi, er, ei, abr, abi, bre, bim, cre, cim, d, *, B, L, name):
    T, D = h.shape
    S = T // B
    L = min(L, S)
    nc = S // L
    nblk, cb, sb = bre.shape
    GN = abr.shape[1]
    dims_nt = (((1,), (1,)), ((), ()))
    dims_tn = (((0,), (0,)), ((), ()))

    def body(dy_ref, h_ref, xr_ref, xi_ref, er_ref, ei_ref, ar_ref, ai_ref, bre_ref, bim_ref,
             cre_ref, cim_ref, d_ref,
             dh_ref, dbre_ref, dbim_ref, dcre_ref, dcim_ref, dar_ref, dai_ref, dd_ref,
             lr, li, car, cai):
        b, cstep = pl.program_id(0), pl.program_id(1)
        ci = nc - 1 - cstep

        @pl.when((b == 0) & (cstep == 0))
        def _():
            for r in (dbre_ref, dbim_ref, dcre_ref, dcim_ref, dar_ref, dai_ref, dd_ref):
                r[...] = jnp.zeros_like(r)

        @pl.when(cstep == 0)
        def _():
            car[...] = jnp.zeros_like(car)
            cai[...] = jnp.zeros_like(cai)

        for j in range(nblk):
            dyj = dy_ref[:, j * cb:(j + 1) * cb].astype(BF16)
            lr[:, j * sb:(j + 1) * sb] = lax.dot_general(dyj, cre_ref[j], dims_nt, preferred_element_type=F32)
            li[:, j * sb:(j + 1) * sb] = -lax.dot_general(dyj, cim_ref[j], dims_nt, preferred_element_type=F32)
        ar, ai = ar_ref[...], ai_ref[...]

        def step(s, carry):
            t = L - 1 - s
            pr, pi = carry
            nr = lr[pl.ds(t, 1), :] + ar * pr + ai * pi
            ni = li[pl.ds(t, 1), :] - ai * pr + ar * pi
            lr[pl.ds(t, 1), :] = nr
            li[pl.ds(t, 1), :] = ni
            return nr, ni

        pr, pi = lax.fori_loop(0, L, step, (car[...], cai[...]), unroll=4)
        car[...] = pr
        cai[...] = pi
        has_prev = (ci > 0).astype(F32)
        first_row = lax.broadcasted_iota(jnp.int32, (L, sb), 0) == 0
        for j in range(nblk):
            cs = slice(j * cb, (j + 1) * cb)
            ss = slice(j * sb, (j + 1) * sb)
            lrj, lij = lr[:, ss], li[:, ss]
            xrj, xij = xr_ref[:, ss], xi_ref[:, ss]
            pr_j = jnp.where(first_row, er_ref[0][:, ss] * has_prev, pltpu.roll(xrj.astype(F32), 1, 0))
            pi_j = jnp.where(first_row, ei_ref[0][:, ss] * has_prev, pltpu.roll(xij.astype(F32), 1, 0))
            dar_ref[:, ss] += jnp.sum(lrj * pr_j + lij * pi_j, axis=0, keepdims=True)
            dai_ref[:, ss] += jnp.sum(lij * pr_j - lrj * pi_j, axis=0, keepdims=True)
            lrb, lib = lrj.astype(BF16), lij.astype(BF16)
            hj = h_ref[:, cs]
            dyf = dy_ref[:, cs]
            dyj = dyf.astype(BF16)
            dbre_ref[j] += lax.dot_general(hj, lrb, dims_tn, preferred_element_type=F32)
            dbim_ref[j] += lax.dot_general(hj, lib, dims_tn, preferred_element_type=F32)
            dcre_ref[j] += lax.dot_general(xrj, dyj, dims_tn, preferred_element_type=F32)
            dcim_ref[j] -= lax.dot_general(xij, dyj, dims_tn, preferred_element_type=F32)
            du = (lax.dot_general(lrb, bre_ref[j], dims_nt, preferred_element_type=F32)
                  + lax.dot_general(lib, bim_ref[j], dims_nt, preferred_element_type=F32))
            dh_ref[:, cs] = du + d_ref[:, cs] * dyf
            dd_ref[:, cs] += jnp.sum(dyf * hj.astype(F32), axis=0, keepdims=True)

    tok = lambda w: pl.BlockSpec((L, w), lambda b, c: (b * nc + nc - 1 - c, 0))
    whole = lambda p: pl.BlockSpec(p.shape, lambda b, c, nd=p.ndim: (0,) * nd)
    prev_end = pl.BlockSpec((1, 1, GN), lambda b, c: (b * nc + jnp.maximum(nc - 2 - c, 0), 0, 0))
    params = (abr, abi, bre, bim, cre, cim, d)
    acc_shapes = [bre.shape, bim.shape, cre.shape, cim.shape, abr.shape, abi.shape, d.shape]
    out = pl.pallas_call(
        body, grid=(B, nc),
        in_specs=[tok(D), tok(D), tok(GN), tok(GN), prev_end, prev_end] + [whole(p) for p in params],
        out_specs=[tok(D)] + [pl.BlockSpec(s, lambda b, c, nd=len(s): (0,) * nd) for s in acc_shapes],
        out_shape=[jax.ShapeDtypeStruct((T, D), F32)] + [jax.ShapeDtypeStruct(s, F32) for s in acc_shapes],
        scratch_shapes=[pltpu.VMEM((L, GN), F32), pltpu.VMEM((L, GN), F32),
                        pltpu.VMEM((1, GN), F32), pltpu.VMEM((1, GN), F32)],
        compiler_params=_cp(("arbitrary", "arbitrary")), name=name,
    )(dy, h, xr, xi, er, ei, *params)
    return out


def _conv_fwd(zp, w, *, R, tc, name):
    B, SP, C = zp.shape
    S = SP - CONV_HALO
    R, tc = min(R, S), min(tc, C)

    def body(z_ref, w_ref, y_ref):
        def chunk(ci, _):
            start = pl.multiple_of(ci * R, 8)
            ze = z_ref[pl.ds(start, R + CONV_HALO), :]
            acc = jnp.zeros((R, tc), F32)
            for m in range(CONV_WIDTH):
                k = CONV_WIDTH - 1 - m
                sh = ze if m == 0 else pltpu.roll(ze, m, 0)
                acc = acc + w_ref[k:k + 1, :] * sh[CONV_HALO:, :]
            y_ref[pl.ds(start, R), :] = acc
            return 0

        lax.fori_loop(0, S // R, chunk, 0)

    return pl.pallas_call(
        body, grid=(B, C // tc),
        in_specs=[pl.BlockSpec((None, SP, tc), lambda b, c: (b, 0, c)),
                  pl.BlockSpec((32, tc), lambda b, c: (0, c))],
        out_specs=pl.BlockSpec((None, S, tc), lambda b, c: (b, 0, c)),
        out_shape=jax.ShapeDtypeStruct((B, S, C), F32),
        compiler_params=_cp(("parallel", "parallel")), name=name,
    )(zp, w)


def _conv_bwd(zp, dyp, w, *, R, tc, name):
    B, SP, C = zp.shape
    S = SP - CONV_HALO
    R, tc = min(R, S), min(tc, C)

    def body(z_ref, dy_ref, w_ref, dz_ref, dw_ref):
        @pl.when(pl.program_id(1) == 0)
        def _():
            dw_ref[...] = jnp.zeros_like(dw_ref)

        def chunk(ci, _):
            start = pl.multiple_of(ci * R, 8)
            ze = z_ref[pl.ds(start, R + CONV_HALO), :]
            de = dy_ref[pl.ds(start, R + CONV_HALO), :]
            dy = de[:R, :]
            acc = jnp.zeros((R, tc), F32)
            for m in range(CONV_WIDTH):
                k = CONV_WIDTH - 1 - m
                zs = ze if m == 0 else pltpu.roll(ze, m, 0)
                ds_ = de if m == 0 else pltpu.roll(de, R + CONV_HALO - m, 0)
                acc = acc + w_ref[k:k + 1, :] * ds_[:R, :]
                dw_ref[k:k + 1, :] += jnp.sum(dy * zs[CONV_HALO:, :], axis=0, keepdims=True)
            dz_ref[pl.ds(start, R), :] = acc
            return 0

        lax.fori_loop(0, S // R, chunk, 0)

    return pl.pallas_call(
        body, grid=(C // tc, B),
        in_specs=[pl.BlockSpec((None, SP, tc), lambda c, b: (b, 0, c)),
                  pl.BlockSpec((None, SP, tc), lambda c, b: (b, 0, c)),
                  pl.BlockSpec((32, tc), lambda c, b: (0, c))],
        out_specs=[pl.BlockSpec((None, S, tc), lambda c, b: (b, 0, c)),
                   pl.BlockSpec((32, tc), lambda c, b: (0, c))],
        out_shape=[jax.ShapeDtypeStruct((B, S, C), F32), jax.ShapeDtypeStruct((32, C), F32)],
        compiler_params=_cp(("parallel", "arbitrary")), name=name,
    )(zp, dyp, w)


def _gmlp_fwd(u, vn, ws, bcol, *, nck, name):
    T, E = u.shape
    H = ws.shape[0]
    he = E // H
    rows = nck * GMLP_CHUNK
    rows = min(rows, T)
    n_in = rows // GMLP_CHUNK

    def body(u_ref, v_ref, ws_ref, b_ref, o_ref):
        for c in range(n_in):
            rs = slice(c * GMLP_CHUNK, (c + 1) * GMLP_CHUNK)
            for hh in range(H):
                cs = slice(hh * he, (hh + 1) * he)
                v2 = jnp.dot(ws_ref[hh], v_ref[rs, cs].astype(BF16), preferred_element_type=F32)
                v2 = v2 + b_ref[:, hh:hh + 1]
                o_ref[rs, cs] = (u_ref[rs, cs] * v2).astype(o_ref.dtype)

    tok = pl.BlockSpec((rows, E), lambda i: (i, 0))
    return pl.pallas_call(
        body, grid=(T // rows,),
        in_specs=[tok, tok, pl.BlockSpec(ws.shape, lambda i: (0, 0, 0)), pl.BlockSpec(bcol.shape, lambda i: (0, 0))],
        out_specs=tok, out_shape=jax.ShapeDtypeStruct((T, E), BF16),
        compiler_params=_cp(("parallel",)), name=name,
    )(u, vn, ws, bcol)


def _gmlp_bwd(duv, u, vn, ws, bcol, *, nck, name):
    T, E = u.shape
    H = ws.shape[0]
    he = E // H
    rows = min(nck * GMLP_CHUNK, T)
    n_in = rows // GMLP_CHUNK
    dims_nt = (((1,), (1,)), ((), ()))
    dims_tn = (((0,), (0,)), ((), ()))

    def body(g_ref, u_ref, v_ref, ws_ref, b_ref, du_ref, dv_ref, dws_ref, db_ref):
        @pl.when(pl.program_id(0) == 0)
        def _():
            dws_ref[...] = jnp.zeros_like(dws_ref)
            db_ref[...] = jnp.zeros_like(db_ref)

        for c in range(n_in):
            rs = slice(c * GMLP_CHUNK, (c + 1) * GMLP_CHUNK)
            for hh in range(H):
                cs = slice(hh * he, (hh + 1) * he)
                vb = v_ref[rs, cs].astype(BF16)
                v2 = jnp.dot(ws_ref[hh], vb, preferred_element_type=F32) + b_ref[:, hh:hh + 1]
                g = g_ref[rs, cs]
                du_ref[rs, cs] = g * v2
                dv2 = g * u_ref[rs, cs]
                dv2b = dv2.astype(BF16)
                dv_ref[rs, cs] = lax.dot_general(ws_ref[hh], dv2b, dims_tn, preferred_element_type=F32)
                dws_ref[hh] += lax.dot_general(dv2b, vb, dims_nt, preferred_element_type=F32)
                db_ref[:, hh:hh + 1] += jnp.sum(dv2, axis=1, keepdims=True)

    tok = pl.BlockSpec((rows, E), lambda i: (i, 0))
    return pl.pallas_call(
        body, grid=(T // rows,),
        in_specs=[tok, tok, tok, pl.BlockSpec(ws.shape, lambda i: (0, 0, 0)), pl.BlockSpec(bcol.shape, lambda i: (0, 0))],
        out_specs=[tok, tok, pl.BlockSpec(ws.shape, lambda i: (0, 0, 0)), pl.BlockSpec(bcol.shape, lambda i: (0, 0))],
        out_shape=[jax.ShapeDtypeStruct((T, E), F32), jax.ShapeDtypeStruct((T, E), F32),
                   jax.ShapeDtypeStruct(ws.shape, F32), jax.ShapeDtypeStruct(bcol.shape, F32)],
        compiler_params=_cp(("arbitrary",)), name=name,
    )(duv, u, vn, ws, bcol)


PAIRS = ATT_HEADS // 2


def _att_consts():
    ji = lax.broadcasted_iota(jnp.int32, (2 * ATT_BLK, ATT_BLK), 0)
    ii = lax.broadcasted_iota(jnp.int32, (2 * ATT_BLK, ATT_BLK), 1)
    dist = ii + ATT_BLK - ji
    band = (dist >= 0) & (dist <= ATT_BLK)
    cur = ji >= ATT_BLK
    first_head = lax.broadcasted_iota(jnp.int32, (ATT_BLK, 2 * HEAD_DIM), 1) < HEAD_DIM
    return band, cur, first_head


def _att_specs(nbk, offs, nsteps, rev):
    rows = nbk * ATT_BLK
    step = (lambda i: nsteps - 1 - i) if rev else (lambda i: i)
    qoff, koff, voff = offs
    blk = lambda off: pl.BlockSpec((rows, 2 * HEAD_DIM), lambda hp, i: (step(i), off + hp))
    prev = lambda off: pl.BlockSpec((ATT_BLK, 2 * HEAD_DIM), lambda hp, i: (jnp.maximum(step(i) * nbk - 1, 0), off + hp))
    out = pl.BlockSpec((rows, 2 * HEAD_DIM), lambda hp, i: (step(i), hp))
    stat = pl.BlockSpec((2, nbk, ATT_BLK), lambda hp, i: (hp, step(i), 0))
    return [blk(qoff), blk(koff), prev(koff), blk(voff), prev(voff)], out, stat


def _att_fwd(arr, offs, *, nb, nbk, name):
    T = arr.shape[0]
    nbk = min(nbk, T // ATT_BLK)
    nsteps = T // (nbk * ATT_BLK)
    scale = HEAD_DIM ** -0.5
    dims_nt = (((1,), (1,)), ((), ()))
    dims_tn = (((0,), (0,)), ((), ()))

    def body(q_ref, k_ref, kp_ref, v_ref, vp_ref, o_ref, lse_ref):
        i = pl.program_id(1)
        band, cur, first_head = _att_consts()
        for jj in range(nbk):
            rs = slice(jj * ATT_BLK, (jj + 1) * ATT_BLK)
            ps = slice((jj - 1) * ATT_BLK, jj * ATT_BLK)
            has_prev = ((i * nbk + jj) & (nb - 1)) != 0
            valid = band & (cur | has_prev)
            kk = jnp.concatenate([kp_ref[...] if jj == 0 else k_ref[ps, :], k_ref[rs, :]], axis=0)
            vv = jnp.concatenate([vp_ref[...] if jj == 0 else v_ref[ps, :], v_ref[rs, :]], axis=0)
            q = q_ref[rs, :]
            outs = []
            for hd in range(2):
                qh = jnp.where(first_head if hd == 0 else ~first_head, q, jnp.zeros_like(q))
                st = lax.dot_general(kk, qh, dims_nt, preferred_element_type=F32) * scale
                st = jnp.where(valid, st, MASK_VALUE)
                m = jnp.max(st, axis=0, keepdims=True)
                p = jnp.exp(st - m)
                l = jnp.sum(p, axis=0, keepdims=True)
                lse_ref[hd, jj:jj + 1, :] = m + jnp.log(l)
                pn = (p / l).astype(BF16)
                outs.append(lax.dot_general(pn, vv, dims_tn, preferred_element_type=F32))
            o_ref[rs, :] = jnp.where(first_head, outs[0], outs[1])

    ins, out, stat = _att_specs(nbk, offs, nsteps, False)
    return pl.pallas_call(
        body, grid=(PAIRS, nsteps), in_specs=ins, out_specs=[out, stat],
        out_shape=[jax.ShapeDtypeStruct((T, ATT_W), F32), jax.ShapeDtypeStruct((ATT_HEADS, T // ATT_BLK, ATT_BLK), F32)],
        compiler_params=_cp(("parallel", "parallel")), name=name,
    )(arr, arr, arr, arr, arr)


def _att_bwd(arr, offs, do, lse, dlse, *, nb, nbk, name):
    T = arr.shape[0]
    nbk = min(nbk, T // ATT_BLK)
    nsteps = T // (nbk * ATT_BLK)
    scale = HEAD_DIM ** -0.5
    dims_nt = (((1,), (1,)), ((), ()))
    dims_tn = (((0,), (0,)), ((), ()))

    def body(q_ref, k_ref, kp_ref, v_ref, vp_ref, do_ref, lse_ref, dlse_ref, dq_ref, dk_ref, dv_ref, ck, cv):
        step = pl.program_id(1)
        i = nsteps - 1 - step
        band, cur, first_head = _att_consts()

        @pl.when(step == 0)
        def _():
            ck[...] = jnp.zeros_like(ck)
            cv[...] = jnp.zeros_like(cv)

        carry_k, carry_v = ck[...], cv[...]
        for jj in reversed(range(nbk)):
            rs = slice(jj * ATT_BLK, (jj + 1) * ATT_BLK)
            ps = slice((jj - 1) * ATT_BLK, jj * ATT_BLK)
            has_prev = ((i * nbk + jj) & (nb - 1)) != 0
            valid = band & (cur | has_prev)
            kk = jnp.concatenate([kp_ref[...] if jj == 0 else k_ref[ps, :], k_ref[rs, :]], axis=0)
            vv = jnp.concatenate([vp_ref[...] if jj == 0 else v_ref[ps, :], v_ref[rs, :]], axis=0)
            q = q_ref[rs, :]
            dob = do_ref[rs, :].astype(BF16)
            dqs, dkk, dvv = [], None, None
            for hd in range(2):
                sel = first_head if hd == 0 else ~first_head
                qh = jnp.where(sel, q, jnp.zeros_like(q))
                doh = jnp.where(sel, dob, jnp.zeros_like(dob))
                st = lax.dot_general(kk, qh, dims_nt, preferred_element_type=F32) * scale
                st = jnp.where(valid, st, MASK_VALUE)
                p = jnp.exp(st - lse_ref[hd, jj:jj + 1, :])
                dp = lax.dot_general(vv, doh, dims_nt, preferred_element_type=F32)
                delta = jnp.sum(p * dp, axis=0, keepdims=True)
                dsb = (p * (dp - delta + dlse_ref[hd, jj:jj + 1, :]) * scale).astype(BF16)
                dqs.append(lax.dot_general(dsb, kk, dims_tn, preferred_element_type=F32))
                dk_h = jnp.dot(dsb, qh, preferred_element_type=F32)
                dv_h = jnp.dot(p.astype(BF16), doh, preferred_element_type=F32)
                dkk = dk_h if dkk is None else dkk + dk_h
                dvv = dv_h if dvv is None else dvv + dv_h
            dq_ref[rs, :] = jnp.where(first_head, dqs[0], dqs[1]).astype(dq_ref.dtype)
            dk_ref[rs, :] = (dkk[ATT_BLK:] + carry_k).astype(dk_ref.dtype)
            dv_ref[rs, :] = (dvv[ATT_BLK:] + carry_v).astype(dv_ref.dtype)
            carry_k, carry_v = dkk[:ATT_BLK], dvv[:ATT_BLK]
        ck[...] = carry_k
        cv[...] = carry_v

    ins, out, stat = _att_specs(nbk, offs, nsteps, True)
    return pl.pallas_call(
        body, grid=(PAIRS, nsteps), in_specs=ins + [out, stat, stat], out_specs=[out] * 3,
        out_shape=[jax.ShapeDtypeStruct((T, ATT_W), BF16)] * 3,
        scratch_shapes=[pltpu.VMEM((ATT_BLK, 2 * HEAD_DIM), F32), pltpu.VMEM((ATT_BLK, 2 * HEAD_DIM), F32)],
        compiler_params=_cp(("arbitrary", "arbitrary")), name=name,
    )(arr, arr, arr, arr, arr, do, lse, dlse)


def _deinterleave(t, B, S, dil):
    if dil == 1:
        return t
    return t.reshape((B, S // dil, dil) + t.shape[1:]).swapaxes(1, 2).reshape(t.shape)


def _interleave(t, B, S, dil):
    if dil == 1:
        return t
    return t.reshape((B, dil, S // dil) + t.shape[1:]).swapaxes(1, 2).reshape(t.shape)


def _stats_to_tokens(lse, B, S, dil):
    return _interleave(lse.reshape(lse.shape[0], -1).T, B, S, dil)


def _stats_from_tokens(dl, B, S, dil):
    return _deinterleave(dl, B, S, dil).T.reshape(dl.shape[1], -1, ATT_BLK)


def _mesh_pos():
    return lax.axis_index("x"), lax.axis_index("y"), lax.axis_index("c")


def _allgather8(xs, *, name):
    m_per, n = xs.shape

    def body(x_ref, out_ref, send_sems, recv_sems, local_sem):
        x, y, c = _mesh_pos()
        me, sibling = (x, y, c), (x, y, 1 - c)
        chips = [(1 - x, y), (x, 1 - y), (1 - x, 1 - y)]

        def rows(px, py, pc):
            return out_ref.at[pl.ds((4 * px + 2 * py + pc) * m_per, m_per), :]

        def copy(k, block, to, src=None):
            return pltpu.make_async_remote_copy(
                src_ref=rows(*block) if src is None else src, dst_ref=rows(*block),
                send_sem=send_sems.at[k], recv_sem=recv_sems.at[k], device_id=to, device_id_type=MESH)

        mine = pltpu.make_async_copy(x_ref, rows(*me), local_sem)
        mine.start()
        first = [copy(0, me, sibling, src=x_ref)]
        first += [copy(1 + j, me, (*chip, c), src=x_ref) for j, chip in enumerate(chips)]
        for cp in first:
            cp.start()
        passed = [copy(4 + j, (*chip, c), sibling) for j, chip in enumerate(chips)]
        for j, chip in enumerate(chips):
            copy(1 + j, (*chip, c), me).wait_recv()
            passed[j].start()
        copy(0, sibling, me).wait_recv()
        for j, chip in enumerate(chips):
            copy(4 + j, (*chip, 1 - c), me).wait_recv()
        for cp in first + passed:
            cp.wait_send()
        mine.wait()

    return pl.pallas_call(
        body, out_shape=jax.ShapeDtypeStruct((8 * m_per, n), xs.dtype),
        in_specs=[pl.BlockSpec(memory_space=pltpu.VMEM)], out_specs=pl.BlockSpec(memory_space=pltpu.VMEM),
        scratch_shapes=[pltpu.SemaphoreType.DMA((7,)), pltpu.SemaphoreType.DMA((7,)), pltpu.SemaphoreType.DMA],
        compiler_params=pltpu.CompilerParams(vmem_limit_bytes=VMEM_LIMIT), name=name,
    )(xs)


def _hbm_call(body, arrays, out_shapes, n_sems, *, name):
    any_spec = pl.BlockSpec(memory_space=pl.ANY)
    return pl.pallas_call(
        body, out_shape=out_shapes, in_specs=[any_spec] * len(arrays), out_specs=[any_spec] * len(out_shapes),
        scratch_shapes=[pltpu.SemaphoreType.DMA((n_sems,)), pltpu.SemaphoreType.DMA((n_sems,))], name=name,
    )(*arrays)


def _other_chips(x, y):
    return [(1 - x, y), (x, 1 - y), (1 - x, 1 - y)]


def _allgather_chips(ws, *, name):
    n = len(ws)

    def body(*refs):
        ins, outs, (send_sems, recv_sems) = refs[:n], refs[n:2 * n], refs[2 * n:]
        x, y, c = _mesh_pos()
        chips = _other_chips(x, y)

        def copy(a, k, px, py, half, to, src=None):
            slot = outs[a].at[2 * px + py, half]
            return pltpu.make_async_remote_copy(
                src_ref=slot if src is None else src, dst_ref=slot,
                send_sem=send_sems.at[6 * a + k], recv_sem=recv_sems.at[6 * a + k], device_id=to, device_id_type=MESH)

        first = [copy(a, j, x, y, c, (*chip, c), src=ins[a].at[c]) for a in range(n) for j, chip in enumerate(chips)]
        for cp in first:
            cp.start()
        passed = []
        for j, chip in enumerate(chips):
            for a in range(n):
                copy(a, j, *chip, c, (x, y, c)).wait_recv()
                passed.append(copy(a, 3 + j, *chip, c, (x, y, 1 - c)))
                passed[-1].start()
        for j, chip in enumerate(chips):
            for a in range(n):
                copy(a, 3 + j, *chip, 1 - c, (x, y, c)).wait_recv()
        for cp in first + passed:
            cp.wait_send()

    return _hbm_call(body, ws, [jax.ShapeDtypeStruct((N_CHIPS,) + w.shape, w.dtype) for w in ws], 6 * n, name=name)


def _split_start(srcs, lands, after, issue, n_sems, *, name):
    ns, nl = len(srcs), len(lands)
    hbm, sem = pl.BlockSpec(memory_space=pltpu.HBM), pl.BlockSpec(memory_space=pltpu.SEMAPHORE)
    extra = [] if after is None else [after]

    def body(*refs):
        n_in = ns + nl + len(extra)
        send_sems, recv_sems = refs[n_in], refs[n_in + 1]
        issue(refs[:ns], refs[ns:ns + nl], send_sems, recv_sems)
        refs[-1][...] = jnp.zeros_like(refs[-1])

    arrays = [pltpu.with_memory_space_constraint(a, pltpu.HBM) for a in list(srcs) + list(lands)]
    out = pl.pallas_call(
        body, name=name,
        out_shape=(pltpu.SemaphoreType.DMA((n_sems,)), pltpu.SemaphoreType.DMA((n_sems,)),
                   *[pltpu.HBM(a.shape, a.dtype) for a in arrays], jax.ShapeDtypeStruct((8, 128), F32)),
        in_specs=[hbm] * (ns + nl) + [pl.BlockSpec(memory_space=pl.ANY)] * len(extra),
        out_specs=(sem, sem, *[hbm] * (ns + nl), pl.BlockSpec(memory_space=pltpu.VMEM)),
        input_output_aliases={i: 2 + i for i in range(ns + nl)},
        compiler_params=pltpu.CompilerParams(has_side_effects=pltpu.SideEffectType.DATAFLOW_SIDE_EFFECTING),
    )(*arrays, *extra)
    return out[0], out[1], list(out[2:2 + ns]), list(out[2 + ns:2 + ns + nl]), out[-1]


def _split_wait(send_sems, recv_sems, srcs, lands, after, waits, *, name):
    ns, nl = len(srcs), len(lands)
    hbm, sem = pl.BlockSpec(memory_space=pltpu.HBM), pl.BlockSpec(memory_space=pltpu.SEMAPHORE)

    def body(*refs):
        waits(refs[:ns], refs[ns:ns + nl], refs[ns + nl], refs[ns + nl + 1])

    out = pl.pallas_call(
        body, name=name,
        out_shape=tuple(pltpu.HBM(a.shape, a.dtype) for a in list(srcs) + list(lands)),
        in_specs=[hbm] * (ns + nl) + [sem, sem, pl.BlockSpec(memory_space=pl.ANY)],
        out_specs=tuple([hbm] * (ns + nl)),
        input_output_aliases={i: i for i in range(ns + nl)},
        compiler_params=pltpu.CompilerParams(has_side_effects=pltpu.SideEffectType.DATAFLOW_SIDE_EFFECTING),
    )(*srcs, *lands, send_sems, recv_sems, after)
    return list(out[:ns]), list(out[ns:])


def _gather_start(halves, after, *, name):
    n = len(halves)
    lands = [lax.empty((N_CHIPS,) + h.shape, h.dtype) for h in halves]

    def issue(srcs, dsts, send_sems, recv_sems):
        x, y, c = _mesh_pos()
        me = 2 * x + y
        for a in range(n):
            for j, (px, py) in enumerate(_other_chips(x, y)):
                for cc in range(2):
                    pltpu.make_async_remote_copy(
                        src_ref=srcs[a].at[c], dst_ref=dsts[a].at[me, c],
                        send_sem=send_sems.at[6 * a + 2 * j + cc], recv_sem=recv_sems.at[6 * a + 2 * j + c],
                        device_id=(px, py, cc), device_id_type=MESH).start()

    return _split_start(halves, lands, after, issue, 6 * n, name=name)


def _gather_wait(started, after, *, name):
    send_sems, recv_sems, halves, lands = started
    n = len(halves)

    def waits(srcs, dsts, send_sems, recv_sems):
        x, y, c = _mesh_pos()
        me = 2 * x + y
        for a in range(n):
            for j, (px, py) in enumerate(_other_chips(x, y)):
                for cc in range(2):
                    pltpu.make_async_remote_copy(
                        src_ref=srcs[a].at[cc], dst_ref=dsts[a].at[2 * px + py, cc],
                        send_sem=send_sems.at[6 * a + 2 * j + cc], recv_sem=recv_sems.at[6 * a + 2 * j + cc],
                        device_id=(px, py, cc), device_id_type=MESH).wait_recv()
        for a in range(n):
            for j, (px, py) in enumerate(_other_chips(x, y)):
                for cc in range(2):
                    pltpu.make_async_remote_copy(
                        src_ref=srcs[a].at[c], dst_ref=dsts[a].at[me, c],
                        send_sem=send_sems.at[6 * a + 2 * j + cc], recv_sem=recv_sems.at[6 * a + 2 * j + c],
                        device_id=(px, py, cc), device_id_type=MESH).wait_send()

    return _split_wait(send_sems, recv_sems, halves, lands, after, waits, name=name)


def _reduce_plan_loops(plans, chip, c, fn):
    for a, plan in enumerate(plans):
        for h, cc in plan:
            for k in range(N_CHIPS):
                fn(a, h, k, cc, jnp.logical_or(chip != k, c != cc))


def _reduce_start(srcs, lands, plans, after, *, name):
    def issue(src_refs, land_refs, send_sems, recv_sems):
        x, y, c = _mesh_pos()
        chip = 2 * x + y
        my_id = 2 * chip + c

        def send(a, h, k, cc, is_other):
            @pl.when(is_other)
            def _():
                pltpu.make_async_remote_copy(
                    src_ref=src_refs[a].at[h, k], dst_ref=land_refs[a].at[my_id],
                    send_sem=send_sems.at[8 * a + 2 * k + cc], recv_sem=recv_sems.at[8 * a + my_id],
                    device_id=(k // 2, k % 2, cc), device_id_type=MESH).start()

        _reduce_plan_loops(plans, chip, c, send)

    return _split_start(srcs, lands, after, issue, 8 * len(srcs), name=name)


def _reduce_wait(started, plans, after, *, name):
    send_sems, recv_sems, srcs, lands = started

    def waits(src_refs, land_refs, send_sems, recv_sems):
        x, y, c = _mesh_pos()
        chip = 2 * x + y
        my_id = 2 * chip + c
        for a, plan in enumerate(plans):
            for h, cc in plan:
                for s in range(2 * N_CHIPS):
                    @pl.when(jnp.logical_and(c == cc, my_id != s))
                    def _(a=a, h=h, s=s):
                        pltpu.make_async_remote_copy(
                            src_ref=src_refs[a].at[h, 0], dst_ref=land_refs[a].at[s],
                            send_sem=send_sems.at[8 * a + s], recv_sem=recv_sems.at[8 * a + s],
                            device_id=(s // 4, (s // 2) % 2, s % 2), device_id_type=MESH).wait_recv()

        def sent(a, h, k, cc, is_other):
            @pl.when(is_other)
            def _():
                pltpu.make_async_remote_copy(
                    src_ref=src_refs[a].at[h, k], dst_ref=land_refs[a].at[my_id],
                    send_sem=send_sems.at[8 * a + 2 * k + cc], recv_sem=recv_sems.at[8 * a + my_id],
                    device_id=(k // 2, k % 2, cc), device_id_type=MESH).wait_send()

        _reduce_plan_loops(plans, chip, c, sent)

    return _split_wait(send_sems, recv_sems, srcs, lands, after, waits, name=name)


def _sum8(land, own, my_id, *, name):
    n_src, R, C = land.shape
    tr = _pick_rows(R, max(8, 1024 * 1024 // (2 * C)))

    def body(id_ref, *refs):
        own_ref, o_ref = refs[n_src], refs[n_src + 1]
        me = id_ref[0]
        acc = None
        for s in range(n_src):
            term = jnp.where(me == s, own_ref[...], refs[s][...]).astype(F32)
            acc = term if acc is None else acc + term
        o_ref[...] = acc

    return pl.pallas_call(
        body, out_shape=jax.ShapeDtypeStruct((R, C), F32),
        grid_spec=pltpu.PrefetchScalarGridSpec(
            num_scalar_prefetch=1, grid=(R // tr,),
            in_specs=[pl.BlockSpec((None, tr, C), lambda i, idr, s=s: (s, i, 0)) for s in range(n_src)]
            + [pl.BlockSpec((tr, C), lambda i, idr: (i, 0))],
            out_specs=pl.BlockSpec((tr, C), lambda i, idr: (i, 0))),
        compiler_params=_cp(("parallel",)), name=name,
    )(my_id.reshape(1).astype(jnp.int32), *([land] * n_src), own)


def _swap_halves(gs, *, name):
    n = len(gs)

    def body(*refs):
        ins, outs, (send_sems, recv_sems) = refs[:n], refs[n:2 * n], refs[2 * n:]
        x, y, c = _mesh_pos()
        cps = [pltpu.make_async_remote_copy(
            src_ref=ins[a].at[1 - c], dst_ref=outs[a], send_sem=send_sems.at[a], recv_sem=recv_sems.at[a],
            device_id=(x, y, 1 - c), device_id_type=MESH) for a in range(n)]
        for cp in cps:
            cp.start()
        for cp in cps:
            cp.wait()

    return _hbm_call(body, gs, [jax.ShapeDtypeStruct(g.shape[1:], g.dtype) for g in gs], n, name=name)


def _scatter_chips(ss, *, name):
    n = len(ss)

    def body(*refs):
        ins, outs, (send_sems, recv_sems) = refs[:n], refs[n:2 * n], refs[2 * n:]
        x, y, c = _mesh_pos()
        me = 2 * x + y
        chips = _other_chips(x, y)

        def copy(a, j, px, py):
            return pltpu.make_async_remote_copy(
                src_ref=ins[a].at[2 * px + py], dst_ref=outs[a].at[me],
                send_sem=send_sems.at[3 * a + j], recv_sem=recv_sems.at[3 * a + j],
                device_id=(px, py, c), device_id_type=MESH)

        def arrival(a, j, px, py):
            return pltpu.make_async_remote_copy(
                src_ref=ins[a].at[me], dst_ref=outs[a].at[2 * px + py],
                send_sem=send_sems.at[3 * a + j], recv_sem=recv_sems.at[3 * a + j],
                device_id=(px, py, c), device_id_type=MESH)

        cps = [copy(a, j, *chip) for a in range(n) for j, chip in enumerate(chips)]
        for cp in cps:
            cp.start()
        for a in range(n):
            for j, chip in enumerate(chips):
                arrival(a, j, *chip).wait_recv()
        for cp in cps:
            cp.wait_send()

    return _hbm_call(body, ss, [jax.ShapeDtypeStruct(s.shape, s.dtype) for s in ss], 3 * n, name=name)


def _share_halves(ts, *, name):
    n = len(ts)

    def body(*refs):
        ins, outs, (send_sems, recv_sems) = refs[:n], refs[n:2 * n], refs[2 * n:]
        x, y, c = _mesh_pos()
        cps = [pltpu.make_async_remote_copy(
            src_ref=ins[a], dst_ref=outs[a].at[c], send_sem=send_sems.at[a], recv_sem=recv_sems.at[a],
            device_id=(x, y, 1 - c), device_id_type=MESH) for a in range(n)]
        for cp in cps:
            cp.start()
        for a in range(n):
            pltpu.make_async_remote_copy(
                src_ref=ins[a], dst_ref=outs[a].at[1 - c], send_sem=send_sems.at[a], recv_sem=recv_sems.at[a],
                device_id=(x, y, 1 - c), device_id_type=MESH).wait_recv()
        for cp in cps:
            cp.wait_send()

    return _hbm_call(body, ts, [jax.ShapeDtypeStruct((2,) + t.shape, t.dtype) for t in ts], n, name=name)


def _half_add(g, ra, core, *, name):
    _, R, C = g.shape
    tr = _pick_rows(R, max(8, 2 * 1024 * 1024 // (4 * C)))

    def body(core_ref, g_ref, ra_ref, o_ref):
        o_ref[...] = (g_ref[...] + ra_ref[...]).astype(o_ref.dtype)

    return pl.pallas_call(
        body, out_shape=jax.ShapeDtypeStruct((R, C), BF16),
        grid_spec=pltpu.PrefetchScalarGridSpec(
            num_scalar_prefetch=1, grid=(R // tr,),
            in_specs=[pl.BlockSpec((None, tr, C), lambda i, cr: (cr[0], i, 0)),
                      pl.BlockSpec((tr, C), lambda i, cr: (i, 0))],
            out_specs=pl.BlockSpec((tr, C), lambda i, cr: (i, 0))),
        compiler_params=_cp(("parallel",)), name=name,
    )(core.reshape(1).astype(jnp.int32), g, ra)


def _sum4(rb, *, name):
    _, R, C = rb.shape
    tr = _pick_rows(R, max(8, 2 * 1024 * 1024 // (4 * C)))

    def body(r0, r1, r2, r3, o_ref):
        f = lambda r: r[...].astype(F32)
        o_ref[...] = ((f(r0) + f(r1)) + f(r2)) + f(r3)

    return pl.pallas_call(
        body, out_shape=jax.ShapeDtypeStruct((R, C), F32), grid=(R // tr,),
        in_specs=[pl.BlockSpec((None, tr, C), lambda i, k=k: (k, i, 0)) for k in range(N_CHIPS)],
        out_specs=pl.BlockSpec((tr, C), lambda i: (i, 0)),
        compiler_params=_cp(("parallel",)), name=name,
    )(rb, rb, rb, rb)


TR = 256
S5_CHUNK = 256


def _rms_fwd(x, g, name):
    return _rowwise(_f_rms, [x], [g], (BF16,), tr=TR, name=name)[0]


def _rms_bwd(x, g, dh, gx, name):
    (dx, dxb), (dg,) = _rowwise_vjp(_f_rms, [x], [g], [dh], [(F32, BF16)], adds={0: gx}, tr=TR, name=name)
    return dx, dxb, dg


def _grad_cols(M, Nq):
    def imap(tm, tn):
        hp, per = (M // 2) // tm, Nq // tn
        assert hp * tm * 2 == M and per * tn == Nq, (M, Nq, tm, tn)
        return lambda i, j, k: (i // hp, j // per, i % hp, j % per)
    return (2, N_CHIPS, M // 2, Nq), lambda tm, tn: (None, None, tm, tn), imap, None, M // 2, Nq


def _grad_rows(Mq, N):
    def imap(tm, tn):
        po, hp = Mq // tm, (Mq // 2) // tm
        assert hp * tm * 2 == Mq, (Mq, tm)
        return lambda i, j, k: ((i % po) // hp, i // po, (i % po) % hp, j)
    return (2, N_CHIPS, Mq // 2, N), lambda tm, tn: (None, None, tm, tn), imap, None, Mq // 2, N


def _grad_layer_cols(slot, lh, M, Nq, buf):
    def imap(tm, tn):
        per = Nq // tn
        return lambda i, j, k: (j // per, slot, i, j % per)
    return (N_CHIPS, lh, M, Nq), lambda tm, tn: (None, None, tm, tn), imap, buf, M, Nq


def _grad_layer_rows(slot, lh, Mq, N, buf):
    def imap(tm, tn):
        po = Mq // tm
        return lambda i, j, k: (i // po, slot, i % po, j)
    return (N_CHIPS, lh, Mq, N), lambda tm, tn: (None, None, tm, tn), imap, buf, Mq, N


def _mlp_fwd(x, g, w_in, w_out, li):
    h2 = _rms_fwd(x, g, f"mlp_rms_{li}")
    r = _mm(h2, w_in, out_dtypes=(BF16,), epi=lambda acc: (jnp.maximum(acc, 0.0),), name=f"mlp_in_{li}")
    x_out = _mm(r, w_out, pro_a=lambda t: t * t, epi=lambda acc, res: (acc + res,), extras=(x,),
                name=f"mlp_out_{li}")
    return x_out, (h2, r)


def _mlp_bwd(gx, gxb, x, g, w_in, w_out, saved, li, nl, bufs):
    h2, r = saved
    D, F = w_in.shape
    lh = nl // 2
    da = _mm(gxb, w_out, tb=True, out_dtypes=(BF16,),
             epi=lambda acc, rt: (acc * 2.0 * rt.astype(F32),), extras=(r,), name=f"mlp_dact_{li}")
    buf_in, buf_out = bufs if bufs is not None else (None, None)
    d_w_out = _mm(r, gxb, ta=True, pro_a=lambda t: t * t, tm=512, tn=1024, tk=2048, out_dtypes=(BF16,),
                  out=_grad_layer_rows(li % lh, lh, F // N_CHIPS, D, buf_out), name=f"mlp_dwout_{li}")
    d_w_in = _mm(h2, da, ta=True, tm=1024, tn=1024, tk=2048, out_dtypes=(BF16,),
                 out=_grad_layer_cols(li % lh, lh, D, F // N_CHIPS, buf_in), name=f"mlp_dwin_{li}")
    dh2 = _mm(da, w_in, tb=True, name=f"mlp_dh_{li}")
    gx_mid, gxb_mid, dg = _rms_bwd(x, g, dh2, gx, f"mlp_rms_bwd_{li}")
    return gx_mid, gxb_mid, dg, (d_w_in, d_w_out)


def _local_step(x3, tgt3, p, layer_weights, token=None, grads_done=lambda group: None):
    B, S, D = x3.shape
    T = B * S
    x = x3.reshape(T, D)
    grads = {}
    row = lambda v: v.reshape(1, -1)
    p = dict(p)
    nl = p["norm_mlp"].shape[0]
    mlp_in, mlp_out = [None] * nl, [None] * nl

    def fetch(li, after):
        wl = dict(layer_weights(li, after))
        mlp_in[li], mlp_out[li] = wl.pop("mlp_w_in"), wl.pop("mlp_w_out")
        p.update(wl)

    g0 = row(p["norm_mix"][0])
    if token is not None:
        g0 = g0 + token[:1, :1]
    h0 = _rms_fwd(x, g0, "rms_mix_0")
    s5_args = (p["ssm_a_re"][0], p["ssm_a_im"][0], p["ssm_b_re"][0], p["ssm_b_im"][0],
               p["ssm_c_re"][0], p["ssm_c_im"][0], p["ssm_log_dt"][0])
    s5_exp, s5_vjp = jax.vjp(_s5_prep, *s5_args)
    abr, abi, bre, bim, cre, cim = s5_exp
    bre_b, bim_b, cre_b, cim_b = (t.astype(BF16) for t in (bre, bim, cre, cim))
    d_skip = p["ssm_d"]
    ypre, yb, sxr, sxi, ser, sei = _s5_fwd(h0, abr, abi, bre_b, bim_b, cre_b, cim_b, d_skip, B=B, L=S5_CHUNK,
                                           name="s5_fwd")
    fetch(0, yb)
    w_glu = p["ssm_w_glu"]
    z0 = _mm(yb, w_glu, name="s5_glu_mm")
    x_mid0 = _rowwise(lambda z, xr: (_f_glu(z)[0] + xr,), [z0, x], [], (F32,), tr=TR, name="s5_glu")[0]
    x1, mlp_saved0 = _mlp_fwd(x_mid0, row(p["norm_mlp"][0]), mlp_in[0], mlp_out[0], 0)

    g1 = row(p["norm_mix"][1])
    h1 = _rms_fwd(x1, g1, "rms_mix_1")
    fetch(1, h1)
    z1 = _mm(h1, p["conv_w_pw1"], name="conv_pw1")
    zg = _rowwise(_f_bias_glu, [z1], [p["conv_b_pw1"]], (F32,), tr=TR, name="conv_glu")[0]
    zp = jnp.pad(zg.reshape(B, S, D), ((0, 0), (CONV_HALO, 0), (0, 0)))
    w_dw = jnp.pad(p["conv_w_dw"], ((0, 32 - CONV_WIDTH), (0, 0)))
    yc = _conv_fwd(zp, w_dw, R=256, tc=128, name="conv_dw").reshape(T, D)
    ln_par = [p["conv_b_dw"], p["conv_ln_g"], p["conv_ln_b"]]
    qc = _rowwise(_f_ln_silu, [yc], ln_par, (BF16,), tr=TR, name="conv_ln_silu")[0]
    x_mid1 = _mm(qc, p["conv_w_pw2"], epi=lambda acc, bias, res: (acc + bias + res,),
                 extras=(p["conv_b_pw2"], x1), name="conv_pw2")
    x2, mlp_saved1 = _mlp_fwd(x_mid1, row(p["norm_mlp"][1]), mlp_in[1], mlp_out[1], 1)

    g2 = row(p["norm_mix"][2])
    h2 = _rms_fwd(x2, g2, "rms_mix_2")
    fetch(2, h2)
    z2 = _mm(h2, p["gmlp_w_in"], name="gmlp_in")
    gl_par = [p["gmlp_ln_g"], p["gmlp_ln_b"]]
    gu, gvn = _rowwise(_f_gelu_ln, [z2], gl_par, (F32, F32), tr=TR, name="gmlp_gelu_ln")
    causal = jnp.tril(jnp.ones((GMLP_CHUNK, GMLP_CHUNK), dtype=bool))
    ws_b = jnp.where(causal[None], p["gmlp_w_s"][0], 0.0).astype(BF16)
    bcol = jnp.pad(p["gmlp_b_s"][0].T, ((0, 0), (0, 128 - GMLP_HEADS)))
    uv = _gmlp_fwd(gu, gvn, ws_b, bcol, nck=4, name="gmlp_spatial")
    x_mid2 = _mm(uv, p["gmlp_w_out"], epi=lambda acc, res: (acc + res,), extras=(x2,), name="gmlp_out")
    x3_, mlp_saved2 = _mlp_fwd(x_mid2, row(p["norm_mlp"][2]), mlp_in[2], mlp_out[2], 2)

    g3 = row(p["norm_mix"][3])
    h3 = _rms_fwd(x3_, g3, "rms_mix_3")
    fetch(3, h3)
    qkv = _mm(h3, p["attn_w_qkv"], out_dtypes=(BF16,), tn=1152, name="attn_qkv")
    ng = len(ATT_DILS)
    qkv4 = qkv.reshape(T, 3, ng, ATT_W)
    att_in, o_tok, l_tok, lses = [], [], [], []
    for gi, dil in enumerate(ATT_DILS):
        if dil == 1:
            arr, offs = qkv, tuple((i * ng + gi) * PAIRS for i in range(3))
        else:
            arr, offs = _deinterleave(qkv4[:, :, gi].reshape(T, 3 * ATT_W), B, S, dil), (0, PAIRS, 2 * PAIRS)
        att_in.append((arr, offs))
        og, lg = _att_fwd(arr, offs, nb=S // dil // ATT_BLK, nbk=8, name=f"attn_fwd_{gi}")
        lses.append(lg)
        o_tok.append(_interleave(og, B, S, dil))
        l_tok.append(_stats_to_tokens(lg, B, S, dil))
    merged2 = _rowwise(_f_merge, o_tok + l_tok, [], (BF16,), tr=TR, name="attn_merge")[0]
    x_mid3 = _mm(merged2, p["attn_w_o"], epi=lambda acc, res: (acc + res,), extras=(x3_,), name="attn_out")
    x4, mlp_saved3 = _mlp_fwd(x_mid3, row(p["norm_mlp"][3]), mlp_in[3], mlp_out[3], 3)

    loss_part, gx, gxb, dgf = _loss_head(x4, tgt3.reshape(T, D), row(p["norm_final"]), tr=TR, name="loss_head")
    grads["norm_final"] = dgf.reshape(-1)
    d_norm_mix, d_norm_mlp = [None] * 4, [None] * 4
    Dq = D // N_CHIPS

    gx, gxb, d_norm_mlp[3], mlp_hi = _mlp_bwd(
        gx, gxb, x_mid3, row(p["norm_mlp"][3]), mlp_in[3], mlp_out[3], mlp_saved3, 3, nl, None)
    dmerged = _mm(gxb, p["attn_w_o"], tb=True, name="attn_dmerged")
    grads["attn_w_o"] = _mm(merged2, gxb, ta=True, tm=256, tn=256, tk=2048, out_dtypes=(BF16,), out=_grad_cols(ATT_W, Dq), name="attn_dwo")
    dml, _ = _rowwise_vjp(_f_merge, o_tok + l_tok, [], [dmerged], [F32] * 6, tr=TR, name="attn_merge_bwd")
    pieces = [[None] * ng for _ in range(3)]
    for gi, dil in enumerate(ATT_DILS):
        arr, offs = att_in[gi]
        dqkv_g = _att_bwd(arr, offs, _deinterleave(dml[gi], B, S, dil), lses[gi],
                          _stats_from_tokens(dml[ng + gi], B, S, dil), nb=S // dil // ATT_BLK, nbk=8,
                          name=f"attn_bwd_{gi}")
        for i in range(3):
            pieces[i][gi] = _interleave(dqkv_g[i], B, S, dil)
    dqkv = jnp.concatenate([pieces[i][gi] for i in range(3) for gi in range(ng)], axis=1)
    qkv_w = 3 * ng * ATT_W
    grads["attn_w_qkv"] = _mm(h3, dqkv, ta=True, tm=512, tn=1152, tk=2048, out_dtypes=(BF16,), out=_grad_cols(D, qkv_w // N_CHIPS),
                              name="attn_dwqkv")
    grads_done({n: grads[n] for n in ("attn_w_qkv", "attn_w_o")})
    dh3 = _mm(dqkv, p["attn_w_qkv"], tb=True, tk=1152, name="attn_dh")
    gx, gxb, d_norm_mix[3] = _rms_bwd(x3_, g3, dh3, gx, "rms_mix_bwd_3")

    gx, gxb, d_norm_mlp[2], mlp_hi = _mlp_bwd(
        gx, gxb, x_mid2, row(p["norm_mlp"][2]), mlp_in[2], mlp_out[2], mlp_saved2, 2, nl, mlp_hi)
    grads_done({"mlp_w_in": (1, mlp_hi[0]), "mlp_w_out": (1, mlp_hi[1])})
    duv = _mm(gxb, p["gmlp_w_out"], tb=True, name="gmlp_duv")
    grads["gmlp_w_out"] = _mm(uv, gxb, ta=True, tm=128, tn=1024, tk=2048, out_dtypes=(BF16,), out=_grad_rows(Dq, D), name="gmlp_dwout")
    du, dvn, dws, dbcol = _gmlp_bwd(duv, gu, gvn, ws_b, bcol, nck=4, name="gmlp_spatial_bwd")
    grads["gmlp_w_s"] = jnp.where(causal[None], dws, 0.0)[None]
    grads["gmlp_b_s"] = dbcol[:, :GMLP_HEADS].T[None]
    (dz2,), (dlg, dlb_) = _rowwise_vjp(_f_gelu_ln, [z2], gl_par, [du, dvn], [BF16], tr=TR, name="gmlp_gelu_ln_bwd")
    grads["gmlp_ln_g"], grads["gmlp_ln_b"] = dlg, dlb_
    grads["gmlp_w_in"] = _mm(h2, dz2, ta=True, tm=512, tn=512, tk=2048, out_dtypes=(BF16,), out=_grad_cols(D, 2 * Dq), name="gmlp_dwin")
    grads_done({n: grads[n] for n in ("gmlp_w_in", "gmlp_w_out")})
    dh2 = _mm(dz2, p["gmlp_w_in"], tb=True, name="gmlp_dh")
    gx, gxb, d_norm_mix[2] = _rms_bwd(x2, g2, dh2, gx, "rms_mix_bwd_2")

    gx, gxb, d_norm_mlp[1], mlp_lo = _mlp_bwd(
        gx, gxb, x_mid1, row(p["norm_mlp"][1]), mlp_in[1], mlp_out[1], mlp_saved1, 1, nl, None)
    dqc = _mm(gxb, p["conv_w_pw2"], tb=True, name="conv_dq")
    grads["conv_w_pw2"] = _mm(qc, gxb, ta=True, tm=128, tn=1024, tk=2048, out_dtypes=(BF16,), out=_grad_rows(Dq, D), name="conv_dwpw2")
    _, (db2,) = _rowwise_vjp(lambda t, b: (t + b,), [gx], [p["conv_b_pw2"]], [gx], [None], tr=TR, name="conv_db2")
    grads["conv_b_pw2"] = db2
    (dyc,), (dbdw, dcg, dcb) = _rowwise_vjp(_f_ln_silu, [yc], ln_par, [dqc], [F32], tr=TR, name="conv_ln_silu_bwd")
    grads["conv_b_dw"], grads["conv_ln_g"], grads["conv_ln_b"] = dbdw, dcg, dcb
    dyp = jnp.pad(dyc.reshape(B, S, D), ((0, 0), (0, CONV_HALO), (0, 0)))
    dzg, dwdw = _conv_bwd(zp, dyp, w_dw, R=256, tc=128, name="conv_dw_bwd")
    grads["conv_w_dw"] = dwdw[:CONV_WIDTH][None]
    (dz1,), (db1,) = _rowwise_vjp(_f_bias_glu, [z1], [p["conv_b_pw1"]], [dzg.reshape(T, D)], [BF16], tr=TR,
                                  name="conv_glu_bwd")
    grads["conv_b_pw1"] = db1
    grads["conv_w_pw1"] = _mm(h1, dz1, ta=True, tm=512, tn=512, tk=2048, out_dtypes=(BF16,), out=_grad_cols(D, 2 * Dq), name="conv_dwpw1")
    grads_done({n: grads[n] for n in ("conv_w_pw1", "conv_w_pw2")})
    dh1 = _mm(dz1, p["conv_w_pw1"], tb=True, name="conv_dh")
    gx, gxb, d_norm_mix[1] = _rms_bwd(x1, g1, dh1, gx, "rms_mix_bwd_1")

    gx, gxb, d_norm_mlp[0], mlp_lo = _mlp_bwd(
        gx, gxb, x_mid0, row(p["norm_mlp"][0]), mlp_in[0], mlp_out[0], mlp_saved0, 0, nl, mlp_lo)
    grads_done({"mlp_w_in": (0, mlp_lo[0]), "mlp_w_out": (0, mlp_lo[1])})
    (dz0,), _ = _rowwise_vjp(_f_glu, [z0], [], [gx], [BF16], tr=TR, name="s5_glu_bwd")
    grads["ssm_w_glu"] = _mm(yb, dz0, ta=True, tm=512, tn=512, tk=2048, out_dtypes=(BF16,), out=_grad_cols(D, 2 * Dq), name="s5_dwglu")
    grads_done({"ssm_w_glu": grads["ssm_w_glu"]})
    dypre = _mm(dz0, w_glu, tb=True, epi=lambda acc, yp: (jax.vjp(lambda t: jax.nn.gelu(t), yp)[1](acc)[0],),
                extras=(ypre,), name="s5_dypre")
    dh0, dbre, dbim, dcre, dcim, dabr, dabi, dd = _s5_bwd(
        dypre, h0, sxr, sxi, ser, sei, abr, abi, bre_b, bim_b, cre_b, cim_b, d_skip, B=B, L=S5_CHUNK, name="s5_bwd")
    s5_grads = s5_vjp((dabr, dabi, dbre, dbim, dcre, dcim))
    for nm, gv in zip(("ssm_a_re", "ssm_a_im", "ssm_b_re", "ssm_b_im", "ssm_c_re", "ssm_c_im", "ssm_log_dt"), s5_grads):
        grads[nm] = gv[None]
    grads["ssm_d"] = dd
    gx, _, d_norm_mix[0] = _rms_bwd(x, g0, dh0, gx, "rms_mix_bwd_0")

    grads["norm_mix"] = jnp.concatenate(d_norm_mix, axis=0)
    grads["norm_mlp"] = jnp.concatenate(d_norm_mlp, axis=0)
    grads["mlp_w_in"], grads["mlp_w_out"] = (mlp_lo[0], mlp_hi[0]), (mlp_lo[1], mlp_hi[1])
    return loss_part, gx.reshape(B, S, D), grads


WEIGHTS = ['norm_mix', 'norm_mlp', 'norm_final', 'ssm_a_re', 'ssm_a_im', 'ssm_b_re', 'ssm_b_im', 'ssm_c_re',
           'ssm_c_im', 'ssm_d', 'ssm_log_dt', 'ssm_w_glu', 'conv_w_pw1', 'conv_b_pw1', 'conv_w_dw', 'conv_b_dw',
           'conv_ln_g', 'conv_ln_b', 'conv_w_pw2', 'conv_b_pw2', 'gmlp_w_in', 'gmlp_ln_g', 'gmlp_ln_b', 'gmlp_w_s',
           'gmlp_b_s', 'gmlp_w_out', 'attn_w_qkv', 'attn_w_o', 'mlp_w_in', 'mlp_w_out']
BIG_AXIS = {'ssm_w_glu': -1, 'conv_w_pw1': -1, 'conv_w_pw2': -2, 'gmlp_w_in': -1, 'gmlp_w_out': -2,
            'attn_w_qkv': -1, 'attn_w_o': -1, 'mlp_w_in': -1, 'mlp_w_out': -2}
BIG = list(BIG_AXIS)
LAYER_MIXER_WEIGHTS = (('ssm_w_glu',), ('conv_w_pw1', 'conv_w_pw2'), ('gmlp_w_in', 'gmlp_w_out'), ('attn_w_qkv', 'attn_w_o'))
SMALL_SHARDED = ['conv_b_pw1', 'conv_w_dw', 'conv_b_dw', 'conv_ln_g', 'conv_ln_b', 'conv_b_pw2', 'gmlp_ln_g', 'gmlp_ln_b']
SMALL_REPL = [n for n in WEIGHTS if n not in BIG_AXIS and n not in SMALL_SHARDED]
SMALL = SMALL_REPL + SMALL_SHARDED
LANES = 128
FLAT_COLS = 1024


def _pack(arrs, cols, row_mult):
    flat = jnp.concatenate([a.reshape(-1) for a in arrs])
    per = cols * row_mult
    n = -(-flat.shape[0] // per) * per
    return jnp.pad(flat, (0, n - flat.shape[0])).reshape(n // cols, cols)


def _unpack(flat2d, shapes):
    flat = flat2d.reshape(-1)
    out, off = [], 0
    for s in shapes:
        n = int(np.prod(s))
        out.append(flat[off:off + n].reshape(s))
        off += n
    return out


def _as_halves(shard):
    if shard.shape[0] == 1:
        shard = shard[0]
    return shard.reshape((2, shard.shape[0] // 2) + shard.shape[1:])


def _stored_weight(name, arr):
    kind = "cols" if BIG_AXIS[name] == -1 else "rows"
    if arr.shape[1] > 1:
        return [_Stored(arr, kind, lead=(li,)) for li in range(arr.shape[1])]
    arr = arr[:, 0]
    if kind == "rows":
        return arr.reshape(-1, arr.shape[-1])
    return _Stored(arr, kind)


def kernel(x, norm_mix, norm_mlp, norm_final, ssm_a_re, ssm_a_im, ssm_b_re, ssm_b_im, ssm_c_re, ssm_c_im, ssm_d, ssm_log_dt, ssm_w_glu, conv_w_pw1, conv_b_pw1, conv_w_dw, conv_b_dw, conv_ln_g, conv_ln_b, conv_w_pw2, conv_b_pw2, gmlp_w_in, gmlp_ln_g, gmlp_ln_b, gmlp_w_s, gmlp_b_s, gmlp_w_out, attn_w_qkv, attn_w_o, mlp_w_in, mlp_w_out, loss_target, m_norm_mix, m_norm_mlp, m_norm_final, m_ssm_a_re, m_ssm_a_im, m_ssm_b_re, m_ssm_b_im, m_ssm_c_re, m_ssm_c_im, m_ssm_d, m_ssm_log_dt, m_ssm_w_glu, m_conv_w_pw1, m_conv_b_pw1, m_conv_w_dw, m_conv_b_dw, m_conv_ln_g, m_conv_ln_b, m_conv_w_pw2, m_conv_b_pw2, m_gmlp_w_in, m_gmlp_ln_g, m_gmlp_ln_b, m_gmlp_w_s, m_gmlp_b_s, m_gmlp_w_out, m_attn_w_qkv, m_attn_w_o, m_mlp_w_in, m_mlp_w_out, v_norm_mix, v_norm_mlp, v_norm_final, v_ssm_a_re, v_ssm_a_im, v_ssm_b_re, v_ssm_b_im, v_ssm_c_re, v_ssm_c_im, v_ssm_d, v_ssm_log_dt, v_ssm_w_glu, v_conv_w_pw1, v_conv_b_pw1, v_conv_w_dw, v_conv_b_dw, v_conv_ln_g, v_conv_ln_b, v_conv_w_pw2, v_conv_b_pw2, v_gmlp_w_in, v_gmlp_ln_g, v_gmlp_ln_b, v_gmlp_w_s, v_gmlp_b_s, v_gmlp_w_out, v_attn_w_qkv, v_attn_w_o, v_mlp_w_in, v_mlp_w_out):
    args = dict(locals())
    w = {n: args[n] for n in WEIGHTS}
    m = {n: args["m_" + n] for n in WEIGHTS}
    v = {n: args["v_" + n] for n in WEIGHTS}
    chip = 2 * lax.axis_index("x") + lax.axis_index("y")
    core = lax.axis_index("c")

    big_shapes = [w[n].shape for n in BIG]
    started, token = [], None
    for li, mixer in enumerate(LAYER_MIXER_WEIGHTS):
        names = list(mixer) + ["mlp_w_in", "mlp_w_out"]
        shards = [w[n][0] for n in mixer] + [w["mlp_w_in"][li], w["mlp_w_out"][li]]
        halves = [s.astype(BF16).reshape((2, s.shape[0] // 2) + s.shape[1:]) for s in shards]
        send_sems, recv_sems, halves, lands, token = _gather_start(halves, token, name=f"gather_start_{li}")
        started.append((names, (send_sems, recv_sems, halves, lands)))

    def layer_weights(li, after):
        names, st = started[li]
        halves, lands = _gather_wait(st, after, name=f"gather_wait_{li}")
        out = {}
        for n, h, arr in zip(names, halves, lands):
            arr = lax.dynamic_update_index_in_dim(arr, h, chip, axis=0)
            arr = arr.reshape((N_CHIPS, arr.shape[1] * arr.shape[2]) + arr.shape[3:])
            if BIG_AXIS[n] == -1:
                out[n] = _Stored(arr, "cols")
            else:
                out[n] = arr.reshape(-1, arr.shape[-1])
        return out

    p = {}
    sm_shapes = [w[n].shape for n in SMALL_SHARDED]
    sflat = _pack([w[n] for n in SMALL_SHARDED], LANES, 8)
    rs = sflat.shape[0]
    sall = _allgather8(sflat, name="gather_small").reshape(8, rs, LANES)
    per_chip = [_unpack(sall[2 * k], sm_shapes) for k in range(N_CHIPS)]
    for i, n in enumerate(SMALL_SHARDED):
        p[n] = jnp.concatenate([per_chip[k][i] for k in range(N_CHIPS)], axis=-1)
    for n in SMALL_REPL:
        p[n] = w[n]
    p['conv_w_dw'] = p['conv_w_dw'][0]

    in_flight, arrived, n_rounds = [], {}, [0]

    def finish_round(after):
        k, names, plans, st = in_flight.pop(0)
        srcs, lands = _reduce_wait(st, plans, after, name=f"grads_wait_{k}")
        for n, plan, src, land in zip(names, plans, srcs, lands):
            arrived.setdefault(n, []).append((plan, src, land))

    def grads_done(group):
        names, srcs, plans, lands = [], [], [], []
        for n, v in group.items():
            if isinstance(v, tuple):
                src, plan = v[1].reshape(1, N_CHIPS, -1, v[1].shape[-1]), ((0, v[0]),)
                while any(n in rd[1] for rd in in_flight):
                    finish_round(src)
            else:
                src, plan = v.reshape(2, N_CHIPS, -1, v.shape[-1]), ((0, 0), (1, 1))
            land = arrived[n][-1][2] if n in arrived else lax.empty((2 * N_CHIPS,) + src.shape[2:], BF16)
            names.append(n), srcs.append(src), plans.append(plan), lands.append(land)
        st = _reduce_start(srcs, lands, plans, None, name=f"grads_start_{n_rounds[0]}")
        in_flight.append((n_rounds[0], names, plans, st[:4]))
        n_rounds[0] += 1

    loss_part, grad_x, g = _local_step(x, loss_target, p, layer_weights, token, grads_done)
    loss = lax.psum(loss_part[0, 0], ("x", "y", "c"))

    my_id = 2 * chip + core
    while in_flight:
        finish_round(grad_x)
    totals = []
    for n in BIG:
        own = None
        for plan, src, land in arrived[n]:
            slab = lax.dynamic_index_in_dim(src, chip, axis=1, keepdims=False)
            if len(plan) == 2:
                own = lax.dynamic_index_in_dim(slab, core, axis=0, keepdims=False)
            else:
                own = slab[0] if own is None else jnp.where(core == plan[0][1], slab[0], own)
        totals.append(_sum8(arrived[n][-1][2], own, my_id, name="owner_sum_" + n))
    shared = _share_halves(totals, name="grads_share_halves")
    big_grads = {}
    for n, arr, t in zip(BIG, shared, totals):
        arr = lax.dynamic_update_index_in_dim(arr, t[None], core, axis=0)
        big_grads[n] = arr.reshape(w[n].shape)

    small_full_shapes = [g[n].shape for n in SMALL]
    gs = _pack([g[n] for n in SMALL], LANES, 8)
    rg = gs.shape[0]
    gs_all = _allgather8(gs, name="gather_small_grads").reshape(8, rg, LANES)
    gs_sum = _rowwise(lambda *a: (functools.reduce(lambda s, t: s + t, a),), [gs_all[k] for k in range(8)], [], (F32,),
                      tr=rg, name="small_grads_sum")[0]
    small_grads = dict(zip(SMALL, _unpack(gs_sum, small_full_shapes)))
    for n in SMALL:
        small_grads[n] = small_grads[n].reshape(p_shape_full(w[n], -1 if n in SMALL_SHARDED else None))
    for n in SMALL_SHARDED:
        width = w[n].shape[-1]
        small_grads[n] = lax.dynamic_slice_in_dim(small_grads[n], chip * width, width, axis=-1)

    grad, delta, new_m, new_v = {}, {}, {}, {}
    for n in BIG:
        shape = w[n].shape
        two_d = lambda t: t.reshape(-1, shape[-1])
        grad[n] = big_grads[n]
        d_, m_, v_ = _adamw(two_d(w[n]), two_d(grad[n]), two_d(m[n]), two_d(v[n]), name="adamw_" + n)
        delta[n], new_m[n], new_v[n] = d_.reshape(shape), m_.reshape(shape), v_.reshape(shape)
    sm_own_shapes = [w[n].shape for n in SMALL]
    packed = [_pack([src[n] for n in SMALL], LANES, 8) for src in (w, small_grads, m, v)]
    outs = _adamw(*packed, name="adamw_small")
    for dst, flat in zip((delta, new_m, new_v), outs):
        dst.update(dict(zip(SMALL, _unpack(flat, sm_own_shapes))))
    for n in SMALL:
        grad[n] = small_grads[n]

    return (loss, grad_x, *[grad[n] for n in WEIGHTS], *[delta[n] for n in WEIGHTS],
            *[new_m[n] for n in WEIGHTS], *[new_v[n] for n in WEIGHTS])


def p_shape_full(shard, axis):
    s = list(shard.shape)
    if axis is not None:
        s[axis] *= N_CHIPS
    return tuple(s)
```

```python
import functools
import math

import jax
import jax.numpy as jnp
import numpy as np
from jax import lax
from jax.experimental import pallas as pl
from jax.experimental.pallas import tpu as pltpu

F32 = jnp.float32
BF16 = jnp.bfloat16
MESH = pl.DeviceIdType.MESH

EPS = 1e-6
SSM_GROUP = 16
SSM_STATE = 64
CONV_WIDTH = 31
CONV_HALO = 32
GMLP_CHUNK = 128
GMLP_HEADS = 4
ATT_DILS = (1, 4, 16)
ATT_BLK = 128
ATT_HEADS = 8
HEAD_DIM = 64
ATT_W = ATT_HEADS * HEAD_DIM
N_CHIPS = 4
ADAM_LR, ADAM_B1, ADAM_B2, ADAM_EPS, ADAM_WD, ADAM_STEP = 1e-3, 0.9, 0.999, 1e-8, 0.01, 10

VMEM_BYTES_V7X = 64 * 1024 * 1024
VMEM_LIMIT = VMEM_BYTES_V7X - 8 * 1024 * 1024
MASK_VALUE = -1e30
LANE_TILE = 128


def _cp(sem=None):
    return pltpu.CompilerParams(dimension_semantics=sem, vmem_limit_bytes=VMEM_LIMIT)


def _pick_tile(total, target):
    for cand in range(min(target, total) // LANE_TILE * LANE_TILE, 0, -LANE_TILE):
        if total % cand == 0:
            return cand
    return total


class _Stored:
    def __init__(self, arr, kind="plain", lead=()):
        self.arr, self.kind, self.lead = arr, kind, tuple(lead)
        r, c = arr.shape[-2:]
        self.shape = (r, c * N_CHIPS) if kind == "cols" else (r * N_CHIPS, c) if kind == "rows" else (r, c)

    def spec(self, br, bc, rc_of):
        lead, nl = self.lead, len(self.lead)
        if self.kind == "plain":
            return pl.BlockSpec((None,) * nl + (br, bc), lambda i, j, k: (*lead, *rc_of(i, j, k)))
        if self.kind == "cols":
            per = self.arr.shape[-1] // bc
            assert per * bc == self.arr.shape[-1]

            def imap(i, j, k):
                r, c = rc_of(i, j, k)
                return (c // per, *lead, r, c % per)
        else:
            per = self.arr.shape[-2] // br
            assert per * br == self.arr.shape[-2]

            def imap(i, j, k):
                r, c = rc_of(i, j, k)
                return (r // per, *lead, r % per, c)
        return pl.BlockSpec((None,) * (nl + 1) + (br, bc), imap)


def _mm(a, b, *, ta=False, tb=False, out_dtypes=(F32,), tm=1024, tn=1024, tk=1024,
        pro_a=None, pro_b=None, epi=None, extras=(), out=None, after=None, name):
    if ta:
        K, M = a.shape
    else:
        M, K = a.shape
    if not isinstance(b, _Stored):
        b = _Stored(b)
    N, Kb = b.shape if tb else b.shape[::-1]
    assert K == Kb, (a.shape, b.shape, ta, tb)
    n_unit = b.arr.shape[-1] if (b.kind == "cols" and not tb) else b.arr.shape[-2] if (b.kind == "rows" and tb) else N
    k_unit = b.arr.shape[-1] if (b.kind == "cols" and tb) else b.arr.shape[-2] if (b.kind == "rows" and not tb) else K
    m_unit = M
    if out is not None:
        m_unit, n_unit = out[4], math.gcd(n_unit, out[5])
    tm, tn, tk = _pick_tile(m_unit, tm), _pick_tile(n_unit, tn), _pick_tile(k_unit, tk)
    nk = K // tk
    a_spec = (pl.BlockSpec((tk, tm), lambda i, j, k: (k, i)) if ta
              else pl.BlockSpec((tm, tk), lambda i, j, k: (i, k)))
    b_spec = b.spec(tn, tk, lambda i, j, k: (j, k)) if tb else b.spec(tk, tn, lambda i, j, k: (k, j))
    ex_specs = []
    for e in extras:
        if e.shape[0] == 1:
            ex_specs.append(pl.BlockSpec((1, tn), lambda i, j, k: (0, j)))
        else:
            assert e.shape == (M, N), (e.shape, M, N)
            ex_specs.append(pl.BlockSpec((tm, tn), lambda i, j, k: (i, j)))
    dims = (((0 if ta else 1,), (1 if tb else 0,)), ((), ()))
    n_ex, n_out = len(extras), len(out_dtypes)
    direct = epi is None and n_out == 1 and out_dtypes[0] == F32
    use_acc = nk > 1 and not direct
    operands, aliases, alias_specs = [a, b.arr, *extras], {}, []
    if after is not None:
        operands.append(after)
        alias_specs.append(pl.BlockSpec(memory_space=pl.ANY))
    if out is None:
        out_specs = [pl.BlockSpec((tm, tn), lambda i, j, k: (i, j)) for _ in out_dtypes]
        out_shape = [jax.ShapeDtypeStruct((M, N), dt) for dt in out_dtypes]
    else:
        shape, block_fn, imap_fn, alias = out[:4]
        assert n_out == 1
        out_specs = [pl.BlockSpec(block_fn(tm, tn), imap_fn(tm, tn))]
        out_shape = [jax.ShapeDtypeStruct(shape, out_dtypes[0])]
        if alias is not None:
            operands.append(alias)
            aliases = {len(operands) - 1: 0}
            alias_specs.append(pl.BlockSpec(memory_space=pl.ANY))
    n_in = len(operands)

    def finish(r, ex, outs):
        res = epi(r, *[e[...] for e in ex]) if epi is not None else (r,)
        for o, v in zip(outs, res):
            o[...] = v.astype(o.dtype)

    def body(*refs):
        a_ref, b_ref = refs[:2]
        ex = refs[2:2 + n_ex]
        outs = refs[n_in:n_in + n_out]
        at, bt = a_ref[...], b_ref[...]
        if pro_a is not None:
            at = pro_a(at)
        if pro_b is not None:
            bt = pro_b(bt)
        part = lax.dot_general(at, bt, dims, preferred_element_type=F32)
        if nk == 1:
            finish(part, ex, outs)
            return
        acc = refs[-1] if use_acc else outs[0]
        k = pl.program_id(2)

        @pl.when(k == 0)
        def _():
            acc[...] = part

        @pl.when(k > 0)
        def _():
            acc[...] += part

        if use_acc:
            @pl.when(k == nk - 1)
            def _():
                finish(acc[...], ex, outs)

    res = pl.pallas_call(
        body, grid=(M // tm, N // tn, nk),
        in_specs=[a_spec, b_spec] + ex_specs + alias_specs,
        out_specs=out_specs, out_shape=out_shape,
        scratch_shapes=[pltpu.VMEM((tm, tn), F32)] if use_acc else [],
        input_output_aliases=aliases,
        compiler_params=_cp(("parallel", "parallel", "arbitrary")), name=name,
    )(*operands)
    return res[0] if n_out == 1 else res


def _to_bf16(t):
    return t.astype(BF16)


def _pick_rows(total, target):
    for cand in range(min(target, total) // 8 * 8, 0, -8):
        if total % cand == 0:
            return cand
    return total


def _rowwise(f, rows, params, out_dtypes, *, tr, name):
    T = rows[0].shape[0]
    tr = _pick_rows(T, tr)
    nr, npar = len(rows), len(params)
    blk = [jax.ShapeDtypeStruct((tr, r.shape[1]), F32) for r in rows]
    blk += [jax.ShapeDtypeStruct(p.shape, F32) for p in params]
    out_avals = jax.eval_shape(f, *blk)

    def body(*refs):
        res = f(*[r[...].astype(F32) for r in refs[:nr + npar]])
        for o, v in zip(refs[nr + npar:], res):
            o[...] = v.astype(o.dtype)

    out = pl.pallas_call(
        body, grid=(T // tr,),
        in_specs=[pl.BlockSpec((tr, r.shape[1]), lambda i: (i, 0)) for r in rows]
        + [pl.BlockSpec(p.shape, lambda i, nd=p.ndim: (0,) * nd) for p in params],
        out_specs=[pl.BlockSpec((tr, o.shape[1]), lambda i: (i, 0)) for o in out_avals],
        out_shape=[jax.ShapeDtypeStruct((T, o.shape[1]), dt) for o, dt in zip(out_avals, out_dtypes)],
        compiler_params=_cp(("parallel",)), name=name,
    )(*rows, *params)
    return out


def _rowwise_vjp(f, rows, params, cots, drow_dtypes, *, adds=None, after=None, tr, name):
    adds = adds or {}
    T = rows[0].shape[0]
    tr = _pick_rows(T, tr)
    nr, npar, nc = len(rows), len(params), len(cots)
    want, want_dt = [], []
    for i, dt in enumerate(drow_dtypes):
        for one in (dt if isinstance(dt, tuple) else (dt,)):
            if one is not None:
                want.append(i)
                want_dt.append(one)
    add_idx = sorted(set(i for i in want if i in adds))
    add_arrays = [adds[i] for i in add_idx]
    na = len(add_arrays)
    extra = [] if after is None else [after]

    def body(*refs):
        ins = [r[...].astype(F32) for r in refs[:nr + npar]]
        cvals = [r[...].astype(F32) for r in refs[nr + npar:nr + npar + nc]]
        avals = refs[nr + npar + nc:nr + npar + nc + na]
        outs = refs[nr + npar + nc + na + len(extra):]
        _, vjp = jax.vjp(f, *ins)
        grads = vjp(tuple(cvals))
        for o, i in zip(outs[:len(want)], want):
            g = grads[i]
            if i in adds:
                g = g + avals[add_idx.index(i)][...].astype(F32)
            o[...] = g.astype(o.dtype)
        step = pl.program_id(0)
        for o, g in zip(outs[len(want):], grads[nr:]):
            @pl.when(step == 0)
            def _(o=o):
                o[...] = jnp.zeros_like(o)
            o[...] += g

    rspec = lambda r: pl.BlockSpec((tr, r.shape[1]), lambda i: (i, 0))
    pspec = lambda p: pl.BlockSpec(p.shape, lambda i, nd=p.ndim: (0,) * nd)
    out = pl.pallas_call(
        body, grid=(T // tr,),
        in_specs=[rspec(r) for r in rows] + [pspec(p) for p in params] + [rspec(c) for c in cots]
        + [rspec(a) for a in add_arrays] + [pl.BlockSpec(memory_space=pl.ANY)] * len(extra),
        out_specs=[rspec(rows[i]) for i in want] + [pspec(p) for p in params],
        out_shape=[jax.ShapeDtypeStruct(rows[i].shape, dt) for i, dt in zip(want, want_dt)]
        + [jax.ShapeDtypeStruct(p.shape, F32) for p in params],
        compiler_params=_cp(("arbitrary",)), name=name,
    )(*rows, *params, *cots, *add_arrays, *extra)
    return out[:len(want)], out[len(want):]


def _f_rms(x, g):
    return (x * lax.rsqrt(jnp.mean(x * x, axis=-1, keepdims=True) + EPS) * g,)


def _ln(x, g, b):
    mu = jnp.mean(x, axis=-1, keepdims=True)
    var = jnp.mean(jnp.square(x - mu), axis=-1, keepdims=True)
    return (x - mu) * lax.rsqrt(var + EPS) * g + b


def _f_glu(z):
    d = z.shape[1] // 2
    return (z[:, :d] * jax.nn.sigmoid(z[:, d:]),)


def _f_bias_glu(z, b):
    return _f_glu(z + b)


def _f_ln_silu(y, b_dw, g, b):
    return (jax.nn.silu(_ln(y + b_dw, g, b)),)


def _f_gelu_ln(z, g, b):
    d = z.shape[1] // 2
    zz = jax.nn.gelu(z)
    return zz[:, :d], _ln(zz[:, d:], g, b)


def _f_gelu(y):
    return (jax.nn.gelu(y),)


def _f_merge(o0, o1, o2, l0, l1, l2):
    m = jnp.maximum(jnp.maximum(l0, l1), l2)
    e0, e1, e2 = jnp.exp(l0 - m), jnp.exp(l1 - m), jnp.exp(l2 - m)
    s = e0 + e1 + e2
    pair = 2 * HEAD_DIM
    first_head = lax.broadcasted_iota(jnp.int32, (o0.shape[0], pair), 1) < HEAD_DIM
    cols = []
    for hp in range(o0.shape[1] // pair):
        acc = None
        for o, e in ((o0, e0), (o1, e1), (o2, e2)):
            wgt = e / s
            wp = jnp.where(first_head, wgt[:, 2 * hp:2 * hp + 1], wgt[:, 2 * hp + 1:2 * hp + 2])
            term = wp * o[:, hp * pair:(hp + 1) * pair]
            acc = term if acc is None else acc + term
        cols.append(acc)
    return (jnp.concatenate(cols, axis=1),)


def _f_add(a, b):
    return (a + b,)


def _loss_head(x, tgt, g, *, tr, name):
    T, D = x.shape
    tr = min(tr, T)

    def f(xv, gv, tv):
        y = _f_rms(xv, gv)[0]
        return 0.5 * jnp.mean(jnp.square(y - tv), axis=-1, keepdims=True)

    def body(x_ref, t_ref, g_ref, loss_ref, dx_ref, dxb_ref, dg_ref):
        tv = t_ref[...]
        l, vjp = jax.vjp(lambda xv, gv: f(xv, gv, tv), x_ref[...], g_ref[...])
        dx, dg = vjp(jnp.ones_like(l))
        dx_ref[...] = dx
        dxb_ref[...] = dx.astype(BF16)

        @pl.when(pl.program_id(0) == 0)
        def _():
            loss_ref[...] = jnp.zeros_like(loss_ref)
            dg_ref[...] = jnp.zeros_like(dg_ref)

        loss_ref[...] += jnp.sum(l)
        dg_ref[...] += dg

    return pl.pallas_call(
        body, grid=(T // tr,),
        in_specs=[pl.BlockSpec((tr, D), lambda i: (i, 0)), pl.BlockSpec((tr, D), lambda i: (i, 0)),
                  pl.BlockSpec((1, D), lambda i: (0, 0))],
        out_specs=[pl.BlockSpec((1, 128), lambda i: (0, 0)), pl.BlockSpec((tr, D), lambda i: (i, 0)),
                   pl.BlockSpec((tr, D), lambda i: (i, 0)), pl.BlockSpec((1, D), lambda i: (0, 0))],
        out_shape=[jax.ShapeDtypeStruct((1, 128), F32), jax.ShapeDtypeStruct((T, D), F32),
                   jax.ShapeDtypeStruct((T, D), BF16), jax.ShapeDtypeStruct((1, D), F32)],
        compiler_params=_cp(("arbitrary",)), name=name,
    )(x, tgt, g)


def _adamw(w, g, m, v, *, name):
    R, C = w.shape
    tr = _pick_rows(R, max(8, 2 * 1024 * 1024 // (4 * C)))
    c1 = 1.0 - ADAM_B1 ** ADAM_STEP
    c2 = 1.0 - ADAM_B2 ** ADAM_STEP

    def body(w_ref, g_ref, m_ref, v_ref, d_ref, nm_ref, nv_ref):
        gv = g_ref[...]
        nm = ADAM_B1 * m_ref[...] + (1.0 - ADAM_B1) * gv
        nv = ADAM_B2 * v_ref[...] + (1.0 - ADAM_B2) * jnp.square(gv)
        nm_ref[...] = nm
        nv_ref[...] = nv
        d_ref[...] = -ADAM_LR * ((nm / c1) / (jnp.sqrt(nv / c2) + ADAM_EPS) + ADAM_WD * w_ref[...])

    spec = pl.BlockSpec((tr, C), lambda i: (i, 0))
    return pl.pallas_call(
        body, grid=(R // tr,), in_specs=[spec] * 4, out_specs=[spec] * 3,
        out_shape=[jax.ShapeDtypeStruct((R, C), F32)] * 3,
        compiler_params=_cp(("parallel",)), name=name,
    )(w, g, m, v)


def _s5_prep(a_re, a_im, b_re, b_im, c_re, c_im, log_dt):
    G, N = a_re.shape
    P = b_re.shape[2]
    gpb = 128 // P
    nblk = G // gpb
    dt = jnp.exp(log_dt)[:, None]
    mag = jnp.exp(a_re * dt)
    abr, abi = mag * jnp.cos(a_im * dt), mag * jnp.sin(a_im * dt)
    den = a_re * a_re + a_im * a_im
    nr, ni = abr - 1.0, abi
    qr, qi = (nr * a_re + ni * a_im) / den, (ni * a_re - nr * a_im) / den
    bbr = qr[..., None] * b_re - qi[..., None] * b_im
    bbi = qr[..., None] * b_im + qi[..., None] * b_re
    eye = jnp.eye(gpb, dtype=F32)

    def expand_b(t):
        t = t.reshape(nblk, gpb, N, P).transpose(0, 1, 3, 2)
        return (t[:, :, :, None, :] * eye[None, :, None, :, None]).reshape(nblk, gpb * P, gpb * N)

    def expand_c(t):
        t = t.reshape(nblk, gpb, P, N).transpose(0, 1, 3, 2)
        return (t[:, :, :, None, :] * eye[None, :, None, :, None]).reshape(nblk, gpb * N, gpb * P)

    return (abr.reshape(1, G * N), abi.reshape(1, G * N), expand_b(bbr), expand_b(bbi),
            expand_c(c_re), expand_c(c_im))


def _s5_fwd(h, abr, abi, bre, bim, cre, cim, d, *, B, L, name):
    T, D = h.shape
    S = T // B
    L = min(L, S)
    nc = S // L
    nblk, cb, sb = bre.shape
    GN = abr.shape[1]

    def body(h_ref, ar_ref, ai_ref, bre_ref, bim_ref, cre_ref, cim_ref, d_ref,
             y_ref, yb_ref, xr_ref, xi_ref, er_ref, ei_ref, sr, si, car, cai):
        ci = pl.program_id(1)

        @pl.when(ci == 0)
        def _():
            car[...] = jnp.zeros_like(car)
            cai[...] = jnp.zeros_like(cai)

        for j in range(nblk):
            u = h_ref[:, j * cb:(j + 1) * cb]
            sr[:, j * sb:(j + 1) * sb] = jnp.dot(u, bre_ref[j], preferred_element_type=F32)
            si[:, j * sb:(j + 1) * sb] = jnp.dot(u, bim_ref[j], preferred_element_type=F32)
        ar, ai = ar_ref[...], ai_ref[...]

        def step(t, carry):
            pr, pi = carry
            nr = ar * pr - ai * pi + sr[pl.ds(t, 1), :]
            ni = ar * pi + ai * pr + si[pl.ds(t, 1), :]
            sr[pl.ds(t, 1), :] = nr
            si[pl.ds(t, 1), :] = ni
            return nr, ni

        pr, pi = lax.fori_loop(0, L, step, (car[...], cai[...]), unroll=4)
        car[...] = pr
        cai[...] = pi
        er_ref[0] = pr
        ei_ref[0] = pi
        for j in range(nblk):
            xr = sr[:, j * sb:(j + 1) * sb].astype(BF16)
            xi = si[:, j * sb:(j + 1) * sb].astype(BF16)
            xr_ref[:, j * sb:(j + 1) * sb] = xr
            xi_ref[:, j * sb:(j + 1) * sb] = xi
            y = (jnp.dot(xr, cre_ref[j], preferred_element_type=F32)
                 - jnp.dot(xi, cim_ref[j], preferred_element_type=F32))
            u = h_ref[:, j * cb:(j + 1) * cb].astype(F32)
            y = y + d_ref[:, j * cb:(j + 1) * cb] * u
            y_ref[:, j * cb:(j + 1) * cb] = y
            yb_ref[:, j * cb:(j + 1) * cb] = jax.nn.gelu(y).astype(BF16)

    tok = lambda w: pl.BlockSpec((L, w), lambda b, c: (b * nc + c, 0))
    whole = lambda p: pl.BlockSpec(p.shape, lambda b, c, nd=p.ndim: (0,) * nd)
    end = pl.BlockSpec((1, 1, GN), lambda b, c: (b * nc + c, 0, 0))
    return pl.pallas_call(
        body, grid=(B, nc),
        in_specs=[tok(D)] + [whole(p) for p in (abr, abi, bre, bim, cre, cim, d)],
        out_specs=[tok(D), tok(D), tok(GN), tok(GN), end, end],
        out_shape=[jax.ShapeDtypeStruct((T, D), F32), jax.ShapeDtypeStruct((T, D), BF16),
                   jax.ShapeDtypeStruct((T, GN), BF16),
                   jax.ShapeDtypeStruct((T, GN), BF16), jax.ShapeDtypeStruct((B * nc, 1, GN), F32),
                   jax.ShapeDtypeStruct((B * nc, 1, GN), F32)],
        scratch_shapes=[pltpu.VMEM((L, GN), F32), pltpu.VMEM((L, GN), F32),
                        pltpu.VMEM((1, GN), F32), pltpu.VMEM((1, GN), F32)],
        compiler_params=_cp(("arbitrary", "arbitrary")), name=name,
    )(h, abr, abi, bre, bim, cre, cim, d)


def _s5_bwd(dy, h, xr, xi, er, ei, abr, abi, bre, bim, cre, cim, d, *, B, L, name):
    T, D = h.shape
    S = T // B
    L = min(L, S)
    nc = S // L
    nblk, cb, sb = bre.shape
    GN = abr.shape[1]
    dims_nt = (((1,), (1,)), ((), ()))
    dims_tn = (((0,), (0,)), ((), ()))

    def body(dy_ref, h_ref, xr_ref, xi_ref, er_ref, ei_ref, ar_ref, ai_ref, bre_ref, bim_ref,
             cre_ref, cim_ref, d_ref,
             dh_ref, dbre_ref, dbim_ref, dcre_ref, dcim_ref, dar_ref, dai_ref, dd_ref,
             lr, li, car, cai):
        b, cstep = pl.program_id(0), pl.program_id(1)
        ci = nc - 1 - cstep

        @pl.when((b == 0) & (cstep == 0))
        def _():
            for r in (dbre_ref, dbim_ref, dcre_ref, dcim_ref, dar_ref, dai_ref, dd_ref):
                r[...] = jnp.zeros_like(r)

        @pl.when(cstep == 0)
        def _():
            car[...] = jnp.zeros_like(car)
            cai[...] = jnp.zeros_like(cai)

        for j in range(nblk):
            dyj = dy_ref[:, j * cb:(j + 1) * cb].astype(BF16)
            lr[:, j * sb:(j + 1) * sb] = lax.dot_general(dyj, cre_ref[j], dims_nt, preferred_element_type=F32)
            li[:, j * sb:(j + 1) * sb] = -lax.dot_general(dyj, cim_ref[j], dims_nt, preferred_element_type=F32)
        ar, ai = ar_ref[...], ai_ref[...]

        def step(s, carry):
            t = L - 1 - s
            pr, pi = carry
            nr = lr[pl.ds(t, 1), :] + ar * pr + ai * pi
            ni = li[pl.ds(t, 1), :] - ai * pr + ar * pi
            lr[pl.ds(t, 1), :] = nr
            li[pl.ds(t, 1), :] = ni
            return nr, ni

        pr, pi = lax.fori_loop(0, L, step, (car[...], cai[...]), unroll=4)
        car[...] = pr
        cai[...] = pi
        has_prev = (ci > 0).astype(F32)
        first_row = lax.broadcasted_iota(jnp.int32, (L, sb), 0) == 0
        for j in range(nblk):
            cs = slice(j * cb, (j + 1) * cb)
            ss = slice(j * sb, (j + 1) * sb)
            lrj, lij = lr[:, ss], li[:, ss]
            xrj, xij = xr_ref[:, ss], xi_ref[:, ss]
            pr_j = jnp.where(first_row, er_ref[0][:, ss] * has_prev, pltpu.roll(xrj.astype(F32), 1, 0))
            pi_j = jnp.where(first_row, ei_ref[0][:, ss] * has_prev, pltpu.roll(xij.astype(F32), 1, 0))
            dar_ref[:, ss] += jnp.sum(lrj * pr_j + lij * pi_j, axis=0, keepdims=True)
            dai_ref[:, ss] += jnp.sum(lij * pr_j - lrj * pi_j, axis=0, keepdims=True)
            lrb, lib = lrj.astype(BF16), lij.astype(BF16)
            hj = h_ref[:, cs]
            dyf = dy_ref[:, cs]
            dyj = dyf.astype(BF16)
            dbre_ref[j] += lax.dot_general(hj, lrb, dims_tn, preferred_element_type=F32)
            dbim_ref[j] += lax.dot_general(hj, lib, dims_tn, preferred_element_type=F32)
            dcre_ref[j] += lax.dot_general(xrj, dyj, dims_tn, preferred_element_type=F32)
            dcim_ref[j] -= lax.dot_general(xij, dyj, dims_tn, preferred_element_type=F32)
            du = (lax.dot_general(lrb, bre_ref[j], dims_nt, preferred_element_type=F32)
                  + lax.dot_general(lib, bim_ref[j], dims_nt, preferred_element_type=F32))
            dh_ref[:, cs] = du + d_ref[:, cs] * dyf
            dd_ref[:, cs] += jnp.sum(dyf * hj.astype(F32), axis=0, keepdims=True)

    tok = lambda w: pl.BlockSpec((L, w), lambda b, c: (b * nc + nc - 1 - c, 0))
    whole = lambda p: pl.BlockSpec(p.shape, lambda b, c, nd=p.ndim: (0,) * nd)
    prev_end = pl.BlockSpec((1, 1, GN), lambda b, c: (b * nc + jnp.maximum(nc - 2 - c, 0), 0, 0))
    params = (abr, abi, bre, bim, cre, cim, d)
    acc_shapes = [bre.shape, bim.shape, cre.shape, cim.shape, abr.shape, abi.shape, d.shape]
    out = pl.pallas_call(
        body, grid=(B, nc),
        in_specs=[tok(D), tok(D), tok(GN), tok(GN), prev_end, prev_end] + [whole(p) for p in params],
        out_specs=[tok(D)] + [pl.BlockSpec(s, lambda b, c, nd=len(s): (0,) * nd) for s in acc_shapes],
        out_shape=[jax.ShapeDtypeStruct((T, D), F32)] + [jax.ShapeDtypeStruct(s, F32) for s in acc_shapes],
        scratch_shapes=[pltpu.VMEM((L, GN), F32), pltpu.VMEM((L, GN), F32),
                        pltpu.VMEM((1, GN), F32), pltpu.VMEM((1, GN), F32)],
        compiler_params=_cp(("arbitrary", "arbitrary")), name=name,
    )(dy, h, xr, xi, er, ei, *params)
    return out


def _conv_fwd(zp, w, *, R, tc, name):
    B, SP, C = zp.shape
    S = SP - CONV_HALO
    R, tc = min(R, S), min(tc, C)

    def body(z_ref, w_ref, y_ref):
        def chunk(ci, _):
            start = pl.multiple_of(ci * R, 8)
            ze = z_ref[pl.ds(start, R + CONV_HALO), :]
            acc = jnp.zeros((R, tc), F32)
            for m in range(CONV_WIDTH):
                k = CONV_WIDTH - 1 - m
                sh = ze if m == 0 else pltpu.roll(ze, m, 0)
                acc = acc + w_ref[k:k + 1, :] * sh[CONV_HALO:, :]
            y_ref[pl.ds(start, R), :] = acc
            return 0

        lax.fori_loop(0, S // R, chunk, 0)

    return pl.pallas_call(
        body, grid=(B, C // tc),
        in_specs=[pl.BlockSpec((None, SP, tc), lambda b, c: (b, 0, c)),
                  pl.BlockSpec((32, tc), lambda b, c: (0, c))],
        out_specs=pl.BlockSpec((None, S, tc), lambda b, c: (b, 0, c)),
        out_shape=jax.ShapeDtypeStruct((B, S, C), F32),
        compiler_params=_cp(("parallel", "parallel")), name=name,
    )(zp, w)


def _conv_bwd(zp, dyp, w, *, R, tc, name):
    B, SP, C = zp.shape
    S = SP - CONV_HALO
    R, tc = min(R, S), min(tc, C)

    def body(z_ref, dy_ref, w_ref, dz_ref, dw_ref):
        @pl.when(pl.program_id(1) == 0)
        def _():
            dw_ref[...] = jnp.zeros_like(dw_ref)

        def chunk(ci, _):
            start = pl.multiple_of(ci * R, 8)
            ze = z_ref[pl.ds(start, R + CONV_HALO), :]
            de = dy_ref[pl.ds(start, R + CONV_HALO), :]
            dy = de[:R, :]
            acc = jnp.zeros((R, tc), F32)
            for m in range(CONV_WIDTH):
                k = CONV_WIDTH - 1 - m
                zs = ze if m == 0 else pltpu.roll(ze, m, 0)
                ds_ = de if m == 0 else pltpu.roll(de, R + CONV_HALO - m, 0)
                acc = acc + w_ref[k:k + 1, :] * ds_[:R, :]
                dw_ref[k:k + 1, :] += jnp.sum(dy * zs[CONV_HALO:, :], axis=0, keepdims=True)
            dz_ref[pl.ds(start, R), :] = acc
            return 0

        lax.fori_loop(0, S // R, chunk, 0)

    return pl.pallas_call(
        body, grid=(C // tc, B),
        in_specs=[pl.BlockSpec((None, SP, tc), lambda c, b: (b, 0, c)),
                  pl.BlockSpec((None, SP, tc), lambda c, b: (b, 0, c)),
                  pl.BlockSpec((32, tc), lambda c, b: (0, c))],
        out_specs=[pl.BlockSpec((None, S, tc), lambda c, b: (b, 0, c)),
                   pl.BlockSpec((32, tc), lambda c, b: (0, c))],
        out_shape=[jax.ShapeDtypeStruct((B, S, C), F32), jax.ShapeDtypeStruct((32, C), F32)],
        compiler_params=_cp(("parallel", "arbitrary")), name=name,
    )(zp, dyp, w)


def _gmlp_fwd(u, vn, ws, bcol, *, nck, name):
    T, E = u.shape
    H = ws.shape[0]
    he = E // H
    rows = nck * GMLP_CHUNK
    rows = min(rows, T)
    n_in = rows // GMLP_CHUNK

    def body(u_ref, v_ref, ws_ref, b_ref, o_ref):
        for c in range(n_in):
            rs = slice(c * GMLP_CHUNK, (c + 1) * GMLP_CHUNK)
            for hh in range(H):
                cs = slice(hh * he, (hh + 1) * he)
                v2 = jnp.dot(ws_ref[hh], v_ref[rs, cs].astype(BF16), preferred_element_type=F32)
                v2 = v2 + b_ref[:, hh:hh + 1]
                o_ref[rs, cs] = (u_ref[rs, cs] * v2).astype(o_ref.dtype)

    tok = pl.BlockSpec((rows, E), lambda i: (i, 0))
    return pl.pallas_call(
        body, grid=(T // rows,),
        in_specs=[tok, tok, pl.BlockSpec(ws.shape, lambda i: (0, 0, 0)), pl.BlockSpec(bcol.shape, lambda i: (0, 0))],
        out_specs=tok, out_shape=jax.ShapeDtypeStruct((T, E), BF16),
        compiler_params=_cp(("parallel",)), name=name,
    )(u, vn, ws, bcol)


def _gmlp_bwd(duv, u, vn, ws, bcol, *, nck, name):
    T, E = u.shape
    H = ws.shape[0]
    he = E // H
    rows = min(nck * GMLP_CHUNK, T)
    n_in = rows // GMLP_CHUNK
    dims_nt = (((1,), (1,)), ((), ()))
    dims_tn = (((0,), (0,)), ((), ()))

    def body(g_ref, u_ref, v_ref, ws_ref, b_ref, du_ref, dv_ref, dws_ref, db_ref):
        @pl.when(pl.program_id(0) == 0)
        def _():
            dws_ref[...] = jnp.zeros_like(dws_ref)
            db_ref[...] = jnp.zeros_like(db_ref)

        for c in range(n_in):
            rs = slice(c * GMLP_CHUNK, (c + 1) * GMLP_CHUNK)
            for hh in range(H):
                cs = slice(hh * he, (hh + 1) * he)
                vb = v_ref[rs, cs].astype(BF16)
                v2 = jnp.dot(ws_ref[hh], vb, preferred_element_type=F32) + b_ref[:, hh:hh + 1]
                g = g_ref[rs, cs]
                du_ref[rs, cs] = g * v2
                dv2 = g * u_ref[rs, cs]
                dv2b = dv2.astype(BF16)
                dv_ref[rs, cs] = lax.dot_general(ws_ref[hh], dv2b, dims_tn, preferred_element_type=F32)
                dws_ref[hh] += lax.dot_general(dv2b, vb, dims_nt, preferred_element_type=F32)
                db_ref[:, hh:hh + 1] += jnp.sum(dv2, axis=1, keepdims=True)

    tok = pl.BlockSpec((rows, E), lambda i: (i, 0))
    return pl.pallas_call(
        body, grid=(T // rows,),
        in_specs=[tok, tok, tok, pl.BlockSpec(ws.shape, lambda i: (0, 0, 0)), pl.BlockSpec(bcol.shape, lambda i: (0, 0))],
        out_specs=[tok, tok, pl.BlockSpec(ws.shape, lambda i: (0, 0, 0)), pl.BlockSpec(bcol.shape, lambda i: (0, 0))],
        out_shape=[jax.ShapeDtypeStruct((T, E), F32), jax.ShapeDtypeStruct((T, E), F32),
                   jax.ShapeDtypeStruct(ws.shape, F32), jax.ShapeDtypeStruct(bcol.shape, F32)],
        compiler_params=_cp(("arbitrary",)), name=name,
    )(duv, u, vn, ws, bcol)


PAIRS = ATT_HEADS // 2


def _att_consts():
    ji = lax.broadcasted_iota(jnp.int32, (2 * ATT_BLK, ATT_BLK), 0)
    ii = lax.broadcasted_iota(jnp.int32, (2 * ATT_BLK, ATT_BLK), 1)
    dist = ii + ATT_BLK - ji
    band = (dist >= 0) & (dist <= ATT_BLK)
    cur = ji >= ATT_BLK
    first_head = lax.broadcasted_iota(jnp.int32, (ATT_BLK, 2 * HEAD_DIM), 1) < HEAD_DIM
    return band, cur, first_head


def _att_specs(nbk, offs, nsteps, rev):
    rows = nbk * ATT_BLK
    step = (lambda i: nsteps - 1 - i) if rev else (lambda i: i)
    qoff, koff, voff = offs
    blk = lambda off: pl.BlockSpec((rows, 2 * HEAD_DIM), lambda hp, i: (step(i), off + hp))
    prev = lambda off: pl.BlockSpec((ATT_BLK, 2 * HEAD_DIM), lambda hp, i: (jnp.maximum(step(i) * nbk - 1, 0), off + hp))
    out = pl.BlockSpec((rows, 2 * HEAD_DIM), lambda hp, i: (step(i), hp))
    stat = pl.BlockSpec((2, nbk, ATT_BLK), lambda hp, i: (hp, step(i), 0))
    return [blk(qoff), blk(koff), prev(koff), blk(voff), prev(voff)], out, stat


def _att_fwd(arr, offs, *, nb, nbk, name):
    T = arr.shape[0]
    nbk = min(nbk, T // ATT_BLK)
    nsteps = T // (nbk * ATT_BLK)
    scale = HEAD_DIM ** -0.5
    dims_nt = (((1,), (1,)), ((), ()))
    dims_tn = (((0,), (0,)), ((), ()))

    def body(q_ref, k_ref, kp_ref, v_ref, vp_ref, o_ref, lse_ref):
        i = pl.program_id(1)
        band, cur, first_head = _att_consts()
        for jj in range(nbk):
            rs = slice(jj * ATT_BLK, (jj + 1) * ATT_BLK)
            ps = slice((jj - 1) * ATT_BLK, jj * ATT_BLK)
            has_prev = ((i * nbk + jj) & (nb - 1)) != 0
            valid = band & (cur | has_prev)
            kk = jnp.concatenate([kp_ref[...] if jj == 0 else k_ref[ps, :], k_ref[rs, :]], axis=0)
            vv = jnp.concatenate([vp_ref[...] if jj == 0 else v_ref[ps, :], v_ref[rs, :]], axis=0)
            q = q_ref[rs, :]
            outs = []
            for hd in range(2):
                qh = jnp.where(first_head if hd == 0 else ~first_head, q, jnp.zeros_like(q))
                st = lax.dot_general(kk, qh, dims_nt, preferred_element_type=F32) * scale
                st = jnp.where(valid, st, MASK_VALUE)
                m = jnp.max(st, axis=0, keepdims=True)
                p = jnp.exp(st - m)
                l = jnp.sum(p, axis=0, keepdims=True)
                lse_ref[hd, jj:jj + 1, :] = m + jnp.log(l)
                pn = (p / l).astype(BF16)
                outs.append(lax.dot_general(pn, vv, dims_tn, preferred_element_type=F32))
            o_ref[rs, :] = jnp.where(first_head, outs[0], outs[1])

    ins, out, stat = _att_specs(nbk, offs, nsteps, False)
    return pl.pallas_call(
        body, grid=(PAIRS, nsteps), in_specs=ins, out_specs=[out, stat],
        out_shape=[jax.ShapeDtypeStruct((T, ATT_W), F32), jax.ShapeDtypeStruct((ATT_HEADS, T // ATT_BLK, ATT_BLK), F32)],
        compiler_params=_cp(("parallel", "parallel")), name=name,
    )(arr, arr, arr, arr, arr)


def _att_bwd(arr, offs, do, lse, dlse, *, nb, nbk, name):
    T = arr.shape[0]
    nbk = min(nbk, T // ATT_BLK)
    nsteps = T // (nbk * ATT_BLK)
    scale = HEAD_DIM ** -0.5
    dims_nt = (((1,), (1,)), ((), ()))
    dims_tn = (((0,), (0,)), ((), ()))

    def body(q_ref, k_ref, kp_ref, v_ref, vp_ref, do_ref, lse_ref, dlse_ref, dq_ref, dk_ref, dv_ref, ck, cv):
        step = pl.program_id(1)
        i = nsteps - 1 - step
        band, cur, first_head = _att_consts()

        @pl.when(step == 0)
        def _():
            ck[...] = jnp.zeros_like(ck)
            cv[...] = jnp.zeros_like(cv)

        carry_k, carry_v = ck[...], cv[...]
        for jj in reversed(range(nbk)):
            rs = slice(jj * ATT_BLK, (jj + 1) * ATT_BLK)
            ps = slice((jj - 1) * ATT_BLK, jj * ATT_BLK)
            has_prev = ((i * nbk + jj) & (nb - 1)) != 0
            valid = band & (cur | has_prev)
            kk = jnp.concatenate([kp_ref[...] if jj == 0 else k_ref[ps, :], k_ref[rs, :]], axis=0)
            vv = jnp.concatenate([vp_ref[...] if jj == 0 else v_ref[ps, :], v_ref[rs, :]], axis=0)
            q = q_ref[rs, :]
            dob = do_ref[rs, :].astype(BF16)
            dqs, dkk, dvv = [], None, None
            for hd in range(2):
                sel = first_head if hd == 0 else ~first_head
                qh = jnp.where(sel, q, jnp.zeros_like(q))
                doh = jnp.where(sel, dob, jnp.zeros_like(dob))
                st = lax.dot_general(kk, qh, dims_nt, preferred_element_type=F32) * scale
                st = jnp.where(valid, st, MASK_VALUE)
                p = jnp.exp(st - lse_ref[hd, jj:jj + 1, :])
                dp = lax.dot_general(vv, doh, dims_nt, preferred_element_type=F32)
                delta = jnp.sum(p * dp, axis=0, keepdims=True)
                dsb = (p * (dp - delta + dlse_ref[hd, jj:jj + 1, :]) * scale).astype(BF16)
                dqs.append(lax.dot_general(dsb, kk, dims_tn, preferred_element_type=F32))
                dk_h = jnp.dot(dsb, qh, preferred_element_type=F32)
                dv_h = jnp.dot(p.astype(BF16), doh, preferred_element_type=F32)
                dkk = dk_h if dkk is None else dkk + dk_h
                dvv = dv_h if dvv is None else dvv + dv_h
            dq_ref[rs, :] = jnp.where(first_head, dqs[0], dqs[1]).astype(dq_ref.dtype)
            dk_ref[rs, :] = (dkk[ATT_BLK:] + carry_k).astype(dk_ref.dtype)
            dv_ref[rs, :] = (dvv[ATT_BLK:] + carry_v).astype(dv_ref.dtype)
            carry_k, carry_v = dkk[:ATT_BLK], dvv[:ATT_BLK]
        ck[...] = carry_k
        cv[...] = carry_v

    ins, out, stat = _att_specs(nbk, offs, nsteps, True)
    return pl.pallas_call(
        body, grid=(PAIRS, nsteps), in_specs=ins + [out, stat, stat], out_specs=[out] * 3,
        out_shape=[jax.ShapeDtypeStruct((T, ATT_W), BF16)] * 3,
        scratch_shapes=[pltpu.VMEM((ATT_BLK, 2 * HEAD_DIM), F32), pltpu.VMEM((ATT_BLK, 2 * HEAD_DIM), F32)],
        compiler_params=_cp(("arbitrary", "arbitrary")), name=name,
    )(arr, arr, arr, arr, arr, do, lse, dlse)


def _deinterleave(t, B, S, dil):
    if dil == 1:
        return t
    return t.reshape((B, S // dil, dil) + t.shape[1:]).swapaxes(1, 2).reshape(t.shape)


def _interleave(t, B, S, dil):
    if dil == 1:
        return t
    return t.reshape((B, dil, S // dil) + t.shape[1:]).swapaxes(1, 2).reshape(t.shape)


def _stats_to_tokens(lse, B, S, dil):
    return _interleave(lse.reshape(lse.shape[0], -1).T, B, S, dil)


def _stats_from_tokens(dl, B, S, dil):
    return _deinterleave(dl, B, S, dil).T.reshape(dl.shape[1], -1, ATT_BLK)


def _mesh_pos():
    return lax.axis_index("x"), lax.axis_index("y"), lax.axis_index("c")


def _allgather8(xs, *, name):
    m_per, n = xs.shape

    def body(x_ref, out_ref, send_sems, recv_sems, local_sem):
        x, y, c = _mesh_pos()
        me, sibling = (x, y, c), (x, y, 1 - c)
        chips = [(1 - x, y), (x, 1 - y), (1 - x, 1 - y)]

        def rows(px, py, pc):
            return out_ref.at[pl.ds((4 * px + 2 * py + pc) * m_per, m_per), :]

        def copy(k, block, to, src=None):
            return pltpu.make_async_remote_copy(
                src_ref=rows(*block) if src is None else src, dst_ref=rows(*block),
                send_sem=send_sems.at[k], recv_sem=recv_sems.at[k], device_id=to, device_id_type=MESH)

        mine = pltpu.make_async_copy(x_ref, rows(*me), local_sem)
        mine.start()
        first = [copy(0, me, sibling, src=x_ref)]
        first += [copy(1 + j, me, (*chip, c), src=x_ref) for j, chip in enumerate(chips)]
        for cp in first:
            cp.start()
        passed = [copy(4 + j, (*chip, c), sibling) for j, chip in enumerate(chips)]
        for j, chip in enumerate(chips):
            copy(1 + j, (*chip, c), me).wait_recv()
            passed[j].start()
        copy(0, sibling, me).wait_recv()
        for j, chip in enumerate(chips):
            copy(4 + j, (*chip, 1 - c), me).wait_recv()
        for cp in first + passed:
            cp.wait_send()
        mine.wait()

    return pl.pallas_call(
        body, out_shape=jax.ShapeDtypeStruct((8 * m_per, n), xs.dtype),
        in_specs=[pl.BlockSpec(memory_space=pltpu.VMEM)], out_specs=pl.BlockSpec(memory_space=pltpu.VMEM),
        scratch_shapes=[pltpu.SemaphoreType.DMA((7,)), pltpu.SemaphoreType.DMA((7,)), pltpu.SemaphoreType.DMA],
        compiler_params=pltpu.CompilerParams(vmem_limit_bytes=VMEM_LIMIT), name=name,
    )(xs)


def _hbm_call(body, arrays, out_shapes, n_sems, *, name):
    any_spec = pl.BlockSpec(memory_space=pl.ANY)
    return pl.pallas_call(
        body, out_shape=out_shapes, in_specs=[any_spec] * len(arrays), out_specs=[any_spec] * len(out_shapes),
        scratch_shapes=[pltpu.SemaphoreType.DMA((n_sems,)), pltpu.SemaphoreType.DMA((n_sems,))], name=name,
    )(*arrays)


def _other_chips(x, y):
    return [(1 - x, y), (x, 1 - y), (1 - x, 1 - y)]


def _allgather_chips(ws, *, name):
    n = len(ws)

    def body(*refs):
        ins, outs, (send_sems, recv_sems) = refs[:n], refs[n:2 * n], refs[2 * n:]
        x, y, c = _mesh_pos()
        chips = _other_chips(x, y)

        def copy(a, k, px, py, half, to, src=None):
            slot = outs[a].at[2 * px + py, half]
            return pltpu.make_async_remote_copy(
                src_ref=slot if src is None else src, dst_ref=slot,
                send_sem=send_sems.at[6 * a + k], recv_sem=recv_sems.at[6 * a + k], device_id=to, device_id_type=MESH)

        first = [copy(a, j, x, y, c, (*chip, c), src=ins[a].at[c]) for a in range(n) for j, chip in enumerate(chips)]
        for cp in first:
            cp.start()
        passed = []
        for j, chip in enumerate(chips):
            for a in range(n):
                copy(a, j, *chip, c, (x, y, c)).wait_recv()
                passed.append(copy(a, 3 + j, *chip, c, (x, y, 1 - c)))
                passed[-1].start()
        for j, chip in enumerate(chips):
            for a in range(n):
                copy(a, 3 + j, *chip, 1 - c, (x, y, c)).wait_recv()
        for cp in first + passed:
            cp.wait_send()

    return _hbm_call(body, ws, [jax.ShapeDtypeStruct((N_CHIPS,) + w.shape, w.dtype) for w in ws], 6 * n, name=name)


def _split_start(srcs, lands, after, issue, n_sems, *, name):
    ns, nl = len(srcs), len(lands)
    hbm, sem = pl.BlockSpec(memory_space=pltpu.HBM), pl.BlockSpec(memory_space=pltpu.SEMAPHORE)
    extra = [] if after is None else [after]

    def body(*refs):
        n_in = ns + nl + len(extra)
        send_sems, recv_sems = refs[n_in], refs[n_in + 1]
        issue(refs[:ns], refs[ns:ns + nl], send_sems, recv_sems)
        refs[-1][...] = jnp.zeros_like(refs[-1])

    arrays = [pltpu.with_memory_space_constraint(a, pltpu.HBM) for a in list(srcs) + list(lands)]
    out = pl.pallas_call(
        body, name=name,
        out_shape=(pltpu.SemaphoreType.DMA((n_sems,)), pltpu.SemaphoreType.DMA((n_sems,)),
                   *[pltpu.HBM(a.shape, a.dtype) for a in arrays], jax.ShapeDtypeStruct((8, 128), F32)),
        in_specs=[hbm] * (ns + nl) + [pl.BlockSpec(memory_space=pl.ANY)] * len(extra),
        out_specs=(sem, sem, *[hbm] * (ns + nl), pl.BlockSpec(memory_space=pltpu.VMEM)),
        input_output_aliases={i: 2 + i for i in range(ns + nl)},
        compiler_params=pltpu.CompilerParams(has_side_effects=pltpu.SideEffectType.DATAFLOW_SIDE_EFFECTING),
    )(*arrays, *extra)
    return out[0], out[1], list(out[2:2 + ns]), list(out[2 + ns:2 + ns + nl]), out[-1]


def _split_wait(send_sems, recv_sems, srcs, lands, after, waits, *, name):
    ns, nl = len(srcs), len(lands)
    hbm, sem = pl.BlockSpec(memory_space=pltpu.HBM), pl.BlockSpec(memory_space=pltpu.SEMAPHORE)

    def body(*refs):
        waits(refs[:ns], refs[ns:ns + nl], refs[ns + nl], refs[ns + nl + 1])

    out = pl.pallas_call(
        body, name=name,
        out_shape=tuple(pltpu.HBM(a.shape, a.dtype) for a in list(srcs) + list(lands)),
        in_specs=[hbm] * (ns + nl) + [sem, sem, pl.BlockSpec(memory_space=pl.ANY)],
        out_specs=tuple([hbm] * (ns + nl)),
        input_output_aliases={i: i for i in range(ns + nl)},
        compiler_params=pltpu.CompilerParams(has_side_effects=pltpu.SideEffectType.DATAFLOW_SIDE_EFFECTING),
    )(*srcs, *lands, send_sems, recv_sems, after)
    return list(out[:ns]), list(out[ns:])


def _gather_start(halves, after, *, name):
    n = len(halves)
    lands = [lax.empty((N_CHIPS,) + h.shape, h.dtype) for h in halves]

    def issue(srcs, dsts, send_sems, recv_sems):
        x, y, c = _mesh_pos()
        me = 2 * x + y
        for a in range(n):
            for j, (px, py) in enumerate(_other_chips(x, y)):
                for cc in range(2):
                    pltpu.make_async_remote_copy(
                        src_ref=srcs[a].at[c], dst_ref=dsts[a].at[me, c],
                        send_sem=send_sems.at[6 * a + 2 * j + cc], recv_sem=recv_sems.at[6 * a + 2 * j + c],
                        device_id=(px, py, cc), device_id_type=MESH).start()

    return _split_start(halves, lands, after, issue, 6 * n, name=name)


def _gather_wait(started, after, *, name):
    send_sems, recv_sems, halves, lands = started
    n = len(halves)

    def waits(srcs, dsts, send_sems, recv_sems):
        x, y, c = _mesh_pos()
        me = 2 * x + y
        for a in range(n):
            for j, (px, py) in enumerate(_other_chips(x, y)):
                for cc in range(2):
                    pltpu.make_async_remote_copy(
                        src_ref=srcs[a].at[cc], dst_ref=dsts[a].at[2 * px + py, cc],
                        send_sem=send_sems.at[6 * a + 2 * j + cc], recv_sem=recv_sems.at[6 * a + 2 * j + cc],
                        device_id=(px, py, cc), device_id_type=MESH).wait_recv()
        for a in range(n):
            for j, (px, py) in enumerate(_other_chips(x, y)):
                for cc in range(2):
                    pltpu.make_async_remote_copy(
                        src_ref=srcs[a].at[c], dst_ref=dsts[a].at[me, c],
                        send_sem=send_sems.at[6 * a + 2 * j + cc], recv_sem=recv_sems.at[6 * a + 2 * j + c],
                        device_id=(px, py, cc), device_id_type=MESH).wait_send()

    return _split_wait(send_sems, recv_sems, halves, lands, after, waits, name=name)


def _reduce_plan_loops(plans, chip, c, fn):
    for a, plan in enumerate(plans):
        for h, cc in plan:
            for k in range(N_CHIPS):
                fn(a, h, k, cc, jnp.logical_or(chip != k, c != cc))


def _reduce_start(srcs, lands, plans, after, *, name):
    def issue(src_refs, land_refs, send_sems, recv_sems):
        x, y, c = _mesh_pos()
        chip = 2 * x + y
        my_id = 2 * chip + c

        def send(a, h, k, cc, is_other):
            @pl.when(is_other)
            def _():
                pltpu.make_async_remote_copy(
                    src_ref=src_refs[a].at[h, k], dst_ref=land_refs[a].at[my_id],
                    send_sem=send_sems.at[8 * a + 2 * k + cc], recv_sem=recv_sems.at[8 * a + my_id],
                    device_id=(k // 2, k % 2, cc), device_id_type=MESH).start()

        _reduce_plan_loops(plans, chip, c, send)

    return _split_start(srcs, lands, after, issue, 8 * len(srcs), name=name)


def _reduce_wait(started, plans, after, *, name):
    send_sems, recv_sems, srcs, lands = started

    def waits(src_refs, land_refs, send_sems, recv_sems):
        x, y, c = _mesh_pos()
        chip = 2 * x + y
        my_id = 2 * chip + c
        for a, plan in enumerate(plans):
            for h, cc in plan:
                for s in range(2 * N_CHIPS):
                    @pl.when(jnp.logical_and(c == cc, my_id != s))
                    def _(a=a, h=h, s=s):
                        pltpu.make_async_remote_copy(
                            src_ref=src_refs[a].at[h, 0], dst_ref=land_refs[a].at[s],
                            send_sem=send_sems.at[8 * a + s], recv_sem=recv_sems.at[8 * a + s],
                            device_id=(s // 4, (s // 2) % 2, s % 2), device_id_type=MESH).wait_recv()

        def sent(a, h, k, cc, is_other):
            @pl.when(is_other)
            def _():
                pltpu.make_async_remote_copy(
                    src_ref=src_refs[a].at[h, k], dst_ref=land_refs[a].at[my_id],
                    send_sem=send_sems.at[8 * a + 2 * k + cc], recv_sem=recv_sems.at[8 * a + my_id],
                    device_id=(k // 2, k % 2, cc), device_id_type=MESH).wait_send()

        _reduce_plan_loops(plans, chip, c, sent)

    return _split_wait(send_sems, recv_sems, srcs, lands, after, waits, name=name)


def _sum8(land, own, my_id, *, name):
    n_src, R, C = land.shape
    tr = _pick_rows(R, max(8, 1024 * 1024 // (2 * C)))

    def body(id_ref, *refs):
        own_ref, o_ref = refs[n_src], refs[n_src + 1]
        me = id_ref[0]
        acc = None
        for s in range(n_src):
            term = jnp.where(me == s, own_ref[...], refs[s][...]).astype(F32)
            acc = term if acc is None else acc + term
        o_ref[...] = acc

    return pl.pallas_call(
        body, out_shape=jax.ShapeDtypeStruct((R, C), F32),
        grid_spec=pltpu.PrefetchScalarGridSpec(
            num_scalar_prefetch=1, grid=(R // tr,),
            in_specs=[pl.BlockSpec((None, tr, C), lambda i, idr, s=s: (s, i, 0)) for s in range(n_src)]
            + [pl.BlockSpec((tr, C), lambda i, idr: (i, 0))],
            out_specs=pl.BlockSpec((tr, C), lambda i, idr: (i, 0))),
        compiler_params=_cp(("parallel",)), name=name,
    )(my_id.reshape(1).astype(jnp.int32), *([land] * n_src), own)


def _swap_halves(gs, *, name):
    n = len(gs)

    def body(*refs):
        ins, outs, (send_sems, recv_sems) = refs[:n], refs[n:2 * n], refs[2 * n:]
        x, y, c = _mesh_pos()
        cps = [pltpu.make_async_remote_copy(
            src_ref=ins[a].at[1 - c], dst_ref=outs[a], send_sem=send_sems.at[a], recv_sem=recv_sems.at[a],
            device_id=(x, y, 1 - c), device_id_type=MESH) for a in range(n)]
        for cp in cps:
            cp.start()
        for cp in cps:
            cp.wait()

    return _hbm_call(body, gs, [jax.ShapeDtypeStruct(g.shape[1:], g.dtype) for g in gs], n, name=name)


def _scatter_chips(ss, *, name):
    n = len(ss)

    def body(*refs):
        ins, outs, (send_sems, recv_sems) = refs[:n], refs[n:2 * n], refs[2 * n:]
        x, y, c = _mesh_pos()
        me = 2 * x + y
        chips = _other_chips(x, y)

        def copy(a, j, px, py):
            return pltpu.make_async_remote_copy(
                src_ref=ins[a].at[2 * px + py], dst_ref=outs[a].at[me],
                send_sem=send_sems.at[3 * a + j], recv_sem=recv_sems.at[3 * a + j],
                device_id=(px, py, c), device_id_type=MESH)

        def arrival(a, j, px, py):
            return pltpu.make_async_remote_copy(
                src_ref=ins[a].at[me], dst_ref=outs[a].at[2 * px + py],
                send_sem=send_sems.at[3 * a + j], recv_sem=recv_sems.at[3 * a + j],
                device_id=(px, py, c), device_id_type=MESH)

        cps = [copy(a, j, *chip) for a in range(n) for j, chip in enumerate(chips)]
        for cp in cps:
            cp.start()
        for a in range(n):
            for j, chip in enumerate(chips):
                arrival(a, j, *chip).wait_recv()
        for cp in cps:
            cp.wait_send()

    return _hbm_call(body, ss, [jax.ShapeDtypeStruct(s.shape, s.dtype) for s in ss], 3 * n, name=name)


def _share_halves(ts, *, name):
    n = len(ts)

    def body(*refs):
        ins, outs, (send_sems, recv_sems) = refs[:n], refs[n:2 * n], refs[2 * n:]
        x, y, c = _mesh_pos()
        cps = [pltpu.make_async_remote_copy(
            src_ref=ins[a], dst_ref=outs[a].at[c], send_sem=send_sems.at[a], recv_sem=recv_sems.at[a],
            device_id=(x, y, 1 - c), device_id_type=MESH) for a in range(n)]
        for cp in cps:
            cp.start()
        for a in range(n):
            pltpu.make_async_remote_copy(
                src_ref=ins[a], dst_ref=outs[a].at[1 - c], send_sem=send_sems.at[a], recv_sem=recv_sems.at[a],
                device_id=(x, y, 1 - c), device_id_type=MESH).wait_recv()
        for cp in cps:
            cp.wait_send()

    return _hbm_call(body, ts, [jax.ShapeDtypeStruct((2,) + t.shape, t.dtype) for t in ts], n, name=name)


def _half_add(g, ra, core, *, name):
    _, R, C = g.shape
    tr = _pick_rows(R, max(8, 2 * 1024 * 1024 // (4 * C)))

    def body(core_ref, g_ref, ra_ref, o_ref):
        o_ref[...] = (g_ref[...] + ra_ref[...]).astype(o_ref.dtype)

    return pl.pallas_call(
        body, out_shape=jax.ShapeDtypeStruct((R, C), BF16),
        grid_spec=pltpu.PrefetchScalarGridSpec(
            num_scalar_prefetch=1, grid=(R // tr,),
            in_specs=[pl.BlockSpec((None, tr, C), lambda i, cr: (cr[0], i, 0)),
                      pl.BlockSpec((tr, C), lambda i, cr: (i, 0))],
            out_specs=pl.BlockSpec((tr, C), lambda i, cr: (i, 0))),
        compiler_params=_cp(("parallel",)), name=name,
    )(core.reshape(1).astype(jnp.int32), g, ra)


def _sum4(rb, *, name):
    _, R, C = rb.shape
    tr = _pick_rows(R, max(8, 2 * 1024 * 1024 // (4 * C)))

    def body(r0, r1, r2, r3, o_ref):
        f = lambda r: r[...].astype(F32)
        o_ref[...] = ((f(r0) + f(r1)) + f(r2)) + f(r3)

    return pl.pallas_call(
        body, out_shape=jax.ShapeDtypeStruct((R, C), F32), grid=(R // tr,),
        in_specs=[pl.BlockSpec((None, tr, C), lambda i, k=k: (k, i, 0)) for k in range(N_CHIPS)],
        out_specs=pl.BlockSpec((tr, C), lambda i: (i, 0)),
        compiler_params=_cp(("parallel",)), name=name,
    )(rb, rb, rb, rb)


TR = 256
S5_CHUNK = 256


def _rms_fwd(x, g, name):
    return _rowwise(_f_rms, [x], [g], (BF16,), tr=TR, name=name)[0]


def _rms_bwd(x, g, dh, gx, name, after=None):
    (dx, dxb), (dg,) = _rowwise_vjp(_f_rms, [x], [g], [dh], [(F32, BF16)], adds={0: gx}, after=after, tr=TR,
                                    name=name)
    return dx, dxb, dg


def _grad_cols(M, Nq):
    def imap(tm, tn):
        hp, per = (M // 2) // tm, Nq // tn
        assert hp * tm * 2 == M and per * tn == Nq, (M, Nq, tm, tn)
        return lambda i, j, k: (i // hp, j // per, i % hp, j % per)
    return (2, N_CHIPS, M // 2, Nq), lambda tm, tn: (None, None, tm, tn), imap, None, M // 2, Nq


def _grad_rows(Mq, N):
    def imap(tm, tn):
        po, hp = Mq // tm, (Mq // 2) // tm
        assert hp * tm * 2 == Mq, (Mq, tm)
        return lambda i, j, k: ((i % po) // hp, i // po, (i % po) % hp, j)
    return (2, N_CHIPS, Mq // 2, N), lambda tm, tn: (None, None, tm, tn), imap, None, Mq // 2, N


def _grad_layer_cols(slot, lh, M, Nq, buf):
    def imap(tm, tn):
        per = Nq // tn
        return lambda i, j, k: (j // per, slot, i, j % per)
    return (N_CHIPS, lh, M, Nq), lambda tm, tn: (None, None, tm, tn), imap, buf, M, Nq


def _grad_layer_rows(slot, lh, Mq, N, buf):
    def imap(tm, tn):
        po = Mq // tm
        return lambda i, j, k: (i // po, slot, i % po, j)
    return (N_CHIPS, lh, Mq, N), lambda tm, tn: (None, None, tm, tn), imap, buf, Mq, N


def _mlp_fwd(x, g, w_in, w_out, li):
    h2 = _rms_fwd(x, g, f"mlp_rms_{li}")
    r = _mm(h2, w_in, out_dtypes=(BF16,), epi=lambda acc: (jnp.maximum(acc, 0.0),), name=f"mlp_in_{li}")
    x_out = _mm(r, w_out, pro_a=lambda t: t * t, epi=lambda acc, res: (acc + res,), extras=(x,),
                name=f"mlp_out_{li}")
    return x_out, (h2, r)


def _mlp_bwd(gx, gxb, x, g, w_in, w_out, saved, li, nl, bufs):
    h2, r = saved
    D, F = w_in.shape
    lh = nl // 2
    da = _mm(gxb, w_out, tb=True, out_dtypes=(BF16,),
             epi=lambda acc, rt: (acc * 2.0 * rt.astype(F32),), extras=(r,), name=f"mlp_dact_{li}")
    buf_in, buf_out = bufs if bufs is not None else (None, None)
    d_w_out = _mm(r, gxb, ta=True, pro_a=lambda t: t * t, tm=512, tn=1024, tk=2048, out_dtypes=(BF16,),
                  out=_grad_layer_rows(li % lh, lh, F // N_CHIPS, D, buf_out), name=f"mlp_dwout_{li}")
    d_w_in = _mm(h2, da, ta=True, tm=1024, tn=1024, tk=2048, out_dtypes=(BF16,),
                 out=_grad_layer_cols(li % lh, lh, D, F // N_CHIPS, buf_in), name=f"mlp_dwin_{li}")
    dh2 = _mm(da, w_in, tb=True, name=f"mlp_dh_{li}")
    gx_mid, gxb_mid, dg = _rms_bwd(x, g, dh2, gx, f"mlp_rms_bwd_{li}")
    return gx_mid, gxb_mid, dg, (d_w_in, d_w_out)


def _local_step(x3, tgt3, p, layer_weights, token=None, grads_done=lambda group: None):
    B, S, D = x3.shape
    T = B * S
    x = x3.reshape(T, D)
    grads = {}
    row = lambda v: v.reshape(1, -1)
    p = dict(p)
    nl = p["norm_mlp"].shape[0]
    mlp_in, mlp_out = [None] * nl, [None] * nl

    def fetch(li, after):
        wl = dict(layer_weights(li, after))
        mlp_in[li], mlp_out[li] = wl.pop("mlp_w_in"), wl.pop("mlp_w_out")
        p.update(wl)

    g0 = row(p["norm_mix"][0])
    if token is not None:
        g0 = g0 + token[:1, :1]
    h0 = _rms_fwd(x, g0, "rms_mix_0")
    s5_args = (p["ssm_a_re"][0], p["ssm_a_im"][0], p["ssm_b_re"][0], p["ssm_b_im"][0],
               p["ssm_c_re"][0], p["ssm_c_im"][0], p["ssm_log_dt"][0])
    s5_exp, s5_vjp = jax.vjp(_s5_prep, *s5_args)
    abr, abi, bre, bim, cre, cim = s5_exp
    bre_b, bim_b, cre_b, cim_b = (t.astype(BF16) for t in (bre, bim, cre, cim))
    d_skip = p["ssm_d"]
    ypre, yb, sxr, sxi, ser, sei = _s5_fwd(h0, abr, abi, bre_b, bim_b, cre_b, cim_b, d_skip, B=B, L=S5_CHUNK,
                                           name="s5_fwd")
    fetch(0, yb)
    w_glu = p["ssm_w_glu"]
    z0 = _mm(yb, w_glu, name="s5_glu_mm")
    x_mid0 = _rowwise(lambda z, xr: (_f_glu(z)[0] + xr,), [z0, x], [], (F32,), tr=TR, name="s5_glu")[0]
    x1, mlp_saved0 = _mlp_fwd(x_mid0, row(p["norm_mlp"][0]), mlp_in[0], mlp_out[0], 0)

    g1 = row(p["norm_mix"][1])
    h1 = _rms_fwd(x1, g1, "rms_mix_1")
    fetch(1, h1)
    z1 = _mm(h1, p["conv_w_pw1"], name="conv_pw1")
    zg = _rowwise(_f_bias_glu, [z1], [p["conv_b_pw1"]], (F32,), tr=TR, name="conv_glu")[0]
    zp = jnp.pad(zg.reshape(B, S, D), ((0, 0), (CONV_HALO, 0), (0, 0)))
    w_dw = jnp.pad(p["conv_w_dw"], ((0, 32 - CONV_WIDTH), (0, 0)))
    yc = _conv_fwd(zp, w_dw, R=256, tc=128, name="conv_dw").reshape(T, D)
    ln_par = [p["conv_b_dw"], p["conv_ln_g"], p["conv_ln_b"]]
    qc = _rowwise(_f_ln_silu, [yc], ln_par, (BF16,), tr=TR, name="conv_ln_silu")[0]
    x_mid1 = _mm(qc, p["conv_w_pw2"], epi=lambda acc, bias, res: (acc + bias + res,),
                 extras=(p["conv_b_pw2"], x1), name="conv_pw2")
    x2, mlp_saved1 = _mlp_fwd(x_mid1, row(p["norm_mlp"][1]), mlp_in[1], mlp_out[1], 1)

    g2 = row(p["norm_mix"][2])
    h2 = _rms_fwd(x2, g2, "rms_mix_2")
    fetch(2, h2)
    z2 = _mm(h2, p["gmlp_w_in"], name="gmlp_in")
    gl_par = [p["gmlp_ln_g"], p["gmlp_ln_b"]]
    gu, gvn = _rowwise(_f_gelu_ln, [z2], gl_par, (F32, F32), tr=TR, name="gmlp_gelu_ln")
    causal = jnp.tril(jnp.ones((GMLP_CHUNK, GMLP_CHUNK), dtype=bool))
    ws_b = jnp.where(causal[None], p["gmlp_w_s"][0], 0.0).astype(BF16)
    bcol = jnp.pad(p["gmlp_b_s"][0].T, ((0, 0), (0, 128 - GMLP_HEADS)))
    uv = _gmlp_fwd(gu, gvn, ws_b, bcol, nck=4, name="gmlp_spatial")
    x_mid2 = _mm(uv, p["gmlp_w_out"], epi=lambda acc, res: (acc + res,), extras=(x2,), name="gmlp_out")
    x3_, mlp_saved2 = _mlp_fwd(x_mid2, row(p["norm_mlp"][2]), mlp_in[2], mlp_out[2], 2)

    g3 = row(p["norm_mix"][3])
    h3 = _rms_fwd(x3_, g3, "rms_mix_3")
    fetch(3, h3)
    qkv = _mm(h3, p["attn_w_qkv"], out_dtypes=(BF16,), tn=1152, name="attn_qkv")
    ng = len(ATT_DILS)
    qkv4 = qkv.reshape(T, 3, ng, ATT_W)
    att_in, o_tok, l_tok, lses = [], [], [], []
    for gi, dil in enumerate(ATT_DILS):
        if dil == 1:
            arr, offs = qkv, tuple((i * ng + gi) * PAIRS for i in range(3))
        else:
            arr, offs = _deinterleave(qkv4[:, :, gi].reshape(T, 3 * ATT_W), B, S, dil), (0, PAIRS, 2 * PAIRS)
        att_in.append((arr, offs))
        og, lg = _att_fwd(arr, offs, nb=S // dil // ATT_BLK, nbk=8, name=f"attn_fwd_{gi}")
        lses.append(lg)
        o_tok.append(_interleave(og, B, S, dil))
        l_tok.append(_stats_to_tokens(lg, B, S, dil))
    merged2 = _rowwise(_f_merge, o_tok + l_tok, [], (BF16,), tr=TR, name="attn_merge")[0]
    x_mid3 = _mm(merged2, p["attn_w_o"], epi=lambda acc, res: (acc + res,), extras=(x3_,), name="attn_out")
    x4, mlp_saved3 = _mlp_fwd(x_mid3, row(p["norm_mlp"][3]), mlp_in[3], mlp_out[3], 3)

    loss_part, gx, gxb, dgf = _loss_head(x4, tgt3.reshape(T, D), row(p["norm_final"]), tr=TR, name="loss_head")
    grads["norm_final"] = dgf.reshape(-1)
    d_norm_mix, d_norm_mlp = [None] * 4, [None] * 4
    Dq = D // N_CHIPS

    gx, gxb, d_norm_mlp[3], mlp_hi = _mlp_bwd(
        gx, gxb, x_mid3, row(p["norm_mlp"][3]), mlp_in[3], mlp_out[3], mlp_saved3, 3, nl, None)
    dmerged = _mm(gxb, p["attn_w_o"], tb=True, name="attn_dmerged")
    grads["attn_w_o"] = _mm(merged2, gxb, ta=True, tm=256, tn=256, tk=2048, out_dtypes=(BF16,), out=_grad_cols(ATT_W, Dq), name="attn_dwo")
    dml, _ = _rowwise_vjp(_f_merge, o_tok + l_tok, [], [dmerged], [F32] * 6, tr=TR, name="attn_merge_bwd")
    pieces = [[None] * ng for _ in range(3)]
    for gi, dil in enumerate(ATT_DILS):
        arr, offs = att_in[gi]
        dqkv_g = _att_bwd(arr, offs, _deinterleave(dml[gi], B, S, dil), lses[gi],
                          _stats_from_tokens(dml[ng + gi], B, S, dil), nb=S // dil // ATT_BLK, nbk=8,
                          name=f"attn_bwd_{gi}")
        for i in range(3):
            pieces[i][gi] = _interleave(dqkv_g[i], B, S, dil)
    dqkv = jnp.concatenate([pieces[i][gi] for i in range(3) for gi in range(ng)], axis=1)
    qkv_w = 3 * ng * ATT_W
    grads["attn_w_qkv"] = _mm(h3, dqkv, ta=True, tm=512, tn=1152, tk=2048, out_dtypes=(BF16,), out=_grad_cols(D, qkv_w // N_CHIPS),
                              name="attn_dwqkv")
    tok = grads_done({n: grads[n] for n in ("attn_w_qkv", "attn_w_o")})
    dh3 = _mm(dqkv, p["attn_w_qkv"], tb=True, tk=1152, name="attn_dh")
    gx, gxb, d_norm_mix[3] = _rms_bwd(x3_, g3, dh3, gx, "rms_mix_bwd_3", after=tok)

    gx, gxb, d_norm_mlp[2], mlp_hi = _mlp_bwd(
        gx, gxb, x_mid2, row(p["norm_mlp"][2]), mlp_in[2], mlp_out[2], mlp_saved2, 2, nl, mlp_hi)
    tok = grads_done({"mlp_w_in": (1, mlp_hi[0]), "mlp_w_out": (1, mlp_hi[1])})
    duv = _mm(gxb, p["gmlp_w_out"], tb=True, after=tok, name="gmlp_duv")
    grads["gmlp_w_out"] = _mm(uv, gxb, ta=True, tm=128, tn=1024, tk=2048, out_dtypes=(BF16,), out=_grad_rows(Dq, D), name="gmlp_dwout")
    du, dvn, dws, dbcol = _gmlp_bwd(duv, gu, gvn, ws_b, bcol, nck=4, name="gmlp_spatial_bwd")
    grads["gmlp_w_s"] = jnp.where(causal[None], dws, 0.0)[None]
    grads["gmlp_b_s"] = dbcol[:, :GMLP_HEADS].T[None]
    (dz2,), (dlg, dlb_) = _rowwise_vjp(_f_gelu_ln, [z2], gl_par, [du, dvn], [BF16], tr=TR, name="gmlp_gelu_ln_bwd")
    grads["gmlp_ln_g"], grads["gmlp_ln_b"] = dlg, dlb_
    grads["gmlp_w_in"] = _mm(h2, dz2, ta=True, tm=512, tn=512, tk=2048, out_dtypes=(BF16,), out=_grad_cols(D, 2 * Dq), name="gmlp_dwin")
    tok = grads_done({n: grads[n] for n in ("gmlp_w_in", "gmlp_w_out")})
    dh2 = _mm(dz2, p["gmlp_w_in"], tb=True, name="gmlp_dh")
    gx, gxb, d_norm_mix[2] = _rms_bwd(x2, g2, dh2, gx, "rms_mix_bwd_2", after=tok)

    gx, gxb, d_norm_mlp[1], mlp_lo = _mlp_bwd(
        gx, gxb, x_mid1, row(p["norm_mlp"][1]), mlp_in[1], mlp_out[1], mlp_saved1, 1, nl, None)
    dqc = _mm(gxb, p["conv_w_pw2"], tb=True, name="conv_dq")
    grads["conv_w_pw2"] = _mm(qc, gxb, ta=True, tm=128, tn=1024, tk=2048, out_dtypes=(BF16,), out=_grad_rows(Dq, D), name="conv_dwpw2")
    _, (db2,) = _rowwise_vjp(lambda t, b: (t + b,), [gx], [p["conv_b_pw2"]], [gx], [None], tr=TR, name="conv_db2")
    grads["conv_b_pw2"] = db2
    (dyc,), (dbdw, dcg, dcb) = _rowwise_vjp(_f_ln_silu, [yc], ln_par, [dqc], [F32], tr=TR, name="conv_ln_silu_bwd")
    grads["conv_b_dw"], grads["conv_ln_g"], grads["conv_ln_b"] = dbdw, dcg, dcb
    dyp = jnp.pad(dyc.reshape(B, S, D), ((0, 0), (0, CONV_HALO), (0, 0)))
    dzg, dwdw = _conv_bwd(zp, dyp, w_dw, R=256, tc=128, name="conv_dw_bwd")
    grads["conv_w_dw"] = dwdw[:CONV_WIDTH][None]
    (dz1,), (db1,) = _rowwise_vjp(_f_bias_glu, [z1], [p["conv_b_pw1"]], [dzg.reshape(T, D)], [BF16], tr=TR,
                                  name="conv_glu_bwd")
    grads["conv_b_pw1"] = db1
    grads["conv_w_pw1"] = _mm(h1, dz1, ta=True, tm=512, tn=512, tk=2048, out_dtypes=(BF16,), out=_grad_cols(D, 2 * Dq), name="conv_dwpw1")
    tok = grads_done({n: grads[n] for n in ("conv_w_pw1", "conv_w_pw2")})
    dh1 = _mm(dz1, p["conv_w_pw1"], tb=True, name="conv_dh")
    gx, gxb, d_norm_mix[1] = _rms_bwd(x1, g1, dh1, gx, "rms_mix_bwd_1", after=tok)

    gx, gxb, d_norm_mlp[0], mlp_lo = _mlp_bwd(
        gx, gxb, x_mid0, row(p["norm_mlp"][0]), mlp_in[0], mlp_out[0], mlp_saved0, 0, nl, mlp_lo)
    tok = grads_done({"mlp_w_in": (0, mlp_lo[0]), "mlp_w_out": (0, mlp_lo[1])})
    (dz0,), _ = _rowwise_vjp(_f_glu, [z0], [], [gx], [BF16], after=tok, tr=TR, name="s5_glu_bwd")
    grads["ssm_w_glu"] = _mm(yb, dz0, ta=True, tm=512, tn=512, tk=2048, out_dtypes=(BF16,), out=_grad_cols(D, 2 * Dq), name="s5_dwglu")
    tok = grads_done({"ssm_w_glu": grads["ssm_w_glu"]})
    dypre = _mm(dz0, w_glu, tb=True, epi=lambda acc, yp: (jax.vjp(lambda t: jax.nn.gelu(t), yp)[1](acc)[0],),
                extras=(ypre,), name="s5_dypre")
    dh0, dbre, dbim, dcre, dcim, dabr, dabi, dd = _s5_bwd(
        dypre, h0, sxr, sxi, ser, sei, abr, abi, bre_b, bim_b, cre_b, cim_b, d_skip, B=B, L=S5_CHUNK, name="s5_bwd")
    s5_grads = s5_vjp((dabr, dabi, dbre, dbim, dcre, dcim))
    for nm, gv in zip(("ssm_a_re", "ssm_a_im", "ssm_b_re", "ssm_b_im", "ssm_c_re", "ssm_c_im", "ssm_log_dt"), s5_grads):
        grads[nm] = gv[None]
    grads["ssm_d"] = dd
    gx, _, d_norm_mix[0] = _rms_bwd(x, g0, dh0, gx, "rms_mix_bwd_0", after=tok)

    grads["norm_mix"] = jnp.concatenate(d_norm_mix, axis=0)
    grads["norm_mlp"] = jnp.concatenate(d_norm_mlp, axis=0)
    grads["mlp_w_in"], grads["mlp_w_out"] = (mlp_lo[0], mlp_hi[0]), (mlp_lo[1], mlp_hi[1])
    return loss_part, gx.reshape(B, S, D), grads


WEIGHTS = ['norm_mix', 'norm_mlp', 'norm_final', 'ssm_a_re', 'ssm_a_im', 'ssm_b_re', 'ssm_b_im', 'ssm_c_re',
           'ssm_c_im', 'ssm_d', 'ssm_log_dt', 'ssm_w_glu', 'conv_w_pw1', 'conv_b_pw1', 'conv_w_dw', 'conv_b_dw',
           'conv_ln_g', 'conv_ln_b', 'conv_w_pw2', 'conv_b_pw2', 'gmlp_w_in', 'gmlp_ln_g', 'gmlp_ln_b', 'gmlp_w_s',
           'gmlp_b_s', 'gmlp_w_out', 'attn_w_qkv', 'attn_w_o', 'mlp_w_in', 'mlp_w_out']
BIG_AXIS = {'ssm_w_glu': -1, 'conv_w_pw1': -1, 'conv_w_pw2': -2, 'gmlp_w_in': -1, 'gmlp_w_out': -2,
            'attn_w_qkv': -1, 'attn_w_o': -1, 'mlp_w_in': -1, 'mlp_w_out': -2}
BIG = list(BIG_AXIS)
LAYER_MIXER_WEIGHTS = (('ssm_w_glu',), ('conv_w_pw1', 'conv_w_pw2'), ('gmlp_w_in', 'gmlp_w_out'), ('attn_w_qkv', 'attn_w_o'))
SMALL_SHARDED = ['conv_b_pw1', 'conv_w_dw', 'conv_b_dw', 'conv_ln_g', 'conv_ln_b', 'conv_b_pw2', 'gmlp_ln_g', 'gmlp_ln_b']
SMALL_REPL = [n for n in WEIGHTS if n not in BIG_AXIS and n not in SMALL_SHARDED]
SMALL = SMALL_REPL + SMALL_SHARDED
LANES = 128
FLAT_COLS = 1024


def _pack(arrs, cols, row_mult):
    flat = jnp.concatenate([a.reshape(-1) for a in arrs])
    per = cols * row_mult
    n = -(-flat.shape[0] // per) * per
    return jnp.pad(flat, (0, n - flat.shape[0])).reshape(n // cols, cols)


def _unpack(flat2d, shapes):
    flat = flat2d.reshape(-1)
    out, off = [], 0
    for s in shapes:
        n = int(np.prod(s))
        out.append(flat[off:off + n].reshape(s))
        off += n
    return out


def _as_halves(shard):
    if shard.shape[0] == 1:
        shard = shard[0]
    return shard.reshape((2, shard.shape[0] // 2) + shard.shape[1:])


def _stored_weight(name, arr):
    kind = "cols" if BIG_AXIS[name] == -1 else "rows"
    if arr.shape[1] > 1:
        return [_Stored(arr, kind, lead=(li,)) for li in range(arr.shape[1])]
    arr = arr[:, 0]
    if kind == "rows":
        return arr.reshape(-1, arr.shape[-1])
    return _Stored(arr, kind)


def kernel(x, norm_mix, norm_mlp, norm_final, ssm_a_re, ssm_a_im, ssm_b_re, ssm_b_im, ssm_c_re, ssm_c_im, ssm_d, ssm_log_dt, ssm_w_glu, conv_w_pw1, conv_b_pw1, conv_w_dw, conv_b_dw, conv_ln_g, conv_ln_b, conv_w_pw2, conv_b_pw2, gmlp_w_in, gmlp_ln_g, gmlp_ln_b, gmlp_w_s, gmlp_b_s, gmlp_w_out, attn_w_qkv, attn_w_o, mlp_w_in, mlp_w_out, loss_target, m_norm_mix, m_norm_mlp, m_norm_final, m_ssm_a_re, m_ssm_a_im, m_ssm_b_re, m_ssm_b_im, m_ssm_c_re, m_ssm_c_im, m_ssm_d, m_ssm_log_dt, m_ssm_w_glu, m_conv_w_pw1, m_conv_b_pw1, m_conv_w_dw, m_conv_b_dw, m_conv_ln_g, m_conv_ln_b, m_conv_w_pw2, m_conv_b_pw2, m_gmlp_w_in, m_gmlp_ln_g, m_gmlp_ln_b, m_gmlp_w_s, m_gmlp_b_s, m_gmlp_w_out, m_attn_w_qkv, m_attn_w_o, m_mlp_w_in, m_mlp_w_out, v_norm_mix, v_norm_mlp, v_norm_final, v_ssm_a_re, v_ssm_a_im, v_ssm_b_re, v_ssm_b_im, v_ssm_c_re, v_ssm_c_im, v_ssm_d, v_ssm_log_dt, v_ssm_w_glu, v_conv_w_pw1, v_conv_b_pw1, v_conv_w_dw, v_conv_b_dw, v_conv_ln_g, v_conv_ln_b, v_conv_w_pw2, v_conv_b_pw2, v_gmlp_w_in, v_gmlp_ln_g, v_gmlp_ln_b, v_gmlp_w_s, v_gmlp_b_s, v_gmlp_w_out, v_attn_w_qkv, v_attn_w_o, v_mlp_w_in, v_mlp_w_out):
    args = dict(locals())
    w = {n: args[n] for n in WEIGHTS}
    m = {n: args["m_" + n] for n in WEIGHTS}
    v = {n: args["v_" + n] for n in WEIGHTS}
    chip = 2 * lax.axis_index("x") + lax.axis_index("y")
    core = lax.axis_index("c")

    big_shapes = [w[n].shape for n in BIG]
    started, token = [], None
    for li, mixer in enumerate(LAYER_MIXER_WEIGHTS):
        names = list(mixer) + ["mlp_w_in", "mlp_w_out"]
        shards = [w[n][0] for n in mixer] + [w["mlp_w_in"][li], w["mlp_w_out"][li]]
        halves = [s.astype(BF16).reshape((2, s.shape[0] // 2) + s.shape[1:]) for s in shards]
        send_sems, recv_sems, halves, lands, token = _gather_start(halves, token, name=f"gather_start_{li}")
        started.append((names, (send_sems, recv_sems, halves, lands)))

    def layer_weights(li, after):
        names, st = started[li]
        halves, lands = _gather_wait(st, after, name=f"gather_wait_{li}")
        out = {}
        for n, h, arr in zip(names, halves, lands):
            arr = lax.dynamic_update_index_in_dim(arr, h, chip, axis=0)
            arr = arr.reshape((N_CHIPS, arr.shape[1] * arr.shape[2]) + arr.shape[3:])
            if BIG_AXIS[n] == -1:
                out[n] = _Stored(arr, "cols")
            else:
                out[n] = arr.reshape(-1, arr.shape[-1])
        return out

    p = {}
    sm_shapes = [w[n].shape for n in SMALL_SHARDED]
    sflat = _pack([w[n] for n in SMALL_SHARDED], LANES, 8)
    rs = sflat.shape[0]
    sall = _allgather8(sflat, name="gather_small").reshape(8, rs, LANES)
    per_chip = [_unpack(sall[2 * k], sm_shapes) for k in range(N_CHIPS)]
    for i, n in enumerate(SMALL_SHARDED):
        p[n] = jnp.concatenate([per_chip[k][i] for k in range(N_CHIPS)], axis=-1)
    for n in SMALL_REPL:
        p[n] = w[n]
    p['conv_w_dw'] = p['conv_w_dw'][0]

    in_flight, arrived, n_rounds = [], {}, [0]

    def finish_round(after):
        k, names, plans, st = in_flight.pop(0)
        srcs, lands = _reduce_wait(st, plans, after, name=f"grads_wait_{k}")
        for n, plan, src, land in zip(names, plans, srcs, lands):
            arrived.setdefault(n, []).append((plan, src, land))

    def grads_done(group):
        names, srcs, plans, lands = [], [], [], []
        for n, v in group.items():
            if isinstance(v, tuple):
                src, plan = v[1].reshape(1, N_CHIPS, -1, v[1].shape[-1]), ((0, v[0]),)
                while any(n in rd[1] for rd in in_flight):
                    finish_round(src)
            else:
                src, plan = v.reshape(2, N_CHIPS, -1, v.shape[-1]), ((0, 0), (1, 1))
            land = arrived[n][-1][2] if n in arrived else lax.empty((2 * N_CHIPS,) + src.shape[2:], BF16)
            names.append(n), srcs.append(src), plans.append(plan), lands.append(land)
        st = _reduce_start(srcs, lands, plans, None, name=f"grads_start_{n_rounds[0]}")
        in_flight.append((n_rounds[0], names, plans, st[:4]))
        n_rounds[0] += 1
        return st[4]

    loss_part, grad_x, g = _local_step(x, loss_target, p, layer_weights, token, grads_done)
    loss = lax.psum(loss_part[0, 0], ("x", "y", "c"))

    my_id = 2 * chip + core
    while in_flight:
        finish_round(grad_x)
    totals = []
    for n in BIG:
        own = None
        for plan, src, land in arrived[n]:
            slab = lax.dynamic_index_in_dim(src, chip, axis=1, keepdims=False)
            if len(plan) == 2:
                own = lax.dynamic_index_in_dim(slab, core, axis=0, keepdims=False)
            else:
                own = slab[0] if own is None else jnp.where(core == plan[0][1], slab[0], own)
        totals.append(_sum8(arrived[n][-1][2], own, my_id, name="owner_sum_" + n))
    shared = _share_halves(totals, name="grads_share_halves")
    big_grads = {}
    for n, arr, t in zip(BIG, shared, totals):
        arr = lax.dynamic_update_index_in_dim(arr, t[None], core, axis=0)
        big_grads[n] = arr.reshape(w[n].shape)

    small_full_shapes = [g[n].shape for n in SMALL]
    gs = _pack([g[n] for n in SMALL], LANES, 8)
    rg = gs.shape[0]
    gs_all = _allgather8(gs, name="gather_small_grads").reshape(8, rg, LANES)
    gs_sum = _rowwise(lambda *a: (functools.reduce(lambda s, t: s + t, a),), [gs_all[k] for k in range(8)], [], (F32,),
                      tr=rg, name="small_grads_sum")[0]
    small_grads = dict(zip(SMALL, _unpack(gs_sum, small_full_shapes)))
    for n in SMALL:
        small_grads[n] = small_grads[n].reshape(p_shape_full(w[n], -1 if n in SMALL_SHARDED else None))
    for n in SMALL_SHARDED:
        width = w[n].shape[-1]
        small_grads[n] = lax.dynamic_slice_in_dim(small_grads[n], chip * width, width, axis=-1)

    grad, delta, new_m, new_v = {}, {}, {}, {}
    for n in BIG:
        shape = w[n].shape
        two_d = lambda t: t.reshape(-1, shape[-1])
        grad[n] = big_grads[n]
        d_, m_, v_ = _adamw(two_d(w[n]), two_d(grad[n]), two_d(m[n]), two_d(v[n]), name="adamw_" + n)
        delta[n], new_m[n], new_v[n] = d_.reshape(shape), m_.reshape(shape), v_.reshape(shape)
    sm_own_shapes = [w[n].shape for n in SMALL]
    packed = [_pack([src[n] for n in SMALL], LANES, 8) for src in (w, small_grads, m, v)]
    outs = _adamw(*packed, name="adamw_small")
    for dst, flat in zip((delta, new_m, new_v), outs):
        dst.update(dict(zip(SMALL, _unpack(flat, sm_own_shapes))))
    for n in SMALL:
        grad[n] = small_grads[n]

    return (loss, grad_x, *[grad[n] for n in WEIGHTS], *[delta[n] for n in WEIGHTS],
            *[new_m[n] for n in WEIGHTS], *[new_v[n] for n in WEIGHTS])


def p_shape_full(shard, axis):
    s = list(shard.shape)
    if axis is not None:
        s[axis] *= N_CHIPS
    return tuple(s)
```

```python
import functools
import math

import jax
import jax.numpy as jnp
import numpy as np
from jax import lax
from jax.experimental import pallas as pl
from jax.experimental.pallas import tpu as pltpu

F32 = jnp.float32
BF16 = jnp.bfloat16
MESH = pl.DeviceIdType.MESH

EPS = 1e-6
SSM_GROUP = 16
SSM_STATE = 64
CONV_WIDTH = 31
CONV_HALO = 32
GMLP_CHUNK = 128
GMLP_HEADS = 4
ATT_DILS = (1, 4, 16)
ATT_BLK = 128
ATT_HEADS = 8
HEAD_DIM = 64
ATT_W = ATT_HEADS * HEAD_DIM
N_CHIPS = 4
ADAM_LR, ADAM_B1, ADAM_B2, ADAM_EPS, ADAM_WD, ADAM_STEP = 1e-3, 0.9, 0.999, 1e-8, 0.01, 10

VMEM_BYTES_V7X = 64 * 1024 * 1024
VMEM_LIMIT = VMEM_BYTES_V7X - 8 * 1024 * 1024
MASK_VALUE = -1e30
LANE_TILE = 128


def _cp(sem=None):
    return pltpu.CompilerParams(dimension_semantics=sem, vmem_limit_bytes=VMEM_LIMIT)


def _pick_tile(total, target):
    for cand in range(min(target, total) // LANE_TILE * LANE_TILE, 0, -LANE_TILE):
        if total % cand == 0:
            return cand
    return total


class _Stored:
    def __init__(self, arr, kind="plain", lead=()):
        self.arr, self.kind, self.lead = arr, kind, tuple(lead)
        r, c = arr.shape[-2:]
        self.shape = (r, c * N_CHIPS) if kind == "cols" else (r * N_CHIPS, c) if kind == "rows" else (r, c)

    def spec(self, br, bc, rc_of):
        lead, nl = self.lead, len(self.lead)
        if self.kind == "plain":
            return pl.BlockSpec((None,) * nl + (br, bc), lambda i, j, k: (*lead, *rc_of(i, j, k)))
        if self.kind == "cols":
            per = self.arr.shape[-1] // bc
            assert per * bc == self.arr.shape[-1]

            def imap(i, j, k):
                r, c = rc_of(i, j, k)
                return (c // per, *lead, r, c % per)
        else:
            per = self.arr.shape[-2] // br
            assert per * br == self.arr.shape[-2]

            def imap(i, j, k):
                r, c = rc_of(i, j, k)
                return (r // per, *lead, r % per, c)
        return pl.BlockSpec((None,) * (nl + 1) + (br, bc), imap)


class _ColBlocks:
    kind = "colblocks"

    def __init__(self, arr, first, stride, count, width):
        self.arr, self.first, self.stride, self.width = arr, first, stride, width
        self.shape = (arr.shape[0], count * width)

    def spec(self, br, bc, rc_of):
        per = self.width // bc
        assert per * bc == self.width

        def imap(i, j, k):
            r, c = rc_of(i, j, k)
            return (r, (self.first + (c // per) * self.stride) * per + c % per)
        return pl.BlockSpec((br, bc), imap)


def _mm(a, b, *, ta=False, tb=False, out_dtypes=(F32,), tm=1024, tn=1024, tk=1024,
        pro_a=None, pro_b=None, epi=None, extras=(), n_row_sums=0, out=None, after=None, name):
    if ta:
        K, M = a.shape
    else:
        M, K = a.shape
    if not isinstance(b, (_Stored, _ColBlocks)):
        b = _Stored(b)
    N, Kb = b.shape if tb else b.shape[::-1]
    assert K == Kb, (a.shape, b.shape, ta, tb)
    col_unit = b.width if b.kind == "colblocks" else b.arr.shape[-1] if b.kind == "cols" else b.shape[1]
    row_unit = b.arr.shape[-2] if b.kind == "rows" else b.shape[0]
    n_unit, k_unit = (row_unit, col_unit) if tb else (col_unit, row_unit)
    m_unit = M
    if out is not None:
        m_unit, n_unit = out[4], math.gcd(n_unit, out[5])
    tm, tn, tk = _pick_tile(m_unit, tm), _pick_tile(n_unit, tn), _pick_tile(k_unit, tk)
    assert not n_row_sums or tn == N
    nk = K // tk
    a_spec = (pl.BlockSpec((tk, tm), lambda i, j, k: (k, i)) if ta
              else pl.BlockSpec((tm, tk), lambda i, j, k: (i, k)))
    b_spec = b.spec(tn, tk, lambda i, j, k: (j, k)) if tb else b.spec(tk, tn, lambda i, j, k: (k, j))
    ex_specs = []
    for e in extras:
        if e.shape[0] == 1:
            ex_specs.append(pl.BlockSpec((1, tn), lambda i, j, k: (0, j)))
        else:
            assert e.shape == (M, N), (e.shape, M, N)
            ex_specs.append(pl.BlockSpec((tm, tn), lambda i, j, k: (i, j)))
    dims = (((0 if ta else 1,), (1 if tb else 0,)), ((), ()))
    n_ex, n_out = len(extras), len(out_dtypes)
    direct = epi is None and n_out == 1 and out_dtypes[0] == F32
    use_acc = nk > 1 and not direct
    operands, aliases, alias_specs = [a, b.arr, *extras], {}, []
    if after is not None:
        operands.append(after)
        alias_specs.append(pl.BlockSpec(memory_space=pl.ANY))
    if out is None:
        n_tile_out = n_out - n_row_sums
        out_specs = ([pl.BlockSpec((tm, tn), lambda i, j, k: (i, j))] * n_tile_out
                     + [pl.BlockSpec((1, tn), lambda i, j, k: (0, j))] * n_row_sums)
        out_shape = ([jax.ShapeDtypeStruct((M, N), dt) for dt in out_dtypes[:n_tile_out]]
                     + [jax.ShapeDtypeStruct((1, N), dt) for dt in out_dtypes[n_tile_out:]])
    else:
        shape, block_fn, imap_fn, alias = out[:4]
        assert n_out == 1
        out_specs = [pl.BlockSpec(block_fn(tm, tn), imap_fn(tm, tn))]
        out_shape = [jax.ShapeDtypeStruct(shape, out_dtypes[0])]
        if alias is not None:
            operands.append(alias)
            aliases = {len(operands) - 1: 0}
            alias_specs.append(pl.BlockSpec(memory_space=pl.ANY))
    n_in = len(operands)

    def finish(r, ex, outs, first_row_tile):
        res = epi(r, *[e[...] for e in ex]) if epi is not None else (r,)
        n_tile_out = n_out - n_row_sums
        for o, v in zip(outs[:n_tile_out], res):
            o[...] = v.astype(o.dtype)
        for o, v in zip(outs[n_tile_out:], res[n_tile_out:]):
            @pl.when(first_row_tile)
            def _(o=o):
                o[...] = jnp.zeros_like(o)
            o[...] += v

    def body(*refs):
        a_ref, b_ref = refs[:2]
        ex = refs[2:2 + n_ex]
        outs = refs[n_in:n_in + n_out]
        first_row_tile = pl.program_id(0) == 0
        at, bt = a_ref[...], b_ref[...]
        if pro_a is not None:
            at = pro_a(at)
        if pro_b is not None:
            bt = pro_b(bt)
        part = lax.dot_general(at, bt, dims, preferred_element_type=F32)
        if nk == 1:
            finish(part, ex, outs, first_row_tile)
            return
        acc = refs[-1] if use_acc else outs[0]
        k = pl.program_id(2)

        @pl.when(k == 0)
        def _():
            acc[...] = part

        @pl.when(k > 0)
        def _():
            acc[...] += part

        if use_acc:
            @pl.when(k == nk - 1)
            def _():
                finish(acc[...], ex, outs, first_row_tile)

    res = pl.pallas_call(
        body, grid=(M // tm, N // tn, nk),
        in_specs=[a_spec, b_spec] + ex_specs + alias_specs,
        out_specs=out_specs, out_shape=out_shape,
        scratch_shapes=[pltpu.VMEM((tm, tn), F32)] if use_acc else [],
        input_output_aliases=aliases,
        compiler_params=_cp(("arbitrary" if n_row_sums else "parallel", "parallel", "arbitrary")), name=name,
    )(*operands)
    return res[0] if n_out == 1 else res


def _to_bf16(t):
    return t.astype(BF16)


def _pick_rows(total, target):
    for cand in range(min(target, total) // 8 * 8, 0, -8):
        if total % cand == 0:
            return cand
    return total


def _rowwise(f, rows, params, out_dtypes, *, tr, name):
    T = rows[0].shape[0]
    tr = _pick_rows(T, tr)
    nr, npar = len(rows), len(params)
    blk = [jax.ShapeDtypeStruct((tr, r.shape[1]), F32) for r in rows]
    blk += [jax.ShapeDtypeStruct(p.shape, F32) for p in params]
    out_avals = jax.eval_shape(f, *blk)

    def body(*refs):
        res = f(*[r[...].astype(F32) for r in refs[:nr + npar]])
        for o, v in zip(refs[nr + npar:], res):
            o[...] = v.astype(o.dtype)

    out = pl.pallas_call(
        body, grid=(T // tr,),
        in_specs=[pl.BlockSpec((tr, r.shape[1]), lambda i: (i, 0)) for r in rows]
        + [pl.BlockSpec(p.shape, lambda i, nd=p.ndim: (0,) * nd) for p in params],
        out_specs=[pl.BlockSpec((tr, o.shape[1]), lambda i: (i, 0)) for o in out_avals],
        out_shape=[jax.ShapeDtypeStruct((T, o.shape[1]), dt) for o, dt in zip(out_avals, out_dtypes)],
        compiler_params=_cp(("parallel",)), name=name,
    )(*rows, *params)
    return out


def _rowwise_vjp(f, rows, params, cots, drow_dtypes, *, adds=None, after=None, tr, name):
    adds = adds or {}
    T = rows[0].shape[0]
    tr = _pick_rows(T, tr)
    nr, npar, nc = len(rows), len(params), len(cots)
    want, want_dt = [], []
    for i, dt in enumerate(drow_dtypes):
        for one in (dt if isinstance(dt, tuple) else (dt,)):
            if one is not None:
                want.append(i)
                want_dt.append(one)
    add_idx = sorted(set(i for i in want if i in adds))
    add_arrays = [adds[i] for i in add_idx]
    na = len(add_arrays)
    extra = [] if after is None else [after]

    def body(*refs):
        ins = [r[...].astype(F32) for r in refs[:nr + npar]]
        cvals = [r[...].astype(F32) for r in refs[nr + npar:nr + npar + nc]]
        avals = refs[nr + npar + nc:nr + npar + nc + na]
        outs = refs[nr + npar + nc + na + len(extra):]
        _, vjp = jax.vjp(f, *ins)
        grads = vjp(tuple(cvals))
        for o, i in zip(outs[:len(want)], want):
            g = grads[i]
            if i in adds:
                g = g + avals[add_idx.index(i)][...].astype(F32)
            o[...] = g.astype(o.dtype)
        step = pl.program_id(0)
        for o, g in zip(outs[len(want):], grads[nr:]):
            @pl.when(step == 0)
            def _(o=o):
                o[...] = jnp.zeros_like(o)
            o[...] += g

    rspec = lambda r: pl.BlockSpec((tr, r.shape[1]), lambda i: (i, 0))
    pspec = lambda p: pl.BlockSpec(p.shape, lambda i, nd=p.ndim: (0,) * nd)
    out = pl.pallas_call(
        body, grid=(T // tr,),
        in_specs=[rspec(r) for r in rows] + [pspec(p) for p in params] + [rspec(c) for c in cots]
        + [rspec(a) for a in add_arrays] + [pl.BlockSpec(memory_space=pl.ANY)] * len(extra),
        out_specs=[rspec(rows[i]) for i in want] + [pspec(p) for p in params],
        out_shape=[jax.ShapeDtypeStruct(rows[i].shape, dt) for i, dt in zip(want, want_dt)]
        + [jax.ShapeDtypeStruct(p.shape, F32) for p in params],
        compiler_params=_cp(("arbitrary",)), name=name,
    )(*rows, *params, *cots, *add_arrays, *extra)
    return out[:len(want)], out[len(want):]


def _f_rms(x, g):
    return (x * lax.rsqrt(jnp.mean(x * x, axis=-1, keepdims=True) + EPS) * g,)


def _ln(x, g, b):
    mu = jnp.mean(x, axis=-1, keepdims=True)
    var = jnp.mean(jnp.square(x - mu), axis=-1, keepdims=True)
    return (x - mu) * lax.rsqrt(var + EPS) * g + b


def _f_glu(z):
    d = z.shape[1] // 2
    return (z[:, :d] * jax.nn.sigmoid(z[:, d:]),)


def _f_bias_glu(z, b):
    return _f_glu(z + b)


def _f_ln_silu(y, b_dw, g, b):
    return (jax.nn.silu(_ln(y + b_dw, g, b)),)


def _f_gelu_ln(z, g, b):
    d = z.shape[1] // 2
    zz = jax.nn.gelu(z)
    return zz[:, :d], _ln(zz[:, d:], g, b)


def _f_gelu(y):
    return (jax.nn.gelu(y),)


def _f_merge(o0, o1, o2, l0, l1, l2):
    m = jnp.maximum(jnp.maximum(l0, l1), l2)
    e0, e1, e2 = jnp.exp(l0 - m), jnp.exp(l1 - m), jnp.exp(l2 - m)
    s = e0 + e1 + e2
    pair = 2 * HEAD_DIM
    first_head = lax.broadcasted_iota(jnp.int32, (o0.shape[0], pair), 1) < HEAD_DIM
    cols = []
    for hp in range(o0.shape[1] // pair):
        acc = None
        for o, e in ((o0, e0), (o1, e1), (o2, e2)):
            wgt = e / s
            wp = jnp.where(first_head, wgt[:, 2 * hp:2 * hp + 1], wgt[:, 2 * hp + 1:2 * hp + 2])
            term = wp * o[:, hp * pair:(hp + 1) * pair]
            acc = term if acc is None else acc + term
        cols.append(acc)
    return (jnp.concatenate(cols, axis=1),)


def _f_add(a, b):
    return (a + b,)


def _loss_head(x, tgt, g, *, tr, name):
    T, D = x.shape
    tr = min(tr, T)

    def f(xv, gv, tv):
        y = _f_rms(xv, gv)[0]
        return 0.5 * jnp.mean(jnp.square(y - tv), axis=-1, keepdims=True)

    def body(x_ref, t_ref, g_ref, loss_ref, dx_ref, dxb_ref, dg_ref):
        tv = t_ref[...]
        l, vjp = jax.vjp(lambda xv, gv: f(xv, gv, tv), x_ref[...], g_ref[...])
        dx, dg = vjp(jnp.ones_like(l))
        dx_ref[...] = dx
        dxb_ref[...] = dx.astype(BF16)

        @pl.when(pl.program_id(0) == 0)
        def _():
            loss_ref[...] = jnp.zeros_like(loss_ref)
            dg_ref[...] = jnp.zeros_like(dg_ref)

        loss_ref[...] += jnp.sum(l)
        dg_ref[...] += dg

    return pl.pallas_call(
        body, grid=(T // tr,),
        in_specs=[pl.BlockSpec((tr, D), lambda i: (i, 0)), pl.BlockSpec((tr, D), lambda i: (i, 0)),
                  pl.BlockSpec((1, D), lambda i: (0, 0))],
        out_specs=[pl.BlockSpec((1, 128), lambda i: (0, 0)), pl.BlockSpec((tr, D), lambda i: (i, 0)),
                   pl.BlockSpec((tr, D), lambda i: (i, 0)), pl.BlockSpec((1, D), lambda i: (0, 0))],
        out_shape=[jax.ShapeDtypeStruct((1, 128), F32), jax.ShapeDtypeStruct((T, D), F32),
                   jax.ShapeDtypeStruct((T, D), BF16), jax.ShapeDtypeStruct((1, D), F32)],
        compiler_params=_cp(("arbitrary",)), name=name,
    )(x, tgt, g)


def _adamw(w, g, m, v, *, name):
    R, C = w.shape
    tr = _pick_rows(R, max(8, 2 * 1024 * 1024 // (4 * C)))
    c1 = 1.0 - ADAM_B1 ** ADAM_STEP
    c2 = 1.0 - ADAM_B2 ** ADAM_STEP

    def body(w_ref, g_ref, m_ref, v_ref, d_ref, nm_ref, nv_ref):
        gv = g_ref[...]
        nm = ADAM_B1 * m_ref[...] + (1.0 - ADAM_B1) * gv
        nv = ADAM_B2 * v_ref[...] + (1.0 - ADAM_B2) * jnp.square(gv)
        nm_ref[...] = nm
        nv_ref[...] = nv
        d_ref[...] = -ADAM_LR * ((nm / c1) / (jnp.sqrt(nv / c2) + ADAM_EPS) + ADAM_WD * w_ref[...])

    spec = pl.BlockSpec((tr, C), lambda i: (i, 0))
    return pl.pallas_call(
        body, grid=(R // tr,), in_specs=[spec] * 4, out_specs=[spec] * 3,
        out_shape=[jax.ShapeDtypeStruct((R, C), F32)] * 3,
        compiler_params=_cp(("parallel",)), name=name,
    )(w, g, m, v)


def _s5_prep(a_re, a_im, b_re, b_im, c_re, c_im, log_dt):
    G, N = a_re.shape
    P = b_re.shape[2]
    gpb = 128 // P
    nblk = G // gpb
    dt = jnp.exp(log_dt)[:, None]
    mag = jnp.exp(a_re * dt)
    abr, abi = mag * jnp.cos(a_im * dt), mag * jnp.sin(a_im * dt)
    den = a_re * a_re + a_im * a_im
    nr, ni = abr - 1.0, abi
    qr, qi = (nr * a_re + ni * a_im) / den, (ni * a_re - nr * a_im) / den
    bbr = qr[..., None] * b_re - qi[..., None] * b_im
    bbi = qr[..., None] * b_im + qi[..., None] * b_re
    eye = jnp.eye(gpb, dtype=F32)

    def expand_b(t):
        t = t.reshape(nblk, gpb, N, P).transpose(0, 1, 3, 2)
        return (t[:, :, :, None, :] * eye[None, :, None, :, None]).reshape(nblk, gpb * P, gpb * N)

    def expand_c(t):
        t = t.reshape(nblk, gpb, P, N).transpose(0, 1, 3, 2)
        return (t[:, :, :, None, :] * eye[None, :, None, :, None]).reshape(nblk, gpb * N, gpb * P)

    return (abr.reshape(1, G * N), abi.reshape(1, G * N), expand_b(bbr), expand_b(bbi),
            expand_c(c_re), expand_c(c_im))


def _s5_fwd(h, abr, abi, bre, bim, cre, cim, d, *, B, L, name):
    T, D = h.shape
    S = T // B
    L = min(L, S)
    nc = S // L
    nblk, cb, sb = bre.shape
    GN = abr.shape[1]

    def body(h_ref, ar_ref, ai_ref, bre_ref, bim_ref, cre_ref, cim_ref, d_ref,
             y_ref, yb_ref, xr_ref, xi_ref, er_ref, ei_ref, sr, si, car, cai):
        ci = pl.program_id(1)

        @pl.when(ci == 0)
        def _():
            car[...] = jnp.zeros_like(car)
            cai[...] = jnp.zeros_like(cai)

        for j in range(nblk):
            u = h_ref[:, j * cb:(j + 1) * cb]
            sr[:, j * sb:(j + 1) * sb] = jnp.dot(u, bre_ref[j], preferred_element_type=F32)
            si[:, j * sb:(j + 1) * sb] = jnp.dot(u, bim_ref[j], preferred_element_type=F32)
        ar, ai = ar_ref[...], ai_ref[...]

        def step(t, carry):
            pr, pi = carry
            nr = ar * pr - ai * pi + sr[pl.ds(t, 1), :]
            ni = ar * pi + ai * pr + si[pl.ds(t, 1), :]
            sr[pl.ds(t, 1), :] = nr
            si[pl.ds(t, 1), :] = ni
            return nr, ni

        pr, pi = lax.fori_loop(0, L, step, (car[...], cai[...]), unroll=4)
        car[...] = pr
        cai[...] = pi
        er_ref[0] = pr
        ei_ref[0] = pi
        for j in range(nblk):
            xr = sr[:, j * sb:(j + 1) * sb].astype(BF16)
            xi = si[:, j * sb:(j + 1) * sb].astype(BF16)
            xr_ref[:, j * sb:(j + 1) * sb] = xr
            xi_ref[:, j * sb:(j + 1) * sb] = xi
            y = (jnp.dot(xr, cre_ref[j], preferred_element_type=F32)
                 - jnp.dot(xi, cim_ref[j], preferred_element_type=F32))
            u = h_ref[:, j * cb:(j + 1) * cb].astype(F32)
            y = y + d_ref[:, j * cb:(j + 1) * cb] * u
            y_ref[:, j * cb:(j + 1) * cb] = y
            yb_ref[:, j * cb:(j + 1) * cb] = jax.nn.gelu(y).astype(BF16)

    tok = lambda w: pl.BlockSpec((L, w), lambda b, c: (b * nc + c, 0))
    whole = lambda p: pl.BlockSpec(p.shape, lambda b, c, nd=p.ndim: (0,) * nd)
    end = pl.BlockSpec((1, 1, GN), lambda b, c: (b * nc + c, 0, 0))
    return pl.pallas_call(
        body, grid=(B, nc),
        in_specs=[tok(D)] + [whole(p) for p in (abr, abi, bre, bim, cre, cim, d)],
        out_specs=[tok(D), tok(D), tok(GN), tok(GN), end, end],
        out_shape=[jax.ShapeDtypeStruct((T, D), F32), jax.ShapeDtypeStruct((T, D), BF16),
                   jax.ShapeDtypeStruct((T, GN), BF16),
                   jax.ShapeDtypeStruct((T, GN), BF16), jax.ShapeDtypeStruct((B * nc, 1, GN), F32),
                   jax.ShapeDtypeStruct((B * nc, 1, GN), F32)],
        scratch_shapes=[pltpu.VMEM((L, GN), F32), pltpu.VMEM((L, GN), F32),
                        pltpu.VMEM((1, GN), F32), pltpu.VMEM((1, GN), F32)],
        compiler_params=_cp(("arbitrary", "arbitrary")), name=name,
    )(h, abr, abi, bre, bim, cre, cim, d)


def _s5_bwd(dy, h, xr, xi, er, ei, abr, abi, bre, bim, cre, cim, d, *, B, L, name):
    T, D = h.shape
    S = T // B
    L = min(L, S)
    nc = S // L
    nblk, cb, sb = bre.shape
    GN = abr.shape[1]
    dims_nt = (((1,), (1,)), ((), ()))
    dims_tn = (((0,), (0,)), ((), ()))

    def body(dy_ref, h_ref, xr_ref, xi_ref, er_ref, ei_ref, ar_ref, ai_ref, bre_ref, bim_ref,
             cre_ref, cim_ref, d_ref,
             dh_ref, dbre_ref, dbim_ref, dcre_ref, dcim_ref, dar_ref, dai_ref, dd_ref,
             lr, li, car, cai):
        b, cstep = pl.program_id(0), pl.program_id(1)
        ci = nc - 1 - cstep

        @pl.when((b == 0) & (cstep == 0))
        def _():
            for r in (dbre_ref, dbim_ref, dcre_ref, dcim_ref, dar_ref, dai_ref, dd_ref):
                r[...] = jnp.zeros_like(r)

        @pl.when(cstep == 0)
        def _():
            car[...] = jnp.zeros_like(car)
            cai[...] = jnp.zeros_like(cai)

        for j in range(nblk):
            dyj = dy_ref[:, j * cb:(j + 1) * cb].astype(BF16)
            lr[:, j * sb:(j + 1) * sb] = lax.dot_general(dyj, cre_ref[j], dims_nt, preferred_element_type=F32)
            li[:, j * sb:(j + 1) * sb] = -lax.dot_general(dyj, cim_ref[j], dims_nt, preferred_element_type=F32)
        ar, ai = ar_ref[...], ai_ref[...]

        def step(s, carry):
            t = L - 1 - s
            pr, pi = carry
            nr = lr[pl.ds(t, 1), :] + ar * pr + ai * pi
            ni = li[pl.ds(t, 1), :] - ai * pr + ar * pi
            lr[pl.ds(t, 1), :] = nr
            li[pl.ds(t, 1), :] = ni
            return nr, ni

        pr, pi = lax.fori_loop(0, L, step, (car[...], cai[...]), unroll=4)
        car[...] = pr
        cai[...] = pi
        has_prev = (ci > 0).astype(F32)
        first_row = lax.broadcasted_iota(jnp.int32, (L, sb), 0) == 0
        for j in range(nblk):
            cs = slice(j * cb, (j + 1) * cb)
            ss = slice(j * sb, (j + 1) * sb)
            lrj, lij = lr[:, ss], li[:, ss]
            xrj, xij = xr_ref[:, ss], xi_ref[:, ss]
            pr_j = jnp.where(first_row, er_ref[0][:, ss] * has_prev, pltpu.roll(xrj.astype(F32), 1, 0))
            pi_j = jnp.where(first_row, ei_ref[0][:, ss] * has_prev, pltpu.roll(xij.astype(F32), 1, 0))
            dar_ref[:, ss] += jnp.sum(lrj * pr_j + lij * pi_j, axis=0, keepdims=True)
            dai_ref[:, ss] += jnp.sum(lij * pr_j - lrj * pi_j, axis=0, keepdims=True)
            lrb, lib = lrj.astype(BF16), lij.astype(BF16)
            hj = h_ref[:, cs]
            dyf = dy_ref[:, cs]
            dyj = dyf.astype(BF16)
            dbre_ref[j] += lax.dot_general(hj, lrb, dims_tn, preferred_element_type=F32)
            dbim_ref[j] += lax.dot_general(hj, lib, dims_tn, preferred_element_type=F32)
            dcre_ref[j] += lax.dot_general(xrj, dyj, dims_tn, preferred_element_type=F32)
            dcim_ref[j] -= lax.dot_general(xij, dyj, dims_tn, preferred_element_type=F32)
            du = (lax.dot_general(lrb, bre_ref[j], dims_nt, preferred_element_type=F32)
                  + lax.dot_general(lib, bim_ref[j], dims_nt, preferred_element_type=F32))
            dh_ref[:, cs] = du + d_ref[:, cs] * dyf
            dd_ref[:, cs] += jnp.sum(dyf * hj.astype(F32), axis=0, keepdims=True)

    tok = lambda w: pl.BlockSpec((L, w), lambda b, c: (b * nc + nc - 1 - c, 0))
    whole = lambda p: pl.BlockSpec(p.shape, lambda b, c, nd=p.ndim: (0,) * nd)
    prev_end = pl.BlockSpec((1, 1, GN), lambda b, c: (b * nc + jnp.maximum(nc - 2 - c, 0), 0, 0))
    params = (abr, abi, bre, bim, cre, cim, d)
    acc_shapes = [bre.shape, bim.shape, cre.shape, cim.shape, abr.shape, abi.shape, d.shape]
    out = pl.pallas_call(
        body, grid=(B, nc),
        in_specs=[tok(D), tok(D), tok(GN), tok(GN), prev_end, prev_end] + [whole(p) for p in params],
        out_specs=[tok(D)] + [pl.BlockSpec(s, lambda b, c, nd=len(s): (0,) * nd) for s in acc_shapes],
        out_shape=[jax.ShapeDtypeStruct((T, D), F32)] + [jax.ShapeDtypeStruct(s, F32) for s in acc_shapes],
        scratch_shapes=[pltpu.VMEM((L, GN), F32), pltpu.VMEM((L, GN), F32),
                        pltpu.VMEM((1, GN), F32), pltpu.VMEM((1, GN), F32)],
        compiler_params=_cp(("arbitrary", "arbitrary")), name=name,
    )(dy, h, xr, xi, er, ei, *params)
    return out


def _conv_fwd(zp, w, *, R, tc, name):
    B, SP, C = zp.shape
    S = SP - CONV_HALO
    R, tc = min(R, S), min(tc, C)

    def body(z_ref, w_ref, y_ref):
        def chunk(ci, _):
            start = pl.multiple_of(ci * R, 8)
            ze = z_ref[pl.ds(start, R + CONV_HALO), :]
            acc = jnp.zeros((R, tc), F32)
            for m in range(CONV_WIDTH):
                k = CONV_WIDTH - 1 - m
                sh = ze if m == 0 else pltpu.roll(ze, m, 0)
                acc = acc + w_ref[k:k + 1, :] * sh[CONV_HALO:, :]
            y_ref[pl.ds(start, R), :] = acc
            return 0

        lax.fori_loop(0, S // R, chunk, 0)

    return pl.pallas_call(
        body, grid=(B, C // tc),
        in_specs=[pl.BlockSpec((None, SP, tc), lambda b, c: (b, 0, c)),
                  pl.BlockSpec((32, tc), lambda b, c: (0, c))],
        out_specs=pl.BlockSpec((None, S, tc), lambda b, c: (b, 0, c)),
        out_shape=jax.ShapeDtypeStruct((B, S, C), F32),
        compiler_params=_cp(("parallel", "parallel")), name=name,
    )(zp, w)


def _conv_bwd(zp, dyp, w, *, R, tc, name):
    B, SP, C = zp.shape
    S = SP - CONV_HALO
    R, tc = min(R, S), min(tc, C)

    def body(z_ref, dy_ref, w_ref, dz_ref, dw_ref):
        @pl.when(pl.program_id(1) == 0)
        def _():
            dw_ref[...] = jnp.zeros_like(dw_ref)

        def chunk(ci, _):
            start = pl.multiple_of(ci * R, 8)
            ze = z_ref[pl.ds(start, R + CONV_HALO), :]
            de = dy_ref[pl.ds(start, R + CONV_HALO), :]
            dy = de[:R, :]
            acc = jnp.zeros((R, tc), F32)
            for m in range(CONV_WIDTH):
                k = CONV_WIDTH - 1 - m
                zs = ze if m == 0 else pltpu.roll(ze, m, 0)
                ds_ = de if m == 0 else pltpu.roll(de, R + CONV_HALO - m, 0)
                acc = acc + w_ref[k:k + 1, :] * ds_[:R, :]
                dw_ref[k:k + 1, :] += jnp.sum(dy * zs[CONV_HALO:, :], axis=0, keepdims=True)
            dz_ref[pl.ds(start, R), :] = acc
            return 0

        lax.fori_loop(0, S // R, chunk, 0)

    return pl.pallas_call(
        body, grid=(C // tc, B),
        in_specs=[pl.BlockSpec((None, SP, tc), lambda c, b: (b, 0, c)),
                  pl.BlockSpec((None, SP, tc), lambda c, b: (b, 0, c)),
                  pl.BlockSpec((32, tc), lambda c, b: (0, c))],
        out_specs=[pl.BlockSpec((None, S, tc), lambda c, b: (b, 0, c)),
                   pl.BlockSpec((32, tc), lambda c, b: (0, c))],
        out_shape=[jax.ShapeDtypeStruct((B, S, C), F32), jax.ShapeDtypeStruct((32, C), F32)],
        compiler_params=_cp(("parallel", "arbitrary")), name=name,
    )(zp, dyp, w)


def _gmlp_fwd(u, vn, ws, bcol, *, nck, name):
    T, E = u.shape
    H = ws.shape[0]
    he = E // H
    rows = nck * GMLP_CHUNK
    rows = min(rows, T)
    n_in = rows // GMLP_CHUNK

    def body(u_ref, v_ref, ws_ref, b_ref, o_ref):
        for c in range(n_in):
            rs = slice(c * GMLP_CHUNK, (c + 1) * GMLP_CHUNK)
            for hh in range(H):
                cs = slice(hh * he, (hh + 1) * he)
                v2 = jnp.dot(ws_ref[hh], v_ref[rs, cs].astype(BF16), preferred_element_type=F32)
                v2 = v2 + b_ref[:, hh:hh + 1]
                o_ref[rs, cs] = (u_ref[rs, cs] * v2).astype(o_ref.dtype)

    tok = pl.BlockSpec((rows, E), lambda i: (i, 0))
    return pl.pallas_call(
        body, grid=(T // rows,),
        in_specs=[tok, tok, pl.BlockSpec(ws.shape, lambda i: (0, 0, 0)), pl.BlockSpec(bcol.shape, lambda i: (0, 0))],
        out_specs=tok, out_shape=jax.ShapeDtypeStruct((T, E), BF16),
        compiler_params=_cp(("parallel",)), name=name,
    )(u, vn, ws, bcol)


def _gmlp_bwd(duv, u, vn, ws, bcol, *, nck, name):
    T, E = u.shape
    H = ws.shape[0]
    he = E // H
    rows = min(nck * GMLP_CHUNK, T)
    n_in = rows // GMLP_CHUNK
    dims_nt = (((1,), (1,)), ((), ()))
    dims_tn = (((0,), (0,)), ((), ()))

    def body(g_ref, u_ref, v_ref, ws_ref, b_ref, du_ref, dv_ref, dws_ref, db_ref):
        @pl.when(pl.program_id(0) == 0)
        def _():
            dws_ref[...] = jnp.zeros_like(dws_ref)
            db_ref[...] = jnp.zeros_like(db_ref)

        for c in range(n_in):
            rs = slice(c * GMLP_CHUNK, (c + 1) * GMLP_CHUNK)
            for hh in range(H):
                cs = slice(hh * he, (hh + 1) * he)
                vb = v_ref[rs, cs].astype(BF16)
                v2 = jnp.dot(ws_ref[hh], vb, preferred_element_type=F32) + b_ref[:, hh:hh + 1]
                g = g_ref[rs, cs]
                du_ref[rs, cs] = g * v2
                dv2 = g * u_ref[rs, cs]
                dv2b = dv2.astype(BF16)
                dv_ref[rs, cs] = lax.dot_general(ws_ref[hh], dv2b, dims_tn, preferred_element_type=F32)
                dws_ref[hh] += lax.dot_general(dv2b, vb, dims_nt, preferred_element_type=F32)
                db_ref[:, hh:hh + 1] += jnp.sum(dv2, axis=1, keepdims=True)

    tok = pl.BlockSpec((rows, E), lambda i: (i, 0))
    return pl.pallas_call(
        body, grid=(T // rows,),
        in_specs=[tok, tok, tok, pl.BlockSpec(ws.shape, lambda i: (0, 0, 0)), pl.BlockSpec(bcol.shape, lambda i: (0, 0))],
        out_specs=[tok, tok, pl.BlockSpec(ws.shape, lambda i: (0, 0, 0)), pl.BlockSpec(bcol.shape, lambda i: (0, 0))],
        out_shape=[jax.ShapeDtypeStruct((T, E), F32), jax.ShapeDtypeStruct((T, E), F32),
                   jax.ShapeDtypeStruct(ws.shape, F32), jax.ShapeDtypeStruct(bcol.shape, F32)],
        compiler_params=_cp(("arbitrary",)), name=name,
    )(duv, u, vn, ws, bcol)


PAIRS = ATT_HEADS // 2


def _att_consts():
    ji = lax.broadcasted_iota(jnp.int32, (2 * ATT_BLK, ATT_BLK), 0)
    ii = lax.broadcasted_iota(jnp.int32, (2 * ATT_BLK, ATT_BLK), 1)
    dist = ii + ATT_BLK - ji
    band = (dist >= 0) & (dist <= ATT_BLK)
    cur = ji >= ATT_BLK
    first_head = lax.broadcasted_iota(jnp.int32, (ATT_BLK, 2 * HEAD_DIM), 1) < HEAD_DIM
    return band, cur, first_head


def _att_specs(nbk, offs, nsteps, rev):
    rows = nbk * ATT_BLK
    step = (lambda i: nsteps - 1 - i) if rev else (lambda i: i)
    qoff, koff, voff = offs
    blk = lambda off: pl.BlockSpec((rows, 2 * HEAD_DIM), lambda hp, i: (step(i), off + hp))
    prev = lambda off: pl.BlockSpec((ATT_BLK, 2 * HEAD_DIM), lambda hp, i: (jnp.maximum(step(i) * nbk - 1, 0), off + hp))
    out = pl.BlockSpec((rows, 2 * HEAD_DIM), lambda hp, i: (step(i), hp))
    stat = pl.BlockSpec((2, nbk, ATT_BLK), lambda hp, i: (hp, step(i), 0))
    return [blk(qoff), blk(koff), prev(koff), blk(voff), prev(voff)], out, stat


def _att_fwd(arr, offs, *, nb, nbk, name):
    T = arr.shape[0]
    nbk = min(nbk, T // ATT_BLK)
    nsteps = T // (nbk * ATT_BLK)
    scale = HEAD_DIM ** -0.5
    dims_nt = (((1,), (1,)), ((), ()))
    dims_tn = (((0,), (0,)), ((), ()))

    def body(q_ref, k_ref, kp_ref, v_ref, vp_ref, o_ref, lse_ref):
        i = pl.program_id(1)
        band, cur, first_head = _att_consts()
        for jj in range(nbk):
            rs = slice(jj * ATT_BLK, (jj + 1) * ATT_BLK)
            ps = slice((jj - 1) * ATT_BLK, jj * ATT_BLK)
            has_prev = ((i * nbk + jj) & (nb - 1)) != 0
            valid = band & (cur | has_prev)
            kk = jnp.concatenate([kp_ref[...] if jj == 0 else k_ref[ps, :], k_ref[rs, :]], axis=0)
            vv = jnp.concatenate([vp_ref[...] if jj == 0 else v_ref[ps, :], v_ref[rs, :]], axis=0)
            q = q_ref[rs, :]
            outs = []
            for hd in range(2):
                qh = jnp.where(first_head if hd == 0 else ~first_head, q, jnp.zeros_like(q))
                st = lax.dot_general(kk, qh, dims_nt, preferred_element_type=F32) * scale
                st = jnp.where(valid, st, MASK_VALUE)
                m = jnp.max(st, axis=0, keepdims=True)
                p = jnp.exp(st - m)
                l = jnp.sum(p, axis=0, keepdims=True)
                lse_ref[hd, jj:jj + 1, :] = m + jnp.log(l)
                pn = (p / l).astype(BF16)
                outs.append(lax.dot_general(pn, vv, dims_tn, preferred_element_type=F32))
            o_ref[rs, :] = jnp.where(first_head, outs[0], outs[1])

    ins, out, stat = _att_specs(nbk, offs, nsteps, False)
    return pl.pallas_call(
        body, grid=(PAIRS, nsteps), in_specs=ins, out_specs=[out, stat],
        out_shape=[jax.ShapeDtypeStruct((T, ATT_W), F32), jax.ShapeDtypeStruct((ATT_HEADS, T // ATT_BLK, ATT_BLK), F32)],
        compiler_params=_cp(("parallel", "parallel")), name=name,
    )(arr, arr, arr, arr, arr)


def _att_bwd(arr, offs, do, lse, dlse, *, nb, nbk, name):
    T = arr.shape[0]
    nbk = min(nbk, T // ATT_BLK)
    nsteps = T // (nbk * ATT_BLK)
    scale = HEAD_DIM ** -0.5
    dims_nt = (((1,), (1,)), ((), ()))
    dims_tn = (((0,), (0,)), ((), ()))

    def body(q_ref, k_ref, kp_ref, v_ref, vp_ref, do_ref, lse_ref, dlse_ref, dq_ref, dk_ref, dv_ref, ck, cv):
        step = pl.program_id(1)
        i = nsteps - 1 - step
        band, cur, first_head = _att_consts()

        @pl.when(step == 0)
        def _():
            ck[...] = jnp.zeros_like(ck)
            cv[...] = jnp.zeros_like(cv)

        carry_k, carry_v = ck[...], cv[...]
        for jj in reversed(range(nbk)):
            rs = slice(jj * ATT_BLK, (jj + 1) * ATT_BLK)
            ps = slice((jj - 1) * ATT_BLK, jj * ATT_BLK)
            has_prev = ((i * nbk + jj) & (nb - 1)) != 0
            valid = band & (cur | has_prev)
            kk = jnp.concatenate([kp_ref[...] if jj == 0 else k_ref[ps, :], k_ref[rs, :]], axis=0)
            vv = jnp.concatenate([vp_ref[...] if jj == 0 else v_ref[ps, :], v_ref[rs, :]], axis=0)
            q = q_ref[rs, :]
            dob = do_ref[rs, :].astype(BF16)
            dqs, dkk, dvv = [], None, None
            for hd in range(2):
                sel = first_head if hd == 0 else ~first_head
                qh = jnp.where(sel, q, jnp.zeros_like(q))
                doh = jnp.where(sel, dob, jnp.zeros_like(dob))
                st = lax.dot_general(kk, qh, dims_nt, preferred_element_type=F32) * scale
                st = jnp.where(valid, st, MASK_VALUE)
                p = jnp.exp(st - lse_ref[hd, jj:jj + 1, :])
                dp = lax.dot_general(vv, doh, dims_nt, preferred_element_type=F32)
                delta = jnp.sum(p * dp, axis=0, keepdims=True)
                dsb = (p * (dp - delta + dlse_ref[hd, jj:jj + 1, :]) * scale).astype(BF16)
                dqs.append(lax.dot_general(dsb, kk, dims_tn, preferred_element_type=F32))
                dk_h = jnp.dot(dsb, qh, preferred_element_type=F32)
                dv_h = jnp.dot(p.astype(BF16), doh, preferred_element_type=F32)
                dkk = dk_h if dkk is None else dkk + dk_h
                dvv = dv_h if dvv is None else dvv + dv_h
            dq_ref[rs, :] = jnp.where(first_head, dqs[0], dqs[1]).astype(dq_ref.dtype)
            dk_ref[rs, :] = (dkk[ATT_BLK:] + carry_k).astype(dk_ref.dtype)
            dv_ref[rs, :] = (dvv[ATT_BLK:] + carry_v).astype(dv_ref.dtype)
            carry_k, carry_v = dkk[:ATT_BLK], dvv[:ATT_BLK]
        ck[...] = carry_k
        cv[...] = carry_v

    ins, out, stat = _att_specs(nbk, offs, nsteps, True)
    return pl.pallas_call(
        body, grid=(PAIRS, nsteps), in_specs=ins + [out, stat, stat], out_specs=[out] * 3,
        out_shape=[jax.ShapeDtypeStruct((T, ATT_W), BF16)] * 3,
        scratch_shapes=[pltpu.VMEM((ATT_BLK, 2 * HEAD_DIM), F32), pltpu.VMEM((ATT_BLK, 2 * HEAD_DIM), F32)],
        compiler_params=_cp(("arbitrary", "arbitrary")), name=name,
    )(arr, arr, arr, arr, arr, do, lse, dlse)


def _deinterleave(t, B, S, dil):
    if dil == 1:
        return t
    return t.reshape((B, S // dil, dil) + t.shape[1:]).swapaxes(1, 2).reshape(t.shape)


def _interleave(t, B, S, dil):
    if dil == 1:
        return t
    return t.reshape((B, dil, S // dil) + t.shape[1:]).swapaxes(1, 2).reshape(t.shape)


def _stats_to_tokens(lse, B, S, dil):
    return _interleave(lse.reshape(lse.shape[0], -1).T, B, S, dil)


def _stats_from_tokens(dl, B, S, dil):
    return _deinterleave(dl, B, S, dil).T.reshape(dl.shape[1], -1, ATT_BLK)


def _mesh_pos():
    return lax.axis_index("x"), lax.axis_index("y"), lax.axis_index("c")


def _allgather8(xs, *, name):
    m_per, n = xs.shape

    def body(x_ref, out_ref, send_sems, recv_sems, local_sem):
        x, y, c = _mesh_pos()
        me, sibling = (x, y, c), (x, y, 1 - c)
        chips = [(1 - x, y), (x, 1 - y), (1 - x, 1 - y)]

        def rows(px, py, pc):
            return out_ref.at[pl.ds((4 * px + 2 * py + pc) * m_per, m_per), :]

        def copy(k, block, to, src=None):
            return pltpu.make_async_remote_copy(
                src_ref=rows(*block) if src is None else src, dst_ref=rows(*block),
                send_sem=send_sems.at[k], recv_sem=recv_sems.at[k], device_id=to, device_id_type=MESH)

        mine = pltpu.make_async_copy(x_ref, rows(*me), local_sem)
        mine.start()
        first = [copy(0, me, sibling, src=x_ref)]
        first += [copy(1 + j, me, (*chip, c), src=x_ref) for j, chip in enumerate(chips)]
        for cp in first:
            cp.start()
        passed = [copy(4 + j, (*chip, c), sibling) for j, chip in enumerate(chips)]
        for j, chip in enumerate(chips):
            copy(1 + j, (*chip, c), me).wait_recv()
            passed[j].start()
        copy(0, sibling, me).wait_recv()
        for j, chip in enumerate(chips):
            copy(4 + j, (*chip, 1 - c), me).wait_recv()
        for cp in first + passed:
            cp.wait_send()
        mine.wait()

    return pl.pallas_call(
        body, out_shape=jax.ShapeDtypeStruct((8 * m_per, n), xs.dtype),
        in_specs=[pl.BlockSpec(memory_space=pltpu.VMEM)], out_specs=pl.BlockSpec(memory_space=pltpu.VMEM),
        scratch_shapes=[pltpu.SemaphoreType.DMA((7,)), pltpu.SemaphoreType.DMA((7,)), pltpu.SemaphoreType.DMA],
        compiler_params=pltpu.CompilerParams(vmem_limit_bytes=VMEM_LIMIT), name=name,
    )(xs)


def _hbm_call(body, arrays, out_shapes, n_sems, *, name):
    any_spec = pl.BlockSpec(memory_space=pl.ANY)
    return pl.pallas_call(
        body, out_shape=out_shapes, in_specs=[any_spec] * len(arrays), out_specs=[any_spec] * len(out_shapes),
        scratch_shapes=[pltpu.SemaphoreType.DMA((n_sems,)), pltpu.SemaphoreType.DMA((n_sems,))], name=name,
    )(*arrays)


def _other_chips(x, y):
    return [(1 - x, y), (x, 1 - y), (1 - x, 1 - y)]


def _allgather_chips(ws, *, name):
    n = len(ws)

    def body(*refs):
        ins, outs, (send_sems, recv_sems) = refs[:n], refs[n:2 * n], refs[2 * n:]
        x, y, c = _mesh_pos()
        chips = _other_chips(x, y)

        def copy(a, k, px, py, half, to, src=None):
            slot = outs[a].at[2 * px + py, half]
            return pltpu.make_async_remote_copy(
                src_ref=slot if src is None else src, dst_ref=slot,
                send_sem=send_sems.at[6 * a + k], recv_sem=recv_sems.at[6 * a + k], device_id=to, device_id_type=MESH)

        first = [copy(a, j, x, y, c, (*chip, c), src=ins[a].at[c]) for a in range(n) for j, chip in enumerate(chips)]
        for cp in first:
            cp.start()
        passed = []
        for j, chip in enumerate(chips):
            for a in range(n):
                copy(a, j, *chip, c, (x, y, c)).wait_recv()
                passed.append(copy(a, 3 + j, *chip, c, (x, y, 1 - c)))
                passed[-1].start()
        for j, chip in enumerate(chips):
            for a in range(n):
                copy(a, 3 + j, *chip, 1 - c, (x, y, c)).wait_recv()
        for cp in first + passed:
            cp.wait_send()

    return _hbm_call(body, ws, [jax.ShapeDtypeStruct((N_CHIPS,) + w.shape, w.dtype) for w in ws], 6 * n, name=name)


def _split_start(srcs, lands, after, issue, n_sems, *, name):
    ns, nl = len(srcs), len(lands)
    hbm, sem = pl.BlockSpec(memory_space=pltpu.HBM), pl.BlockSpec(memory_space=pltpu.SEMAPHORE)
    extra = [] if after is None else [after]

    def body(*refs):
        n_in = ns + nl + len(extra)
        send_sems, recv_sems = refs[n_in], refs[n_in + 1]
        issue(refs[:ns], refs[ns:ns + nl], send_sems, recv_sems)
        refs[-1][...] = jnp.zeros_like(refs[-1])

    arrays = [pltpu.with_memory_space_constraint(a, pltpu.HBM) for a in list(srcs) + list(lands)]
    out = pl.pallas_call(
        body, name=name,
        out_shape=(pltpu.SemaphoreType.DMA((n_sems,)), pltpu.SemaphoreType.DMA((n_sems,)),
                   *[pltpu.HBM(a.shape, a.dtype) for a in arrays], jax.ShapeDtypeStruct((8, 128), F32)),
        in_specs=[hbm] * (ns + nl) + [pl.BlockSpec(memory_space=pl.ANY)] * len(extra),
        out_specs=(sem, sem, *[hbm] * (ns + nl), pl.BlockSpec(memory_space=pltpu.VMEM)),
        input_output_aliases={i: 2 + i for i in range(ns + nl)},
        compiler_params=pltpu.CompilerParams(has_side_effects=pltpu.SideEffectType.DATAFLOW_SIDE_EFFECTING),
    )(*arrays, *extra)
    return out[0], out[1], list(out[2:2 + ns]), list(out[2 + ns:2 + ns + nl]), out[-1]


def _split_wait(send_sems, recv_sems, srcs, lands, after, waits, *, name):
    ns, nl = len(srcs), len(lands)
    hbm, sem = pl.BlockSpec(memory_space=pltpu.HBM), pl.BlockSpec(memory_space=pltpu.SEMAPHORE)

    def body(*refs):
        waits(refs[:ns], refs[ns:ns + nl], refs[ns + nl], refs[ns + nl + 1])

    out = pl.pallas_call(
        body, name=name,
        out_shape=tuple(pltpu.HBM(a.shape, a.dtype) for a in list(srcs) + list(lands)),
        in_specs=[hbm] * (ns + nl) + [sem, sem, pl.BlockSpec(memory_space=pl.ANY)],
        out_specs=tuple([hbm] * (ns + nl)),
        input_output_aliases={i: i for i in range(ns + nl)},
        compiler_params=pltpu.CompilerParams(has_side_effects=pltpu.SideEffectType.DATAFLOW_SIDE_EFFECTING),
    )(*srcs, *lands, send_sems, recv_sems, after)
    return list(out[:ns]), list(out[ns:])


def _gather_start(halves, after, *, name):
    n = len(halves)
    lands = [lax.empty((N_CHIPS,) + h.shape, h.dtype) for h in halves]

    def issue(srcs, dsts, send_sems, recv_sems):
        x, y, c = _mesh_pos()
        me = 2 * x + y
        for a in range(n):
            for j, (px, py) in enumerate(_other_chips(x, y)):
                for cc in range(2):
                    pltpu.make_async_remote_copy(
                        src_ref=srcs[a].at[c], dst_ref=dsts[a].at[me, c],
                        send_sem=send_sems.at[6 * a + 2 * j + cc], recv_sem=recv_sems.at[6 * a + 2 * j + c],
                        device_id=(px, py, cc), device_id_type=MESH).start()

    return _split_start(halves, lands, after, issue, 6 * n, name=name)


def _gather_wait(started, after, *, name):
    send_sems, recv_sems, halves, lands = started
    n = len(halves)

    def waits(srcs, dsts, send_sems, recv_sems):
        x, y, c = _mesh_pos()
        me = 2 * x + y
        for a in range(n):
            for j, (px, py) in enumerate(_other_chips(x, y)):
                for cc in range(2):
                    pltpu.make_async_remote_copy(
                        src_ref=srcs[a].at[cc], dst_ref=dsts[a].at[2 * px + py, cc],
                        send_sem=send_sems.at[6 * a + 2 * j + cc], recv_sem=recv_sems.at[6 * a + 2 * j + cc],
                        device_id=(px, py, cc), device_id_type=MESH).wait_recv()
        for a in range(n):
            for j, (px, py) in enumerate(_other_chips(x, y)):
                for cc in range(2):
                    pltpu.make_async_remote_copy(
                        src_ref=srcs[a].at[c], dst_ref=dsts[a].at[me, c],
                        send_sem=send_sems.at[6 * a + 2 * j + cc], recv_sem=recv_sems.at[6 * a + 2 * j + c],
                        device_id=(px, py, cc), device_id_type=MESH).wait_send()

    return _split_wait(send_sems, recv_sems, halves, lands, after, waits, name=name)


def _reduce_plan_loops(plans, chip, c, fn):
    for a, plan in enumerate(plans):
        for h, cc in plan:
            for k in range(N_CHIPS):
                fn(a, h, k, cc, jnp.logical_or(chip != k, c != cc))


def _reduce_start(srcs, lands, plans, after, *, name):
    def issue(src_refs, land_refs, send_sems, recv_sems):
        x, y, c = _mesh_pos()
        chip = 2 * x + y
        my_id = 2 * chip + c

        def send(a, h, k, cc, is_other):
            @pl.when(is_other)
            def _():
                pltpu.make_async_remote_copy(
                    src_ref=src_refs[a].at[h, k], dst_ref=land_refs[a].at[my_id],
                    send_sem=send_sems.at[8 * a + 2 * k + cc], recv_sem=recv_sems.at[8 * a + my_id],
                    device_id=(k // 2, k % 2, cc), device_id_type=MESH).start()

        _reduce_plan_loops(plans, chip, c, send)

    return _split_start(srcs, lands, after, issue, 8 * len(srcs), name=name)


def _reduce_wait(started, plans, after, *, name):
    send_sems, recv_sems, srcs, lands = started

    def waits(src_refs, land_refs, send_sems, recv_sems):
        x, y, c = _mesh_pos()
        chip = 2 * x + y
        my_id = 2 * chip + c
        for a, plan in enumerate(plans):
            for h, cc in plan:
                for s in range(2 * N_CHIPS):
                    @pl.when(jnp.logical_and(c == cc, my_id != s))
                    def _(a=a, h=h, s=s):
                        pltpu.make_async_remote_copy(
                            src_ref=src_refs[a].at[h, 0], dst_ref=land_refs[a].at[s],
                            send_sem=send_sems.at[8 * a + s], recv_sem=recv_sems.at[8 * a + s],
                            device_id=(s // 4, (s // 2) % 2, s % 2), device_id_type=MESH).wait_recv()

        def sent(a, h, k, cc, is_other):
            @pl.when(is_other)
            def _():
                pltpu.make_async_remote_copy(
                    src_ref=src_refs[a].at[h, k], dst_ref=land_refs[a].at[my_id],
                    send_sem=send_sems.at[8 * a + 2 * k + cc], recv_sem=recv_sems.at[8 * a + my_id],
                    device_id=(k // 2, k % 2, cc), device_id_type=MESH).wait_send()

        _reduce_plan_loops(plans, chip, c, sent)

    return _split_wait(send_sems, recv_sems, srcs, lands, after, waits, name=name)


def _sum8(land, own, my_id, *, name):
    n_src, R, C = land.shape
    tr = _pick_rows(R, max(8, 1024 * 1024 // (2 * C)))

    def body(id_ref, *refs):
        own_ref, o_ref = refs[n_src], refs[n_src + 1]
        me = id_ref[0]
        acc = None
        for s in range(n_src):
            term = jnp.where(me == s, own_ref[...], refs[s][...]).astype(F32)
            acc = term if acc is None else acc + term
        o_ref[...] = acc

    return pl.pallas_call(
        body, out_shape=jax.ShapeDtypeStruct((R, C), F32),
        grid_spec=pltpu.PrefetchScalarGridSpec(
            num_scalar_prefetch=1, grid=(R // tr,),
            in_specs=[pl.BlockSpec((None, tr, C), lambda i, idr, s=s: (s, i, 0)) for s in range(n_src)]
            + [pl.BlockSpec((tr, C), lambda i, idr: (i, 0))],
            out_specs=pl.BlockSpec((tr, C), lambda i, idr: (i, 0))),
        compiler_params=_cp(("parallel",)), name=name,
    )(my_id.reshape(1).astype(jnp.int32), *([land] * n_src), own)


def _swap_halves(gs, *, name):
    n = len(gs)

    def body(*refs):
        ins, outs, (send_sems, recv_sems) = refs[:n], refs[n:2 * n], refs[2 * n:]
        x, y, c = _mesh_pos()
        cps = [pltpu.make_async_remote_copy(
            src_ref=ins[a].at[1 - c], dst_ref=outs[a], send_sem=send_sems.at[a], recv_sem=recv_sems.at[a],
            device_id=(x, y, 1 - c), device_id_type=MESH) for a in range(n)]
        for cp in cps:
            cp.start()
        for cp in cps:
            cp.wait()

    return _hbm_call(body, gs, [jax.ShapeDtypeStruct(g.shape[1:], g.dtype) for g in gs], n, name=name)


def _scatter_chips(ss, *, name):
    n = len(ss)

    def body(*refs):
        ins, outs, (send_sems, recv_sems) = refs[:n], refs[n:2 * n], refs[2 * n:]
        x, y, c = _mesh_pos()
        me = 2 * x + y
        chips = _other_chips(x, y)

        def copy(a, j, px, py):
            return pltpu.make_async_remote_copy(
                src_ref=ins[a].at[2 * px + py], dst_ref=outs[a].at[me],
                send_sem=send_sems.at[3 * a + j], recv_sem=recv_sems.at[3 * a + j],
                device_id=(px, py, c), device_id_type=MESH)

        def arrival(a, j, px, py):
            return pltpu.make_async_remote_copy(
                src_ref=ins[a].at[me], dst_ref=outs[a].at[2 * px + py],
                send_sem=send_sems.at[3 * a + j], recv_sem=recv_sems.at[3 * a + j],
                device_id=(px, py, c), device_id_type=MESH)

        cps = [copy(a, j, *chip) for a in range(n) for j, chip in enumerate(chips)]
        for cp in cps:
            cp.start()
        for a in range(n):
            for j, chip in enumerate(chips):
                arrival(a, j, *chip).wait_recv()
        for cp in cps:
            cp.wait_send()

    return _hbm_call(body, ss, [jax.ShapeDtypeStruct(s.shape, s.dtype) for s in ss], 3 * n, name=name)


def _share_halves(ts, *, name):
    n = len(ts)

    def body(*refs):
        ins, outs, (send_sems, recv_sems) = refs[:n], refs[n:2 * n], refs[2 * n:]
        x, y, c = _mesh_pos()
        cps = [pltpu.make_async_remote_copy(
            src_ref=ins[a], dst_ref=outs[a].at[c], send_sem=send_sems.at[a], recv_sem=recv_sems.at[a],
            device_id=(x, y, 1 - c), device_id_type=MESH) for a in range(n)]
        for cp in cps:
            cp.start()
        for a in range(n):
            pltpu.make_async_remote_copy(
                src_ref=ins[a], dst_ref=outs[a].at[1 - c], send_sem=send_sems.at[a], recv_sem=recv_sems.at[a],
                device_id=(x, y, 1 - c), device_id_type=MESH).wait_recv()
        for cp in cps:
            cp.wait_send()

    return _hbm_call(body, ts, [jax.ShapeDtypeStruct((2,) + t.shape, t.dtype) for t in ts], n, name=name)


def _half_add(g, ra, core, *, name):
    _, R, C = g.shape
    tr = _pick_rows(R, max(8, 2 * 1024 * 1024 // (4 * C)))

    def body(core_ref, g_ref, ra_ref, o_ref):
        o_ref[...] = (g_ref[...] + ra_ref[...]).astype(o_ref.dtype)

    return pl.pallas_call(
        body, out_shape=jax.ShapeDtypeStruct((R, C), BF16),
        grid_spec=pltpu.PrefetchScalarGridSpec(
            num_scalar_prefetch=1, grid=(R // tr,),
            in_specs=[pl.BlockSpec((None, tr, C), lambda i, cr: (cr[0], i, 0)),
                      pl.BlockSpec((tr, C), lambda i, cr: (i, 0))],
            out_specs=pl.BlockSpec((tr, C), lambda i, cr: (i, 0))),
        compiler_params=_cp(("parallel",)), name=name,
    )(core.reshape(1).astype(jnp.int32), g, ra)


def _sum4(rb, *, name):
    _, R, C = rb.shape
    tr = _pick_rows(R, max(8, 2 * 1024 * 1024 // (4 * C)))

    def body(r0, r1, r2, r3, o_ref):
        f = lambda r: r[...].astype(F32)
        o_ref[...] = ((f(r0) + f(r1)) + f(r2)) + f(r3)

    return pl.pallas_call(
        body, out_shape=jax.ShapeDtypeStruct((R, C), F32), grid=(R // tr,),
        in_specs=[pl.BlockSpec((None, tr, C), lambda i, k=k: (k, i, 0)) for k in range(N_CHIPS)],
        out_specs=pl.BlockSpec((tr, C), lambda i: (i, 0)),
        compiler_params=_cp(("parallel",)), name=name,
    )(rb, rb, rb, rb)


TR = 256
S5_CHUNK = 256


def _rms_fwd(x, g, name):
    return _rowwise(_f_rms, [x], [g], (BF16,), tr=TR, name=name)[0]


def _rms_bwd_epi(dh, x, g, gx):
    _, vjp = jax.vjp(lambda xv, gv: _f_rms(xv, gv)[0], x, g)
    dx, dg = vjp(dh)
    dx = dx + gx
    return dx, dx, dg


def _mm_rms_bwd(a, w, x, g, gx, *, after=None, name, **kw):
    return _mm(a, w, tb=True, epi=_rms_bwd_epi, extras=(x, g, gx), out_dtypes=(F32, BF16, F32), n_row_sums=1,
               tm=512, tn=x.shape[1], after=after, name=name, **kw)


def _rms_bwd(x, g, dh, gx, name, after=None):
    (dx, dxb), (dg,) = _rowwise_vjp(_f_rms, [x], [g], [dh], [(F32, BF16)], adds={0: gx}, after=after, tr=TR,
                                    name=name)
    return dx, dxb, dg


def _grad_cols(M, Nq):
    def imap(tm, tn):
        hp, per = (M // 2) // tm, Nq // tn
        assert hp * tm * 2 == M and per * tn == Nq, (M, Nq, tm, tn)
        return lambda i, j, k: (i // hp, j // per, i % hp, j % per)
    return (2, N_CHIPS, M // 2, Nq), lambda tm, tn: (None, None, tm, tn), imap, None, M // 2, Nq


def _grad_rows(Mq, N):
    def imap(tm, tn):
        po, hp = Mq // tm, (Mq // 2) // tm
        assert hp * tm * 2 == Mq, (Mq, tm)
        return lambda i, j, k: ((i % po) // hp, i // po, (i % po) % hp, j)
    return (2, N_CHIPS, Mq // 2, N), lambda tm, tn: (None, None, tm, tn), imap, None, Mq // 2, N


def _grad_layer_cols(slot, lh, M, Nq, buf):
    def imap(tm, tn):
        per = Nq // tn
        return lambda i, j, k: (j // per, slot, i, j % per)
    return (N_CHIPS, lh, M, Nq), lambda tm, tn: (None, None, tm, tn), imap, buf, M, Nq


def _grad_layer_rows(slot, lh, Mq, N, buf):
    def imap(tm, tn):
        po = Mq // tm
        return lambda i, j, k: (i // po, slot, i % po, j)
    return (N_CHIPS, lh, Mq, N), lambda tm, tn: (None, None, tm, tn), imap, buf, Mq, N


def _add_then_rms(acc, res, g):
    xo = acc + res
    return xo, _f_rms(xo, g)[0]


def _mlp_fwd(x, h2, w_in, w_out, g_next, li):
    r = _mm(h2, w_in, out_dtypes=(BF16,), epi=lambda acc: (jnp.maximum(acc, 0.0),), name=f"mlp_in_{li}")
    if g_next is None:
        x_out, h_next = _mm(r, w_out, pro_a=lambda t: t * t, epi=lambda acc, res: (acc + res,), extras=(x,),
                            name=f"mlp_out_{li}"), None
    else:
        x_out, h_next = _mm(r, w_out, pro_a=lambda t: t * t, epi=_add_then_rms, extras=(x, g_next),
                            out_dtypes=(F32, BF16), tn=x.shape[1], name=f"mlp_out_{li}")
    return x_out, h_next, (h2, r)


def _mlp_bwd(gx, gxb, x, g, w_in, w_out, saved, li, nl, bufs):
    h2, r = saved
    D, F = w_in.shape
    lh = nl // 2
    da = _mm(gxb, w_out, tb=True, out_dtypes=(BF16,),
             epi=lambda acc, rt: (acc * 2.0 * rt.astype(F32),), extras=(r,), name=f"mlp_dact_{li}")
    buf_in, buf_out = bufs if bufs is not None else (None, None)
    d_w_out = _mm(r, gxb, ta=True, pro_a=lambda t: t * t, tm=512, tn=1024, tk=2048, out_dtypes=(BF16,),
                  out=_grad_layer_rows(li % lh, lh, F // N_CHIPS, D, buf_out), name=f"mlp_dwout_{li}")
    d_w_in = _mm(h2, da, ta=True, tm=1024, tn=1024, tk=2048, out_dtypes=(BF16,),
                 out=_grad_layer_cols(li % lh, lh, D, F // N_CHIPS, buf_in), name=f"mlp_dwin_{li}")
    gx_mid, gxb_mid, dg = _mm_rms_bwd(da, w_in, x, g, gx, name=f"mlp_dh_{li}")
    return gx_mid, gxb_mid, dg, (d_w_in, d_w_out)


def _local_step(x3, tgt3, p, layer_weights, token=None, grads_done=lambda group: None):
    B, S, D = x3.shape
    T = B * S
    x = x3.reshape(T, D)
    grads = {}
    row = lambda v: v.reshape(1, -1)
    p = dict(p)
    nl = p["norm_mlp"].shape[0]
    mlp_in, mlp_out = [None] * nl, [None] * nl

    def fetch(li, after):
        wl = dict(layer_weights(li, after))
        mlp_in[li], mlp_out[li] = wl.pop("mlp_w_in"), wl.pop("mlp_w_out")
        p.update(wl)

    g0 = row(p["norm_mix"][0])
    if token is not None:
        g0 = g0 + token[:1, :1]
    h0 = _rms_fwd(x, g0, "rms_mix_0")
    s5_args = (p["ssm_a_re"][0], p["ssm_a_im"][0], p["ssm_b_re"][0], p["ssm_b_im"][0],
               p["ssm_c_re"][0], p["ssm_c_im"][0], p["ssm_log_dt"][0])
    s5_exp, s5_vjp = jax.vjp(_s5_prep, *s5_args)
    abr, abi, bre, bim, cre, cim = s5_exp
    bre_b, bim_b, cre_b, cim_b = (t.astype(BF16) for t in (bre, bim, cre, cim))
    d_skip = p["ssm_d"]
    ypre, yb, sxr, sxi, ser, sei = _s5_fwd(h0, abr, abi, bre_b, bim_b, cre_b, cim_b, d_skip, B=B, L=S5_CHUNK,
                                           name="s5_fwd")
    fetch(0, yb)
    w_glu = p["ssm_w_glu"]
    z0 = _mm(yb, w_glu, name="s5_glu_mm")
    gm = [row(p["norm_mlp"][i]) for i in range(nl)]
    g1, g2, g3 = (row(p["norm_mix"][i]) for i in range(1, nl))
    x_mid0, hm0 = _rowwise(lambda z, xr, g: _add_then_rms(_f_glu(z)[0], xr, g), [z0, x], [gm[0]], (F32, BF16),
                           tr=TR, name="s5_glu")
    x1, h1, mlp_saved0 = _mlp_fwd(x_mid0, hm0, mlp_in[0], mlp_out[0], g1, 0)

    fetch(1, h1)
    z1 = _mm(h1, p["conv_w_pw1"], name="conv_pw1")
    zg = _rowwise(_f_bias_glu, [z1], [p["conv_b_pw1"]], (F32,), tr=TR, name="conv_glu")[0]
    zp = jnp.pad(zg.reshape(B, S, D), ((0, 0), (CONV_HALO, 0), (0, 0)))
    w_dw = jnp.pad(p["conv_w_dw"], ((0, 32 - CONV_WIDTH), (0, 0)))
    yc = _conv_fwd(zp, w_dw, R=256, tc=128, name="conv_dw").reshape(T, D)
    ln_par = [p["conv_b_dw"], p["conv_ln_g"], p["conv_ln_b"]]
    qc = _rowwise(_f_ln_silu, [yc], ln_par, (BF16,), tr=TR, name="conv_ln_silu")[0]
    x_mid1, hm1 = _mm(qc, p["conv_w_pw2"], epi=lambda acc, bias, res, g: _add_then_rms(acc + bias, res, g),
                      extras=(p["conv_b_pw2"], x1, gm[1]), out_dtypes=(F32, BF16), tn=D, name="conv_pw2")
    x2, h2, mlp_saved1 = _mlp_fwd(x_mid1, hm1, mlp_in[1], mlp_out[1], g2, 1)

    fetch(2, h2)
    z2 = _mm(h2, p["gmlp_w_in"], name="gmlp_in")
    gl_par = [p["gmlp_ln_g"], p["gmlp_ln_b"]]
    gu, gvn = _rowwise(_f_gelu_ln, [z2], gl_par, (F32, F32), tr=TR, name="gmlp_gelu_ln")
    causal = jnp.tril(jnp.ones((GMLP_CHUNK, GMLP_CHUNK), dtype=bool))
    ws_b = jnp.where(causal[None], p["gmlp_w_s"][0], 0.0).astype(BF16)
    bcol = jnp.pad(p["gmlp_b_s"][0].T, ((0, 0), (0, 128 - GMLP_HEADS)))
    uv = _gmlp_fwd(gu, gvn, ws_b, bcol, nck=4, name="gmlp_spatial")
    x_mid2, hm2 = _mm(uv, p["gmlp_w_out"], epi=_add_then_rms, extras=(x2, gm[2]), out_dtypes=(F32, BF16), tn=D,
                      name="gmlp_out")
    x3_, h3, mlp_saved2 = _mlp_fwd(x_mid2, hm2, mlp_in[2], mlp_out[2], g3, 2)

    fetch(3, h3)
    ng = len(ATT_DILS)
    att_in, o_tok, l_tok, lses = [], [], [], []
    offs = (0, PAIRS, 2 * PAIRS)
    for gi, dil in enumerate(ATT_DILS):
        w_g = _ColBlocks(p["attn_w_qkv_plain"], gi, ng, 3, ATT_W)
        arr = _mm(h3, w_g, out_dtypes=(BF16,), tn=ATT_W, name=f"attn_qkv_{gi}")
        arr = _deinterleave(arr, B, S, dil)
        att_in.append((arr, offs))
        og, lg = _att_fwd(arr, offs, nb=S // dil // ATT_BLK, nbk=8, name=f"attn_fwd_{gi}")
        lses.append(lg)
        o_tok.append(_interleave(og, B, S, dil))
        l_tok.append(_stats_to_tokens(lg, B, S, dil))
    merged2 = _rowwise(_f_merge, o_tok + l_tok, [], (BF16,), tr=TR, name="attn_merge")[0]
    x_mid3, hm3 = _mm(merged2, p["attn_w_o_plain"], epi=_add_then_rms, extras=(x3_, gm[3]), out_dtypes=(F32, BF16),
                      tn=D, name="attn_out")
    x4, _, mlp_saved3 = _mlp_fwd(x_mid3, hm3, mlp_in[3], mlp_out[3], None, 3)

    loss_part, gx, gxb, dgf = _loss_head(x4, tgt3.reshape(T, D), row(p["norm_final"]), tr=TR, name="loss_head")
    grads["norm_final"] = dgf.reshape(-1)
    d_norm_mix, d_norm_mlp = [None] * 4, [None] * 4
    Dq = D // N_CHIPS

    gx, gxb, d_norm_mlp[3], mlp_hi = _mlp_bwd(
        gx, gxb, x_mid3, row(p["norm_mlp"][3]), mlp_in[3], mlp_out[3], mlp_saved3, 3, nl, None)
    dmerged = _mm(gxb, p["attn_w_o"], tb=True, name="attn_dmerged")
    grads["attn_w_o"] = _mm(merged2, gxb, ta=True, tm=256, tn=256, tk=2048, out_dtypes=(BF16,), out=_grad_cols(ATT_W, Dq), name="attn_dwo")
    dml, _ = _rowwise_vjp(_f_merge, o_tok + l_tok, [], [dmerged], [F32] * 6, tr=TR, name="attn_merge_bwd")
    pieces = [[None] * ng for _ in range(3)]
    for gi, dil in enumerate(ATT_DILS):
        arr, offs = att_in[gi]
        dqkv_g = _att_bwd(arr, offs, _deinterleave(dml[gi], B, S, dil), lses[gi],
                          _stats_from_tokens(dml[ng + gi], B, S, dil), nb=S // dil // ATT_BLK, nbk=8,
                          name=f"attn_bwd_{gi}")
        for i in range(3):
            pieces[i][gi] = _interleave(dqkv_g[i], B, S, dil)
    dqkv = jnp.concatenate([pieces[i][gi] for i in range(3) for gi in range(ng)], axis=1)
    qkv_w = 3 * ng * ATT_W
    grads["attn_w_qkv"] = _mm(h3, dqkv, ta=True, tm=512, tn=1152, tk=2048, out_dtypes=(BF16,), out=_grad_cols(D, qkv_w // N_CHIPS),
                              name="attn_dwqkv")
    tok = grads_done({n: grads[n] for n in ("attn_w_qkv", "attn_w_o")})
    gx, gxb, d_norm_mix[3] = _mm_rms_bwd(dqkv, p["attn_w_qkv"], x3_, g3, gx, tk=1152, after=tok, name="attn_dh")

    gx, gxb, d_norm_mlp[2], mlp_hi = _mlp_bwd(
        gx, gxb, x_mid2, row(p["norm_mlp"][2]), mlp_in[2], mlp_out[2], mlp_saved2, 2, nl, mlp_hi)
    tok = grads_done({"mlp_w_in": (1, mlp_hi[0]), "mlp_w_out": (1, mlp_hi[1])})
    duv = _mm(gxb, p["gmlp_w_out"], tb=True, after=tok, name="gmlp_duv")
    grads["gmlp_w_out"] = _mm(uv, gxb, ta=True, tm=128, tn=1024, tk=2048, out_dtypes=(BF16,), out=_grad_rows(Dq, D), name="gmlp_dwout")
    du, dvn, dws, dbcol = _gmlp_bwd(duv, gu, gvn, ws_b, bcol, nck=4, name="gmlp_spatial_bwd")
    grads["gmlp_w_s"] = jnp.where(causal[None], dws, 0.0)[None]
    grads["gmlp_b_s"] = dbcol[:, :GMLP_HEADS].T[None]
    (dz2,), (dlg, dlb_) = _rowwise_vjp(_f_gelu_ln, [z2], gl_par, [du, dvn], [BF16], tr=TR, name="gmlp_gelu_ln_bwd")
    grads["gmlp_ln_g"], grads["gmlp_ln_b"] = dlg, dlb_
    grads["gmlp_w_in"] = _mm(h2, dz2, ta=True, tm=512, tn=512, tk=2048, out_dtypes=(BF16,), out=_grad_cols(D, 2 * Dq), name="gmlp_dwin")
    tok = grads_done({n: grads[n] for n in ("gmlp_w_in", "gmlp_w_out")})
    gx, gxb, d_norm_mix[2] = _mm_rms_bwd(dz2, p["gmlp_w_in"], x2, g2, gx, after=tok, name="gmlp_dh")

    gx, gxb, d_norm_mlp[1], mlp_lo = _mlp_bwd(
        gx, gxb, x_mid1, row(p["norm_mlp"][1]), mlp_in[1], mlp_out[1], mlp_saved1, 1, nl, None)
    dqc = _mm(gxb, p["conv_w_pw2"], tb=True, name="conv_dq")
    grads["conv_w_pw2"] = _mm(qc, gxb, ta=True, tm=128, tn=1024, tk=2048, out_dtypes=(BF16,), out=_grad_rows(Dq, D), name="conv_dwpw2")
    _, (db2,) = _rowwise_vjp(lambda t, b: (t + b,), [gx], [p["conv_b_pw2"]], [gx], [None], tr=TR, name="conv_db2")
    grads["conv_b_pw2"] = db2
    (dyc,), (dbdw, dcg, dcb) = _rowwise_vjp(_f_ln_silu, [yc], ln_par, [dqc], [F32], tr=TR, name="conv_ln_silu_bwd")
    grads["conv_b_dw"], grads["conv_ln_g"], grads["conv_ln_b"] = dbdw, dcg, dcb
    dyp = jnp.pad(dyc.reshape(B, S, D), ((0, 0), (0, CONV_HALO), (0, 0)))
    dzg, dwdw = _conv_bwd(zp, dyp, w_dw, R=256, tc=128, name="conv_dw_bwd")
    grads["conv_w_dw"] = dwdw[:CONV_WIDTH][None]
    (dz1,), (db1,) = _rowwise_vjp(_f_bias_glu, [z1], [p["conv_b_pw1"]], [dzg.reshape(T, D)], [BF16], tr=TR,
                                  name="conv_glu_bwd")
    grads["conv_b_pw1"] = db1
    grads["conv_w_pw1"] = _mm(h1, dz1, ta=True, tm=512, tn=512, tk=2048, out_dtypes=(BF16,), out=_grad_cols(D, 2 * Dq), name="conv_dwpw1")
    tok = grads_done({n: grads[n] for n in ("conv_w_pw1", "conv_w_pw2")})
    gx, gxb, d_norm_mix[1] = _mm_rms_bwd(dz1, p["conv_w_pw1"], x1, g1, gx, after=tok, name="conv_dh")

    gx, gxb, d_norm_mlp[0], mlp_lo = _mlp_bwd(
        gx, gxb, x_mid0, row(p["norm_mlp"][0]), mlp_in[0], mlp_out[0], mlp_saved0, 0, nl, mlp_lo)
    tok = grads_done({"mlp_w_in": (0, mlp_lo[0]), "mlp_w_out": (0, mlp_lo[1])})
    (dz0,), _ = _rowwise_vjp(_f_glu, [z0], [], [gx], [BF16], after=tok, tr=TR, name="s5_glu_bwd")
    grads["ssm_w_glu"] = _mm(yb, dz0, ta=True, tm=512, tn=512, tk=2048, out_dtypes=(BF16,), out=_grad_cols(D, 2 * Dq), name="s5_dwglu")
    tok = grads_done({"ssm_w_glu": grads["ssm_w_glu"]})
    dypre = _mm(dz0, w_glu, tb=True, epi=lambda acc, yp: (jax.vjp(lambda t: jax.nn.gelu(t), yp)[1](acc)[0],),
                extras=(ypre,), name="s5_dypre")
    dh0, dbre, dbim, dcre, dcim, dabr, dabi, dd = _s5_bwd(
        dypre, h0, sxr, sxi, ser, sei, abr, abi, bre_b, bim_b, cre_b, cim_b, d_skip, B=B, L=S5_CHUNK, name="s5_bwd")
    s5_grads = s5_vjp((dabr, dabi, dbre, dbim, dcre, dcim))
    for nm, gv in zip(("ssm_a_re", "ssm_a_im", "ssm_b_re", "ssm_b_im", "ssm_c_re", "ssm_c_im", "ssm_log_dt"), s5_grads):
        grads[nm] = gv[None]
    grads["ssm_d"] = dd
    gx, _, d_norm_mix[0] = _rms_bwd(x, g0, dh0, gx, "rms_mix_bwd_0", after=tok)

    grads["norm_mix"] = jnp.concatenate(d_norm_mix, axis=0)
    grads["norm_mlp"] = jnp.concatenate(d_norm_mlp, axis=0)
    grads["mlp_w_in"], grads["mlp_w_out"] = (mlp_lo[0], mlp_hi[0]), (mlp_lo[1], mlp_hi[1])
    return loss_part, gx.reshape(B, S, D), grads


WEIGHTS = ['norm_mix', 'norm_mlp', 'norm_final', 'ssm_a_re', 'ssm_a_im', 'ssm_b_re', 'ssm_b_im', 'ssm_c_re',
           'ssm_c_im', 'ssm_d', 'ssm_log_dt', 'ssm_w_glu', 'conv_w_pw1', 'conv_b_pw1', 'conv_w_dw', 'conv_b_dw',
           'conv_ln_g', 'conv_ln_b', 'conv_w_pw2', 'conv_b_pw2', 'gmlp_w_in', 'gmlp_ln_g', 'gmlp_ln_b', 'gmlp_w_s',
           'gmlp_b_s', 'gmlp_w_out', 'attn_w_qkv', 'attn_w_o', 'mlp_w_in', 'mlp_w_out']
BIG_AXIS = {'ssm_w_glu': -1, 'conv_w_pw1': -1, 'conv_w_pw2': -2, 'gmlp_w_in': -1, 'gmlp_w_out': -2,
            'attn_w_qkv': -1, 'attn_w_o': -1, 'mlp_w_in': -1, 'mlp_w_out': -2}
BIG = list(BIG_AXIS)
LAYER_MIXER_WEIGHTS = (('ssm_w_glu',), ('conv_w_pw1', 'conv_w_pw2'), ('gmlp_w_in', 'gmlp_w_out'), ('attn_w_qkv', 'attn_w_o'))
SMALL_SHARDED = ['conv_b_pw1', 'conv_w_dw', 'conv_b_dw', 'conv_ln_g', 'conv_ln_b', 'conv_b_pw2', 'gmlp_ln_g', 'gmlp_ln_b']
SMALL_REPL = [n for n in WEIGHTS if n not in BIG_AXIS and n not in SMALL_SHARDED]
SMALL = SMALL_REPL + SMALL_SHARDED
LANES = 128
FLAT_COLS = 1024


def _pack(arrs, cols, row_mult):
    flat = jnp.concatenate([a.reshape(-1) for a in arrs])
    per = cols * row_mult
    n = -(-flat.shape[0] // per) * per
    return jnp.pad(flat, (0, n - flat.shape[0])).reshape(n // cols, cols)


def _unpack(flat2d, shapes):
    flat = flat2d.reshape(-1)
    out, off = [], 0
    for s in shapes:
        n = int(np.prod(s))
        out.append(flat[off:off + n].reshape(s))
        off += n
    return out


def _as_halves(shard):
    if shard.shape[0] == 1:
        shard = shard[0]
    return shard.reshape((2, shard.shape[0] // 2) + shard.shape[1:])


def _stored_weight(name, arr):
    kind = "cols" if BIG_AXIS[name] == -1 else "rows"
    if arr.shape[1] > 1:
        return [_Stored(arr, kind, lead=(li,)) for li in range(arr.shape[1])]
    arr = arr[:, 0]
    if kind == "rows":
        return arr.reshape(-1, arr.shape[-1])
    return _Stored(arr, kind)


def kernel(x, norm_mix, norm_mlp, norm_final, ssm_a_re, ssm_a_im, ssm_b_re, ssm_b_im, ssm_c_re, ssm_c_im, ssm_d, ssm_log_dt, ssm_w_glu, conv_w_pw1, conv_b_pw1, conv_w_dw, conv_b_dw, conv_ln_g, conv_ln_b, conv_w_pw2, conv_b_pw2, gmlp_w_in, gmlp_ln_g, gmlp_ln_b, gmlp_w_s, gmlp_b_s, gmlp_w_out, attn_w_qkv, attn_w_o, mlp_w_in, mlp_w_out, loss_target, m_norm_mix, m_norm_mlp, m_norm_final, m_ssm_a_re, m_ssm_a_im, m_ssm_b_re, m_ssm_b_im, m_ssm_c_re, m_ssm_c_im, m_ssm_d, m_ssm_log_dt, m_ssm_w_glu, m_conv_w_pw1, m_conv_b_pw1, m_conv_w_dw, m_conv_b_dw, m_conv_ln_g, m_conv_ln_b, m_conv_w_pw2, m_conv_b_pw2, m_gmlp_w_in, m_gmlp_ln_g, m_gmlp_ln_b, m_gmlp_w_s, m_gmlp_b_s, m_gmlp_w_out, m_attn_w_qkv, m_attn_w_o, m_mlp_w_in, m_mlp_w_out, v_norm_mix, v_norm_mlp, v_norm_final, v_ssm_a_re, v_ssm_a_im, v_ssm_b_re, v_ssm_b_im, v_ssm_c_re, v_ssm_c_im, v_ssm_d, v_ssm_log_dt, v_ssm_w_glu, v_conv_w_pw1, v_conv_b_pw1, v_conv_w_dw, v_conv_b_dw, v_conv_ln_g, v_conv_ln_b, v_conv_w_pw2, v_conv_b_pw2, v_gmlp_w_in, v_gmlp_ln_g, v_gmlp_ln_b, v_gmlp_w_s, v_gmlp_b_s, v_gmlp_w_out, v_attn_w_qkv, v_attn_w_o, v_mlp_w_in, v_mlp_w_out):
    args = dict(locals())
    w = {n: args[n] for n in WEIGHTS}
    m = {n: args["m_" + n] for n in WEIGHTS}
    v = {n: args["v_" + n] for n in WEIGHTS}
    chip = 2 * lax.axis_index("x") + lax.axis_index("y")
    core = lax.axis_index("c")

    big_shapes = [w[n].shape for n in BIG]
    started, token = [], None
    for li, mixer in enumerate(LAYER_MIXER_WEIGHTS):
        names = list(mixer) + ["mlp_w_in", "mlp_w_out"]
        shards = [w[n][0] for n in mixer] + [w["mlp_w_in"][li], w["mlp_w_out"][li]]
        halves = [s.astype(BF16).reshape((2, s.shape[0] // 2) + s.shape[1:]) for s in shards]
        send_sems, recv_sems, halves, lands, token = _gather_start(halves, token, name=f"gather_start_{li}")
        started.append((names, (send_sems, recv_sems, halves, lands)))

    def layer_weights(li, after):
        names, st = started[li]
        halves, lands = _gather_wait(st, after, name=f"gather_wait_{li}")
        out = {}
        for n, h, arr in zip(names, halves, lands):
            arr = lax.dynamic_update_index_in_dim(arr, h, chip, axis=0)
            arr = arr.reshape((N_CHIPS, arr.shape[1] * arr.shape[2]) + arr.shape[3:])
            if BIG_AXIS[n] == -1:
                out[n] = _Stored(arr, "cols")
                if n in ("attn_w_qkv", "attn_w_o"):
                    out[n + "_plain"] = jnp.swapaxes(arr, 0, 1).reshape(arr.shape[1], -1)
            else:
                out[n] = arr.reshape(-1, arr.shape[-1])
        return out

    p = {}
    sm_shapes = [w[n].shape for n in SMALL_SHARDED]
    sflat = _pack([w[n] for n in SMALL_SHARDED], LANES, 8)
    rs = sflat.shape[0]
    sall = _allgather8(sflat, name="gather_small").reshape(8, rs, LANES)
    per_chip = [_unpack(sall[2 * k], sm_shapes) for k in range(N_CHIPS)]
    for i, n in enumerate(SMALL_SHARDED):
        p[n] = jnp.concatenate([per_chip[k][i] for k in range(N_CHIPS)], axis=-1)
    for n in SMALL_REPL:
        p[n] = w[n]
    p['conv_w_dw'] = p['conv_w_dw'][0]

    in_flight, arrived, n_rounds = [], {}, [0]

    def finish_round(after):
        k, names, plans, st = in_flight.pop(0)
        srcs, lands = _reduce_wait(st, plans, after, name=f"grads_wait_{k}")
        for n, plan, src, land in zip(names, plans, srcs, lands):
            arrived.setdefault(n, []).append((plan, src, land))

    def grads_done(group):
        names, srcs, plans, lands = [], [], [], []
        for n, v in group.items():
            if isinstance(v, tuple):
                src, plan = v[1].reshape(1, N_CHIPS, -1, v[1].shape[-1]), ((0, v[0]),)
                while any(n in rd[1] for rd in in_flight):
                    finish_round(src)
            else:
                src, plan = v.reshape(2, N_CHIPS, -1, v.shape[-1]), ((0, 0), (1, 1))
            land = arrived[n][-1][2] if n in arrived else lax.empty((2 * N_CHIPS,) + src.shape[2:], BF16)
            names.append(n), srcs.append(src), plans.append(plan), lands.append(land)
        st = _reduce_start(srcs, lands, plans, None, name=f"grads_start_{n_rounds[0]}")
        in_flight.append((n_rounds[0], names, plans, st[:4]))
        n_rounds[0] += 1
        return st[4]

    loss_part, grad_x, g = _local_step(x, loss_target, p, layer_weights, token, grads_done)
    loss = lax.psum(loss_part[0, 0], ("x", "y", "c"))

    my_id = 2 * chip + core
    while in_flight:
        finish_round(grad_x)
    totals = []
    for n in BIG:
        own = None
        for plan, src, land in arrived[n]:
            slab = lax.dynamic_index_in_dim(src, chip, axis=1, keepdims=False)
            if len(plan) == 2:
                own = lax.dynamic_index_in_dim(slab, core, axis=0, keepdims=False)
            else:
                own = slab[0] if own is None else jnp.where(core == plan[0][1], slab[0], own)
        totals.append(_sum8(arrived[n][-1][2], own, my_id, name="owner_sum_" + n))
    shared = _share_halves(totals, name="grads_share_halves")
    big_grads = {}
    for n, arr, t in zip(BIG, shared, totals):
        arr = lax.dynamic_update_index_in_dim(arr, t[None], core, axis=0)
        big_grads[n] = arr.reshape(w[n].shape)

    small_full_shapes = [g[n].shape for n in SMALL]
    gs = _pack([g[n] for n in SMALL], LANES, 8)
    rg = gs.shape[0]
    gs_all = _allgather8(gs, name="gather_small_grads").reshape(8, rg, LANES)
    gs_sum = _rowwise(lambda *a: (functools.reduce(lambda s, t: s + t, a),), [gs_all[k] for k in range(8)], [], (F32,),
                      tr=rg, name="small_grads_sum")[0]
    small_grads = dict(zip(SMALL, _unpack(gs_sum, small_full_shapes)))
    for n in SMALL:
        small_grads[n] = small_grads[n].reshape(p_shape_full(w[n], -1 if n in SMALL_SHARDED else None))
    for n in SMALL_SHARDED:
        width = w[n].shape[-1]
        small_grads[n] = lax.dynamic_slice_in_dim(small_grads[n], chip * width, width, axis=-1)

    grad, delta, new_m, new_v = {}, {}, {}, {}
    for n in BIG:
        shape = w[n].shape
        two_d = lambda t: t.reshape(-1, shape[-1])
        grad[n] = big_grads[n]
        d_, m_, v_ = _adamw(two_d(w[n]), two_d(grad[n]), two_d(m[n]), two_d(v[n]), name="adamw_" + n)
        delta[n], new_m[n], new_v[n] = d_.reshape(shape), m_.reshape(shape), v_.reshape(shape)
    sm_own_shapes = [w[n].shape for n in SMALL]
    packed = [_pack([src[n] for n in SMALL], LANES, 8) for src in (w, small_grads, m, v)]
    outs = _adamw(*packed, name="adamw_small")
    for dst, flat in zip((delta, new_m, new_v), outs):
        dst.update(dict(zip(SMALL, _unpack(flat, sm_own_shapes))))
    for n in SMALL:
        grad[n] = small_grads[n]

    return (loss, grad_x, *[grad[n] for n in WEIGHTS], *[delta[n] for n in WEIGHTS],
            *[new_m[n] for n in WEIGHTS], *[new_v[n] for n in WEIGHTS])


def p_shape_full(shard, axis):
    s = list(shard.shape)
    if axis is not None:
        s[axis] *= N_CHIPS
    return tuple(s)
```

```python
import functools
import math

import jax
import jax.numpy as jnp
import numpy as np
from jax import lax
from jax.experimental import pallas as pl
from jax.experimental.pallas import tpu as pltpu

F32 = jnp.float32
BF16 = jnp.bfloat16
MESH = pl.DeviceIdType.MESH

EPS = 1e-6
SSM_GROUP = 16
SSM_STATE = 64
CONV_WIDTH = 31
CONV_HALO = 32
GMLP_CHUNK = 128
GMLP_HEADS = 4
ATT_DILS = (1, 4, 16)
ATT_BLK = 128
ATT_HEADS = 8
HEAD_DIM = 64
ATT_W = ATT_HEADS * HEAD_DIM
N_CHIPS = 4
ADAM_LR, ADAM_B1, ADAM_B2, ADAM_EPS, ADAM_WD, ADAM_STEP = 1e-3, 0.9, 0.999, 1e-8, 0.01, 10

VMEM_BYTES_V7X = 64 * 1024 * 1024
VMEM_LIMIT = VMEM_BYTES_V7X - 8 * 1024 * 1024
MASK_VALUE = -1e30
LANE_TILE = 128


def _cp(sem=None):
    return pltpu.CompilerParams(dimension_semantics=sem, vmem_limit_bytes=VMEM_LIMIT)


def _pick_tile(total, target):
    for cand in range(min(target, total) // LANE_TILE * LANE_TILE, 0, -LANE_TILE):
        if total % cand == 0:
            return cand
    return total


class _Stored:
    def __init__(self, arr, kind="plain", lead=()):
        self.arr, self.kind, self.lead = arr, kind, tuple(lead)
        r, c = arr.shape[-2:]
        self.shape = (r, c * N_CHIPS) if kind == "cols" else (r * N_CHIPS, c) if kind == "rows" else (r, c)

    def spec(self, br, bc, rc_of):
        lead, nl = self.lead, len(self.lead)
        if self.kind == "plain":
            return pl.BlockSpec((None,) * nl + (br, bc), lambda i, j, k: (*lead, *rc_of(i, j, k)))
        if self.kind == "cols":
            per = self.arr.shape[-1] // bc
            assert per * bc == self.arr.shape[-1]

            def imap(i, j, k):
                r, c = rc_of(i, j, k)
                return (c // per, *lead, r, c % per)
        else:
            per = self.arr.shape[-2] // br
            assert per * br == self.arr.shape[-2]

            def imap(i, j, k):
                r, c = rc_of(i, j, k)
                return (r // per, *lead, r % per, c)
        return pl.BlockSpec((None,) * (nl + 1) + (br, bc), imap)


class _ColBlocks:
    kind = "colblocks"

    def __init__(self, arr, first, stride, count, width):
        self.arr, self.first, self.stride, self.width = arr, first, stride, width
        self.shape = (arr.shape[0], count * width)

    def spec(self, br, bc, rc_of):
        per = self.width // bc
        assert per * bc == self.width

        def imap(i, j, k):
            r, c = rc_of(i, j, k)
            return (r, (self.first + (c // per) * self.stride) * per + c % per)
        return pl.BlockSpec((br, bc), imap)


def _mm(a, b, *, ta=False, tb=False, out_dtypes=(F32,), tm=1024, tn=1024, tk=1024,
        pro_a=None, pro_b=None, epi=None, extras=(), n_row_sums=0, out=None, after=None, name):
    if ta:
        K, M = a.shape
    else:
        M, K = a.shape
    if not isinstance(b, (_Stored, _ColBlocks)):
        b = _Stored(b)
    N, Kb = b.shape if tb else b.shape[::-1]
    assert K == Kb, (a.shape, b.shape, ta, tb)
    col_unit = b.width if b.kind == "colblocks" else b.arr.shape[-1] if b.kind == "cols" else b.shape[1]
    row_unit = b.arr.shape[-2] if b.kind == "rows" else b.shape[0]
    n_unit, k_unit = (row_unit, col_unit) if tb else (col_unit, row_unit)
    m_unit = M
    if out is not None:
        m_unit, n_unit = out[4], math.gcd(n_unit, out[5])
    tm, tn, tk = _pick_tile(m_unit, tm), _pick_tile(n_unit, tn), _pick_tile(k_unit, tk)
    assert not n_row_sums or tn == N
    nk = K // tk
    a_spec = (pl.BlockSpec((tk, tm), lambda i, j, k: (k, i)) if ta
              else pl.BlockSpec((tm, tk), lambda i, j, k: (i, k)))
    b_spec = b.spec(tn, tk, lambda i, j, k: (j, k)) if tb else b.spec(tk, tn, lambda i, j, k: (k, j))
    ex_specs = []
    for e in extras:
        if e.shape[0] == 1:
            ex_specs.append(pl.BlockSpec((1, tn), lambda i, j, k: (0, j)))
        else:
            assert e.shape == (M, N), (e.shape, M, N)
            ex_specs.append(pl.BlockSpec((tm, tn), lambda i, j, k: (i, j)))
    dims = (((0 if ta else 1,), (1 if tb else 0,)), ((), ()))
    n_ex, n_out = len(extras), len(out_dtypes)
    direct = epi is None and n_out == 1 and out_dtypes[0] == F32
    use_acc = nk > 1 and not direct
    operands, aliases, alias_specs = [a, b.arr, *extras], {}, []
    if after is not None:
        operands.append(after)
        alias_specs.append(pl.BlockSpec(memory_space=pl.ANY))
    if out is None:
        n_tile_out = n_out - n_row_sums
        out_specs = ([pl.BlockSpec((tm, tn), lambda i, j, k: (i, j))] * n_tile_out
                     + [pl.BlockSpec((1, tn), lambda i, j, k: (0, j))] * n_row_sums)
        out_shape = ([jax.ShapeDtypeStruct((M, N), dt) for dt in out_dtypes[:n_tile_out]]
                     + [jax.ShapeDtypeStruct((1, N), dt) for dt in out_dtypes[n_tile_out:]])
    else:
        shape, block_fn, imap_fn, alias = out[:4]
        assert n_out == 1
        out_specs = [pl.BlockSpec(block_fn(tm, tn), imap_fn(tm, tn))]
        out_shape = [jax.ShapeDtypeStruct(shape, out_dtypes[0])]
        if alias is not None:
            operands.append(alias)
            aliases = {len(operands) - 1: 0}
            alias_specs.append(pl.BlockSpec(memory_space=pl.ANY))
    n_in = len(operands)

    def finish(r, ex, outs, first_row_tile):
        res = epi(r, *[e[...] for e in ex]) if epi is not None else (r,)
        n_tile_out = n_out - n_row_sums
        for o, v in zip(outs[:n_tile_out], res):
            o[...] = v.astype(o.dtype)
        for o, v in zip(outs[n_tile_out:], res[n_tile_out:]):
            @pl.when(first_row_tile)
            def _(o=o):
                o[...] = jnp.zeros_like(o)
            o[...] += v

    def body(*refs):
        a_ref, b_ref = refs[:2]
        ex = refs[2:2 + n_ex]
        outs = refs[n_in:n_in + n_out]
        first_row_tile = pl.program_id(0) == 0
        at, bt = a_ref[...], b_ref[...]
        if pro_a is not None:
            at = pro_a(at)
        if pro_b is not None:
            bt = pro_b(bt)
        part = lax.dot_general(at, bt, dims, preferred_element_type=F32)
        if nk == 1:
            finish(part, ex, outs, first_row_tile)
            return
        acc = refs[-1] if use_acc else outs[0]
        k = pl.program_id(2)

        @pl.when(k == 0)
        def _():
            acc[...] = part

        @pl.when(k > 0)
        def _():
            acc[...] += part

        if use_acc:
            @pl.when(k == nk - 1)
            def _():
                finish(acc[...], ex, outs, first_row_tile)

    res = pl.pallas_call(
        body, grid=(M // tm, N // tn, nk),
        in_specs=[a_spec, b_spec] + ex_specs + alias_specs,
        out_specs=out_specs, out_shape=out_shape,
        scratch_shapes=[pltpu.VMEM((tm, tn), F32)] if use_acc else [],
        input_output_aliases=aliases,
        compiler_params=_cp(("arbitrary" if n_row_sums else "parallel", "parallel", "arbitrary")), name=name,
    )(*operands)
    return res[0] if n_out == 1 else res


def _to_bf16(t):
    return t.astype(BF16)


def _pick_rows(total, target):
    for cand in range(min(target, total) // 8 * 8, 0, -8):
        if total % cand == 0:
            return cand
    return total


def _rowwise(f, rows, params, out_dtypes, *, tr, name):
    T = rows[0].shape[0]
    tr = _pick_rows(T, tr)
    nr, npar = len(rows), len(params)
    blk = [jax.ShapeDtypeStruct((tr, r.shape[1]), F32) for r in rows]
    blk += [jax.ShapeDtypeStruct(p.shape, F32) for p in params]
    out_avals = jax.eval_shape(f, *blk)

    def body(*refs):
        res = f(*[r[...].astype(F32) for r in refs[:nr + npar]])
        for o, v in zip(refs[nr + npar:], res):
            o[...] = v.astype(o.dtype)

    out = pl.pallas_call(
        body, grid=(T // tr,),
        in_specs=[pl.BlockSpec((tr, r.shape[1]), lambda i: (i, 0)) for r in rows]
        + [pl.BlockSpec(p.shape, lambda i, nd=p.ndim: (0,) * nd) for p in params],
        out_specs=[pl.BlockSpec((tr, o.shape[1]), lambda i: (i, 0)) for o in out_avals],
        out_shape=[jax.ShapeDtypeStruct((T, o.shape[1]), dt) for o, dt in zip(out_avals, out_dtypes)],
        compiler_params=_cp(("parallel",)), name=name,
    )(*rows, *params)
    return out


def _rowwise_vjp(f, rows, params, cots, drow_dtypes, *, adds=None, after=None, tr, name):
    adds = adds or {}
    T = rows[0].shape[0]
    tr = _pick_rows(T, tr)
    nr, npar, nc = len(rows), len(params), len(cots)
    want, want_dt = [], []
    for i, dt in enumerate(drow_dtypes):
        for one in (dt if isinstance(dt, tuple) else (dt,)):
            if one is not None:
                want.append(i)
                want_dt.append(one)
    add_idx = sorted(set(i for i in want if i in adds))
    add_arrays = [adds[i] for i in add_idx]
    na = len(add_arrays)
    extra = [] if after is None else [after]

    def body(*refs):
        ins = [r[...].astype(F32) for r in refs[:nr + npar]]
        cvals = [r[...].astype(F32) for r in refs[nr + npar:nr + npar + nc]]
        avals = refs[nr + npar + nc:nr + npar + nc + na]
        outs = refs[nr + npar + nc + na + len(extra):]
        _, vjp = jax.vjp(f, *ins)
        grads = vjp(tuple(cvals))
        for o, i in zip(outs[:len(want)], want):
            g = grads[i]
            if i in adds:
                g = g + avals[add_idx.index(i)][...].astype(F32)
            o[...] = g.astype(o.dtype)
        step = pl.program_id(0)
        for o, g in zip(outs[len(want):], grads[nr:]):
            @pl.when(step == 0)
            def _(o=o):
                o[...] = jnp.zeros_like(o)
            o[...] += g

    rspec = lambda r: pl.BlockSpec((tr, r.shape[1]), lambda i: (i, 0))
    pspec = lambda p: pl.BlockSpec(p.shape, lambda i, nd=p.ndim: (0,) * nd)
    out = pl.pallas_call(
        body, grid=(T // tr,),
        in_specs=[rspec(r) for r in rows] + [pspec(p) for p in params] + [rspec(c) for c in cots]
        + [rspec(a) for a in add_arrays] + [pl.BlockSpec(memory_space=pl.ANY)] * len(extra),
        out_specs=[rspec(rows[i]) for i in want] + [pspec(p) for p in params],
        out_shape=[jax.ShapeDtypeStruct(rows[i].shape, dt) for i, dt in zip(want, want_dt)]
        + [jax.ShapeDtypeStruct(p.shape, F32) for p in params],
        compiler_params=_cp(("arbitrary",)), name=name,
    )(*rows, *params, *cots, *add_arrays, *extra)
    return out[:len(want)], out[len(want):]


def _f_rms(x, g):
    return (x * lax.rsqrt(jnp.mean(x * x, axis=-1, keepdims=True) + EPS) * g,)


def _ln(x, g, b):
    mu = jnp.mean(x, axis=-1, keepdims=True)
    var = jnp.mean(jnp.square(x - mu), axis=-1, keepdims=True)
    return (x - mu) * lax.rsqrt(var + EPS) * g + b


def _f_glu(z):
    d = z.shape[1] // 2
    return (z[:, :d] * jax.nn.sigmoid(z[:, d:]),)


def _f_bias_glu(z, b):
    return _f_glu(z + b)


def _f_ln_silu(y, b_dw, g, b):
    return (jax.nn.silu(_ln(y + b_dw, g, b)),)


def _f_gelu_ln(z, g, b):
    d = z.shape[1] // 2
    zz = jax.nn.gelu(z)
    return zz[:, :d], _ln(zz[:, d:], g, b)


def _f_gelu(y):
    return (jax.nn.gelu(y),)


def _f_merge(o0, o1, o2, l0, l1, l2):
    m = jnp.maximum(jnp.maximum(l0, l1), l2)
    e0, e1, e2 = jnp.exp(l0 - m), jnp.exp(l1 - m), jnp.exp(l2 - m)
    s = e0 + e1 + e2
    pair = 2 * HEAD_DIM
    first_head = lax.broadcasted_iota(jnp.int32, (o0.shape[0], pair), 1) < HEAD_DIM
    cols = []
    for hp in range(o0.shape[1] // pair):
        acc = None
        for o, e in ((o0, e0), (o1, e1), (o2, e2)):
            wgt = e / s
            wp = jnp.where(first_head, wgt[:, 2 * hp:2 * hp + 1], wgt[:, 2 * hp + 1:2 * hp + 2])
            term = wp * o[:, hp * pair:(hp + 1) * pair]
            acc = term if acc is None else acc + term
        cols.append(acc)
    return (jnp.concatenate(cols, axis=1),)


def _f_add(a, b):
    return (a + b,)


def _loss_head(x, tgt, g, *, tr, name):
    T, D = x.shape
    tr = min(tr, T)

    def f(xv, gv, tv):
        y = _f_rms(xv, gv)[0]
        return 0.5 * jnp.mean(jnp.square(y - tv), axis=-1, keepdims=True)

    def body(x_ref, t_ref, g_ref, loss_ref, dx_ref, dxb_ref, dg_ref):
        tv = t_ref[...]
        l, vjp = jax.vjp(lambda xv, gv: f(xv, gv, tv), x_ref[...], g_ref[...])
        dx, dg = vjp(jnp.ones_like(l))
        dx_ref[...] = dx
        dxb_ref[...] = dx.astype(BF16)

        @pl.when(pl.program_id(0) == 0)
        def _():
            loss_ref[...] = jnp.zeros_like(loss_ref)
            dg_ref[...] = jnp.zeros_like(dg_ref)

        loss_ref[...] += jnp.sum(l)
        dg_ref[...] += dg

    return pl.pallas_call(
        body, grid=(T // tr,),
        in_specs=[pl.BlockSpec((tr, D), lambda i: (i, 0)), pl.BlockSpec((tr, D), lambda i: (i, 0)),
                  pl.BlockSpec((1, D), lambda i: (0, 0))],
        out_specs=[pl.BlockSpec((1, 128), lambda i: (0, 0)), pl.BlockSpec((tr, D), lambda i: (i, 0)),
                   pl.BlockSpec((tr, D), lambda i: (i, 0)), pl.BlockSpec((1, D), lambda i: (0, 0))],
        out_shape=[jax.ShapeDtypeStruct((1, 128), F32), jax.ShapeDtypeStruct((T, D), F32),
                   jax.ShapeDtypeStruct((T, D), BF16), jax.ShapeDtypeStruct((1, D), F32)],
        compiler_params=_cp(("arbitrary",)), name=name,
    )(x, tgt, g)


def _adamw(w, g, m, v, *, name):
    R, C = w.shape
    tr = _pick_rows(R, max(8, 2 * 1024 * 1024 // (4 * C)))
    c1 = 1.0 - ADAM_B1 ** ADAM_STEP
    c2 = 1.0 - ADAM_B2 ** ADAM_STEP

    def body(w_ref, g_ref, m_ref, v_ref, d_ref, nm_ref, nv_ref):
        gv = g_ref[...]
        nm = ADAM_B1 * m_ref[...] + (1.0 - ADAM_B1) * gv
        nv = ADAM_B2 * v_ref[...] + (1.0 - ADAM_B2) * jnp.square(gv)
        nm_ref[...] = nm
        nv_ref[...] = nv
        d_ref[...] = -ADAM_LR * ((nm / c1) / (jnp.sqrt(nv / c2) + ADAM_EPS) + ADAM_WD * w_ref[...])

    spec = pl.BlockSpec((tr, C), lambda i: (i, 0))
    return pl.pallas_call(
        body, grid=(R // tr,), in_specs=[spec] * 4, out_specs=[spec] * 3,
        out_shape=[jax.ShapeDtypeStruct((R, C), F32)] * 3,
        compiler_params=_cp(("parallel",)), name=name,
    )(w, g, m, v)


def _s5_prep(a_re, a_im, b_re, b_im, c_re, c_im, log_dt):
    G, N = a_re.shape
    P = b_re.shape[2]
    gpb = 128 // P
    nblk = G // gpb
    dt = jnp.exp(log_dt)[:, None]
    mag = jnp.exp(a_re * dt)
    abr, abi = mag * jnp.cos(a_im * dt), mag * jnp.sin(a_im * dt)
    den = a_re * a_re + a_im * a_im
    nr, ni = abr - 1.0, abi
    qr, qi = (nr * a_re + ni * a_im) / den, (ni * a_re - nr * a_im) / den
    bbr = qr[..., None] * b_re - qi[..., None] * b_im
    bbi = qr[..., None] * b_im + qi[..., None] * b_re
    eye = jnp.eye(gpb, dtype=F32)

    def expand_b(t):
        t = t.reshape(nblk, gpb, N, P).transpose(0, 1, 3, 2)
        return (t[:, :, :, None, :] * eye[None, :, None, :, None]).reshape(nblk, gpb * P, gpb * N)

    def expand_c(t):
        t = t.reshape(nblk, gpb, P, N).transpose(0, 1, 3, 2)
        return (t[:, :, :, None, :] * eye[None, :, None, :, None]).reshape(nblk, gpb * N, gpb * P)

    return (abr.reshape(1, G * N), abi.reshape(1, G * N), expand_b(bbr), expand_b(bbi),
            expand_c(c_re), expand_c(c_im))


def _s5_fwd(h, abr, abi, bre, bim, cre, cim, d, *, B, L, name):
    T, D = h.shape
    S = T // B
    L = min(L, S)
    nc = S // L
    nblk, cb, sb = bre.shape
    GN = abr.shape[1]

    def body(h_ref, ar_ref, ai_ref, bre_ref, bim_ref, cre_ref, cim_ref, d_ref,
             y_ref, yb_ref, xr_ref, xi_ref, er_ref, ei_ref, sr, si, car, cai):
        ci = pl.program_id(1)

        @pl.when(ci == 0)
        def _():
            car[...] = jnp.zeros_like(car)
            cai[...] = jnp.zeros_like(cai)

        for j in range(nblk):
            u = h_ref[:, j * cb:(j + 1) * cb]
            sr[:, j * sb:(j + 1) * sb] = jnp.dot(u, bre_ref[j], preferred_element_type=F32)
            si[:, j * sb:(j + 1) * sb] = jnp.dot(u, bim_ref[j], preferred_element_type=F32)
        ar, ai = ar_ref[...], ai_ref[...]

        def step(t, carry):
            pr, pi = carry
            nr = ar * pr - ai * pi + sr[pl.ds(t, 1), :]
            ni = ar * pi + ai * pr + si[pl.ds(t, 1), :]
            sr[pl.ds(t, 1), :] = nr
            si[pl.ds(t, 1), :] = ni
            return nr, ni

        pr, pi = lax.fori_loop(0, L, step, (car[...], cai[...]), unroll=4)
        car[...] = pr
        cai[...] = pi
        er_ref[0] = pr
        ei_ref[0] = pi
        for j in range(nblk):
            xr = sr[:, j * sb:(j + 1) * sb].astype(BF16)
            xi = si[:, j * sb:(j + 1) * sb].astype(BF16)
            xr_ref[:, j * sb:(j + 1) * sb] = xr
            xi_ref[:, j * sb:(j + 1) * sb] = xi
            y = (jnp.dot(xr, cre_ref[j], preferred_element_type=F32)
                 - jnp.dot(xi, cim_ref[j], preferred_element_type=F32))
            u = h_ref[:, j * cb:(j + 1) * cb].astype(F32)
            y = y + d_ref[:, j * cb:(j + 1) * cb] * u
            y_ref[:, j * cb:(j + 1) * cb] = y
            yb_ref[:, j * cb:(j + 1) * cb] = jax.nn.gelu(y).astype(BF16)

    tok = lambda w: pl.BlockSpec((L, w), lambda b, c: (b * nc + c, 0))
    whole = lambda p: pl.BlockSpec(p.shape, lambda b, c, nd=p.ndim: (0,) * nd)
    end = pl.BlockSpec((1, 1, GN), lambda b, c: (b * nc + c, 0, 0))
    return pl.pallas_call(
        body, grid=(B, nc),
        in_specs=[tok(D)] + [whole(p) for p in (abr, abi, bre, bim, cre, cim, d)],
        out_specs=[tok(D), tok(D), tok(GN), tok(GN), end, end],
        out_shape=[jax.ShapeDtypeStruct((T, D), F32), jax.ShapeDtypeStruct((T, D), BF16),
                   jax.ShapeDtypeStruct((T, GN), BF16),
                   jax.ShapeDtypeStruct((T, GN), BF16), jax.ShapeDtypeStruct((B * nc, 1, GN), F32),
                   jax.ShapeDtypeStruct((B * nc, 1, GN), F32)],
        scratch_shapes=[pltpu.VMEM((L, GN), F32), pltpu.VMEM((L, GN), F32),
                        pltpu.VMEM((1, GN), F32), pltpu.VMEM((1, GN), F32)],
        compiler_params=_cp(("arbitrary", "arbitrary")), name=name,
    )(h, abr, abi, bre, bim, cre, cim, d)


def _s5_bwd(dy, h, xr, xi, er, ei, abr, abi, bre, bim, cre, cim, d, *, B, L, name):
    T, D = h.shape
    S = T // B
    L = min(L, S)
    nc = S // L
    nblk, cb, sb = bre.shape
    GN = abr.shape[1]
    dims_nt = (((1,), (1,)), ((), ()))
    dims_tn = (((0,), (0,)), ((), ()))

    def body(dy_ref, h_ref, xr_ref, xi_ref, er_ref, ei_ref, ar_ref, ai_ref, bre_ref, bim_ref,
             cre_ref, cim_ref, d_ref,
             dh_ref, dbre_ref, dbim_ref, dcre_ref, dcim_ref, dar_ref, dai_ref, dd_ref,
             lr, li, car, cai):
        b, cstep = pl.program_id(0), pl.program_id(1)
        ci = nc - 1 - cstep

        @pl.when((b == 0) & (cstep == 0))
        def _():
            for r in (dbre_ref, dbim_ref, dcre_ref, dcim_ref, dar_ref, dai_ref, dd_ref):
                r[...] = jnp.zeros_like(r)

        @pl.when(cstep == 0)
        def _():
            car[...] = jnp.zeros_like(car)
            cai[...] = jnp.zeros_like(cai)

        for j in range(nblk):
            dyj = dy_ref[:, j * cb:(j + 1) * cb].astype(BF16)
            lr[:, j * sb:(j + 1) * sb] = lax.dot_general(dyj, cre_ref[j], dims_nt, preferred_element_type=F32)
            li[:, j * sb:(j + 1) * sb] = -lax.dot_general(dyj, cim_ref[j], dims_nt, preferred_element_type=F32)
        ar, ai = ar_ref[...], ai_ref[...]

        def step(s, carry):
            t = L - 1 - s
            pr, pi = carry
            nr = lr[pl.ds(t, 1), :] + ar * pr + ai * pi
            ni = li[pl.ds(t, 1), :] - ai * pr + ar * pi
            lr[pl.ds(t, 1), :] = nr
            li[pl.ds(t, 1), :] = ni
            return nr, ni

        pr, pi = lax.fori_loop(0, L, step, (car[...], cai[...]), unroll=4)
        car[...] = pr
        cai[...] = pi
        has_prev = (ci > 0).astype(F32)
        first_row = lax.broadcasted_iota(jnp.int32, (L, sb), 0) == 0
        for j in range(nblk):
            cs = slice(j * cb, (j + 1) * cb)
            ss = slice(j * sb, (j + 1) * sb)
            lrj, lij = lr[:, ss], li[:, ss]
            xrj, xij = xr_ref[:, ss], xi_ref[:, ss]
            pr_j = jnp.where(first_row, er_ref[0][:, ss] * has_prev, pltpu.roll(xrj.astype(F32), 1, 0))
            pi_j = jnp.where(first_row, ei_ref[0][:, ss] * has_prev, pltpu.roll(xij.astype(F32), 1, 0))
            dar_ref[:, ss] += jnp.sum(lrj * pr_j + lij * pi_j, axis=0, keepdims=True)
            dai_ref[:, ss] += jnp.sum(lij * pr_j - lrj * pi_j, axis=0, keepdims=True)
            lrb, lib = lrj.astype(BF16), lij.astype(BF16)
            hj = h_ref[:, cs]
            dyf = dy_ref[:, cs]
            dyj = dyf.astype(BF16)
            dbre_ref[j] += lax.dot_general(hj, lrb, dims_tn, preferred_element_type=F32)
            dbim_ref[j] += lax.dot_general(hj, lib, dims_tn, preferred_element_type=F32)
            dcre_ref[j] += lax.dot_general(xrj, dyj, dims_tn, preferred_element_type=F32)
            dcim_ref[j] -= lax.dot_general(xij, dyj, dims_tn, preferred_element_type=F32)
            du = (lax.dot_general(lrb, bre_ref[j], dims_nt, preferred_element_type=F32)
                  + lax.dot_general(lib, bim_ref[j], dims_nt, preferred_element_type=F32))
            dh_ref[:, cs] = du + d_ref[:, cs] * dyf
            dd_ref[:, cs] += jnp.sum(dyf * hj.astype(F32), axis=0, keepdims=True)

    tok = lambda w: pl.BlockSpec((L, w), lambda b, c: (b * nc + nc - 1 - c, 0))
    whole = lambda p: pl.BlockSpec(p.shape, lambda b, c, nd=p.ndim: (0,) * nd)
    prev_end = pl.BlockSpec((1, 1, GN), lambda b, c: (b * nc + jnp.maximum(nc - 2 - c, 0), 0, 0))
    params = (abr, abi, bre, bim, cre, cim, d)
    acc_shapes = [bre.shape, bim.shape, cre.shape, cim.shape, abr.shape, abi.shape, d.shape]
    out = pl.pallas_call(
        body, grid=(B, nc),
        in_specs=[tok(D), tok(D), tok(GN), tok(GN), prev_end, prev_end] + [whole(p) for p in params],
        out_specs=[tok(D)] + [pl.BlockSpec(s, lambda b, c, nd=len(s): (0,) * nd) for s in acc_shapes],
        out_shape=[jax.ShapeDtypeStruct((T, D), F32)] + [jax.ShapeDtypeStruct(s, F32) for s in acc_shapes],
        scratch_shapes=[pltpu.VMEM((L, GN), F32), pltpu.VMEM((L, GN), F32),
                        pltpu.VMEM((1, GN), F32), pltpu.VMEM((1, GN), F32)],
        compiler_params=_cp(("arbitrary", "arbitrary")), name=name,
    )(dy, h, xr, xi, er, ei, *params)
    return out


def _conv_fwd(zp, w, *, R, tc, name):
    B, SP, C = zp.shape
    S = SP - CONV_HALO
    R, tc = min(R, S), min(tc, C)

    def body(z_ref, w_ref, y_ref):
        def chunk(ci, _):
            start = pl.multiple_of(ci * R, 8)
            ze = z_ref[pl.ds(start, R + CONV_HALO), :]
            acc = jnp.zeros((R, tc), F32)
            for m in range(CONV_WIDTH):
                k = CONV_WIDTH - 1 - m
                sh = ze if m == 0 else pltpu.roll(ze, m, 0)
                acc = acc + w_ref[k:k + 1, :] * sh[CONV_HALO:, :]
            y_ref[pl.ds(start, R), :] = acc
            return 0

        lax.fori_loop(0, S // R, chunk, 0)

    return pl.pallas_call(
        body, grid=(B, C // tc),
        in_specs=[pl.BlockSpec((None, SP, tc), lambda b, c: (b, 0, c)),
                  pl.BlockSpec((32, tc), lambda b, c: (0, c))],
        out_specs=pl.BlockSpec((None, S, tc), lambda b, c: (b, 0, c)),
        out_shape=jax.ShapeDtypeStruct((B, S, C), F32),
        compiler_params=_cp(("parallel", "parallel")), name=name,
    )(zp, w)


def _conv_bwd(zp, dyp, w, *, R, tc, name):
    B, SP, C = zp.shape
    S = SP - CONV_HALO
    R, tc = min(R, S), min(tc, C)

    def body(z_ref, dy_ref, w_ref, dz_ref, dw_ref):
        @pl.when(pl.program_id(1) == 0)
        def _():
            dw_ref[...] = jnp.zeros_like(dw_ref)

        def chunk(ci, _):
            start = pl.multiple_of(ci * R, 8)
            ze = z_ref[pl.ds(start, R + CONV_HALO), :]
            de = dy_ref[pl.ds(start, R + CONV_HALO), :]
            dy = de[:R, :]
            acc = jnp.zeros((R, tc), F32)
            for m in range(CONV_WIDTH):
                k = CONV_WIDTH - 1 - m
                zs = ze if m == 0 else pltpu.roll(ze, m, 0)
                ds_ = de if m == 0 else pltpu.roll(de, R + CONV_HALO - m, 0)
                acc = acc + w_ref[k:k + 1, :] * ds_[:R, :]
                dw_ref[k:k + 1, :] += jnp.sum(dy * zs[CONV_HALO:, :], axis=0, keepdims=True)
            dz_ref[pl.ds(start, R), :] = acc
            return 0

        lax.fori_loop(0, S // R, chunk, 0)

    return pl.pallas_call(
        body, grid=(C // tc, B),
        in_specs=[pl.BlockSpec((None, SP, tc), lambda c, b: (b, 0, c)),
                  pl.BlockSpec((None, SP, tc), lambda c, b: (b, 0, c)),
                  pl.BlockSpec((32, tc), lambda c, b: (0, c))],
        out_specs=[pl.BlockSpec((None, S, tc), lambda c, b: (b, 0, c)),
                   pl.BlockSpec((32, tc), lambda c, b: (0, c))],
        out_shape=[jax.ShapeDtypeStruct((B, S, C), F32), jax.ShapeDtypeStruct((32, C), F32)],
        compiler_params=_cp(("parallel", "arbitrary")), name=name,
    )(zp, dyp, w)


def _gmlp_fwd(u, vn, ws, bcol, *, nck, name):
    T, E = u.shape
    H = ws.shape[0]
    he = E // H
    rows = nck * GMLP_CHUNK
    rows = min(rows, T)
    n_in = rows // GMLP_CHUNK

    def body(u_ref, v_ref, ws_ref, b_ref, o_ref):
        for c in range(n_in):
            rs = slice(c * GMLP_CHUNK, (c + 1) * GMLP_CHUNK)
            for hh in range(H):
                cs = slice(hh * he, (hh + 1) * he)
                v2 = jnp.dot(ws_ref[hh], v_ref[rs, cs].astype(BF16), preferred_element_type=F32)
                v2 = v2 + b_ref[:, hh:hh + 1]
                o_ref[rs, cs] = (u_ref[rs, cs] * v2).astype(o_ref.dtype)

    tok = pl.BlockSpec((rows, E), lambda i: (i, 0))
    return pl.pallas_call(
        body, grid=(T // rows,),
        in_specs=[tok, tok, pl.BlockSpec(ws.shape, lambda i: (0, 0, 0)), pl.BlockSpec(bcol.shape, lambda i: (0, 0))],
        out_specs=tok, out_shape=jax.ShapeDtypeStruct((T, E), BF16),
        compiler_params=_cp(("parallel",)), name=name,
    )(u, vn, ws, bcol)


def _gmlp_bwd(duv, u, vn, ws, bcol, *, nck, name):
    T, E = u.shape
    H = ws.shape[0]
    he = E // H
    rows = min(nck * GMLP_CHUNK, T)
    n_in = rows // GMLP_CHUNK
    dims_nt = (((1,), (1,)), ((), ()))
    dims_tn = (((0,), (0,)), ((), ()))

    def body(g_ref, u_ref, v_ref, ws_ref, b_ref, du_ref, dv_ref, dws_ref, db_ref):
        @pl.when(pl.program_id(0) == 0)
        def _():
            dws_ref[...] = jnp.zeros_like(dws_ref)
            db_ref[...] = jnp.zeros_like(db_ref)

        for c in range(n_in):
            rs = slice(c * GMLP_CHUNK, (c + 1) * GMLP_CHUNK)
            for hh in range(H):
                cs = slice(hh * he, (hh + 1) * he)
                vb = v_ref[rs, cs].astype(BF16)
                v2 = jnp.dot(ws_ref[hh], vb, preferred_element_type=F32) + b_ref[:, hh:hh + 1]
                g = g_ref[rs, cs]
                du_ref[rs, cs] = g * v2
                dv2 = g * u_ref[rs, cs]
                dv2b = dv2.astype(BF16)
                dv_ref[rs, cs] = lax.dot_general(ws_ref[hh], dv2b, dims_tn, preferred_element_type=F32)
                dws_ref[hh] += lax.dot_general(dv2b, vb, dims_nt, preferred_element_type=F32)
                db_ref[:, hh:hh + 1] += jnp.sum(dv2, axis=1, keepdims=True)

    tok = pl.BlockSpec((rows, E), lambda i: (i, 0))
    return pl.pallas_call(
        body, grid=(T // rows,),
        in_specs=[tok, tok, tok, pl.BlockSpec(ws.shape, lambda i: (0, 0, 0)), pl.BlockSpec(bcol.shape, lambda i: (0, 0))],
        out_specs=[tok, tok, pl.BlockSpec(ws.shape, lambda i: (0, 0, 0)), pl.BlockSpec(bcol.shape, lambda i: (0, 0))],
        out_shape=[jax.ShapeDtypeStruct((T, E), F32), jax.ShapeDtypeStruct((T, E), F32),
                   jax.ShapeDtypeStruct(ws.shape, F32), jax.ShapeDtypeStruct(bcol.shape, F32)],
        compiler_params=_cp(("arbitrary",)), name=name,
    )(duv, u, vn, ws, bcol)


PAIRS = ATT_HEADS // 2


def _att_consts():
    ji = lax.broadcasted_iota(jnp.int32, (2 * ATT_BLK, ATT_BLK), 0)
    ii = lax.broadcasted_iota(jnp.int32, (2 * ATT_BLK, ATT_BLK), 1)
    dist = ii + ATT_BLK - ji
    band = (dist >= 0) & (dist <= ATT_BLK)
    cur = ji >= ATT_BLK
    first_head = lax.broadcasted_iota(jnp.int32, (ATT_BLK, 2 * HEAD_DIM), 1) < HEAD_DIM
    return band, cur, first_head


def _both_heads(t, first_head):
    zero = jnp.zeros_like(t)
    return jnp.concatenate([jnp.where(first_head, t, zero), jnp.where(first_head, zero, t)], axis=0)


def _att_specs(nbk, offs, nsteps, rev):
    rows = nbk * ATT_BLK
    step = (lambda i: nsteps - 1 - i) if rev else (lambda i: i)
    qoff, koff, voff = offs
    blk = lambda off: pl.BlockSpec((rows, 2 * HEAD_DIM), lambda hp, i: (step(i), off + hp))
    prev = lambda off: pl.BlockSpec((ATT_BLK, 2 * HEAD_DIM), lambda hp, i: (jnp.maximum(step(i) * nbk - 1, 0), off + hp))
    out = pl.BlockSpec((rows, 2 * HEAD_DIM), lambda hp, i: (step(i), hp))
    stat = pl.BlockSpec((2, nbk, ATT_BLK), lambda hp, i: (hp, step(i), 0))
    return [blk(qoff), blk(koff), prev(koff), blk(voff), prev(voff)], out, stat


def _att_fwd(arr, offs, *, nb, nbk, name):
    T = arr.shape[0]
    nbk = min(nbk, T // ATT_BLK)
    nsteps = T // (nbk * ATT_BLK)
    scale = HEAD_DIM ** -0.5
    dims_nt = (((1,), (1,)), ((), ()))
    dims_tn = (((0,), (0,)), ((), ()))

    def body(q_ref, k_ref, kp_ref, v_ref, vp_ref, o_ref, lse_ref):
        i = pl.program_id(1)
        band, cur, first_head = _att_consts()
        for jj in range(nbk):
            rs = slice(jj * ATT_BLK, (jj + 1) * ATT_BLK)
            ps = slice((jj - 1) * ATT_BLK, jj * ATT_BLK)
            has_prev = ((i * nbk + jj) & (nb - 1)) != 0
            valid = band & (cur | has_prev)
            kk = jnp.concatenate([kp_ref[...] if jj == 0 else k_ref[ps, :], k_ref[rs, :]], axis=0)
            vv = jnp.concatenate([vp_ref[...] if jj == 0 else v_ref[ps, :], v_ref[rs, :]], axis=0)
            q2 = _both_heads(q_ref[rs, :], first_head)
            st = lax.dot_general(kk, q2, dims_nt, preferred_element_type=F32) * scale
            st = jnp.where(jnp.concatenate([valid, valid], axis=1), st, MASK_VALUE)
            m = jnp.max(st, axis=0, keepdims=True)
            p = jnp.exp(st - m)
            l = jnp.sum(p, axis=0, keepdims=True)
            lse = m + jnp.log(l)
            lse_ref[0, jj:jj + 1, :] = lse[:, :ATT_BLK]
            lse_ref[1, jj:jj + 1, :] = lse[:, ATT_BLK:]
            pn = (p / l).astype(BF16)
            o2 = lax.dot_general(pn, vv, dims_tn, preferred_element_type=F32)
            o_ref[rs, :] = jnp.where(first_head, o2[:ATT_BLK], o2[ATT_BLK:])

    ins, out, stat = _att_specs(nbk, offs, nsteps, False)
    return pl.pallas_call(
        body, grid=(PAIRS, nsteps), in_specs=ins, out_specs=[out, stat],
        out_shape=[jax.ShapeDtypeStruct((T, ATT_W), F32), jax.ShapeDtypeStruct((ATT_HEADS, T // ATT_BLK, ATT_BLK), F32)],
        compiler_params=_cp(("parallel", "parallel")), name=name,
    )(arr, arr, arr, arr, arr)


def _att_bwd(arr, offs, do, lse, dlse, *, nb, nbk, name):
    T = arr.shape[0]
    nbk = min(nbk, T // ATT_BLK)
    nsteps = T // (nbk * ATT_BLK)
    scale = HEAD_DIM ** -0.5
    dims_nt = (((1,), (1,)), ((), ()))
    dims_tn = (((0,), (0,)), ((), ()))

    def body(q_ref, k_ref, kp_ref, v_ref, vp_ref, do_ref, lse_ref, dlse_ref, dq_ref, dk_ref, dv_ref, ck, cv):
        step = pl.program_id(1)
        i = nsteps - 1 - step
        band, cur, first_head = _att_consts()

        @pl.when(step == 0)
        def _():
            ck[...] = jnp.zeros_like(ck)
            cv[...] = jnp.zeros_like(cv)

        carry_k, carry_v = ck[...], cv[...]
        for jj in reversed(range(nbk)):
            rs = slice(jj * ATT_BLK, (jj + 1) * ATT_BLK)
            ps = slice((jj - 1) * ATT_BLK, jj * ATT_BLK)
            has_prev = ((i * nbk + jj) & (nb - 1)) != 0
            valid = band & (cur | has_prev)
            kk = jnp.concatenate([kp_ref[...] if jj == 0 else k_ref[ps, :], k_ref[rs, :]], axis=0)
            vv = jnp.concatenate([vp_ref[...] if jj == 0 else v_ref[ps, :], v_ref[rs, :]], axis=0)
            q2 = _both_heads(q_ref[rs, :], first_head)
            do2 = _both_heads(do_ref[rs, :].astype(BF16), first_head)
            lse = jnp.concatenate([lse_ref[0, jj:jj + 1, :], lse_ref[1, jj:jj + 1, :]], axis=1)
            dlse = jnp.concatenate([dlse_ref[0, jj:jj + 1, :], dlse_ref[1, jj:jj + 1, :]], axis=1)
            st = lax.dot_general(kk, q2, dims_nt, preferred_element_type=F32) * scale
            st = jnp.where(jnp.concatenate([valid, valid], axis=1), st, MASK_VALUE)
            p = jnp.exp(st - lse)
            dp = lax.dot_general(vv, do2, dims_nt, preferred_element_type=F32)
            delta = jnp.sum(p * dp, axis=0, keepdims=True)
            dsb = (p * (dp - delta + dlse) * scale).astype(BF16)
            dq2 = lax.dot_general(dsb, kk, dims_tn, preferred_element_type=F32)
            dkk = jnp.dot(dsb, q2, preferred_element_type=F32)
            dvv = jnp.dot(p.astype(BF16), do2, preferred_element_type=F32)
            dq_ref[rs, :] = jnp.where(first_head, dq2[:ATT_BLK], dq2[ATT_BLK:]).astype(dq_ref.dtype)
            dk_ref[rs, :] = (dkk[ATT_BLK:] + carry_k).astype(dk_ref.dtype)
            dv_ref[rs, :] = (dvv[ATT_BLK:] + carry_v).astype(dv_ref.dtype)
            carry_k, carry_v = dkk[:ATT_BLK], dvv[:ATT_BLK]
        ck[...] = carry_k
        cv[...] = carry_v

    ins, out, stat = _att_specs(nbk, offs, nsteps, True)
    return pl.pallas_call(
        body, grid=(PAIRS, nsteps), in_specs=ins + [out, stat, stat], out_specs=[out] * 3,
        out_shape=[jax.ShapeDtypeStruct((T, ATT_W), BF16)] * 3,
        scratch_shapes=[pltpu.VMEM((ATT_BLK, 2 * HEAD_DIM), F32), pltpu.VMEM((ATT_BLK, 2 * HEAD_DIM), F32)],
        compiler_params=_cp(("arbitrary", "arbitrary")), name=name,
    )(arr, arr, arr, arr, arr, do, lse, dlse)


def _deinterleave(t, B, S, dil):
    if dil == 1:
        return t
    return t.reshape((B, S // dil, dil) + t.shape[1:]).swapaxes(1, 2).reshape(t.shape)


def _interleave(t, B, S, dil):
    if dil == 1:
        return t
    return t.reshape((B, dil, S // dil) + t.shape[1:]).swapaxes(1, 2).reshape(t.shape)


def _stats_to_tokens(lse, B, S, dil):
    return _interleave(lse.reshape(lse.shape[0], -1).T, B, S, dil)


def _stats_from_tokens(dl, B, S, dil):
    return _deinterleave(dl, B, S, dil).T.reshape(dl.shape[1], -1, ATT_BLK)


def _mesh_pos():
    return lax.axis_index("x"), lax.axis_index("y"), lax.axis_index("c")


def _allgather8(xs, *, name):
    m_per, n = xs.shape

    def body(x_ref, out_ref, send_sems, recv_sems, local_sem):
        x, y, c = _mesh_pos()
        me, sibling = (x, y, c), (x, y, 1 - c)
        chips = [(1 - x, y), (x, 1 - y), (1 - x, 1 - y)]

        def rows(px, py, pc):
            return out_ref.at[pl.ds((4 * px + 2 * py + pc) * m_per, m_per), :]

        def copy(k, block, to, src=None):
            return pltpu.make_async_remote_copy(
                src_ref=rows(*block) if src is None else src, dst_ref=rows(*block),
                send_sem=send_sems.at[k], recv_sem=recv_sems.at[k], device_id=to, device_id_type=MESH)

        mine = pltpu.make_async_copy(x_ref, rows(*me), local_sem)
        mine.start()
        first = [copy(0, me, sibling, src=x_ref)]
        first += [copy(1 + j, me, (*chip, c), src=x_ref) for j, chip in enumerate(chips)]
        for cp in first:
            cp.start()
        passed = [copy(4 + j, (*chip, c), sibling) for j, chip in enumerate(chips)]
        for j, chip in enumerate(chips):
            copy(1 + j, (*chip, c), me).wait_recv()
            passed[j].start()
        copy(0, sibling, me).wait_recv()
        for j, chip in enumerate(chips):
            copy(4 + j, (*chip, 1 - c), me).wait_recv()
        for cp in first + passed:
            cp.wait_send()
        mine.wait()

    return pl.pallas_call(
        body, out_shape=jax.ShapeDtypeStruct((8 * m_per, n), xs.dtype),
        in_specs=[pl.BlockSpec(memory_space=pltpu.VMEM)], out_specs=pl.BlockSpec(memory_space=pltpu.VMEM),
        scratch_shapes=[pltpu.SemaphoreType.DMA((7,)), pltpu.SemaphoreType.DMA((7,)), pltpu.SemaphoreType.DMA],
        compiler_params=pltpu.CompilerParams(vmem_limit_bytes=VMEM_LIMIT), name=name,
    )(xs)


def _hbm_call(body, arrays, out_shapes, n_sems, *, name):
    any_spec = pl.BlockSpec(memory_space=pl.ANY)
    return pl.pallas_call(
        body, out_shape=out_shapes, in_specs=[any_spec] * len(arrays), out_specs=[any_spec] * len(out_shapes),
        scratch_shapes=[pltpu.SemaphoreType.DMA((n_sems,)), pltpu.SemaphoreType.DMA((n_sems,))], name=name,
    )(*arrays)


def _other_chips(x, y):
    return [(1 - x, y), (x, 1 - y), (1 - x, 1 - y)]


def _allgather_chips(ws, *, name):
    n = len(ws)

    def body(*refs):
        ins, outs, (send_sems, recv_sems) = refs[:n], refs[n:2 * n], refs[2 * n:]
        x, y, c = _mesh_pos()
        chips = _other_chips(x, y)

        def copy(a, k, px, py, half, to, src=None):
            slot = outs[a].at[2 * px + py, half]
            return pltpu.make_async_remote_copy(
                src_ref=slot if src is None else src, dst_ref=slot,
                send_sem=send_sems.at[6 * a + k], recv_sem=recv_sems.at[6 * a + k], device_id=to, device_id_type=MESH)

        first = [copy(a, j, x, y, c, (*chip, c), src=ins[a].at[c]) for a in range(n) for j, chip in enumerate(chips)]
        for cp in first:
            cp.start()
        passed = []
        for j, chip in enumerate(chips):
            for a in range(n):
                copy(a, j, *chip, c, (x, y, c)).wait_recv()
                passed.append(copy(a, 3 + j, *chip, c, (x, y, 1 - c)))
                passed[-1].start()
        for j, chip in enumerate(chips):
            for a in range(n):
                copy(a, 3 + j, *chip, 1 - c, (x, y, c)).wait_recv()
        for cp in first + passed:
            cp.wait_send()

    return _hbm_call(body, ws, [jax.ShapeDtypeStruct((N_CHIPS,) + w.shape, w.dtype) for w in ws], 6 * n, name=name)


def _split_start(srcs, lands, after, issue, n_sems, *, name):
    ns, nl = len(srcs), len(lands)
    hbm, sem = pl.BlockSpec(memory_space=pltpu.HBM), pl.BlockSpec(memory_space=pltpu.SEMAPHORE)
    extra = [] if after is None else [after]

    def body(*refs):
        n_in = ns + nl + len(extra)
        send_sems, recv_sems = refs[n_in], refs[n_in + 1]
        issue(refs[:ns], refs[ns:ns + nl], send_sems, recv_sems)
        refs[-1][...] = jnp.zeros_like(refs[-1])

    arrays = [pltpu.with_memory_space_constraint(a, pltpu.HBM) for a in list(srcs) + list(lands)]
    out = pl.pallas_call(
        body, name=name,
        out_shape=(pltpu.SemaphoreType.DMA((n_sems,)), pltpu.SemaphoreType.DMA((n_sems,)),
                   *[pltpu.HBM(a.shape, a.dtype) for a in arrays], jax.ShapeDtypeStruct((8, 128), F32)),
        in_specs=[hbm] * (ns + nl) + [pl.BlockSpec(memory_space=pl.ANY)] * len(extra),
        out_specs=(sem, sem, *[hbm] * (ns + nl), pl.BlockSpec(memory_space=pltpu.VMEM)),
        input_output_aliases={i: 2 + i for i in range(ns + nl)},
        compiler_params=pltpu.CompilerParams(has_side_effects=pltpu.SideEffectType.DATAFLOW_SIDE_EFFECTING),
    )(*arrays, *extra)
    return out[0], out[1], list(out[2:2 + ns]), list(out[2 + ns:2 + ns + nl]), out[-1]


def _split_wait(send_sems, recv_sems, srcs, lands, after, waits, *, name):
    ns, nl = len(srcs), len(lands)
    hbm, sem = pl.BlockSpec(memory_space=pltpu.HBM), pl.BlockSpec(memory_space=pltpu.SEMAPHORE)

    def body(*refs):
        waits(refs[:ns], refs[ns:ns + nl], refs[ns + nl], refs[ns + nl + 1])

    out = pl.pallas_call(
        body, name=name,
        out_shape=tuple(pltpu.HBM(a.shape, a.dtype) for a in list(srcs) + list(lands)),
        in_specs=[hbm] * (ns + nl) + [sem, sem, pl.BlockSpec(memory_space=pl.ANY)],
        out_specs=tuple([hbm] * (ns + nl)),
        input_output_aliases={i: i for i in range(ns + nl)},
        compiler_params=pltpu.CompilerParams(has_side_effects=pltpu.SideEffectType.DATAFLOW_SIDE_EFFECTING),
    )(*srcs, *lands, send_sems, recv_sems, after)
    return list(out[:ns]), list(out[ns:])


def _gather_start(halves, after, *, name):
    n = len(halves)
    lands = [lax.empty((N_CHIPS,) + h.shape, h.dtype) for h in halves]

    def issue(srcs, dsts, send_sems, recv_sems):
        x, y, c = _mesh_pos()
        me = 2 * x + y
        for a in range(n):
            for j, (px, py) in enumerate(_other_chips(x, y)):
                for cc in range(2):
                    pltpu.make_async_remote_copy(
                        src_ref=srcs[a].at[c], dst_ref=dsts[a].at[me, c],
                        send_sem=send_sems.at[6 * a + 2 * j + cc], recv_sem=recv_sems.at[6 * a + 2 * j + c],
                        device_id=(px, py, cc), device_id_type=MESH).start()

    return _split_start(halves, lands, after, issue, 6 * n, name=name)


def _gather_wait(started, after, *, name):
    send_sems, recv_sems, halves, lands = started
    n = len(halves)

    def waits(srcs, dsts, send_sems, recv_sems):
        x, y, c = _mesh_pos()
        me = 2 * x + y
        for a in range(n):
            for j, (px, py) in enumerate(_other_chips(x, y)):
                for cc in range(2):
                    pltpu.make_async_remote_copy(
                        src_ref=srcs[a].at[cc], dst_ref=dsts[a].at[2 * px + py, cc],
                        send_sem=send_sems.at[6 * a + 2 * j + cc], recv_sem=recv_sems.at[6 * a + 2 * j + cc],
                        device_id=(px, py, cc), device_id_type=MESH).wait_recv()
        for a in range(n):
            for j, (px, py) in enumerate(_other_chips(x, y)):
                for cc in range(2):
                    pltpu.make_async_remote_copy(
                        src_ref=srcs[a].at[c], dst_ref=dsts[a].at[me, c],
                        send_sem=send_sems.at[6 * a + 2 * j + cc], recv_sem=recv_sems.at[6 * a + 2 * j + c],
                        device_id=(px, py, cc), device_id_type=MESH).wait_send()

    return _split_wait(send_sems, recv_sems, halves, lands, after, waits, name=name)


def _reduce_plan_loops(plans, chip, c, fn):
    for a, plan in enumerate(plans):
        for h, cc in plan:
            for k in range(N_CHIPS):
                fn(a, h, k, cc, jnp.logical_or(chip != k, c != cc))


def _reduce_start(srcs, lands, plans, after, *, name):
    def issue(src_refs, land_refs, send_sems, recv_sems):
        x, y, c = _mesh_pos()
        chip = 2 * x + y
        my_id = 2 * chip + c

        def send(a, h, k, cc, is_other):
            @pl.when(is_other)
            def _():
                pltpu.make_async_remote_copy(
                    src_ref=src_refs[a].at[h, k], dst_ref=land_refs[a].at[my_id],
                    send_sem=send_sems.at[8 * a + 2 * k + cc], recv_sem=recv_sems.at[8 * a + my_id],
                    device_id=(k // 2, k % 2, cc), device_id_type=MESH).start()

        _reduce_plan_loops(plans, chip, c, send)

    return _split_start(srcs, lands, after, issue, 8 * len(srcs), name=name)


def _reduce_wait(started, plans, after, *, name):
    send_sems, recv_sems, srcs, lands = started

    def waits(src_refs, land_refs, send_sems, recv_sems):
        x, y, c = _mesh_pos()
        chip = 2 * x + y
        my_id = 2 * chip + c
        for a, plan in enumerate(plans):
            for h, cc in plan:
                for s in range(2 * N_CHIPS):
                    @pl.when(jnp.logical_and(c == cc, my_id != s))
                    def _(a=a, h=h, s=s):
                        pltpu.make_async_remote_copy(
                            src_ref=src_refs[a].at[h, 0], dst_ref=land_refs[a].at[s],
                            send_sem=send_sems.at[8 * a + s], recv_sem=recv_sems.at[8 * a + s],
                            device_id=(s // 4, (s // 2) % 2, s % 2), device_id_type=MESH).wait_recv()

        def sent(a, h, k, cc, is_other):
            @pl.when(is_other)
            def _():
                pltpu.make_async_remote_copy(
                    src_ref=src_refs[a].at[h, k], dst_ref=land_refs[a].at[my_id],
                    send_sem=send_sems.at[8 * a + 2 * k + cc], recv_sem=recv_sems.at[8 * a + my_id],
                    device_id=(k // 2, k % 2, cc), device_id_type=MESH).wait_send()

        _reduce_plan_loops(plans, chip, c, sent)

    return _split_wait(send_sems, recv_sems, srcs, lands, after, waits, name=name)


def _sum8(land, own, my_id, *, name):
    n_src, R, C = land.shape
    tr = _pick_rows(R, max(8, 1024 * 1024 // (2 * C)))

    def body(id_ref, *refs):
        own_ref, o_ref = refs[n_src], refs[n_src + 1]
        me = id_ref[0]
        acc = None
        for s in range(n_src):
            term = jnp.where(me == s, own_ref[...], refs[s][...]).astype(F32)
            acc = term if acc is None else acc + term
        o_ref[...] = acc

    return pl.pallas_call(
        body, out_shape=jax.ShapeDtypeStruct((R, C), F32),
        grid_spec=pltpu.PrefetchScalarGridSpec(
            num_scalar_prefetch=1, grid=(R // tr,),
            in_specs=[pl.BlockSpec((None, tr, C), lambda i, idr, s=s: (s, i, 0)) for s in range(n_src)]
            + [pl.BlockSpec((tr, C), lambda i, idr: (i, 0))],
            out_specs=pl.BlockSpec((tr, C), lambda i, idr: (i, 0))),
        compiler_params=_cp(("parallel",)), name=name,
    )(my_id.reshape(1).astype(jnp.int32), *([land] * n_src), own)


def _swap_halves(gs, *, name):
    n = len(gs)

    def body(*refs):
        ins, outs, (send_sems, recv_sems) = refs[:n], refs[n:2 * n], refs[2 * n:]
        x, y, c = _mesh_pos()
        cps = [pltpu.make_async_remote_copy(
            src_ref=ins[a].at[1 - c], dst_ref=outs[a], send_sem=send_sems.at[a], recv_sem=recv_sems.at[a],
            device_id=(x, y, 1 - c), device_id_type=MESH) for a in range(n)]
        for cp in cps:
            cp.start()
        for cp in cps:
            cp.wait()

    return _hbm_call(body, gs, [jax.ShapeDtypeStruct(g.shape[1:], g.dtype) for g in gs], n, name=name)


def _scatter_chips(ss, *, name):
    n = len(ss)

    def body(*refs):
        ins, outs, (send_sems, recv_sems) = refs[:n], refs[n:2 * n], refs[2 * n:]
        x, y, c = _mesh_pos()
        me = 2 * x + y
        chips = _other_chips(x, y)

        def copy(a, j, px, py):
            return pltpu.make_async_remote_copy(
                src_ref=ins[a].at[2 * px + py], dst_ref=outs[a].at[me],
                send_sem=send_sems.at[3 * a + j], recv_sem=recv_sems.at[3 * a + j],
                device_id=(px, py, c), device_id_type=MESH)

        def arrival(a, j, px, py):
            return pltpu.make_async_remote_copy(
                src_ref=ins[a].at[me], dst_ref=outs[a].at[2 * px + py],
                send_sem=send_sems.at[3 * a + j], recv_sem=recv_sems.at[3 * a + j],
                device_id=(px, py, c), device_id_type=MESH)

        cps = [copy(a, j, *chip) for a in range(n) for j, chip in enumerate(chips)]
        for cp in cps:
            cp.start()
        for a in range(n):
            for j, chip in enumerate(chips):
                arrival(a, j, *chip).wait_recv()
        for cp in cps:
            cp.wait_send()

    return _hbm_call(body, ss, [jax.ShapeDtypeStruct(s.shape, s.dtype) for s in ss], 3 * n, name=name)


def _share_halves(ts, *, name):
    n = len(ts)

    def body(*refs):
        ins, outs, (send_sems, recv_sems) = refs[:n], refs[n:2 * n], refs[2 * n:]
        x, y, c = _mesh_pos()
        cps = [pltpu.make_async_remote_copy(
            src_ref=ins[a], dst_ref=outs[a].at[c], send_sem=send_sems.at[a], recv_sem=recv_sems.at[a],
            device_id=(x, y, 1 - c), device_id_type=MESH) for a in range(n)]
        for cp in cps:
            cp.start()
        for a in range(n):
            pltpu.make_async_remote_copy(
                src_ref=ins[a], dst_ref=outs[a].at[1 - c], send_sem=send_sems.at[a], recv_sem=recv_sems.at[a],
                device_id=(x, y, 1 - c), device_id_type=MESH).wait_recv()
        for cp in cps:
            cp.wait_send()

    return _hbm_call(body, ts, [jax.ShapeDtypeStruct((2,) + t.shape, t.dtype) for t in ts], n, name=name)


def _half_add(g, ra, core, *, name):
    _, R, C = g.shape
    tr = _pick_rows(R, max(8, 2 * 1024 * 1024 // (4 * C)))

    def body(core_ref, g_ref, ra_ref, o_ref):
        o_ref[...] = (g_ref[...] + ra_ref[...]).astype(o_ref.dtype)

    return pl.pallas_call(
        body, out_shape=jax.ShapeDtypeStruct((R, C), BF16),
        grid_spec=pltpu.PrefetchScalarGridSpec(
            num_scalar_prefetch=1, grid=(R // tr,),
            in_specs=[pl.BlockSpec((None, tr, C), lambda i, cr: (cr[0], i, 0)),
                      pl.BlockSpec((tr, C), lambda i, cr: (i, 0))],
            out_specs=pl.BlockSpec((tr, C), lambda i, cr: (i, 0))),
        compiler_params=_cp(("parallel",)), name=name,
    )(core.reshape(1).astype(jnp.int32), g, ra)


def _sum4(rb, *, name):
    _, R, C = rb.shape
    tr = _pick_rows(R, max(8, 2 * 1024 * 1024 // (4 * C)))

    def body(r0, r1, r2, r3, o_ref):
        f = lambda r: r[...].astype(F32)
        o_ref[...] = ((f(r0) + f(r1)) + f(r2)) + f(r3)

    return pl.pallas_call(
        body, out_shape=jax.ShapeDtypeStruct((R, C), F32), grid=(R // tr,),
        in_specs=[pl.BlockSpec((None, tr, C), lambda i, k=k: (k, i, 0)) for k in range(N_CHIPS)],
        out_specs=pl.BlockSpec((tr, C), lambda i: (i, 0)),
        compiler_params=_cp(("parallel",)), name=name,
    )(rb, rb, rb, rb)


TR = 256
S5_CHUNK = 256


def _rms_fwd(x, g, name):
    return _rowwise(_f_rms, [x], [g], (BF16,), tr=TR, name=name)[0]


def _rms_bwd_epi(dh, x, g, gx):
    _, vjp = jax.vjp(lambda xv, gv: _f_rms(xv, gv)[0], x, g)
    dx, dg = vjp(dh)
    dx = dx + gx
    return dx, dx, dg


def _mm_rms_bwd(a, w, x, g, gx, *, after=None, name, **kw):
    return _mm(a, w, tb=True, epi=_rms_bwd_epi, extras=(x, g, gx), out_dtypes=(F32, BF16, F32), n_row_sums=1,
               tm=512, tn=x.shape[1], after=after, name=name, **kw)


def _rms_bwd(x, g, dh, gx, name, after=None):
    (dx, dxb), (dg,) = _rowwise_vjp(_f_rms, [x], [g], [dh], [(F32, BF16)], adds={0: gx}, after=after, tr=TR,
                                    name=name)
    return dx, dxb, dg


def _grad_cols(M, Nq):
    def imap(tm, tn):
        hp, per = (M // 2) // tm, Nq // tn
        assert hp * tm * 2 == M and per * tn == Nq, (M, Nq, tm, tn)
        return lambda i, j, k: (i // hp, j // per, i % hp, j % per)
    return (2, N_CHIPS, M // 2, Nq), lambda tm, tn: (None, None, tm, tn), imap, None, M // 2, Nq


def _grad_rows(Mq, N):
    def imap(tm, tn):
        po, hp = Mq // tm, (Mq // 2) // tm
        assert hp * tm * 2 == Mq, (Mq, tm)
        return lambda i, j, k: ((i % po) // hp, i // po, (i % po) % hp, j)
    return (2, N_CHIPS, Mq // 2, N), lambda tm, tn: (None, None, tm, tn), imap, None, Mq // 2, N


def _grad_layer_cols(slot, lh, M, Nq, buf):
    def imap(tm, tn):
        per = Nq // tn
        return lambda i, j, k: (j // per, slot, i, j % per)
    return (N_CHIPS, lh, M, Nq), lambda tm, tn: (None, None, tm, tn), imap, buf, M, Nq


def _grad_layer_rows(slot, lh, Mq, N, buf):
    def imap(tm, tn):
        po = Mq // tm
        return lambda i, j, k: (i // po, slot, i % po, j)
    return (N_CHIPS, lh, Mq, N), lambda tm, tn: (None, None, tm, tn), imap, buf, Mq, N


def _add_then_rms(acc, res, g):
    xo = acc + res
    return xo, _f_rms(xo, g)[0]


def _mlp_fwd(x, h2, w_in, w_out, g_next, li):
    r = _mm(h2, w_in, out_dtypes=(BF16,), epi=lambda acc: (jnp.maximum(acc, 0.0),), tm=2048, name=f"mlp_in_{li}")
    if g_next is None:
        x_out, h_next = _mm(r, w_out, pro_a=lambda t: t * t, epi=lambda acc, res: (acc + res,), extras=(x,),
                            name=f"mlp_out_{li}"), None
    else:
        x_out, h_next = _mm(r, w_out, pro_a=lambda t: t * t, epi=_add_then_rms, extras=(x, g_next),
                            out_dtypes=(F32, BF16), tn=x.shape[1], name=f"mlp_out_{li}")
    return x_out, h_next, (h2, r)


def _mlp_bwd(gx, gxb, x, g, w_in, w_out, saved, li, nl, bufs):
    h2, r = saved
    D, F = w_in.shape
    lh = nl // 2
    da = _mm(gxb, w_out, tb=True, out_dtypes=(BF16,),
             epi=lambda acc, rt: (acc * 2.0 * rt.astype(F32),), extras=(r,), tm=2048, name=f"mlp_dact_{li}")
    buf_in, buf_out = bufs if bufs is not None else (None, None)
    d_w_out = _mm(r, gxb, ta=True, pro_a=lambda t: t * t, tm=512, tn=1024, tk=2048, out_dtypes=(BF16,),
                  out=_grad_layer_rows(li % lh, lh, F // N_CHIPS, D, buf_out), name=f"mlp_dwout_{li}")
    d_w_in = _mm(h2, da, ta=True, tm=1024, tn=1024, tk=2048, out_dtypes=(BF16,),
                 out=_grad_layer_cols(li % lh, lh, D, F // N_CHIPS, buf_in), name=f"mlp_dwin_{li}")
    gx_mid, gxb_mid, dg = _mm_rms_bwd(da, w_in, x, g, gx, name=f"mlp_dh_{li}")
    return gx_mid, gxb_mid, dg, (d_w_in, d_w_out)


def _local_step(x3, tgt3, p, layer_weights, token=None, grads_done=lambda group: None):
    B, S, D = x3.shape
    T = B * S
    x = x3.reshape(T, D)
    grads = {}
    row = lambda v: v.reshape(1, -1)
    p = dict(p)
    nl = p["norm_mlp"].shape[0]
    mlp_in, mlp_out = [None] * nl, [None] * nl

    def fetch(li, after):
        wl = dict(layer_weights(li, after))
        mlp_in[li], mlp_out[li] = wl.pop("mlp_w_in"), wl.pop("mlp_w_out")
        p.update(wl)

    g0 = row(p["norm_mix"][0])
    if token is not None:
        g0 = g0 + token[:1, :1]
    h0 = _rms_fwd(x, g0, "rms_mix_0")
    s5_args = (p["ssm_a_re"][0], p["ssm_a_im"][0], p["ssm_b_re"][0], p["ssm_b_im"][0],
               p["ssm_c_re"][0], p["ssm_c_im"][0], p["ssm_log_dt"][0])
    s5_exp, s5_vjp = jax.vjp(_s5_prep, *s5_args)
    abr, abi, bre, bim, cre, cim = s5_exp
    bre_b, bim_b, cre_b, cim_b = (t.astype(BF16) for t in (bre, bim, cre, cim))
    d_skip = p["ssm_d"]
    ypre, yb, sxr, sxi, ser, sei = _s5_fwd(h0, abr, abi, bre_b, bim_b, cre_b, cim_b, d_skip, B=B, L=S5_CHUNK,
                                           name="s5_fwd")
    fetch(0, yb)
    w_glu = p["ssm_w_glu"]
    z0 = _mm(yb, w_glu, name="s5_glu_mm")
    gm = [row(p["norm_mlp"][i]) for i in range(nl)]
    g1, g2, g3 = (row(p["norm_mix"][i]) for i in range(1, nl))
    x_mid0, hm0 = _rowwise(lambda z, xr, g: _add_then_rms(_f_glu(z)[0], xr, g), [z0, x], [gm[0]], (F32, BF16),
                           tr=TR, name="s5_glu")
    x1, h1, mlp_saved0 = _mlp_fwd(x_mid0, hm0, mlp_in[0], mlp_out[0], g1, 0)

    fetch(1, h1)
    z1 = _mm(h1, p["conv_w_pw1"], name="conv_pw1")
    zg = _rowwise(_f_bias_glu, [z1], [p["conv_b_pw1"]], (F32,), tr=TR, name="conv_glu")[0]
    zp = jnp.pad(zg.reshape(B, S, D), ((0, 0), (CONV_HALO, 0), (0, 0)))
    w_dw = jnp.pad(p["conv_w_dw"], ((0, 32 - CONV_WIDTH), (0, 0)))
    yc = _conv_fwd(zp, w_dw, R=256, tc=128, name="conv_dw").reshape(T, D)
    ln_par = [p["conv_b_dw"], p["conv_ln_g"], p["conv_ln_b"]]
    qc = _rowwise(_f_ln_silu, [yc], ln_par, (BF16,), tr=TR, name="conv_ln_silu")[0]
    x_mid1, hm1 = _mm(qc, p["conv_w_pw2"], epi=lambda acc, bias, res, g: _add_then_rms(acc + bias, res, g),
                      extras=(p["conv_b_pw2"], x1, gm[1]), out_dtypes=(F32, BF16), tn=D, name="conv_pw2")
    x2, h2, mlp_saved1 = _mlp_fwd(x_mid1, hm1, mlp_in[1], mlp_out[1], g2, 1)

    fetch(2, h2)
    z2 = _mm(h2, p["gmlp_w_in"], name="gmlp_in")
    gl_par = [p["gmlp_ln_g"], p["gmlp_ln_b"]]
    gu, gvn = _rowwise(_f_gelu_ln, [z2], gl_par, (F32, F32), tr=TR, name="gmlp_gelu_ln")
    causal = jnp.tril(jnp.ones((GMLP_CHUNK, GMLP_CHUNK), dtype=bool))
    ws_b = jnp.where(causal[None], p["gmlp_w_s"][0], 0.0).astype(BF16)
    bcol = jnp.pad(p["gmlp_b_s"][0].T, ((0, 0), (0, 128 - GMLP_HEADS)))
    uv = _gmlp_fwd(gu, gvn, ws_b, bcol, nck=4, name="gmlp_spatial")
    x_mid2, hm2 = _mm(uv, p["gmlp_w_out"], epi=_add_then_rms, extras=(x2, gm[2]), out_dtypes=(F32, BF16), tn=D,
                      name="gmlp_out")
    x3_, h3, mlp_saved2 = _mlp_fwd(x_mid2, hm2, mlp_in[2], mlp_out[2], g3, 2)

    fetch(3, h3)
    ng = len(ATT_DILS)
    att_in, o_tok, l_tok, lses = [], [], [], []
    offs = (0, PAIRS, 2 * PAIRS)
    for gi, dil in enumerate(ATT_DILS):
        w_g = _ColBlocks(p["attn_w_qkv_plain"], gi, ng, 3, ATT_W)
        arr = _mm(h3, w_g, out_dtypes=(BF16,), tn=ATT_W, name=f"attn_qkv_{gi}")
        arr = _deinterleave(arr, B, S, dil)
        att_in.append((arr, offs))
        og, lg = _att_fwd(arr, offs, nb=S // dil // ATT_BLK, nbk=8, name=f"attn_fwd_{gi}")
        lses.append(lg)
        o_tok.append(_interleave(og, B, S, dil))
        l_tok.append(_stats_to_tokens(lg, B, S, dil))
    merged2 = _rowwise(_f_merge, o_tok + l_tok, [], (BF16,), tr=TR, name="attn_merge")[0]
    x_mid3, hm3 = _mm(merged2, p["attn_w_o_plain"], epi=_add_then_rms, extras=(x3_, gm[3]), out_dtypes=(F32, BF16),
                      tn=D, name="attn_out")
    x4, _, mlp_saved3 = _mlp_fwd(x_mid3, hm3, mlp_in[3], mlp_out[3], None, 3)

    loss_part, gx, gxb, dgf = _loss_head(x4, tgt3.reshape(T, D), row(p["norm_final"]), tr=TR, name="loss_head")
    grads["norm_final"] = dgf.reshape(-1)
    d_norm_mix, d_norm_mlp = [None] * 4, [None] * 4
    Dq = D // N_CHIPS

    gx, gxb, d_norm_mlp[3], mlp_hi = _mlp_bwd(
        gx, gxb, x_mid3, row(p["norm_mlp"][3]), mlp_in[3], mlp_out[3], mlp_saved3, 3, nl, None)
    dmerged = _mm(gxb, p["attn_w_o"], tb=True, name="attn_dmerged")
    grads["attn_w_o"] = _mm(merged2, gxb, ta=True, tm=256, tn=256, tk=2048, out_dtypes=(BF16,), out=_grad_cols(ATT_W, Dq), name="attn_dwo")
    dml, _ = _rowwise_vjp(_f_merge, o_tok + l_tok, [], [dmerged], [F32] * 6, tr=TR, name="attn_merge_bwd")
    pieces = [[None] * ng for _ in range(3)]
    for gi, dil in enumerate(ATT_DILS):
        arr, offs = att_in[gi]
        dqkv_g = _att_bwd(arr, offs, _deinterleave(dml[gi], B, S, dil), lses[gi],
                          _stats_from_tokens(dml[ng + gi], B, S, dil), nb=S // dil // ATT_BLK, nbk=8,
                          name=f"attn_bwd_{gi}")
        for i in range(3):
            pieces[i][gi] = _interleave(dqkv_g[i], B, S, dil)
    dqkv = jnp.concatenate([pieces[i][gi] for i in range(3) for gi in range(ng)], axis=1)
    qkv_w = 3 * ng * ATT_W
    grads["attn_w_qkv"] = _mm(h3, dqkv, ta=True, tm=512, tn=1152, tk=2048, out_dtypes=(BF16,), out=_grad_cols(D, qkv_w // N_CHIPS),
                              name="attn_dwqkv")
    tok = grads_done({n: grads[n] for n in ("attn_w_qkv", "attn_w_o")})
    gx, gxb, d_norm_mix[3] = _mm_rms_bwd(dqkv, p["attn_w_qkv"], x3_, g3, gx, tk=1152, after=tok, name="attn_dh")

    gx, gxb, d_norm_mlp[2], mlp_hi = _mlp_bwd(
        gx, gxb, x_mid2, row(p["norm_mlp"][2]), mlp_in[2], mlp_out[2], mlp_saved2, 2, nl, mlp_hi)
    tok = grads_done({"mlp_w_in": (1, mlp_hi[0]), "mlp_w_out": (1, mlp_hi[1])})
    duv = _mm(gxb, p["gmlp_w_out"], tb=True, after=tok, name="gmlp_duv")
    grads["gmlp_w_out"] = _mm(uv, gxb, ta=True, tm=128, tn=1024, tk=2048, out_dtypes=(BF16,), out=_grad_rows(Dq, D), name="gmlp_dwout")
    du, dvn, dws, dbcol = _gmlp_bwd(duv, gu, gvn, ws_b, bcol, nck=4, name="gmlp_spatial_bwd")
    grads["gmlp_w_s"] = jnp.where(causal[None], dws, 0.0)[None]
    grads["gmlp_b_s"] = dbcol[:, :GMLP_HEADS].T[None]
    (dz2,), (dlg, dlb_) = _rowwise_vjp(_f_gelu_ln, [z2], gl_par, [du, dvn], [BF16], tr=TR, name="gmlp_gelu_ln_bwd")
    grads["gmlp_ln_g"], grads["gmlp_ln_b"] = dlg, dlb_
    grads["gmlp_w_in"] = _mm(h2, dz2, ta=True, tm=512, tn=512, tk=2048, out_dtypes=(BF16,), out=_grad_cols(D, 2 * Dq), name="gmlp_dwin")
    tok = grads_done({n: grads[n] for n in ("gmlp_w_in", "gmlp_w_out")})
    gx, gxb, d_norm_mix[2] = _mm_rms_bwd(dz2, p["gmlp_w_in"], x2, g2, gx, after=tok, name="gmlp_dh")

    gx, gxb, d_norm_mlp[1], mlp_lo = _mlp_bwd(
        gx, gxb, x_mid1, row(p["norm_mlp"][1]), mlp_in[1], mlp_out[1], mlp_saved1, 1, nl, None)
    dqc = _mm(gxb, p["conv_w_pw2"], tb=True, name="conv_dq")
    grads["conv_w_pw2"] = _mm(qc, gxb, ta=True, tm=128, tn=1024, tk=2048, out_dtypes=(BF16,), out=_grad_rows(Dq, D), name="conv_dwpw2")
    _, (db2,) = _rowwise_vjp(lambda t, b: (t + b,), [gx], [p["conv_b_pw2"]], [gx], [None], tr=TR, name="conv_db2")
    grads["conv_b_pw2"] = db2
    (dyc,), (dbdw, dcg, dcb) = _rowwise_vjp(_f_ln_silu, [yc], ln_par, [dqc], [F32], tr=TR, name="conv_ln_silu_bwd")
    grads["conv_b_dw"], grads["conv_ln_g"], grads["conv_ln_b"] = dbdw, dcg, dcb
    dyp = jnp.pad(dyc.reshape(B, S, D), ((0, 0), (0, CONV_HALO), (0, 0)))
    dzg, dwdw = _conv_bwd(zp, dyp, w_dw, R=256, tc=128, name="conv_dw_bwd")
    grads["conv_w_dw"] = dwdw[:CONV_WIDTH][None]
    (dz1,), (db1,) = _rowwise_vjp(_f_bias_glu, [z1], [p["conv_b_pw1"]], [dzg.reshape(T, D)], [BF16], tr=TR,
                                  name="conv_glu_bwd")
    grads["conv_b_pw1"] = db1
    grads["conv_w_pw1"] = _mm(h1, dz1, ta=True, tm=512, tn=512, tk=2048, out_dtypes=(BF16,), out=_grad_cols(D, 2 * Dq), name="conv_dwpw1")
    tok = grads_done({n: grads[n] for n in ("conv_w_pw1", "conv_w_pw2")})
    gx, gxb, d_norm_mix[1] = _mm_rms_bwd(dz1, p["conv_w_pw1"], x1, g1, gx, after=tok, name="conv_dh")

    gx, gxb, d_norm_mlp[0], mlp_lo = _mlp_bwd(
        gx, gxb, x_mid0, row(p["norm_mlp"][0]), mlp_in[0], mlp_out[0], mlp_saved0, 0, nl, mlp_lo)
    tok = grads_done({"mlp_w_in": (0, mlp_lo[0]), "mlp_w_out": (0, mlp_lo[1])})
    (dz0,), _ = _rowwise_vjp(_f_glu, [z0], [], [gx], [BF16], after=tok, tr=TR, name="s5_glu_bwd")
    grads["ssm_w_glu"] = _mm(yb, dz0, ta=True, tm=512, tn=512, tk=2048, out_dtypes=(BF16,), out=_grad_cols(D, 2 * Dq), name="s5_dwglu")
    tok = grads_done({"ssm_w_glu": grads["ssm_w_glu"]})
    dypre = _mm(dz0, w_glu, tb=True, epi=lambda acc, yp: (jax.vjp(lambda t: jax.nn.gelu(t), yp)[1](acc)[0],),
                extras=(ypre,), name="s5_dypre")
    dh0, dbre, dbim, dcre, dcim, dabr, dabi, dd = _s5_bwd(
        dypre, h0, sxr, sxi, ser, sei, abr, abi, bre_b, bim_b, cre_b, cim_b, d_skip, B=B, L=S5_CHUNK, name="s5_bwd")
    s5_grads = s5_vjp((dabr, dabi, dbre, dbim, dcre, dcim))
    for nm, gv in zip(("ssm_a_re", "ssm_a_im", "ssm_b_re", "ssm_b_im", "ssm_c_re", "ssm_c_im", "ssm_log_dt"), s5_grads):
        grads[nm] = gv[None]
    grads["ssm_d"] = dd
    gx, _, d_norm_mix[0] = _rms_bwd(x, g0, dh0, gx, "rms_mix_bwd_0", after=tok)

    grads["norm_mix"] = jnp.concatenate(d_norm_mix, axis=0)
    grads["norm_mlp"] = jnp.concatenate(d_norm_mlp, axis=0)
    grads["mlp_w_in"], grads["mlp_w_out"] = (mlp_lo[0], mlp_hi[0]), (mlp_lo[1], mlp_hi[1])
    return loss_part, gx.reshape(B, S, D), grads


WEIGHTS = ['norm_mix', 'norm_mlp', 'norm_final', 'ssm_a_re', 'ssm_a_im', 'ssm_b_re', 'ssm_b_im', 'ssm_c_re',
           'ssm_c_im', 'ssm_d', 'ssm_log_dt', 'ssm_w_glu', 'conv_w_pw1', 'conv_b_pw1', 'conv_w_dw', 'conv_b_dw',
           'conv_ln_g', 'conv_ln_b', 'conv_w_pw2', 'conv_b_pw2', 'gmlp_w_in', 'gmlp_ln_g', 'gmlp_ln_b', 'gmlp_w_s',
           'gmlp_b_s', 'gmlp_w_out', 'attn_w_qkv', 'attn_w_o', 'mlp_w_in', 'mlp_w_out']
BIG_AXIS = {'ssm_w_glu': -1, 'conv_w_pw1': -1, 'conv_w_pw2': -2, 'gmlp_w_in': -1, 'gmlp_w_out': -2,
            'attn_w_qkv': -1, 'attn_w_o': -1, 'mlp_w_in': -1, 'mlp_w_out': -2}
BIG = list(BIG_AXIS)
LAYER_MIXER_WEIGHTS = (('ssm_w_glu',), ('conv_w_pw1', 'conv_w_pw2'), ('gmlp_w_in', 'gmlp_w_out'), ('attn_w_qkv', 'attn_w_o'))
SMALL_SHARDED = ['conv_b_pw1', 'conv_w_dw', 'conv_b_dw', 'conv_ln_g', 'conv_ln_b', 'conv_b_pw2', 'gmlp_ln_g', 'gmlp_ln_b']
SMALL_REPL = [n for n in WEIGHTS if n not in BIG_AXIS and n not in SMALL_SHARDED]
SMALL = SMALL_REPL + SMALL_SHARDED
LANES = 128
FLAT_COLS = 1024


def _pack(arrs, cols, row_mult):
    flat = jnp.concatenate([a.reshape(-1) for a in arrs])
    per = cols * row_mult
    n = -(-flat.shape[0] // per) * per
    return jnp.pad(flat, (0, n - flat.shape[0])).reshape(n // cols, cols)


def _unpack(flat2d, shapes):
    flat = flat2d.reshape(-1)
    out, off = [], 0
    for s in shapes:
        n = int(np.prod(s))
        out.append(flat[off:off + n].reshape(s))
        off += n
    return out


def _as_halves(shard):
    if shard.shape[0] == 1:
        shard = shard[0]
    return shard.reshape((2, shard.shape[0] // 2) + shard.shape[1:])


def _stored_weight(name, arr):
    kind = "cols" if BIG_AXIS[name] == -1 else "rows"
    if arr.shape[1] > 1:
        return [_Stored(arr, kind, lead=(li,)) for li in range(arr.shape[1])]
    arr = arr[:, 0]
    if kind == "rows":
        return arr.reshape(-1, arr.shape[-1])
    return _Stored(arr, kind)


def kernel(x, norm_mix, norm_mlp, norm_final, ssm_a_re, ssm_a_im, ssm_b_re, ssm_b_im, ssm_c_re, ssm_c_im, ssm_d, ssm_log_dt, ssm_w_glu, conv_w_pw1, conv_b_pw1, conv_w_dw, conv_b_dw, conv_ln_g, conv_ln_b, conv_w_pw2, conv_b_pw2, gmlp_w_in, gmlp_ln_g, gmlp_ln_b, gmlp_w_s, gmlp_b_s, gmlp_w_out, attn_w_qkv, attn_w_o, mlp_w_in, mlp_w_out, loss_target, m_norm_mix, m_norm_mlp, m_norm_final, m_ssm_a_re, m_ssm_a_im, m_ssm_b_re, m_ssm_b_im, m_ssm_c_re, m_ssm_c_im, m_ssm_d, m_ssm_log_dt, m_ssm_w_glu, m_conv_w_pw1, m_conv_b_pw1, m_conv_w_dw, m_conv_b_dw, m_conv_ln_g, m_conv_ln_b, m_conv_w_pw2, m_conv_b_pw2, m_gmlp_w_in, m_gmlp_ln_g, m_gmlp_ln_b, m_gmlp_w_s, m_gmlp_b_s, m_gmlp_w_out, m_attn_w_qkv, m_attn_w_o, m_mlp_w_in, m_mlp_w_out, v_norm_mix, v_norm_mlp, v_norm_final, v_ssm_a_re, v_ssm_a_im, v_ssm_b_re, v_ssm_b_im, v_ssm_c_re, v_ssm_c_im, v_ssm_d, v_ssm_log_dt, v_ssm_w_glu, v_conv_w_pw1, v_conv_b_pw1, v_conv_w_dw, v_conv_b_dw, v_conv_ln_g, v_conv_ln_b, v_conv_w_pw2, v_conv_b_pw2, v_gmlp_w_in, v_gmlp_ln_g, v_gmlp_ln_b, v_gmlp_w_s, v_gmlp_b_s, v_gmlp_w_out, v_attn_w_qkv, v_attn_w_o, v_mlp_w_in, v_mlp_w_out):
    args = dict(locals())
    w = {n: args[n] for n in WEIGHTS}
    m = {n: args["m_" + n] for n in WEIGHTS}
    v = {n: args["v_" + n] for n in WEIGHTS}
    chip = 2 * lax.axis_index("x") + lax.axis_index("y")
    core = lax.axis_index("c")

    big_shapes = [w[n].shape for n in BIG]
    started, token = [], None
    for li, mixer in enumerate(LAYER_MIXER_WEIGHTS):
        names = list(mixer) + ["mlp_w_in", "mlp_w_out"]
        shards = [w[n][0] for n in mixer] + [w["mlp_w_in"][li], w["mlp_w_out"][li]]
        halves = [s.astype(BF16).reshape((2, s.shape[0] // 2) + s.shape[1:]) for s in shards]
        send_sems, recv_sems, halves, lands, token = _gather_start(halves, token, name=f"gather_start_{li}")
        started.append((names, (send_sems, recv_sems, halves, lands)))

    def layer_weights(li, after):
        names, st = started[li]
        halves, lands = _gather_wait(st, after, name=f"gather_wait_{li}")
        out = {}
        for n, h, arr in zip(names, halves, lands):
            arr = lax.dynamic_update_index_in_dim(arr, h, chip, axis=0)
            arr = arr.reshape((N_CHIPS, arr.shape[1] * arr.shape[2]) + arr.shape[3:])
            if BIG_AXIS[n] == -1:
                out[n] = _Stored(arr, "cols")
                if n in ("attn_w_qkv", "attn_w_o"):
                    out[n + "_plain"] = jnp.swapaxes(arr, 0, 1).reshape(arr.shape[1], -1)
            else:
                out[n] = arr.reshape(-1, arr.shape[-1])
        return out

    p = {}
    sm_shapes = [w[n].shape for n in SMALL_SHARDED]
    sflat = _pack([w[n] for n in SMALL_SHARDED], LANES, 8)
    rs = sflat.shape[0]
    sall = _allgather8(sflat, name="gather_small").reshape(8, rs, LANES)
    per_chip = [_unpack(sall[2 * k], sm_shapes) for k in range(N_CHIPS)]
    for i, n in enumerate(SMALL_SHARDED):
        p[n] = jnp.concatenate([per_chip[k][i] for k in range(N_CHIPS)], axis=-1)
    for n in SMALL_REPL:
        p[n] = w[n]
    p['conv_w_dw'] = p['conv_w_dw'][0]

    in_flight, arrived, n_rounds = [], {}, [0]

    def finish_round(after):
        k, names, plans, st = in_flight.pop(0)
        srcs, lands = _reduce_wait(st, plans, after, name=f"grads_wait_{k}")
        for n, plan, src, land in zip(names, plans, srcs, lands):
            arrived.setdefault(n, []).append((plan, src, land))

    def grads_done(group):
        names, srcs, plans, lands = [], [], [], []
        for n, v in group.items():
            if isinstance(v, tuple):
                src, plan = v[1].reshape(1, N_CHIPS, -1, v[1].shape[-1]), ((0, v[0]),)
                while any(n in rd[1] for rd in in_flight):
                    finish_round(src)
            else:
                src, plan = v.reshape(2, N_CHIPS, -1, v.shape[-1]), ((0, 0), (1, 1))
            land = arrived[n][-1][2] if n in arrived else lax.empty((2 * N_CHIPS,) + src.shape[2:], BF16)
            names.append(n), srcs.append(src), plans.append(plan), lands.append(land)
        st = _reduce_start(srcs, lands, plans, None, name=f"grads_start_{n_rounds[0]}")
        in_flight.append((n_rounds[0], names, plans, st[:4]))
        n_rounds[0] += 1
        return st[4]

    loss_part, grad_x, g = _local_step(x, loss_target, p, layer_weights, token, grads_done)
    loss = lax.psum(loss_part[0, 0], ("x", "y", "c"))

    my_id = 2 * chip + core
    while in_flight:
        finish_round(grad_x)
    totals = []
    for n in BIG:
        own = None
        for plan, src, land in arrived[n]:
            slab = lax.dynamic_index_in_dim(src, chip, axis=1, keepdims=False)
            if len(plan) == 2:
                own = lax.dynamic_index_in_dim(slab, core, axis=0, keepdims=False)
            else:
                own = slab[0] if own is None else jnp.where(core == plan[0][1], slab[0], own)
        totals.append(_sum8(arrived[n][-1][2], own, my_id, name="owner_sum_" + n))
    shared = _share_halves(totals, name="grads_share_halves")
    big_grads = {}
    for n, arr, t in zip(BIG, shared, totals):
        arr = lax.dynamic_update_index_in_dim(arr, t[None], core, axis=0)
        big_grads[n] = arr.reshape(w[n].shape)

    small_full_shapes = [g[n].shape for n in SMALL]
    gs = _pack([g[n] for n in SMALL], LANES, 8)
    rg = gs.shape[0]
    gs_all = _allgather8(gs, name="gather_small_grads").reshape(8, rg, LANES)
    gs_sum = _rowwise(lambda *a: (functools.reduce(lambda s, t: s + t, a),), [gs_all[k] for k in range(8)], [], (F32,),
                      tr=rg, name="small_grads_sum")[0]
    small_grads = dict(zip(SMALL, _unpack(gs_sum, small_full_shapes)))
    for n in SMALL:
        small_grads[n] = small_grads[n].reshape(p_shape_full(w[n], -1 if n in SMALL_SHARDED else None))
    for n in SMALL_SHARDED:
        width = w[n].shape[-1]
        small_grads[n] = lax.dynamic_slice_in_dim(small_grads[n], chip * width, width, axis=-1)

    grad, delta, new_m, new_v = {}, {}, {}, {}
    for n in BIG:
        shape = w[n].shape
        two_d = lambda t: t.reshape(-1, shape[-1])
        grad[n] = big_grads[n]
        d_, m_, v_ = _adamw(two_d(w[n]), two_d(grad[n]), two_d(m[n]), two_d(v[n]), name="adamw_" + n)
        delta[n], new_m[n], new_v[n] = d_.reshape(shape), m_.reshape(shape), v_.reshape(shape)
    for n in SMALL:
        shape = w[n].shape
        two_d = lambda t: t.reshape(-1, shape[-1])
        grad[n] = small_grads[n]
        d_, m_, v_ = _adamw(two_d(w[n]), two_d(grad[n]), two_d(m[n]), two_d(v[n]), name="adamw_" + n)
        delta[n], new_m[n], new_v[n] = d_.reshape(shape), m_.reshape(shape), v_.reshape(shape)

    return (loss, grad_x, *[grad[n] for n in WEIGHTS], *[delta[n] for n in WEIGHTS],
            *[new_m[n] for n in WEIGHTS], *[new_v[n] for n in WEIGHTS])


def p_shape_full(shard, axis):
    s = list(shard.shape)
    if axis is not None:
        s[axis] *= N_CHIPS
    return tuple(s)
```

```python
import functools
import math

import jax
import jax.numpy as jnp
import numpy as np
from jax import lax
from jax.experimental import pallas as pl
from jax.experimental.pallas import tpu as pltpu

F32 = jnp.float32
BF16 = jnp.bfloat16
MESH = pl.DeviceIdType.MESH

EPS = 1e-6
SSM_GROUP = 16
SSM_STATE = 64
CONV_WIDTH = 31
CONV_HALO = 32
GMLP_CHUNK = 128
GMLP_HEADS = 4
ATT_DILS = (1, 4, 16)
ATT_BLK = 128
ATT_HEADS = 8
HEAD_DIM = 64
ATT_W = ATT_HEADS * HEAD_DIM
N_CHIPS = 4
ADAM_LR, ADAM_B1, ADAM_B2, ADAM_EPS, ADAM_WD, ADAM_STEP = 1e-3, 0.9, 0.999, 1e-8, 0.01, 10

VMEM_BYTES_V7X = 64 * 1024 * 1024
VMEM_LIMIT = VMEM_BYTES_V7X - 8 * 1024 * 1024
MASK_VALUE = -1e30
LANE_TILE = 128


def _cp(sem=None):
    return pltpu.CompilerParams(dimension_semantics=sem, vmem_limit_bytes=VMEM_LIMIT)


def _pick_tile(total, target):
    for cand in range(min(target, total) // LANE_TILE * LANE_TILE, 0, -LANE_TILE):
        if total % cand == 0:
            return cand
    return total


class _Stored:
    def __init__(self, arr, kind="plain", lead=()):
        self.arr, self.kind, self.lead = arr, kind, tuple(lead)
        r, c = arr.shape[-2:]
        self.shape = (r, c * N_CHIPS) if kind == "cols" else (r * N_CHIPS, c) if kind == "rows" else (r, c)

    def spec(self, br, bc, rc_of):
        lead, nl = self.lead, len(self.lead)
        if self.kind == "plain":
            return pl.BlockSpec((None,) * nl + (br, bc), lambda i, j, k: (*lead, *rc_of(i, j, k)))
        if self.kind == "cols":
            per = self.arr.shape[-1] // bc
            assert per * bc == self.arr.shape[-1]

            def imap(i, j, k):
                r, c = rc_of(i, j, k)
                return (c // per, *lead, r, c % per)
        else:
            per = self.arr.shape[-2] // br
            assert per * br == self.arr.shape[-2]

            def imap(i, j, k):
                r, c = rc_of(i, j, k)
                return (r // per, *lead, r % per, c)
        return pl.BlockSpec((None,) * (nl + 1) + (br, bc), imap)


class _ColBlocks:
    kind = "colblocks"

    def __init__(self, arr, first, stride, count, width):
        self.arr, self.first, self.stride, self.width = arr, first, stride, width
        self.shape = (arr.shape[0], count * width)

    def spec(self, br, bc, rc_of):
        per = self.width // bc
        assert per * bc == self.width

        def imap(i, j, k):
            r, c = rc_of(i, j, k)
            return (r, (self.first + (c // per) * self.stride) * per + c % per)
        return pl.BlockSpec((br, bc), imap)


def _mm(a, b, *, ta=False, tb=False, out_dtypes=(F32,), tm=1024, tn=1024, tk=1024,
        pro_a=None, pro_b=None, epi=None, extras=(), n_row_sums=0, out=None, after=None, name):
    if ta:
        K, M = a.shape
    else:
        M, K = a.shape
    if not isinstance(b, (_Stored, _ColBlocks)):
        b = _Stored(b)
    N, Kb = b.shape if tb else b.shape[::-1]
    assert K == Kb, (a.shape, b.shape, ta, tb)
    col_unit = b.width if b.kind == "colblocks" else b.arr.shape[-1] if b.kind == "cols" else b.shape[1]
    row_unit = b.arr.shape[-2] if b.kind == "rows" else b.shape[0]
    n_unit, k_unit = (row_unit, col_unit) if tb else (col_unit, row_unit)
    m_unit = M
    if out is not None:
        m_unit, n_unit = out[4], math.gcd(n_unit, out[5])
    tm, tn, tk = _pick_tile(m_unit, tm), _pick_tile(n_unit, tn), _pick_tile(k_unit, tk)
    assert not n_row_sums or tn == N
    nk = K // tk
    a_spec = (pl.BlockSpec((tk, tm), lambda i, j, k: (k, i)) if ta
              else pl.BlockSpec((tm, tk), lambda i, j, k: (i, k)))
    b_spec = b.spec(tn, tk, lambda i, j, k: (j, k)) if tb else b.spec(tk, tn, lambda i, j, k: (k, j))
    ex_specs = []
    for e in extras:
        if e.shape[0] == 1:
            ex_specs.append(pl.BlockSpec((1, tn), lambda i, j, k: (0, j)))
        else:
            assert e.shape == (M, N), (e.shape, M, N)
            ex_specs.append(pl.BlockSpec((tm, tn), lambda i, j, k: (i, j)))
    dims = (((0 if ta else 1,), (1 if tb else 0,)), ((), ()))
    n_ex, n_out = len(extras), len(out_dtypes)
    direct = epi is None and n_out == 1 and out_dtypes[0] == F32
    use_acc = nk > 1 and not direct
    operands, aliases, alias_specs = [a, b.arr, *extras], {}, []
    if after is not None:
        operands.append(after)
        alias_specs.append(pl.BlockSpec(memory_space=pl.ANY))
    if out is None:
        n_tile_out = n_out - n_row_sums
        out_specs = ([pl.BlockSpec((tm, tn), lambda i, j, k: (i, j))] * n_tile_out
                     + [pl.BlockSpec((1, tn), lambda i, j, k: (0, j))] * n_row_sums)
        out_shape = ([jax.ShapeDtypeStruct((M, N), dt) for dt in out_dtypes[:n_tile_out]]
                     + [jax.ShapeDtypeStruct((1, N), dt) for dt in out_dtypes[n_tile_out:]])
    else:
        shape, block_fn, imap_fn, alias = out[:4]
        assert n_out == 1
        out_specs = [pl.BlockSpec(block_fn(tm, tn), imap_fn(tm, tn))]
        out_shape = [jax.ShapeDtypeStruct(shape, out_dtypes[0])]
        if alias is not None:
            operands.append(alias)
            aliases = {len(operands) - 1: 0}
            alias_specs.append(pl.BlockSpec(memory_space=pl.ANY))
    n_in = len(operands)

    def finish(r, ex, outs, first_row_tile):
        res = epi(r, *[e[...] for e in ex]) if epi is not None else (r,)
        n_tile_out = n_out - n_row_sums
        for o, v in zip(outs[:n_tile_out], res):
            o[...] = v.astype(o.dtype)
        for o, v in zip(outs[n_tile_out:], res[n_tile_out:]):
            @pl.when(first_row_tile)
            def _(o=o):
                o[...] = jnp.zeros_like(o)
            o[...] += v

    def body(*refs):
        a_ref, b_ref = refs[:2]
        ex = refs[2:2 + n_ex]
        outs = refs[n_in:n_in + n_out]
        first_row_tile = pl.program_id(0) == 0
        at, bt = a_ref[...], b_ref[...]
        if pro_a is not None:
            at = pro_a(at)
        if pro_b is not None:
            bt = pro_b(bt)
        part = lax.dot_general(at, bt, dims, preferred_element_type=F32)
        if nk == 1:
            finish(part, ex, outs, first_row_tile)
            return
        acc = refs[-1] if use_acc else outs[0]
        k = pl.program_id(2)

        @pl.when(k == 0)
        def _():
            acc[...] = part

        @pl.when(k > 0)
        def _():
            acc[...] += part

        if use_acc:
            @pl.when(k == nk - 1)
            def _():
                finish(acc[...], ex, outs, first_row_tile)

    res = pl.pallas_call(
        body, grid=(M // tm, N // tn, nk),
        in_specs=[a_spec, b_spec] + ex_specs + alias_specs,
        out_specs=out_specs, out_shape=out_shape,
        scratch_shapes=[pltpu.VMEM((tm, tn), F32)] if use_acc else [],
        input_output_aliases=aliases,
        compiler_params=_cp(("arbitrary" if n_row_sums else "parallel", "parallel", "arbitrary")), name=name,
    )(*operands)
    return res[0] if n_out == 1 else res


def _to_bf16(t):
    return t.astype(BF16)


def _pick_rows(total, target):
    for cand in range(min(target, total) // 8 * 8, 0, -8):
        if total % cand == 0:
            return cand
    return total


def _rowwise(f, rows, params, out_dtypes, *, tr, name):
    T = rows[0].shape[0]
    tr = _pick_rows(T, tr)
    nr, npar = len(rows), len(params)
    blk = [jax.ShapeDtypeStruct((tr, r.shape[1]), F32) for r in rows]
    blk += [jax.ShapeDtypeStruct(p.shape, F32) for p in params]
    out_avals = jax.eval_shape(f, *blk)

    def body(*refs):
        res = f(*[r[...].astype(F32) for r in refs[:nr + npar]])
        for o, v in zip(refs[nr + npar:], res):
            o[...] = v.astype(o.dtype)

    out = pl.pallas_call(
        body, grid=(T // tr,),
        in_specs=[pl.BlockSpec((tr, r.shape[1]), lambda i: (i, 0)) for r in rows]
        + [pl.BlockSpec(p.shape, lambda i, nd=p.ndim: (0,) * nd) for p in params],
        out_specs=[pl.BlockSpec((tr, o.shape[1]), lambda i: (i, 0)) for o in out_avals],
        out_shape=[jax.ShapeDtypeStruct((T, o.shape[1]), dt) for o, dt in zip(out_avals, out_dtypes)],
        compiler_params=_cp(("parallel",)), name=name,
    )(*rows, *params)
    return out


def _rowwise_vjp(f, rows, params, cots, drow_dtypes, *, adds=None, after=None, tr, name):
    adds = adds or {}
    T = rows[0].shape[0]
    tr = _pick_rows(T, tr)
    nr, npar, nc = len(rows), len(params), len(cots)
    want, want_dt = [], []
    for i, dt in enumerate(drow_dtypes):
        for one in (dt if isinstance(dt, tuple) else (dt,)):
            if one is not None:
                want.append(i)
                want_dt.append(one)
    add_idx = sorted(set(i for i in want if i in adds))
    add_arrays = [adds[i] for i in add_idx]
    na = len(add_arrays)
    extra = [] if after is None else [after]

    def body(*refs):
        ins = [r[...].astype(F32) for r in refs[:nr + npar]]
        cvals = [r[...].astype(F32) for r in refs[nr + npar:nr + npar + nc]]
        avals = refs[nr + npar + nc:nr + npar + nc + na]
        outs = refs[nr + npar + nc + na + len(extra):]
        _, vjp = jax.vjp(f, *ins)
        grads = vjp(tuple(cvals))
        for o, i in zip(outs[:len(want)], want):
            g = grads[i]
            if i in adds:
                g = g + avals[add_idx.index(i)][...].astype(F32)
            o[...] = g.astype(o.dtype)
        step = pl.program_id(0)
        for o, g in zip(outs[len(want):], grads[nr:]):
            @pl.when(step == 0)
            def _(o=o):
                o[...] = jnp.zeros_like(o)
            o[...] += g

    rspec = lambda r: pl.BlockSpec((tr, r.shape[1]), lambda i: (i, 0))
    pspec = lambda p: pl.BlockSpec(p.shape, lambda i, nd=p.ndim: (0,) * nd)
    out = pl.pallas_call(
        body, grid=(T // tr,),
        in_specs=[rspec(r) for r in rows] + [pspec(p) for p in params] + [rspec(c) for c in cots]
        + [rspec(a) for a in add_arrays] + [pl.BlockSpec(memory_space=pl.ANY)] * len(extra),
        out_specs=[rspec(rows[i]) for i in want] + [pspec(p) for p in params],
        out_shape=[jax.ShapeDtypeStruct(rows[i].shape, dt) for i, dt in zip(want, want_dt)]
        + [jax.ShapeDtypeStruct(p.shape, F32) for p in params],
        compiler_params=_cp(("arbitrary",)), name=name,
    )(*rows, *params, *cots, *add_arrays, *extra)
    return out[:len(want)], out[len(want):]


def _f_rms(x, g):
    return (x * lax.rsqrt(jnp.mean(x * x, axis=-1, keepdims=True) + EPS) * g,)


def _ln(x, g, b):
    mu = jnp.mean(x, axis=-1, keepdims=True)
    var = jnp.mean(jnp.square(x - mu), axis=-1, keepdims=True)
    return (x - mu) * lax.rsqrt(var + EPS) * g + b


def _f_glu(z):
    d = z.shape[1] // 2
    return (z[:, :d] * jax.nn.sigmoid(z[:, d:]),)


def _f_bias_glu(z, b):
    return _f_glu(z + b)


def _f_ln_silu(y, b_dw, g, b):
    return (jax.nn.silu(_ln(y + b_dw, g, b)),)


def _f_gelu_ln(z, g, b):
    d = z.shape[1] // 2
    zz = jax.nn.gelu(z)
    return zz[:, :d], _ln(zz[:, d:], g, b)


def _f_gelu(y):
    return (jax.nn.gelu(y),)


def _f_merge(o0, o1, o2, l0, l1, l2):
    m = jnp.maximum(jnp.maximum(l0, l1), l2)
    e0, e1, e2 = jnp.exp(l0 - m), jnp.exp(l1 - m), jnp.exp(l2 - m)
    s = e0 + e1 + e2
    pair = 2 * HEAD_DIM
    first_head = lax.broadcasted_iota(jnp.int32, (o0.shape[0], pair), 1) < HEAD_DIM
    cols = []
    for hp in range(o0.shape[1] // pair):
        acc = None
        for o, e in ((o0, e0), (o1, e1), (o2, e2)):
            wgt = e / s
            wp = jnp.where(first_head, wgt[:, 2 * hp:2 * hp + 1], wgt[:, 2 * hp + 1:2 * hp + 2])
            term = wp * o[:, hp * pair:(hp + 1) * pair]
            acc = term if acc is None else acc + term
        cols.append(acc)
    return (jnp.concatenate(cols, axis=1),)


def _f_add(a, b):
    return (a + b,)


def _loss_head(x, tgt, g, *, tr, name):
    T, D = x.shape
    tr = min(tr, T)

    def f(xv, gv, tv):
        y = _f_rms(xv, gv)[0]
        return 0.5 * jnp.mean(jnp.square(y - tv), axis=-1, keepdims=True)

    def body(x_ref, t_ref, g_ref, loss_ref, dx_ref, dxb_ref, dg_ref):
        tv = t_ref[...]
        l, vjp = jax.vjp(lambda xv, gv: f(xv, gv, tv), x_ref[...], g_ref[...])
        dx, dg = vjp(jnp.ones_like(l))
        dx_ref[...] = dx
        dxb_ref[...] = dx.astype(BF16)

        @pl.when(pl.program_id(0) == 0)
        def _():
            loss_ref[...] = jnp.zeros_like(loss_ref)
            dg_ref[...] = jnp.zeros_like(dg_ref)

        loss_ref[...] += jnp.sum(l)
        dg_ref[...] += dg

    return pl.pallas_call(
        body, grid=(T // tr,),
        in_specs=[pl.BlockSpec((tr, D), lambda i: (i, 0)), pl.BlockSpec((tr, D), lambda i: (i, 0)),
                  pl.BlockSpec((1, D), lambda i: (0, 0))],
        out_specs=[pl.BlockSpec((1, 128), lambda i: (0, 0)), pl.BlockSpec((tr, D), lambda i: (i, 0)),
                   pl.BlockSpec((tr, D), lambda i: (i, 0)), pl.BlockSpec((1, D), lambda i: (0, 0))],
        out_shape=[jax.ShapeDtypeStruct((1, 128), F32), jax.ShapeDtypeStruct((T, D), F32),
                   jax.ShapeDtypeStruct((T, D), BF16), jax.ShapeDtypeStruct((1, D), F32)],
        compiler_params=_cp(("arbitrary",)), name=name,
    )(x, tgt, g)


def _adamw(w, g, m, v, *, name):
    R, C = w.shape
    tr = _pick_rows(R, max(8, 2 * 1024 * 1024 // (4 * C)))
    c1 = 1.0 - ADAM_B1 ** ADAM_STEP
    c2 = 1.0 - ADAM_B2 ** ADAM_STEP

    def body(w_ref, g_ref, m_ref, v_ref, d_ref, nm_ref, nv_ref):
        gv = g_ref[...]
        nm = ADAM_B1 * m_ref[...] + (1.0 - ADAM_B1) * gv
        nv = ADAM_B2 * v_ref[...] + (1.0 - ADAM_B2) * jnp.square(gv)
        nm_ref[...] = nm
        nv_ref[...] = nv
        d_ref[...] = -ADAM_LR * ((nm / c1) / (jnp.sqrt(nv / c2) + ADAM_EPS) + ADAM_WD * w_ref[...])

    spec = pl.BlockSpec((tr, C), lambda i: (i, 0))
    return pl.pallas_call(
        body, grid=(R // tr,), in_specs=[spec] * 4, out_specs=[spec] * 3,
        out_shape=[jax.ShapeDtypeStruct((R, C), F32)] * 3,
        compiler_params=_cp(("parallel",)), name=name,
    )(w, g, m, v)


def _s5_prep(a_re, a_im, b_re, b_im, c_re, c_im, log_dt):
    G, N = a_re.shape
    P = b_re.shape[2]
    gpb = 128 // P
    nblk = G // gpb
    dt = jnp.exp(log_dt)[:, None]
    mag = jnp.exp(a_re * dt)
    abr, abi = mag * jnp.cos(a_im * dt), mag * jnp.sin(a_im * dt)
    den = a_re * a_re + a_im * a_im
    nr, ni = abr - 1.0, abi
    qr, qi = (nr * a_re + ni * a_im) / den, (ni * a_re - nr * a_im) / den
    bbr = qr[..., None] * b_re - qi[..., None] * b_im
    bbi = qr[..., None] * b_im + qi[..., None] * b_re
    eye = jnp.eye(gpb, dtype=F32)

    def expand_b(t):
        t = t.reshape(nblk, gpb, N, P).transpose(0, 1, 3, 2)
        return (t[:, :, :, None, :] * eye[None, :, None, :, None]).reshape(nblk, gpb * P, gpb * N)

    def expand_c(t):
        t = t.reshape(nblk, gpb, P, N).transpose(0, 1, 3, 2)
        return (t[:, :, :, None, :] * eye[None, :, None, :, None]).reshape(nblk, gpb * N, gpb * P)

    return (abr.reshape(1, G * N), abi.reshape(1, G * N), expand_b(bbr), expand_b(bbi),
            expand_c(c_re), expand_c(c_im))


def _s5_fwd(h, abr, abi, bre, bim, cre, cim, d, *, B, L, name):
    T, D = h.shape
    S = T // B
    L = min(L, S)
    nc = S // L
    nblk, cb, sb = bre.shape
    GN = abr.shape[1]

    def body(h_ref, ar_ref, ai_ref, bre_ref, bim_ref, cre_ref, cim_ref, d_ref,
             y_ref, yb_ref, xr_ref, xi_ref, er_ref, ei_ref, sr, si, car, cai):
        ci = pl.program_id(1)

        @pl.when(ci == 0)
        def _():
            car[...] = jnp.zeros_like(car)
            cai[...] = jnp.zeros_like(cai)

        for j in range(nblk):
            u = h_ref[:, j * cb:(j + 1) * cb]
            sr[:, j * sb:(j + 1) * sb] = jnp.dot(u, bre_ref[j], preferred_element_type=F32)
            si[:, j * sb:(j + 1) * sb] = jnp.dot(u, bim_ref[j], preferred_element_type=F32)
        ar, ai = ar_ref[...], ai_ref[...]

        def step(t, carry):
            pr, pi = carry
            nr = ar * pr - ai * pi + sr[pl.ds(t, 1), :]
            ni = ar * pi + ai * pr + si[pl.ds(t, 1), :]
            sr[pl.ds(t, 1), :] = nr
            si[pl.ds(t, 1), :] = ni
            return nr, ni

        pr, pi = lax.fori_loop(0, L, step, (car[...], cai[...]), unroll=4)
        car[...] = pr
        cai[...] = pi
        er_ref[0] = pr
        ei_ref[0] = pi
        for j in range(nblk):
            xr = sr[:, j * sb:(j + 1) * sb].astype(BF16)
            xi = si[:, j * sb:(j + 1) * sb].astype(BF16)
            xr_ref[:, j * sb:(j + 1) * sb] = xr
            xi_ref[:, j * sb:(j + 1) * sb] = xi
            y = (jnp.dot(xr, cre_ref[j], preferred_element_type=F32)
                 - jnp.dot(xi, cim_ref[j], preferred_element_type=F32))
            u = h_ref[:, j * cb:(j + 1) * cb].astype(F32)
            y = y + d_ref[:, j * cb:(j + 1) * cb] * u
            y_ref[:, j * cb:(j + 1) * cb] = y
            yb_ref[:, j * cb:(j + 1) * cb] = jax.nn.gelu(y).astype(BF16)

    tok = lambda w: pl.BlockSpec((L, w), lambda b, c: (b * nc + c, 0))
    whole = lambda p: pl.BlockSpec(p.shape, lambda b, c, nd=p.ndim: (0,) * nd)
    end = pl.BlockSpec((1, 1, GN), lambda b, c: (b * nc + c, 0, 0))
    return pl.pallas_call(
        body, grid=(B, nc),
        in_specs=[tok(D)] + [whole(p) for p in (abr, abi, bre, bim, cre, cim, d)],
        out_specs=[tok(D), tok(D), tok(GN), tok(GN), end, end],
        out_shape=[jax.ShapeDtypeStruct((T, D), F32), jax.ShapeDtypeStruct((T, D), BF16),
                   jax.ShapeDtypeStruct((T, GN), BF16),
                   jax.ShapeDtypeStruct((T, GN), BF16), jax.ShapeDtypeStruct((B * nc, 1, GN), F32),
                   jax.ShapeDtypeStruct((B * nc, 1, GN), F32)],
        scratch_shapes=[pltpu.VMEM((L, GN), F32), pltpu.VMEM((L, GN), F32),
                        pltpu.VMEM((1, GN), F32), pltpu.VMEM((1, GN), F32)],
        compiler_params=_cp(("arbitrary", "arbitrary")), name=name,
    )(h, abr, abi, bre, bim, cre, cim, d)


def _s5_bwd(dy, h, xr, xi, er, ei, abr, abi, bre, bim, cre, cim, d, *, B, L, name):
    T, D = h.shape
    S = T // B
    L = min(L, S)
    nc = S // L
    nblk, cb, sb = bre.shape
    GN = abr.shape[1]
    dims_nt = (((1,), (1,)), ((), ()))
    dims_tn = (((0,), (0,)), ((), ()))

    def body(dy_ref, h_ref, xr_ref, xi_ref, er_ref, ei_ref, ar_ref, ai_ref, bre_ref, bim_ref,
             cre_ref, cim_ref, d_ref,
             dh_ref, dbre_ref, dbim_ref, dcre_ref, dcim_ref, dar_ref, dai_ref, dd_ref,
             lr, li, car, cai):
        b, cstep = pl.program_id(0), pl.program_id(1)
        ci = nc - 1 - cstep

        @pl.when((b == 0) & (cstep == 0))
        def _():
            for r in (dbre_ref, dbim_ref, dcre_ref, dcim_ref, dar_ref, dai_ref, dd_ref):
                r[...] = jnp.zeros_like(r)

        @pl.when(cstep == 0)
        def _():
            car[...] = jnp.zeros_like(car)
            cai[...] = jnp.zeros_like(cai)

        for j in range(nblk):
            dyj = dy_ref[:, j * cb:(j + 1) * cb].astype(BF16)
            lr[:, j * sb:(j + 1) * sb] = lax.dot_general(dyj, cre_ref[j], dims_nt, preferred_element_type=F32)
            li[:, j * sb:(j + 1) * sb] = -lax.dot_general(dyj, cim_ref[j], dims_nt, preferred_element_type=F32)
        ar, ai = ar_ref[...], ai_ref[...]

        def step(s, carry):
            t = L - 1 - s
            pr, pi = carry
            nr = lr[pl.ds(t, 1), :] + ar * pr + ai * pi
            ni = li[pl.ds(t, 1), :] - ai * pr + ar * pi
            lr[pl.ds(t, 1), :] = nr
            li[pl.ds(t, 1), :] = ni
            return nr, ni

        pr, pi = lax.fori_loop(0, L, step, (car[...], cai[...]), unroll=4)
        car[...] = pr
        cai[...] = pi
        has_prev = (ci > 0).astype(F32)
        first_row = lax.broadcasted_iota(jnp.int32, (L, sb), 0) == 0
        for j in range(nblk):
            cs = slice(j * cb, (j + 1) * cb)
            ss = slice(j * sb, (j + 1) * sb)
            lrj, lij = lr[:, ss], li[:, ss]
            xrj, xij = xr_ref[:, ss], xi_ref[:, ss]
            pr_j = jnp.where(first_row, er_ref[0][:, ss] * has_prev, pltpu.roll(xrj.astype(F32), 1, 0))
            pi_j = jnp.where(first_row, ei_ref[0][:, ss] * has_prev, pltpu.roll(xij.astype(F32), 1, 0))
            dar_ref[:, ss] += jnp.sum(lrj * pr_j + lij * pi_j, axis=0, keepdims=True)
            dai_ref[:, ss] += jnp.sum(lij * pr_j - lrj * pi_j, axis=0, keepdims=True)
            lrb, lib = lrj.astype(BF16), lij.astype(BF16)
            hj = h_ref[:, cs]
            dyf = dy_ref[:, cs]
            dyj = dyf.astype(BF16)
            dbre_ref[j] += lax.dot_general(hj, lrb, dims_tn, preferred_element_type=F32)
            dbim_ref[j] += lax.dot_general(hj, lib, dims_tn, preferred_element_type=F32)
            dcre_ref[j] += lax.dot_general(xrj, dyj, dims_tn, preferred_element_type=F32)
            dcim_ref[j] -= lax.dot_general(xij, dyj, dims_tn, preferred_element_type=F32)
            du = (lax.dot_general(lrb, bre_ref[j], dims_nt, preferred_element_type=F32)
                  + lax.dot_general(lib, bim_ref[j], dims_nt, preferred_element_type=F32))
            dh_ref[:, cs] = du + d_ref[:, cs] * dyf
            dd_ref[:, cs] += jnp.sum(dyf * hj.astype(F32), axis=0, keepdims=True)

    tok = lambda w: pl.BlockSpec((L, w), lambda b, c: (b * nc + nc - 1 - c, 0))
    whole = lambda p: pl.BlockSpec(p.shape, lambda b, c, nd=p.ndim: (0,) * nd)
    prev_end = pl.BlockSpec((1, 1, GN), lambda b, c: (b * nc + jnp.maximum(nc - 2 - c, 0), 0, 0))
    params = (abr, abi, bre, bim, cre, cim, d)
    acc_shapes = [bre.shape, bim.shape, cre.shape, cim.shape, abr.shape, abi.shape, d.shape]
    out = pl.pallas_call(
        body, grid=(B, nc),
        in_specs=[tok(D), tok(D), tok(GN), tok(GN), prev_end, prev_end] + [whole(p) for p in params],
        out_specs=[tok(D)] + [pl.BlockSpec(s, lambda b, c, nd=len(s): (0,) * nd) for s in acc_shapes],
        out_shape=[jax.ShapeDtypeStruct((T, D), F32)] + [jax.ShapeDtypeStruct(s, F32) for s in acc_shapes],
        scratch_shapes=[pltpu.VMEM((L, GN), F32), pltpu.VMEM((L, GN), F32),
                        pltpu.VMEM((1, GN), F32), pltpu.VMEM((1, GN), F32)],
        compiler_params=_cp(("arbitrary", "arbitrary")), name=name,
    )(dy, h, xr, xi, er, ei, *params)
    return out


def _conv_fwd(zp, w, *, R, tc, name):
    B, SP, C = zp.shape
    S = SP - CONV_HALO
    R, tc = min(R, S), min(tc, C)

    def body(z_ref, w_ref, y_ref):
        def chunk(ci, _):
            start = pl.multiple_of(ci * R, 8)
            ze = z_ref[pl.ds(start, R + CONV_HALO), :]
            acc = jnp.zeros((R, tc), F32)
            for m in range(CONV_WIDTH):
                k = CONV_WIDTH - 1 - m
                sh = ze if m == 0 else pltpu.roll(ze, m, 0)
                acc = acc + w_ref[k:k + 1, :] * sh[CONV_HALO:, :]
            y_ref[pl.ds(start, R), :] = acc
            return 0

        lax.fori_loop(0, S // R, chunk, 0)

    return pl.pallas_call(
        body, grid=(B, C // tc),
        in_specs=[pl.BlockSpec((None, SP, tc), lambda b, c: (b, 0, c)),
                  pl.BlockSpec((32, tc), lambda b, c: (0, c))],
        out_specs=pl.BlockSpec((None, S, tc), lambda b, c: (b, 0, c)),
        out_shape=jax.ShapeDtypeStruct((B, S, C), F32),
        compiler_params=_cp(("parallel", "parallel")), name=name,
    )(zp, w)


def _conv_bwd(zp, dyp, w, *, R, tc, name):
    B, SP, C = zp.shape
    S = SP - CONV_HALO
    R, tc = min(R, S), min(tc, C)

    def body(z_ref, dy_ref, w_ref, dz_ref, dw_ref):
        @pl.when(pl.program_id(1) == 0)
        def _():
            dw_ref[...] = jnp.zeros_like(dw_ref)

        def chunk(ci, _):
            start = pl.multiple_of(ci * R, 8)
            ze = z_ref[pl.ds(start, R + CONV_HALO), :]
            de = dy_ref[pl.ds(start, R + CONV_HALO), :]
            dy = de[:R, :]
            acc = jnp.zeros((R, tc), F32)
            for m in range(CONV_WIDTH):
                k = CONV_WIDTH - 1 - m
                zs = ze if m == 0 else pltpu.roll(ze, m, 0)
                ds_ = de if m == 0 else pltpu.roll(de, R + CONV_HALO - m, 0)
                acc = acc + w_ref[k:k + 1, :] * ds_[:R, :]
                dw_ref[k:k + 1, :] += jnp.sum(dy * zs[CONV_HALO:, :], axis=0, keepdims=True)
            dz_ref[pl.ds(start, R), :] = acc
            return 0

        lax.fori_loop(0, S // R, chunk, 0)

    return pl.pallas_call(
        body, grid=(C // tc, B),
        in_specs=[pl.BlockSpec((None, SP, tc), lambda c, b: (b, 0, c)),
                  pl.BlockSpec((None, SP, tc), lambda c, b: (b, 0, c)),
                  pl.BlockSpec((32, tc), lambda c, b: (0, c))],
        out_specs=[pl.BlockSpec((None, S, tc), lambda c, b: (b, 0, c)),
                   pl.BlockSpec((32, tc), lambda c, b: (0, c))],
        out_shape=[jax.ShapeDtypeStruct((B, S, C), F32), jax.ShapeDtypeStruct((32, C), F32)],
        compiler_params=_cp(("parallel", "arbitrary")), name=name,
    )(zp, dyp, w)


def _gmlp_fwd(u, vn, ws, bcol, *, nck, name):
    T, E = u.shape
    H = ws.shape[0]
    he = E // H
    rows = nck * GMLP_CHUNK
    rows = min(rows, T)
    n_in = rows // GMLP_CHUNK

    def body(u_ref, v_ref, ws_ref, b_ref, o_ref):
        for c in range(n_in):
            rs = slice(c * GMLP_CHUNK, (c + 1) * GMLP_CHUNK)
            for hh in range(H):
                cs = slice(hh * he, (hh + 1) * he)
                v2 = jnp.dot(ws_ref[hh], v_ref[rs, cs].astype(BF16), preferred_element_type=F32)
                v2 = v2 + b_ref[:, hh:hh + 1]
                o_ref[rs, cs] = (u_ref[rs, cs] * v2).astype(o_ref.dtype)

    tok = pl.BlockSpec((rows, E), lambda i: (i, 0))
    return pl.pallas_call(
        body, grid=(T // rows,),
        in_specs=[tok, tok, pl.BlockSpec(ws.shape, lambda i: (0, 0, 0)), pl.BlockSpec(bcol.shape, lambda i: (0, 0))],
        out_specs=tok, out_shape=jax.ShapeDtypeStruct((T, E), BF16),
        compiler_params=_cp(("parallel",)), name=name,
    )(u, vn, ws, bcol)


def _gmlp_bwd(duv, u, vn, ws, bcol, *, nck, name):
    T, E = u.shape
    H = ws.shape[0]
    he = E // H
    rows = min(nck * GMLP_CHUNK, T)
    n_in = rows // GMLP_CHUNK
    dims_nt = (((1,), (1,)), ((), ()))
    dims_tn = (((0,), (0,)), ((), ()))

    def body(g_ref, u_ref, v_ref, ws_ref, b_ref, du_ref, dv_ref, dws_ref, db_ref):
        @pl.when(pl.program_id(0) == 0)
        def _():
            dws_ref[...] = jnp.zeros_like(dws_ref)
            db_ref[...] = jnp.zeros_like(db_ref)

        for c in range(n_in):
            rs = slice(c * GMLP_CHUNK, (c + 1) * GMLP_CHUNK)
            for hh in range(H):
                cs = slice(hh * he, (hh + 1) * he)
                vb = v_ref[rs, cs].astype(BF16)
                v2 = jnp.dot(ws_ref[hh], vb, preferred_element_type=F32) + b_ref[:, hh:hh + 1]
                g = g_ref[rs, cs]
                du_ref[rs, cs] = g * v2
                dv2 = g * u_ref[rs, cs]
                dv2b = dv2.astype(BF16)
                dv_ref[rs, cs] = lax.dot_general(ws_ref[hh], dv2b, dims_tn, preferred_element_type=F32)
                dws_ref[hh] += lax.dot_general(dv2b, vb, dims_nt, preferred_element_type=F32)
                db_ref[:, hh:hh + 1] += jnp.sum(dv2, axis=1, keepdims=True)

    tok = pl.BlockSpec((rows, E), lambda i: (i, 0))
    return pl.pallas_call(
        body, grid=(T // rows,),
        in_specs=[tok, tok, tok, pl.BlockSpec(ws.shape, lambda i: (0, 0, 0)), pl.BlockSpec(bcol.shape, lambda i: (0, 0))],
        out_specs=[tok, tok, pl.BlockSpec(ws.shape, lambda i: (0, 0, 0)), pl.BlockSpec(bcol.shape, lambda i: (0, 0))],
        out_shape=[jax.ShapeDtypeStruct((T, E), F32), jax.ShapeDtypeStruct((T, E), F32),
                   jax.ShapeDtypeStruct(ws.shape, F32), jax.ShapeDtypeStruct(bcol.shape, F32)],
        compiler_params=_cp(("arbitrary",)), name=name,
    )(duv, u, vn, ws, bcol)


PAIRS = ATT_HEADS // 2


def _att_consts():
    ji = lax.broadcasted_iota(jnp.int32, (2 * ATT_BLK, ATT_BLK), 0)
    ii = lax.broadcasted_iota(jnp.int32, (2 * ATT_BLK, ATT_BLK), 1)
    dist = ii + ATT_BLK - ji
    band = (dist >= 0) & (dist <= ATT_BLK)
    cur = ji >= ATT_BLK
    first_head = lax.broadcasted_iota(jnp.int32, (ATT_BLK, 2 * HEAD_DIM), 1) < HEAD_DIM
    return band, cur, first_head


def _both_heads(t, first_head):
    zero = jnp.zeros_like(t)
    return jnp.concatenate([jnp.where(first_head, t, zero), jnp.where(first_head, zero, t)], axis=0)


def _att_specs(nbk, offs, nsteps, rev):
    rows = nbk * ATT_BLK
    step = (lambda i: nsteps - 1 - i) if rev else (lambda i: i)
    qoff, koff, voff = offs
    blk = lambda off: pl.BlockSpec((rows, 2 * HEAD_DIM), lambda hp, i: (step(i), off + hp))
    prev = lambda off: pl.BlockSpec((ATT_BLK, 2 * HEAD_DIM), lambda hp, i: (jnp.maximum(step(i) * nbk - 1, 0), off + hp))
    out = pl.BlockSpec((rows, 2 * HEAD_DIM), lambda hp, i: (step(i), hp))
    stat = pl.BlockSpec((2, nbk, ATT_BLK), lambda hp, i: (hp, step(i), 0))
    return [blk(qoff), blk(koff), prev(koff), blk(voff), prev(voff)], out, stat


def _att_fwd(arr, offs, *, nb, nbk, name):
    T = arr.shape[0]
    nbk = min(nbk, T // ATT_BLK)
    nsteps = T // (nbk * ATT_BLK)
    scale = HEAD_DIM ** -0.5
    dims_nt = (((1,), (1,)), ((), ()))
    dims_tn = (((0,), (0,)), ((), ()))

    def body(q_ref, k_ref, kp_ref, v_ref, vp_ref, o_ref, lse_ref):
        i = pl.program_id(1)
        band, cur, first_head = _att_consts()
        for jj in range(nbk):
            rs = slice(jj * ATT_BLK, (jj + 1) * ATT_BLK)
            ps = slice((jj - 1) * ATT_BLK, jj * ATT_BLK)
            has_prev = ((i * nbk + jj) & (nb - 1)) != 0
            valid = band & (cur | has_prev)
            kk = jnp.concatenate([kp_ref[...] if jj == 0 else k_ref[ps, :], k_ref[rs, :]], axis=0)
            vv = jnp.concatenate([vp_ref[...] if jj == 0 else v_ref[ps, :], v_ref[rs, :]], axis=0)
            q2 = _both_heads(q_ref[rs, :], first_head)
            st = lax.dot_general(kk, q2, dims_nt, preferred_element_type=F32) * scale
            st = jnp.where(jnp.concatenate([valid, valid], axis=1), st, MASK_VALUE)
            m = jnp.max(st, axis=0, keepdims=True)
            p = jnp.exp(st - m)
            l = jnp.sum(p, axis=0, keepdims=True)
            lse = m + jnp.log(l)
            lse_ref[0, jj:jj + 1, :] = lse[:, :ATT_BLK]
            lse_ref[1, jj:jj + 1, :] = lse[:, ATT_BLK:]
            pn = (p / l).astype(BF16)
            o2 = lax.dot_general(pn, vv, dims_tn, preferred_element_type=F32)
            o_ref[rs, :] = jnp.where(first_head, o2[:ATT_BLK], o2[ATT_BLK:])

    ins, out, stat = _att_specs(nbk, offs, nsteps, False)
    return pl.pallas_call(
        body, grid=(PAIRS, nsteps), in_specs=ins, out_specs=[out, stat],
        out_shape=[jax.ShapeDtypeStruct((T, ATT_W), F32), jax.ShapeDtypeStruct((ATT_HEADS, T // ATT_BLK, ATT_BLK), F32)],
        compiler_params=_cp(("parallel", "parallel")), name=name,
    )(arr, arr, arr, arr, arr)


def _att_bwd(arr, offs, do, lse, dlse, *, nb, nbk, name):
    T = arr.shape[0]
    nbk = min(nbk, T // ATT_BLK)
    nsteps = T // (nbk * ATT_BLK)
    scale = HEAD_DIM ** -0.5
    dims_nt = (((1,), (1,)), ((), ()))
    dims_tn = (((0,), (0,)), ((), ()))

    def body(q_ref, k_ref, kp_ref, v_ref, vp_ref, do_ref, lse_ref, dlse_ref, dq_ref, dk_ref, dv_ref, ck, cv):
        step = pl.program_id(1)
        i = nsteps - 1 - step
        band, cur, first_head = _att_consts()

        @pl.when(step == 0)
        def _():
            ck[...] = jnp.zeros_like(ck)
            cv[...] = jnp.zeros_like(cv)

        carry_k, carry_v = ck[...], cv[...]
        for jj in reversed(range(nbk)):
            rs = slice(jj * ATT_BLK, (jj + 1) * ATT_BLK)
            ps = slice((jj - 1) * ATT_BLK, jj * ATT_BLK)
            has_prev = ((i * nbk + jj) & (nb - 1)) != 0
            valid = band & (cur | has_prev)
            kk = jnp.concatenate([kp_ref[...] if jj == 0 else k_ref[ps, :], k_ref[rs, :]], axis=0)
            vv = jnp.concatenate([vp_ref[...] if jj == 0 else v_ref[ps, :], v_ref[rs, :]], axis=0)
            q2 = _both_heads(q_ref[rs, :], first_head)
            do2 = _both_heads(do_ref[rs, :].astype(BF16), first_head)
            lse = jnp.concatenate([lse_ref[0, jj:jj + 1, :], lse_ref[1, jj:jj + 1, :]], axis=1)
            dlse = jnp.concatenate([dlse_ref[0, jj:jj + 1, :], dlse_ref[1, jj:jj + 1, :]], axis=1)
            st = lax.dot_general(kk, q2, dims_nt, preferred_element_type=F32) * scale
            st = jnp.where(jnp.concatenate([valid, valid], axis=1), st, MASK_VALUE)
            p = jnp.exp(st - lse)
            dp = lax.dot_general(vv, do2, dims_nt, preferred_element_type=F32)
            delta = jnp.sum(p * dp, axis=0, keepdims=True)
            dsb = (p * (dp - delta + dlse) * scale).astype(BF16)
            dq2 = lax.dot_general(dsb, kk, dims_tn, preferred_element_type=F32)
            dkk = jnp.dot(dsb, q2, preferred_element_type=F32)
            dvv = jnp.dot(p.astype(BF16), do2, preferred_element_type=F32)
            dq_ref[rs, :] = jnp.where(first_head, dq2[:ATT_BLK], dq2[ATT_BLK:]).astype(dq_ref.dtype)
            dk_ref[rs, :] = (dkk[ATT_BLK:] + carry_k).astype(dk_ref.dtype)
            dv_ref[rs, :] = (dvv[ATT_BLK:] + carry_v).astype(dv_ref.dtype)
            carry_k, carry_v = dkk[:ATT_BLK], dvv[:ATT_BLK]
        ck[...] = carry_k
        cv[...] = carry_v

    ins, out, stat = _att_specs(nbk, offs, nsteps, True)
    return pl.pallas_call(
        body, grid=(PAIRS, nsteps), in_specs=ins + [out, stat, stat], out_specs=[out] * 3,
        out_shape=[jax.ShapeDtypeStruct((T, ATT_W), BF16)] * 3,
        scratch_shapes=[pltpu.VMEM((ATT_BLK, 2 * HEAD_DIM), F32), pltpu.VMEM((ATT_BLK, 2 * HEAD_DIM), F32)],
        compiler_params=_cp(("arbitrary", "arbitrary")), name=name,
    )(arr, arr, arr, arr, arr, do, lse, dlse)


def _deinterleave(t, B, S, dil):
    if dil == 1:
        return t
    return t.reshape((B, S // dil, dil) + t.shape[1:]).swapaxes(1, 2).reshape(t.shape)


def _interleave(t, B, S, dil):
    if dil == 1:
        return t
    return t.reshape((B, dil, S // dil) + t.shape[1:]).swapaxes(1, 2).reshape(t.shape)


def _stats_to_tokens(lse, B, S, dil):
    return _interleave(lse.reshape(lse.shape[0], -1).T, B, S, dil)


def _stats_from_tokens(dl, B, S, dil):
    return _deinterleave(dl, B, S, dil).T.reshape(dl.shape[1], -1, ATT_BLK)


def _mesh_pos():
    return lax.axis_index("x"), lax.axis_index("y"), lax.axis_index("c")


def _allgather8(xs, *, name):
    m_per, n = xs.shape

    def body(x_ref, out_ref, send_sems, recv_sems, local_sem):
        x, y, c = _mesh_pos()
        me, sibling = (x, y, c), (x, y, 1 - c)
        chips = [(1 - x, y), (x, 1 - y), (1 - x, 1 - y)]

        def rows(px, py, pc):
            return out_ref.at[pl.ds((4 * px + 2 * py + pc) * m_per, m_per), :]

        def copy(k, block, to, src=None):
            return pltpu.make_async_remote_copy(
                src_ref=rows(*block) if src is None else src, dst_ref=rows(*block),
                send_sem=send_sems.at[k], recv_sem=recv_sems.at[k], device_id=to, device_id_type=MESH)

        mine = pltpu.make_async_copy(x_ref, rows(*me), local_sem)
        mine.start()
        first = [copy(0, me, sibling, src=x_ref)]
        first += [copy(1 + j, me, (*chip, c), src=x_ref) for j, chip in enumerate(chips)]
        for cp in first:
            cp.start()
        passed = [copy(4 + j, (*chip, c), sibling) for j, chip in enumerate(chips)]
        for j, chip in enumerate(chips):
            copy(1 + j, (*chip, c), me).wait_recv()
            passed[j].start()
        copy(0, sibling, me).wait_recv()
        for j, chip in enumerate(chips):
            copy(4 + j, (*chip, 1 - c), me).wait_recv()
        for cp in first + passed:
            cp.wait_send()
        mine.wait()

    return pl.pallas_call(
        body, out_shape=jax.ShapeDtypeStruct((8 * m_per, n), xs.dtype),
        in_specs=[pl.BlockSpec(memory_space=pltpu.VMEM)], out_specs=pl.BlockSpec(memory_space=pltpu.VMEM),
        scratch_shapes=[pltpu.SemaphoreType.DMA((7,)), pltpu.SemaphoreType.DMA((7,)), pltpu.SemaphoreType.DMA],
        compiler_params=pltpu.CompilerParams(vmem_limit_bytes=VMEM_LIMIT), name=name,
    )(xs)


def _hbm_call(body, arrays, out_shapes, n_sems, *, name):
    any_spec = pl.BlockSpec(memory_space=pl.ANY)
    return pl.pallas_call(
        body, out_shape=out_shapes, in_specs=[any_spec] * len(arrays), out_specs=[any_spec] * len(out_shapes),
        scratch_shapes=[pltpu.SemaphoreType.DMA((n_sems,)), pltpu.SemaphoreType.DMA((n_sems,))], name=name,
    )(*arrays)


def _other_chips(x, y):
    return [(1 - x, y), (x, 1 - y), (1 - x, 1 - y)]


def _allgather_chips(ws, *, name):
    n = len(ws)

    def body(*refs):
        ins, outs, (send_sems, recv_sems) = refs[:n], refs[n:2 * n], refs[2 * n:]
        x, y, c = _mesh_pos()
        chips = _other_chips(x, y)

        def copy(a, k, px, py, half, to, src=None):
            slot = outs[a].at[2 * px + py, half]
            return pltpu.make_async_remote_copy(
                src_ref=slot if src is None else src, dst_ref=slot,
                send_sem=send_sems.at[6 * a + k], recv_sem=recv_sems.at[6 * a + k], device_id=to, device_id_type=MESH)

        first = [copy(a, j, x, y, c, (*chip, c), src=ins[a].at[c]) for a in range(n) for j, chip in enumerate(chips)]
        for cp in first:
            cp.start()
        passed = []
        for j, chip in enumerate(chips):
            for a in range(n):
                copy(a, j, *chip, c, (x, y, c)).wait_recv()
                passed.append(copy(a, 3 + j, *chip, c, (x, y, 1 - c)))
                passed[-1].start()
        for j, chip in enumerate(chips):
            for a in range(n):
                copy(a, 3 + j, *chip, 1 - c, (x, y, c)).wait_recv()
        for cp in first + passed:
            cp.wait_send()

    return _hbm_call(body, ws, [jax.ShapeDtypeStruct((N_CHIPS,) + w.shape, w.dtype) for w in ws], 6 * n, name=name)


def _split_start(srcs, lands, after, issue, n_sems, *, name):
    ns, nl = len(srcs), len(lands)
    hbm, sem = pl.BlockSpec(memory_space=pltpu.HBM), pl.BlockSpec(memory_space=pltpu.SEMAPHORE)
    extra = [] if after is None else [after]

    def body(*refs):
        n_in = ns + nl + len(extra)
        send_sems, recv_sems = refs[n_in], refs[n_in + 1]
        issue(refs[:ns], refs[ns:ns + nl], send_sems, recv_sems)
        refs[-1][...] = jnp.zeros_like(refs[-1])

    arrays = [pltpu.with_memory_space_constraint(a, pltpu.HBM) for a in list(srcs) + list(lands)]
    out = pl.pallas_call(
        body, name=name,
        out_shape=(pltpu.SemaphoreType.DMA((n_sems,)), pltpu.SemaphoreType.DMA((n_sems,)),
                   *[pltpu.HBM(a.shape, a.dtype) for a in arrays], jax.ShapeDtypeStruct((8, 128), F32)),
        in_specs=[hbm] * (ns + nl) + [pl.BlockSpec(memory_space=pl.ANY)] * len(extra),
        out_specs=(sem, sem, *[hbm] * (ns + nl), pl.BlockSpec(memory_space=pltpu.VMEM)),
        input_output_aliases={i: 2 + i for i in range(ns + nl)},
        compiler_params=pltpu.CompilerParams(has_side_effects=pltpu.SideEffectType.DATAFLOW_SIDE_EFFECTING),
    )(*arrays, *extra)
    return out[0], out[1], list(out[2:2 + ns]), list(out[2 + ns:2 + ns + nl]), out[-1]


def _split_wait(send_sems, recv_sems, srcs, lands, after, waits, *, name):
    ns, nl = len(srcs), len(lands)
    hbm, sem = pl.BlockSpec(memory_space=pltpu.HBM), pl.BlockSpec(memory_space=pltpu.SEMAPHORE)

    def body(*refs):
        waits(refs[:ns], refs[ns:ns + nl], refs[ns + nl], refs[ns + nl + 1])

    out = pl.pallas_call(
        body, name=name,
        out_shape=tuple(pltpu.HBM(a.shape, a.dtype) for a in list(srcs) + list(lands)),
        in_specs=[hbm] * (ns + nl) + [sem, sem, pl.BlockSpec(memory_space=pl.ANY)],
        out_specs=tuple([hbm] * (ns + nl)),
        input_output_aliases={i: i for i in range(ns + nl)},
        compiler_params=pltpu.CompilerParams(has_side_effects=pltpu.SideEffectType.DATAFLOW_SIDE_EFFECTING),
    )(*srcs, *lands, send_sems, recv_sems, after)
    return list(out[:ns]), list(out[ns:])


def _gather_start(halves, after, *, name):
    n = len(halves)
    lands = [lax.empty((N_CHIPS,) + h.shape, h.dtype) for h in halves]

    def issue(srcs, dsts, send_sems, recv_sems):
        x, y, c = _mesh_pos()
        me = 2 * x + y
        for a in range(n):
            for j, (px, py) in enumerate(_other_chips(x, y)):
                for cc in range(2):
                    pltpu.make_async_remote_copy(
                        src_ref=srcs[a].at[c], dst_ref=dsts[a].at[me, c],
                        send_sem=send_sems.at[6 * a + 2 * j + cc], recv_sem=recv_sems.at[6 * a + 2 * j + c],
                        device_id=(px, py, cc), device_id_type=MESH).start()

    return _split_start(halves, lands, after, issue, 6 * n, name=name)


def _gather_wait(started, after, *, name):
    send_sems, recv_sems, halves, lands = started
    n = len(halves)

    def waits(srcs, dsts, send_sems, recv_sems):
        x, y, c = _mesh_pos()
        me = 2 * x + y
        for a in range(n):
            for j, (px, py) in enumerate(_other_chips(x, y)):
                for cc in range(2):
                    pltpu.make_async_remote_copy(
                        src_ref=srcs[a].at[cc], dst_ref=dsts[a].at[2 * px + py, cc],
                        send_sem=send_sems.at[6 * a + 2 * j + cc], recv_sem=recv_sems.at[6 * a + 2 * j + cc],
                        device_id=(px, py, cc), device_id_type=MESH).wait_recv()
        for a in range(n):
            for j, (px, py) in enumerate(_other_chips(x, y)):
                for cc in range(2):
                    pltpu.make_async_remote_copy(
                        src_ref=srcs[a].at[c], dst_ref=dsts[a].at[me, c],
                        send_sem=send_sems.at[6 * a + 2 * j + cc], recv_sem=recv_sems.at[6 * a + 2 * j + c],
                        device_id=(px, py, cc), device_id_type=MESH).wait_send()

    return _split_wait(send_sems, recv_sems, halves, lands, after, waits, name=name)


def _reduce_plan_loops(plans, chip, c, fn):
    for a, plan in enumerate(plans):
        for h, cc in plan:
            for k in range(N_CHIPS):
                fn(a, h, k, cc, jnp.logical_or(chip != k, c != cc))


def _reduce_start(srcs, lands, plans, after, *, name):
    def issue(src_refs, land_refs, send_sems, recv_sems):
        x, y, c = _mesh_pos()
        chip = 2 * x + y
        my_id = 2 * chip + c

        def send(a, h, k, cc, is_other):
            @pl.when(is_other)
            def _():
                pltpu.make_async_remote_copy(
                    src_ref=src_refs[a].at[h, k], dst_ref=land_refs[a].at[my_id],
                    send_sem=send_sems.at[8 * a + 2 * k + cc], recv_sem=recv_sems.at[8 * a + my_id],
                    device_id=(k // 2, k % 2, cc), device_id_type=MESH).start()

        _reduce_plan_loops(plans, chip, c, send)

    return _split_start(srcs, lands, after, issue, 8 * len(srcs), name=name)


def _reduce_wait(started, plans, after, *, name):
    send_sems, recv_sems, srcs, lands = started

    def waits(src_refs, land_refs, send_sems, recv_sems):
        x, y, c = _mesh_pos()
        chip = 2 * x + y
        my_id = 2 * chip + c
        for a, plan in enumerate(plans):
            for h, cc in plan:
                for s in range(2 * N_CHIPS):
                    @pl.when(jnp.logical_and(c == cc, my_id != s))
                    def _(a=a, h=h, s=s):
                        pltpu.make_async_remote_copy(
                            src_ref=src_refs[a].at[h, 0], dst_ref=land_refs[a].at[s],
                            send_sem=send_sems.at[8 * a + s], recv_sem=recv_sems.at[8 * a + s],
                            device_id=(s // 4, (s // 2) % 2, s % 2), device_id_type=MESH).wait_recv()

        def sent(a, h, k, cc, is_other):
            @pl.when(is_other)
            def _():
                pltpu.make_async_remote_copy(
                    src_ref=src_refs[a].at[h, k], dst_ref=land_refs[a].at[my_id],
                    send_sem=send_sems.at[8 * a + 2 * k + cc], recv_sem=recv_sems.at[8 * a + my_id],
                    device_id=(k // 2, k % 2, cc), device_id_type=MESH).wait_send()

        _reduce_plan_loops(plans, chip, c, sent)

    return _split_wait(send_sems, recv_sems, srcs, lands, after, waits, name=name)


def _sum8(land, own, my_id, *, name):
    n_src, R, C = land.shape
    tr = _pick_rows(R, max(8, 1024 * 1024 // (2 * C)))

    def body(id_ref, *refs):
        own_ref, o_ref = refs[n_src], refs[n_src + 1]
        me = id_ref[0]
        acc = None
        for s in range(n_src):
            term = jnp.where(me == s, own_ref[...], refs[s][...]).astype(F32)
            acc = term if acc is None else acc + term
        o_ref[...] = acc

    return pl.pallas_call(
        body, out_shape=jax.ShapeDtypeStruct((R, C), F32),
        grid_spec=pltpu.PrefetchScalarGridSpec(
            num_scalar_prefetch=1, grid=(R // tr,),
            in_specs=[pl.BlockSpec((None, tr, C), lambda i, idr, s=s: (s, i, 0)) for s in range(n_src)]
            + [pl.BlockSpec((tr, C), lambda i, idr: (i, 0))],
            out_specs=pl.BlockSpec((tr, C), lambda i, idr: (i, 0))),
        compiler_params=_cp(("parallel",)), name=name,
    )(my_id.reshape(1).astype(jnp.int32), *([land] * n_src), own)


def _swap_halves(gs, *, name):
    n = len(gs)

    def body(*refs):
        ins, outs, (send_sems, recv_sems) = refs[:n], refs[n:2 * n], refs[2 * n:]
        x, y, c = _mesh_pos()
        cps = [pltpu.make_async_remote_copy(
            src_ref=ins[a].at[1 - c], dst_ref=outs[a], send_sem=send_sems.at[a], recv_sem=recv_sems.at[a],
            device_id=(x, y, 1 - c), device_id_type=MESH) for a in range(n)]
        for cp in cps:
            cp.start()
        for cp in cps:
            cp.wait()

    return _hbm_call(body, gs, [jax.ShapeDtypeStruct(g.shape[1:], g.dtype) for g in gs], n, name=name)


def _scatter_chips(ss, *, name):
    n = len(ss)

    def body(*refs):
        ins, outs, (send_sems, recv_sems) = refs[:n], refs[n:2 * n], refs[2 * n:]
        x, y, c = _mesh_pos()
        me = 2 * x + y
        chips = _other_chips(x, y)

        def copy(a, j, px, py):
            return pltpu.make_async_remote_copy(
                src_ref=ins[a].at[2 * px + py], dst_ref=outs[a].at[me],
                send_sem=send_sems.at[3 * a + j], recv_sem=recv_sems.at[3 * a + j],
                device_id=(px, py, c), device_id_type=MESH)

        def arrival(a, j, px, py):
            return pltpu.make_async_remote_copy(
                src_ref=ins[a].at[me], dst_ref=outs[a].at[2 * px + py],
                send_sem=send_sems.at[3 * a + j], recv_sem=recv_sems.at[3 * a + j],
                device_id=(px, py, c), device_id_type=MESH)

        cps = [copy(a, j, *chip) for a in range(n) for j, chip in enumerate(chips)]
        for cp in cps:
            cp.start()
        for a in range(n):
            for j, chip in enumerate(chips):
                arrival(a, j, *chip).wait_recv()
        for cp in cps:
            cp.wait_send()

    return _hbm_call(body, ss, [jax.ShapeDtypeStruct(s.shape, s.dtype) for s in ss], 3 * n, name=name)


def _share_halves(ts, *, name):
    n = len(ts)

    def body(*refs):
        ins, outs, (send_sems, recv_sems) = refs[:n], refs[n:2 * n], refs[2 * n:]
        x, y, c = _mesh_pos()
        cps = [pltpu.make_async_remote_copy(
            src_ref=ins[a], dst_ref=outs[a].at[c], send_sem=send_sems.at[a], recv_sem=recv_sems.at[a],
            device_id=(x, y, 1 - c), device_id_type=MESH) for a in range(n)]
        for cp in cps:
            cp.start()
        for a in range(n):
            pltpu.make_async_remote_copy(
                src_ref=ins[a], dst_ref=outs[a].at[1 - c], send_sem=send_sems.at[a], recv_sem=recv_sems.at[a],
                device_id=(x, y, 1 - c), device_id_type=MESH).wait_recv()
        for cp in cps:
            cp.wait_send()

    return _hbm_call(body, ts, [jax.ShapeDtypeStruct((2,) + t.shape, t.dtype) for t in ts], n, name=name)


def _half_add(g, ra, core, *, name):
    _, R, C = g.shape
    tr = _pick_rows(R, max(8, 2 * 1024 * 1024 // (4 * C)))

    def body(core_ref, g_ref, ra_ref, o_ref):
        o_ref[...] = (g_ref[...] + ra_ref[...]).astype(o_ref.dtype)

    return pl.pallas_call(
        body, out_shape=jax.ShapeDtypeStruct((R, C), BF16),
        grid_spec=pltpu.PrefetchScalarGridSpec(
            num_scalar_prefetch=1, grid=(R // tr,),
            in_specs=[pl.BlockSpec((None, tr, C), lambda i, cr: (cr[0], i, 0)),
                      pl.BlockSpec((tr, C), lambda i, cr: (i, 0))],
            out_specs=pl.BlockSpec((tr, C), lambda i, cr: (i, 0))),
        compiler_params=_cp(("parallel",)), name=name,
    )(core.reshape(1).astype(jnp.int32), g, ra)


def _sum4(rb, *, name):
    _, R, C = rb.shape
    tr = _pick_rows(R, max(8, 2 * 1024 * 1024 // (4 * C)))

    def body(r0, r1, r2, r3, o_ref):
        f = lambda r: r[...].astype(F32)
        o_ref[...] = ((f(r0) + f(r1)) + f(r2)) + f(r3)

    return pl.pallas_call(
        body, out_shape=jax.ShapeDtypeStruct((R, C), F32), grid=(R // tr,),
        in_specs=[pl.BlockSpec((None, tr, C), lambda i, k=k: (k, i, 0)) for k in range(N_CHIPS)],
        out_specs=pl.BlockSpec((tr, C), lambda i: (i, 0)),
        compiler_params=_cp(("parallel",)), name=name,
    )(rb, rb, rb, rb)


TR = 256
S5_CHUNK = 256


def _rms_fwd(x, g, name):
    return _rowwise(_f_rms, [x], [g], (BF16,), tr=TR, name=name)[0]


def _rms_bwd_epi(dh, x, g, gx):
    r = lax.rsqrt(jnp.mean(x * x, axis=-1, keepdims=True) + EPS)
    xr = x * r
    t = dh * g
    dx = r * (t - xr * jnp.mean(t * xr, axis=-1, keepdims=True)) + gx
    return dx, dx, jnp.sum(dh * xr, axis=0, keepdims=True)


def _mm_rms_bwd(a, w, x, g, gx, *, after=None, name, **kw):
    return _mm(a, w, tb=True, epi=_rms_bwd_epi, extras=(x, g, gx), out_dtypes=(F32, BF16, F32), n_row_sums=1,
               tm=512, tn=x.shape[1], after=after, name=name, **kw)


def _rms_bwd(x, g, dh, gx, name, after=None):
    (dx, dxb), (dg,) = _rowwise_vjp(_f_rms, [x], [g], [dh], [(F32, BF16)], adds={0: gx}, after=after, tr=TR,
                                    name=name)
    return dx, dxb, dg


def _grad_cols(M, Nq):
    def imap(tm, tn):
        hp, per = (M // 2) // tm, Nq // tn
        assert hp * tm * 2 == M and per * tn == Nq, (M, Nq, tm, tn)
        return lambda i, j, k: (i // hp, j // per, i % hp, j % per)
    return (2, N_CHIPS, M // 2, Nq), lambda tm, tn: (None, None, tm, tn), imap, None, M // 2, Nq


def _grad_rows(Mq, N):
    def imap(tm, tn):
        po, hp = Mq // tm, (Mq // 2) // tm
        assert hp * tm * 2 == Mq, (Mq, tm)
        return lambda i, j, k: ((i % po) // hp, i // po, (i % po) % hp, j)
    return (2, N_CHIPS, Mq // 2, N), lambda tm, tn: (None, None, tm, tn), imap, None, Mq // 2, N


def _grad_layer_cols(slot, lh, M, Nq, buf):
    def imap(tm, tn):
        per = Nq // tn
        return lambda i, j, k: (j // per, slot, i, j % per)
    return (N_CHIPS, lh, M, Nq), lambda tm, tn: (None, None, tm, tn), imap, buf, M, Nq


def _grad_layer_rows(slot, lh, Mq, N, buf):
    def imap(tm, tn):
        po = Mq // tm
        return lambda i, j, k: (i // po, slot, i % po, j)
    return (N_CHIPS, lh, Mq, N), lambda tm, tn: (None, None, tm, tn), imap, buf, Mq, N


def _add_then_rms(acc, res, g):
    xo = acc + res
    return xo, _f_rms(xo, g)[0]


def _mlp_fwd(x, h2, w_in, w_out, g_next, li):
    r = _mm(h2, w_in, out_dtypes=(BF16,), epi=lambda acc: (jnp.maximum(acc, 0.0),), tm=2048, name=f"mlp_in_{li}")
    if g_next is None:
        x_out, h_next = _mm(r, w_out, pro_a=lambda t: t * t, epi=lambda acc, res: (acc + res,), extras=(x,),
                            name=f"mlp_out_{li}"), None
    else:
        x_out, h_next = _mm(r, w_out, pro_a=lambda t: t * t, epi=_add_then_rms, extras=(x, g_next),
                            out_dtypes=(F32, BF16), tn=x.shape[1], tk=2048, name=f"mlp_out_{li}")
    return x_out, h_next, (h2, r)


def _mlp_bwd(gx, gxb, x, g, w_in, w_out, saved, li, nl, bufs):
    h2, r = saved
    D, F = w_in.shape
    lh = nl // 2
    da = _mm(gxb, w_out, tb=True, out_dtypes=(BF16,),
             epi=lambda acc, rt: (acc * 2.0 * rt.astype(F32),), extras=(r,), tm=2048, name=f"mlp_dact_{li}")
    buf_in, buf_out = bufs if bufs is not None else (None, None)
    d_w_out = _mm(r, gxb, ta=True, pro_a=lambda t: t * t, tm=512, tn=1024, tk=4096, out_dtypes=(BF16,),
                  out=_grad_layer_rows(li % lh, lh, F // N_CHIPS, D, buf_out), name=f"mlp_dwout_{li}")
    d_w_in = _mm(h2, da, ta=True, tm=1024, tn=1024, tk=4096, out_dtypes=(BF16,),
                 out=_grad_layer_cols(li % lh, lh, D, F // N_CHIPS, buf_in), name=f"mlp_dwin_{li}")
    gx_mid, gxb_mid, dg = _mm_rms_bwd(da, w_in, x, g, gx, name=f"mlp_dh_{li}")
    return gx_mid, gxb_mid, dg, (d_w_in, d_w_out)


def _local_step(x3, tgt3, p, layer_weights, token=None, grads_done=lambda group: None):
    B, S, D = x3.shape
    T = B * S
    x = x3.reshape(T, D)
    grads = {}
    row = lambda v: v.reshape(1, -1)
    p = dict(p)
    nl = p["norm_mlp"].shape[0]
    mlp_in, mlp_out = [None] * nl, [None] * nl

    def fetch(li, after):
        wl = dict(layer_weights(li, after))
        mlp_in[li], mlp_out[li] = wl.pop("mlp_w_in"), wl.pop("mlp_w_out")
        p.update(wl)

    g0 = row(p["norm_mix"][0])
    if token is not None:
        g0 = g0 + token[:1, :1]
    h0 = _rms_fwd(x, g0, "rms_mix_0")
    s5_args = (p["ssm_a_re"][0], p["ssm_a_im"][0], p["ssm_b_re"][0], p["ssm_b_im"][0],
               p["ssm_c_re"][0], p["ssm_c_im"][0], p["ssm_log_dt"][0])
    s5_exp, s5_vjp = jax.vjp(_s5_prep, *s5_args)
    abr, abi, bre, bim, cre, cim = s5_exp
    bre_b, bim_b, cre_b, cim_b = (t.astype(BF16) for t in (bre, bim, cre, cim))
    d_skip = p["ssm_d"]
    ypre, yb, sxr, sxi, ser, sei = _s5_fwd(h0, abr, abi, bre_b, bim_b, cre_b, cim_b, d_skip, B=B, L=S5_CHUNK,
                                           name="s5_fwd")
    fetch(0, yb)
    w_glu = p["ssm_w_glu"]
    z0 = _mm(yb, w_glu, tm=2048, name="s5_glu_mm")
    gm = [row(p["norm_mlp"][i]) for i in range(nl)]
    g1, g2, g3 = (row(p["norm_mix"][i]) for i in range(1, nl))
    x_mid0, hm0 = _rowwise(lambda z, xr, g: _add_then_rms(_f_glu(z)[0], xr, g), [z0, x], [gm[0]], (F32, BF16),
                           tr=TR, name="s5_glu")
    x1, h1, mlp_saved0 = _mlp_fwd(x_mid0, hm0, mlp_in[0], mlp_out[0], g1, 0)

    fetch(1, h1)
    z1 = _mm(h1, p["conv_w_pw1"], tm=2048, name="conv_pw1")
    zg = _rowwise(_f_bias_glu, [z1], [p["conv_b_pw1"]], (F32,), tr=TR, name="conv_glu")[0]
    zp = jnp.pad(zg.reshape(B, S, D), ((0, 0), (CONV_HALO, 0), (0, 0)))
    w_dw = jnp.pad(p["conv_w_dw"], ((0, 32 - CONV_WIDTH), (0, 0)))
    yc = _conv_fwd(zp, w_dw, R=256, tc=128, name="conv_dw").reshape(T, D)
    ln_par = [p["conv_b_dw"], p["conv_ln_g"], p["conv_ln_b"]]
    qc = _rowwise(_f_ln_silu, [yc], ln_par, (BF16,), tr=TR, name="conv_ln_silu")[0]
    x_mid1, hm1 = _mm(qc, p["conv_w_pw2"], epi=lambda acc, bias, res, g: _add_then_rms(acc + bias, res, g),
                      extras=(p["conv_b_pw2"], x1, gm[1]), out_dtypes=(F32, BF16), tn=D, name="conv_pw2")
    x2, h2, mlp_saved1 = _mlp_fwd(x_mid1, hm1, mlp_in[1], mlp_out[1], g2, 1)

    fetch(2, h2)
    z2 = _mm(h2, p["gmlp_w_in"], tm=2048, name="gmlp_in")
    gl_par = [p["gmlp_ln_g"], p["gmlp_ln_b"]]
    gu, gvn = _rowwise(_f_gelu_ln, [z2], gl_par, (F32, F32), tr=TR, name="gmlp_gelu_ln")
    causal = jnp.tril(jnp.ones((GMLP_CHUNK, GMLP_CHUNK), dtype=bool))
    ws_b = jnp.where(causal[None], p["gmlp_w_s"][0], 0.0).astype(BF16)
    bcol = jnp.pad(p["gmlp_b_s"][0].T, ((0, 0), (0, 128 - GMLP_HEADS)))
    uv = _gmlp_fwd(gu, gvn, ws_b, bcol, nck=4, name="gmlp_spatial")
    x_mid2, hm2 = _mm(uv, p["gmlp_w_out"], epi=_add_then_rms, extras=(x2, gm[2]), out_dtypes=(F32, BF16), tn=D,
                      name="gmlp_out")
    x3_, h3, mlp_saved2 = _mlp_fwd(x_mid2, hm2, mlp_in[2], mlp_out[2], g3, 2)

    fetch(3, h3)
    ng = len(ATT_DILS)
    att_in, o_tok, l_tok, lses = [], [], [], []
    offs = (0, PAIRS, 2 * PAIRS)
    for gi, dil in enumerate(ATT_DILS):
        w_g = _ColBlocks(p["attn_w_qkv_plain"], gi, ng, 3, ATT_W)
        arr = _mm(h3, w_g, out_dtypes=(BF16,), tm=2048, tn=ATT_W, name=f"attn_qkv_{gi}")
        arr = _deinterleave(arr, B, S, dil)
        att_in.append((arr, offs))
        og, lg = _att_fwd(arr, offs, nb=S // dil // ATT_BLK, nbk=8, name=f"attn_fwd_{gi}")
        lses.append(lg)
        o_tok.append(_interleave(og, B, S, dil))
        l_tok.append(_stats_to_tokens(lg, B, S, dil))
    merged2 = _rowwise(_f_merge, o_tok + l_tok, [], (BF16,), tr=TR, name="attn_merge")[0]
    x_mid3, hm3 = _mm(merged2, p["attn_w_o_plain"], epi=_add_then_rms, extras=(x3_, gm[3]), out_dtypes=(F32, BF16),
                      tn=D, name="attn_out")
    x4, _, mlp_saved3 = _mlp_fwd(x_mid3, hm3, mlp_in[3], mlp_out[3], None, 3)

    loss_part, gx, gxb, dgf = _loss_head(x4, tgt3.reshape(T, D), row(p["norm_final"]), tr=TR, name="loss_head")
    grads["norm_final"] = dgf.reshape(-1)
    d_norm_mix, d_norm_mlp = [None] * 4, [None] * 4
    Dq = D // N_CHIPS

    gx, gxb, d_norm_mlp[3], mlp_hi = _mlp_bwd(
        gx, gxb, x_mid3, row(p["norm_mlp"][3]), mlp_in[3], mlp_out[3], mlp_saved3, 3, nl, None)
    dmerged = _mm(gxb, p["attn_w_o"], tb=True, name="attn_dmerged")
    grads["attn_w_o"] = _mm(merged2, gxb, ta=True, tm=256, tn=256, tk=4096, out_dtypes=(BF16,), out=_grad_cols(ATT_W, Dq), name="attn_dwo")
    dml, _ = _rowwise_vjp(_f_merge, o_tok + l_tok, [], [dmerged], [F32] * 6, tr=TR, name="attn_merge_bwd")
    pieces = [[None] * ng for _ in range(3)]
    for gi, dil in enumerate(ATT_DILS):
        arr, offs = att_in[gi]
        dqkv_g = _att_bwd(arr, offs, _deinterleave(dml[gi], B, S, dil), lses[gi],
                          _stats_from_tokens(dml[ng + gi], B, S, dil), nb=S // dil // ATT_BLK, nbk=8,
                          name=f"attn_bwd_{gi}")
        for i in range(3):
            pieces[i][gi] = _interleave(dqkv_g[i], B, S, dil)
    dqkv = jnp.concatenate([pieces[i][gi] for i in range(3) for gi in range(ng)], axis=1)
    qkv_w = 3 * ng * ATT_W
    grads["attn_w_qkv"] = _mm(h3, dqkv, ta=True, tm=512, tn=1152, tk=4096, out_dtypes=(BF16,), out=_grad_cols(D, qkv_w // N_CHIPS),
                              name="attn_dwqkv")
    tok = grads_done({n: grads[n] for n in ("attn_w_qkv", "attn_w_o")})
    gx, gxb, d_norm_mix[3] = _mm_rms_bwd(dqkv, p["attn_w_qkv"], x3_, g3, gx, tk=1152, after=tok, name="attn_dh")

    gx, gxb, d_norm_mlp[2], mlp_hi = _mlp_bwd(
        gx, gxb, x_mid2, row(p["norm_mlp"][2]), mlp_in[2], mlp_out[2], mlp_saved2, 2, nl, mlp_hi)
    tok = grads_done({"mlp_w_in": (1, mlp_hi[0]), "mlp_w_out": (1, mlp_hi[1])})
    duv = _mm(gxb, p["gmlp_w_out"], tb=True, after=tok, name="gmlp_duv")
    grads["gmlp_w_out"] = _mm(uv, gxb, ta=True, tm=128, tn=1024, tk=4096, out_dtypes=(BF16,), out=_grad_rows(Dq, D), name="gmlp_dwout")
    du, dvn, dws, dbcol = _gmlp_bwd(duv, gu, gvn, ws_b, bcol, nck=4, name="gmlp_spatial_bwd")
    grads["gmlp_w_s"] = jnp.where(causal[None], dws, 0.0)[None]
    grads["gmlp_b_s"] = dbcol[:, :GMLP_HEADS].T[None]
    (dz2,), (dlg, dlb_) = _rowwise_vjp(_f_gelu_ln, [z2], gl_par, [du, dvn], [BF16], tr=TR, name="gmlp_gelu_ln_bwd")
    grads["gmlp_ln_g"], grads["gmlp_ln_b"] = dlg, dlb_
    grads["gmlp_w_in"] = _mm(h2, dz2, ta=True, tm=512, tn=512, tk=4096, out_dtypes=(BF16,), out=_grad_cols(D, 2 * Dq), name="gmlp_dwin")
    tok = grads_done({n: grads[n] for n in ("gmlp_w_in", "gmlp_w_out")})
    gx, gxb, d_norm_mix[2] = _mm_rms_bwd(dz2, p["gmlp_w_in"], x2, g2, gx, after=tok, name="gmlp_dh")

    gx, gxb, d_norm_mlp[1], mlp_lo = _mlp_bwd(
        gx, gxb, x_mid1, row(p["norm_mlp"][1]), mlp_in[1], mlp_out[1], mlp_saved1, 1, nl, None)
    dqc = _mm(gxb, p["conv_w_pw2"], tb=True, name="conv_dq")
    grads["conv_w_pw2"] = _mm(qc, gxb, ta=True, tm=128, tn=1024, tk=4096, out_dtypes=(BF16,), out=_grad_rows(Dq, D), name="conv_dwpw2")
    _, (db2,) = _rowwise_vjp(lambda t, b: (t + b,), [gx], [p["conv_b_pw2"]], [gx], [None], tr=TR, name="conv_db2")
    grads["conv_b_pw2"] = db2
    (dyc,), (dbdw, dcg, dcb) = _rowwise_vjp(_f_ln_silu, [yc], ln_par, [dqc], [F32], tr=TR, name="conv_ln_silu_bwd")
    grads["conv_b_dw"], grads["conv_ln_g"], grads["conv_ln_b"] = dbdw, dcg, dcb
    dyp = jnp.pad(dyc.reshape(B, S, D), ((0, 0), (0, CONV_HALO), (0, 0)))
    dzg, dwdw = _conv_bwd(zp, dyp, w_dw, R=256, tc=128, name="conv_dw_bwd")
    grads["conv_w_dw"] = dwdw[:CONV_WIDTH][None]
    (dz1,), (db1,) = _rowwise_vjp(_f_bias_glu, [z1], [p["conv_b_pw1"]], [dzg.reshape(T, D)], [BF16], tr=TR,
                                  name="conv_glu_bwd")
    grads["conv_b_pw1"] = db1
    grads["conv_w_pw1"] = _mm(h1, dz1, ta=True, tm=512, tn=512, tk=4096, out_dtypes=(BF16,), out=_grad_cols(D, 2 * Dq), name="conv_dwpw1")
    tok = grads_done({n: grads[n] for n in ("conv_w_pw1", "conv_w_pw2")})
    gx, gxb, d_norm_mix[1] = _mm_rms_bwd(dz1, p["conv_w_pw1"], x1, g1, gx, after=tok, name="conv_dh")

    gx, gxb, d_norm_mlp[0], mlp_lo = _mlp_bwd(
        gx, gxb, x_mid0, row(p["norm_mlp"][0]), mlp_in[0], mlp_out[0], mlp_saved0, 0, nl, mlp_lo)
    tok = grads_done({"mlp_w_in": (0, mlp_lo[0]), "mlp_w_out": (0, mlp_lo[1])})
    (dz0,), _ = _rowwise_vjp(_f_glu, [z0], [], [gx], [BF16], after=tok, tr=TR, name="s5_glu_bwd")
    grads["ssm_w_glu"] = _mm(yb, dz0, ta=True, tm=512, tn=512, tk=4096, out_dtypes=(BF16,), out=_grad_cols(D, 2 * Dq), name="s5_dwglu")
    tok = grads_done({"ssm_w_glu": grads["ssm_w_glu"]})
    dypre = _mm(dz0, w_glu, tb=True, epi=lambda acc, yp: (jax.vjp(lambda t: jax.nn.gelu(t), yp)[1](acc)[0],),
                extras=(ypre,), name="s5_dypre")
    dh0, dbre, dbim, dcre, dcim, dabr, dabi, dd = _s5_bwd(
        dypre, h0, sxr, sxi, ser, sei, abr, abi, bre_b, bim_b, cre_b, cim_b, d_skip, B=B, L=S5_CHUNK, name="s5_bwd")
    s5_grads = s5_vjp((dabr, dabi, dbre, dbim, dcre, dcim))
    for nm, gv in zip(("ssm_a_re", "ssm_a_im", "ssm_b_re", "ssm_b_im", "ssm_c_re", "ssm_c_im", "ssm_log_dt"), s5_grads):
        grads[nm] = gv[None]
    grads["ssm_d"] = dd
    gx, _, d_norm_mix[0] = _rms_bwd(x, g0, dh0, gx, "rms_mix_bwd_0", after=tok)

    grads["norm_mix"] = jnp.concatenate(d_norm_mix, axis=0)
    grads["norm_mlp"] = jnp.concatenate(d_norm_mlp, axis=0)
    grads["mlp_w_in"], grads["mlp_w_out"] = (mlp_lo[0], mlp_hi[0]), (mlp_lo[1], mlp_hi[1])
    return loss_part, gx.reshape(B, S, D), grads


WEIGHTS = ['norm_mix', 'norm_mlp', 'norm_final', 'ssm_a_re', 'ssm_a_im', 'ssm_b_re', 'ssm_b_im', 'ssm_c_re',
           'ssm_c_im', 'ssm_d', 'ssm_log_dt', 'ssm_w_glu', 'conv_w_pw1', 'conv_b_pw1', 'conv_w_dw', 'conv_b_dw',
           'conv_ln_g', 'conv_ln_b', 'conv_w_pw2', 'conv_b_pw2', 'gmlp_w_in', 'gmlp_ln_g', 'gmlp_ln_b', 'gmlp_w_s',
           'gmlp_b_s', 'gmlp_w_out', 'attn_w_qkv', 'attn_w_o', 'mlp_w_in', 'mlp_w_out']
BIG_AXIS = {'ssm_w_glu': -1, 'conv_w_pw1': -1, 'conv_w_pw2': -2, 'gmlp_w_in': -1, 'gmlp_w_out': -2,
            'attn_w_qkv': -1, 'attn_w_o': -1, 'mlp_w_in': -1, 'mlp_w_out': -2}
BIG = list(BIG_AXIS)
LAYER_MIXER_WEIGHTS = (('ssm_w_glu',), ('conv_w_pw1', 'conv_w_pw2'), ('gmlp_w_in', 'gmlp_w_out'), ('attn_w_qkv', 'attn_w_o'))
SMALL_SHARDED = ['conv_b_pw1', 'conv_w_dw', 'conv_b_dw', 'conv_ln_g', 'conv_ln_b', 'conv_b_pw2', 'gmlp_ln_g', 'gmlp_ln_b']
SMALL_REPL = [n for n in WEIGHTS if n not in BIG_AXIS and n not in SMALL_SHARDED]
SMALL = SMALL_REPL + SMALL_SHARDED
LANES = 128
FLAT_COLS = 1024


def _pack(arrs, cols, row_mult):
    flat = jnp.concatenate([a.reshape(-1) for a in arrs])
    per = cols * row_mult
    n = -(-flat.shape[0] // per) * per
    return jnp.pad(flat, (0, n - flat.shape[0])).reshape(n // cols, cols)


def _unpack(flat2d, shapes):
    flat = flat2d.reshape(-1)
    out, off = [], 0
    for s in shapes:
        n = int(np.prod(s))
        out.append(flat[off:off + n].reshape(s))
        off += n
    return out


def _as_halves(shard):
    if shard.shape[0] == 1:
        shard = shard[0]
    return shard.reshape((2, shard.shape[0] // 2) + shard.shape[1:])


def _stored_weight(name, arr):
    kind = "cols" if BIG_AXIS[name] == -1 else "rows"
    if arr.shape[1] > 1:
        return [_Stored(arr, kind, lead=(li,)) for li in range(arr.shape[1])]
    arr = arr[:, 0]
    if kind == "rows":
        return arr.reshape(-1, arr.shape[-1])
    return _Stored(arr, kind)


def kernel(x, norm_mix, norm_mlp, norm_final, ssm_a_re, ssm_a_im, ssm_b_re, ssm_b_im, ssm_c_re, ssm_c_im, ssm_d, ssm_log_dt, ssm_w_glu, conv_w_pw1, conv_b_pw1, conv_w_dw, conv_b_dw, conv_ln_g, conv_ln_b, conv_w_pw2, conv_b_pw2, gmlp_w_in, gmlp_ln_g, gmlp_ln_b, gmlp_w_s, gmlp_b_s, gmlp_w_out, attn_w_qkv, attn_w_o, mlp_w_in, mlp_w_out, loss_target, m_norm_mix, m_norm_mlp, m_norm_final, m_ssm_a_re, m_ssm_a_im, m_ssm_b_re, m_ssm_b_im, m_ssm_c_re, m_ssm_c_im, m_ssm_d, m_ssm_log_dt, m_ssm_w_glu, m_conv_w_pw1, m_conv_b_pw1, m_conv_w_dw, m_conv_b_dw, m_conv_ln_g, m_conv_ln_b, m_conv_w_pw2, m_conv_b_pw2, m_gmlp_w_in, m_gmlp_ln_g, m_gmlp_ln_b, m_gmlp_w_s, m_gmlp_b_s, m_gmlp_w_out, m_attn_w_qkv, m_attn_w_o, m_mlp_w_in, m_mlp_w_out, v_norm_mix, v_norm_mlp, v_norm_final, v_ssm_a_re, v_ssm_a_im, v_ssm_b_re, v_ssm_b_im, v_ssm_c_re, v_ssm_c_im, v_ssm_d, v_ssm_log_dt, v_ssm_w_glu, v_conv_w_pw1, v_conv_b_pw1, v_conv_w_dw, v_conv_b_dw, v_conv_ln_g, v_conv_ln_b, v_conv_w_pw2, v_conv_b_pw2, v_gmlp_w_in, v_gmlp_ln_g, v_gmlp_ln_b, v_gmlp_w_s, v_gmlp_b_s, v_gmlp_w_out, v_attn_w_qkv, v_attn_w_o, v_mlp_w_in, v_mlp_w_out):
    args = dict(locals())
    w = {n: args[n] for n in WEIGHTS}
    m = {n: args["m_" + n] for n in WEIGHTS}
    v = {n: args["v_" + n] for n in WEIGHTS}
    chip = 2 * lax.axis_index("x") + lax.axis_index("y")
    core = lax.axis_index("c")

    big_shapes = [w[n].shape for n in BIG]
    started, token = [], None
    for li, mixer in enumerate(LAYER_MIXER_WEIGHTS):
        names = list(mixer) + ["mlp_w_in", "mlp_w_out"]
        shards = [w[n][0] for n in mixer] + [w["mlp_w_in"][li], w["mlp_w_out"][li]]
        halves = [s.astype(BF16).reshape((2, s.shape[0] // 2) + s.shape[1:]) for s in shards]
        send_sems, recv_sems, halves, lands, token = _gather_start(halves, token, name=f"gather_start_{li}")
        started.append((names, (send_sems, recv_sems, halves, lands)))

    def layer_weights(li, after):
        names, st = started[li]
        halves, lands = _gather_wait(st, after, name=f"gather_wait_{li}")
        out = {}
        for n, h, arr in zip(names, halves, lands):
            arr = lax.dynamic_update_index_in_dim(arr, h, chip, axis=0)
            arr = arr.reshape((N_CHIPS, arr.shape[1] * arr.shape[2]) + arr.shape[3:])
            if BIG_AXIS[n] == -1:
                out[n] = _Stored(arr, "cols")
                if n in ("attn_w_qkv", "attn_w_o"):
                    out[n + "_plain"] = jnp.swapaxes(arr, 0, 1).reshape(arr.shape[1], -1)
            else:
                out[n] = arr.reshape(-1, arr.shape[-1])
        return out

    p = {}
    sm_shapes = [w[n].shape for n in SMALL_SHARDED]
    sflat = _pack([w[n] for n in SMALL_SHARDED], LANES, 8)
    rs = sflat.shape[0]
    sall = _allgather8(sflat, name="gather_small").reshape(8, rs, LANES)
    per_chip = [_unpack(sall[2 * k], sm_shapes) for k in range(N_CHIPS)]
    for i, n in enumerate(SMALL_SHARDED):
        p[n] = jnp.concatenate([per_chip[k][i] for k in range(N_CHIPS)], axis=-1)
    for n in SMALL_REPL:
        p[n] = w[n]
    p['conv_w_dw'] = p['conv_w_dw'][0]

    in_flight, arrived, n_rounds = [], {}, [0]

    def finish_round(after):
        k, names, plans, st = in_flight.pop(0)
        srcs, lands = _reduce_wait(st, plans, after, name=f"grads_wait_{k}")
        for n, plan, src, land in zip(names, plans, srcs, lands):
            arrived.setdefault(n, []).append((plan, src, land))

    def grads_done(group):
        names, srcs, plans, lands = [], [], [], []
        for n, v in group.items():
            if isinstance(v, tuple):
                src, plan = v[1].reshape(1, N_CHIPS, -1, v[1].shape[-1]), ((0, v[0]),)
                while any(n in rd[1] for rd in in_flight):
                    finish_round(src)
            else:
                src, plan = v.reshape(2, N_CHIPS, -1, v.shape[-1]), ((0, 0), (1, 1))
            land = arrived[n][-1][2] if n in arrived else lax.empty((2 * N_CHIPS,) + src.shape[2:], BF16)
            names.append(n), srcs.append(src), plans.append(plan), lands.append(land)
        st = _reduce_start(srcs, lands, plans, None, name=f"grads_start_{n_rounds[0]}")
        in_flight.append((n_rounds[0], names, plans, st[:4]))
        n_rounds[0] += 1
        return st[4]

    loss_part, grad_x, g = _local_step(x, loss_target, p, layer_weights, token, grads_done)
    loss = lax.psum(loss_part[0, 0], ("x", "y", "c"))

    my_id = 2 * chip + core
    while in_flight:
        finish_round(grad_x)
    totals = []
    for n in BIG:
        own = None
        for plan, src, land in arrived[n]:
            slab = lax.dynamic_index_in_dim(src, chip, axis=1, keepdims=False)
            if len(plan) == 2:
                own = lax.dynamic_index_in_dim(slab, core, axis=0, keepdims=False)
            else:
                own = slab[0] if own is None else jnp.where(core == plan[0][1], slab[0], own)
        totals.append(_sum8(arrived[n][-1][2], own, my_id, name="owner_sum_" + n))
    shared = _share_halves(totals, name="grads_share_halves")
    big_grads = {}
    for n, arr, t in zip(BIG, shared, totals):
        arr = lax.dynamic_update_index_in_dim(arr, t[None], core, axis=0)
        big_grads[n] = arr.reshape(w[n].shape)

    small_full_shapes = [g[n].shape for n in SMALL]
    gs = _pack([g[n] for n in SMALL], LANES, 8)
    rg = gs.shape[0]
    gs_all = _allgather8(gs, name="gather_small_grads").reshape(8, rg, LANES)
    gs_sum = _rowwise(lambda *a: (functools.reduce(lambda s, t: s + t, a),), [gs_all[k] for k in range(8)], [], (F32,),
                      tr=rg, name="small_grads_sum")[0]
    small_grads = dict(zip(SMALL, _unpack(gs_sum, small_full_shapes)))
    for n in SMALL:
        small_grads[n] = small_grads[n].reshape(p_shape_full(w[n], -1 if n in SMALL_SHARDED else None))
    for n in SMALL_SHARDED:
        width = w[n].shape[-1]
        small_grads[n] = lax.dynamic_slice_in_dim(small_grads[n], chip * width, width, axis=-1)

    grad, delta, new_m, new_v = {}, {}, {}, {}
    for n in BIG:
        shape = w[n].shape
        two_d = lambda t: t.reshape(-1, shape[-1])
        grad[n] = big_grads[n]
        d_, m_, v_ = _adamw(two_d(w[n]), two_d(grad[n]), two_d(m[n]), two_d(v[n]), name="adamw_" + n)
        delta[n], new_m[n], new_v[n] = d_.reshape(shape), m_.reshape(shape), v_.reshape(shape)
    for n in SMALL:
        shape = w[n].shape
        two_d = lambda t: t.reshape(-1, shape[-1])
        grad[n] = small_grads[n]
        d_, m_, v_ = _adamw(two_d(w[n]), two_d(grad[n]), two_d(m[n]), two_d(v[n]), name="adamw_" + n)
        delta[n], new_m[n], new_v[n] = d_.reshape(shape), m_.reshape(shape), v_.reshape(shape)

    return (loss, grad_x, *[grad[n] for n in WEIGHTS], *[delta[n] for n in WEIGHTS],
            *[new_m[n] for n in WEIGHTS], *[new_v[n] for n in WEIGHTS])


def p_shape_full(shard, axis):
    s = list(shard.shape)
    if axis is not None:
        s[axis] *= N_CHIPS
    return tuple(s)
```

```python
import functools
import math

import jax
import jax.numpy as jnp
import numpy as np
from jax import lax
from jax.experimental import pallas as pl
from jax.experimental.pallas import tpu as pltpu

F32 = jnp.float32
BF16 = jnp.bfloat16
MESH = pl.DeviceIdType.MESH

EPS = 1e-6
SSM_GROUP = 16
SSM_STATE = 64
CONV_WIDTH = 31
CONV_HALO = 32
GMLP_CHUNK = 128
GMLP_HEADS = 4
ATT_DILS = (1, 4, 16)
ATT_BLK = 128
ATT_HEADS = 8
HEAD_DIM = 64
ATT_W = ATT_HEADS * HEAD_DIM
N_CHIPS = 4
ADAM_LR, ADAM_B1, ADAM_B2, ADAM_EPS, ADAM_WD, ADAM_STEP = 1e-3, 0.9, 0.999, 1e-8, 0.01, 10

VMEM_BYTES_V7X = 64 * 1024 * 1024
VMEM_LIMIT = VMEM_BYTES_V7X - 8 * 1024 * 1024
MASK_VALUE = -1e30
LANE_TILE = 128


def _cp(sem=None):
    return pltpu.CompilerParams(dimension_semantics=sem, vmem_limit_bytes=VMEM_LIMIT)


def _pick_tile(total, target):
    for cand in range(min(target, total) // LANE_TILE * LANE_TILE, 0, -LANE_TILE):
        if total % cand == 0:
            return cand
    return total


class _Stored:
    def __init__(self, arr, kind="plain", lead=()):
        self.arr, self.kind, self.lead = arr, kind, tuple(lead)
        r, c = arr.shape[-2:]
        self.shape = (r, c * N_CHIPS) if kind == "cols" else (r * N_CHIPS, c) if kind == "rows" else (r, c)

    def spec(self, br, bc, rc_of):
        lead, nl = self.lead, len(self.lead)
        if self.kind == "plain":
            return pl.BlockSpec((None,) * nl + (br, bc), lambda i, j, k: (*lead, *rc_of(i, j, k)))
        if self.kind == "cols":
            per = self.arr.shape[-1] // bc
            assert per * bc == self.arr.shape[-1]

            def imap(i, j, k):
                r, c = rc_of(i, j, k)
                return (c // per, *lead, r, c % per)
        else:
            per = self.arr.shape[-2] // br
            assert per * br == self.arr.shape[-2]

            def imap(i, j, k):
                r, c = rc_of(i, j, k)
                return (r // per, *lead, r % per, c)
        return pl.BlockSpec((None,) * (nl + 1) + (br, bc), imap)


class _ColBlocks:
    kind = "colblocks"

    def __init__(self, arr, first, stride, count, width):
        self.arr, self.first, self.stride, self.width = arr, first, stride, width
        self.shape = (arr.shape[0], count * width)

    def spec(self, br, bc, rc_of):
        per = self.width // bc
        assert per * bc == self.width

        def imap(i, j, k):
            r, c = rc_of(i, j, k)
            return (r, (self.first + (c // per) * self.stride) * per + c % per)
        return pl.BlockSpec((br, bc), imap)


def _mm(a, b, *, ta=False, tb=False, out_dtypes=(F32,), tm=1024, tn=1024, tk=1024,
        pro_a=None, pro_b=None, epi=None, extras=(), n_row_sums=0, out=None, after=None, name):
    if ta:
        K, M = a.shape
    else:
        M, K = a.shape
    if not isinstance(b, (_Stored, _ColBlocks)):
        b = _Stored(b)
    N, Kb = b.shape if tb else b.shape[::-1]
    assert K == Kb, (a.shape, b.shape, ta, tb)
    col_unit = b.width if b.kind == "colblocks" else b.arr.shape[-1] if b.kind == "cols" else b.shape[1]
    row_unit = b.arr.shape[-2] if b.kind == "rows" else b.shape[0]
    n_unit, k_unit = (row_unit, col_unit) if tb else (col_unit, row_unit)
    m_unit = M
    if out is not None:
        m_unit, n_unit = out[4], math.gcd(n_unit, out[5])
    tm, tn, tk = _pick_tile(m_unit, tm), _pick_tile(n_unit, tn), _pick_tile(k_unit, tk)
    assert not n_row_sums or tn == N
    nk = K // tk
    a_spec = (pl.BlockSpec((tk, tm), lambda i, j, k: (k, i)) if ta
              else pl.BlockSpec((tm, tk), lambda i, j, k: (i, k)))
    b_spec = b.spec(tn, tk, lambda i, j, k: (j, k)) if tb else b.spec(tk, tn, lambda i, j, k: (k, j))
    ex_specs = []
    for e in extras:
        if e.shape[0] == 1:
            ex_specs.append(pl.BlockSpec((1, tn), lambda i, j, k: (0, j)))
        else:
            assert e.shape == (M, N), (e.shape, M, N)
            ex_specs.append(pl.BlockSpec((tm, tn), lambda i, j, k: (i, j)))
    dims = (((0 if ta else 1,), (1 if tb else 0,)), ((), ()))
    n_ex, n_out = len(extras), len(out_dtypes)
    direct = epi is None and n_out == 1 and out_dtypes[0] == F32
    use_acc = nk > 1 and not direct
    operands, aliases, alias_specs = [a, b.arr, *extras], {}, []
    if after is not None:
        operands.append(after)
        alias_specs.append(pl.BlockSpec(memory_space=pl.ANY))
    if out is None:
        n_tile_out = n_out - n_row_sums
        out_specs = ([pl.BlockSpec((tm, tn), lambda i, j, k: (i, j))] * n_tile_out
                     + [pl.BlockSpec((1, tn), lambda i, j, k: (0, j))] * n_row_sums)
        out_shape = ([jax.ShapeDtypeStruct((M, N), dt) for dt in out_dtypes[:n_tile_out]]
                     + [jax.ShapeDtypeStruct((1, N), dt) for dt in out_dtypes[n_tile_out:]])
    else:
        shape, block_fn, imap_fn, alias = out[:4]
        assert n_out == 1
        out_specs = [pl.BlockSpec(block_fn(tm, tn), imap_fn(tm, tn))]
        out_shape = [jax.ShapeDtypeStruct(shape, out_dtypes[0])]
        if alias is not None:
            operands.append(alias)
            aliases = {len(operands) - 1: 0}
            alias_specs.append(pl.BlockSpec(memory_space=pl.ANY))
    n_in = len(operands)

    def finish(r, ex, outs, first_row_tile):
        res = epi(r, *[e[...] for e in ex]) if epi is not None else (r,)
        n_tile_out = n_out - n_row_sums
        for o, v in zip(outs[:n_tile_out], res):
            o[...] = v.astype(o.dtype)
        for o, v in zip(outs[n_tile_out:], res[n_tile_out:]):
            @pl.when(first_row_tile)
            def _(o=o):
                o[...] = jnp.zeros_like(o)
            o[...] += v

    def body(*refs):
        a_ref, b_ref = refs[:2]
        ex = refs[2:2 + n_ex]
        outs = refs[n_in:n_in + n_out]
        first_row_tile = pl.program_id(0) == 0
        at, bt = a_ref[...], b_ref[...]
        if pro_a is not None:
            at = pro_a(at)
        if pro_b is not None:
            bt = pro_b(bt)
        part = lax.dot_general(at, bt, dims, preferred_element_type=F32)
        if nk == 1:
            finish(part, ex, outs, first_row_tile)
            return
        acc = refs[-1] if use_acc else outs[0]
        k = pl.program_id(2)

        @pl.when(k == 0)
        def _():
            acc[...] = part

        @pl.when(k > 0)
        def _():
            acc[...] += part

        if use_acc:
            @pl.when(k == nk - 1)
            def _():
                finish(acc[...], ex, outs, first_row_tile)

    res = pl.pallas_call(
        body, grid=(M // tm, N // tn, nk),
        in_specs=[a_spec, b_spec] + ex_specs + alias_specs,
        out_specs=out_specs, out_shape=out_shape,
        scratch_shapes=[pltpu.VMEM((tm, tn), F32)] if use_acc else [],
        input_output_aliases=aliases,
        compiler_params=_cp(("arbitrary" if n_row_sums else "parallel", "parallel", "arbitrary")), name=name,
    )(*operands)
    return res[0] if n_out == 1 else res


def _to_bf16(t):
    return t.astype(BF16)


def _pick_rows(total, target):
    for cand in range(min(target, total) // 8 * 8, 0, -8):
        if total % cand == 0:
            return cand
    return total


def _rowwise(f, rows, params, out_dtypes, *, tr, name):
    T = rows[0].shape[0]
    tr = _pick_rows(T, tr)
    nr, npar = len(rows), len(params)
    blk = [jax.ShapeDtypeStruct((tr, r.shape[1]), F32) for r in rows]
    blk += [jax.ShapeDtypeStruct(p.shape, F32) for p in params]
    out_avals = jax.eval_shape(f, *blk)

    def body(*refs):
        res = f(*[r[...].astype(F32) for r in refs[:nr + npar]])
        for o, v in zip(refs[nr + npar:], res):
            o[...] = v.astype(o.dtype)

    out = pl.pallas_call(
        body, grid=(T // tr,),
        in_specs=[pl.BlockSpec((tr, r.shape[1]), lambda i: (i, 0)) for r in rows]
        + [pl.BlockSpec(p.shape, lambda i, nd=p.ndim: (0,) * nd) for p in params],
        out_specs=[pl.BlockSpec((tr, o.shape[1]), lambda i: (i, 0)) for o in out_avals],
        out_shape=[jax.ShapeDtypeStruct((T, o.shape[1]), dt) for o, dt in zip(out_avals, out_dtypes)],
        compiler_params=_cp(("parallel",)), name=name,
    )(*rows, *params)
    return out


def _rowwise_vjp(f, rows, params, cots, drow_dtypes, *, adds=None, after=None, tr, name):
    adds = adds or {}
    T = rows[0].shape[0]
    tr = _pick_rows(T, tr)
    nr, npar, nc = len(rows), len(params), len(cots)
    want, want_dt = [], []
    for i, dt in enumerate(drow_dtypes):
        for one in (dt if isinstance(dt, tuple) else (dt,)):
            if one is not None:
                want.append(i)
                want_dt.append(one)
    add_idx = sorted(set(i for i in want if i in adds))
    add_arrays = [adds[i] for i in add_idx]
    na = len(add_arrays)
    extra = [] if after is None else [after]

    def body(*refs):
        ins = [r[...].astype(F32) for r in refs[:nr + npar]]
        cvals = [r[...].astype(F32) for r in refs[nr + npar:nr + npar + nc]]
        avals = refs[nr + npar + nc:nr + npar + nc + na]
        outs = refs[nr + npar + nc + na + len(extra):]
        _, vjp = jax.vjp(f, *ins)
        grads = vjp(tuple(cvals))
        for o, i in zip(outs[:len(want)], want):
            g = grads[i]
            if i in adds:
                g = g + avals[add_idx.index(i)][...].astype(F32)
            o[...] = g.astype(o.dtype)
        step = pl.program_id(0)
        for o, g in zip(outs[len(want):], grads[nr:]):
            @pl.when(step == 0)
            def _(o=o):
                o[...] = jnp.zeros_like(o)
            o[...] += g

    rspec = lambda r: pl.BlockSpec((tr, r.shape[1]), lambda i: (i, 0))
    pspec = lambda p: pl.BlockSpec(p.shape, lambda i, nd=p.ndim: (0,) * nd)
    out = pl.pallas_call(
        body, grid=(T // tr,),
        in_specs=[rspec(r) for r in rows] + [pspec(p) for p in params] + [rspec(c) for c in cots]
        + [rspec(a) for a in add_arrays] + [pl.BlockSpec(memory_space=pl.ANY)] * len(extra),
        out_specs=[rspec(rows[i]) for i in want] + [pspec(p) for p in params],
        out_shape=[jax.ShapeDtypeStruct(rows[i].shape, dt) for i, dt in zip(want, want_dt)]
        + [jax.ShapeDtypeStruct(p.shape, F32) for p in params],
        compiler_params=_cp(("arbitrary",)), name=name,
    )(*rows, *params, *cots, *add_arrays, *extra)
    return out[:len(want)], out[len(want):]


def _f_rms(x, g):
    return (x * lax.rsqrt(jnp.mean(x * x, axis=-1, keepdims=True) + EPS) * g,)


def _ln(x, g, b):
    mu = jnp.mean(x, axis=-1, keepdims=True)
    var = jnp.mean(jnp.square(x - mu), axis=-1, keepdims=True)
    return (x - mu) * lax.rsqrt(var + EPS) * g + b


def _f_glu(z):
    d = z.shape[1] // 2
    return (z[:, :d] * jax.nn.sigmoid(z[:, d:]),)


def _f_bias_glu(z, b):
    return _f_glu(z + b)


def _f_ln_silu(y, b_dw, g, b):
    return (jax.nn.silu(_ln(y + b_dw, g, b)),)


def _f_gelu_ln(z, g, b):
    d = z.shape[1] // 2
    zz = jax.nn.gelu(z)
    return zz[:, :d], _ln(zz[:, d:], g, b)


def _f_gelu(y):
    return (jax.nn.gelu(y),)


def _f_merge(o0, o1, o2, l0, l1, l2):
    m = jnp.maximum(jnp.maximum(l0, l1), l2)
    e0, e1, e2 = jnp.exp(l0 - m), jnp.exp(l1 - m), jnp.exp(l2 - m)
    s = e0 + e1 + e2
    pair = 2 * HEAD_DIM
    first_head = lax.broadcasted_iota(jnp.int32, (o0.shape[0], pair), 1) < HEAD_DIM
    cols = []
    for hp in range(o0.shape[1] // pair):
        acc = None
        for o, e in ((o0, e0), (o1, e1), (o2, e2)):
            wgt = e / s
            wp = jnp.where(first_head, wgt[:, 2 * hp:2 * hp + 1], wgt[:, 2 * hp + 1:2 * hp + 2])
            term = wp * o[:, hp * pair:(hp + 1) * pair]
            acc = term if acc is None else acc + term
        cols.append(acc)
    return (jnp.concatenate(cols, axis=1),)


def _f_add(a, b):
    return (a + b,)


def _loss_head(x, tgt, g, *, tr, name):
    T, D = x.shape
    tr = min(tr, T)

    def f(xv, gv, tv):
        y = _f_rms(xv, gv)[0]
        return 0.5 * jnp.mean(jnp.square(y - tv), axis=-1, keepdims=True)

    def body(x_ref, t_ref, g_ref, loss_ref, dx_ref, dxb_ref, dg_ref):
        tv = t_ref[...]
        l, vjp = jax.vjp(lambda xv, gv: f(xv, gv, tv), x_ref[...], g_ref[...])
        dx, dg = vjp(jnp.ones_like(l))
        dx_ref[...] = dx
        dxb_ref[...] = dx.astype(BF16)

        @pl.when(pl.program_id(0) == 0)
        def _():
            loss_ref[...] = jnp.zeros_like(loss_ref)
            dg_ref[...] = jnp.zeros_like(dg_ref)

        loss_ref[...] += jnp.sum(l)
        dg_ref[...] += dg

    return pl.pallas_call(
        body, grid=(T // tr,),
        in_specs=[pl.BlockSpec((tr, D), lambda i: (i, 0)), pl.BlockSpec((tr, D), lambda i: (i, 0)),
                  pl.BlockSpec((1, D), lambda i: (0, 0))],
        out_specs=[pl.BlockSpec((1, 128), lambda i: (0, 0)), pl.BlockSpec((tr, D), lambda i: (i, 0)),
                   pl.BlockSpec((tr, D), lambda i: (i, 0)), pl.BlockSpec((1, D), lambda i: (0, 0))],
        out_shape=[jax.ShapeDtypeStruct((1, 128), F32), jax.ShapeDtypeStruct((T, D), F32),
                   jax.ShapeDtypeStruct((T, D), BF16), jax.ShapeDtypeStruct((1, D), F32)],
        compiler_params=_cp(("arbitrary",)), name=name,
    )(x, tgt, g)


def _adamw(w, g, m, v, *, name):
    R, C = w.shape
    tr = _pick_rows(R, max(8, 2 * 1024 * 1024 // (4 * C)))
    c1 = 1.0 - ADAM_B1 ** ADAM_STEP
    c2 = 1.0 - ADAM_B2 ** ADAM_STEP

    def body(w_ref, g_ref, m_ref, v_ref, d_ref, nm_ref, nv_ref):
        gv = g_ref[...]
        nm = ADAM_B1 * m_ref[...] + (1.0 - ADAM_B1) * gv
        nv = ADAM_B2 * v_ref[...] + (1.0 - ADAM_B2) * jnp.square(gv)
        nm_ref[...] = nm
        nv_ref[...] = nv
        d_ref[...] = -ADAM_LR * ((nm / c1) / (jnp.sqrt(nv / c2) + ADAM_EPS) + ADAM_WD * w_ref[...])

    spec = pl.BlockSpec((tr, C), lambda i: (i, 0))
    return pl.pallas_call(
        body, grid=(R // tr,), in_specs=[spec] * 4, out_specs=[spec] * 3,
        out_shape=[jax.ShapeDtypeStruct((R, C), F32)] * 3,
        compiler_params=_cp(("parallel",)), name=name,
    )(w, g, m, v)


def _s5_prep(a_re, a_im, b_re, b_im, c_re, c_im, log_dt):
    G, N = a_re.shape
    P = b_re.shape[2]
    gpb = 128 // P
    nblk = G // gpb
    dt = jnp.exp(log_dt)[:, None]
    mag = jnp.exp(a_re * dt)
    abr, abi = mag * jnp.cos(a_im * dt), mag * jnp.sin(a_im * dt)
    den = a_re * a_re + a_im * a_im
    nr, ni = abr - 1.0, abi
    qr, qi = (nr * a_re + ni * a_im) / den, (ni * a_re - nr * a_im) / den
    bbr = qr[..., None] * b_re - qi[..., None] * b_im
    bbi = qr[..., None] * b_im + qi[..., None] * b_re
    eye = jnp.eye(gpb, dtype=F32)

    def expand_b(t):
        t = t.reshape(nblk, gpb, N, P).transpose(0, 1, 3, 2)
        return (t[:, :, :, None, :] * eye[None, :, None, :, None]).reshape(nblk, gpb * P, gpb * N)

    def expand_c(t):
        t = t.reshape(nblk, gpb, P, N).transpose(0, 1, 3, 2)
        return (t[:, :, :, None, :] * eye[None, :, None, :, None]).reshape(nblk, gpb * N, gpb * P)

    return (abr.reshape(1, G * N), abi.reshape(1, G * N), expand_b(bbr), expand_b(bbi),
            expand_c(c_re), expand_c(c_im))


def _s5_fwd(h, abr, abi, bre, bim, cre, cim, d, *, B, L, name):
    T, D = h.shape
    S = T // B
    L = min(L, S)
    nc = S // L
    nblk, cb, sb = bre.shape
    GN = abr.shape[1]

    def body(h_ref, ar_ref, ai_ref, bre_ref, bim_ref, cre_ref, cim_ref, d_ref,
             y_ref, yb_ref, xr_ref, xi_ref, er_ref, ei_ref, sr, si, car, cai):
        ci = pl.program_id(1)

        @pl.when(ci == 0)
        def _():
            car[...] = jnp.zeros_like(car)
            cai[...] = jnp.zeros_like(cai)

        for j in range(nblk):
            u = h_ref[:, j * cb:(j + 1) * cb]
            sr[:, j * sb:(j + 1) * sb] = jnp.dot(u, bre_ref[j], preferred_element_type=F32)
            si[:, j * sb:(j + 1) * sb] = jnp.dot(u, bim_ref[j], preferred_element_type=F32)
        ar, ai = ar_ref[...], ai_ref[...]

        def step(t, carry):
            pr, pi = carry
            nr = ar * pr - ai * pi + sr[pl.ds(t, 1), :]
            ni = ar * pi + ai * pr + si[pl.ds(t, 1), :]
            sr[pl.ds(t, 1), :] = nr
            si[pl.ds(t, 1), :] = ni
            return nr, ni

        pr, pi = lax.fori_loop(0, L, step, (car[...], cai[...]), unroll=4)
        car[...] = pr
        cai[...] = pi
        er_ref[0] = pr
        ei_ref[0] = pi
        for j in range(nblk):
            xr = sr[:, j * sb:(j + 1) * sb].astype(BF16)
            xi = si[:, j * sb:(j + 1) * sb].astype(BF16)
            xr_ref[:, j * sb:(j + 1) * sb] = xr
            xi_ref[:, j * sb:(j + 1) * sb] = xi
            y = (jnp.dot(xr, cre_ref[j], preferred_element_type=F32)
                 - jnp.dot(xi, cim_ref[j], preferred_element_type=F32))
            u = h_ref[:, j * cb:(j + 1) * cb].astype(F32)
            y = y + d_ref[:, j * cb:(j + 1) * cb] * u
            y_ref[:, j * cb:(j + 1) * cb] = y
            yb_ref[:, j * cb:(j + 1) * cb] = jax.nn.gelu(y).astype(BF16)

    tok = lambda w: pl.BlockSpec((L, w), lambda b, c: (b * nc + c, 0))
    whole = lambda p: pl.BlockSpec(p.shape, lambda b, c, nd=p.ndim: (0,) * nd)
    end = pl.BlockSpec((1, 1, GN), lambda b, c: (b * nc + c, 0, 0))
    return pl.pallas_call(
        body, grid=(B, nc),
        in_specs=[tok(D)] + [whole(p) for p in (abr, abi, bre, bim, cre, cim, d)],
        out_specs=[tok(D), tok(D), tok(GN), tok(GN), end, end],
        out_shape=[jax.ShapeDtypeStruct((T, D), F32), jax.ShapeDtypeStruct((T, D), BF16),
                   jax.ShapeDtypeStruct((T, GN), BF16),
                   jax.ShapeDtypeStruct((T, GN), BF16), jax.ShapeDtypeStruct((B * nc, 1, GN), F32),
                   jax.ShapeDtypeStruct((B * nc, 1, GN), F32)],
        scratch_shapes=[pltpu.VMEM((L, GN), F32), pltpu.VMEM((L, GN), F32),
                        pltpu.VMEM((1, GN), F32), pltpu.VMEM((1, GN), F32)],
        compiler_params=_cp(("arbitrary", "arbitrary")), name=name,
    )(h, abr, abi, bre, bim, cre, cim, d)


def _s5_bwd(dy, h, xr, xi, er, ei, abr, abi, bre, bim, cre, cim, d, *, B, L, name):
    T, D = h.shape
    S = T // B
    L = min(L, S)
    nc = S // L
    nblk, cb, sb = bre.shape
    GN = abr.shape[1]
    dims_nt = (((1,), (1,)), ((), ()))
    dims_tn = (((0,), (0,)), ((), ()))

    def body(dy_ref, h_ref, xr_ref, xi_ref, er_ref, ei_ref, ar_ref, ai_ref, bre_ref, bim_ref,
             cre_ref, cim_ref, d_ref,
             dh_ref, dbre_ref, dbim_ref, dcre_ref, dcim_ref, dar_ref, dai_ref, dd_ref,
             lr, li, car, cai):
        b, cstep = pl.program_id(0), pl.program_id(1)
        ci = nc - 1 - cstep

        @pl.when((b == 0) & (cstep == 0))
        def _():
            for r in (dbre_ref, dbim_ref, dcre_ref, dcim_ref, dar_ref, dai_ref, dd_ref):
                r[...] = jnp.zeros_like(r)

        @pl.when(cstep == 0)
        def _():
            car[...] = jnp.zeros_like(car)
            cai[...] = jnp.zeros_like(cai)

        for j in range(nblk):
            dyj = dy_ref[:, j * cb:(j + 1) * cb].astype(BF16)
            lr[:, j * sb:(j + 1) * sb] = lax.dot_general(dyj, cre_ref[j], dims_nt, preferred_element_type=F32)
            li[:, j * sb:(j + 1) * sb] = -lax.dot_general(dyj, cim_ref[j], dims_nt, preferred_element_type=F32)
        ar, ai = ar_ref[...], ai_ref[...]

        def step(s, carry):
            t = L - 1 - s
            pr, pi = carry
            nr = lr[pl.ds(t, 1), :] + ar * pr + ai * pi
            ni = li[pl.ds(t, 1), :] - ai * pr + ar * pi
            lr[pl.ds(t, 1), :] = nr
            li[pl.ds(t, 1), :] = ni
            return nr, ni

        pr, pi = lax.fori_loop(0, L, step, (car[...], cai[...]), unroll=4)
        car[...] = pr
        cai[...] = pi
        has_prev = (ci > 0).astype(F32)
        first_row = lax.broadcasted_iota(jnp.int32, (L, sb), 0) == 0
        for j in range(nblk):
            cs = slice(j * cb, (j + 1) * cb)
            ss = slice(j * sb, (j + 1) * sb)
            lrj, lij = lr[:, ss], li[:, ss]
            xrj, xij = xr_ref[:, ss], xi_ref[:, ss]
            pr_j = jnp.where(first_row, er_ref[0][:, ss] * has_prev, pltpu.roll(xrj.astype(F32), 1, 0))
            pi_j = jnp.where(first_row, ei_ref[0][:, ss] * has_prev, pltpu.roll(xij.astype(F32), 1, 0))
            dar_ref[:, ss] += jnp.sum(lrj * pr_j + lij * pi_j, axis=0, keepdims=True)
            dai_ref[:, ss] += jnp.sum(lij * pr_j - lrj * pi_j, axis=0, keepdims=True)
            lrb, lib = lrj.astype(BF16), lij.astype(BF16)
            hj = h_ref[:, cs]
            dyf = dy_ref[:, cs]
            dyj = dyf.astype(BF16)
            dbre_ref[j] += lax.dot_general(hj, lrb, dims_tn, preferred_element_type=F32)
            dbim_ref[j] += lax.dot_general(hj, lib, dims_tn, preferred_element_type=F32)
            dcre_ref[j] += lax.dot_general(xrj, dyj, dims_tn, preferred_element_type=F32)
            dcim_ref[j] -= lax.dot_general(xij, dyj, dims_tn, preferred_element_type=F32)
            du = (lax.dot_general(lrb, bre_ref[j], dims_nt, preferred_element_type=F32)
                  + lax.dot_general(lib, bim_ref[j], dims_nt, preferred_element_type=F32))
            dh_ref[:, cs] = du + d_ref[:, cs] * dyf
            dd_ref[:, cs] += jnp.sum(dyf * hj.astype(F32), axis=0, keepdims=True)

    tok = lambda w: pl.BlockSpec((L, w), lambda b, c: (b * nc + nc - 1 - c, 0))
    whole = lambda p: pl.BlockSpec(p.shape, lambda b, c, nd=p.ndim: (0,) * nd)
    prev_end = pl.BlockSpec((1, 1, GN), lambda b, c: (b * nc + jnp.maximum(nc - 2 - c, 0), 0, 0))
    params = (abr, abi, bre, bim, cre, cim, d)
    acc_shapes = [bre.shape, bim.shape, cre.shape, cim.shape, abr.shape, abi.shape, d.shape]
    out = pl.pallas_call(
        body, grid=(B, nc),
        in_specs=[tok(D), tok(D), tok(GN), tok(GN), prev_end, prev_end] + [whole(p) for p in params],
        out_specs=[tok(D)] + [pl.BlockSpec(s, lambda b, c, nd=len(s): (0,) * nd) for s in acc_shapes],
        out_shape=[jax.ShapeDtypeStruct((T, D), F32)] + [jax.ShapeDtypeStruct(s, F32) for s in acc_shapes],
        scratch_shapes=[pltpu.VMEM((L, GN), F32), pltpu.VMEM((L, GN), F32),
                        pltpu.VMEM((1, GN), F32), pltpu.VMEM((1, GN), F32)],
        compiler_params=_cp(("arbitrary", "arbitrary")), name=name,
    )(dy, h, xr, xi, er, ei, *params)
    return out


def _conv_fwd(zp, w, *, R, tc, name):
    B, SP, C = zp.shape
    S = SP - CONV_HALO
    R, tc = min(R, S), min(tc, C)

    def body(z_ref, w_ref, y_ref):
        def chunk(ci, _):
            start = pl.multiple_of(ci * R, 8)
            ze = z_ref[pl.ds(start, R + CONV_HALO), :]
            acc = jnp.zeros((R, tc), F32)
            for m in range(CONV_WIDTH):
                k = CONV_WIDTH - 1 - m
                sh = ze if m == 0 else pltpu.roll(ze, m, 0)
                acc = acc + w_ref[k:k + 1, :] * sh[CONV_HALO:, :]
            y_ref[pl.ds(start, R), :] = acc
            return 0

        lax.fori_loop(0, S // R, chunk, 0)

    return pl.pallas_call(
        body, grid=(B, C // tc),
        in_specs=[pl.BlockSpec((None, SP, tc), lambda b, c: (b, 0, c)),
                  pl.BlockSpec((32, tc), lambda b, c: (0, c))],
        out_specs=pl.BlockSpec((None, S, tc), lambda b, c: (b, 0, c)),
        out_shape=jax.ShapeDtypeStruct((B, S, C), F32),
        compiler_params=_cp(("parallel", "parallel")), name=name,
    )(zp, w)


def _conv_bwd(zp, dyp, w, *, R, tc, name):
    B, SP, C = zp.shape
    S = SP - CONV_HALO
    R, tc = min(R, S), min(tc, C)

    def body(z_ref, dy_ref, w_ref, dz_ref, dw_ref):
        @pl.when(pl.program_id(1) == 0)
        def _():
            dw_ref[...] = jnp.zeros_like(dw_ref)

        def chunk(ci, _):
            start = pl.multiple_of(ci * R, 8)
            zc = z_ref[pl.ds(start + CONV_HALO, R), :]
            de = dy_ref[pl.ds(start, R + CONV_HALO), :]
            acc = jnp.zeros((R, tc), F32)
            for m in range(CONV_WIDTH):
                k = CONV_WIDTH - 1 - m
                ds_ = (de if m == 0 else pltpu.roll(de, R + CONV_HALO - m, 0))[:R, :]
                acc = acc + w_ref[k:k + 1, :] * ds_
                dw_ref[k:k + 1, :] += jnp.sum(ds_ * zc, axis=0, keepdims=True)
            dz_ref[pl.ds(start, R), :] = acc
            return 0

        lax.fori_loop(0, S // R, chunk, 0)

    return pl.pallas_call(
        body, grid=(C // tc, B),
        in_specs=[pl.BlockSpec((None, SP, tc), lambda c, b: (b, 0, c)),
                  pl.BlockSpec((None, SP, tc), lambda c, b: (b, 0, c)),
                  pl.BlockSpec((32, tc), lambda c, b: (0, c))],
        out_specs=[pl.BlockSpec((None, S, tc), lambda c, b: (b, 0, c)),
                   pl.BlockSpec((32, tc), lambda c, b: (0, c))],
        out_shape=[jax.ShapeDtypeStruct((B, S, C), F32), jax.ShapeDtypeStruct((32, C), F32)],
        compiler_params=_cp(("parallel", "arbitrary")), name=name,
    )(zp, dyp, w)


def _gmlp_fwd(u, vn, ws, bcol, *, nck, name):
    T, E = u.shape
    H = ws.shape[0]
    he = E // H
    rows = nck * GMLP_CHUNK
    rows = min(rows, T)
    n_in = rows // GMLP_CHUNK

    def body(u_ref, v_ref, ws_ref, b_ref, o_ref):
        for c in range(n_in):
            rs = slice(c * GMLP_CHUNK, (c + 1) * GMLP_CHUNK)
            for hh in range(H):
                cs = slice(hh * he, (hh + 1) * he)
                v2 = jnp.dot(ws_ref[hh], v_ref[rs, cs].astype(BF16), preferred_element_type=F32)
                v2 = v2 + b_ref[:, hh:hh + 1]
                o_ref[rs, cs] = (u_ref[rs, cs] * v2).astype(o_ref.dtype)

    tok = pl.BlockSpec((rows, E), lambda i: (i, 0))
    return pl.pallas_call(
        body, grid=(T // rows,),
        in_specs=[tok, tok, pl.BlockSpec(ws.shape, lambda i: (0, 0, 0)), pl.BlockSpec(bcol.shape, lambda i: (0, 0))],
        out_specs=tok, out_shape=jax.ShapeDtypeStruct((T, E), BF16),
        compiler_params=_cp(("parallel",)), name=name,
    )(u, vn, ws, bcol)


def _gmlp_bwd(duv, u, vn, ws, bcol, *, nck, name):
    T, E = u.shape
    H = ws.shape[0]
    he = E // H
    rows = min(nck * GMLP_CHUNK, T)
    n_in = rows // GMLP_CHUNK
    dims_nt = (((1,), (1,)), ((), ()))
    dims_tn = (((0,), (0,)), ((), ()))

    def body(g_ref, u_ref, v_ref, ws_ref, b_ref, du_ref, dv_ref, dws_ref, db_ref):
        @pl.when(pl.program_id(0) == 0)
        def _():
            dws_ref[...] = jnp.zeros_like(dws_ref)
            db_ref[...] = jnp.zeros_like(db_ref)

        for c in range(n_in):
            rs = slice(c * GMLP_CHUNK, (c + 1) * GMLP_CHUNK)
            for hh in range(H):
                cs = slice(hh * he, (hh + 1) * he)
                vb = v_ref[rs, cs].astype(BF16)
                v2 = jnp.dot(ws_ref[hh], vb, preferred_element_type=F32) + b_ref[:, hh:hh + 1]
                g = g_ref[rs, cs]
                du_ref[rs, cs] = g * v2
                dv2 = g * u_ref[rs, cs]
                dv2b = dv2.astype(BF16)
                dv_ref[rs, cs] = lax.dot_general(ws_ref[hh], dv2b, dims_tn, preferred_element_type=F32)
                dws_ref[hh] += lax.dot_general(dv2b, vb, dims_nt, preferred_element_type=F32)
                db_ref[:, hh:hh + 1] += jnp.sum(dv2, axis=1, keepdims=True)

    tok = pl.BlockSpec((rows, E), lambda i: (i, 0))
    return pl.pallas_call(
        body, grid=(T // rows,),
        in_specs=[tok, tok, tok, pl.BlockSpec(ws.shape, lambda i: (0, 0, 0)), pl.BlockSpec(bcol.shape, lambda i: (0, 0))],
        out_specs=[tok, tok, pl.BlockSpec(ws.shape, lambda i: (0, 0, 0)), pl.BlockSpec(bcol.shape, lambda i: (0, 0))],
        out_shape=[jax.ShapeDtypeStruct((T, E), F32), jax.ShapeDtypeStruct((T, E), F32),
                   jax.ShapeDtypeStruct(ws.shape, F32), jax.ShapeDtypeStruct(bcol.shape, F32)],
        compiler_params=_cp(("arbitrary",)), name=name,
    )(duv, u, vn, ws, bcol)


PAIRS = ATT_HEADS // 2


def _att_consts():
    ji = lax.broadcasted_iota(jnp.int32, (2 * ATT_BLK, ATT_BLK), 0)
    ii = lax.broadcasted_iota(jnp.int32, (2 * ATT_BLK, ATT_BLK), 1)
    dist = ii + ATT_BLK - ji
    band = (dist >= 0) & (dist <= ATT_BLK)
    cur = ji >= ATT_BLK
    first_head = lax.broadcasted_iota(jnp.int32, (ATT_BLK, 2 * HEAD_DIM), 1) < HEAD_DIM
    return band, cur, first_head


def _both_heads(t, first_head):
    zero = jnp.zeros_like(t)
    return jnp.concatenate([jnp.where(first_head, t, zero), jnp.where(first_head, zero, t)], axis=0)


def _att_specs(nbk, offs, nsteps, rev):
    rows = nbk * ATT_BLK
    step = (lambda i: nsteps - 1 - i) if rev else (lambda i: i)
    qoff, koff, voff = offs
    blk = lambda off: pl.BlockSpec((rows, 2 * HEAD_DIM), lambda hp, i: (step(i), off + hp))
    prev = lambda off: pl.BlockSpec((ATT_BLK, 2 * HEAD_DIM), lambda hp, i: (jnp.maximum(step(i) * nbk - 1, 0), off + hp))
    out = pl.BlockSpec((rows, 2 * HEAD_DIM), lambda hp, i: (step(i), hp))
    stat = pl.BlockSpec((2, nbk, ATT_BLK), lambda hp, i: (hp, step(i), 0))
    return [blk(qoff), blk(koff), prev(koff), blk(voff), prev(voff)], out, stat


def _att_fwd(arr, offs, *, nb, nbk, name):
    T = arr.shape[0]
    nbk = min(nbk, T // ATT_BLK)
    nsteps = T // (nbk * ATT_BLK)
    scale = HEAD_DIM ** -0.5
    dims_nt = (((1,), (1,)), ((), ()))
    dims_tn = (((0,), (0,)), ((), ()))

    def body(q_ref, k_ref, kp_ref, v_ref, vp_ref, o_ref, lse_ref):
        i = pl.program_id(1)
        band, cur, first_head = _att_consts()
        for jj in range(nbk):
            rs = slice(jj * ATT_BLK, (jj + 1) * ATT_BLK)
            ps = slice((jj - 1) * ATT_BLK, jj * ATT_BLK)
            has_prev = ((i * nbk + jj) & (nb - 1)) != 0
            valid = band & (cur | has_prev)
            kk = jnp.concatenate([kp_ref[...] if jj == 0 else k_ref[ps, :], k_ref[rs, :]], axis=0)
            vv = jnp.concatenate([vp_ref[...] if jj == 0 else v_ref[ps, :], v_ref[rs, :]], axis=0)
            q2 = _both_heads(q_ref[rs, :], first_head)
            st = lax.dot_general(kk, q2, dims_nt, preferred_element_type=F32) * scale
            st = jnp.where(jnp.concatenate([valid, valid], axis=1), st, MASK_VALUE)
            m = jnp.max(st, axis=0, keepdims=True)
            p = jnp.exp(st - m)
            l = jnp.sum(p, axis=0, keepdims=True)
            lse = m + jnp.log(l)
            lse_ref[0, jj:jj + 1, :] = lse[:, :ATT_BLK]
            lse_ref[1, jj:jj + 1, :] = lse[:, ATT_BLK:]
            pn = (p / l).astype(BF16)
            o2 = lax.dot_general(pn, vv, dims_tn, preferred_element_type=F32)
            o_ref[rs, :] = jnp.where(first_head, o2[:ATT_BLK], o2[ATT_BLK:])

    ins, out, stat = _att_specs(nbk, offs, nsteps, False)
    return pl.pallas_call(
        body, grid=(PAIRS, nsteps), in_specs=ins, out_specs=[out, stat],
        out_shape=[jax.ShapeDtypeStruct((T, ATT_W), F32), jax.ShapeDtypeStruct((ATT_HEADS, T // ATT_BLK, ATT_BLK), F32)],
        compiler_params=_cp(("parallel", "parallel")), name=name,
    )(arr, arr, arr, arr, arr)


def _att_bwd(arr, offs, do, lse, dlse, *, nb, nbk, name):
    T = arr.shape[0]
    nbk = min(nbk, T // ATT_BLK)
    nsteps = T // (nbk * ATT_BLK)
    scale = HEAD_DIM ** -0.5
    dims_nt = (((1,), (1,)), ((), ()))
    dims_tn = (((0,), (0,)), ((), ()))

    def body(q_ref, k_ref, kp_ref, v_ref, vp_ref, do_ref, lse_ref, dlse_ref, dq_ref, dk_ref, dv_ref, ck, cv):
        step = pl.program_id(1)
        i = nsteps - 1 - step
        band, cur, first_head = _att_consts()

        @pl.when(step == 0)
        def _():
            ck[...] = jnp.zeros_like(ck)
            cv[...] = jnp.zeros_like(cv)

        carry_k, carry_v = ck[...], cv[...]
        for jj in reversed(range(nbk)):
            rs = slice(jj * ATT_BLK, (jj + 1) * ATT_BLK)
            ps = slice((jj - 1) * ATT_BLK, jj * ATT_BLK)
            has_prev = ((i * nbk + jj) & (nb - 1)) != 0
            valid = band & (cur | has_prev)
            kk = jnp.concatenate([kp_ref[...] if jj == 0 else k_ref[ps, :], k_ref[rs, :]], axis=0)
            vv = jnp.concatenate([vp_ref[...] if jj == 0 else v_ref[ps, :], v_ref[rs, :]], axis=0)
            q2 = _both_heads(q_ref[rs, :], first_head)
            do2 = _both_heads(do_ref[rs, :].astype(BF16), first_head)
            lse = jnp.concatenate([lse_ref[0, jj:jj + 1, :], lse_ref[1, jj:jj + 1, :]], axis=1)
            dlse = jnp.concatenate([dlse_ref[0, jj:jj + 1, :], dlse_ref[1, jj:jj + 1, :]], axis=1)
            st = lax.dot_general(kk, q2, dims_nt, preferred_element_type=F32) * scale
            st = jnp.where(jnp.concatenate([valid, valid], axis=1), st, MASK_VALUE)
            p = jnp.exp(st - lse)
            dp = lax.dot_general(vv, do2, dims_nt, preferred_element_type=F32)
            delta = jnp.sum(p * dp, axis=0, keepdims=True)
            dsb = (p * (dp - delta + dlse) * scale).astype(BF16)
            dq2 = lax.dot_general(dsb, kk, dims_tn, preferred_element_type=F32)
            dkk = jnp.dot(dsb, q2, preferred_element_type=F32)
            dvv = jnp.dot(p.astype(BF16), do2, preferred_element_type=F32)
            dq_ref[rs, :] = jnp.where(first_head, dq2[:ATT_BLK], dq2[ATT_BLK:]).astype(dq_ref.dtype)
            dk_ref[rs, :] = (dkk[ATT_BLK:] + carry_k).astype(dk_ref.dtype)
            dv_ref[rs, :] = (dvv[ATT_BLK:] + carry_v).astype(dv_ref.dtype)
            carry_k, carry_v = dkk[:ATT_BLK], dvv[:ATT_BLK]
        ck[...] = carry_k
        cv[...] = carry_v

    ins, out, stat = _att_specs(nbk, offs, nsteps, True)
    return pl.pallas_call(
        body, grid=(PAIRS, nsteps), in_specs=ins + [out, stat, stat], out_specs=[out] * 3,
        out_shape=[jax.ShapeDtypeStruct((T, ATT_W), BF16)] * 3,
        scratch_shapes=[pltpu.VMEM((ATT_BLK, 2 * HEAD_DIM), F32), pltpu.VMEM((ATT_BLK, 2 * HEAD_DIM), F32)],
        compiler_params=_cp(("arbitrary", "arbitrary")), name=name,
    )(arr, arr, arr, arr, arr, do, lse, dlse)


def _deinterleave(t, B, S, dil):
    if dil == 1:
        return t
    return t.reshape((B, S // dil, dil) + t.shape[1:]).swapaxes(1, 2).reshape(t.shape)


def _interleave(t, B, S, dil):
    if dil == 1:
        return t
    return t.reshape((B, dil, S // dil) + t.shape[1:]).swapaxes(1, 2).reshape(t.shape)


def _stats_to_tokens(lse, B, S, dil):
    return _interleave(lse.reshape(lse.shape[0], -1).T, B, S, dil)


def _stats_from_tokens(dl, B, S, dil):
    return _deinterleave(dl, B, S, dil).T.reshape(dl.shape[1], -1, ATT_BLK)


def _mesh_pos():
    return lax.axis_index("x"), lax.axis_index("y"), lax.axis_index("c")


def _allgather8(xs, *, name):
    m_per, n = xs.shape

    def body(x_ref, out_ref, send_sems, recv_sems, local_sem):
        x, y, c = _mesh_pos()
        me, sibling = (x, y, c), (x, y, 1 - c)
        chips = [(1 - x, y), (x, 1 - y), (1 - x, 1 - y)]

        def rows(px, py, pc):
            return out_ref.at[pl.ds((4 * px + 2 * py + pc) * m_per, m_per), :]

        def copy(k, block, to, src=None):
            return pltpu.make_async_remote_copy(
                src_ref=rows(*block) if src is None else src, dst_ref=rows(*block),
                send_sem=send_sems.at[k], recv_sem=recv_sems.at[k], device_id=to, device_id_type=MESH)

        mine = pltpu.make_async_copy(x_ref, rows(*me), local_sem)
        mine.start()
        first = [copy(0, me, sibling, src=x_ref)]
        first += [copy(1 + j, me, (*chip, c), src=x_ref) for j, chip in enumerate(chips)]
        for cp in first:
            cp.start()
        passed = [copy(4 + j, (*chip, c), sibling) for j, chip in enumerate(chips)]
        for j, chip in enumerate(chips):
            copy(1 + j, (*chip, c), me).wait_recv()
            passed[j].start()
        copy(0, sibling, me).wait_recv()
        for j, chip in enumerate(chips):
            copy(4 + j, (*chip, 1 - c), me).wait_recv()
        for cp in first + passed:
            cp.wait_send()
        mine.wait()

    return pl.pallas_call(
        body, out_shape=jax.ShapeDtypeStruct((8 * m_per, n), xs.dtype),
        in_specs=[pl.BlockSpec(memory_space=pltpu.VMEM)], out_specs=pl.BlockSpec(memory_space=pltpu.VMEM),
        scratch_shapes=[pltpu.SemaphoreType.DMA((7,)), pltpu.SemaphoreType.DMA((7,)), pltpu.SemaphoreType.DMA],
        compiler_params=pltpu.CompilerParams(vmem_limit_bytes=VMEM_LIMIT), name=name,
    )(xs)


def _hbm_call(body, arrays, out_shapes, n_sems, *, name):
    any_spec = pl.BlockSpec(memory_space=pl.ANY)
    return pl.pallas_call(
        body, out_shape=out_shapes, in_specs=[any_spec] * len(arrays), out_specs=[any_spec] * len(out_shapes),
        scratch_shapes=[pltpu.SemaphoreType.DMA((n_sems,)), pltpu.SemaphoreType.DMA((n_sems,))], name=name,
    )(*arrays)


def _other_chips(x, y):
    return [(1 - x, y), (x, 1 - y), (1 - x, 1 - y)]


def _allgather_chips(ws, *, name):
    n = len(ws)

    def body(*refs):
        ins, outs, (send_sems, recv_sems) = refs[:n], refs[n:2 * n], refs[2 * n:]
        x, y, c = _mesh_pos()
        chips = _other_chips(x, y)

        def copy(a, k, px, py, half, to, src=None):
            slot = outs[a].at[2 * px + py, half]
            return pltpu.make_async_remote_copy(
                src_ref=slot if src is None else src, dst_ref=slot,
                send_sem=send_sems.at[6 * a + k], recv_sem=recv_sems.at[6 * a + k], device_id=to, device_id_type=MESH)

        first = [copy(a, j, x, y, c, (*chip, c), src=ins[a].at[c]) for a in range(n) for j, chip in enumerate(chips)]
        for cp in first:
            cp.start()
        passed = []
        for j, chip in enumerate(chips):
            for a in range(n):
                copy(a, j, *chip, c, (x, y, c)).wait_recv()
                passed.append(copy(a, 3 + j, *chip, c, (x, y, 1 - c)))
                passed[-1].start()
        for j, chip in enumerate(chips):
            for a in range(n):
                copy(a, 3 + j, *chip, 1 - c, (x, y, c)).wait_recv()
        for cp in first + passed:
            cp.wait_send()

    return _hbm_call(body, ws, [jax.ShapeDtypeStruct((N_CHIPS,) + w.shape, w.dtype) for w in ws], 6 * n, name=name)


def _split_start(srcs, lands, after, issue, n_sems, *, name):
    ns, nl = len(srcs), len(lands)
    hbm, sem = pl.BlockSpec(memory_space=pltpu.HBM), pl.BlockSpec(memory_space=pltpu.SEMAPHORE)
    extra = [] if after is None else [after]

    def body(*refs):
        n_in = ns + nl + len(extra)
        send_sems, recv_sems = refs[n_in], refs[n_in + 1]
        issue(refs[:ns], refs[ns:ns + nl], send_sems, recv_sems)
        refs[-1][...] = jnp.zeros_like(refs[-1])

    arrays = [pltpu.with_memory_space_constraint(a, pltpu.HBM) for a in list(srcs) + list(lands)]
    out = pl.pallas_call(
        body, name=name,
        out_shape=(pltpu.SemaphoreType.DMA((n_sems,)), pltpu.SemaphoreType.DMA((n_sems,)),
                   *[pltpu.HBM(a.shape, a.dtype) for a in arrays], jax.ShapeDtypeStruct((8, 128), F32)),
        in_specs=[hbm] * (ns + nl) + [pl.BlockSpec(memory_space=pl.ANY)] * len(extra),
        out_specs=(sem, sem, *[hbm] * (ns + nl), pl.BlockSpec(memory_space=pltpu.VMEM)),
        input_output_aliases={i: 2 + i for i in range(ns + nl)},
        compiler_params=pltpu.CompilerParams(has_side_effects=pltpu.SideEffectType.DATAFLOW_SIDE_EFFECTING),
    )(*arrays, *extra)
    return out[0], out[1], list(out[2:2 + ns]), list(out[2 + ns:2 + ns + nl]), out[-1]


def _split_wait(send_sems, recv_sems, srcs, lands, after, waits, *, name):
    ns, nl = len(srcs), len(lands)
    hbm, sem = pl.BlockSpec(memory_space=pltpu.HBM), pl.BlockSpec(memory_space=pltpu.SEMAPHORE)

    def body(*refs):
        waits(refs[:ns], refs[ns:ns + nl], refs[ns + nl], refs[ns + nl + 1])

    out = pl.pallas_call(
        body, name=name,
        out_shape=tuple(pltpu.HBM(a.shape, a.dtype) for a in list(srcs) + list(lands)),
        in_specs=[hbm] * (ns + nl) + [sem, sem, pl.BlockSpec(memory_space=pl.ANY)],
        out_specs=tuple([hbm] * (ns + nl)),
        input_output_aliases={i: i for i in range(ns + nl)},
        compiler_params=pltpu.CompilerParams(has_side_effects=pltpu.SideEffectType.DATAFLOW_SIDE_EFFECTING),
    )(*srcs, *lands, send_sems, recv_sems, after)
    return list(out[:ns]), list(out[ns:])


def _gather_start(halves, after, *, name):
    n = len(halves)
    lands = [lax.empty((N_CHIPS,) + h.shape, h.dtype) for h in halves]

    def issue(srcs, dsts, send_sems, recv_sems):
        x, y, c = _mesh_pos()
        me = 2 * x + y
        for a in range(n):
            for j, (px, py) in enumerate(_other_chips(x, y)):
                for cc in range(2):
                    pltpu.make_async_remote_copy(
                        src_ref=srcs[a].at[c], dst_ref=dsts[a].at[me, c],
                        send_sem=send_sems.at[6 * a + 2 * j + cc], recv_sem=recv_sems.at[6 * a + 2 * j + c],
                        device_id=(px, py, cc), device_id_type=MESH).start()

    return _split_start(halves, lands, after, issue, 6 * n, name=name)


def _gather_wait(started, after, *, name):
    send_sems, recv_sems, halves, lands = started
    n = len(halves)

    def waits(srcs, dsts, send_sems, recv_sems):
        x, y, c = _mesh_pos()
        me = 2 * x + y
        for a in range(n):
            for j, (px, py) in enumerate(_other_chips(x, y)):
                for cc in range(2):
                    pltpu.make_async_remote_copy(
                        src_ref=srcs[a].at[cc], dst_ref=dsts[a].at[2 * px + py, cc],
                        send_sem=send_sems.at[6 * a + 2 * j + cc], recv_sem=recv_sems.at[6 * a + 2 * j + cc],
                        device_id=(px, py, cc), device_id_type=MESH).wait_recv()
        for a in range(n):
            for j, (px, py) in enumerate(_other_chips(x, y)):
                for cc in range(2):
                    pltpu.make_async_remote_copy(
                        src_ref=srcs[a].at[c], dst_ref=dsts[a].at[me, c],
                        send_sem=send_sems.at[6 * a + 2 * j + cc], recv_sem=recv_sems.at[6 * a + 2 * j + c],
                        device_id=(px, py, cc), device_id_type=MESH).wait_send()

    return _split_wait(send_sems, recv_sems, halves, lands, after, waits, name=name)


def _reduce_plan_loops(plans, chip, c, fn):
    for a, plan in enumerate(plans):
        for h, cc in plan:
            for k in range(N_CHIPS):
                fn(a, h, k, cc, jnp.logical_or(chip != k, c != cc))


def _reduce_start(srcs, lands, plans, after, *, name):
    def issue(src_refs, land_refs, send_sems, recv_sems):
        x, y, c = _mesh_pos()
        chip = 2 * x + y
        my_id = 2 * chip + c

        def send(a, h, k, cc, is_other):
            @pl.when(is_other)
            def _():
                pltpu.make_async_remote_copy(
                    src_ref=src_refs[a].at[h, k], dst_ref=land_refs[a].at[my_id],
                    send_sem=send_sems.at[8 * a + 2 * k + cc], recv_sem=recv_sems.at[8 * a + my_id],
                    device_id=(k // 2, k % 2, cc), device_id_type=MESH).start()

        _reduce_plan_loops(plans, chip, c, send)

    return _split_start(srcs, lands, after, issue, 8 * len(srcs), name=name)


def _reduce_wait(started, plans, after, *, name):
    send_sems, recv_sems, srcs, lands = started

    def waits(src_refs, land_refs, send_sems, recv_sems):
        x, y, c = _mesh_pos()
        chip = 2 * x + y
        my_id = 2 * chip + c
        for a, plan in enumerate(plans):
            for h, cc in plan:
                for s in range(2 * N_CHIPS):
                    @pl.when(jnp.logical_and(c == cc, my_id != s))
                    def _(a=a, h=h, s=s):
                        pltpu.make_async_remote_copy(
                            src_ref=src_refs[a].at[h, 0], dst_ref=land_refs[a].at[s],
                            send_sem=send_sems.at[8 * a + s], recv_sem=recv_sems.at[8 * a + s],
                            device_id=(s // 4, (s // 2) % 2, s % 2), device_id_type=MESH).wait_recv()

        def sent(a, h, k, cc, is_other):
            @pl.when(is_other)
            def _():
                pltpu.make_async_remote_copy(
                    src_ref=src_refs[a].at[h, k], dst_ref=land_refs[a].at[my_id],
                    send_sem=send_sems.at[8 * a + 2 * k + cc], recv_sem=recv_sems.at[8 * a + my_id],
                    device_id=(k // 2, k % 2, cc), device_id_type=MESH).wait_send()

        _reduce_plan_loops(plans, chip, c, sent)

    return _split_wait(send_sems, recv_sems, srcs, lands, after, waits, name=name)


def _sum8(land, own, my_id, *, name):
    n_src, R, C = land.shape
    tr = _pick_rows(R, max(8, 1024 * 1024 // (2 * C)))

    def body(id_ref, *refs):
        own_ref, o_ref = refs[n_src], refs[n_src + 1]
        me = id_ref[0]
        acc = None
        for s in range(n_src):
            term = jnp.where(me == s, own_ref[...], refs[s][...]).astype(F32)
            acc = term if acc is None else acc + term
        o_ref[...] = acc

    return pl.pallas_call(
        body, out_shape=jax.ShapeDtypeStruct((R, C), F32),
        grid_spec=pltpu.PrefetchScalarGridSpec(
            num_scalar_prefetch=1, grid=(R // tr,),
            in_specs=[pl.BlockSpec((None, tr, C), lambda i, idr, s=s: (s, i, 0)) for s in range(n_src)]
            + [pl.BlockSpec((tr, C), lambda i, idr: (i, 0))],
            out_specs=pl.BlockSpec((tr, C), lambda i, idr: (i, 0))),
        compiler_params=_cp(("parallel",)), name=name,
    )(my_id.reshape(1).astype(jnp.int32), *([land] * n_src), own)


def _swap_halves(gs, *, name):
    n = len(gs)

    def body(*refs):
        ins, outs, (send_sems, recv_sems) = refs[:n], refs[n:2 * n], refs[2 * n:]
        x, y, c = _mesh_pos()
        cps = [pltpu.make_async_remote_copy(
            src_ref=ins[a].at[1 - c], dst_ref=outs[a], send_sem=send_sems.at[a], recv_sem=recv_sems.at[a],
            device_id=(x, y, 1 - c), device_id_type=MESH) for a in range(n)]
        for cp in cps:
            cp.start()
        for cp in cps:
            cp.wait()

    return _hbm_call(body, gs, [jax.ShapeDtypeStruct(g.shape[1:], g.dtype) for g in gs], n, name=name)


def _scatter_chips(ss, *, name):
    n = len(ss)

    def body(*refs):
        ins, outs, (send_sems, recv_sems) = refs[:n], refs[n:2 * n], refs[2 * n:]
        x, y, c = _mesh_pos()
        me = 2 * x + y
        chips = _other_chips(x, y)

        def copy(a, j, px, py):
            return pltpu.make_async_remote_copy(
                src_ref=ins[a].at[2 * px + py], dst_ref=outs[a].at[me],
                send_sem=send_sems.at[3 * a + j], recv_sem=recv_sems.at[3 * a + j],
                device_id=(px, py, c), device_id_type=MESH)

        def arrival(a, j, px, py):
            return pltpu.make_async_remote_copy(
                src_ref=ins[a].at[me], dst_ref=outs[a].at[2 * px + py],
                send_sem=send_sems.at[3 * a + j], recv_sem=recv_sems.at[3 * a + j],
                device_id=(px, py, c), device_id_type=MESH)

        cps = [copy(a, j, *chip) for a in range(n) for j, chip in enumerate(chips)]
        for cp in cps:
            cp.start()
        for a in range(n):
            for j, chip in enumerate(chips):
                arrival(a, j, *chip).wait_recv()
        for cp in cps:
            cp.wait_send()

    return _hbm_call(body, ss, [jax.ShapeDtypeStruct(s.shape, s.dtype) for s in ss], 3 * n, name=name)


def _share_halves(ts, *, name):
    n = len(ts)

    def body(*refs):
        ins, outs, (send_sems, recv_sems) = refs[:n], refs[n:2 * n], refs[2 * n:]
        x, y, c = _mesh_pos()
        cps = [pltpu.make_async_remote_copy(
            src_ref=ins[a], dst_ref=outs[a].at[c], send_sem=send_sems.at[a], recv_sem=recv_sems.at[a],
            device_id=(x, y, 1 - c), device_id_type=MESH) for a in range(n)]
        for cp in cps:
            cp.start()
        for a in range(n):
            pltpu.make_async_remote_copy(
                src_ref=ins[a], dst_ref=outs[a].at[1 - c], send_sem=send_sems.at[a], recv_sem=recv_sems.at[a],
                device_id=(x, y, 1 - c), device_id_type=MESH).wait_recv()
        for cp in cps:
            cp.wait_send()

    return _hbm_call(body, ts, [jax.ShapeDtypeStruct((2,) + t.shape, t.dtype) for t in ts], n, name=name)


def _half_add(g, ra, core, *, name):
    _, R, C = g.shape
    tr = _pick_rows(R, max(8, 2 * 1024 * 1024 // (4 * C)))

    def body(core_ref, g_ref, ra_ref, o_ref):
        o_ref[...] = (g_ref[...] + ra_ref[...]).astype(o_ref.dtype)

    return pl.pallas_call(
        body, out_shape=jax.ShapeDtypeStruct((R, C), BF16),
        grid_spec=pltpu.PrefetchScalarGridSpec(
            num_scalar_prefetch=1, grid=(R // tr,),
            in_specs=[pl.BlockSpec((None, tr, C), lambda i, cr: (cr[0], i, 0)),
                      pl.BlockSpec((tr, C), lambda i, cr: (i, 0))],
            out_specs=pl.BlockSpec((tr, C), lambda i, cr: (i, 0))),
        compiler_params=_cp(("parallel",)), name=name,
    )(core.reshape(1).astype(jnp.int32), g, ra)


def _sum4(rb, *, name):
    _, R, C = rb.shape
    tr = _pick_rows(R, max(8, 2 * 1024 * 1024 // (4 * C)))

    def body(r0, r1, r2, r3, o_ref):
        f = lambda r: r[...].astype(F32)
        o_ref[...] = ((f(r0) + f(r1)) + f(r2)) + f(r3)

    return pl.pallas_call(
        body, out_shape=jax.ShapeDtypeStruct((R, C), F32), grid=(R // tr,),
        in_specs=[pl.BlockSpec((None, tr, C), lambda i, k=k: (k, i, 0)) for k in range(N_CHIPS)],
        out_specs=pl.BlockSpec((tr, C), lambda i: (i, 0)),
        compiler_params=_cp(("parallel",)), name=name,
    )(rb, rb, rb, rb)


TR = 256
S5_CHUNK = 256


def _rms_fwd(x, g, name):
    return _rowwise(_f_rms, [x], [g], (BF16,), tr=TR, name=name)[0]


def _rms_bwd_epi(dh, x, g, gx):
    r = lax.rsqrt(jnp.mean(x * x, axis=-1, keepdims=True) + EPS)
    xr = x * r
    t = dh * g
    dx = r * (t - xr * jnp.mean(t * xr, axis=-1, keepdims=True)) + gx
    return dx, dx, jnp.sum(dh * xr, axis=0, keepdims=True)


def _mm_rms_bwd(a, w, x, g, gx, *, after=None, name, **kw):
    return _mm(a, w, tb=True, epi=_rms_bwd_epi, extras=(x, g, gx), out_dtypes=(F32, BF16, F32), n_row_sums=1,
               tm=1024, tn=x.shape[1], after=after, name=name, **kw)


def _rms_bwd(x, g, dh, gx, name, after=None):
    (dx, dxb), (dg,) = _rowwise_vjp(_f_rms, [x], [g], [dh], [(F32, BF16)], adds={0: gx}, after=after, tr=TR,
                                    name=name)
    return dx, dxb, dg


def _grad_cols(M, Nq):
    def imap(tm, tn):
        hp, per = (M // 2) // tm, Nq // tn
        assert hp * tm * 2 == M and per * tn == Nq, (M, Nq, tm, tn)
        return lambda i, j, k: (i // hp, j // per, i % hp, j % per)
    return (2, N_CHIPS, M // 2, Nq), lambda tm, tn: (None, None, tm, tn), imap, None, M // 2, Nq


def _grad_rows(Mq, N):
    def imap(tm, tn):
        po, hp = Mq // tm, (Mq // 2) // tm
        assert hp * tm * 2 == Mq, (Mq, tm)
        return lambda i, j, k: ((i % po) // hp, i // po, (i % po) % hp, j)
    return (2, N_CHIPS, Mq // 2, N), lambda tm, tn: (None, None, tm, tn), imap, None, Mq // 2, N


def _grad_layer_cols(slot, lh, M, Nq, buf):
    def imap(tm, tn):
        per = Nq // tn
        return lambda i, j, k: (j // per, slot, i, j % per)
    return (N_CHIPS, lh, M, Nq), lambda tm, tn: (None, None, tm, tn), imap, buf, M, Nq


def _grad_layer_rows(slot, lh, Mq, N, buf):
    def imap(tm, tn):
        po = Mq // tm
        return lambda i, j, k: (i // po, slot, i % po, j)
    return (N_CHIPS, lh, Mq, N), lambda tm, tn: (None, None, tm, tn), imap, buf, Mq, N


def _add_then_rms(acc, res, g):
    xo = acc + res
    return xo, _f_rms(xo, g)[0]


def _mlp_fwd(x, h2, w_in, w_out, g_next, li):
    r = _mm(h2, w_in, out_dtypes=(BF16,), epi=lambda acc: (jnp.maximum(acc, 0.0),), tm=4096, name=f"mlp_in_{li}")
    if g_next is None:
        x_out, h_next = _mm(r, w_out, pro_a=lambda t: t * t, epi=lambda acc, res: (acc + res,), extras=(x,),
                            name=f"mlp_out_{li}"), None
    else:
        x_out, h_next = _mm(r, w_out, pro_a=lambda t: t * t, epi=_add_then_rms, extras=(x, g_next),
                            out_dtypes=(F32, BF16), tn=x.shape[1], tk=2048, name=f"mlp_out_{li}")
    return x_out, h_next, (h2, r)


def _mlp_bwd(gx, gxb, x, g, w_in, w_out, saved, li, nl, bufs):
    h2, r = saved
    D, F = w_in.shape
    lh = nl // 2
    da = _mm(gxb, w_out, tb=True, out_dtypes=(BF16,),
             epi=lambda acc, rt: (acc * 2.0 * rt.astype(F32),), extras=(r,), tm=2048, name=f"mlp_dact_{li}")
    buf_in, buf_out = bufs if bufs is not None else (None, None)
    d_w_out = _mm(r, gxb, ta=True, pro_a=lambda t: t * t, tm=512, tn=1024, tk=4096, out_dtypes=(BF16,),
                  out=_grad_layer_rows(li % lh, lh, F // N_CHIPS, D, buf_out), name=f"mlp_dwout_{li}")
    d_w_in = _mm(h2, da, ta=True, tm=1024, tn=1024, tk=4096, out_dtypes=(BF16,),
                 out=_grad_layer_cols(li % lh, lh, D, F // N_CHIPS, buf_in), name=f"mlp_dwin_{li}")
    gx_mid, gxb_mid, dg = _mm_rms_bwd(da, w_in, x, g, gx, name=f"mlp_dh_{li}")
    return gx_mid, gxb_mid, dg, (d_w_in, d_w_out)


def _local_step(x3, tgt3, p, layer_weights, token=None, grads_done=lambda group: None):
    B, S, D = x3.shape
    T = B * S
    x = x3.reshape(T, D)
    grads = {}
    row = lambda v: v.reshape(1, -1)
    p = dict(p)
    nl = p["norm_mlp"].shape[0]
    mlp_in, mlp_out = [None] * nl, [None] * nl

    def fetch(li, after):
        wl = dict(layer_weights(li, after))
        mlp_in[li], mlp_out[li] = wl.pop("mlp_w_in"), wl.pop("mlp_w_out")
        p.update(wl)

    g0 = row(p["norm_mix"][0])
    if token is not None:
        g0 = g0 + token[:1, :1]
    h0 = _rms_fwd(x, g0, "rms_mix_0")
    s5_args = (p["ssm_a_re"][0], p["ssm_a_im"][0], p["ssm_b_re"][0], p["ssm_b_im"][0],
               p["ssm_c_re"][0], p["ssm_c_im"][0], p["ssm_log_dt"][0])
    s5_exp, s5_vjp = jax.vjp(_s5_prep, *s5_args)
    abr, abi, bre, bim, cre, cim = s5_exp
    bre_b, bim_b, cre_b, cim_b = (t.astype(BF16) for t in (bre, bim, cre, cim))
    d_skip = p["ssm_d"]
    ypre, yb, sxr, sxi, ser, sei = _s5_fwd(h0, abr, abi, bre_b, bim_b, cre_b, cim_b, d_skip, B=B, L=S5_CHUNK,
                                           name="s5_fwd")
    fetch(0, yb)
    w_glu = p["ssm_w_glu"]
    z0 = _mm(yb, w_glu, tm=2048, name="s5_glu_mm")
    gm = [row(p["norm_mlp"][i]) for i in range(nl)]
    g1, g2, g3 = (row(p["norm_mix"][i]) for i in range(1, nl))
    x_mid0, hm0 = _rowwise(lambda z, xr, g: _add_then_rms(_f_glu(z)[0], xr, g), [z0, x], [gm[0]], (F32, BF16),
                           tr=TR, name="s5_glu")
    x1, h1, mlp_saved0 = _mlp_fwd(x_mid0, hm0, mlp_in[0], mlp_out[0], g1, 0)

    fetch(1, h1)
    z1 = _mm(h1, p["conv_w_pw1"], tm=2048, name="conv_pw1")
    zg = _rowwise(_f_bias_glu, [z1], [p["conv_b_pw1"]], (F32,), tr=TR, name="conv_glu")[0]
    zp = jnp.pad(zg.reshape(B, S, D), ((0, 0), (CONV_HALO, 0), (0, 0)))
    w_dw = jnp.pad(p["conv_w_dw"], ((0, 32 - CONV_WIDTH), (0, 0)))
    yc = _conv_fwd(zp, w_dw, R=256, tc=128, name="conv_dw").reshape(T, D)
    ln_par = [p["conv_b_dw"], p["conv_ln_g"], p["conv_ln_b"]]
    qc = _rowwise(_f_ln_silu, [yc], ln_par, (BF16,), tr=TR, name="conv_ln_silu")[0]
    x_mid1, hm1 = _mm(qc, p["conv_w_pw2"], epi=lambda acc, bias, res, g: _add_then_rms(acc + bias, res, g),
                      extras=(p["conv_b_pw2"], x1, gm[1]), out_dtypes=(F32, BF16), tn=D, name="conv_pw2")
    x2, h2, mlp_saved1 = _mlp_fwd(x_mid1, hm1, mlp_in[1], mlp_out[1], g2, 1)

    fetch(2, h2)
    z2 = _mm(h2, p["gmlp_w_in"], tm=2048, name="gmlp_in")
    gl_par = [p["gmlp_ln_g"], p["gmlp_ln_b"]]
    gu, gvn = _rowwise(_f_gelu_ln, [z2], gl_par, (F32, F32), tr=TR, name="gmlp_gelu_ln")
    causal = jnp.tril(jnp.ones((GMLP_CHUNK, GMLP_CHUNK), dtype=bool))
    ws_b = jnp.where(causal[None], p["gmlp_w_s"][0], 0.0).astype(BF16)
    bcol = jnp.pad(p["gmlp_b_s"][0].T, ((0, 0), (0, 128 - GMLP_HEADS)))
    uv = _gmlp_fwd(gu, gvn, ws_b, bcol, nck=4, name="gmlp_spatial")
    x_mid2, hm2 = _mm(uv, p["gmlp_w_out"], epi=_add_then_rms, extras=(x2, gm[2]), out_dtypes=(F32, BF16), tn=D,
                      name="gmlp_out")
    x3_, h3, mlp_saved2 = _mlp_fwd(x_mid2, hm2, mlp_in[2], mlp_out[2], g3, 2)

    fetch(3, h3)
    ng = len(ATT_DILS)
    att_in, o_tok, l_tok, lses = [], [], [], []
    offs = (0, PAIRS, 2 * PAIRS)
    for gi, dil in enumerate(ATT_DILS):
        w_g = _ColBlocks(p["attn_w_qkv_plain"], gi, ng, 3, ATT_W)
        arr = _mm(h3, w_g, out_dtypes=(BF16,), tm=2048, tn=ATT_W, name=f"attn_qkv_{gi}")
        arr = _deinterleave(arr, B, S, dil)
        att_in.append((arr, offs))
        og, lg = _att_fwd(arr, offs, nb=S // dil // ATT_BLK, nbk=8, name=f"attn_fwd_{gi}")
        lses.append(lg)
        o_tok.append(_interleave(og, B, S, dil))
        l_tok.append(_stats_to_tokens(lg, B, S, dil))
    merged2 = _rowwise(_f_merge, o_tok + l_tok, [], (BF16,), tr=TR, name="attn_merge")[0]
    x_mid3, hm3 = _mm(merged2, p["attn_w_o_plain"], epi=_add_then_rms, extras=(x3_, gm[3]), out_dtypes=(F32, BF16),
                      tn=D, name="attn_out")
    x4, _, mlp_saved3 = _mlp_fwd(x_mid3, hm3, mlp_in[3], mlp_out[3], None, 3)

    loss_part, gx, gxb, dgf = _loss_head(x4, tgt3.reshape(T, D), row(p["norm_final"]), tr=TR, name="loss_head")
    grads["norm_final"] = dgf.reshape(-1)
    d_norm_mix, d_norm_mlp = [None] * 4, [None] * 4
    Dq = D // N_CHIPS

    gx, gxb, d_norm_mlp[3], mlp_hi = _mlp_bwd(
        gx, gxb, x_mid3, row(p["norm_mlp"][3]), mlp_in[3], mlp_out[3], mlp_saved3, 3, nl, None)
    dmerged = _mm(gxb, p["attn_w_o"], tb=True, name="attn_dmerged")
    grads["attn_w_o"] = _mm(merged2, gxb, ta=True, tm=256, tn=256, tk=4096, out_dtypes=(BF16,), out=_grad_cols(ATT_W, Dq), name="attn_dwo")
    dml, _ = _rowwise_vjp(_f_merge, o_tok + l_tok, [], [dmerged], [F32] * 6, tr=TR, name="attn_merge_bwd")
    pieces = [[None] * ng for _ in range(3)]
    for gi, dil in enumerate(ATT_DILS):
        arr, offs = att_in[gi]
        dqkv_g = _att_bwd(arr, offs, _deinterleave(dml[gi], B, S, dil), lses[gi],
                          _stats_from_tokens(dml[ng + gi], B, S, dil), nb=S // dil // ATT_BLK, nbk=8,
                          name=f"attn_bwd_{gi}")
        for i in range(3):
            pieces[i][gi] = _interleave(dqkv_g[i], B, S, dil)
    dqkv = jnp.concatenate([pieces[i][gi] for i in range(3) for gi in range(ng)], axis=1)
    qkv_w = 3 * ng * ATT_W
    grads["attn_w_qkv"] = _mm(h3, dqkv, ta=True, tm=512, tn=1152, tk=4096, out_dtypes=(BF16,), out=_grad_cols(D, qkv_w // N_CHIPS),
                              name="attn_dwqkv")
    tok = grads_done({n: grads[n] for n in ("attn_w_qkv", "attn_w_o")})
    gx, gxb, d_norm_mix[3] = _mm_rms_bwd(dqkv, p["attn_w_qkv"], x3_, g3, gx, tk=1152, after=tok, name="attn_dh")

    gx, gxb, d_norm_mlp[2], mlp_hi = _mlp_bwd(
        gx, gxb, x_mid2, row(p["norm_mlp"][2]), mlp_in[2], mlp_out[2], mlp_saved2, 2, nl, mlp_hi)
    tok = grads_done({"mlp_w_in": (1, mlp_hi[0]), "mlp_w_out": (1, mlp_hi[1])})
    duv = _mm(gxb, p["gmlp_w_out"], tb=True, after=tok, name="gmlp_duv")
    grads["gmlp_w_out"] = _mm(uv, gxb, ta=True, tm=128, tn=1024, tk=4096, out_dtypes=(BF16,), out=_grad_rows(Dq, D), name="gmlp_dwout")
    du, dvn, dws, dbcol = _gmlp_bwd(duv, gu, gvn, ws_b, bcol, nck=4, name="gmlp_spatial_bwd")
    grads["gmlp_w_s"] = jnp.where(causal[None], dws, 0.0)[None]
    grads["gmlp_b_s"] = dbcol[:, :GMLP_HEADS].T[None]
    (dz2,), (dlg, dlb_) = _rowwise_vjp(_f_gelu_ln, [z2], gl_par, [du, dvn], [BF16], tr=TR, name="gmlp_gelu_ln_bwd")
    grads["gmlp_ln_g"], grads["gmlp_ln_b"] = dlg, dlb_
    grads["gmlp_w_in"] = _mm(h2, dz2, ta=True, tm=512, tn=512, tk=4096, out_dtypes=(BF16,), out=_grad_cols(D, 2 * Dq), name="gmlp_dwin")
    tok = grads_done({n: grads[n] for n in ("gmlp_w_in", "gmlp_w_out")})
    gx, gxb, d_norm_mix[2] = _mm_rms_bwd(dz2, p["gmlp_w_in"], x2, g2, gx, after=tok, name="gmlp_dh")

    gx, gxb, d_norm_mlp[1], mlp_lo = _mlp_bwd(
        gx, gxb, x_mid1, row(p["norm_mlp"][1]), mlp_in[1], mlp_out[1], mlp_saved1, 1, nl, None)
    dqc = _mm(gxb, p["conv_w_pw2"], tb=True, name="conv_dq")
    grads["conv_w_pw2"] = _mm(qc, gxb, ta=True, tm=128, tn=1024, tk=4096, out_dtypes=(BF16,), out=_grad_rows(Dq, D), name="conv_dwpw2")
    _, (db2,) = _rowwise_vjp(lambda t, b: (t + b,), [gx], [p["conv_b_pw2"]], [gx], [None], tr=TR, name="conv_db2")
    grads["conv_b_pw2"] = db2
    (dyc,), (dbdw, dcg, dcb) = _rowwise_vjp(_f_ln_silu, [yc], ln_par, [dqc], [F32], tr=TR, name="conv_ln_silu_bwd")
    grads["conv_b_dw"], grads["conv_ln_g"], grads["conv_ln_b"] = dbdw, dcg, dcb
    dyp = jnp.pad(dyc.reshape(B, S, D), ((0, 0), (0, CONV_HALO), (0, 0)))
    dzg, dwdw = _conv_bwd(zp, dyp, w_dw, R=256, tc=128, name="conv_dw_bwd")
    grads["conv_w_dw"] = dwdw[:CONV_WIDTH][None]
    (dz1,), (db1,) = _rowwise_vjp(_f_bias_glu, [z1], [p["conv_b_pw1"]], [dzg.reshape(T, D)], [BF16], tr=TR,
                                  name="conv_glu_bwd")
    grads["conv_b_pw1"] = db1
    grads["conv_w_pw1"] = _mm(h1, dz1, ta=True, tm=512, tn=512, tk=4096, out_dtypes=(BF16,), out=_grad_cols(D, 2 * Dq), name="conv_dwpw1")
    tok = grads_done({n: grads[n] for n in ("conv_w_pw1", "conv_w_pw2")})
    gx, gxb, d_norm_mix[1] = _mm_rms_bwd(dz1, p["conv_w_pw1"], x1, g1, gx, after=tok, name="conv_dh")

    gx, gxb, d_norm_mlp[0], mlp_lo = _mlp_bwd(
        gx, gxb, x_mid0, row(p["norm_mlp"][0]), mlp_in[0], mlp_out[0], mlp_saved0, 0, nl, mlp_lo)
    tok = grads_done({"mlp_w_in": (0, mlp_lo[0]), "mlp_w_out": (0, mlp_lo[1])})
    (dz0,), _ = _rowwise_vjp(_f_glu, [z0], [], [gx], [BF16], after=tok, tr=TR, name="s5_glu_bwd")
    grads["ssm_w_glu"] = _mm(yb, dz0, ta=True, tm=512, tn=512, tk=4096, out_dtypes=(BF16,), out=_grad_cols(D, 2 * Dq), name="s5_dwglu")
    tok = grads_done({"ssm_w_glu": grads["ssm_w_glu"]})
    dypre = _mm(dz0, w_glu, tb=True, epi=lambda acc, yp: (jax.vjp(lambda t: jax.nn.gelu(t), yp)[1](acc)[0],),
                extras=(ypre,), name="s5_dypre")
    dh0, dbre, dbim, dcre, dcim, dabr, dabi, dd = _s5_bwd(
        dypre, h0, sxr, sxi, ser, sei, abr, abi, bre_b, bim_b, cre_b, cim_b, d_skip, B=B, L=S5_CHUNK, name="s5_bwd")
    s5_grads = s5_vjp((dabr, dabi, dbre, dbim, dcre, dcim))
    for nm, gv in zip(("ssm_a_re", "ssm_a_im", "ssm_b_re", "ssm_b_im", "ssm_c_re", "ssm_c_im", "ssm_log_dt"), s5_grads):
        grads[nm] = gv[None]
    grads["ssm_d"] = dd
    gx, _, d_norm_mix[0] = _rms_bwd(x, g0, dh0, gx, "rms_mix_bwd_0", after=tok)

    grads["norm_mix"] = jnp.concatenate(d_norm_mix, axis=0)
    grads["norm_mlp"] = jnp.concatenate(d_norm_mlp, axis=0)
    grads["mlp_w_in"], grads["mlp_w_out"] = (mlp_lo[0], mlp_hi[0]), (mlp_lo[1], mlp_hi[1])
    return loss_part, gx.reshape(B, S, D), grads


WEIGHTS = ['norm_mix', 'norm_mlp', 'norm_final', 'ssm_a_re', 'ssm_a_im', 'ssm_b_re', 'ssm_b_im', 'ssm_c_re',
           'ssm_c_im', 'ssm_d', 'ssm_log_dt', 'ssm_w_glu', 'conv_w_pw1', 'conv_b_pw1', 'conv_w_dw', 'conv_b_dw',
           'conv_ln_g', 'conv_ln_b', 'conv_w_pw2', 'conv_b_pw2', 'gmlp_w_in', 'gmlp_ln_g', 'gmlp_ln_b', 'gmlp_w_s',
           'gmlp_b_s', 'gmlp_w_out', 'attn_w_qkv', 'attn_w_o', 'mlp_w_in', 'mlp_w_out']
BIG_AXIS = {'ssm_w_glu': -1, 'conv_w_pw1': -1, 'conv_w_pw2': -2, 'gmlp_w_in': -1, 'gmlp_w_out': -2,
            'attn_w_qkv': -1, 'attn_w_o': -1, 'mlp_w_in': -1, 'mlp_w_out': -2}
BIG = list(BIG_AXIS)
LAYER_MIXER_WEIGHTS = (('ssm_w_glu',), ('conv_w_pw1', 'conv_w_pw2'), ('gmlp_w_in', 'gmlp_w_out'), ('attn_w_qkv', 'attn_w_o'))
SMALL_SHARDED = ['conv_b_pw1', 'conv_w_dw', 'conv_b_dw', 'conv_ln_g', 'conv_ln_b', 'conv_b_pw2', 'gmlp_ln_g', 'gmlp_ln_b']
SMALL_REPL = [n for n in WEIGHTS if n not in BIG_AXIS and n not in SMALL_SHARDED]
SMALL = SMALL_REPL + SMALL_SHARDED
LANES = 128
FLAT_COLS = 1024


def _pack(arrs, cols, row_mult):
    flat = jnp.concatenate([a.reshape(-1) for a in arrs])
    per = cols * row_mult
    n = -(-flat.shape[0] // per) * per
    return jnp.pad(flat, (0, n - flat.shape[0])).reshape(n // cols, cols)


def _unpack(flat2d, shapes):
    flat = flat2d.reshape(-1)
    out, off = [], 0
    for s in shapes:
        n = int(np.prod(s))
        out.append(flat[off:off + n].reshape(s))
        off += n
    return out


def _as_halves(shard):
    if shard.shape[0] == 1:
        shard = shard[0]
    return shard.reshape((2, shard.shape[0] // 2) + shard.shape[1:])


def _stored_weight(name, arr):
    kind = "cols" if BIG_AXIS[name] == -1 else "rows"
    if arr.shape[1] > 1:
        return [_Stored(arr, kind, lead=(li,)) for li in range(arr.shape[1])]
    arr = arr[:, 0]
    if kind == "rows":
        return arr.reshape(-1, arr.shape[-1])
    return _Stored(arr, kind)


def kernel(x, norm_mix, norm_mlp, norm_final, ssm_a_re, ssm_a_im, ssm_b_re, ssm_b_im, ssm_c_re, ssm_c_im, ssm_d, ssm_log_dt, ssm_w_glu, conv_w_pw1, conv_b_pw1, conv_w_dw, conv_b_dw, conv_ln_g, conv_ln_b, conv_w_pw2, conv_b_pw2, gmlp_w_in, gmlp_ln_g, gmlp_ln_b, gmlp_w_s, gmlp_b_s, gmlp_w_out, attn_w_qkv, attn_w_o, mlp_w_in, mlp_w_out, loss_target, m_norm_mix, m_norm_mlp, m_norm_final, m_ssm_a_re, m_ssm_a_im, m_ssm_b_re, m_ssm_b_im, m_ssm_c_re, m_ssm_c_im, m_ssm_d, m_ssm_log_dt, m_ssm_w_glu, m_conv_w_pw1, m_conv_b_pw1, m_conv_w_dw, m_conv_b_dw, m_conv_ln_g, m_conv_ln_b, m_conv_w_pw2, m_conv_b_pw2, m_gmlp_w_in, m_gmlp_ln_g, m_gmlp_ln_b, m_gmlp_w_s, m_gmlp_b_s, m_gmlp_w_out, m_attn_w_qkv, m_attn_w_o, m_mlp_w_in, m_mlp_w_out, v_norm_mix, v_norm_mlp, v_norm_final, v_ssm_a_re, v_ssm_a_im, v_ssm_b_re, v_ssm_b_im, v_ssm_c_re, v_ssm_c_im, v_ssm_d, v_ssm_log_dt, v_ssm_w_glu, v_conv_w_pw1, v_conv_b_pw1, v_conv_w_dw, v_conv_b_dw, v_conv_ln_g, v_conv_ln_b, v_conv_w_pw2, v_conv_b_pw2, v_gmlp_w_in, v_gmlp_ln_g, v_gmlp_ln_b, v_gmlp_w_s, v_gmlp_b_s, v_gmlp_w_out, v_attn_w_qkv, v_attn_w_o, v_mlp_w_in, v_mlp_w_out):
    args = dict(locals())
    w = {n: args[n] for n in WEIGHTS}
    m = {n: args["m_" + n] for n in WEIGHTS}
    v = {n: args["v_" + n] for n in WEIGHTS}
    chip = 2 * lax.axis_index("x") + lax.axis_index("y")
    core = lax.axis_index("c")

    big_shapes = [w[n].shape for n in BIG]
    started, token = [], None
    for li, mixer in enumerate(LAYER_MIXER_WEIGHTS):
        names = list(mixer) + ["mlp_w_in", "mlp_w_out"]
        shards = [w[n][0] for n in mixer] + [w["mlp_w_in"][li], w["mlp_w_out"][li]]
        halves = [s.astype(BF16).reshape((2, s.shape[0] // 2) + s.shape[1:]) for s in shards]
        send_sems, recv_sems, halves, lands, token = _gather_start(halves, token, name=f"gather_start_{li}")
        started.append((names, (send_sems, recv_sems, halves, lands)))

    def layer_weights(li, after):
        names, st = started[li]
        halves, lands = _gather_wait(st, after, name=f"gather_wait_{li}")
        out = {}
        for n, h, arr in zip(names, halves, lands):
            arr = lax.dynamic_update_index_in_dim(arr, h, chip, axis=0)
            arr = arr.reshape((N_CHIPS, arr.shape[1] * arr.shape[2]) + arr.shape[3:])
            if BIG_AXIS[n] == -1:
                out[n] = _Stored(arr, "cols")
                if n in ("attn_w_qkv", "attn_w_o"):
                    out[n + "_plain"] = jnp.swapaxes(arr, 0, 1).reshape(arr.shape[1], -1)
            else:
                out[n] = arr.reshape(-1, arr.shape[-1])
        return out

    p = {}
    sm_shapes = [w[n].shape for n in SMALL_SHARDED]
    sflat = _pack([w[n] for n in SMALL_SHARDED], LANES, 8)
    rs = sflat.shape[0]
    sall = _allgather8(sflat, name="gather_small").reshape(8, rs, LANES)
    per_chip = [_unpack(sall[2 * k], sm_shapes) for k in range(N_CHIPS)]
    for i, n in enumerate(SMALL_SHARDED):
        p[n] = jnp.concatenate([per_chip[k][i] for k in range(N_CHIPS)], axis=-1)
    for n in SMALL_REPL:
        p[n] = w[n]
    p['conv_w_dw'] = p['conv_w_dw'][0]

    in_flight, arrived, n_rounds = [], {}, [0]

    def finish_round(after):
        k, names, plans, st = in_flight.pop(0)
        srcs, lands = _reduce_wait(st, plans, after, name=f"grads_wait_{k}")
        for n, plan, src, land in zip(names, plans, srcs, lands):
            arrived.setdefault(n, []).append((plan, src, land))

    def grads_done(group):
        names, srcs, plans, lands = [], [], [], []
        for n, v in group.items():
            if isinstance(v, tuple):
                src, plan = v[1].reshape(1, N_CHIPS, -1, v[1].shape[-1]), ((0, v[0]),)
                while any(n in rd[1] for rd in in_flight):
                    finish_round(src)
            else:
                src, plan = v.reshape(2, N_CHIPS, -1, v.shape[-1]), ((0, 0), (1, 1))
            land = arrived[n][-1][2] if n in arrived else lax.empty((2 * N_CHIPS,) + src.shape[2:], BF16)
            names.append(n), srcs.append(src), plans.append(plan), lands.append(land)
        st = _reduce_start(srcs, lands, plans, None, name=f"grads_start_{n_rounds[0]}")
        in_flight.append((n_rounds[0], names, plans, st[:4]))
        n_rounds[0] += 1
        return st[4]

    loss_part, grad_x, g = _local_step(x, loss_target, p, layer_weights, token, grads_done)
    loss = lax.psum(loss_part[0, 0], ("x", "y", "c"))

    my_id = 2 * chip + core
    while in_flight:
        finish_round(grad_x)
    totals = []
    for n in BIG:
        own = None
        for plan, src, land in arrived[n]:
            slab = lax.dynamic_index_in_dim(src, chip, axis=1, keepdims=False)
            if len(plan) == 2:
                own = lax.dynamic_index_in_dim(slab, core, axis=0, keepdims=False)
            else:
                own = slab[0] if own is None else jnp.where(core == plan[0][1], slab[0], own)
        totals.append(_sum8(arrived[n][-1][2], own, my_id, name="owner_sum_" + n))
    shared = _share_halves(totals, name="grads_share_halves")
    big_grads = {}
    for n, arr, t in zip(BIG, shared, totals):
        arr = lax.dynamic_update_index_in_dim(arr, t[None], core, axis=0)
        big_grads[n] = arr.reshape(w[n].shape)

    small_full_shapes = [g[n].shape for n in SMALL]
    gs = _pack([g[n] for n in SMALL], LANES, 8)
    rg = gs.shape[0]
    gs_all = _allgather8(gs, name="gather_small_grads").reshape(8, rg, LANES)
    gs_sum = _rowwise(lambda *a: (functools.reduce(lambda s, t: s + t, a),), [gs_all[k] for k in range(8)], [], (F32,),
                      tr=rg, name="small_grads_sum")[0]
    small_grads = dict(zip(SMALL, _unpack(gs_sum, small_full_shapes)))
    for n in SMALL:
        small_grads[n] = small_grads[n].reshape(p_shape_full(w[n], -1 if n in SMALL_SHARDED else None))
    for n in SMALL_SHARDED:
        width = w[n].shape[-1]
        small_grads[n] = lax.dynamic_slice_in_dim(small_grads[n], chip * width, width, axis=-1)

    grad, delta, new_m, new_v = {}, {}, {}, {}
    for n in BIG:
        shape = w[n].shape
        two_d = lambda t: t.reshape(-1, shape[-1])
        grad[n] = big_grads[n]
        d_, m_, v_ = _adamw(two_d(w[n]), two_d(grad[n]), two_d(m[n]), two_d(v[n]), name="adamw_" + n)
        delta[n], new_m[n], new_v[n] = d_.reshape(shape), m_.reshape(shape), v_.reshape(shape)
    for n in SMALL:
        shape = w[n].shape
        two_d = lambda t: t.reshape(-1, shape[-1])
        grad[n] = small_grads[n]
        d_, m_, v_ = _adamw(two_d(w[n]), two_d(grad[n]), two_d(m[n]), two_d(v[n]), name="adamw_" + n)
        delta[n], new_m[n], new_v[n] = d_.reshape(shape), m_.reshape(shape), v_.reshape(shape)

    return (loss, grad_x, *[grad[n] for n in WEIGHTS], *[delta[n] for n in WEIGHTS],
            *[new_m[n] for n in WEIGHTS], *[new_v[n] for n in WEIGHTS])


def p_shape_full(shard, axis):
    s = list(shard.shape)
    if axis is not None:
        s[axis] *= N_CHIPS
    return tuple(s)
```

```python
import functools
import math

import jax
import jax.numpy as jnp
import numpy as np
from jax import lax
from jax.experimental import pallas as pl
from jax.experimental.pallas import tpu as pltpu

F32 = jnp.float32
BF16 = jnp.bfloat16
MESH = pl.DeviceIdType.MESH

EPS = 1e-6
SSM_GROUP = 16
SSM_STATE = 64
CONV_WIDTH = 31
CONV_HALO = 32
GMLP_CHUNK = 128
GMLP_HEADS = 4
ATT_DILS = (1, 4, 16)
ATT_BLK = 128
ATT_HEADS = 8
HEAD_DIM = 64
ATT_W = ATT_HEADS * HEAD_DIM
N_CHIPS = 4
ADAM_LR, ADAM_B1, ADAM_B2, ADAM_EPS, ADAM_WD, ADAM_STEP = 1e-3, 0.9, 0.999, 1e-8, 0.01, 10

VMEM_BYTES_V7X = 64 * 1024 * 1024
VMEM_LIMIT = VMEM_BYTES_V7X - 8 * 1024 * 1024
MASK_VALUE = -1e30
LANE_TILE = 128


def _cp(sem=None):
    return pltpu.CompilerParams(dimension_semantics=sem, vmem_limit_bytes=VMEM_LIMIT)


def _pick_tile(total, target):
    for cand in range(min(target, total) // LANE_TILE * LANE_TILE, 0, -LANE_TILE):
        if total % cand == 0:
            return cand
    return total


class _Stored:
    def __init__(self, arr, kind="plain", lead=()):
        self.arr, self.kind, self.lead = arr, kind, tuple(lead)
        r, c = arr.shape[-2:]
        self.shape = (r, c * N_CHIPS) if kind == "cols" else (r * N_CHIPS, c) if kind == "rows" else (r, c)

    def spec(self, br, bc, rc_of):
        lead, nl = self.lead, len(self.lead)
        if self.kind == "plain":
            return pl.BlockSpec((None,) * nl + (br, bc), lambda i, j, k: (*lead, *rc_of(i, j, k)))
        if self.kind == "cols":
            per = self.arr.shape[-1] // bc
            assert per * bc == self.arr.shape[-1]

            def imap(i, j, k):
                r, c = rc_of(i, j, k)
                return (c // per, *lead, r, c % per)
        else:
            per = self.arr.shape[-2] // br
            assert per * br == self.arr.shape[-2]

            def imap(i, j, k):
                r, c = rc_of(i, j, k)
                return (r // per, *lead, r % per, c)
        return pl.BlockSpec((None,) * (nl + 1) + (br, bc), imap)


class _ColBlocks:
    kind = "colblocks"

    def __init__(self, arr, first, stride, count, width):
        self.arr, self.first, self.stride, self.width = arr, first, stride, width
        self.shape = (arr.shape[0], count * width)

    def spec(self, br, bc, rc_of):
        per = self.width // bc
        assert per * bc == self.width

        def imap(i, j, k):
            r, c = rc_of(i, j, k)
            return (r, (self.first + (c // per) * self.stride) * per + c % per)
        return pl.BlockSpec((br, bc), imap)


def _mm(a, b, *, ta=False, tb=False, out_dtypes=(F32,), tm=1024, tn=1024, tk=1024,
        pro_a=None, pro_b=None, epi=None, extras=(), n_row_sums=0, out=None, after=None, name):
    if ta:
        K, M = a.shape
    else:
        M, K = a.shape
    if not isinstance(b, (_Stored, _ColBlocks)):
        b = _Stored(b)
    N, Kb = b.shape if tb else b.shape[::-1]
    assert K == Kb, (a.shape, b.shape, ta, tb)
    col_unit = b.width if b.kind == "colblocks" else b.arr.shape[-1] if b.kind == "cols" else b.shape[1]
    row_unit = b.arr.shape[-2] if b.kind == "rows" else b.shape[0]
    n_unit, k_unit = (row_unit, col_unit) if tb else (col_unit, row_unit)
    m_unit = M
    if out is not None:
        m_unit, n_unit = out[4], math.gcd(n_unit, out[5])
    tm, tn, tk = _pick_tile(m_unit, tm), _pick_tile(n_unit, tn), _pick_tile(k_unit, tk)
    assert not n_row_sums or tn == N
    nk = K // tk
    a_spec = (pl.BlockSpec((tk, tm), lambda i, j, k: (k, i)) if ta
              else pl.BlockSpec((tm, tk), lambda i, j, k: (i, k)))
    b_spec = b.spec(tn, tk, lambda i, j, k: (j, k)) if tb else b.spec(tk, tn, lambda i, j, k: (k, j))
    ex_specs = []
    for e in extras:
        if e.shape[0] == 1:
            ex_specs.append(pl.BlockSpec((1, tn), lambda i, j, k: (0, j)))
        else:
            assert e.shape == (M, N), (e.shape, M, N)
            ex_specs.append(pl.BlockSpec((tm, tn), lambda i, j, k: (i, j)))
    dims = (((0 if ta else 1,), (1 if tb else 0,)), ((), ()))
    n_ex, n_out = len(extras), len(out_dtypes)
    direct = epi is None and n_out == 1 and out_dtypes[0] == F32
    use_acc = nk > 1 and not direct
    operands, aliases, alias_specs = [a, b.arr, *extras], {}, []
    if after is not None:
        operands.append(after)
        alias_specs.append(pl.BlockSpec(memory_space=pl.ANY))
    if out is None:
        n_tile_out = n_out - n_row_sums
        out_specs = ([pl.BlockSpec((tm, tn), lambda i, j, k: (i, j))] * n_tile_out
                     + [pl.BlockSpec((1, tn), lambda i, j, k: (0, j))] * n_row_sums)
        out_shape = ([jax.ShapeDtypeStruct((M, N), dt) for dt in out_dtypes[:n_tile_out]]
                     + [jax.ShapeDtypeStruct((1, N), dt) for dt in out_dtypes[n_tile_out:]])
    else:
        shape, block_fn, imap_fn, alias = out[:4]
        assert n_out == 1
        out_specs = [pl.BlockSpec(block_fn(tm, tn), imap_fn(tm, tn))]
        out_shape = [jax.ShapeDtypeStruct(shape, out_dtypes[0])]
        if alias is not None:
            operands.append(alias)
            aliases = {len(operands) - 1: 0}
            alias_specs.append(pl.BlockSpec(memory_space=pl.ANY))
    n_in = len(operands)

    def finish(r, ex, outs, first_row_tile):
        res = epi(r, *[e[...] for e in ex]) if epi is not None else (r,)
        n_tile_out = n_out - n_row_sums
        for o, v in zip(outs[:n_tile_out], res):
            o[...] = v.astype(o.dtype)
        for o, v in zip(outs[n_tile_out:], res[n_tile_out:]):
            @pl.when(first_row_tile)
            def _(o=o):
                o[...] = jnp.zeros_like(o)
            o[...] += v

    def body(*refs):
        a_ref, b_ref = refs[:2]
        ex = refs[2:2 + n_ex]
        outs = refs[n_in:n_in + n_out]
        first_row_tile = pl.program_id(0) == 0
        at, bt = a_ref[...], b_ref[...]
        if pro_a is not None:
            at = pro_a(at)
        if pro_b is not None:
            bt = pro_b(bt)
        part = lax.dot_general(at, bt, dims, preferred_element_type=F32)
        if nk == 1:
            finish(part, ex, outs, first_row_tile)
            return
        acc = refs[-1] if use_acc else outs[0]
        k = pl.program_id(2)

        @pl.when(k == 0)
        def _():
            acc[...] = part

        @pl.when(k > 0)
        def _():
            acc[...] += part

        if use_acc:
            @pl.when(k == nk - 1)
            def _():
                finish(acc[...], ex, outs, first_row_tile)

    res = pl.pallas_call(
        body, grid=(M // tm, N // tn, nk),
        in_specs=[a_spec, b_spec] + ex_specs + alias_specs,
        out_specs=out_specs, out_shape=out_shape,
        scratch_shapes=[pltpu.VMEM((tm, tn), F32)] if use_acc else [],
        input_output_aliases=aliases,
        compiler_params=_cp(("arbitrary" if n_row_sums else "parallel", "parallel", "arbitrary")), name=name,
    )(*operands)
    return res[0] if n_out == 1 else res


def _to_bf16(t):
    return t.astype(BF16)


def _pick_rows(total, target):
    for cand in range(min(target, total) // 8 * 8, 0, -8):
        if total % cand == 0:
            return cand
    return total


def _rowwise(f, rows, params, out_dtypes, *, tr, name):
    T = rows[0].shape[0]
    tr = _pick_rows(T, tr)
    nr, npar = len(rows), len(params)
    blk = [jax.ShapeDtypeStruct((tr, r.shape[1]), F32) for r in rows]
    blk += [jax.ShapeDtypeStruct(p.shape, F32) for p in params]
    out_avals = jax.eval_shape(f, *blk)

    def body(*refs):
        res = f(*[r[...].astype(F32) for r in refs[:nr + npar]])
        for o, v in zip(refs[nr + npar:], res):
            o[...] = v.astype(o.dtype)

    out = pl.pallas_call(
        body, grid=(T // tr,),
        in_specs=[pl.BlockSpec((tr, r.shape[1]), lambda i: (i, 0)) for r in rows]
        + [pl.BlockSpec(p.shape, lambda i, nd=p.ndim: (0,) * nd) for p in params],
        out_specs=[pl.BlockSpec((tr, o.shape[1]), lambda i: (i, 0)) for o in out_avals],
        out_shape=[jax.ShapeDtypeStruct((T, o.shape[1]), dt) for o, dt in zip(out_avals, out_dtypes)],
        compiler_params=_cp(("parallel",)), name=name,
    )(*rows, *params)
    return out


def _rowwise_vjp(f, rows, params, cots, drow_dtypes, *, adds=None, after=None, tr, name):
    adds = adds or {}
    T = rows[0].shape[0]
    tr = _pick_rows(T, tr)
    nr, npar, nc = len(rows), len(params), len(cots)
    want, want_dt = [], []
    for i, dt in enumerate(drow_dtypes):
        for one in (dt if isinstance(dt, tuple) else (dt,)):
            if one is not None:
                want.append(i)
                want_dt.append(one)
    add_idx = sorted(set(i for i in want if i in adds))
    add_arrays = [adds[i] for i in add_idx]
    na = len(add_arrays)
    extra = [] if after is None else [after]

    def body(*refs):
        ins = [r[...].astype(F32) for r in refs[:nr + npar]]
        cvals = [r[...].astype(F32) for r in refs[nr + npar:nr + npar + nc]]
        avals = refs[nr + npar + nc:nr + npar + nc + na]
        outs = refs[nr + npar + nc + na + len(extra):]
        _, vjp = jax.vjp(f, *ins)
        grads = vjp(tuple(cvals))
        for o, i in zip(outs[:len(want)], want):
            g = grads[i]
            if i in adds:
                g = g + avals[add_idx.index(i)][...].astype(F32)
            o[...] = g.astype(o.dtype)
        step = pl.program_id(0)
        for o, g in zip(outs[len(want):], grads[nr:]):
            @pl.when(step == 0)
            def _(o=o):
                o[...] = jnp.zeros_like(o)
            o[...] += g

    rspec = lambda r: pl.BlockSpec((tr, r.shape[1]), lambda i: (i, 0))
    pspec = lambda p: pl.BlockSpec(p.shape, lambda i, nd=p.ndim: (0,) * nd)
    out = pl.pallas_call(
        body, grid=(T // tr,),
        in_specs=[rspec(r) for r in rows] + [pspec(p) for p in params] + [rspec(c) for c in cots]
        + [rspec(a) for a in add_arrays] + [pl.BlockSpec(memory_space=pl.ANY)] * len(extra),
        out_specs=[rspec(rows[i]) for i in want] + [pspec(p) for p in params],
        out_shape=[jax.ShapeDtypeStruct(rows[i].shape, dt) for i, dt in zip(want, want_dt)]
        + [jax.ShapeDtypeStruct(p.shape, F32) for p in params],
        compiler_params=_cp(("arbitrary",)), name=name,
    )(*rows, *params, *cots, *add_arrays, *extra)
    return out[:len(want)], out[len(want):]


def _f_rms(x, g):
    return (x * lax.rsqrt(jnp.mean(x * x, axis=-1, keepdims=True) + EPS) * g,)


def _ln(x, g, b):
    mu = jnp.mean(x, axis=-1, keepdims=True)
    var = jnp.mean(jnp.square(x - mu), axis=-1, keepdims=True)
    return (x - mu) * lax.rsqrt(var + EPS) * g + b


def _f_glu(z):
    d = z.shape[1] // 2
    return (z[:, :d] * jax.nn.sigmoid(z[:, d:]),)


def _f_bias_glu(z, b):
    return _f_glu(z + b)


def _f_ln_silu(y, b_dw, g, b):
    return (jax.nn.silu(_ln(y + b_dw, g, b)),)


def _f_gelu_ln(z, g, b):
    d = z.shape[1] // 2
    zz = jax.nn.gelu(z)
    return zz[:, :d], _ln(zz[:, d:], g, b)


def _f_gelu(y):
    return (jax.nn.gelu(y),)


def _f_merge(o0, o1, o2, l0, l1, l2):
    m = jnp.maximum(jnp.maximum(l0, l1), l2)
    e0, e1, e2 = jnp.exp(l0 - m), jnp.exp(l1 - m), jnp.exp(l2 - m)
    s = e0 + e1 + e2
    pair = 2 * HEAD_DIM
    first_head = lax.broadcasted_iota(jnp.int32, (o0.shape[0], pair), 1) < HEAD_DIM
    cols = []
    for hp in range(o0.shape[1] // pair):
        acc = None
        for o, e in ((o0, e0), (o1, e1), (o2, e2)):
            wgt = e / s
            wp = jnp.where(first_head, wgt[:, 2 * hp:2 * hp + 1], wgt[:, 2 * hp + 1:2 * hp + 2])
            term = wp * o[:, hp * pair:(hp + 1) * pair]
            acc = term if acc is None else acc + term
        cols.append(acc)
    return (jnp.concatenate(cols, axis=1),)


def _f_add(a, b):
    return (a + b,)


def _loss_head(x, tgt, g, *, tr, name):
    T, D = x.shape
    tr = min(tr, T)

    def f(xv, gv, tv):
        y = _f_rms(xv, gv)[0]
        return 0.5 * jnp.mean(jnp.square(y - tv), axis=-1, keepdims=True)

    def body(x_ref, t_ref, g_ref, loss_ref, dx_ref, dxb_ref, dg_ref):
        tv = t_ref[...]
        l, vjp = jax.vjp(lambda xv, gv: f(xv, gv, tv), x_ref[...], g_ref[...])
        dx, dg = vjp(jnp.ones_like(l))
        dx_ref[...] = dx
        dxb_ref[...] = dx.astype(BF16)

        @pl.when(pl.program_id(0) == 0)
        def _():
            loss_ref[...] = jnp.zeros_like(loss_ref)
            dg_ref[...] = jnp.zeros_like(dg_ref)

        loss_ref[...] += jnp.sum(l)
        dg_ref[...] += dg

    return pl.pallas_call(
        body, grid=(T // tr,),
        in_specs=[pl.BlockSpec((tr, D), lambda i: (i, 0)), pl.BlockSpec((tr, D), lambda i: (i, 0)),
                  pl.BlockSpec((1, D), lambda i: (0, 0))],
        out_specs=[pl.BlockSpec((1, 128), lambda i: (0, 0)), pl.BlockSpec((tr, D), lambda i: (i, 0)),
                   pl.BlockSpec((tr, D), lambda i: (i, 0)), pl.BlockSpec((1, D), lambda i: (0, 0))],
        out_shape=[jax.ShapeDtypeStruct((1, 128), F32), jax.ShapeDtypeStruct((T, D), F32),
                   jax.ShapeDtypeStruct((T, D), BF16), jax.ShapeDtypeStruct((1, D), F32)],
        compiler_params=_cp(("arbitrary",)), name=name,
    )(x, tgt, g)


def _adamw(w, g, m, v, *, name):
    R, C = w.shape
    tr = _pick_rows(R, max(8, 2 * 1024 * 1024 // (4 * C)))
    c1 = 1.0 - ADAM_B1 ** ADAM_STEP
    c2 = 1.0 - ADAM_B2 ** ADAM_STEP

    def body(w_ref, g_ref, m_ref, v_ref, d_ref, nm_ref, nv_ref):
        gv = g_ref[...]
        nm = ADAM_B1 * m_ref[...] + (1.0 - ADAM_B1) * gv
        nv = ADAM_B2 * v_ref[...] + (1.0 - ADAM_B2) * jnp.square(gv)
        nm_ref[...] = nm
        nv_ref[...] = nv
        d_ref[...] = -ADAM_LR * ((nm / c1) / (jnp.sqrt(nv / c2) + ADAM_EPS) + ADAM_WD * w_ref[...])

    spec = pl.BlockSpec((tr, C), lambda i: (i, 0))
    return pl.pallas_call(
        body, grid=(R // tr,), in_specs=[spec] * 4, out_specs=[spec] * 3,
        out_shape=[jax.ShapeDtypeStruct((R, C), F32)] * 3,
        compiler_params=_cp(("parallel",)), name=name,
    )(w, g, m, v)


def _s5_prep(a_re, a_im, b_re, b_im, c_re, c_im, log_dt):
    G, N = a_re.shape
    P = b_re.shape[2]
    gpb = 128 // P
    nblk = G // gpb
    dt = jnp.exp(log_dt)[:, None]
    mag = jnp.exp(a_re * dt)
    abr, abi = mag * jnp.cos(a_im * dt), mag * jnp.sin(a_im * dt)
    den = a_re * a_re + a_im * a_im
    nr, ni = abr - 1.0, abi
    qr, qi = (nr * a_re + ni * a_im) / den, (ni * a_re - nr * a_im) / den
    bbr = qr[..., None] * b_re - qi[..., None] * b_im
    bbi = qr[..., None] * b_im + qi[..., None] * b_re
    eye = jnp.eye(gpb, dtype=F32)

    def expand_b(t):
        t = t.reshape(nblk, gpb, N, P).transpose(0, 1, 3, 2)
        return (t[:, :, :, None, :] * eye[None, :, None, :, None]).reshape(nblk, gpb * P, gpb * N)

    def expand_c(t):
        t = t.reshape(nblk, gpb, P, N).transpose(0, 1, 3, 2)
        return (t[:, :, :, None, :] * eye[None, :, None, :, None]).reshape(nblk, gpb * N, gpb * P)

    return (abr.reshape(1, G * N), abi.reshape(1, G * N), expand_b(bbr), expand_b(bbi),
            expand_c(c_re), expand_c(c_im))


def _s5_fwd(h, abr, abi, bre, bim, cre, cim, d, *, B, L, name):
    T, D = h.shape
    S = T // B
    L = min(L, S)
    nc = S // L
    nblk, cb, sb = bre.shape
    GN = abr.shape[1]

    def body(h_ref, ar_ref, ai_ref, bre_ref, bim_ref, cre_ref, cim_ref, d_ref,
             y_ref, yb_ref, xr_ref, xi_ref, er_ref, ei_ref, sr, si, car, cai):
        ci = pl.program_id(1)

        @pl.when(ci == 0)
        def _():
            car[...] = jnp.zeros_like(car)
            cai[...] = jnp.zeros_like(cai)

        for j in range(nblk):
            u = h_ref[:, j * cb:(j + 1) * cb]
            sr[:, j * sb:(j + 1) * sb] = jnp.dot(u, bre_ref[j], preferred_element_type=F32)
            si[:, j * sb:(j + 1) * sb] = jnp.dot(u, bim_ref[j], preferred_element_type=F32)
        ar, ai = ar_ref[...], ai_ref[...]

        def step(t, carry):
            pr, pi = carry
            nr = ar * pr - ai * pi + sr[pl.ds(t, 1), :]
            ni = ar * pi + ai * pr + si[pl.ds(t, 1), :]
            sr[pl.ds(t, 1), :] = nr
            si[pl.ds(t, 1), :] = ni
            return nr, ni

        pr, pi = lax.fori_loop(0, L, step, (car[...], cai[...]), unroll=4)
        car[...] = pr
        cai[...] = pi
        er_ref[0] = pr
        ei_ref[0] = pi
        for j in range(nblk):
            xr = sr[:, j * sb:(j + 1) * sb].astype(BF16)
            xi = si[:, j * sb:(j + 1) * sb].astype(BF16)
            xr_ref[:, j * sb:(j + 1) * sb] = xr
            xi_ref[:, j * sb:(j + 1) * sb] = xi
            y = (jnp.dot(xr, cre_ref[j], preferred_element_type=F32)
                 - jnp.dot(xi, cim_ref[j], preferred_element_type=F32))
            u = h_ref[:, j * cb:(j + 1) * cb].astype(F32)
            y = y + d_ref[:, j * cb:(j + 1) * cb] * u
            y_ref[:, j * cb:(j + 1) * cb] = y
            yb_ref[:, j * cb:(j + 1) * cb] = jax.nn.gelu(y).astype(BF16)

    tok = lambda w: pl.BlockSpec((L, w), lambda b, c: (b * nc + c, 0))
    whole = lambda p: pl.BlockSpec(p.shape, lambda b, c, nd=p.ndim: (0,) * nd)
    end = pl.BlockSpec((1, 1, GN), lambda b, c: (b * nc + c, 0, 0))
    return pl.pallas_call(
        body, grid=(B, nc),
        in_specs=[tok(D)] + [whole(p) for p in (abr, abi, bre, bim, cre, cim, d)],
        out_specs=[tok(D), tok(D), tok(GN), tok(GN), end, end],
        out_shape=[jax.ShapeDtypeStruct((T, D), F32), jax.ShapeDtypeStruct((T, D), BF16),
                   jax.ShapeDtypeStruct((T, GN), BF16),
                   jax.ShapeDtypeStruct((T, GN), BF16), jax.ShapeDtypeStruct((B * nc, 1, GN), F32),
                   jax.ShapeDtypeStruct((B * nc, 1, GN), F32)],
        scratch_shapes=[pltpu.VMEM((L, GN), F32), pltpu.VMEM((L, GN), F32),
                        pltpu.VMEM((1, GN), F32), pltpu.VMEM((1, GN), F32)],
        compiler_params=_cp(("arbitrary", "arbitrary")), name=name,
    )(h, abr, abi, bre, bim, cre, cim, d)


def _s5_bwd(dy, h, xr, xi, er, ei, abr, abi, bre, bim, cre, cim, d, *, B, L, name):
    T, D = h.shape
    S = T // B
    L = min(L, S)
    nc = S // L
    nblk, cb, sb = bre.shape
    GN = abr.shape[1]
    dims_nt = (((1,), (1,)), ((), ()))
    dims_tn = (((0,), (0,)), ((), ()))

    def body(dy_ref, h_ref, xr_ref, xi_ref, er_ref, ei_ref, ar_ref, ai_ref, bre_ref, bim_ref,
             cre_ref, cim_ref, d_ref,
             dh_ref, dbre_ref, dbim_ref, dcre_ref, dcim_ref, dar_ref, dai_ref, dd_ref,
             lr, li, car, cai):
        b, cstep = pl.program_id(0), pl.program_id(1)
        ci = nc - 1 - cstep

        @pl.when((b == 0) & (cstep == 0))
        def _():
            for r in (dbre_ref, dbim_ref, dcre_ref, dcim_ref, dar_ref, dai_ref, dd_ref):
                r[...] = jnp.zeros_like(r)

        @pl.when(cstep == 0)
        def _():
            car[...] = jnp.zeros_like(car)
            cai[...] = jnp.zeros_like(cai)

        for j in range(nblk):
            dyj = dy_ref[:, j * cb:(j + 1) * cb].astype(BF16)
            lr[:, j * sb:(j + 1) * sb] = lax.dot_general(dyj, cre_ref[j], dims_nt, preferred_element_type=F32)
            li[:, j * sb:(j + 1) * sb] = -lax.dot_general(dyj, cim_ref[j], dims_nt, preferred_element_type=F32)
        ar, ai = ar_ref[...], ai_ref[...]

        def step(s, carry):
            t = L - 1 - s
            pr, pi = carry
            nr = lr[pl.ds(t, 1), :] + ar * pr + ai * pi
            ni = li[pl.ds(t, 1), :] - ai * pr + ar * pi
            lr[pl.ds(t, 1), :] = nr
            li[pl.ds(t, 1), :] = ni
            return nr, ni

        pr, pi = lax.fori_loop(0, L, step, (car[...], cai[...]), unroll=4)
        car[...] = pr
        cai[...] = pi
        has_prev = (ci > 0).astype(F32)
        first_row = lax.broadcasted_iota(jnp.int32, (L, sb), 0) == 0
        for j in range(nblk):
            cs = slice(j * cb, (j + 1) * cb)
            ss = slice(j * sb, (j + 1) * sb)
            lrj, lij = lr[:, ss], li[:, ss]
            xrj, xij = xr_ref[:, ss], xi_ref[:, ss]
            pr_j = jnp.where(first_row, er_ref[0][:, ss] * has_prev, pltpu.roll(xrj.astype(F32), 1, 0))
            pi_j = jnp.where(first_row, ei_ref[0][:, ss] * has_prev, pltpu.roll(xij.astype(F32), 1, 0))
            dar_ref[:, ss] += jnp.sum(lrj * pr_j + lij * pi_j, axis=0, keepdims=True)
            dai_ref[:, ss] += jnp.sum(lij * pr_j - lrj * pi_j, axis=0, keepdims=True)
            lrb, lib = lrj.astype(BF16), lij.astype(BF16)
            hj = h_ref[:, cs]
            dyf = dy_ref[:, cs]
            dyj = dyf.astype(BF16)
            dbre_ref[j] += lax.dot_general(hj, lrb, dims_tn, preferred_element_type=F32)
            dbim_ref[j] += lax.dot_general(hj, lib, dims_tn, preferred_element_type=F32)
            dcre_ref[j] += lax.dot_general(xrj, dyj, dims_tn, preferred_element_type=F32)
            dcim_ref[j] -= lax.dot_general(xij, dyj, dims_tn, preferred_element_type=F32)
            du = (lax.dot_general(lrb, bre_ref[j], dims_nt, preferred_element_type=F32)
                  + lax.dot_general(lib, bim_ref[j], dims_nt, preferred_element_type=F32))
            dh_ref[:, cs] = du + d_ref[:, cs] * dyf
            dd_ref[:, cs] += jnp.sum(dyf * hj.astype(F32), axis=0, keepdims=True)

    tok = lambda w: pl.BlockSpec((L, w), lambda b, c: (b * nc + nc - 1 - c, 0))
    whole = lambda p: pl.BlockSpec(p.shape, lambda b, c, nd=p.ndim: (0,) * nd)
    prev_end = pl.BlockSpec((1, 1, GN), lambda b, c: (b * nc + jnp.maximum(nc - 2 - c, 0), 0, 0))
    params = (abr, abi, bre, bim, cre, cim, d)
    acc_shapes = [bre.shape, bim.shape, cre.shape, cim.shape, abr.shape, abi.shape, d.shape]
    out = pl.pallas_call(
        body, grid=(B, nc),
        in_specs=[tok(D), tok(D), tok(GN), tok(GN), prev_end, prev_end] + [whole(p) for p in params],
        out_specs=[tok(D)] + [pl.BlockSpec(s, lambda b, c, nd=len(s): (0,) * nd) for s in acc_shapes],
        out_shape=[jax.ShapeDtypeStruct((T, D), F32)] + [jax.ShapeDtypeStruct(s, F32) for s in acc_shapes],
        scratch_shapes=[pltpu.VMEM((L, GN), F32), pltpu.VMEM((L, GN), F32),
                        pltpu.VMEM((1, GN), F32), pltpu.VMEM((1, GN), F32)],
        compiler_params=_cp(("arbitrary", "arbitrary")), name=name,
    )(dy, h, xr, xi, er, ei, *params)
    return out


def _conv_fwd(zp, w, *, R, tc, name):
    B, SP, C = zp.shape
    S = SP - CONV_HALO
    R, tc = min(R, S), min(tc, C)

    def body(z_ref, w_ref, y_ref):
        def chunk(ci, _):
            start = pl.multiple_of(ci * R, 8)
            ze = z_ref[pl.ds(start, R + CONV_HALO), :]
            acc = jnp.zeros((R, tc), F32)
            for m in range(CONV_WIDTH):
                k = CONV_WIDTH - 1 - m
                sh = ze if m == 0 else pltpu.roll(ze, m, 0)
                acc = acc + w_ref[k:k + 1, :] * sh[CONV_HALO:, :]
            y_ref[pl.ds(start, R), :] = acc
            return 0

        lax.fori_loop(0, S // R, chunk, 0)

    return pl.pallas_call(
        body, grid=(B, C // tc),
        in_specs=[pl.BlockSpec((None, SP, tc), lambda b, c: (b, 0, c)),
                  pl.BlockSpec((32, tc), lambda b, c: (0, c))],
        out_specs=pl.BlockSpec((None, S, tc), lambda b, c: (b, 0, c)),
        out_shape=jax.ShapeDtypeStruct((B, S, C), F32),
        compiler_params=_cp(("parallel", "parallel")), name=name,
    )(zp, w)


def _conv_bwd(zp, dyp, w, *, R, tc, name):
    B, SP, C = zp.shape
    S = SP - CONV_HALO
    R, tc = min(R, S), min(tc, C)

    def body(z_ref, dy_ref, w_ref, dz_ref, dw_ref):
        @pl.when(pl.program_id(1) == 0)
        def _():
            dw_ref[...] = jnp.zeros_like(dw_ref)

        def chunk(ci, _):
            start = pl.multiple_of(ci * R, 8)
            zc = z_ref[pl.ds(start + CONV_HALO, R), :]
            de = dy_ref[pl.ds(start, R + CONV_HALO), :]
            acc = jnp.zeros((R, tc), F32)
            for m in range(CONV_WIDTH):
                k = CONV_WIDTH - 1 - m
                ds_ = (de if m == 0 else pltpu.roll(de, R + CONV_HALO - m, 0))[:R, :]
                acc = acc + w_ref[k:k + 1, :] * ds_
                dw_ref[k:k + 1, :] += jnp.sum(ds_ * zc, axis=0, keepdims=True)
            dz_ref[pl.ds(start, R), :] = acc
            return 0

        lax.fori_loop(0, S // R, chunk, 0)

    return pl.pallas_call(
        body, grid=(C // tc, B),
        in_specs=[pl.BlockSpec((None, SP, tc), lambda c, b: (b, 0, c)),
                  pl.BlockSpec((None, SP, tc), lambda c, b: (b, 0, c)),
                  pl.BlockSpec((32, tc), lambda c, b: (0, c))],
        out_specs=[pl.BlockSpec((None, S, tc), lambda c, b: (b, 0, c)),
                   pl.BlockSpec((32, tc), lambda c, b: (0, c))],
        out_shape=[jax.ShapeDtypeStruct((B, S, C), F32), jax.ShapeDtypeStruct((32, C), F32)],
        compiler_params=_cp(("parallel", "arbitrary")), name=name,
    )(zp, dyp, w)


def _gmlp_fwd(u, vn, ws, bcol, *, nck, name):
    T, E = u.shape
    H = ws.shape[0]
    he = E // H
    rows = nck * GMLP_CHUNK
    rows = min(rows, T)
    n_in = rows // GMLP_CHUNK

    def body(u_ref, v_ref, ws_ref, b_ref, o_ref):
        for c in range(n_in):
            rs = slice(c * GMLP_CHUNK, (c + 1) * GMLP_CHUNK)
            for hh in range(H):
                cs = slice(hh * he, (hh + 1) * he)
                v2 = jnp.dot(ws_ref[hh], v_ref[rs, cs].astype(BF16), preferred_element_type=F32)
                v2 = v2 + b_ref[:, hh:hh + 1]
                o_ref[rs, cs] = (u_ref[rs, cs] * v2).astype(o_ref.dtype)

    tok = pl.BlockSpec((rows, E), lambda i: (i, 0))
    return pl.pallas_call(
        body, grid=(T // rows,),
        in_specs=[tok, tok, pl.BlockSpec(ws.shape, lambda i: (0, 0, 0)), pl.BlockSpec(bcol.shape, lambda i: (0, 0))],
        out_specs=tok, out_shape=jax.ShapeDtypeStruct((T, E), BF16),
        compiler_params=_cp(("parallel",)), name=name,
    )(u, vn, ws, bcol)


def _gmlp_bwd(duv, u, vn, ws, bcol, *, nck, name):
    T, E = u.shape
    H = ws.shape[0]
    he = E // H
    rows = min(nck * GMLP_CHUNK, T)
    n_in = rows // GMLP_CHUNK
    dims_nt = (((1,), (1,)), ((), ()))
    dims_tn = (((0,), (0,)), ((), ()))

    def body(g_ref, u_ref, v_ref, ws_ref, b_ref, du_ref, dv_ref, dws_ref, db_ref):
        @pl.when(pl.program_id(0) == 0)
        def _():
            dws_ref[...] = jnp.zeros_like(dws_ref)
            db_ref[...] = jnp.zeros_like(db_ref)

        for c in range(n_in):
            rs = slice(c * GMLP_CHUNK, (c + 1) * GMLP_CHUNK)
            for hh in range(H):
                cs = slice(hh * he, (hh + 1) * he)
                vb = v_ref[rs, cs].astype(BF16)
                v2 = jnp.dot(ws_ref[hh], vb, preferred_element_type=F32) + b_ref[:, hh:hh + 1]
                g = g_ref[rs, cs]
                du_ref[rs, cs] = g * v2
                dv2 = g * u_ref[rs, cs]
                dv2b = dv2.astype(BF16)
                dv_ref[rs, cs] = lax.dot_general(ws_ref[hh], dv2b, dims_tn, preferred_element_type=F32)
                dws_ref[hh] += lax.dot_general(dv2b, vb, dims_nt, preferred_element_type=F32)
                db_ref[:, hh:hh + 1] += jnp.sum(dv2, axis=1, keepdims=True)

    tok = pl.BlockSpec((rows, E), lambda i: (i, 0))
    return pl.pallas_call(
        body, grid=(T // rows,),
        in_specs=[tok, tok, tok, pl.BlockSpec(ws.shape, lambda i: (0, 0, 0)), pl.BlockSpec(bcol.shape, lambda i: (0, 0))],
        out_specs=[tok, tok, pl.BlockSpec(ws.shape, lambda i: (0, 0, 0)), pl.BlockSpec(bcol.shape, lambda i: (0, 0))],
        out_shape=[jax.ShapeDtypeStruct((T, E), F32), jax.ShapeDtypeStruct((T, E), F32),
                   jax.ShapeDtypeStruct(ws.shape, F32), jax.ShapeDtypeStruct(bcol.shape, F32)],
        compiler_params=_cp(("arbitrary",)), name=name,
    )(duv, u, vn, ws, bcol)


PAIRS = ATT_HEADS // 2


def _att_consts():
    ji = lax.broadcasted_iota(jnp.int32, (2 * ATT_BLK, ATT_BLK), 0)
    ii = lax.broadcasted_iota(jnp.int32, (2 * ATT_BLK, ATT_BLK), 1)
    dist = ii + ATT_BLK - ji
    band = (dist >= 0) & (dist <= ATT_BLK)
    cur = ji >= ATT_BLK
    first_head = lax.broadcasted_iota(jnp.int32, (ATT_BLK, 2 * HEAD_DIM), 1) < HEAD_DIM
    return band, cur, first_head


def _both_heads(t, first_head):
    zero = jnp.zeros_like(t)
    return jnp.concatenate([jnp.where(first_head, t, zero), jnp.where(first_head, zero, t)], axis=0)


def _att_specs(nbk, offs, nsteps, rev):
    rows = nbk * ATT_BLK
    step = (lambda i: nsteps - 1 - i) if rev else (lambda i: i)
    qoff, koff, voff = offs
    blk = lambda off: pl.BlockSpec((rows, 2 * HEAD_DIM), lambda hp, i: (step(i), off + hp))
    prev = lambda off: pl.BlockSpec((ATT_BLK, 2 * HEAD_DIM), lambda hp, i: (jnp.maximum(step(i) * nbk - 1, 0), off + hp))
    out = pl.BlockSpec((rows, 2 * HEAD_DIM), lambda hp, i: (step(i), hp))
    stat = pl.BlockSpec((2, nbk, ATT_BLK), lambda hp, i: (hp, step(i), 0))
    return [blk(qoff), blk(koff), prev(koff), blk(voff), prev(voff)], out, stat


def _att_fwd(arr, offs, *, nb, nbk, name):
    T = arr.shape[0]
    nbk = min(nbk, T // ATT_BLK)
    nsteps = T // (nbk * ATT_BLK)
    scale = HEAD_DIM ** -0.5
    dims_nt = (((1,), (1,)), ((), ()))
    dims_tn = (((0,), (0,)), ((), ()))

    def body(q_ref, k_ref, kp_ref, v_ref, vp_ref, o_ref, lse_ref):
        i = pl.program_id(1)
        band, cur, first_head = _att_consts()
        for jj in range(nbk):
            rs = slice(jj * ATT_BLK, (jj + 1) * ATT_BLK)
            ps = slice((jj - 1) * ATT_BLK, jj * ATT_BLK)
            has_prev = ((i * nbk + jj) & (nb - 1)) != 0
            valid = band & (cur | has_prev)
            kk = jnp.concatenate([kp_ref[...] if jj == 0 else k_ref[ps, :], k_ref[rs, :]], axis=0)
            vv = jnp.concatenate([vp_ref[...] if jj == 0 else v_ref[ps, :], v_ref[rs, :]], axis=0)
            q2 = _both_heads(q_ref[rs, :], first_head)
            st = lax.dot_general(kk, q2, dims_nt, preferred_element_type=F32) * scale
            st = jnp.where(jnp.concatenate([valid, valid], axis=1), st, MASK_VALUE)
            m = jnp.max(st, axis=0, keepdims=True)
            p = jnp.exp(st - m)
            l = jnp.sum(p, axis=0, keepdims=True)
            lse = m + jnp.log(l)
            lse_ref[0, jj:jj + 1, :] = lse[:, :ATT_BLK]
            lse_ref[1, jj:jj + 1, :] = lse[:, ATT_BLK:]
            pn = (p / l).astype(BF16)
            o2 = lax.dot_general(pn, vv, dims_tn, preferred_element_type=F32)
            o_ref[rs, :] = jnp.where(first_head, o2[:ATT_BLK], o2[ATT_BLK:])

    ins, out, stat = _att_specs(nbk, offs, nsteps, False)
    return pl.pallas_call(
        body, grid=(PAIRS, nsteps), in_specs=ins, out_specs=[out, stat],
        out_shape=[jax.ShapeDtypeStruct((T, ATT_W), F32), jax.ShapeDtypeStruct((ATT_HEADS, T // ATT_BLK, ATT_BLK), F32)],
        compiler_params=_cp(("parallel", "parallel")), name=name,
    )(arr, arr, arr, arr, arr)


def _att_bwd(arr, offs, do, lse, dlse, *, nb, nbk, name):
    T = arr.shape[0]
    nbk = min(nbk, T // ATT_BLK)
    nsteps = T // (nbk * ATT_BLK)
    scale = HEAD_DIM ** -0.5
    dims_nt = (((1,), (1,)), ((), ()))
    dims_tn = (((0,), (0,)), ((), ()))

    def body(q_ref, k_ref, kp_ref, v_ref, vp_ref, do_ref, lse_ref, dlse_ref, dq_ref, dk_ref, dv_ref, ck, cv):
        step = pl.program_id(1)
        i = nsteps - 1 - step
        band, cur, first_head = _att_consts()

        @pl.when(step == 0)
        def _():
            ck[...] = jnp.zeros_like(ck)
            cv[...] = jnp.zeros_like(cv)

        carry_k, carry_v = ck[...], cv[...]
        for jj in reversed(range(nbk)):
            rs = slice(jj * ATT_BLK, (jj + 1) * ATT_BLK)
            ps = slice((jj - 1) * ATT_BLK, jj * ATT_BLK)
            has_prev = ((i * nbk + jj) & (nb - 1)) != 0
            valid = band & (cur | has_prev)
            kk = jnp.concatenate([kp_ref[...] if jj == 0 else k_ref[ps, :], k_ref[rs, :]], axis=0)
            vv = jnp.concatenate([vp_ref[...] if jj == 0 else v_ref[ps, :], v_ref[rs, :]], axis=0)
            q2 = _both_heads(q_ref[rs, :], first_head)
            do2 = _both_heads(do_ref[rs, :].astype(BF16), first_head)
            lse = jnp.concatenate([lse_ref[0, jj:jj + 1, :], lse_ref[1, jj:jj + 1, :]], axis=1)
            dlse = jnp.concatenate([dlse_ref[0, jj:jj + 1, :], dlse_ref[1, jj:jj + 1, :]], axis=1)
            st = lax.dot_general(kk, q2, dims_nt, preferred_element_type=F32) * scale
            st = jnp.where(jnp.concatenate([valid, valid], axis=1), st, MASK_VALUE)
            p = jnp.exp(st - lse)
            dp = lax.dot_general(vv, do2, dims_nt, preferred_element_type=F32)
            delta = jnp.sum(p * dp, axis=0, keepdims=True)
            dsb = (p * (dp - delta + dlse) * scale).astype(BF16)
            dq2 = lax.dot_general(dsb, kk, dims_tn, preferred_element_type=F32)
            dkk = jnp.dot(dsb, q2, preferred_element_type=F32)
            dvv = jnp.dot(p.astype(BF16), do2, preferred_element_type=F32)
            dq_ref[rs, :] = jnp.where(first_head, dq2[:ATT_BLK], dq2[ATT_BLK:]).astype(dq_ref.dtype)
            dk_ref[rs, :] = (dkk[ATT_BLK:] + carry_k).astype(dk_ref.dtype)
            dv_ref[rs, :] = (dvv[ATT_BLK:] + carry_v).astype(dv_ref.dtype)
            carry_k, carry_v = dkk[:ATT_BLK], dvv[:ATT_BLK]
        ck[...] = carry_k
        cv[...] = carry_v

    ins, out, stat = _att_specs(nbk, offs, nsteps, True)
    return pl.pallas_call(
        body, grid=(PAIRS, nsteps), in_specs=ins + [out, stat, stat], out_specs=[out] * 3,
        out_shape=[jax.ShapeDtypeStruct((T, ATT_W), BF16)] * 3,
        scratch_shapes=[pltpu.VMEM((ATT_BLK, 2 * HEAD_DIM), F32), pltpu.VMEM((ATT_BLK, 2 * HEAD_DIM), F32)],
        compiler_params=_cp(("arbitrary", "arbitrary")), name=name,
    )(arr, arr, arr, arr, arr, do, lse, dlse)


def _deinterleave(t, B, S, dil):
    if dil == 1:
        return t
    return t.reshape((B, S // dil, dil) + t.shape[1:]).swapaxes(1, 2).reshape(t.shape)


def _interleave(t, B, S, dil):
    if dil == 1:
        return t
    return t.reshape((B, dil, S // dil) + t.shape[1:]).swapaxes(1, 2).reshape(t.shape)


def _stats_to_tokens(lse, B, S, dil):
    return _interleave(lse.reshape(lse.shape[0], -1).T, B, S, dil)


def _stats_from_tokens(dl, B, S, dil):
    return _deinterleave(dl, B, S, dil).T.reshape(dl.shape[1], -1, ATT_BLK)


def _mesh_pos():
    return lax.axis_index("x"), lax.axis_index("y"), lax.axis_index("c")


def _allgather8(xs, *, name):
    m_per, n = xs.shape

    def body(x_ref, out_ref, send_sems, recv_sems, local_sem):
        x, y, c = _mesh_pos()
        me, sibling = (x, y, c), (x, y, 1 - c)
        chips = [(1 - x, y), (x, 1 - y), (1 - x, 1 - y)]

        def rows(px, py, pc):
            return out_ref.at[pl.ds((4 * px + 2 * py + pc) * m_per, m_per), :]

        def copy(k, block, to, src=None):
            return pltpu.make_async_remote_copy(
                src_ref=rows(*block) if src is None else src, dst_ref=rows(*block),
                send_sem=send_sems.at[k], recv_sem=recv_sems.at[k], device_id=to, device_id_type=MESH)

        mine = pltpu.make_async_copy(x_ref, rows(*me), local_sem)
        mine.start()
        first = [copy(0, me, sibling, src=x_ref)]
        first += [copy(1 + j, me, (*chip, c), src=x_ref) for j, chip in enumerate(chips)]
        for cp in first:
            cp.start()
        passed = [copy(4 + j, (*chip, c), sibling) for j, chip in enumerate(chips)]
        for j, chip in enumerate(chips):
            copy(1 + j, (*chip, c), me).wait_recv()
            passed[j].start()
        copy(0, sibling, me).wait_recv()
        for j, chip in enumerate(chips):
            copy(4 + j, (*chip, 1 - c), me).wait_recv()
        for cp in first + passed:
            cp.wait_send()
        mine.wait()

    return pl.pallas_call(
        body, out_shape=jax.ShapeDtypeStruct((8 * m_per, n), xs.dtype),
        in_specs=[pl.BlockSpec(memory_space=pltpu.VMEM)], out_specs=pl.BlockSpec(memory_space=pltpu.VMEM),
        scratch_shapes=[pltpu.SemaphoreType.DMA((7,)), pltpu.SemaphoreType.DMA((7,)), pltpu.SemaphoreType.DMA],
        compiler_params=pltpu.CompilerParams(vmem_limit_bytes=VMEM_LIMIT), name=name,
    )(xs)


def _hbm_call(body, arrays, out_shapes, n_sems, *, name):
    any_spec = pl.BlockSpec(memory_space=pl.ANY)
    return pl.pallas_call(
        body, out_shape=out_shapes, in_specs=[any_spec] * len(arrays), out_specs=[any_spec] * len(out_shapes),
        scratch_shapes=[pltpu.SemaphoreType.DMA((n_sems,)), pltpu.SemaphoreType.DMA((n_sems,))], name=name,
    )(*arrays)


def _other_chips(x, y):
    return [(1 - x, y), (x, 1 - y), (1 - x, 1 - y)]


def _allgather_chips(ws, *, name):
    n = len(ws)

    def body(*refs):
        ins, outs, (send_sems, recv_sems) = refs[:n], refs[n:2 * n], refs[2 * n:]
        x, y, c = _mesh_pos()
        chips = _other_chips(x, y)

        def copy(a, k, px, py, half, to, src=None):
            slot = outs[a].at[2 * px + py, half]
            return pltpu.make_async_remote_copy(
                src_ref=slot if src is None else src, dst_ref=slot,
                send_sem=send_sems.at[6 * a + k], recv_sem=recv_sems.at[6 * a + k], device_id=to, device_id_type=MESH)

        first = [copy(a, j, x, y, c, (*chip, c), src=ins[a].at[c]) for a in range(n) for j, chip in enumerate(chips)]
        for cp in first:
            cp.start()
        passed = []
        for j, chip in enumerate(chips):
            for a in range(n):
                copy(a, j, *chip, c, (x, y, c)).wait_recv()
                passed.append(copy(a, 3 + j, *chip, c, (x, y, 1 - c)))
                passed[-1].start()
        for j, chip in enumerate(chips):
            for a in range(n):
                copy(a, 3 + j, *chip, 1 - c, (x, y, c)).wait_recv()
        for cp in first + passed:
            cp.wait_send()

    return _hbm_call(body, ws, [jax.ShapeDtypeStruct((N_CHIPS,) + w.shape, w.dtype) for w in ws], 6 * n, name=name)


def _split_start(srcs, lands, after, issue, n_sems, *, name):
    ns, nl = len(srcs), len(lands)
    hbm, sem = pl.BlockSpec(memory_space=pltpu.HBM), pl.BlockSpec(memory_space=pltpu.SEMAPHORE)
    extra = [] if after is None else [after]

    def body(*refs):
        n_in = ns + nl + len(extra)
        send_sems, recv_sems = refs[n_in], refs[n_in + 1]
        issue(refs[:ns], refs[ns:ns + nl], send_sems, recv_sems)
        refs[-1][...] = jnp.zeros_like(refs[-1])

    arrays = [pltpu.with_memory_space_constraint(a, pltpu.HBM) for a in list(srcs) + list(lands)]
    out = pl.pallas_call(
        body, name=name,
        out_shape=(pltpu.SemaphoreType.DMA((n_sems,)), pltpu.SemaphoreType.DMA((n_sems,)),
                   *[pltpu.HBM(a.shape, a.dtype) for a in arrays], jax.ShapeDtypeStruct((8, 128), F32)),
        in_specs=[hbm] * (ns + nl) + [pl.BlockSpec(memory_space=pl.ANY)] * len(extra),
        out_specs=(sem, sem, *[hbm] * (ns + nl), pl.BlockSpec(memory_space=pltpu.VMEM)),
        input_output_aliases={i: 2 + i for i in range(ns + nl)},
        compiler_params=pltpu.CompilerParams(has_side_effects=pltpu.SideEffectType.DATAFLOW_SIDE_EFFECTING),
    )(*arrays, *extra)
    return out[0], out[1], list(out[2:2 + ns]), list(out[2 + ns:2 + ns + nl]), out[-1]


def _split_wait(send_sems, recv_sems, srcs, lands, after, waits, *, name):
    ns, nl = len(srcs), len(lands)
    hbm, sem = pl.BlockSpec(memory_space=pltpu.HBM), pl.BlockSpec(memory_space=pltpu.SEMAPHORE)

    def body(*refs):
        waits(refs[:ns], refs[ns:ns + nl], refs[ns + nl], refs[ns + nl + 1])

    out = pl.pallas_call(
        body, name=name,
        out_shape=tuple(pltpu.HBM(a.shape, a.dtype) for a in list(srcs) + list(lands)),
        in_specs=[hbm] * (ns + nl) + [sem, sem, pl.BlockSpec(memory_space=pl.ANY)],
        out_specs=tuple([hbm] * (ns + nl)),
        input_output_aliases={i: i for i in range(ns + nl)},
        compiler_params=pltpu.CompilerParams(has_side_effects=pltpu.SideEffectType.DATAFLOW_SIDE_EFFECTING),
    )(*srcs, *lands, send_sems, recv_sems, after)
    return list(out[:ns]), list(out[ns:])


def _gather_start(halves, after, *, name):
    n = len(halves)
    lands = [lax.empty((N_CHIPS,) + h.shape, h.dtype) for h in halves]

    def issue(srcs, dsts, send_sems, recv_sems):
        x, y, c = _mesh_pos()
        me = 2 * x + y
        for a in range(n):
            for j, (px, py) in enumerate(_other_chips(x, y)):
                for cc in range(2):
                    pltpu.make_async_remote_copy(
                        src_ref=srcs[a].at[c], dst_ref=dsts[a].at[me, c],
                        send_sem=send_sems.at[6 * a + 2 * j + cc], recv_sem=recv_sems.at[6 * a + 2 * j + c],
                        device_id=(px, py, cc), device_id_type=MESH).start()

    return _split_start(halves, lands, after, issue, 6 * n, name=name)


def _gather_wait(started, after, *, name):
    send_sems, recv_sems, halves, lands = started
    n = len(halves)

    def waits(srcs, dsts, send_sems, recv_sems):
        x, y, c = _mesh_pos()
        me = 2 * x + y
        for a in range(n):
            for j, (px, py) in enumerate(_other_chips(x, y)):
                for cc in range(2):
                    pltpu.make_async_remote_copy(
                        src_ref=srcs[a].at[cc], dst_ref=dsts[a].at[2 * px + py, cc],
                        send_sem=send_sems.at[6 * a + 2 * j + cc], recv_sem=recv_sems.at[6 * a + 2 * j + cc],
                        device_id=(px, py, cc), device_id_type=MESH).wait_recv()
        for a in range(n):
            for j, (px, py) in enumerate(_other_chips(x, y)):
                for cc in range(2):
                    pltpu.make_async_remote_copy(
                        src_ref=srcs[a].at[c], dst_ref=dsts[a].at[me, c],
                        send_sem=send_sems.at[6 * a + 2 * j + cc], recv_sem=recv_sems.at[6 * a + 2 * j + c],
                        device_id=(px, py, cc), device_id_type=MESH).wait_send()

    return _split_wait(send_sems, recv_sems, halves, lands, after, waits, name=name)


def _reduce_plan_loops(plans, chip, c, fn):
    for a, plan in enumerate(plans):
        for h, cc in plan:
            for k in range(N_CHIPS):
                fn(a, h, k, cc, jnp.logical_or(chip != k, c != cc))


def _reduce_start(srcs, lands, plans, after, *, name):
    def issue(src_refs, land_refs, send_sems, recv_sems):
        x, y, c = _mesh_pos()
        chip = 2 * x + y
        my_id = 2 * chip + c

        def send(a, h, k, cc, is_other):
            @pl.when(is_other)
            def _():
                pltpu.make_async_remote_copy(
                    src_ref=src_refs[a].at[h, k], dst_ref=land_refs[a].at[my_id],
                    send_sem=send_sems.at[8 * a + 2 * k + cc], recv_sem=recv_sems.at[8 * a + my_id],
                    device_id=(k // 2, k % 2, cc), device_id_type=MESH).start()

        _reduce_plan_loops(plans, chip, c, send)

    return _split_start(srcs, lands, after, issue, 8 * len(srcs), name=name)


def _reduce_wait(started, plans, after, *, name):
    send_sems, recv_sems, srcs, lands = started

    def waits(src_refs, land_refs, send_sems, recv_sems):
        x, y, c = _mesh_pos()
        chip = 2 * x + y
        my_id = 2 * chip + c
        for a, plan in enumerate(plans):
            for h, cc in plan:
                for s in range(2 * N_CHIPS):
                    @pl.when(jnp.logical_and(c == cc, my_id != s))
                    def _(a=a, h=h, s=s):
                        pltpu.make_async_remote_copy(
                            src_ref=src_refs[a].at[h, 0], dst_ref=land_refs[a].at[s],
                            send_sem=send_sems.at[8 * a + s], recv_sem=recv_sems.at[8 * a + s],
                            device_id=(s // 4, (s // 2) % 2, s % 2), device_id_type=MESH).wait_recv()

        def sent(a, h, k, cc, is_other):
            @pl.when(is_other)
            def _():
                pltpu.make_async_remote_copy(
                    src_ref=src_refs[a].at[h, k], dst_ref=land_refs[a].at[my_id],
                    send_sem=send_sems.at[8 * a + 2 * k + cc], recv_sem=recv_sems.at[8 * a + my_id],
                    device_id=(k // 2, k % 2, cc), device_id_type=MESH).wait_send()

        _reduce_plan_loops(plans, chip, c, sent)

    return _split_wait(send_sems, recv_sems, srcs, lands, after, waits, name=name)


def _sum8(land, own, my_id, *, name):
    n_src, R, C = land.shape
    tr = _pick_rows(R, max(8, 1024 * 1024 // (2 * C)))

    def body(id_ref, *refs):
        own_ref, o_ref = refs[n_src], refs[n_src + 1]
        me = id_ref[0]
        acc = None
        for s in range(n_src):
            term = jnp.where(me == s, own_ref[...], refs[s][...]).astype(F32)
            acc = term if acc is None else acc + term
        o_ref[...] = acc

    return pl.pallas_call(
        body, out_shape=jax.ShapeDtypeStruct((R, C), F32),
        grid_spec=pltpu.PrefetchScalarGridSpec(
            num_scalar_prefetch=1, grid=(R // tr,),
            in_specs=[pl.BlockSpec((None, tr, C), lambda i, idr, s=s: (s, i, 0)) for s in range(n_src)]
            + [pl.BlockSpec((tr, C), lambda i, idr: (i, 0))],
            out_specs=pl.BlockSpec((tr, C), lambda i, idr: (i, 0))),
        compiler_params=_cp(("parallel",)), name=name,
    )(my_id.reshape(1).astype(jnp.int32), *([land] * n_src), own)


def _swap_halves(gs, *, name):
    n = len(gs)

    def body(*refs):
        ins, outs, (send_sems, recv_sems) = refs[:n], refs[n:2 * n], refs[2 * n:]
        x, y, c = _mesh_pos()
        cps = [pltpu.make_async_remote_copy(
            src_ref=ins[a].at[1 - c], dst_ref=outs[a], send_sem=send_sems.at[a], recv_sem=recv_sems.at[a],
            device_id=(x, y, 1 - c), device_id_type=MESH) for a in range(n)]
        for cp in cps:
            cp.start()
        for cp in cps:
            cp.wait()

    return _hbm_call(body, gs, [jax.ShapeDtypeStruct(g.shape[1:], g.dtype) for g in gs], n, name=name)


def _scatter_chips(ss, *, name):
    n = len(ss)

    def body(*refs):
        ins, outs, (send_sems, recv_sems) = refs[:n], refs[n:2 * n], refs[2 * n:]
        x, y, c = _mesh_pos()
        me = 2 * x + y
        chips = _other_chips(x, y)

        def copy(a, j, px, py):
            return pltpu.make_async_remote_copy(
                src_ref=ins[a].at[2 * px + py], dst_ref=outs[a].at[me],
                send_sem=send_sems.at[3 * a + j], recv_sem=recv_sems.at[3 * a + j],
                device_id=(px, py, c), device_id_type=MESH)

        def arrival(a, j, px, py):
            return pltpu.make_async_remote_copy(
                src_ref=ins[a].at[me], dst_ref=outs[a].at[2 * px + py],
                send_sem=send_sems.at[3 * a + j], recv_sem=recv_sems.at[3 * a + j],
                device_id=(px, py, c), device_id_type=MESH)

        cps = [copy(a, j, *chip) for a in range(n) for j, chip in enumerate(chips)]
        for cp in cps:
            cp.start()
        for a in range(n):
            for j, chip in enumerate(chips):
                arrival(a, j, *chip).wait_recv()
        for cp in cps:
            cp.wait_send()

    return _hbm_call(body, ss, [jax.ShapeDtypeStruct(s.shape, s.dtype) for s in ss], 3 * n, name=name)


def _share_halves(ts, *, name):
    n = len(ts)

    def body(*refs):
        ins, outs, (send_sems, recv_sems) = refs[:n], refs[n:2 * n], refs[2 * n:]
        x, y, c = _mesh_pos()
        cps = [pltpu.make_async_remote_copy(
            src_ref=ins[a], dst_ref=outs[a].at[c], send_sem=send_sems.at[a], recv_sem=recv_sems.at[a],
            device_id=(x, y, 1 - c), device_id_type=MESH) for a in range(n)]
        for cp in cps:
            cp.start()
        for a in range(n):
            pltpu.make_async_remote_copy(
                src_ref=ins[a], dst_ref=outs[a].at[1 - c], send_sem=send_sems.at[a], recv_sem=recv_sems.at[a],
                device_id=(x, y, 1 - c), device_id_type=MESH).wait_recv()
        for cp in cps:
            cp.wait_send()

    return _hbm_call(body, ts, [jax.ShapeDtypeStruct((2,) + t.shape, t.dtype) for t in ts], n, name=name)


def _half_add(g, ra, core, *, name):
    _, R, C = g.shape
    tr = _pick_rows(R, max(8, 2 * 1024 * 1024 // (4 * C)))

    def body(core_ref, g_ref, ra_ref, o_ref):
        o_ref[...] = (g_ref[...] + ra_ref[...]).astype(o_ref.dtype)

    return pl.pallas_call(
        body, out_shape=jax.ShapeDtypeStruct((R, C), BF16),
        grid_spec=pltpu.PrefetchScalarGridSpec(
            num_scalar_prefetch=1, grid=(R // tr,),
            in_specs=[pl.BlockSpec((None, tr, C), lambda i, cr: (cr[0], i, 0)),
                      pl.BlockSpec((tr, C), lambda i, cr: (i, 0))],
            out_specs=pl.BlockSpec((tr, C), lambda i, cr: (i, 0))),
        compiler_params=_cp(("parallel",)), name=name,
    )(core.reshape(1).astype(jnp.int32), g, ra)


def _sum4(rb, *, name):
    _, R, C = rb.shape
    tr = _pick_rows(R, max(8, 2 * 1024 * 1024 // (4 * C)))

    def body(r0, r1, r2, r3, o_ref):
        f = lambda r: r[...].astype(F32)
        o_ref[...] = ((f(r0) + f(r1)) + f(r2)) + f(r3)

    return pl.pallas_call(
        body, out_shape=jax.ShapeDtypeStruct((R, C), F32), grid=(R // tr,),
        in_specs=[pl.BlockSpec((None, tr, C), lambda i, k=k: (k, i, 0)) for k in range(N_CHIPS)],
        out_specs=pl.BlockSpec((tr, C), lambda i: (i, 0)),
        compiler_params=_cp(("parallel",)), name=name,
    )(rb, rb, rb, rb)


TR = 512
S5_CHUNK = 256


def _rms_fwd(x, g, name):
    return _rowwise(_f_rms, [x], [g], (BF16,), tr=TR, name=name)[0]


def _rms_bwd_epi(dh, x, g, gx):
    r = lax.rsqrt(jnp.mean(x * x, axis=-1, keepdims=True) + EPS)
    xr = x * r
    t = dh * g
    dx = r * (t - xr * jnp.mean(t * xr, axis=-1, keepdims=True)) + gx
    return dx, dx, jnp.sum(dh * xr, axis=0, keepdims=True)


def _mm_rms_bwd(a, w, x, g, gx, *, after=None, name, **kw):
    kw.setdefault("tm", 1024)
    return _mm(a, w, tb=True, epi=_rms_bwd_epi, extras=(x, g, gx), out_dtypes=(F32, BF16, F32), n_row_sums=1,
               tn=x.shape[1], after=after, name=name, **kw)


def _rms_bwd(x, g, dh, gx, name, after=None):
    (dx, dxb), (dg,) = _rowwise_vjp(_f_rms, [x], [g], [dh], [(F32, BF16)], adds={0: gx}, after=after, tr=TR,
                                    name=name)
    return dx, dxb, dg


def _grad_cols(M, Nq):
    def imap(tm, tn):
        hp, per = (M // 2) // tm, Nq // tn
        assert hp * tm * 2 == M and per * tn == Nq, (M, Nq, tm, tn)
        return lambda i, j, k: (i // hp, j // per, i % hp, j % per)
    return (2, N_CHIPS, M // 2, Nq), lambda tm, tn: (None, None, tm, tn), imap, None, M // 2, Nq


def _grad_rows(Mq, N):
    def imap(tm, tn):
        po, hp = Mq // tm, (Mq // 2) // tm
        assert hp * tm * 2 == Mq, (Mq, tm)
        return lambda i, j, k: ((i % po) // hp, i // po, (i % po) % hp, j)
    return (2, N_CHIPS, Mq // 2, N), lambda tm, tn: (None, None, tm, tn), imap, None, Mq // 2, N


def _grad_layer_cols(slot, lh, M, Nq, buf):
    def imap(tm, tn):
        per = Nq // tn
        return lambda i, j, k: (j // per, slot, i, j % per)
    return (N_CHIPS, lh, M, Nq), lambda tm, tn: (None, None, tm, tn), imap, buf, M, Nq


def _grad_layer_rows(slot, lh, Mq, N, buf):
    def imap(tm, tn):
        po = Mq // tm
        return lambda i, j, k: (i // po, slot, i % po, j)
    return (N_CHIPS, lh, Mq, N), lambda tm, tn: (None, None, tm, tn), imap, buf, Mq, N


def _add_then_rms(acc, res, g):
    xo = acc + res
    return xo, _f_rms(xo, g)[0]


def _mlp_fwd(x, h2, w_in, w_out, g_next, li):
    r = _mm(h2, w_in, out_dtypes=(BF16,), epi=lambda acc: (jnp.maximum(acc, 0.0),), tm=4096, name=f"mlp_in_{li}")
    tiles = dict(tm=512, tn=x.shape[1], tk=r.shape[1])
    if g_next is None:
        x_out, h_next = _mm(r, w_out, pro_a=lambda t: t * t, epi=lambda acc, res: (acc + res,), extras=(x,),
                            name=f"mlp_out_{li}", **tiles), None
    else:
        x_out, h_next = _mm(r, w_out, pro_a=lambda t: t * t, epi=_add_then_rms, extras=(x, g_next),
                            out_dtypes=(F32, BF16), name=f"mlp_out_{li}", **tiles)
    return x_out, h_next, (h2, r)


def _mlp_bwd(gx, gxb, x, g, w_in, w_out, saved, li, nl, bufs):
    h2, r = saved
    D, F = w_in.shape
    lh = nl // 2
    da = _mm(gxb, w_out, tb=True, out_dtypes=(BF16,),
             epi=lambda acc, rt: (acc * 2.0 * rt.astype(F32),), extras=(r,), tm=2048, name=f"mlp_dact_{li}")
    buf_in, buf_out = bufs if bufs is not None else (None, None)
    d_w_out = _mm(r, gxb, ta=True, pro_a=lambda t: t * t, tm=512, tn=1024, tk=4096, out_dtypes=(BF16,),
                  out=_grad_layer_rows(li % lh, lh, F // N_CHIPS, D, buf_out), name=f"mlp_dwout_{li}")
    d_w_in = _mm(h2, da, ta=True, tm=1024, tn=1024, tk=4096, out_dtypes=(BF16,),
                 out=_grad_layer_cols(li % lh, lh, D, F // N_CHIPS, buf_in), name=f"mlp_dwin_{li}")
    gx_mid, gxb_mid, dg = _mm_rms_bwd(da, getattr(w_in, "plain", w_in), x, g, gx, tm=512, tk=F, name=f"mlp_dh_{li}")
    return gx_mid, gxb_mid, dg, (d_w_in, d_w_out)


def _local_step(x3, tgt3, p, layer_weights, token=None, grads_done=lambda group: None):
    B, S, D = x3.shape
    T = B * S
    x = x3.reshape(T, D)
    grads = {}
    row = lambda v: v.reshape(1, -1)
    p = dict(p)
    nl = p["norm_mlp"].shape[0]
    mlp_in, mlp_out = [None] * nl, [None] * nl

    def fetch(li, after):
        wl = dict(layer_weights(li, after))
        mlp_in[li], mlp_out[li] = wl.pop("mlp_w_in"), wl.pop("mlp_w_out")
        p.update(wl)

    g0 = row(p["norm_mix"][0])
    if token is not None:
        g0 = g0 + token[:1, :1]
    h0 = _rms_fwd(x, g0, "rms_mix_0")
    s5_args = (p["ssm_a_re"][0], p["ssm_a_im"][0], p["ssm_b_re"][0], p["ssm_b_im"][0],
               p["ssm_c_re"][0], p["ssm_c_im"][0], p["ssm_log_dt"][0])
    s5_exp, s5_vjp = jax.vjp(_s5_prep, *s5_args)
    abr, abi, bre, bim, cre, cim = s5_exp
    bre_b, bim_b, cre_b, cim_b = (t.astype(BF16) for t in (bre, bim, cre, cim))
    d_skip = p["ssm_d"]
    ypre, yb, sxr, sxi, ser, sei = _s5_fwd(h0, abr, abi, bre_b, bim_b, cre_b, cim_b, d_skip, B=B, L=S5_CHUNK,
                                           name="s5_fwd")
    fetch(0, yb)
    w_glu = p["ssm_w_glu"]
    z0 = _mm(yb, w_glu, tm=2048, name="s5_glu_mm")
    gm = [row(p["norm_mlp"][i]) for i in range(nl)]
    g1, g2, g3 = (row(p["norm_mix"][i]) for i in range(1, nl))
    x_mid0, hm0 = _rowwise(lambda z, xr, g: _add_then_rms(_f_glu(z)[0], xr, g), [z0, x], [gm[0]], (F32, BF16),
                           tr=TR, name="s5_glu")
    x1, h1, mlp_saved0 = _mlp_fwd(x_mid0, hm0, mlp_in[0], mlp_out[0], g1, 0)

    fetch(1, h1)
    z1 = _mm(h1, p["conv_w_pw1"], tm=2048, name="conv_pw1")
    zg = _rowwise(_f_bias_glu, [z1], [p["conv_b_pw1"]], (F32,), tr=TR, name="conv_glu")[0]
    zp = jnp.pad(zg.reshape(B, S, D), ((0, 0), (CONV_HALO, 0), (0, 0)))
    w_dw = jnp.pad(p["conv_w_dw"], ((0, 32 - CONV_WIDTH), (0, 0)))
    yc = _conv_fwd(zp, w_dw, R=256, tc=128, name="conv_dw").reshape(T, D)
    ln_par = [p["conv_b_dw"], p["conv_ln_g"], p["conv_ln_b"]]
    qc = _rowwise(_f_ln_silu, [yc], ln_par, (BF16,), tr=TR, name="conv_ln_silu")[0]
    x_mid1, hm1 = _mm(qc, p["conv_w_pw2"], epi=lambda acc, bias, res, g: _add_then_rms(acc + bias, res, g),
                      extras=(p["conv_b_pw2"], x1, gm[1]), out_dtypes=(F32, BF16), tn=D, name="conv_pw2")
    x2, h2, mlp_saved1 = _mlp_fwd(x_mid1, hm1, mlp_in[1], mlp_out[1], g2, 1)

    fetch(2, h2)
    z2 = _mm(h2, p["gmlp_w_in"], tm=2048, name="gmlp_in")
    gl_par = [p["gmlp_ln_g"], p["gmlp_ln_b"]]
    gu, gvn = _rowwise(_f_gelu_ln, [z2], gl_par, (F32, F32), tr=TR, name="gmlp_gelu_ln")
    causal = jnp.tril(jnp.ones((GMLP_CHUNK, GMLP_CHUNK), dtype=bool))
    ws_b = jnp.where(causal[None], p["gmlp_w_s"][0], 0.0).astype(BF16)
    bcol = jnp.pad(p["gmlp_b_s"][0].T, ((0, 0), (0, 128 - GMLP_HEADS)))
    uv = _gmlp_fwd(gu, gvn, ws_b, bcol, nck=4, name="gmlp_spatial")
    x_mid2, hm2 = _mm(uv, p["gmlp_w_out"], epi=_add_then_rms, extras=(x2, gm[2]), out_dtypes=(F32, BF16), tn=D,
                      name="gmlp_out")
    x3_, h3, mlp_saved2 = _mlp_fwd(x_mid2, hm2, mlp_in[2], mlp_out[2], g3, 2)

    fetch(3, h3)
    ng = len(ATT_DILS)
    att_in, o_tok, l_tok, lses = [], [], [], []
    offs = (0, PAIRS, 2 * PAIRS)
    for gi, dil in enumerate(ATT_DILS):
        w_g = _ColBlocks(p["attn_w_qkv_plain"], gi, ng, 3, ATT_W)
        arr = _mm(h3, w_g, out_dtypes=(BF16,), tm=2048, tn=ATT_W, name=f"attn_qkv_{gi}")
        arr = _deinterleave(arr, B, S, dil)
        att_in.append((arr, offs))
        og, lg = _att_fwd(arr, offs, nb=S // dil // ATT_BLK, nbk=8, name=f"attn_fwd_{gi}")
        lses.append(lg)
        o_tok.append(_interleave(og, B, S, dil))
        l_tok.append(_stats_to_tokens(lg, B, S, dil))
    merged2 = _rowwise(_f_merge, o_tok + l_tok, [], (BF16,), tr=TR, name="attn_merge")[0]
    x_mid3, hm3 = _mm(merged2, p["attn_w_o_plain"], epi=_add_then_rms, extras=(x3_, gm[3]), out_dtypes=(F32, BF16),
                      tn=D, name="attn_out")
    x4, _, mlp_saved3 = _mlp_fwd(x_mid3, hm3, mlp_in[3], mlp_out[3], None, 3)

    loss_part, gx, gxb, dgf = _loss_head(x4, tgt3.reshape(T, D), row(p["norm_final"]), tr=TR, name="loss_head")
    grads["norm_final"] = dgf.reshape(-1)
    d_norm_mix, d_norm_mlp = [None] * 4, [None] * 4
    Dq = D // N_CHIPS

    gx, gxb, d_norm_mlp[3], mlp_hi = _mlp_bwd(
        gx, gxb, x_mid3, row(p["norm_mlp"][3]), mlp_in[3], mlp_out[3], mlp_saved3, 3, nl, None)
    dmerged = _mm(gxb, p["attn_w_o"], tb=True, name="attn_dmerged")
    grads["attn_w_o"] = _mm(merged2, gxb, ta=True, tm=256, tn=256, tk=4096, out_dtypes=(BF16,), out=_grad_cols(ATT_W, Dq), name="attn_dwo")
    dml, _ = _rowwise_vjp(_f_merge, o_tok + l_tok, [], [dmerged], [F32] * 6, tr=TR, name="attn_merge_bwd")
    pieces = [[None] * ng for _ in range(3)]
    for gi, dil in enumerate(ATT_DILS):
        arr, offs = att_in[gi]
        dqkv_g = _att_bwd(arr, offs, _deinterleave(dml[gi], B, S, dil), lses[gi],
                          _stats_from_tokens(dml[ng + gi], B, S, dil), nb=S // dil // ATT_BLK, nbk=8,
                          name=f"attn_bwd_{gi}")
        for i in range(3):
            pieces[i][gi] = _interleave(dqkv_g[i], B, S, dil)
    dqkv = jnp.concatenate([pieces[i][gi] for i in range(3) for gi in range(ng)], axis=1)
    qkv_w = 3 * ng * ATT_W
    grads["attn_w_qkv"] = _mm(h3, dqkv, ta=True, tm=512, tn=1152, tk=4096, out_dtypes=(BF16,), out=_grad_cols(D, qkv_w // N_CHIPS),
                              name="attn_dwqkv")
    tok = grads_done({n: grads[n] for n in ("attn_w_qkv", "attn_w_o")})
    gx, gxb, d_norm_mix[3] = _mm_rms_bwd(dqkv, p["attn_w_qkv"], x3_, g3, gx, tk=1152, after=tok, name="attn_dh")

    gx, gxb, d_norm_mlp[2], mlp_hi = _mlp_bwd(
        gx, gxb, x_mid2, row(p["norm_mlp"][2]), mlp_in[2], mlp_out[2], mlp_saved2, 2, nl, mlp_hi)
    tok = grads_done({"mlp_w_in": (1, mlp_hi[0]), "mlp_w_out": (1, mlp_hi[1])})
    duv = _mm(gxb, p["gmlp_w_out"], tb=True, after=tok, name="gmlp_duv")
    grads["gmlp_w_out"] = _mm(uv, gxb, ta=True, tm=128, tn=1024, tk=4096, out_dtypes=(BF16,), out=_grad_rows(Dq, D), name="gmlp_dwout")
    du, dvn, dws, dbcol = _gmlp_bwd(duv, gu, gvn, ws_b, bcol, nck=4, name="gmlp_spatial_bwd")
    grads["gmlp_w_s"] = jnp.where(causal[None], dws, 0.0)[None]
    grads["gmlp_b_s"] = dbcol[:, :GMLP_HEADS].T[None]
    (dz2,), (dlg, dlb_) = _rowwise_vjp(_f_gelu_ln, [z2], gl_par, [du, dvn], [BF16], tr=TR, name="gmlp_gelu_ln_bwd")
    grads["gmlp_ln_g"], grads["gmlp_ln_b"] = dlg, dlb_
    grads["gmlp_w_in"] = _mm(h2, dz2, ta=True, tm=512, tn=512, tk=4096, out_dtypes=(BF16,), out=_grad_cols(D, 2 * Dq), name="gmlp_dwin")
    tok = grads_done({n: grads[n] for n in ("gmlp_w_in", "gmlp_w_out")})
    gx, gxb, d_norm_mix[2] = _mm_rms_bwd(dz2, p["gmlp_w_in"], x2, g2, gx, after=tok, name="gmlp_dh")

    gx, gxb, d_norm_mlp[1], mlp_lo = _mlp_bwd(
        gx, gxb, x_mid1, row(p["norm_mlp"][1]), mlp_in[1], mlp_out[1], mlp_saved1, 1, nl, None)
    dqc = _mm(gxb, p["conv_w_pw2"], tb=True, name="conv_dq")
    grads["conv_w_pw2"] = _mm(qc, gxb, ta=True, tm=128, tn=1024, tk=4096, out_dtypes=(BF16,), out=_grad_rows(Dq, D), name="conv_dwpw2")
    _, (db2,) = _rowwise_vjp(lambda t, b: (t + b,), [gx], [p["conv_b_pw2"]], [gx], [None], tr=TR, name="conv_db2")
    grads["conv_b_pw2"] = db2
    (dyc,), (dbdw, dcg, dcb) = _rowwise_vjp(_f_ln_silu, [yc], ln_par, [dqc], [F32], tr=TR, name="conv_ln_silu_bwd")
    grads["conv_b_dw"], grads["conv_ln_g"], grads["conv_ln_b"] = dbdw, dcg, dcb
    dyp = jnp.pad(dyc.reshape(B, S, D), ((0, 0), (0, CONV_HALO), (0, 0)))
    dzg, dwdw = _conv_bwd(zp, dyp, w_dw, R=256, tc=128, name="conv_dw_bwd")
    grads["conv_w_dw"] = dwdw[:CONV_WIDTH][None]
    (dz1,), (db1,) = _rowwise_vjp(_f_bias_glu, [z1], [p["conv_b_pw1"]], [dzg.reshape(T, D)], [BF16], tr=TR,
                                  name="conv_glu_bwd")
    grads["conv_b_pw1"] = db1
    grads["conv_w_pw1"] = _mm(h1, dz1, ta=True, tm=512, tn=512, tk=4096, out_dtypes=(BF16,), out=_grad_cols(D, 2 * Dq), name="conv_dwpw1")
    tok = grads_done({n: grads[n] for n in ("conv_w_pw1", "conv_w_pw2")})
    gx, gxb, d_norm_mix[1] = _mm_rms_bwd(dz1, p["conv_w_pw1"], x1, g1, gx, after=tok, name="conv_dh")

    gx, gxb, d_norm_mlp[0], mlp_lo = _mlp_bwd(
        gx, gxb, x_mid0, row(p["norm_mlp"][0]), mlp_in[0], mlp_out[0], mlp_saved0, 0, nl, mlp_lo)
    tok = grads_done({"mlp_w_in": (0, mlp_lo[0]), "mlp_w_out": (0, mlp_lo[1])})
    (dz0,), _ = _rowwise_vjp(_f_glu, [z0], [], [gx], [BF16], after=tok, tr=TR, name="s5_glu_bwd")
    grads["ssm_w_glu"] = _mm(yb, dz0, ta=True, tm=512, tn=512, tk=4096, out_dtypes=(BF16,), out=_grad_cols(D, 2 * Dq), name="s5_dwglu")
    tok = grads_done({"ssm_w_glu": grads["ssm_w_glu"]})
    dypre = _mm(dz0, w_glu, tb=True, epi=lambda acc, yp: (jax.vjp(lambda t: jax.nn.gelu(t), yp)[1](acc)[0],),
                extras=(ypre,), name="s5_dypre")
    dh0, dbre, dbim, dcre, dcim, dabr, dabi, dd = _s5_bwd(
        dypre, h0, sxr, sxi, ser, sei, abr, abi, bre_b, bim_b, cre_b, cim_b, d_skip, B=B, L=S5_CHUNK, name="s5_bwd")
    s5_grads = s5_vjp((dabr, dabi, dbre, dbim, dcre, dcim))
    for nm, gv in zip(("ssm_a_re", "ssm_a_im", "ssm_b_re", "ssm_b_im", "ssm_c_re", "ssm_c_im", "ssm_log_dt"), s5_grads):
        grads[nm] = gv[None]
    grads["ssm_d"] = dd
    gx, _, d_norm_mix[0] = _rms_bwd(x, g0, dh0, gx, "rms_mix_bwd_0", after=tok)

    grads["norm_mix"] = jnp.concatenate(d_norm_mix, axis=0)
    grads["norm_mlp"] = jnp.concatenate(d_norm_mlp, axis=0)
    grads["mlp_w_in"], grads["mlp_w_out"] = (mlp_lo[0], mlp_hi[0]), (mlp_lo[1], mlp_hi[1])
    return loss_part, gx.reshape(B, S, D), grads


WEIGHTS = ['norm_mix', 'norm_mlp', 'norm_final', 'ssm_a_re', 'ssm_a_im', 'ssm_b_re', 'ssm_b_im', 'ssm_c_re',
           'ssm_c_im', 'ssm_d', 'ssm_log_dt', 'ssm_w_glu', 'conv_w_pw1', 'conv_b_pw1', 'conv_w_dw', 'conv_b_dw',
           'conv_ln_g', 'conv_ln_b', 'conv_w_pw2', 'conv_b_pw2', 'gmlp_w_in', 'gmlp_ln_g', 'gmlp_ln_b', 'gmlp_w_s',
           'gmlp_b_s', 'gmlp_w_out', 'attn_w_qkv', 'attn_w_o', 'mlp_w_in', 'mlp_w_out']
BIG_AXIS = {'ssm_w_glu': -1, 'conv_w_pw1': -1, 'conv_w_pw2': -2, 'gmlp_w_in': -1, 'gmlp_w_out': -2,
            'attn_w_qkv': -1, 'attn_w_o': -1, 'mlp_w_in': -1, 'mlp_w_out': -2}
BIG = list(BIG_AXIS)
LAYER_MIXER_WEIGHTS = (('ssm_w_glu',), ('conv_w_pw1', 'conv_w_pw2'), ('gmlp_w_in', 'gmlp_w_out'), ('attn_w_qkv', 'attn_w_o'))
SMALL_SHARDED = ['conv_b_pw1', 'conv_w_dw', 'conv_b_dw', 'conv_ln_g', 'conv_ln_b', 'conv_b_pw2', 'gmlp_ln_g', 'gmlp_ln_b']
SMALL_REPL = [n for n in WEIGHTS if n not in BIG_AXIS and n not in SMALL_SHARDED]
SMALL = SMALL_REPL + SMALL_SHARDED
LANES = 128
FLAT_COLS = 1024


def _pack(arrs, cols, row_mult):
    flat = jnp.concatenate([a.reshape(-1) for a in arrs])
    per = cols * row_mult
    n = -(-flat.shape[0] // per) * per
    return jnp.pad(flat, (0, n - flat.shape[0])).reshape(n // cols, cols)


def _unpack(flat2d, shapes):
    flat = flat2d.reshape(-1)
    out, off = [], 0
    for s in shapes:
        n = int(np.prod(s))
        out.append(flat[off:off + n].reshape(s))
        off += n
    return out


def _as_halves(shard):
    if shard.shape[0] == 1:
        shard = shard[0]
    return shard.reshape((2, shard.shape[0] // 2) + shard.shape[1:])


def _stored_weight(name, arr):
    kind = "cols" if BIG_AXIS[name] == -1 else "rows"
    if arr.shape[1] > 1:
        return [_Stored(arr, kind, lead=(li,)) for li in range(arr.shape[1])]
    arr = arr[:, 0]
    if kind == "rows":
        return arr.reshape(-1, arr.shape[-1])
    return _Stored(arr, kind)


def kernel(x, norm_mix, norm_mlp, norm_final, ssm_a_re, ssm_a_im, ssm_b_re, ssm_b_im, ssm_c_re, ssm_c_im, ssm_d, ssm_log_dt, ssm_w_glu, conv_w_pw1, conv_b_pw1, conv_w_dw, conv_b_dw, conv_ln_g, conv_ln_b, conv_w_pw2, conv_b_pw2, gmlp_w_in, gmlp_ln_g, gmlp_ln_b, gmlp_w_s, gmlp_b_s, gmlp_w_out, attn_w_qkv, attn_w_o, mlp_w_in, mlp_w_out, loss_target, m_norm_mix, m_norm_mlp, m_norm_final, m_ssm_a_re, m_ssm_a_im, m_ssm_b_re, m_ssm_b_im, m_ssm_c_re, m_ssm_c_im, m_ssm_d, m_ssm_log_dt, m_ssm_w_glu, m_conv_w_pw1, m_conv_b_pw1, m_conv_w_dw, m_conv_b_dw, m_conv_ln_g, m_conv_ln_b, m_conv_w_pw2, m_conv_b_pw2, m_gmlp_w_in, m_gmlp_ln_g, m_gmlp_ln_b, m_gmlp_w_s, m_gmlp_b_s, m_gmlp_w_out, m_attn_w_qkv, m_attn_w_o, m_mlp_w_in, m_mlp_w_out, v_norm_mix, v_norm_mlp, v_norm_final, v_ssm_a_re, v_ssm_a_im, v_ssm_b_re, v_ssm_b_im, v_ssm_c_re, v_ssm_c_im, v_ssm_d, v_ssm_log_dt, v_ssm_w_glu, v_conv_w_pw1, v_conv_b_pw1, v_conv_w_dw, v_conv_b_dw, v_conv_ln_g, v_conv_ln_b, v_conv_w_pw2, v_conv_b_pw2, v_gmlp_w_in, v_gmlp_ln_g, v_gmlp_ln_b, v_gmlp_w_s, v_gmlp_b_s, v_gmlp_w_out, v_attn_w_qkv, v_attn_w_o, v_mlp_w_in, v_mlp_w_out):
    args = dict(locals())
    w = {n: args[n] for n in WEIGHTS}
    m = {n: args["m_" + n] for n in WEIGHTS}
    v = {n: args["v_" + n] for n in WEIGHTS}
    chip = 2 * lax.axis_index("x") + lax.axis_index("y")
    core = lax.axis_index("c")

    big_shapes = [w[n].shape for n in BIG]
    started, token = [], None
    for li, mixer in enumerate(LAYER_MIXER_WEIGHTS):
        names = list(mixer) + ["mlp_w_in", "mlp_w_out"]
        shards = [w[n][0] for n in mixer] + [w["mlp_w_in"][li], w["mlp_w_out"][li]]
        halves = [s.astype(BF16).reshape((2, s.shape[0] // 2) + s.shape[1:]) for s in shards]
        send_sems, recv_sems, halves, lands, token = _gather_start(halves, token, name=f"gather_start_{li}")
        started.append((names, (send_sems, recv_sems, halves, lands)))

    def layer_weights(li, after):
        names, st = started[li]
        halves, lands = _gather_wait(st, after, name=f"gather_wait_{li}")
        out = {}
        for n, h, arr in zip(names, halves, lands):
            arr = lax.dynamic_update_index_in_dim(arr, h, chip, axis=0)
            arr = arr.reshape((N_CHIPS, arr.shape[1] * arr.shape[2]) + arr.shape[3:])
            if BIG_AXIS[n] == -1:
                out[n] = _Stored(arr, "cols")
                if n in ("attn_w_qkv", "attn_w_o"):
                    out[n + "_plain"] = jnp.swapaxes(arr, 0, 1).reshape(arr.shape[1], -1)
                if n == "mlp_w_in":
                    out[n].plain = jnp.swapaxes(arr, 0, 1).reshape(arr.shape[1], -1)
            else:
                out[n] = arr.reshape(-1, arr.shape[-1])
        return out

    p = {}
    sm_shapes = [w[n].shape for n in SMALL_SHARDED]
    sflat = _pack([w[n] for n in SMALL_SHARDED], LANES, 8)
    rs = sflat.shape[0]
    sall = _allgather8(sflat, name="gather_small").reshape(8, rs, LANES)
    per_chip = [_unpack(sall[2 * k], sm_shapes) for k in range(N_CHIPS)]
    for i, n in enumerate(SMALL_SHARDED):
        p[n] = jnp.concatenate([per_chip[k][i] for k in range(N_CHIPS)], axis=-1)
    for n in SMALL_REPL:
        p[n] = w[n]
    p['conv_w_dw'] = p['conv_w_dw'][0]

    in_flight, arrived, n_rounds = [], {}, [0]

    def finish_round(after):
        k, names, plans, st = in_flight.pop(0)
        srcs, lands = _reduce_wait(st, plans, after, name=f"grads_wait_{k}")
        for n, plan, src, land in zip(names, plans, srcs, lands):
            arrived.setdefault(n, []).append((plan, src, land))

    def grads_done(group):
        names, srcs, plans, lands = [], [], [], []
        for n, v in group.items():
            if isinstance(v, tuple):
                src, plan = v[1].reshape(1, N_CHIPS, -1, v[1].shape[-1]), ((0, v[0]),)
                while any(n in rd[1] for rd in in_flight):
                    finish_round(src)
            else:
                src, plan = v.reshape(2, N_CHIPS, -1, v.shape[-1]), ((0, 0), (1, 1))
            land = arrived[n][-1][2] if n in arrived else lax.empty((2 * N_CHIPS,) + src.shape[2:], BF16)
            names.append(n), srcs.append(src), plans.append(plan), lands.append(land)
        st = _reduce_start(srcs, lands, plans, None, name=f"grads_start_{n_rounds[0]}")
        in_flight.append((n_rounds[0], names, plans, st[:4]))
        n_rounds[0] += 1
        return st[4]

    loss_part, grad_x, g = _local_step(x, loss_target, p, layer_weights, token, grads_done)
    loss = lax.psum(loss_part[0, 0], ("x", "y", "c"))

    my_id = 2 * chip + core
    while in_flight:
        finish_round(grad_x)
    totals = []
    for n in BIG:
        own = None
        for plan, src, land in arrived[n]:
            slab = lax.dynamic_index_in_dim(src, chip, axis=1, keepdims=False)
            if len(plan) == 2:
                own = lax.dynamic_index_in_dim(slab, core, axis=0, keepdims=False)
            else:
                own = slab[0] if own is None else jnp.where(core == plan[0][1], slab[0], own)
        totals.append(_sum8(arrived[n][-1][2], own, my_id, name="owner_sum_" + n))
    shared = _share_halves(totals, name="grads_share_halves")
    big_grads = {}
    for n, arr, t in zip(BIG, shared, totals):
        arr = lax.dynamic_update_index_in_dim(arr, t[None], core, axis=0)
        big_grads[n] = arr.reshape(w[n].shape)

    small_full_shapes = [g[n].shape for n in SMALL]
    gs = _pack([g[n] for n in SMALL], LANES, 8)
    rg = gs.shape[0]
    gs_all = _allgather8(gs, name="gather_small_grads").reshape(8, rg, LANES)
    gs_sum = _rowwise(lambda *a: (functools.reduce(lambda s, t: s + t, a),), [gs_all[k] for k in range(8)], [], (F32,),
                      tr=rg, name="small_grads_sum")[0]
    small_grads = dict(zip(SMALL, _unpack(gs_sum, small_full_shapes)))
    for n in SMALL:
        small_grads[n] = small_grads[n].reshape(p_shape_full(w[n], -1 if n in SMALL_SHARDED else None))
    for n in SMALL_SHARDED:
        width = w[n].shape[-1]
        small_grads[n] = lax.dynamic_slice_in_dim(small_grads[n], chip * width, width, axis=-1)

    grad, delta, new_m, new_v = {}, {}, {}, {}
    for n in BIG:
        shape = w[n].shape
        two_d = lambda t: t.reshape(-1, shape[-1])
        grad[n] = big_grads[n]
        d_, m_, v_ = _adamw(two_d(w[n]), two_d(grad[n]), two_d(m[n]), two_d(v[n]), name="adamw_" + n)
        delta[n], new_m[n], new_v[n] = d_.reshape(shape), m_.reshape(shape), v_.reshape(shape)
    for n in SMALL:
        shape = w[n].shape
        two_d = lambda t: t.reshape(-1, shape[-1])
        grad[n] = small_grads[n]
        d_, m_, v_ = _adamw(two_d(w[n]), two_d(grad[n]), two_d(m[n]), two_d(v[n]), name="adamw_" + n)
        delta[n], new_m[n], new_v[n] = d_.reshape(shape), m_.reshape(shape), v_.reshape(shape)

    return (loss, grad_x, *[grad[n] for n in WEIGHTS], *[delta[n] for n in WEIGHTS],
            *[new_m[n] for n in WEIGHTS], *[new_v[n] for n in WEIGHTS])


def p_shape_full(shard, axis):
    s = list(shard.shape)
    if axis is not None:
        s[axis] *= N_CHIPS
    return tuple(s)
```

```python
import functools
import math

import jax
import jax.numpy as jnp
import numpy as np
from jax import lax
from jax.experimental import pallas as pl
from jax.experimental.pallas import tpu as pltpu

F32 = jnp.float32
BF16 = jnp.bfloat16
MESH = pl.DeviceIdType.MESH

EPS = 1e-6
SSM_GROUP = 16
SSM_STATE = 64
CONV_WIDTH = 31
CONV_HALO = 32
GMLP_CHUNK = 128
GMLP_HEADS = 4
ATT_DILS = (1, 4, 16)
ATT_BLK = 128
ATT_HEADS = 8
HEAD_DIM = 64
ATT_W = ATT_HEADS * HEAD_DIM
N_CHIPS = 4
ADAM_LR, ADAM_B1, ADAM_B2, ADAM_EPS, ADAM_WD, ADAM_STEP = 1e-3, 0.9, 0.999, 1e-8, 0.01, 10

VMEM_BYTES_V7X = 64 * 1024 * 1024
VMEM_LIMIT = VMEM_BYTES_V7X - 8 * 1024 * 1024
MASK_VALUE = -1e30
LANE_TILE = 128


def _cp(sem=None):
    return pltpu.CompilerParams(dimension_semantics=sem, vmem_limit_bytes=VMEM_LIMIT)


def _pick_tile(total, target):
    for cand in range(min(target, total) // LANE_TILE * LANE_TILE, 0, -LANE_TILE):
        if total % cand == 0:
            return cand
    return total


class _Stored:
    def __init__(self, arr, kind="plain", lead=()):
        self.arr, self.kind, self.lead = arr, kind, tuple(lead)
        r, c = arr.shape[-2:]
        self.shape = (r, c * N_CHIPS) if kind == "cols" else (r * N_CHIPS, c) if kind == "rows" else (r, c)

    def spec(self, br, bc, rc_of):
        lead, nl = self.lead, len(self.lead)
        if self.kind == "plain":
            return pl.BlockSpec((None,) * nl + (br, bc), lambda i, j, k: (*lead, *rc_of(i, j, k)))
        if self.kind == "cols":
            per = self.arr.shape[-1] // bc
            assert per * bc == self.arr.shape[-1]

            def imap(i, j, k):
                r, c = rc_of(i, j, k)
                return (c // per, *lead, r, c % per)
        else:
            per = self.arr.shape[-2] // br
            assert per * br == self.arr.shape[-2]

            def imap(i, j, k):
                r, c = rc_of(i, j, k)
                return (r // per, *lead, r % per, c)
        return pl.BlockSpec((None,) * (nl + 1) + (br, bc), imap)


class _ColBlocks:
    kind = "colblocks"

    def __init__(self, arr, first, stride, count, width):
        self.arr, self.first, self.stride, self.width = arr, first, stride, width
        self.shape = (arr.shape[0], count * width)

    def spec(self, br, bc, rc_of):
        per = self.width // bc
        assert per * bc == self.width

        def imap(i, j, k):
            r, c = rc_of(i, j, k)
            return (r, (self.first + (c // per) * self.stride) * per + c % per)
        return pl.BlockSpec((br, bc), imap)


def _mm(a, b, *, ta=False, tb=False, out_dtypes=(F32,), tm=1024, tn=1024, tk=1024,
        pro_a=None, pro_b=None, epi=None, extras=(), n_row_sums=0, out=None, after=None, name):
    if ta:
        K, M = a.shape
    else:
        M, K = a.shape
    if not isinstance(b, (_Stored, _ColBlocks)):
        b = _Stored(b)
    N, Kb = b.shape if tb else b.shape[::-1]
    assert K == Kb, (a.shape, b.shape, ta, tb)
    col_unit = b.width if b.kind == "colblocks" else b.arr.shape[-1] if b.kind == "cols" else b.shape[1]
    row_unit = b.arr.shape[-2] if b.kind == "rows" else b.shape[0]
    n_unit, k_unit = (row_unit, col_unit) if tb else (col_unit, row_unit)
    m_unit = M
    if out is not None:
        m_unit, n_unit = out[4], math.gcd(n_unit, out[5])
    tm, tn, tk = _pick_tile(m_unit, tm), _pick_tile(n_unit, tn), _pick_tile(k_unit, tk)
    assert not n_row_sums or tn == N
    nk = K // tk
    a_spec = (pl.BlockSpec((tk, tm), lambda i, j, k: (k, i)) if ta
              else pl.BlockSpec((tm, tk), lambda i, j, k: (i, k)))
    b_spec = b.spec(tn, tk, lambda i, j, k: (j, k)) if tb else b.spec(tk, tn, lambda i, j, k: (k, j))
    ex_specs = []
    for e in extras:
        if e.shape[0] == 1:
            ex_specs.append(pl.BlockSpec((1, tn), lambda i, j, k: (0, j)))
        else:
            assert e.shape == (M, N), (e.shape, M, N)
            ex_specs.append(pl.BlockSpec((tm, tn), lambda i, j, k: (i, j)))
    dims = (((0 if ta else 1,), (1 if tb else 0,)), ((), ()))
    n_ex, n_out = len(extras), len(out_dtypes)
    direct = epi is None and n_out == 1 and out_dtypes[0] == F32
    use_acc = nk > 1 and not direct
    operands, aliases, alias_specs = [a, b.arr, *extras], {}, []
    if after is not None:
        operands.append(after)
        alias_specs.append(pl.BlockSpec(memory_space=pl.ANY))
    if out is None:
        n_tile_out = n_out - n_row_sums
        out_specs = ([pl.BlockSpec((tm, tn), lambda i, j, k: (i, j))] * n_tile_out
                     + [pl.BlockSpec((1, tn), lambda i, j, k: (0, j))] * n_row_sums)
        out_shape = ([jax.ShapeDtypeStruct((M, N), dt) for dt in out_dtypes[:n_tile_out]]
                     + [jax.ShapeDtypeStruct((1, N), dt) for dt in out_dtypes[n_tile_out:]])
    else:
        shape, block_fn, imap_fn, alias = out[:4]
        assert n_out == 1
        out_specs = [pl.BlockSpec(block_fn(tm, tn), imap_fn(tm, tn))]
        out_shape = [jax.ShapeDtypeStruct(shape, out_dtypes[0])]
        if alias is not None:
            operands.append(alias)
            aliases = {len(operands) - 1: 0}
            alias_specs.append(pl.BlockSpec(memory_space=pl.ANY))
    n_in = len(operands)

    def finish(r, ex, outs, first_row_tile):
        res = epi(r, *[e[...] for e in ex]) if epi is not None else (r,)
        n_tile_out = n_out - n_row_sums
        for o, v in zip(outs[:n_tile_out], res):
            o[...] = v.astype(o.dtype)
        for o, v in zip(outs[n_tile_out:], res[n_tile_out:]):
            @pl.when(first_row_tile)
            def _(o=o):
                o[...] = jnp.zeros_like(o)
            o[...] += v

    def body(*refs):
        a_ref, b_ref = refs[:2]
        ex = refs[2:2 + n_ex]
        outs = refs[n_in:n_in + n_out]
        first_row_tile = pl.program_id(0) == 0
        at, bt = a_ref[...], b_ref[...]
        if pro_a is not None:
            at = pro_a(at)
        if pro_b is not None:
            bt = pro_b(bt)
        part = lax.dot_general(at, bt, dims, preferred_element_type=F32)
        if nk == 1:
            finish(part, ex, outs, first_row_tile)
            return
        acc = refs[-1] if use_acc else outs[0]
        k = pl.program_id(2)

        @pl.when(k == 0)
        def _():
            acc[...] = part

        @pl.when(k > 0)
        def _():
            acc[...] += part

        if use_acc:
            @pl.when(k == nk - 1)
            def _():
                finish(acc[...], ex, outs, first_row_tile)

    res = pl.pallas_call(
        body, grid=(M // tm, N // tn, nk),
        in_specs=[a_spec, b_spec] + ex_specs + alias_specs,
        out_specs=out_specs, out_shape=out_shape,
        scratch_shapes=[pltpu.VMEM((tm, tn), F32)] if use_acc else [],
        input_output_aliases=aliases,
        compiler_params=_cp(("arbitrary" if n_row_sums else "parallel", "parallel", "arbitrary")), name=name,
    )(*operands)
    return res[0] if n_out == 1 else res


def _to_bf16(t):
    return t.astype(BF16)


def _pick_rows(total, target):
    for cand in range(min(target, total) // 8 * 8, 0, -8):
        if total % cand == 0:
            return cand
    return total


def _rowwise(f, rows, params, out_dtypes, *, tr, name):
    T = rows[0].shape[0]
    tr = _pick_rows(T, tr)
    nr, npar = len(rows), len(params)
    blk = [jax.ShapeDtypeStruct((tr, r.shape[1]), F32) for r in rows]
    blk += [jax.ShapeDtypeStruct(p.shape, F32) for p in params]
    out_avals = jax.eval_shape(f, *blk)

    def body(*refs):
        res = f(*[r[...].astype(F32) for r in refs[:nr + npar]])
        for o, v in zip(refs[nr + npar:], res):
            o[...] = v.astype(o.dtype)

    out = pl.pallas_call(
        body, grid=(T // tr,),
        in_specs=[pl.BlockSpec((tr, r.shape[1]), lambda i: (i, 0)) for r in rows]
        + [pl.BlockSpec(p.shape, lambda i, nd=p.ndim: (0,) * nd) for p in params],
        out_specs=[pl.BlockSpec((tr, o.shape[1]), lambda i: (i, 0)) for o in out_avals],
        out_shape=[jax.ShapeDtypeStruct((T, o.shape[1]), dt) for o, dt in zip(out_avals, out_dtypes)],
        compiler_params=_cp(("parallel",)), name=name,
    )(*rows, *params)
    return out


def _rowwise_vjp(f, rows, params, cots, drow_dtypes, *, adds=None, after=None, tr, name):
    adds = adds or {}
    T = rows[0].shape[0]
    tr = _pick_rows(T, tr)
    nr, npar, nc = len(rows), len(params), len(cots)
    want, want_dt = [], []
    for i, dt in enumerate(drow_dtypes):
        for one in (dt if isinstance(dt, tuple) else (dt,)):
            if one is not None:
                want.append(i)
                want_dt.append(one)
    add_idx = sorted(set(i for i in want if i in adds))
    add_arrays = [adds[i] for i in add_idx]
    na = len(add_arrays)
    extra = [] if after is None else [after]

    def body(*refs):
        ins = [r[...].astype(F32) for r in refs[:nr + npar]]
        cvals = [r[...].astype(F32) for r in refs[nr + npar:nr + npar + nc]]
        avals = refs[nr + npar + nc:nr + npar + nc + na]
        outs = refs[nr + npar + nc + na + len(extra):]
        _, vjp = jax.vjp(f, *ins)
        grads = vjp(tuple(cvals))
        for o, i in zip(outs[:len(want)], want):
            g = grads[i]
            if i in adds:
                g = g + avals[add_idx.index(i)][...].astype(F32)
            o[...] = g.astype(o.dtype)
        step = pl.program_id(0)
        for o, g in zip(outs[len(want):], grads[nr:]):
            @pl.when(step == 0)
            def _(o=o):
                o[...] = jnp.zeros_like(o)
            o[...] += g

    rspec = lambda r: pl.BlockSpec((tr, r.shape[1]), lambda i: (i, 0))
    pspec = lambda p: pl.BlockSpec(p.shape, lambda i, nd=p.ndim: (0,) * nd)
    out = pl.pallas_call(
        body, grid=(T // tr,),
        in_specs=[rspec(r) for r in rows] + [pspec(p) for p in params] + [rspec(c) for c in cots]
        + [rspec(a) for a in add_arrays] + [pl.BlockSpec(memory_space=pl.ANY)] * len(extra),
        out_specs=[rspec(rows[i]) for i in want] + [pspec(p) for p in params],
        out_shape=[jax.ShapeDtypeStruct(rows[i].shape, dt) for i, dt in zip(want, want_dt)]
        + [jax.ShapeDtypeStruct(p.shape, F32) for p in params],
        compiler_params=_cp(("arbitrary",)), name=name,
    )(*rows, *params, *cots, *add_arrays, *extra)
    return out[:len(want)], out[len(want):]


def _f_rms(x, g):
    return (x * lax.rsqrt(jnp.mean(x * x, axis=-1, keepdims=True) + EPS) * g,)


def _ln(x, g, b):
    mu = jnp.mean(x, axis=-1, keepdims=True)
    var = jnp.mean(jnp.square(x - mu), axis=-1, keepdims=True)
    return (x - mu) * lax.rsqrt(var + EPS) * g + b


def _f_glu(z):
    d = z.shape[1] // 2
    return (z[:, :d] * jax.nn.sigmoid(z[:, d:]),)


def _f_bias_glu(z, b):
    return _f_glu(z + b)


def _f_ln_silu(y, b_dw, g, b):
    return (jax.nn.silu(_ln(y + b_dw, g, b)),)


def _f_gelu_ln(z, g, b):
    d = z.shape[1] // 2
    zz = jax.nn.gelu(z)
    return zz[:, :d], _ln(zz[:, d:], g, b)


def _f_gelu(y):
    return (jax.nn.gelu(y),)


def _f_merge(o0, o1, o2, l0, l1, l2):
    m = jnp.maximum(jnp.maximum(l0, l1), l2)
    e0, e1, e2 = jnp.exp(l0 - m), jnp.exp(l1 - m), jnp.exp(l2 - m)
    s = e0 + e1 + e2
    pair = 2 * HEAD_DIM
    first_head = lax.broadcasted_iota(jnp.int32, (o0.shape[0], pair), 1) < HEAD_DIM
    cols = []
    for hp in range(o0.shape[1] // pair):
        acc = None
        for o, e in ((o0, e0), (o1, e1), (o2, e2)):
            wgt = e / s
            wp = jnp.where(first_head, wgt[:, 2 * hp:2 * hp + 1], wgt[:, 2 * hp + 1:2 * hp + 2])
            term = wp * o[:, hp * pair:(hp + 1) * pair]
            acc = term if acc is None else acc + term
        cols.append(acc)
    return (jnp.concatenate(cols, axis=1),)


def _f_add(a, b):
    return (a + b,)


def _loss_head(x, tgt, g, *, tr, name):
    T, D = x.shape
    tr = min(tr, T)

    def f(xv, gv, tv):
        y = _f_rms(xv, gv)[0]
        return 0.5 * jnp.mean(jnp.square(y - tv), axis=-1, keepdims=True)

    def body(x_ref, t_ref, g_ref, loss_ref, dx_ref, dxb_ref, dg_ref):
        tv = t_ref[...]
        l, vjp = jax.vjp(lambda xv, gv: f(xv, gv, tv), x_ref[...], g_ref[...])
        dx, dg = vjp(jnp.ones_like(l))
        dx_ref[...] = dx
        dxb_ref[...] = dx.astype(BF16)

        @pl.when(pl.program_id(0) == 0)
        def _():
            loss_ref[...] = jnp.zeros_like(loss_ref)
            dg_ref[...] = jnp.zeros_like(dg_ref)

        loss_ref[...] += jnp.sum(l)
        dg_ref[...] += dg

    return pl.pallas_call(
        body, grid=(T // tr,),
        in_specs=[pl.BlockSpec((tr, D), lambda i: (i, 0)), pl.BlockSpec((tr, D), lambda i: (i, 0)),
                  pl.BlockSpec((1, D), lambda i: (0, 0))],
        out_specs=[pl.BlockSpec((1, 128), lambda i: (0, 0)), pl.BlockSpec((tr, D), lambda i: (i, 0)),
                   pl.BlockSpec((tr, D), lambda i: (i, 0)), pl.BlockSpec((1, D), lambda i: (0, 0))],
        out_shape=[jax.ShapeDtypeStruct((1, 128), F32), jax.ShapeDtypeStruct((T, D), F32),
                   jax.ShapeDtypeStruct((T, D), BF16), jax.ShapeDtypeStruct((1, D), F32)],
        compiler_params=_cp(("arbitrary",)), name=name,
    )(x, tgt, g)


def _adamw(w, g, m, v, *, name):
    R, C = w.shape
    tr = _pick_rows(R, max(8, 2 * 1024 * 1024 // (4 * C)))
    c1 = 1.0 - ADAM_B1 ** ADAM_STEP
    c2 = 1.0 - ADAM_B2 ** ADAM_STEP

    def body(w_ref, g_ref, m_ref, v_ref, d_ref, nm_ref, nv_ref):
        gv = g_ref[...]
        nm = ADAM_B1 * m_ref[...] + (1.0 - ADAM_B1) * gv
        nv = ADAM_B2 * v_ref[...] + (1.0 - ADAM_B2) * jnp.square(gv)
        nm_ref[...] = nm
        nv_ref[...] = nv
        d_ref[...] = -ADAM_LR * ((nm / c1) / (jnp.sqrt(nv / c2) + ADAM_EPS) + ADAM_WD * w_ref[...])

    spec = pl.BlockSpec((tr, C), lambda i: (i, 0))
    return pl.pallas_call(
        body, grid=(R // tr,), in_specs=[spec] * 4, out_specs=[spec] * 3,
        out_shape=[jax.ShapeDtypeStruct((R, C), F32)] * 3,
        compiler_params=_cp(("parallel",)), name=name,
    )(w, g, m, v)


def _s5_prep(a_re, a_im, b_re, b_im, c_re, c_im, log_dt):
    G, N = a_re.shape
    P = b_re.shape[2]
    gpb = 128 // P
    nblk = G // gpb
    dt = jnp.exp(log_dt)[:, None]
    mag = jnp.exp(a_re * dt)
    abr, abi = mag * jnp.cos(a_im * dt), mag * jnp.sin(a_im * dt)
    den = a_re * a_re + a_im * a_im
    nr, ni = abr - 1.0, abi
    qr, qi = (nr * a_re + ni * a_im) / den, (ni * a_re - nr * a_im) / den
    bbr = qr[..., None] * b_re - qi[..., None] * b_im
    bbi = qr[..., None] * b_im + qi[..., None] * b_re
    eye = jnp.eye(gpb, dtype=F32)

    def expand_b(t):
        t = t.reshape(nblk, gpb, N, P).transpose(0, 1, 3, 2)
        return (t[:, :, :, None, :] * eye[None, :, None, :, None]).reshape(nblk, gpb * P, gpb * N)

    def expand_c(t):
        t = t.reshape(nblk, gpb, P, N).transpose(0, 1, 3, 2)
        return (t[:, :, :, None, :] * eye[None, :, None, :, None]).reshape(nblk, gpb * N, gpb * P)

    return (abr.reshape(1, G * N), abi.reshape(1, G * N), expand_b(bbr), expand_b(bbi),
            expand_c(c_re), expand_c(c_im))


def _s5_fwd(h, abr, abi, bre, bim, cre, cim, d, *, B, L, name):
    T, D = h.shape
    S = T // B
    L = min(L, S)
    nc = S // L
    nblk, cb, sb = bre.shape
    GN = abr.shape[1]

    def body(h_ref, ar_ref, ai_ref, bre_ref, bim_ref, cre_ref, cim_ref, d_ref,
             y_ref, yb_ref, xr_ref, xi_ref, er_ref, ei_ref, sr, si, car, cai):
        ci = pl.program_id(1)

        @pl.when(ci == 0)
        def _():
            car[...] = jnp.zeros_like(car)
            cai[...] = jnp.zeros_like(cai)

        for j in range(nblk):
            u = h_ref[:, j * cb:(j + 1) * cb]
            sr[:, j * sb:(j + 1) * sb] = jnp.dot(u, bre_ref[j], preferred_element_type=F32)
            si[:, j * sb:(j + 1) * sb] = jnp.dot(u, bim_ref[j], preferred_element_type=F32)
        ar, ai = ar_ref[...], ai_ref[...]

        def step(t, carry):
            pr, pi = carry
            nr = ar * pr - ai * pi + sr[pl.ds(t, 1), :]
            ni = ar * pi + ai * pr + si[pl.ds(t, 1), :]
            sr[pl.ds(t, 1), :] = nr
            si[pl.ds(t, 1), :] = ni
            return nr, ni

        pr, pi = lax.fori_loop(0, L, step, (car[...], cai[...]), unroll=4)
        car[...] = pr
        cai[...] = pi
        er_ref[0] = pr
        ei_ref[0] = pi
        for j in range(nblk):
            xr = sr[:, j * sb:(j + 1) * sb].astype(BF16)
            xi = si[:, j * sb:(j + 1) * sb].astype(BF16)
            xr_ref[:, j * sb:(j + 1) * sb] = xr
            xi_ref[:, j * sb:(j + 1) * sb] = xi
            y = (jnp.dot(xr, cre_ref[j], preferred_element_type=F32)
                 - jnp.dot(xi, cim_ref[j], preferred_element_type=F32))
            u = h_ref[:, j * cb:(j + 1) * cb].astype(F32)
            y = y + d_ref[:, j * cb:(j + 1) * cb] * u
            y_ref[:, j * cb:(j + 1) * cb] = y
            yb_ref[:, j * cb:(j + 1) * cb] = jax.nn.gelu(y).astype(BF16)

    tok = lambda w: pl.BlockSpec((L, w), lambda b, c: (b * nc + c, 0))
    whole = lambda p: pl.BlockSpec(p.shape, lambda b, c, nd=p.ndim: (0,) * nd)
    end = pl.BlockSpec((1, 1, GN), lambda b, c: (b * nc + c, 0, 0))
    return pl.pallas_call(
        body, grid=(B, nc),
        in_specs=[tok(D)] + [whole(p) for p in (abr, abi, bre, bim, cre, cim, d)],
        out_specs=[tok(D), tok(D), tok(GN), tok(GN), end, end],
        out_shape=[jax.ShapeDtypeStruct((T, D), F32), jax.ShapeDtypeStruct((T, D), BF16),
                   jax.ShapeDtypeStruct((T, GN), BF16),
                   jax.ShapeDtypeStruct((T, GN), BF16), jax.ShapeDtypeStruct((B * nc, 1, GN), F32),
                   jax.ShapeDtypeStruct((B * nc, 1, GN), F32)],
        scratch_shapes=[pltpu.VMEM((L, GN), F32), pltpu.VMEM((L, GN), F32),
                        pltpu.VMEM((1, GN), F32), pltpu.VMEM((1, GN), F32)],
        compiler_params=_cp(("arbitrary", "arbitrary")), name=name,
    )(h, abr, abi, bre, bim, cre, cim, d)


def _s5_bwd(dy, h, xr, xi, er, ei, abr, abi, bre, bim, cre, cim, d, *, B, L, name):
    T, D = h.shape
    S = T // B
    L = min(L, S)
    nc = S // L
    nblk, cb, sb = bre.shape
    GN = abr.shape[1]
    dims_nt = (((1,), (1,)), ((), ()))
    dims_tn = (((0,), (0,)), ((), ()))

    def body(dy_ref, h_ref, xr_ref, xi_ref, er_ref, ei_ref, ar_ref, ai_ref, bre_ref, bim_ref,
             cre_ref, cim_ref, d_ref,
             dh_ref, dbre_ref, dbim_ref, dcre_ref, dcim_ref, dar_ref, dai_ref, dd_ref,
             lr, li, car, cai):
        b, cstep = pl.program_id(0), pl.program_id(1)
        ci = nc - 1 - cstep

        @pl.when((b == 0) & (cstep == 0))
        def _():
            for r in (dbre_ref, dbim_ref, dcre_ref, dcim_ref, dar_ref, dai_ref, dd_ref):
                r[...] = jnp.zeros_like(r)

        @pl.when(cstep == 0)
        def _():
            car[...] = jnp.zeros_like(car)
            cai[...] = jnp.zeros_like(cai)

        for j in range(nblk):
            dyj = dy_ref[:, j * cb:(j + 1) * cb].astype(BF16)
            lr[:, j * sb:(j + 1) * sb] = lax.dot_general(dyj, cre_ref[j], dims_nt, preferred_element_type=F32)
            li[:, j * sb:(j + 1) * sb] = -lax.dot_general(dyj, cim_ref[j], dims_nt, preferred_element_type=F32)
        ar, ai = ar_ref[...], ai_ref[...]

        def step(s, carry):
            t = L - 1 - s
            pr, pi = carry
            nr = lr[pl.ds(t, 1), :] + ar * pr + ai * pi
            ni = li[pl.ds(t, 1), :] - ai * pr + ar * pi
            lr[pl.ds(t, 1), :] = nr
            li[pl.ds(t, 1), :] = ni
            return nr, ni

        pr, pi = lax.fori_loop(0, L, step, (car[...], cai[...]), unroll=4)
        car[...] = pr
        cai[...] = pi
        has_prev = (ci > 0).astype(F32)
        first_row = lax.broadcasted_iota(jnp.int32, (L, sb), 0) == 0
        for j in range(nblk):
            cs = slice(j * cb, (j + 1) * cb)
            ss = slice(j * sb, (j + 1) * sb)
            lrj, lij = lr[:, ss], li[:, ss]
            xrj, xij = xr_ref[:, ss], xi_ref[:, ss]
            pr_j = jnp.where(first_row, er_ref[0][:, ss] * has_prev, pltpu.roll(xrj.astype(F32), 1, 0))
            pi_j = jnp.where(first_row, ei_ref[0][:, ss] * has_prev, pltpu.roll(xij.astype(F32), 1, 0))
            dar_ref[:, ss] += jnp.sum(lrj * pr_j + lij * pi_j, axis=0, keepdims=True)
            dai_ref[:, ss] += jnp.sum(lij * pr_j - lrj * pi_j, axis=0, keepdims=True)
            lrb, lib = lrj.astype(BF16), lij.astype(BF16)
            hj = h_ref[:, cs]
            dyf = dy_ref[:, cs]
            dyj = dyf.astype(BF16)
            dbre_ref[j] += lax.dot_general(hj, lrb, dims_tn, preferred_element_type=F32)
            dbim_ref[j] += lax.dot_general(hj, lib, dims_tn, preferred_element_type=F32)
            dcre_ref[j] += lax.dot_general(xrj, dyj, dims_tn, preferred_element_type=F32)
            dcim_ref[j] -= lax.dot_general(xij, dyj, dims_tn, preferred_element_type=F32)
            du = (lax.dot_general(lrb, bre_ref[j], dims_nt, preferred_element_type=F32)
                  + lax.dot_general(lib, bim_ref[j], dims_nt, preferred_element_type=F32))
            dh_ref[:, cs] = du + d_ref[:, cs] * dyf
            dd_ref[:, cs] += jnp.sum(dyf * hj.astype(F32), axis=0, keepdims=True)

    tok = lambda w: pl.BlockSpec((L, w), lambda b, c: (b * nc + nc - 1 - c, 0))
    whole = lambda p: pl.BlockSpec(p.shape, lambda b, c, nd=p.ndim: (0,) * nd)
    prev_end = pl.BlockSpec((1, 1, GN), lambda b, c: (b * nc + jnp.maximum(nc - 2 - c, 0), 0, 0))
    params = (abr, abi, bre, bim, cre, cim, d)
    acc_shapes = [bre.shape, bim.shape, cre.shape, cim.shape, abr.shape, abi.shape, d.shape]
    out = pl.pallas_call(
        body, grid=(B, nc),
        in_specs=[tok(D), tok(D), tok(GN), tok(GN), prev_end, prev_end] + [whole(p) for p in params],
        out_specs=[tok(D)] + [pl.BlockSpec(s, lambda b, c, nd=len(s): (0,) * nd) for s in acc_shapes],
        out_shape=[jax.ShapeDtypeStruct((T, D), F32)] + [jax.ShapeDtypeStruct(s, F32) for s in acc_shapes],
        scratch_shapes=[pltpu.VMEM((L, GN), F32), pltpu.VMEM((L, GN), F32),
                        pltpu.VMEM((1, GN), F32), pltpu.VMEM((1, GN), F32)],
        compiler_params=_cp(("arbitrary", "arbitrary")), name=name,
    )(dy, h, xr, xi, er, ei, *params)
    return out


def _conv_fwd(zp, w, *, R, tc, name):
    B, SP, C = zp.shape
    S = SP - CONV_HALO
    R, tc = min(R, S), min(tc, C)

    def body(z_ref, w_ref, y_ref):
        def chunk(ci, _):
            start = pl.multiple_of(ci * R, 8)
            ze = z_ref[pl.ds(start, R + CONV_HALO), :]
            acc = jnp.zeros((R, tc), F32)
            for m in range(CONV_WIDTH):
                k = CONV_WIDTH - 1 - m
                sh = ze if m == 0 else pltpu.roll(ze, m, 0)
                acc = acc + w_ref[k:k + 1, :] * sh[CONV_HALO:, :]
            y_ref[pl.ds(start, R), :] = acc
            return 0

        lax.fori_loop(0, S // R, chunk, 0)

    return pl.pallas_call(
        body, grid=(B, C // tc),
        in_specs=[pl.BlockSpec((None, SP, tc), lambda b, c: (b, 0, c)),
                  pl.BlockSpec((32, tc), lambda b, c: (0, c))],
        out_specs=pl.BlockSpec((None, S, tc), lambda b, c: (b, 0, c)),
        out_shape=jax.ShapeDtypeStruct((B, S, C), F32),
        compiler_params=_cp(("parallel", "parallel")), name=name,
    )(zp, w)


def _conv_bwd(zp, dyp, w, *, R, tc, name):
    B, SP, C = zp.shape
    S = SP - CONV_HALO
    R, tc = min(R, S), min(tc, C)

    def body(z_ref, dy_ref, w_ref, dz_ref, dw_ref):
        @pl.when(pl.program_id(1) == 0)
        def _():
            dw_ref[...] = jnp.zeros_like(dw_ref)

        def chunk(ci, _):
            start = pl.multiple_of(ci * R, 8)
            zc = z_ref[pl.ds(start + CONV_HALO, R), :]
            de = dy_ref[pl.ds(start, R + CONV_HALO), :]
            acc = jnp.zeros((R, tc), F32)
            for m in range(CONV_WIDTH):
                k = CONV_WIDTH - 1 - m
                ds_ = (de if m == 0 else pltpu.roll(de, R + CONV_HALO - m, 0))[:R, :]
                acc = acc + w_ref[k:k + 1, :] * ds_
                dw_ref[k:k + 1, :] += jnp.sum(ds_ * zc, axis=0, keepdims=True)
            dz_ref[pl.ds(start, R), :] = acc
            return 0

        lax.fori_loop(0, S // R, chunk, 0)

    return pl.pallas_call(
        body, grid=(C // tc, B),
        in_specs=[pl.BlockSpec((None, SP, tc), lambda c, b: (b, 0, c)),
                  pl.BlockSpec((None, SP, tc), lambda c, b: (b, 0, c)),
                  pl.BlockSpec((32, tc), lambda c, b: (0, c))],
        out_specs=[pl.BlockSpec((None, S, tc), lambda c, b: (b, 0, c)),
                   pl.BlockSpec((32, tc), lambda c, b: (0, c))],
        out_shape=[jax.ShapeDtypeStruct((B, S, C), F32), jax.ShapeDtypeStruct((32, C), F32)],
        compiler_params=_cp(("parallel", "arbitrary")), name=name,
    )(zp, dyp, w)


def _gmlp_fwd(u, vn, ws, bcol, *, nck, name):
    T, E = u.shape
    H = ws.shape[0]
    he = E // H
    rows = nck * GMLP_CHUNK
    rows = min(rows, T)
    n_in = rows // GMLP_CHUNK

    def body(u_ref, v_ref, ws_ref, b_ref, o_ref):
        for c in range(n_in):
            rs = slice(c * GMLP_CHUNK, (c + 1) * GMLP_CHUNK)
            for hh in range(H):
                cs = slice(hh * he, (hh + 1) * he)
                v2 = jnp.dot(ws_ref[hh], v_ref[rs, cs].astype(BF16), preferred_element_type=F32)
                v2 = v2 + b_ref[:, hh:hh + 1]
                o_ref[rs, cs] = (u_ref[rs, cs] * v2).astype(o_ref.dtype)

    tok = pl.BlockSpec((rows, E), lambda i: (i, 0))
    return pl.pallas_call(
        body, grid=(T // rows,),
        in_specs=[tok, tok, pl.BlockSpec(ws.shape, lambda i: (0, 0, 0)), pl.BlockSpec(bcol.shape, lambda i: (0, 0))],
        out_specs=tok, out_shape=jax.ShapeDtypeStruct((T, E), BF16),
        compiler_params=_cp(("parallel",)), name=name,
    )(u, vn, ws, bcol)


def _gmlp_bwd(duv, u, vn, ws, bcol, *, nck, name):
    T, E = u.shape
    H = ws.shape[0]
    he = E // H
    rows = min(nck * GMLP_CHUNK, T)
    n_in = rows // GMLP_CHUNK
    dims_nt = (((1,), (1,)), ((), ()))
    dims_tn = (((0,), (0,)), ((), ()))

    def body(g_ref, u_ref, v_ref, ws_ref, b_ref, du_ref, dv_ref, dws_ref, db_ref):
        @pl.when(pl.program_id(0) == 0)
        def _():
            dws_ref[...] = jnp.zeros_like(dws_ref)
            db_ref[...] = jnp.zeros_like(db_ref)

        for c in range(n_in):
            rs = slice(c * GMLP_CHUNK, (c + 1) * GMLP_CHUNK)
            for hh in range(H):
                cs = slice(hh * he, (hh + 1) * he)
                vb = v_ref[rs, cs].astype(BF16)
                v2 = jnp.dot(ws_ref[hh], vb, preferred_element_type=F32) + b_ref[:, hh:hh + 1]
                g = g_ref[rs, cs]
                du_ref[rs, cs] = g * v2
                dv2 = g * u_ref[rs, cs]
                dv2b = dv2.astype(BF16)
                dv_ref[rs, cs] = lax.dot_general(ws_ref[hh], dv2b, dims_tn, preferred_element_type=F32)
                dws_ref[hh] += lax.dot_general(dv2b, vb, dims_nt, preferred_element_type=F32)
                db_ref[:, hh:hh + 1] += jnp.sum(dv2, axis=1, keepdims=True)

    tok = pl.BlockSpec((rows, E), lambda i: (i, 0))
    return pl.pallas_call(
        body, grid=(T // rows,),
        in_specs=[tok, tok, tok, pl.BlockSpec(ws.shape, lambda i: (0, 0, 0)), pl.BlockSpec(bcol.shape, lambda i: (0, 0))],
        out_specs=[tok, tok, pl.BlockSpec(ws.shape, lambda i: (0, 0, 0)), pl.BlockSpec(bcol.shape, lambda i: (0, 0))],
        out_shape=[jax.ShapeDtypeStruct((T, E), F32), jax.ShapeDtypeStruct((T, E), F32),
                   jax.ShapeDtypeStruct(ws.shape, F32), jax.ShapeDtypeStruct(bcol.shape, F32)],
        compiler_params=_cp(("arbitrary",)), name=name,
    )(duv, u, vn, ws, bcol)


PAIRS = ATT_HEADS // 2


def _att_consts():
    ji = lax.broadcasted_iota(jnp.int32, (2 * ATT_BLK, ATT_BLK), 0)
    ii = lax.broadcasted_iota(jnp.int32, (2 * ATT_BLK, ATT_BLK), 1)
    dist = ii + ATT_BLK - ji
    band = (dist >= 0) & (dist <= ATT_BLK)
    cur = ji >= ATT_BLK
    first_head = lax.broadcasted_iota(jnp.int32, (ATT_BLK, 2 * HEAD_DIM), 1) < HEAD_DIM
    return band, cur, first_head


def _both_heads(t, first_head):
    zero = jnp.zeros_like(t)
    return jnp.concatenate([jnp.where(first_head, t, zero), jnp.where(first_head, zero, t)], axis=0)


def _att_specs(nbk, offs, nsteps, rev):
    rows = nbk * ATT_BLK
    step = (lambda i: nsteps - 1 - i) if rev else (lambda i: i)
    qoff, koff, voff = offs
    blk = lambda off: pl.BlockSpec((rows, 2 * HEAD_DIM), lambda hp, i: (step(i), off + hp))
    prev = lambda off: pl.BlockSpec((ATT_BLK, 2 * HEAD_DIM), lambda hp, i: (jnp.maximum(step(i) * nbk - 1, 0), off + hp))
    out = pl.BlockSpec((rows, 2 * HEAD_DIM), lambda hp, i: (step(i), hp))
    stat = pl.BlockSpec((2, nbk, ATT_BLK), lambda hp, i: (hp, step(i), 0))
    return [blk(qoff), blk(koff), prev(koff), blk(voff), prev(voff)], out, stat


def _att_fwd(arr, offs, *, nb, nbk, name):
    T = arr.shape[0]
    nbk = min(nbk, T // ATT_BLK)
    nsteps = T // (nbk * ATT_BLK)
    scale = HEAD_DIM ** -0.5
    dims_nt = (((1,), (1,)), ((), ()))
    dims_tn = (((0,), (0,)), ((), ()))

    def body(q_ref, k_ref, kp_ref, v_ref, vp_ref, o_ref, lse_ref):
        i = pl.program_id(1)
        band, cur, first_head = _att_consts()
        for jj in range(nbk):
            rs = slice(jj * ATT_BLK, (jj + 1) * ATT_BLK)
            ps = slice((jj - 1) * ATT_BLK, jj * ATT_BLK)
            has_prev = ((i * nbk + jj) & (nb - 1)) != 0
            valid = band & (cur | has_prev)
            kk = jnp.concatenate([kp_ref[...] if jj == 0 else k_ref[ps, :], k_ref[rs, :]], axis=0)
            vv = jnp.concatenate([vp_ref[...] if jj == 0 else v_ref[ps, :], v_ref[rs, :]], axis=0)
            q2 = _both_heads(q_ref[rs, :], first_head)
            st = lax.dot_general(kk, q2, dims_nt, preferred_element_type=F32) * scale
            st = jnp.where(jnp.concatenate([valid, valid], axis=1), st, MASK_VALUE)
            m = jnp.max(st, axis=0, keepdims=True)
            p = jnp.exp(st - m)
            l = jnp.sum(p, axis=0, keepdims=True)
            lse = m + jnp.log(l)
            lse_ref[0, jj:jj + 1, :] = lse[:, :ATT_BLK]
            lse_ref[1, jj:jj + 1, :] = lse[:, ATT_BLK:]
            pn = (p / l).astype(BF16)
            o2 = lax.dot_general(pn, vv, dims_tn, preferred_element_type=F32)
            o_ref[rs, :] = jnp.where(first_head, o2[:ATT_BLK], o2[ATT_BLK:])

    ins, out, stat = _att_specs(nbk, offs, nsteps, False)
    return pl.pallas_call(
        body, grid=(PAIRS, nsteps), in_specs=ins, out_specs=[out, stat],
        out_shape=[jax.ShapeDtypeStruct((T, ATT_W), F32), jax.ShapeDtypeStruct((ATT_HEADS, T // ATT_BLK, ATT_BLK), F32)],
        compiler_params=_cp(("parallel", "parallel")), name=name,
    )(arr, arr, arr, arr, arr)


def _att_bwd(arr, offs, do, lse, dlse, *, nb, nbk, name):
    T = arr.shape[0]
    nbk = min(nbk, T // ATT_BLK)
    nsteps = T // (nbk * ATT_BLK)
    scale = HEAD_DIM ** -0.5
    dims_nt = (((1,), (1,)), ((), ()))
    dims_tn = (((0,), (0,)), ((), ()))

    def body(q_ref, k_ref, kp_ref, v_ref, vp_ref, do_ref, lse_ref, dlse_ref, dq_ref, dk_ref, dv_ref, ck, cv):
        step = pl.program_id(1)
        i = nsteps - 1 - step
        band, cur, first_head = _att_consts()

        @pl.when(step == 0)
        def _():
            ck[...] = jnp.zeros_like(ck)
            cv[...] = jnp.zeros_like(cv)

        carry_k, carry_v = ck[...], cv[...]
        for jj in reversed(range(nbk)):
            rs = slice(jj * ATT_BLK, (jj + 1) * ATT_BLK)
            ps = slice((jj - 1) * ATT_BLK, jj * ATT_BLK)
            has_prev = ((i * nbk + jj) & (nb - 1)) != 0
            valid = band & (cur | has_prev)
            kk = jnp.concatenate([kp_ref[...] if jj == 0 else k_ref[ps, :], k_ref[rs, :]], axis=0)
            vv = jnp.concatenate([vp_ref[...] if jj == 0 else v_ref[ps, :], v_ref[rs, :]], axis=0)
            q2 = _both_heads(q_ref[rs, :], first_head)
            do2 = _both_heads(do_ref[rs, :].astype(BF16), first_head)
            lse = jnp.concatenate([lse_ref[0, jj:jj + 1, :], lse_ref[1, jj:jj + 1, :]], axis=1)
            dlse = jnp.concatenate([dlse_ref[0, jj:jj + 1, :], dlse_ref[1, jj:jj + 1, :]], axis=1)
            st = lax.dot_general(kk, q2, dims_nt, preferred_element_type=F32) * scale
            st = jnp.where(jnp.concatenate([valid, valid], axis=1), st, MASK_VALUE)
            p = jnp.exp(st - lse)
            dp = lax.dot_general(vv, do2, dims_nt, preferred_element_type=F32)
            delta = jnp.sum(p * dp, axis=0, keepdims=True)
            dsb = (p * (dp - delta + dlse) * scale).astype(BF16)
            dq2 = lax.dot_general(dsb, kk, dims_tn, preferred_element_type=F32)
            dkk = jnp.dot(dsb, q2, preferred_element_type=F32)
            dvv = jnp.dot(p.astype(BF16), do2, preferred_element_type=F32)
            dq_ref[rs, :] = jnp.where(first_head, dq2[:ATT_BLK], dq2[ATT_BLK:]).astype(dq_ref.dtype)
            dk_ref[rs, :] = (dkk[ATT_BLK:] + carry_k).astype(dk_ref.dtype)
            dv_ref[rs, :] = (dvv[ATT_BLK:] + carry_v).astype(dv_ref.dtype)
            carry_k, carry_v = dkk[:ATT_BLK], dvv[:ATT_BLK]
        ck[...] = carry_k
        cv[...] = carry_v

    ins, out, stat = _att_specs(nbk, offs, nsteps, True)
    return pl.pallas_call(
        body, grid=(PAIRS, nsteps), in_specs=ins + [out, stat, stat], out_specs=[out] * 3,
        out_shape=[jax.ShapeDtypeStruct((T, ATT_W), BF16)] * 3,
        scratch_shapes=[pltpu.VMEM((ATT_BLK, 2 * HEAD_DIM), F32), pltpu.VMEM((ATT_BLK, 2 * HEAD_DIM), F32)],
        compiler_params=_cp(("arbitrary", "arbitrary")), name=name,
    )(arr, arr, arr, arr, arr, do, lse, dlse)


def _deinterleave(t, B, S, dil):
    if dil == 1:
        return t
    return t.reshape((B, S // dil, dil) + t.shape[1:]).swapaxes(1, 2).reshape(t.shape)


def _interleave(t, B, S, dil):
    if dil == 1:
        return t
    return t.reshape((B, dil, S // dil) + t.shape[1:]).swapaxes(1, 2).reshape(t.shape)


def _stats_to_tokens(lse, B, S, dil):
    return _interleave(lse.reshape(lse.shape[0], -1).T, B, S, dil)


def _stats_from_tokens(dl, B, S, dil):
    return _deinterleave(dl, B, S, dil).T.reshape(dl.shape[1], -1, ATT_BLK)


def _mesh_pos():
    return lax.axis_index("x"), lax.axis_index("y"), lax.axis_index("c")


def _allgather8(xs, *, name):
    m_per, n = xs.shape

    def body(x_ref, out_ref, send_sems, recv_sems, local_sem):
        x, y, c = _mesh_pos()
        me, sibling = (x, y, c), (x, y, 1 - c)
        chips = [(1 - x, y), (x, 1 - y), (1 - x, 1 - y)]

        def rows(px, py, pc):
            return out_ref.at[pl.ds((4 * px + 2 * py + pc) * m_per, m_per), :]

        def copy(k, block, to, src=None):
            return pltpu.make_async_remote_copy(
                src_ref=rows(*block) if src is None else src, dst_ref=rows(*block),
                send_sem=send_sems.at[k], recv_sem=recv_sems.at[k], device_id=to, device_id_type=MESH)

        mine = pltpu.make_async_copy(x_ref, rows(*me), local_sem)
        mine.start()
        first = [copy(0, me, sibling, src=x_ref)]
        first += [copy(1 + j, me, (*chip, c), src=x_ref) for j, chip in enumerate(chips)]
        for cp in first:
            cp.start()
        passed = [copy(4 + j, (*chip, c), sibling) for j, chip in enumerate(chips)]
        for j, chip in enumerate(chips):
            copy(1 + j, (*chip, c), me).wait_recv()
            passed[j].start()
        copy(0, sibling, me).wait_recv()
        for j, chip in enumerate(chips):
            copy(4 + j, (*chip, 1 - c), me).wait_recv()
        for cp in first + passed:
            cp.wait_send()
        mine.wait()

    return pl.pallas_call(
        body, out_shape=jax.ShapeDtypeStruct((8 * m_per, n), xs.dtype),
        in_specs=[pl.BlockSpec(memory_space=pltpu.VMEM)], out_specs=pl.BlockSpec(memory_space=pltpu.VMEM),
        scratch_shapes=[pltpu.SemaphoreType.DMA((7,)), pltpu.SemaphoreType.DMA((7,)), pltpu.SemaphoreType.DMA],
        compiler_params=pltpu.CompilerParams(vmem_limit_bytes=VMEM_LIMIT), name=name,
    )(xs)


def _hbm_call(body, arrays, out_shapes, n_sems, *, name):
    any_spec = pl.BlockSpec(memory_space=pl.ANY)
    return pl.pallas_call(
        body, out_shape=out_shapes, in_specs=[any_spec] * len(arrays), out_specs=[any_spec] * len(out_shapes),
        scratch_shapes=[pltpu.SemaphoreType.DMA((n_sems,)), pltpu.SemaphoreType.DMA((n_sems,))], name=name,
    )(*arrays)


def _other_chips(x, y):
    return [(1 - x, y), (x, 1 - y), (1 - x, 1 - y)]


def _allgather_chips(ws, *, name):
    n = len(ws)

    def body(*refs):
        ins, outs, (send_sems, recv_sems) = refs[:n], refs[n:2 * n], refs[2 * n:]
        x, y, c = _mesh_pos()
        chips = _other_chips(x, y)

        def copy(a, k, px, py, half, to, src=None):
            slot = outs[a].at[2 * px + py, half]
            return pltpu.make_async_remote_copy(
                src_ref=slot if src is None else src, dst_ref=slot,
                send_sem=send_sems.at[6 * a + k], recv_sem=recv_sems.at[6 * a + k], device_id=to, device_id_type=MESH)

        first = [copy(a, j, x, y, c, (*chip, c), src=ins[a].at[c]) for a in range(n) for j, chip in enumerate(chips)]
        for cp in first:
            cp.start()
        passed = []
        for j, chip in enumerate(chips):
            for a in range(n):
                copy(a, j, *chip, c, (x, y, c)).wait_recv()
                passed.append(copy(a, 3 + j, *chip, c, (x, y, 1 - c)))
                passed[-1].start()
        for j, chip in enumerate(chips):
            for a in range(n):
                copy(a, 3 + j, *chip, 1 - c, (x, y, c)).wait_recv()
        for cp in first + passed:
            cp.wait_send()

    return _hbm_call(body, ws, [jax.ShapeDtypeStruct((N_CHIPS,) + w.shape, w.dtype) for w in ws], 6 * n, name=name)


def _split_start(srcs, lands, after, issue, n_sems, *, name):
    ns, nl = len(srcs), len(lands)
    hbm, sem = pl.BlockSpec(memory_space=pltpu.HBM), pl.BlockSpec(memory_space=pltpu.SEMAPHORE)
    extra = [] if after is None else [after]

    def body(*refs):
        n_in = ns + nl + len(extra)
        send_sems, recv_sems = refs[n_in], refs[n_in + 1]
        issue(refs[:ns], refs[ns:ns + nl], send_sems, recv_sems)
        refs[-1][...] = jnp.zeros_like(refs[-1])

    arrays = [pltpu.with_memory_space_constraint(a, pltpu.HBM) for a in list(srcs) + list(lands)]
    out = pl.pallas_call(
        body, name=name,
        out_shape=(pltpu.SemaphoreType.DMA((n_sems,)), pltpu.SemaphoreType.DMA((n_sems,)),
                   *[pltpu.HBM(a.shape, a.dtype) for a in arrays], jax.ShapeDtypeStruct((8, 128), F32)),
        in_specs=[hbm] * (ns + nl) + [pl.BlockSpec(memory_space=pl.ANY)] * len(extra),
        out_specs=(sem, sem, *[hbm] * (ns + nl), pl.BlockSpec(memory_space=pltpu.VMEM)),
        input_output_aliases={i: 2 + i for i in range(ns + nl)},
        compiler_params=pltpu.CompilerParams(has_side_effects=pltpu.SideEffectType.DATAFLOW_SIDE_EFFECTING),
    )(*arrays, *extra)
    return out[0], out[1], list(out[2:2 + ns]), list(out[2 + ns:2 + ns + nl]), out[-1]


def _split_wait(send_sems, recv_sems, srcs, lands, after, waits, *, name):
    ns, nl = len(srcs), len(lands)
    hbm, sem = pl.BlockSpec(memory_space=pltpu.HBM), pl.BlockSpec(memory_space=pltpu.SEMAPHORE)

    def body(*refs):
        waits(refs[:ns], refs[ns:ns + nl], refs[ns + nl], refs[ns + nl + 1])

    out = pl.pallas_call(
        body, name=name,
        out_shape=tuple(pltpu.HBM(a.shape, a.dtype) for a in list(srcs) + list(lands)),
        in_specs=[hbm] * (ns + nl) + [sem, sem, pl.BlockSpec(memory_space=pl.ANY)],
        out_specs=tuple([hbm] * (ns + nl)),
        input_output_aliases={i: i for i in range(ns + nl)},
        compiler_params=pltpu.CompilerParams(has_side_effects=pltpu.SideEffectType.DATAFLOW_SIDE_EFFECTING),
    )(*srcs, *lands, send_sems, recv_sems, after)
    return list(out[:ns]), list(out[ns:])


def _gather_start(halves, after, *, name):
    n = len(halves)
    lands = [lax.empty((N_CHIPS,) + h.shape, h.dtype) for h in halves]

    def issue(srcs, dsts, send_sems, recv_sems):
        x, y, c = _mesh_pos()
        me = 2 * x + y
        for a in range(n):
            for j, (px, py) in enumerate(_other_chips(x, y)):
                for cc in range(2):
                    pltpu.make_async_remote_copy(
                        src_ref=srcs[a].at[c], dst_ref=dsts[a].at[me, c],
                        send_sem=send_sems.at[6 * a + 2 * j + cc], recv_sem=recv_sems.at[6 * a + 2 * j + c],
                        device_id=(px, py, cc), device_id_type=MESH).start()

    return _split_start(halves, lands, after, issue, 6 * n, name=name)


def _gather_wait(started, after, *, name):
    send_sems, recv_sems, halves, lands = started
    n = len(halves)

    def waits(srcs, dsts, send_sems, recv_sems):
        x, y, c = _mesh_pos()
        me = 2 * x + y
        for a in range(n):
            for j, (px, py) in enumerate(_other_chips(x, y)):
                for cc in range(2):
                    pltpu.make_async_remote_copy(
                        src_ref=srcs[a].at[cc], dst_ref=dsts[a].at[2 * px + py, cc],
                        send_sem=send_sems.at[6 * a + 2 * j + cc], recv_sem=recv_sems.at[6 * a + 2 * j + cc],
                        device_id=(px, py, cc), device_id_type=MESH).wait_recv()
        for a in range(n):
            for j, (px, py) in enumerate(_other_chips(x, y)):
                for cc in range(2):
                    pltpu.make_async_remote_copy(
                        src_ref=srcs[a].at[c], dst_ref=dsts[a].at[me, c],
                        send_sem=send_sems.at[6 * a + 2 * j + cc], recv_sem=recv_sems.at[6 * a + 2 * j + c],
                        device_id=(px, py, cc), device_id_type=MESH).wait_send()

    return _split_wait(send_sems, recv_sems, halves, lands, after, waits, name=name)


def _reduce_plan_loops(plans, chip, c, fn):
    for a, plan in enumerate(plans):
        for h, cc in plan:
            for k in range(N_CHIPS):
                fn(a, h, k, cc, jnp.logical_or(chip != k, c != cc))


def _reduce_start(srcs, lands, plans, after, *, name):
    def issue(src_refs, land_refs, send_sems, recv_sems):
        x, y, c = _mesh_pos()
        chip = 2 * x + y
        my_id = 2 * chip + c

        def send(a, h, k, cc, is_other):
            @pl.when(is_other)
            def _():
                pltpu.make_async_remote_copy(
                    src_ref=src_refs[a].at[h, k], dst_ref=land_refs[a].at[my_id],
                    send_sem=send_sems.at[8 * a + 2 * k + cc], recv_sem=recv_sems.at[8 * a + my_id],
                    device_id=(k // 2, k % 2, cc), device_id_type=MESH).start()

        _reduce_plan_loops(plans, chip, c, send)

    return _split_start(srcs, lands, after, issue, 8 * len(srcs), name=name)


def _reduce_wait(started, plans, after, *, name):
    send_sems, recv_sems, srcs, lands = started

    def waits(src_refs, land_refs, send_sems, recv_sems):
        x, y, c = _mesh_pos()
        chip = 2 * x + y
        my_id = 2 * chip + c
        for a, plan in enumerate(plans):
            for h, cc in plan:
                for s in range(2 * N_CHIPS):
                    @pl.when(jnp.logical_and(c == cc, my_id != s))
                    def _(a=a, h=h, s=s):
                        pltpu.make_async_remote_copy(
                            src_ref=src_refs[a].at[h, 0], dst_ref=land_refs[a].at[s],
                            send_sem=send_sems.at[8 * a + s], recv_sem=recv_sems.at[8 * a + s],
                            device_id=(s // 4, (s // 2) % 2, s % 2), device_id_type=MESH).wait_recv()

        def sent(a, h, k, cc, is_other):
            @pl.when(is_other)
            def _():
                pltpu.make_async_remote_copy(
                    src_ref=src_refs[a].at[h, k], dst_ref=land_refs[a].at[my_id],
                    send_sem=send_sems.at[8 * a + 2 * k + cc], recv_sem=recv_sems.at[8 * a + my_id],
                    device_id=(k // 2, k % 2, cc), device_id_type=MESH).wait_send()

        _reduce_plan_loops(plans, chip, c, sent)

    return _split_wait(send_sems, recv_sems, srcs, lands, after, waits, name=name)


def _sum8(land, own, my_id, *, name):
    n_src, R, C = land.shape
    tr = _pick_rows(R, max(8, 1024 * 1024 // (2 * C)))

    def body(id_ref, *refs):
        own_ref, o_ref = refs[n_src], refs[n_src + 1]
        me = id_ref[0]
        acc = None
        for s in range(n_src):
            term = jnp.where(me == s, own_ref[...], refs[s][...]).astype(F32)
            acc = term if acc is None else acc + term
        o_ref[...] = acc

    return pl.pallas_call(
        body, out_shape=jax.ShapeDtypeStruct((R, C), F32),
        grid_spec=pltpu.PrefetchScalarGridSpec(
            num_scalar_prefetch=1, grid=(R // tr,),
            in_specs=[pl.BlockSpec((None, tr, C), lambda i, idr, s=s: (s, i, 0)) for s in range(n_src)]
            + [pl.BlockSpec((tr, C), lambda i, idr: (i, 0))],
            out_specs=pl.BlockSpec((tr, C), lambda i, idr: (i, 0))),
        compiler_params=_cp(("parallel",)), name=name,
    )(my_id.reshape(1).astype(jnp.int32), *([land] * n_src), own)


def _swap_halves(gs, *, name):
    n = len(gs)

    def body(*refs):
        ins, outs, (send_sems, recv_sems) = refs[:n], refs[n:2 * n], refs[2 * n:]
        x, y, c = _mesh_pos()
        cps = [pltpu.make_async_remote_copy(
            src_ref=ins[a].at[1 - c], dst_ref=outs[a], send_sem=send_sems.at[a], recv_sem=recv_sems.at[a],
            device_id=(x, y, 1 - c), device_id_type=MESH) for a in range(n)]
        for cp in cps:
            cp.start()
        for cp in cps:
            cp.wait()

    return _hbm_call(body, gs, [jax.ShapeDtypeStruct(g.shape[1:], g.dtype) for g in gs], n, name=name)


def _scatter_chips(ss, *, name):
    n = len(ss)

    def body(*refs):
        ins, outs, (send_sems, recv_sems) = refs[:n], refs[n:2 * n], refs[2 * n:]
        x, y, c = _mesh_pos()
        me = 2 * x + y
        chips = _other_chips(x, y)

        def copy(a, j, px, py):
            return pltpu.make_async_remote_copy(
                src_ref=ins[a].at[2 * px + py], dst_ref=outs[a].at[me],
                send_sem=send_sems.at[3 * a + j], recv_sem=recv_sems.at[3 * a + j],
                device_id=(px, py, c), device_id_type=MESH)

        def arrival(a, j, px, py):
            return pltpu.make_async_remote_copy(
                src_ref=ins[a].at[me], dst_ref=outs[a].at[2 * px + py],
                send_sem=send_sems.at[3 * a + j], recv_sem=recv_sems.at[3 * a + j],
                device_id=(px, py, c), device_id_type=MESH)

        cps = [copy(a, j, *chip) for a in range(n) for j, chip in enumerate(chips)]
        for cp in cps:
            cp.start()
        for a in range(n):
            for j, chip in enumerate(chips):
                arrival(a, j, *chip).wait_recv()
        for cp in cps:
            cp.wait_send()

    return _hbm_call(body, ss, [jax.ShapeDtypeStruct(s.shape, s.dtype) for s in ss], 3 * n, name=name)


def _share_halves(ts, *, name):
    n = len(ts)

    def body(*refs):
        ins, outs, (send_sems, recv_sems) = refs[:n], refs[n:2 * n], refs[2 * n:]
        x, y, c = _mesh_pos()
        cps = [pltpu.make_async_remote_copy(
            src_ref=ins[a], dst_ref=outs[a].at[c], send_sem=send_sems.at[a], recv_sem=recv_sems.at[a],
            device_id=(x, y, 1 - c), device_id_type=MESH) for a in range(n)]
        for cp in cps:
            cp.start()
        for a in range(n):
            pltpu.make_async_remote_copy(
                src_ref=ins[a], dst_ref=outs[a].at[1 - c], send_sem=send_sems.at[a], recv_sem=recv_sems.at[a],
                device_id=(x, y, 1 - c), device_id_type=MESH).wait_recv()
        for cp in cps:
            cp.wait_send()

    return _hbm_call(body, ts, [jax.ShapeDtypeStruct((2,) + t.shape, t.dtype) for t in ts], n, name=name)


def _half_add(g, ra, core, *, name):
    _, R, C = g.shape
    tr = _pick_rows(R, max(8, 2 * 1024 * 1024 // (4 * C)))

    def body(core_ref, g_ref, ra_ref, o_ref):
        o_ref[...] = (g_ref[...] + ra_ref[...]).astype(o_ref.dtype)

    return pl.pallas_call(
        body, out_shape=jax.ShapeDtypeStruct((R, C), BF16),
        grid_spec=pltpu.PrefetchScalarGridSpec(
            num_scalar_prefetch=1, grid=(R // tr,),
            in_specs=[pl.BlockSpec((None, tr, C), lambda i, cr: (cr[0], i, 0)),
                      pl.BlockSpec((tr, C), lambda i, cr: (i, 0))],
            out_specs=pl.BlockSpec((tr, C), lambda i, cr: (i, 0))),
        compiler_params=_cp(("parallel",)), name=name,
    )(core.reshape(1).astype(jnp.int32), g, ra)


def _sum4(rb, *, name):
    _, R, C = rb.shape
    tr = _pick_rows(R, max(8, 2 * 1024 * 1024 // (4 * C)))

    def body(r0, r1, r2, r3, o_ref):
        f = lambda r: r[...].astype(F32)
        o_ref[...] = ((f(r0) + f(r1)) + f(r2)) + f(r3)

    return pl.pallas_call(
        body, out_shape=jax.ShapeDtypeStruct((R, C), F32), grid=(R // tr,),
        in_specs=[pl.BlockSpec((None, tr, C), lambda i, k=k: (k, i, 0)) for k in range(N_CHIPS)],
        out_specs=pl.BlockSpec((tr, C), lambda i: (i, 0)),
        compiler_params=_cp(("parallel",)), name=name,
    )(rb, rb, rb, rb)


TR = 512
S5_CHUNK = 256


def _rms_fwd(x, g, name):
    return _rowwise(_f_rms, [x], [g], (BF16,), tr=TR, name=name)[0]


def _rms_bwd_epi(dh, x, g, gx):
    r = lax.rsqrt(jnp.mean(x * x, axis=-1, keepdims=True) + EPS)
    xr = x * r
    t = dh * g
    dx = r * (t - xr * jnp.mean(t * xr, axis=-1, keepdims=True)) + gx
    return dx, dx, jnp.sum(dh * xr, axis=0, keepdims=True)


def _mm_rms_bwd(a, w, x, g, gx, *, after=None, name, **kw):
    kw.setdefault("tm", 1024)
    return _mm(a, w, tb=True, epi=_rms_bwd_epi, extras=(x, g, gx), out_dtypes=(F32, BF16, F32), n_row_sums=1,
               tn=x.shape[1], after=after, name=name, **kw)


def _rms_bwd(x, g, dh, gx, name, after=None):
    (dx, dxb), (dg,) = _rowwise_vjp(_f_rms, [x], [g], [dh], [(F32, BF16)], adds={0: gx}, after=after, tr=TR,
                                    name=name)
    return dx, dxb, dg


def _grad_cols(M, Nq):
    def imap(tm, tn):
        hp, per = (M // 2) // tm, Nq // tn
        assert hp * tm * 2 == M and per * tn == Nq, (M, Nq, tm, tn)
        return lambda i, j, k: (i // hp, j // per, i % hp, j % per)
    return (2, N_CHIPS, M // 2, Nq), lambda tm, tn: (None, None, tm, tn), imap, None, M // 2, Nq


def _grad_rows(Mq, N):
    def imap(tm, tn):
        po, hp = Mq // tm, (Mq // 2) // tm
        assert hp * tm * 2 == Mq, (Mq, tm)
        return lambda i, j, k: ((i % po) // hp, i // po, (i % po) % hp, j)
    return (2, N_CHIPS, Mq // 2, N), lambda tm, tn: (None, None, tm, tn), imap, None, Mq // 2, N


def _grad_layer_cols(slot, lh, M, Nq, buf):
    def imap(tm, tn):
        per = Nq // tn
        return lambda i, j, k: (j // per, slot, i, j % per)
    return (N_CHIPS, lh, M, Nq), lambda tm, tn: (None, None, tm, tn), imap, buf, M, Nq


def _grad_layer_rows(slot, lh, Mq, N, buf):
    def imap(tm, tn):
        po = Mq // tm
        return lambda i, j, k: (i // po, slot, i % po, j)
    return (N_CHIPS, lh, Mq, N), lambda tm, tn: (None, None, tm, tn), imap, buf, Mq, N


def _add_then_rms(acc, res, g):
    xo = acc + res
    return xo, _f_rms(xo, g)[0]


def _mlp_fwd(x, h2, w_in, w_out, g_next, li):
    w_in_full = getattr(w_in, "plain", w_in)
    r = _mm(h2, w_in_full, out_dtypes=(BF16,), epi=lambda acc: (jnp.maximum(acc, 0.0),), tm=512, tn=w_in.shape[1],
            name=f"mlp_in_{li}")
    tiles = dict(tm=512, tn=x.shape[1], tk=r.shape[1])
    if g_next is None:
        x_out, h_next = _mm(r, w_out, pro_a=lambda t: t * t, epi=lambda acc, res: (acc + res,), extras=(x,),
                            name=f"mlp_out_{li}", **tiles), None
    else:
        x_out, h_next = _mm(r, w_out, pro_a=lambda t: t * t, epi=_add_then_rms, extras=(x, g_next),
                            out_dtypes=(F32, BF16), name=f"mlp_out_{li}", **tiles)
    return x_out, h_next, (h2, r)


def _mlp_bwd(gx, gxb, x, g, w_in, w_out, saved, li, nl, bufs):
    h2, r = saved
    D, F = w_in.shape
    lh = nl // 2
    da = _mm(gxb, w_out, tb=True, out_dtypes=(BF16,),
             epi=lambda acc, rt: (acc * 2.0 * rt.astype(F32),), extras=(r,), tm=512, tn=F, name=f"mlp_dact_{li}")
    buf_in, buf_out = bufs if bufs is not None else (None, None)
    d_w_out = _mm(r, gxb, ta=True, pro_a=lambda t: t * t, tm=1024, tn=1024, tk=4096, out_dtypes=(BF16,),
                  out=_grad_layer_rows(li % lh, lh, F // N_CHIPS, D, buf_out), name=f"mlp_dwout_{li}")
    d_w_in = _mm(h2, da, ta=True, tm=1024, tn=1024, tk=4096, out_dtypes=(BF16,),
                 out=_grad_layer_cols(li % lh, lh, D, F // N_CHIPS, buf_in), name=f"mlp_dwin_{li}")
    gx_mid, gxb_mid, dg = _mm_rms_bwd(da, getattr(w_in, "plain", w_in), x, g, gx, tm=512, tk=F, name=f"mlp_dh_{li}")
    return gx_mid, gxb_mid, dg, (d_w_in, d_w_out)


def _local_step(x3, tgt3, p, layer_weights, token=None, grads_done=lambda group: None):
    B, S, D = x3.shape
    T = B * S
    x = x3.reshape(T, D)
    grads = {}
    row = lambda v: v.reshape(1, -1)
    p = dict(p)
    nl = p["norm_mlp"].shape[0]
    mlp_in, mlp_out = [None] * nl, [None] * nl

    def fetch(li, after):
        wl = dict(layer_weights(li, after))
        mlp_in[li], mlp_out[li] = wl.pop("mlp_w_in"), wl.pop("mlp_w_out")
        p.update(wl)

    g0 = row(p["norm_mix"][0])
    if token is not None:
        g0 = g0 + token[:1, :1]
    h0 = _rms_fwd(x, g0, "rms_mix_0")
    s5_args = (p["ssm_a_re"][0], p["ssm_a_im"][0], p["ssm_b_re"][0], p["ssm_b_im"][0],
               p["ssm_c_re"][0], p["ssm_c_im"][0], p["ssm_log_dt"][0])
    s5_exp, s5_vjp = jax.vjp(_s5_prep, *s5_args)
    abr, abi, bre, bim, cre, cim = s5_exp
    bre_b, bim_b, cre_b, cim_b = (t.astype(BF16) for t in (bre, bim, cre, cim))
    d_skip = p["ssm_d"]
    ypre, yb, sxr, sxi, ser, sei = _s5_fwd(h0, abr, abi, bre_b, bim_b, cre_b, cim_b, d_skip, B=B, L=S5_CHUNK,
                                           name="s5_fwd")
    fetch(0, yb)
    w_glu = p["ssm_w_glu"]
    z0 = _mm(yb, w_glu, tm=2048, name="s5_glu_mm")
    gm = [row(p["norm_mlp"][i]) for i in range(nl)]
    g1, g2, g3 = (row(p["norm_mix"][i]) for i in range(1, nl))
    x_mid0, hm0 = _rowwise(lambda z, xr, g: _add_then_rms(_f_glu(z)[0], xr, g), [z0, x], [gm[0]], (F32, BF16),
                           tr=TR, name="s5_glu")
    x1, h1, mlp_saved0 = _mlp_fwd(x_mid0, hm0, mlp_in[0], mlp_out[0], g1, 0)

    fetch(1, h1)
    z1 = _mm(h1, p["conv_w_pw1"], tm=2048, name="conv_pw1")
    zg = _rowwise(_f_bias_glu, [z1], [p["conv_b_pw1"]], (F32,), tr=TR, name="conv_glu")[0]
    zp = jnp.pad(zg.reshape(B, S, D), ((0, 0), (CONV_HALO, 0), (0, 0)))
    w_dw = jnp.pad(p["conv_w_dw"], ((0, 32 - CONV_WIDTH), (0, 0)))
    yc = _conv_fwd(zp, w_dw, R=256, tc=128, name="conv_dw").reshape(T, D)
    ln_par = [p["conv_b_dw"], p["conv_ln_g"], p["conv_ln_b"]]
    qc = _rowwise(_f_ln_silu, [yc], ln_par, (BF16,), tr=TR, name="conv_ln_silu")[0]
    x_mid1, hm1 = _mm(qc, p["conv_w_pw2"], epi=lambda acc, bias, res, g: _add_then_rms(acc + bias, res, g),
                      extras=(p["conv_b_pw2"], x1, gm[1]), out_dtypes=(F32, BF16), tn=D, name="conv_pw2")
    x2, h2, mlp_saved1 = _mlp_fwd(x_mid1, hm1, mlp_in[1], mlp_out[1], g2, 1)

    fetch(2, h2)
    z2 = _mm(h2, p["gmlp_w_in"], tm=2048, name="gmlp_in")
    gl_par = [p["gmlp_ln_g"], p["gmlp_ln_b"]]
    gu, gvn = _rowwise(_f_gelu_ln, [z2], gl_par, (F32, F32), tr=TR, name="gmlp_gelu_ln")
    causal = jnp.tril(jnp.ones((GMLP_CHUNK, GMLP_CHUNK), dtype=bool))
    ws_b = jnp.where(causal[None], p["gmlp_w_s"][0], 0.0).astype(BF16)
    bcol = jnp.pad(p["gmlp_b_s"][0].T, ((0, 0), (0, 128 - GMLP_HEADS)))
    uv = _gmlp_fwd(gu, gvn, ws_b, bcol, nck=4, name="gmlp_spatial")
    x_mid2, hm2 = _mm(uv, p["gmlp_w_out"], epi=_add_then_rms, extras=(x2, gm[2]), out_dtypes=(F32, BF16), tn=D,
                      name="gmlp_out")
    x3_, h3, mlp_saved2 = _mlp_fwd(x_mid2, hm2, mlp_in[2], mlp_out[2], g3, 2)

    fetch(3, h3)
    ng = len(ATT_DILS)
    att_in, o_tok, l_tok, lses = [], [], [], []
    offs = (0, PAIRS, 2 * PAIRS)
    for gi, dil in enumerate(ATT_DILS):
        w_g = _ColBlocks(p["attn_w_qkv_plain"], gi, ng, 3, ATT_W)
        arr = _mm(h3, w_g, out_dtypes=(BF16,), tm=2048, tn=ATT_W, name=f"attn_qkv_{gi}")
        arr = _deinterleave(arr, B, S, dil)
        att_in.append((arr, offs))
        og, lg = _att_fwd(arr, offs, nb=S // dil // ATT_BLK, nbk=8, name=f"attn_fwd_{gi}")
        lses.append(lg)
        o_tok.append(_interleave(og, B, S, dil))
        l_tok.append(_stats_to_tokens(lg, B, S, dil))
    merged2 = _rowwise(_f_merge, o_tok + l_tok, [], (BF16,), tr=TR, name="attn_merge")[0]
    x_mid3, hm3 = _mm(merged2, p["attn_w_o_plain"], epi=_add_then_rms, extras=(x3_, gm[3]), out_dtypes=(F32, BF16),
                      tn=D, name="attn_out")
    x4, _, mlp_saved3 = _mlp_fwd(x_mid3, hm3, mlp_in[3], mlp_out[3], None, 3)

    loss_part, gx, gxb, dgf = _loss_head(x4, tgt3.reshape(T, D), row(p["norm_final"]), tr=TR, name="loss_head")
    grads["norm_final"] = dgf.reshape(-1)
    d_norm_mix, d_norm_mlp = [None] * 4, [None] * 4
    Dq = D // N_CHIPS

    gx, gxb, d_norm_mlp[3], mlp_hi = _mlp_bwd(
        gx, gxb, x_mid3, row(p["norm_mlp"][3]), mlp_in[3], mlp_out[3], mlp_saved3, 3, nl, None)
    dmerged = _mm(gxb, p["attn_w_o"], tb=True, name="attn_dmerged")
    grads["attn_w_o"] = _mm(merged2, gxb, ta=True, tm=256, tn=256, tk=4096, out_dtypes=(BF16,), out=_grad_cols(ATT_W, Dq), name="attn_dwo")
    dml, _ = _rowwise_vjp(_f_merge, o_tok + l_tok, [], [dmerged], [F32] * 6, tr=TR, name="attn_merge_bwd")
    pieces = [[None] * ng for _ in range(3)]
    for gi, dil in enumerate(ATT_DILS):
        arr, offs = att_in[gi]
        dqkv_g = _att_bwd(arr, offs, _deinterleave(dml[gi], B, S, dil), lses[gi],
                          _stats_from_tokens(dml[ng + gi], B, S, dil), nb=S // dil // ATT_BLK, nbk=8,
                          name=f"attn_bwd_{gi}")
        for i in range(3):
            pieces[i][gi] = _interleave(dqkv_g[i], B, S, dil)
    dqkv = jnp.concatenate([pieces[i][gi] for i in range(3) for gi in range(ng)], axis=1)
    qkv_w = 3 * ng * ATT_W
    grads["attn_w_qkv"] = _mm(h3, dqkv, ta=True, tm=512, tn=1152, tk=4096, out_dtypes=(BF16,), out=_grad_cols(D, qkv_w // N_CHIPS),
                              name="attn_dwqkv")
    tok = grads_done({n: grads[n] for n in ("attn_w_qkv", "attn_w_o")})
    gx, gxb, d_norm_mix[3] = _mm_rms_bwd(dqkv, p["attn_w_qkv_plain"], x3_, g3, gx, tm=512, tk=qkv_w, after=tok, name="attn_dh")

    gx, gxb, d_norm_mlp[2], mlp_hi = _mlp_bwd(
        gx, gxb, x_mid2, row(p["norm_mlp"][2]), mlp_in[2], mlp_out[2], mlp_saved2, 2, nl, mlp_hi)
    tok = grads_done({"mlp_w_in": (1, mlp_hi[0]), "mlp_w_out": (1, mlp_hi[1])})
    duv = _mm(gxb, p["gmlp_w_out"], tb=True, after=tok, name="gmlp_duv")
    grads["gmlp_w_out"] = _mm(uv, gxb, ta=True, tm=128, tn=1024, tk=4096, out_dtypes=(BF16,), out=_grad_rows(Dq, D), name="gmlp_dwout")
    du, dvn, dws, dbcol = _gmlp_bwd(duv, gu, gvn, ws_b, bcol, nck=4, name="gmlp_spatial_bwd")
    grads["gmlp_w_s"] = jnp.where(causal[None], dws, 0.0)[None]
    grads["gmlp_b_s"] = dbcol[:, :GMLP_HEADS].T[None]
    (dz2,), (dlg, dlb_) = _rowwise_vjp(_f_gelu_ln, [z2], gl_par, [du, dvn], [BF16], tr=TR, name="gmlp_gelu_ln_bwd")
    grads["gmlp_ln_g"], grads["gmlp_ln_b"] = dlg, dlb_
    grads["gmlp_w_in"] = _mm(h2, dz2, ta=True, tm=512, tn=512, tk=4096, out_dtypes=(BF16,), out=_grad_cols(D, 2 * Dq), name="gmlp_dwin")
    tok = grads_done({n: grads[n] for n in ("gmlp_w_in", "gmlp_w_out")})
    gx, gxb, d_norm_mix[2] = _mm_rms_bwd(dz2, getattr(p["gmlp_w_in"], "plain", p["gmlp_w_in"]), x2, g2, gx, tm=512, tk=2 * D, after=tok,
                                         name="gmlp_dh")

    gx, gxb, d_norm_mlp[1], mlp_lo = _mlp_bwd(
        gx, gxb, x_mid1, row(p["norm_mlp"][1]), mlp_in[1], mlp_out[1], mlp_saved1, 1, nl, None)
    dqc = _mm(gxb, p["conv_w_pw2"], tb=True, name="conv_dq")
    grads["conv_w_pw2"] = _mm(qc, gxb, ta=True, tm=128, tn=1024, tk=4096, out_dtypes=(BF16,), out=_grad_rows(Dq, D), name="conv_dwpw2")
    _, (db2,) = _rowwise_vjp(lambda t, b: (t + b,), [gx], [p["conv_b_pw2"]], [gx], [None], tr=TR, name="conv_db2")
    grads["conv_b_pw2"] = db2
    (dyc,), (dbdw, dcg, dcb) = _rowwise_vjp(_f_ln_silu, [yc], ln_par, [dqc], [F32], tr=TR, name="conv_ln_silu_bwd")
    grads["conv_b_dw"], grads["conv_ln_g"], grads["conv_ln_b"] = dbdw, dcg, dcb
    dyp = jnp.pad(dyc.reshape(B, S, D), ((0, 0), (0, CONV_HALO), (0, 0)))
    dzg, dwdw = _conv_bwd(zp, dyp, w_dw, R=256, tc=128, name="conv_dw_bwd")
    grads["conv_w_dw"] = dwdw[:CONV_WIDTH][None]
    (dz1,), (db1,) = _rowwise_vjp(_f_bias_glu, [z1], [p["conv_b_pw1"]], [dzg.reshape(T, D)], [BF16], tr=TR,
                                  name="conv_glu_bwd")
    grads["conv_b_pw1"] = db1
    grads["conv_w_pw1"] = _mm(h1, dz1, ta=True, tm=512, tn=512, tk=4096, out_dtypes=(BF16,), out=_grad_cols(D, 2 * Dq), name="conv_dwpw1")
    tok = grads_done({n: grads[n] for n in ("conv_w_pw1", "conv_w_pw2")})
    gx, gxb, d_norm_mix[1] = _mm_rms_bwd(dz1, getattr(p["conv_w_pw1"], "plain", p["conv_w_pw1"]), x1, g1, gx, tm=512, tk=2 * D,
                                         after=tok, name="conv_dh")

    gx, gxb, d_norm_mlp[0], mlp_lo = _mlp_bwd(
        gx, gxb, x_mid0, row(p["norm_mlp"][0]), mlp_in[0], mlp_out[0], mlp_saved0, 0, nl, mlp_lo)
    tok = grads_done({"mlp_w_in": (0, mlp_lo[0]), "mlp_w_out": (0, mlp_lo[1])})
    (dz0,), _ = _rowwise_vjp(_f_glu, [z0], [], [gx], [BF16], after=tok, tr=TR, name="s5_glu_bwd")
    grads["ssm_w_glu"] = _mm(yb, dz0, ta=True, tm=512, tn=512, tk=4096, out_dtypes=(BF16,), out=_grad_cols(D, 2 * Dq), name="s5_dwglu")
    tok = grads_done({"ssm_w_glu": grads["ssm_w_glu"]})
    dypre = _mm(dz0, getattr(w_glu, "plain", w_glu), tb=True, tm=512, tk=2 * D,
                epi=lambda acc, yp: (jax.vjp(lambda t: jax.nn.gelu(t), yp)[1](acc)[0],), extras=(ypre,), name="s5_dypre")
    dh0, dbre, dbim, dcre, dcim, dabr, dabi, dd = _s5_bwd(
        dypre, h0, sxr, sxi, ser, sei, abr, abi, bre_b, bim_b, cre_b, cim_b, d_skip, B=B, L=S5_CHUNK, name="s5_bwd")
    s5_grads = s5_vjp((dabr, dabi, dbre, dbim, dcre, dcim))
    for nm, gv in zip(("ssm_a_re", "ssm_a_im", "ssm_b_re", "ssm_b_im", "ssm_c_re", "ssm_c_im", "ssm_log_dt"), s5_grads):
        grads[nm] = gv[None]
    grads["ssm_d"] = dd
    gx, _, d_norm_mix[0] = _rms_bwd(x, g0, dh0, gx, "rms_mix_bwd_0", after=tok)

    grads["norm_mix"] = jnp.concatenate(d_norm_mix, axis=0)
    grads["norm_mlp"] = jnp.concatenate(d_norm_mlp, axis=0)
    grads["mlp_w_in"], grads["mlp_w_out"] = (mlp_lo[0], mlp_hi[0]), (mlp_lo[1], mlp_hi[1])
    return loss_part, gx.reshape(B, S, D), grads


WEIGHTS = ['norm_mix', 'norm_mlp', 'norm_final', 'ssm_a_re', 'ssm_a_im', 'ssm_b_re', 'ssm_b_im', 'ssm_c_re',
           'ssm_c_im', 'ssm_d', 'ssm_log_dt', 'ssm_w_glu', 'conv_w_pw1', 'conv_b_pw1', 'conv_w_dw', 'conv_b_dw',
           'conv_ln_g', 'conv_ln_b', 'conv_w_pw2', 'conv_b_pw2', 'gmlp_w_in', 'gmlp_ln_g', 'gmlp_ln_b', 'gmlp_w_s',
           'gmlp_b_s', 'gmlp_w_out', 'attn_w_qkv', 'attn_w_o', 'mlp_w_in', 'mlp_w_out']
BIG_AXIS = {'ssm_w_glu': -1, 'conv_w_pw1': -1, 'conv_w_pw2': -2, 'gmlp_w_in': -1, 'gmlp_w_out': -2,
            'attn_w_qkv': -1, 'attn_w_o': -1, 'mlp_w_in': -1, 'mlp_w_out': -2}
BIG = list(BIG_AXIS)
LAYER_MIXER_WEIGHTS = (('ssm_w_glu',), ('conv_w_pw1', 'conv_w_pw2'), ('gmlp_w_in', 'gmlp_w_out'), ('attn_w_qkv', 'attn_w_o'))
SMALL_SHARDED = ['conv_b_pw1', 'conv_w_dw', 'conv_b_dw', 'conv_ln_g', 'conv_ln_b', 'conv_b_pw2', 'gmlp_ln_g', 'gmlp_ln_b']
SMALL_REPL = [n for n in WEIGHTS if n not in BIG_AXIS and n not in SMALL_SHARDED]
SMALL = SMALL_REPL + SMALL_SHARDED
LANES = 128
FLAT_COLS = 1024


def _pack(arrs, cols, row_mult):
    flat = jnp.concatenate([a.reshape(-1) for a in arrs])
    per = cols * row_mult
    n = -(-flat.shape[0] // per) * per
    return jnp.pad(flat, (0, n - flat.shape[0])).reshape(n // cols, cols)


def _unpack(flat2d, shapes):
    flat = flat2d.reshape(-1)
    out, off = [], 0
    for s in shapes:
        n = int(np.prod(s))
        out.append(flat[off:off + n].reshape(s))
        off += n
    return out


def _as_halves(shard):
    if shard.shape[0] == 1:
        shard = shard[0]
    return shard.reshape((2, shard.shape[0] // 2) + shard.shape[1:])


def _stored_weight(name, arr):
    kind = "cols" if BIG_AXIS[name] == -1 else "rows"
    if arr.shape[1] > 1:
        return [_Stored(arr, kind, lead=(li,)) for li in range(arr.shape[1])]
    arr = arr[:, 0]
    if kind == "rows":
        return arr.reshape(-1, arr.shape[-1])
    return _Stored(arr, kind)


def kernel(x, norm_mix, norm_mlp, norm_final, ssm_a_re, ssm_a_im, ssm_b_re, ssm_b_im, ssm_c_re, ssm_c_im, ssm_d, ssm_log_dt, ssm_w_glu, conv_w_pw1, conv_b_pw1, conv_w_dw, conv_b_dw, conv_ln_g, conv_ln_b, conv_w_pw2, conv_b_pw2, gmlp_w_in, gmlp_ln_g, gmlp_ln_b, gmlp_w_s, gmlp_b_s, gmlp_w_out, attn_w_qkv, attn_w_o, mlp_w_in, mlp_w_out, loss_target, m_norm_mix, m_norm_mlp, m_norm_final, m_ssm_a_re, m_ssm_a_im, m_ssm_b_re, m_ssm_b_im, m_ssm_c_re, m_ssm_c_im, m_ssm_d, m_ssm_log_dt, m_ssm_w_glu, m_conv_w_pw1, m_conv_b_pw1, m_conv_w_dw, m_conv_b_dw, m_conv_ln_g, m_conv_ln_b, m_conv_w_pw2, m_conv_b_pw2, m_gmlp_w_in, m_gmlp_ln_g, m_gmlp_ln_b, m_gmlp_w_s, m_gmlp_b_s, m_gmlp_w_out, m_attn_w_qkv, m_attn_w_o, m_mlp_w_in, m_mlp_w_out, v_norm_mix, v_norm_mlp, v_norm_final, v_ssm_a_re, v_ssm_a_im, v_ssm_b_re, v_ssm_b_im, v_ssm_c_re, v_ssm_c_im, v_ssm_d, v_ssm_log_dt, v_ssm_w_glu, v_conv_w_pw1, v_conv_b_pw1, v_conv_w_dw, v_conv_b_dw, v_conv_ln_g, v_conv_ln_b, v_conv_w_pw2, v_conv_b_pw2, v_gmlp_w_in, v_gmlp_ln_g, v_gmlp_ln_b, v_gmlp_w_s, v_gmlp_b_s, v_gmlp_w_out, v_attn_w_qkv, v_attn_w_o, v_mlp_w_in, v_mlp_w_out):
    args = dict(locals())
    w = {n: args[n] for n in WEIGHTS}
    m = {n: args["m_" + n] for n in WEIGHTS}
    v = {n: args["v_" + n] for n in WEIGHTS}
    chip = 2 * lax.axis_index("x") + lax.axis_index("y")
    core = lax.axis_index("c")

    big_shapes = [w[n].shape for n in BIG]
    started, token = [], None
    for li, mixer in enumerate(LAYER_MIXER_WEIGHTS):
        names = list(mixer) + ["mlp_w_in", "mlp_w_out"]
        shards = [w[n][0] for n in mixer] + [w["mlp_w_in"][li], w["mlp_w_out"][li]]
        halves = [s.astype(BF16).reshape((2, s.shape[0] // 2) + s.shape[1:]) for s in shards]
        send_sems, recv_sems, halves, lands, token = _gather_start(halves, token, name=f"gather_start_{li}")
        started.append((names, (send_sems, recv_sems, halves, lands)))

    def layer_weights(li, after):
        names, st = started[li]
        halves, lands = _gather_wait(st, after, name=f"gather_wait_{li}")
        out = {}
        for n, h, arr in zip(names, halves, lands):
            arr = lax.dynamic_update_index_in_dim(arr, h, chip, axis=0)
            arr = arr.reshape((N_CHIPS, arr.shape[1] * arr.shape[2]) + arr.shape[3:])
            if BIG_AXIS[n] == -1:
                out[n] = _Stored(arr, "cols")
                if n in ("attn_w_qkv", "attn_w_o"):
                    out[n + "_plain"] = jnp.swapaxes(arr, 0, 1).reshape(arr.shape[1], -1)
                out[n].plain = jnp.swapaxes(arr, 0, 1).reshape(arr.shape[1], -1)
            else:
                out[n] = arr.reshape(-1, arr.shape[-1])
        return out

    p = {}
    sm_shapes = [w[n].shape for n in SMALL_SHARDED]
    sflat = _pack([w[n] for n in SMALL_SHARDED], LANES, 8)
    rs = sflat.shape[0]
    sall = _allgather8(sflat, name="gather_small").reshape(8, rs, LANES)
    per_chip = [_unpack(sall[2 * k], sm_shapes) for k in range(N_CHIPS)]
    for i, n in enumerate(SMALL_SHARDED):
        p[n] = jnp.concatenate([per_chip[k][i] for k in range(N_CHIPS)], axis=-1)
    for n in SMALL_REPL:
        p[n] = w[n]
    p['conv_w_dw'] = p['conv_w_dw'][0]

    in_flight, arrived, n_rounds = [], {}, [0]

    def finish_round(after):
        k, names, plans, st = in_flight.pop(0)
        srcs, lands = _reduce_wait(st, plans, after, name=f"grads_wait_{k}")
        for n, plan, src, land in zip(names, plans, srcs, lands):
            arrived.setdefault(n, []).append((plan, src, land))

    def grads_done(group):
        names, srcs, plans, lands = [], [], [], []
        for n, v in group.items():
            if isinstance(v, tuple):
                src, plan = v[1].reshape(1, N_CHIPS, -1, v[1].shape[-1]), ((0, v[0]),)
                while any(n in rd[1] for rd in in_flight):
                    finish_round(src)
            else:
                src, plan = v.reshape(2, N_CHIPS, -1, v.shape[-1]), ((0, 0), (1, 1))
            land = arrived[n][-1][2] if n in arrived else lax.empty((2 * N_CHIPS,) + src.shape[2:], BF16)
            names.append(n), srcs.append(src), plans.append(plan), lands.append(land)
        st = _reduce_start(srcs, lands, plans, None, name=f"grads_start_{n_rounds[0]}")
        in_flight.append((n_rounds[0], names, plans, st[:4]))
        n_rounds[0] += 1
        return st[4]

    loss_part, grad_x, g = _local_step(x, loss_target, p, layer_weights, token, grads_done)
    loss = lax.psum(loss_part[0, 0], ("x", "y", "c"))

    my_id = 2 * chip + core
    while in_flight:
        finish_round(grad_x)
    totals = []
    for n in BIG:
        own = None
        for plan, src, land in arrived[n]:
            slab = lax.dynamic_index_in_dim(src, chip, axis=1, keepdims=False)
            if len(plan) == 2:
                own = lax.dynamic_index_in_dim(slab, core, axis=0, keepdims=False)
            else:
                own = slab[0] if own is None else jnp.where(core == plan[0][1], slab[0], own)
        totals.append(_sum8(arrived[n][-1][2], own, my_id, name="owner_sum_" + n))
    shared = _share_halves(totals, name="grads_share_halves")
    big_grads = {}
    for n, arr, t in zip(BIG, shared, totals):
        arr = lax.dynamic_update_index_in_dim(arr, t[None], core, axis=0)
        big_grads[n] = arr.reshape(w[n].shape)

    small_full_shapes = [g[n].shape for n in SMALL]
    gs = _pack([g[n] for n in SMALL], LANES, 8)
    rg = gs.shape[0]
    gs_all = _allgather8(gs, name="gather_small_grads").reshape(8, rg, LANES)
    gs_sum = _rowwise(lambda *a: (functools.reduce(lambda s, t: s + t, a),), [gs_all[k] for k in range(8)], [], (F32,),
                      tr=rg, name="small_grads_sum")[0]
    small_grads = dict(zip(SMALL, _unpack(gs_sum, small_full_shapes)))
    for n in SMALL:
        small_grads[n] = small_grads[n].reshape(p_shape_full(w[n], -1 if n in SMALL_SHARDED else None))
    for n in SMALL_SHARDED:
        width = w[n].shape[-1]
        small_grads[n] = lax.dynamic_slice_in_dim(small_grads[n], chip * width, width, axis=-1)

    grad, delta, new_m, new_v = {}, {}, {}, {}
    for n in BIG:
        shape = w[n].shape
        two_d = lambda t: t.reshape(-1, shape[-1])
        grad[n] = big_grads[n]
        d_, m_, v_ = _adamw(two_d(w[n]), two_d(grad[n]), two_d(m[n]), two_d(v[n]), name="adamw_" + n)
        delta[n], new_m[n], new_v[n] = d_.reshape(shape), m_.reshape(shape), v_.reshape(shape)
    for n in SMALL:
        shape = w[n].shape
        two_d = lambda t: t.reshape(-1, shape[-1])
        grad[n] = small_grads[n]
        d_, m_, v_ = _adamw(two_d(w[n]), two_d(grad[n]), two_d(m[n]), two_d(v[n]), name="adamw_" + n)
        delta[n], new_m[n], new_v[n] = d_.reshape(shape), m_.reshape(shape), v_.reshape(shape)

    return (loss, grad_x, *[grad[n] for n in WEIGHTS], *[delta[n] for n in WEIGHTS],
            *[new_m[n] for n in WEIGHTS], *[new_v[n] for n in WEIGHTS])


def p_shape_full(shard, axis):
    s = list(shard.shape)
    if axis is not None:
        s[axis] *= N_CHIPS
    return tuple(s)
```

```python
import functools
import math

import jax
import jax.numpy as jnp
import numpy as np
from jax import lax
from jax.experimental import pallas as pl
from jax.experimental.pallas import tpu as pltpu

F32 = jnp.float32
BF16 = jnp.bfloat16
MESH = pl.DeviceIdType.MESH

EPS = 1e-6
SSM_GROUP = 16
SSM_STATE = 64
CONV_WIDTH = 31
CONV_HALO = 32
GMLP_CHUNK = 128
GMLP_HEADS = 4
ATT_DILS = (1, 4, 16)
ATT_BLK = 128
ATT_HEADS = 8
HEAD_DIM = 64
ATT_W = ATT_HEADS * HEAD_DIM
N_CHIPS = 4
ADAM_LR, ADAM_B1, ADAM_B2, ADAM_EPS, ADAM_WD, ADAM_STEP = 1e-3, 0.9, 0.999, 1e-8, 0.01, 10

VMEM_BYTES_V7X = 64 * 1024 * 1024
VMEM_LIMIT = VMEM_BYTES_V7X - 8 * 1024 * 1024
MASK_VALUE = -1e30
LANE_TILE = 128


def _cp(sem=None):
    return pltpu.CompilerParams(dimension_semantics=sem, vmem_limit_bytes=VMEM_LIMIT)


def _pick_tile(total, target):
    for cand in range(min(target, total) // LANE_TILE * LANE_TILE, 0, -LANE_TILE):
        if total % cand == 0:
            return cand
    return total


class _Stored:
    def __init__(self, arr, kind="plain", lead=()):
        self.arr, self.kind, self.lead = arr, kind, tuple(lead)
        r, c = arr.shape[-2:]
        self.shape = (r, c * N_CHIPS) if kind == "cols" else (r * N_CHIPS, c) if kind == "rows" else (r, c)

    def spec(self, br, bc, rc_of):
        lead, nl = self.lead, len(self.lead)
        if self.kind == "plain":
            return pl.BlockSpec((None,) * nl + (br, bc), lambda i, j, k: (*lead, *rc_of(i, j, k)))
        if self.kind == "cols":
            per = self.arr.shape[-1] // bc
            assert per * bc == self.arr.shape[-1]

            def imap(i, j, k):
                r, c = rc_of(i, j, k)
                return (c // per, *lead, r, c % per)
        else:
            per = self.arr.shape[-2] // br
            assert per * br == self.arr.shape[-2]

            def imap(i, j, k):
                r, c = rc_of(i, j, k)
                return (r // per, *lead, r % per, c)
        return pl.BlockSpec((None,) * (nl + 1) + (br, bc), imap)


class _ColBlocks:
    kind = "colblocks"

    def __init__(self, arr, first, stride, count, width):
        self.arr, self.first, self.stride, self.width = arr, first, stride, width
        self.shape = (arr.shape[0], count * width)

    def spec(self, br, bc, rc_of):
        per = self.width // bc
        assert per * bc == self.width

        def imap(i, j, k):
            r, c = rc_of(i, j, k)
            return (r, (self.first + (c // per) * self.stride) * per + c % per)
        return pl.BlockSpec((br, bc), imap)


def _mm(a, b, *, ta=False, tb=False, out_dtypes=(F32,), tm=1024, tn=1024, tk=1024,
        pro_a=None, pro_b=None, epi=None, extras=(), n_row_sums=0, out=None, after=None, name):
    if ta:
        K, M = a.shape
    else:
        M, K = a.shape
    if not isinstance(b, (_Stored, _ColBlocks)):
        b = _Stored(b)
    N, Kb = b.shape if tb else b.shape[::-1]
    assert K == Kb, (a.shape, b.shape, ta, tb)
    col_unit = b.width if b.kind == "colblocks" else b.arr.shape[-1] if b.kind == "cols" else b.shape[1]
    row_unit = b.arr.shape[-2] if b.kind == "rows" else b.shape[0]
    n_unit, k_unit = (row_unit, col_unit) if tb else (col_unit, row_unit)
    m_unit = M
    if out is not None:
        m_unit, n_unit = out[4], math.gcd(n_unit, out[5])
    tm, tn, tk = _pick_tile(m_unit, tm), _pick_tile(n_unit, tn), _pick_tile(k_unit, tk)
    assert not n_row_sums or tn == N
    nk = K // tk
    a_spec = (pl.BlockSpec((tk, tm), lambda i, j, k: (k, i)) if ta
              else pl.BlockSpec((tm, tk), lambda i, j, k: (i, k)))
    b_spec = b.spec(tn, tk, lambda i, j, k: (j, k)) if tb else b.spec(tk, tn, lambda i, j, k: (k, j))
    ex_specs = []
    for e in extras:
        if e.shape[0] == 1:
            ex_specs.append(pl.BlockSpec((1, tn), lambda i, j, k: (0, j)))
        else:
            assert e.shape == (M, N), (e.shape, M, N)
            ex_specs.append(pl.BlockSpec((tm, tn), lambda i, j, k: (i, j)))
    dims = (((0 if ta else 1,), (1 if tb else 0,)), ((), ()))
    n_ex, n_out = len(extras), len(out_dtypes)
    direct = epi is None and n_out == 1 and out_dtypes[0] == F32
    use_acc = nk > 1 and not direct
    operands, aliases, alias_specs = [a, b.arr, *extras], {}, []
    if after is not None:
        operands.append(after)
        alias_specs.append(pl.BlockSpec(memory_space=pl.ANY))
    if out is None:
        n_tile_out = n_out - n_row_sums
        out_specs = ([pl.BlockSpec((tm, tn), lambda i, j, k: (i, j))] * n_tile_out
                     + [pl.BlockSpec((1, tn), lambda i, j, k: (0, j))] * n_row_sums)
        out_shape = ([jax.ShapeDtypeStruct((M, N), dt) for dt in out_dtypes[:n_tile_out]]
                     + [jax.ShapeDtypeStruct((1, N), dt) for dt in out_dtypes[n_tile_out:]])
    else:
        shape, block_fn, imap_fn, alias = out[:4]
        assert n_out == 1
        out_specs = [pl.BlockSpec(block_fn(tm, tn), imap_fn(tm, tn))]
        out_shape = [jax.ShapeDtypeStruct(shape, out_dtypes[0])]
        if alias is not None:
            operands.append(alias)
            aliases = {len(operands) - 1: 0}
            alias_specs.append(pl.BlockSpec(memory_space=pl.ANY))
    n_in = len(operands)

    def finish(r, ex, outs, first_row_tile):
        res = epi(r, *[e[...] for e in ex]) if epi is not None else (r,)
        n_tile_out = n_out - n_row_sums
        for o, v in zip(outs[:n_tile_out], res):
            o[...] = v.astype(o.dtype)
        for o, v in zip(outs[n_tile_out:], res[n_tile_out:]):
            @pl.when(first_row_tile)
            def _(o=o):
                o[...] = jnp.zeros_like(o)
            o[...] += v

    def body(*refs):
        a_ref, b_ref = refs[:2]
        ex = refs[2:2 + n_ex]
        outs = refs[n_in:n_in + n_out]
        first_row_tile = pl.program_id(0) == 0
        at, bt = a_ref[...], b_ref[...]
        if pro_a is not None:
            at = pro_a(at)
        if pro_b is not None:
            bt = pro_b(bt)
        part = lax.dot_general(at, bt, dims, preferred_element_type=F32)
        if nk == 1:
            finish(part, ex, outs, first_row_tile)
            return
        acc = refs[-1] if use_acc else outs[0]
        k = pl.program_id(2)

        @pl.when(k == 0)
        def _():
            acc[...] = part

        @pl.when(k > 0)
        def _():
            acc[...] += part

        if use_acc:
            @pl.when(k == nk - 1)
            def _():
                finish(acc[...], ex, outs, first_row_tile)

    res = pl.pallas_call(
        body, grid=(M // tm, N // tn, nk),
        in_specs=[a_spec, b_spec] + ex_specs + alias_specs,
        out_specs=out_specs, out_shape=out_shape,
        scratch_shapes=[pltpu.VMEM((tm, tn), F32)] if use_acc else [],
        input_output_aliases=aliases,
        compiler_params=_cp(("arbitrary" if n_row_sums else "parallel", "parallel", "arbitrary")), name=name,
    )(*operands)
    return res[0] if n_out == 1 else res


def _to_bf16(t):
    return t.astype(BF16)


def _pick_rows(total, target):
    for cand in range(min(target, total) // 8 * 8, 0, -8):
        if total % cand == 0:
            return cand
    return total


def _rowwise(f, rows, params, out_dtypes, *, tr, name):
    T = rows[0].shape[0]
    tr = _pick_rows(T, tr)
    nr, npar = len(rows), len(params)
    blk = [jax.ShapeDtypeStruct((tr, r.shape[1]), F32) for r in rows]
    blk += [jax.ShapeDtypeStruct(p.shape, F32) for p in params]
    out_avals = jax.eval_shape(f, *blk)

    def body(*refs):
        res = f(*[r[...].astype(F32) for r in refs[:nr + npar]])
        for o, v in zip(refs[nr + npar:], res):
            o[...] = v.astype(o.dtype)

    out = pl.pallas_call(
        body, grid=(T // tr,),
        in_specs=[pl.BlockSpec((tr, r.shape[1]), lambda i: (i, 0)) for r in rows]
        + [pl.BlockSpec(p.shape, lambda i, nd=p.ndim: (0,) * nd) for p in params],
        out_specs=[pl.BlockSpec((tr, o.shape[1]), lambda i: (i, 0)) for o in out_avals],
        out_shape=[jax.ShapeDtypeStruct((T, o.shape[1]), dt) for o, dt in zip(out_avals, out_dtypes)],
        compiler_params=_cp(("parallel",)), name=name,
    )(*rows, *params)
    return out


def _rowwise_vjp(f, rows, params, cots, drow_dtypes, *, adds=None, after=None, tr, name):
    adds = adds or {}
    T = rows[0].shape[0]
    tr = _pick_rows(T, tr)
    nr, npar, nc = len(rows), len(params), len(cots)
    want, want_dt = [], []
    for i, dt in enumerate(drow_dtypes):
        for one in (dt if isinstance(dt, tuple) else (dt,)):
            if one is not None:
                want.append(i)
                want_dt.append(one)
    add_idx = sorted(set(i for i in want if i in adds))
    add_arrays = [adds[i] for i in add_idx]
    na = len(add_arrays)
    extra = [] if after is None else [after]

    def body(*refs):
        ins = [r[...].astype(F32) for r in refs[:nr + npar]]
        cvals = [r[...].astype(F32) for r in refs[nr + npar:nr + npar + nc]]
        avals = refs[nr + npar + nc:nr + npar + nc + na]
        outs = refs[nr + npar + nc + na + len(extra):]
        _, vjp = jax.vjp(f, *ins)
        grads = vjp(tuple(cvals))
        for o, i in zip(outs[:len(want)], want):
            g = grads[i]
            if i in adds:
                g = g + avals[add_idx.index(i)][...].astype(F32)
            o[...] = g.astype(o.dtype)
        step = pl.program_id(0)
        for o, g in zip(outs[len(want):], grads[nr:]):
            @pl.when(step == 0)
            def _(o=o):
                o[...] = jnp.zeros_like(o)
            o[...] += g

    rspec = lambda r: pl.BlockSpec((tr, r.shape[1]), lambda i: (i, 0))
    pspec = lambda p: pl.BlockSpec(p.shape, lambda i, nd=p.ndim: (0,) * nd)
    out = pl.pallas_call(
        body, grid=(T // tr,),
        in_specs=[rspec(r) for r in rows] + [pspec(p) for p in params] + [rspec(c) for c in cots]
        + [rspec(a) for a in add_arrays] + [pl.BlockSpec(memory_space=pl.ANY)] * len(extra),
        out_specs=[rspec(rows[i]) for i in want] + [pspec(p) for p in params],
        out_shape=[jax.ShapeDtypeStruct(rows[i].shape, dt) for i, dt in zip(want, want_dt)]
        + [jax.ShapeDtypeStruct(p.shape, F32) for p in params],
        compiler_params=_cp(("arbitrary",)), name=name,
    )(*rows, *params, *cots, *add_arrays, *extra)
    return out[:len(want)], out[len(want):]


def _f_rms(x, g):
    return (x * lax.rsqrt(jnp.mean(x * x, axis=-1, keepdims=True) + EPS) * g,)


def _ln(x, g, b):
    mu = jnp.mean(x, axis=-1, keepdims=True)
    var = jnp.mean(jnp.square(x - mu), axis=-1, keepdims=True)
    return (x - mu) * lax.rsqrt(var + EPS) * g + b


def _f_glu(z):
    d = z.shape[1] // 2
    return (z[:, :d] * jax.nn.sigmoid(z[:, d:]),)


def _f_bias_glu(z, b):
    return _f_glu(z + b)


def _f_ln_silu(y, b_dw, g, b):
    return (jax.nn.silu(_ln(y + b_dw, g, b)),)


def _f_gelu_ln(z, g, b):
    d = z.shape[1] // 2
    zz = jax.nn.gelu(z)
    return zz[:, :d], _ln(zz[:, d:], g, b)


def _f_gelu(y):
    return (jax.nn.gelu(y),)


def _f_merge(o0, o1, o2, l0, l1, l2):
    m = jnp.maximum(jnp.maximum(l0, l1), l2)
    e0, e1, e2 = jnp.exp(l0 - m), jnp.exp(l1 - m), jnp.exp(l2 - m)
    s = e0 + e1 + e2
    pair = 2 * HEAD_DIM
    first_head = lax.broadcasted_iota(jnp.int32, (o0.shape[0], pair), 1) < HEAD_DIM
    cols = []
    for hp in range(o0.shape[1] // pair):
        acc = None
        for o, e in ((o0, e0), (o1, e1), (o2, e2)):
            wgt = e / s
            wp = jnp.where(first_head, wgt[:, 2 * hp:2 * hp + 1], wgt[:, 2 * hp + 1:2 * hp + 2])
            term = wp * o[:, hp * pair:(hp + 1) * pair]
            acc = term if acc is None else acc + term
        cols.append(acc)
    return (jnp.concatenate(cols, axis=1),)


def _f_add(a, b):
    return (a + b,)


def _loss_head(x, tgt, g, *, tr, name):
    T, D = x.shape
    tr = min(tr, T)

    def f(xv, gv, tv):
        y = _f_rms(xv, gv)[0]
        return 0.5 * jnp.mean(jnp.square(y - tv), axis=-1, keepdims=True)

    def body(x_ref, t_ref, g_ref, loss_ref, dx_ref, dxb_ref, dg_ref):
        tv = t_ref[...]
        l, vjp = jax.vjp(lambda xv, gv: f(xv, gv, tv), x_ref[...], g_ref[...])
        dx, dg = vjp(jnp.ones_like(l))
        dx_ref[...] = dx
        dxb_ref[...] = dx.astype(BF16)

        @pl.when(pl.program_id(0) == 0)
        def _():
            loss_ref[...] = jnp.zeros_like(loss_ref)
            dg_ref[...] = jnp.zeros_like(dg_ref)

        loss_ref[...] += jnp.sum(l)
        dg_ref[...] += dg

    return pl.pallas_call(
        body, grid=(T // tr,),
        in_specs=[pl.BlockSpec((tr, D), lambda i: (i, 0)), pl.BlockSpec((tr, D), lambda i: (i, 0)),
                  pl.BlockSpec((1, D), lambda i: (0, 0))],
        out_specs=[pl.BlockSpec((1, 128), lambda i: (0, 0)), pl.BlockSpec((tr, D), lambda i: (i, 0)),
                   pl.BlockSpec((tr, D), lambda i: (i, 0)), pl.BlockSpec((1, D), lambda i: (0, 0))],
        out_shape=[jax.ShapeDtypeStruct((1, 128), F32), jax.ShapeDtypeStruct((T, D), F32),
                   jax.ShapeDtypeStruct((T, D), BF16), jax.ShapeDtypeStruct((1, D), F32)],
        compiler_params=_cp(("arbitrary",)), name=name,
    )(x, tgt, g)


def _adamw(w, g, m, v, *, name):
    R, C = w.shape
    tr = _pick_rows(R, max(8, 2 * 1024 * 1024 // (4 * C)))
    c1 = 1.0 - ADAM_B1 ** ADAM_STEP
    c2 = 1.0 - ADAM_B2 ** ADAM_STEP

    def body(w_ref, g_ref, m_ref, v_ref, d_ref, nm_ref, nv_ref):
        gv = g_ref[...]
        nm = ADAM_B1 * m_ref[...] + (1.0 - ADAM_B1) * gv
        nv = ADAM_B2 * v_ref[...] + (1.0 - ADAM_B2) * jnp.square(gv)
        nm_ref[...] = nm
        nv_ref[...] = nv
        d_ref[...] = -ADAM_LR * ((nm / c1) / (jnp.sqrt(nv / c2) + ADAM_EPS) + ADAM_WD * w_ref[...])

    spec = pl.BlockSpec((tr, C), lambda i: (i, 0))
    return pl.pallas_call(
        body, grid=(R // tr,), in_specs=[spec] * 4, out_specs=[spec] * 3,
        out_shape=[jax.ShapeDtypeStruct((R, C), F32)] * 3,
        compiler_params=_cp(("parallel",)), name=name,
    )(w, g, m, v)


def _s5_prep(a_re, a_im, b_re, b_im, c_re, c_im, log_dt):
    G, N = a_re.shape
    P = b_re.shape[2]
    gpb = 128 // P
    nblk = G // gpb
    dt = jnp.exp(log_dt)[:, None]
    mag = jnp.exp(a_re * dt)
    abr, abi = mag * jnp.cos(a_im * dt), mag * jnp.sin(a_im * dt)
    den = a_re * a_re + a_im * a_im
    nr, ni = abr - 1.0, abi
    qr, qi = (nr * a_re + ni * a_im) / den, (ni * a_re - nr * a_im) / den
    bbr = qr[..., None] * b_re - qi[..., None] * b_im
    bbi = qr[..., None] * b_im + qi[..., None] * b_re
    eye = jnp.eye(gpb, dtype=F32)

    def expand_b(t):
        t = t.reshape(nblk, gpb, N, P).transpose(0, 1, 3, 2)
        return (t[:, :, :, None, :] * eye[None, :, None, :, None]).reshape(nblk, gpb * P, gpb * N)

    def expand_c(t):
        t = t.reshape(nblk, gpb, P, N).transpose(0, 1, 3, 2)
        return (t[:, :, :, None, :] * eye[None, :, None, :, None]).reshape(nblk, gpb * N, gpb * P)

    return (abr.reshape(1, G * N), abi.reshape(1, G * N), expand_b(bbr), expand_b(bbi),
            expand_c(c_re), expand_c(c_im))


def _s5_fwd(h, abr, abi, bre, bim, cre, cim, d, *, B, L, name):
    T, D = h.shape
    S = T // B
    L = min(L, S)
    nc = S // L
    nblk, cb, sb = bre.shape
    GN = abr.shape[1]

    def body(h_ref, ar_ref, ai_ref, bre_ref, bim_ref, cre_ref, cim_ref, d_ref,
             y_ref, yb_ref, xr_ref, xi_ref, er_ref, ei_ref, sr, si, car, cai):
        ci = pl.program_id(1)

        @pl.when(ci == 0)
        def _():
            car[...] = jnp.zeros_like(car)
            cai[...] = jnp.zeros_like(cai)

        for j in range(nblk):
            u = h_ref[:, j * cb:(j + 1) * cb]
            sr[:, j * sb:(j + 1) * sb] = jnp.dot(u, bre_ref[j], preferred_element_type=F32)
            si[:, j * sb:(j + 1) * sb] = jnp.dot(u, bim_ref[j], preferred_element_type=F32)
        ar, ai = ar_ref[...], ai_ref[...]

        def step(t, carry):
            pr, pi = carry
            nr = ar * pr - ai * pi + sr[pl.ds(t, 1), :]
            ni = ar * pi + ai * pr + si[pl.ds(t, 1), :]
            sr[pl.ds(t, 1), :] = nr
            si[pl.ds(t, 1), :] = ni
            return nr, ni

        pr, pi = lax.fori_loop(0, L, step, (car[...], cai[...]), unroll=4)
        car[...] = pr
        cai[...] = pi
        er_ref[0] = pr
        ei_ref[0] = pi
        for j in range(nblk):
            xr = sr[:, j * sb:(j + 1) * sb].astype(BF16)
            xi = si[:, j * sb:(j + 1) * sb].astype(BF16)
            xr_ref[:, j * sb:(j + 1) * sb] = xr
            xi_ref[:, j * sb:(j + 1) * sb] = xi
            y = (jnp.dot(xr, cre_ref[j], preferred_element_type=F32)
                 - jnp.dot(xi, cim_ref[j], preferred_element_type=F32))
            u = h_ref[:, j * cb:(j + 1) * cb].astype(F32)
            y = y + d_ref[:, j * cb:(j + 1) * cb] * u
            y_ref[:, j * cb:(j + 1) * cb] = y
            yb_ref[:, j * cb:(j + 1) * cb] = jax.nn.gelu(y).astype(BF16)

    tok = lambda w: pl.BlockSpec((L, w), lambda b, c: (b * nc + c, 0))
    whole = lambda p: pl.BlockSpec(p.shape, lambda b, c, nd=p.ndim: (0,) * nd)
    end = pl.BlockSpec((1, 1, GN), lambda b, c: (b * nc + c, 0, 0))
    return pl.pallas_call(
        body, grid=(B, nc),
        in_specs=[tok(D)] + [whole(p) for p in (abr, abi, bre, bim, cre, cim, d)],
        out_specs=[tok(D), tok(D), tok(GN), tok(GN), end, end],
        out_shape=[jax.ShapeDtypeStruct((T, D), F32), jax.ShapeDtypeStruct((T, D), BF16),
                   jax.ShapeDtypeStruct((T, GN), BF16),
                   jax.ShapeDtypeStruct((T, GN), BF16), jax.ShapeDtypeStruct((B * nc, 1, GN), F32),
                   jax.ShapeDtypeStruct((B * nc, 1, GN), F32)],
        scratch_shapes=[pltpu.VMEM((L, GN), F32), pltpu.VMEM((L, GN), F32),
                        pltpu.VMEM((1, GN), F32), pltpu.VMEM((1, GN), F32)],
        compiler_params=_cp(("arbitrary", "arbitrary")), name=name,
    )(h, abr, abi, bre, bim, cre, cim, d)


def _s5_bwd(dy, h, xr, xi, er, ei, abr, abi, bre, bim, cre, cim, d, *, B, L, name):
    T, D = h.shape
    S = T // B
    L = min(L, S)
    nc = S // L
    nblk, cb, sb = bre.shape
    GN = abr.shape[1]
    dims_nt = (((1,), (1,)), ((), ()))
    dims_tn = (((0,), (0,)), ((), ()))

    def body(dy_ref, h_ref, xr_ref, xi_ref, er_ref, ei_ref, ar_ref, ai_ref, bre_ref, bim_ref,
             cre_ref, cim_ref, d_ref,
             dh_ref, dbre_ref, dbim_ref, dcre_ref, dcim_ref, dar_ref, dai_ref, dd_ref,
             lr, li, car, cai):
        b, cstep = pl.program_id(0), pl.program_id(1)
        ci = nc - 1 - cstep

        @pl.when((b == 0) & (cstep == 0))
        def _():
            for r in (dbre_ref, dbim_ref, dcre_ref, dcim_ref, dar_ref, dai_ref, dd_ref):
                r[...] = jnp.zeros_like(r)

        @pl.when(cstep == 0)
        def _():
            car[...] = jnp.zeros_like(car)
            cai[...] = jnp.zeros_like(cai)

        for j in range(nblk):
            dyj = dy_ref[:, j * cb:(j + 1) * cb].astype(BF16)
            lr[:, j * sb:(j + 1) * sb] = lax.dot_general(dyj, cre_ref[j], dims_nt, preferred_element_type=F32)
            li[:, j * sb:(j + 1) * sb] = -lax.dot_general(dyj, cim_ref[j], dims_nt, preferred_element_type=F32)
        ar, ai = ar_ref[...], ai_ref[...]

        def step(s, carry):
            t = L - 1 - s
            pr, pi = carry
            nr = lr[pl.ds(t, 1), :] + ar * pr + ai * pi
            ni = li[pl.ds(t, 1), :] - ai * pr + ar * pi
            lr[pl.ds(t, 1), :] = nr
            li[pl.ds(t, 1), :] = ni
            return nr, ni

        pr, pi = lax.fori_loop(0, L, step, (car[...], cai[...]), unroll=4)
        car[...] = pr
        cai[...] = pi
        has_prev = (ci > 0).astype(F32)
        first_row = lax.broadcasted_iota(jnp.int32, (L, sb), 0) == 0
        for j in range(nblk):
            cs = slice(j * cb, (j + 1) * cb)
            ss = slice(j * sb, (j + 1) * sb)
            lrj, lij = lr[:, ss], li[:, ss]
            xrj, xij = xr_ref[:, ss], xi_ref[:, ss]
            pr_j = jnp.where(first_row, er_ref[0][:, ss] * has_prev, pltpu.roll(xrj.astype(F32), 1, 0))
            pi_j = jnp.where(first_row, ei_ref[0][:, ss] * has_prev, pltpu.roll(xij.astype(F32), 1, 0))
            dar_ref[:, ss] += jnp.sum(lrj * pr_j + lij * pi_j, axis=0, keepdims=True)
            dai_ref[:, ss] += jnp.sum(lij * pr_j - lrj * pi_j, axis=0, keepdims=True)
            lrb, lib = lrj.astype(BF16), lij.astype(BF16)
            hj = h_ref[:, cs]
            dyf = dy_ref[:, cs]
            dyj = dyf.astype(BF16)
            dbre_ref[j] += lax.dot_general(hj, lrb, dims_tn, preferred_element_type=F32)
            dbim_ref[j] += lax.dot_general(hj, lib, dims_tn, preferred_element_type=F32)
            dcre_ref[j] += lax.dot_general(xrj, dyj, dims_tn, preferred_element_type=F32)
            dcim_ref[j] -= lax.dot_general(xij, dyj, dims_tn, preferred_element_type=F32)
            du = (lax.dot_general(lrb, bre_ref[j], dims_nt, preferred_element_type=F32)
                  + lax.dot_general(lib, bim_ref[j], dims_nt, preferred_element_type=F32))
            dh_ref[:, cs] = du + d_ref[:, cs] * dyf
            dd_ref[:, cs] += jnp.sum(dyf * hj.astype(F32), axis=0, keepdims=True)

    tok = lambda w: pl.BlockSpec((L, w), lambda b, c: (b * nc + nc - 1 - c, 0))
    whole = lambda p: pl.BlockSpec(p.shape, lambda b, c, nd=p.ndim: (0,) * nd)
    prev_end = pl.BlockSpec((1, 1, GN), lambda b, c: (b * nc + jnp.maximum(nc - 2 - c, 0), 0, 0))
    params = (abr, abi, bre, bim, cre, cim, d)
    acc_shapes = [bre.shape, bim.shape, cre.shape, cim.shape, abr.shape, abi.shape, d.shape]
    out = pl.pallas_call(
        body, grid=(B, nc),
        in_specs=[tok(D), tok(D), tok(GN), tok(GN), prev_end, prev_end] + [whole(p) for p in params],
        out_specs=[tok(D)] + [pl.BlockSpec(s, lambda b, c, nd=len(s): (0,) * nd) for s in acc_shapes],
        out_shape=[jax.ShapeDtypeStruct((T, D), F32)] + [jax.ShapeDtypeStruct(s, F32) for s in acc_shapes],
        scratch_shapes=[pltpu.VMEM((L, GN), F32), pltpu.VMEM((L, GN), F32),
                        pltpu.VMEM((1, GN), F32), pltpu.VMEM((1, GN), F32)],
        compiler_params=_cp(("arbitrary", "arbitrary")), name=name,
    )(dy, h, xr, xi, er, ei, *params)
    return out


def _conv_fwd(zp, w, *, R, tc, name):
    B, SP, C = zp.shape
    S = SP - CONV_HALO
    R, tc = min(R, S), min(tc, C)

    def body(z_ref, w_ref, y_ref):
        def chunk(ci, _):
            start = pl.multiple_of(ci * R, 8)
            ze = z_ref[pl.ds(start, R + CONV_HALO), :]
            acc = jnp.zeros((R, tc), F32)
            for m in range(CONV_WIDTH):
                k = CONV_WIDTH - 1 - m
                sh = ze if m == 0 else pltpu.roll(ze, m, 0)
                acc = acc + w_ref[k:k + 1, :] * sh[CONV_HALO:, :]
            y_ref[pl.ds(start, R), :] = acc
            return 0

        lax.fori_loop(0, S // R, chunk, 0)

    return pl.pallas_call(
        body, grid=(B, C // tc),
        in_specs=[pl.BlockSpec((None, SP, tc), lambda b, c: (b, 0, c)),
                  pl.BlockSpec((32, tc), lambda b, c: (0, c))],
        out_specs=pl.BlockSpec((None, S, tc), lambda b, c: (b, 0, c)),
        out_shape=jax.ShapeDtypeStruct((B, S, C), F32),
        compiler_params=_cp(("parallel", "parallel")), name=name,
    )(zp, w)


def _conv_bwd(zp, dyp, w, *, R, tc, name):
    B, SP, C = zp.shape
    S = SP - CONV_HALO
    R, tc = min(R, S), min(tc, C)

    def body(z_ref, dy_ref, w_ref, dz_ref, dw_ref):
        @pl.when(pl.program_id(1) == 0)
        def _():
            dw_ref[...] = jnp.zeros_like(dw_ref)

        def chunk(ci, _):
            start = pl.multiple_of(ci * R, 8)
            zc = z_ref[pl.ds(start + CONV_HALO, R), :]
            de = dy_ref[pl.ds(start, R + CONV_HALO), :]
            acc = jnp.zeros((R, tc), F32)
            for m in range(CONV_WIDTH):
                k = CONV_WIDTH - 1 - m
                ds_ = (de if m == 0 else pltpu.roll(de, R + CONV_HALO - m, 0))[:R, :]
                acc = acc + w_ref[k:k + 1, :] * ds_
                dw_ref[k:k + 1, :] += jnp.sum(ds_ * zc, axis=0, keepdims=True)
            dz_ref[pl.ds(start, R), :] = acc
            return 0

        lax.fori_loop(0, S // R, chunk, 0)

    return pl.pallas_call(
        body, grid=(C // tc, B),
        in_specs=[pl.BlockSpec((None, SP, tc), lambda c, b: (b, 0, c)),
                  pl.BlockSpec((None, SP, tc), lambda c, b: (b, 0, c)),
                  pl.BlockSpec((32, tc), lambda c, b: (0, c))],
        out_specs=[pl.BlockSpec((None, S, tc), lambda c, b: (b, 0, c)),
                   pl.BlockSpec((32, tc), lambda c, b: (0, c))],
        out_shape=[jax.ShapeDtypeStruct((B, S, C), F32), jax.ShapeDtypeStruct((32, C), F32)],
        compiler_params=_cp(("parallel", "arbitrary")), name=name,
    )(zp, dyp, w)


def _gmlp_fwd(u, vn, ws, bcol, *, nck, name):
    T, E = u.shape
    H = ws.shape[0]
    he = E // H
    rows = nck * GMLP_CHUNK
    rows = min(rows, T)
    n_in = rows // GMLP_CHUNK

    def body(u_ref, v_ref, ws_ref, b_ref, o_ref):
        for c in range(n_in):
            rs = slice(c * GMLP_CHUNK, (c + 1) * GMLP_CHUNK)
            for hh in range(H):
                cs = slice(hh * he, (hh + 1) * he)
                v2 = jnp.dot(ws_ref[hh], v_ref[rs, cs].astype(BF16), preferred_element_type=F32)
                v2 = v2 + b_ref[:, hh:hh + 1]
                o_ref[rs, cs] = (u_ref[rs, cs] * v2).astype(o_ref.dtype)

    tok = pl.BlockSpec((rows, E), lambda i: (i, 0))
    return pl.pallas_call(
        body, grid=(T // rows,),
        in_specs=[tok, tok, pl.BlockSpec(ws.shape, lambda i: (0, 0, 0)), pl.BlockSpec(bcol.shape, lambda i: (0, 0))],
        out_specs=tok, out_shape=jax.ShapeDtypeStruct((T, E), BF16),
        compiler_params=_cp(("parallel",)), name=name,
    )(u, vn, ws, bcol)


def _gmlp_bwd(duv, u, vn, ws, bcol, *, nck, name):
    T, E = u.shape
    H = ws.shape[0]
    he = E // H
    rows = min(nck * GMLP_CHUNK, T)
    n_in = rows // GMLP_CHUNK
    dims_nt = (((1,), (1,)), ((), ()))
    dims_tn = (((0,), (0,)), ((), ()))

    def body(g_ref, u_ref, v_ref, ws_ref, b_ref, du_ref, dv_ref, dws_ref, db_ref):
        @pl.when(pl.program_id(0) == 0)
        def _():
            dws_ref[...] = jnp.zeros_like(dws_ref)
            db_ref[...] = jnp.zeros_like(db_ref)

        for c in range(n_in):
            rs = slice(c * GMLP_CHUNK, (c + 1) * GMLP_CHUNK)
            for hh in range(H):
                cs = slice(hh * he, (hh + 1) * he)
                vb = v_ref[rs, cs].astype(BF16)
                v2 = jnp.dot(ws_ref[hh], vb, preferred_element_type=F32) + b_ref[:, hh:hh + 1]
                g = g_ref[rs, cs]
                du_ref[rs, cs] = g * v2
                dv2 = g * u_ref[rs, cs]
                dv2b = dv2.astype(BF16)
                dv_ref[rs, cs] = lax.dot_general(ws_ref[hh], dv2b, dims_tn, preferred_element_type=F32)
                dws_ref[hh] += lax.dot_general(dv2b, vb, dims_nt, preferred_element_type=F32)
                db_ref[:, hh:hh + 1] += jnp.sum(dv2, axis=1, keepdims=True)

    tok = pl.BlockSpec((rows, E), lambda i: (i, 0))
    return pl.pallas_call(
        body, grid=(T // rows,),
        in_specs=[tok, tok, tok, pl.BlockSpec(ws.shape, lambda i: (0, 0, 0)), pl.BlockSpec(bcol.shape, lambda i: (0, 0))],
        out_specs=[tok, tok, pl.BlockSpec(ws.shape, lambda i: (0, 0, 0)), pl.BlockSpec(bcol.shape, lambda i: (0, 0))],
        out_shape=[jax.ShapeDtypeStruct((T, E), F32), jax.ShapeDtypeStruct((T, E), F32),
                   jax.ShapeDtypeStruct(ws.shape, F32), jax.ShapeDtypeStruct(bcol.shape, F32)],
        compiler_params=_cp(("arbitrary",)), name=name,
    )(duv, u, vn, ws, bcol)


PAIRS = ATT_HEADS // 2


def _att_consts():
    ji = lax.broadcasted_iota(jnp.int32, (2 * ATT_BLK, ATT_BLK), 0)
    ii = lax.broadcasted_iota(jnp.int32, (2 * ATT_BLK, ATT_BLK), 1)
    dist = ii + ATT_BLK - ji
    band = (dist >= 0) & (dist <= ATT_BLK)
    cur = ji >= ATT_BLK
    first_head = lax.broadcasted_iota(jnp.int32, (ATT_BLK, 2 * HEAD_DIM), 1) < HEAD_DIM
    return band, cur, first_head


def _both_heads(t, first_head):
    zero = jnp.zeros_like(t)
    return jnp.concatenate([jnp.where(first_head, t, zero), jnp.where(first_head, zero, t)], axis=0)


def _att_specs(nbk, offs, nsteps, rev):
    rows = nbk * ATT_BLK
    step = (lambda i: nsteps - 1 - i) if rev else (lambda i: i)
    qoff, koff, voff = offs
    blk = lambda off: pl.BlockSpec((rows, 2 * HEAD_DIM), lambda hp, i: (step(i), off + hp))
    prev = lambda off: pl.BlockSpec((ATT_BLK, 2 * HEAD_DIM), lambda hp, i: (jnp.maximum(step(i) * nbk - 1, 0), off + hp))
    out = pl.BlockSpec((rows, 2 * HEAD_DIM), lambda hp, i: (step(i), hp))
    stat = pl.BlockSpec((2, nbk, ATT_BLK), lambda hp, i: (hp, step(i), 0))
    return [blk(qoff), blk(koff), prev(koff), blk(voff), prev(voff)], out, stat


def _att_fwd(arr, offs, *, nb, nbk, name):
    T = arr.shape[0]
    nbk = min(nbk, T // ATT_BLK)
    nsteps = T // (nbk * ATT_BLK)
    scale = HEAD_DIM ** -0.5
    dims_nt = (((1,), (1,)), ((), ()))
    dims_tn = (((0,), (0,)), ((), ()))

    def body(q_ref, k_ref, kp_ref, v_ref, vp_ref, o_ref, lse_ref):
        i = pl.program_id(1)
        band, cur, first_head = _att_consts()
        for jj in range(nbk):
            rs = slice(jj * ATT_BLK, (jj + 1) * ATT_BLK)
            ps = slice((jj - 1) * ATT_BLK, jj * ATT_BLK)
            has_prev = ((i * nbk + jj) & (nb - 1)) != 0
            valid = band & (cur | has_prev)
            kk = jnp.concatenate([kp_ref[...] if jj == 0 else k_ref[ps, :], k_ref[rs, :]], axis=0)
            vv = jnp.concatenate([vp_ref[...] if jj == 0 else v_ref[ps, :], v_ref[rs, :]], axis=0)
            q2 = _both_heads(q_ref[rs, :], first_head)
            st = lax.dot_general(kk, q2, dims_nt, preferred_element_type=F32) * scale
            st = jnp.where(jnp.concatenate([valid, valid], axis=1), st, MASK_VALUE)
            m = jnp.max(st, axis=0, keepdims=True)
            p = jnp.exp(st - m)
            l = jnp.sum(p, axis=0, keepdims=True)
            lse = m + jnp.log(l)
            lse_ref[0, jj:jj + 1, :] = lse[:, :ATT_BLK]
            lse_ref[1, jj:jj + 1, :] = lse[:, ATT_BLK:]
            pn = (p / l).astype(BF16)
            o2 = lax.dot_general(pn, vv, dims_tn, preferred_element_type=F32)
            o_ref[rs, :] = jnp.where(first_head, o2[:ATT_BLK], o2[ATT_BLK:])

    ins, out, stat = _att_specs(nbk, offs, nsteps, False)
    return pl.pallas_call(
        body, grid=(PAIRS, nsteps), in_specs=ins, out_specs=[out, stat],
        out_shape=[jax.ShapeDtypeStruct((T, ATT_W), F32), jax.ShapeDtypeStruct((ATT_HEADS, T // ATT_BLK, ATT_BLK), F32)],
        compiler_params=_cp(("parallel", "parallel")), name=name,
    )(arr, arr, arr, arr, arr)


def _att_bwd(arr, offs, do, lse, dlse, *, nb, nbk, name):
    T = arr.shape[0]
    nbk = min(nbk, T // ATT_BLK)
    nsteps = T // (nbk * ATT_BLK)
    scale = HEAD_DIM ** -0.5
    dims_nt = (((1,), (1,)), ((), ()))
    dims_tn = (((0,), (0,)), ((), ()))

    def body(q_ref, k_ref, kp_ref, v_ref, vp_ref, do_ref, lse_ref, dlse_ref, dq_ref, dk_ref, dv_ref, ck, cv):
        step = pl.program_id(1)
        i = nsteps - 1 - step
        band, cur, first_head = _att_consts()

        @pl.when(step == 0)
        def _():
            ck[...] = jnp.zeros_like(ck)
            cv[...] = jnp.zeros_like(cv)

        carry_k, carry_v = ck[...], cv[...]
        for jj in reversed(range(nbk)):
            rs = slice(jj * ATT_BLK, (jj + 1) * ATT_BLK)
            ps = slice((jj - 1) * ATT_BLK, jj * ATT_BLK)
            has_prev = ((i * nbk + jj) & (nb - 1)) != 0
            valid = band & (cur | has_prev)
            kk = jnp.concatenate([kp_ref[...] if jj == 0 else k_ref[ps, :], k_ref[rs, :]], axis=0)
            vv = jnp.concatenate([vp_ref[...] if jj == 0 else v_ref[ps, :], v_ref[rs, :]], axis=0)
            q2 = _both_heads(q_ref[rs, :], first_head)
            do2 = _both_heads(do_ref[rs, :].astype(BF16), first_head)
            lse = jnp.concatenate([lse_ref[0, jj:jj + 1, :], lse_ref[1, jj:jj + 1, :]], axis=1)
            dlse = jnp.concatenate([dlse_ref[0, jj:jj + 1, :], dlse_ref[1, jj:jj + 1, :]], axis=1)
            st = lax.dot_general(kk, q2, dims_nt, preferred_element_type=F32) * scale
            st = jnp.where(jnp.concatenate([valid, valid], axis=1), st, MASK_VALUE)
            p = jnp.exp(st - lse)
            dp = lax.dot_general(vv, do2, dims_nt, preferred_element_type=F32)
            delta = jnp.sum(p * dp, axis=0, keepdims=True)
            dsb = (p * (dp - delta + dlse) * scale).astype(BF16)
            dq2 = lax.dot_general(dsb, kk, dims_tn, preferred_element_type=F32)
            dkk = jnp.dot(dsb, q2, preferred_element_type=F32)
            dvv = jnp.dot(p.astype(BF16), do2, preferred_element_type=F32)
            dq_ref[rs, :] = jnp.where(first_head, dq2[:ATT_BLK], dq2[ATT_BLK:]).astype(dq_ref.dtype)
            dk_ref[rs, :] = (dkk[ATT_BLK:] + carry_k).astype(dk_ref.dtype)
            dv_ref[rs, :] = (dvv[ATT_BLK:] + carry_v).astype(dv_ref.dtype)
            carry_k, carry_v = dkk[:ATT_BLK], dvv[:ATT_BLK]
        ck[...] = carry_k
        cv[...] = carry_v

    ins, out, stat = _att_specs(nbk, offs, nsteps, True)
    return pl.pallas_call(
        body, grid=(PAIRS, nsteps), in_specs=ins + [out, stat, stat], out_specs=[out] * 3,
        out_shape=[jax.ShapeDtypeStruct((T, ATT_W), BF16)] * 3,
        scratch_shapes=[pltpu.VMEM((ATT_BLK, 2 * HEAD_DIM), F32), pltpu.VMEM((ATT_BLK, 2 * HEAD_DIM), F32)],
        compiler_params=_cp(("arbitrary", "arbitrary")), name=name,
    )(arr, arr, arr, arr, arr, do, lse, dlse)


def _deinterleave(t, B, S, dil):
    if dil == 1:
        return t
    return t.reshape((B, S // dil, dil) + t.shape[1:]).swapaxes(1, 2).reshape(t.shape)


def _interleave(t, B, S, dil):
    if dil == 1:
        return t
    return t.reshape((B, dil, S // dil) + t.shape[1:]).swapaxes(1, 2).reshape(t.shape)


def _stats_to_tokens(lse, B, S, dil):
    return _interleave(lse.reshape(lse.shape[0], -1).T, B, S, dil)


def _stats_from_tokens(dl, B, S, dil):
    return _deinterleave(dl, B, S, dil).T.reshape(dl.shape[1], -1, ATT_BLK)


def _mesh_pos():
    return lax.axis_index("x"), lax.axis_index("y"), lax.axis_index("c")


def _allgather8(xs, *, name):
    m_per, n = xs.shape

    def body(x_ref, out_ref, send_sems, recv_sems, local_sem):
        x, y, c = _mesh_pos()
        me, sibling = (x, y, c), (x, y, 1 - c)
        chips = [(1 - x, y), (x, 1 - y), (1 - x, 1 - y)]

        def rows(px, py, pc):
            return out_ref.at[pl.ds((4 * px + 2 * py + pc) * m_per, m_per), :]

        def copy(k, block, to, src=None):
            return pltpu.make_async_remote_copy(
                src_ref=rows(*block) if src is None else src, dst_ref=rows(*block),
                send_sem=send_sems.at[k], recv_sem=recv_sems.at[k], device_id=to, device_id_type=MESH)

        mine = pltpu.make_async_copy(x_ref, rows(*me), local_sem)
        mine.start()
        first = [copy(0, me, sibling, src=x_ref)]
        first += [copy(1 + j, me, (*chip, c), src=x_ref) for j, chip in enumerate(chips)]
        for cp in first:
            cp.start()
        passed = [copy(4 + j, (*chip, c), sibling) for j, chip in enumerate(chips)]
        for j, chip in enumerate(chips):
            copy(1 + j, (*chip, c), me).wait_recv()
            passed[j].start()
        copy(0, sibling, me).wait_recv()
        for j, chip in enumerate(chips):
            copy(4 + j, (*chip, 1 - c), me).wait_recv()
        for cp in first + passed:
            cp.wait_send()
        mine.wait()

    return pl.pallas_call(
        body, out_shape=jax.ShapeDtypeStruct((8 * m_per, n), xs.dtype),
        in_specs=[pl.BlockSpec(memory_space=pltpu.VMEM)], out_specs=pl.BlockSpec(memory_space=pltpu.VMEM),
        scratch_shapes=[pltpu.SemaphoreType.DMA((7,)), pltpu.SemaphoreType.DMA((7,)), pltpu.SemaphoreType.DMA],
        compiler_params=pltpu.CompilerParams(vmem_limit_bytes=VMEM_LIMIT), name=name,
    )(xs)


def _hbm_call(body, arrays, out_shapes, n_sems, *, name):
    any_spec = pl.BlockSpec(memory_space=pl.ANY)
    return pl.pallas_call(
        body, out_shape=out_shapes, in_specs=[any_spec] * len(arrays), out_specs=[any_spec] * len(out_shapes),
        scratch_shapes=[pltpu.SemaphoreType.DMA((n_sems,)), pltpu.SemaphoreType.DMA((n_sems,))], name=name,
    )(*arrays)


def _other_chips(x, y):
    return [(1 - x, y), (x, 1 - y), (1 - x, 1 - y)]


def _allgather_chips(ws, *, name):
    n = len(ws)

    def body(*refs):
        ins, outs, (send_sems, recv_sems) = refs[:n], refs[n:2 * n], refs[2 * n:]
        x, y, c = _mesh_pos()
        chips = _other_chips(x, y)

        def copy(a, k, px, py, half, to, src=None):
            slot = outs[a].at[2 * px + py, half]
            return pltpu.make_async_remote_copy(
                src_ref=slot if src is None else src, dst_ref=slot,
                send_sem=send_sems.at[6 * a + k], recv_sem=recv_sems.at[6 * a + k], device_id=to, device_id_type=MESH)

        first = [copy(a, j, x, y, c, (*chip, c), src=ins[a].at[c]) for a in range(n) for j, chip in enumerate(chips)]
        for cp in first:
            cp.start()
        passed = []
        for j, chip in enumerate(chips):
            for a in range(n):
                copy(a, j, *chip, c, (x, y, c)).wait_recv()
                passed.append(copy(a, 3 + j, *chip, c, (x, y, 1 - c)))
                passed[-1].start()
        for j, chip in enumerate(chips):
            for a in range(n):
                copy(a, 3 + j, *chip, 1 - c, (x, y, c)).wait_recv()
        for cp in first + passed:
            cp.wait_send()

    return _hbm_call(body, ws, [jax.ShapeDtypeStruct((N_CHIPS,) + w.shape, w.dtype) for w in ws], 6 * n, name=name)


def _split_start(srcs, lands, after, issue, n_sems, *, name):
    ns, nl = len(srcs), len(lands)
    hbm, sem = pl.BlockSpec(memory_space=pltpu.HBM), pl.BlockSpec(memory_space=pltpu.SEMAPHORE)
    extra = [] if after is None else [after]

    def body(*refs):
        n_in = ns + nl + len(extra)
        send_sems, recv_sems = refs[n_in], refs[n_in + 1]
        issue(refs[:ns], refs[ns:ns + nl], send_sems, recv_sems)
        refs[-1][...] = jnp.zeros_like(refs[-1])

    arrays = [pltpu.with_memory_space_constraint(a, pltpu.HBM) for a in list(srcs) + list(lands)]
    out = pl.pallas_call(
        body, name=name,
        out_shape=(pltpu.SemaphoreType.DMA((n_sems,)), pltpu.SemaphoreType.DMA((n_sems,)),
                   *[pltpu.HBM(a.shape, a.dtype) for a in arrays], jax.ShapeDtypeStruct((8, 128), F32)),
        in_specs=[hbm] * (ns + nl) + [pl.BlockSpec(memory_space=pl.ANY)] * len(extra),
        out_specs=(sem, sem, *[hbm] * (ns + nl), pl.BlockSpec(memory_space=pltpu.VMEM)),
        input_output_aliases={i: 2 + i for i in range(ns + nl)},
        compiler_params=pltpu.CompilerParams(has_side_effects=pltpu.SideEffectType.DATAFLOW_SIDE_EFFECTING),
    )(*arrays, *extra)
    return out[0], out[1], list(out[2:2 + ns]), list(out[2 + ns:2 + ns + nl]), out[-1]


def _split_wait(send_sems, recv_sems, srcs, lands, after, waits, *, name):
    ns, nl = len(srcs), len(lands)
    hbm, sem = pl.BlockSpec(memory_space=pltpu.HBM), pl.BlockSpec(memory_space=pltpu.SEMAPHORE)

    def body(*refs):
        waits(refs[:ns], refs[ns:ns + nl], refs[ns + nl], refs[ns + nl + 1])

    out = pl.pallas_call(
        body, name=name,
        out_shape=tuple(pltpu.HBM(a.shape, a.dtype) for a in list(srcs) + list(lands)),
        in_specs=[hbm] * (ns + nl) + [sem, sem, pl.BlockSpec(memory_space=pl.ANY)],
        out_specs=tuple([hbm] * (ns + nl)),
        input_output_aliases={i: i for i in range(ns + nl)},
        compiler_params=pltpu.CompilerParams(has_side_effects=pltpu.SideEffectType.DATAFLOW_SIDE_EFFECTING),
    )(*srcs, *lands, send_sems, recv_sems, after)
    return list(out[:ns]), list(out[ns:])


def _gather_start(halves, after, *, name):
    n = len(halves)
    lands = [lax.empty((N_CHIPS,) + h.shape, h.dtype) for h in halves]

    def issue(srcs, dsts, send_sems, recv_sems):
        x, y, c = _mesh_pos()
        me = 2 * x + y
        for a in range(n):
            for j, (px, py) in enumerate(_other_chips(x, y)):
                for cc in range(2):
                    pltpu.make_async_remote_copy(
                        src_ref=srcs[a].at[c], dst_ref=dsts[a].at[me, c],
                        send_sem=send_sems.at[6 * a + 2 * j + cc], recv_sem=recv_sems.at[6 * a + 2 * j + c],
                        device_id=(px, py, cc), device_id_type=MESH).start()

    return _split_start(halves, lands, after, issue, 6 * n, name=name)


def _gather_wait(started, after, *, name):
    send_sems, recv_sems, halves, lands = started
    n = len(halves)

    def waits(srcs, dsts, send_sems, recv_sems):
        x, y, c = _mesh_pos()
        me = 2 * x + y
        for a in range(n):
            for j, (px, py) in enumerate(_other_chips(x, y)):
                for cc in range(2):
                    pltpu.make_async_remote_copy(
                        src_ref=srcs[a].at[cc], dst_ref=dsts[a].at[2 * px + py, cc],
                        send_sem=send_sems.at[6 * a + 2 * j + cc], recv_sem=recv_sems.at[6 * a + 2 * j + cc],
                        device_id=(px, py, cc), device_id_type=MESH).wait_recv()
        for a in range(n):
            for j, (px, py) in enumerate(_other_chips(x, y)):
                for cc in range(2):
                    pltpu.make_async_remote_copy(
                        src_ref=srcs[a].at[c], dst_ref=dsts[a].at[me, c],
                        send_sem=send_sems.at[6 * a + 2 * j + cc], recv_sem=recv_sems.at[6 * a + 2 * j + c],
                        device_id=(px, py, cc), device_id_type=MESH).wait_send()

    return _split_wait(send_sems, recv_sems, halves, lands, after, waits, name=name)


def _reduce_plan_loops(plans, chip, c, fn):
    for a, plan in enumerate(plans):
        for h, cc in plan:
            for k in range(N_CHIPS):
                fn(a, h, k, cc, jnp.logical_or(chip != k, c != cc))


def _reduce_start(srcs, lands, plans, after, *, name):
    def issue(src_refs, land_refs, send_sems, recv_sems):
        x, y, c = _mesh_pos()
        chip = 2 * x + y
        my_id = 2 * chip + c

        def send(a, h, k, cc, is_other):
            @pl.when(is_other)
            def _():
                pltpu.make_async_remote_copy(
                    src_ref=src_refs[a].at[h, k], dst_ref=land_refs[a].at[my_id],
                    send_sem=send_sems.at[8 * a + 2 * k + cc], recv_sem=recv_sems.at[8 * a + my_id],
                    device_id=(k // 2, k % 2, cc), device_id_type=MESH).start()

        _reduce_plan_loops(plans, chip, c, send)

    return _split_start(srcs, lands, after, issue, 8 * len(srcs), name=name)


def _reduce_wait(started, plans, after, *, name):
    send_sems, recv_sems, srcs, lands = started

    def waits(src_refs, land_refs, send_sems, recv_sems):
        x, y, c = _mesh_pos()
        chip = 2 * x + y
        my_id = 2 * chip + c
        for a, plan in enumerate(plans):
            for h, cc in plan:
                for s in range(2 * N_CHIPS):
                    @pl.when(jnp.logical_and(c == cc, my_id != s))
                    def _(a=a, h=h, s=s):
                        pltpu.make_async_remote_copy(
                            src_ref=src_refs[a].at[h, 0], dst_ref=land_refs[a].at[s],
                            send_sem=send_sems.at[8 * a + s], recv_sem=recv_sems.at[8 * a + s],
                            device_id=(s // 4, (s // 2) % 2, s % 2), device_id_type=MESH).wait_recv()

        def sent(a, h, k, cc, is_other):
            @pl.when(is_other)
            def _():
                pltpu.make_async_remote_copy(
                    src_ref=src_refs[a].at[h, k], dst_ref=land_refs[a].at[my_id],
                    send_sem=send_sems.at[8 * a + 2 * k + cc], recv_sem=recv_sems.at[8 * a + my_id],
                    device_id=(k // 2, k % 2, cc), device_id_type=MESH).wait_send()

        _reduce_plan_loops(plans, chip, c, sent)

    return _split_wait(send_sems, recv_sems, srcs, lands, after, waits, name=name)


def _sum8(land, own, my_id, *, name):
    n_src, R, C = land.shape
    tr = _pick_rows(R, max(8, 1024 * 1024 // (2 * C)))

    def body(id_ref, *refs):
        own_ref, o_ref = refs[n_src], refs[n_src + 1]
        me = id_ref[0]
        acc = None
        for s in range(n_src):
            term = jnp.where(me == s, own_ref[...], refs[s][...]).astype(F32)
            acc = term if acc is None else acc + term
        o_ref[...] = acc

    return pl.pallas_call(
        body, out_shape=jax.ShapeDtypeStruct((R, C), F32),
        grid_spec=pltpu.PrefetchScalarGridSpec(
            num_scalar_prefetch=1, grid=(R // tr,),
            in_specs=[pl.BlockSpec((None, tr, C), lambda i, idr, s=s: (s, i, 0)) for s in range(n_src)]
            + [pl.BlockSpec((tr, C), lambda i, idr: (i, 0))],
            out_specs=pl.BlockSpec((tr, C), lambda i, idr: (i, 0))),
        compiler_params=_cp(("parallel",)), name=name,
    )(my_id.reshape(1).astype(jnp.int32), *([land] * n_src), own)


def _swap_halves(gs, *, name):
    n = len(gs)

    def body(*refs):
        ins, outs, (send_sems, recv_sems) = refs[:n], refs[n:2 * n], refs[2 * n:]
        x, y, c = _mesh_pos()
        cps = [pltpu.make_async_remote_copy(
            src_ref=ins[a].at[1 - c], dst_ref=outs[a], send_sem=send_sems.at[a], recv_sem=recv_sems.at[a],
            device_id=(x, y, 1 - c), device_id_type=MESH) for a in range(n)]
        for cp in cps:
            cp.start()
        for cp in cps:
            cp.wait()

    return _hbm_call(body, gs, [jax.ShapeDtypeStruct(g.shape[1:], g.dtype) for g in gs], n, name=name)


def _scatter_chips(ss, *, name):
    n = len(ss)

    def body(*refs):
        ins, outs, (send_sems, recv_sems) = refs[:n], refs[n:2 * n], refs[2 * n:]
        x, y, c = _mesh_pos()
        me = 2 * x + y
        chips = _other_chips(x, y)

        def copy(a, j, px, py):
            return pltpu.make_async_remote_copy(
                src_ref=ins[a].at[2 * px + py], dst_ref=outs[a].at[me],
                send_sem=send_sems.at[3 * a + j], recv_sem=recv_sems.at[3 * a + j],
                device_id=(px, py, c), device_id_type=MESH)

        def arrival(a, j, px, py):
            return pltpu.make_async_remote_copy(
                src_ref=ins[a].at[me], dst_ref=outs[a].at[2 * px + py],
                send_sem=send_sems.at[3 * a + j], recv_sem=recv_sems.at[3 * a + j],
                device_id=(px, py, c), device_id_type=MESH)

        cps = [copy(a, j, *chip) for a in range(n) for j, chip in enumerate(chips)]
        for cp in cps:
            cp.start()
        for a in range(n):
            for j, chip in enumerate(chips):
                arrival(a, j, *chip).wait_recv()
        for cp in cps:
            cp.wait_send()

    return _hbm_call(body, ss, [jax.ShapeDtypeStruct(s.shape, s.dtype) for s in ss], 3 * n, name=name)


def _share_halves(ts, *, name):
    n = len(ts)

    def body(*refs):
        ins, outs, (send_sems, recv_sems) = refs[:n], refs[n:2 * n], refs[2 * n:]
        x, y, c = _mesh_pos()
        cps = [pltpu.make_async_remote_copy(
            src_ref=ins[a], dst_ref=outs[a].at[c], send_sem=send_sems.at[a], recv_sem=recv_sems.at[a],
            device_id=(x, y, 1 - c), device_id_type=MESH) for a in range(n)]
        for cp in cps:
            cp.start()
        for a in range(n):
            pltpu.make_async_remote_copy(
                src_ref=ins[a], dst_ref=outs[a].at[1 - c], send_sem=send_sems.at[a], recv_sem=recv_sems.at[a],
                device_id=(x, y, 1 - c), device_id_type=MESH).wait_recv()
        for cp in cps:
            cp.wait_send()

    return _hbm_call(body, ts, [jax.ShapeDtypeStruct((2,) + t.shape, t.dtype) for t in ts], n, name=name)


def _half_add(g, ra, core, *, name):
    _, R, C = g.shape
    tr = _pick_rows(R, max(8, 2 * 1024 * 1024 // (4 * C)))

    def body(core_ref, g_ref, ra_ref, o_ref):
        o_ref[...] = (g_ref[...] + ra_ref[...]).astype(o_ref.dtype)

    return pl.pallas_call(
        body, out_shape=jax.ShapeDtypeStruct((R, C), BF16),
        grid_spec=pltpu.PrefetchScalarGridSpec(
            num_scalar_prefetch=1, grid=(R // tr,),
            in_specs=[pl.BlockSpec((None, tr, C), lambda i, cr: (cr[0], i, 0)),
                      pl.BlockSpec((tr, C), lambda i, cr: (i, 0))],
            out_specs=pl.BlockSpec((tr, C), lambda i, cr: (i, 0))),
        compiler_params=_cp(("parallel",)), name=name,
    )(core.reshape(1).astype(jnp.int32), g, ra)


def _sum4(rb, *, name):
    _, R, C = rb.shape
    tr = _pick_rows(R, max(8, 2 * 1024 * 1024 // (4 * C)))

    def body(r0, r1, r2, r3, o_ref):
        f = lambda r: r[...].astype(F32)
        o_ref[...] = ((f(r0) + f(r1)) + f(r2)) + f(r3)

    return pl.pallas_call(
        body, out_shape=jax.ShapeDtypeStruct((R, C), F32), grid=(R // tr,),
        in_specs=[pl.BlockSpec((None, tr, C), lambda i, k=k: (k, i, 0)) for k in range(N_CHIPS)],
        out_specs=pl.BlockSpec((tr, C), lambda i: (i, 0)),
        compiler_params=_cp(("parallel",)), name=name,
    )(rb, rb, rb, rb)


TR = 512
S5_CHUNK = 256


def _rms_fwd(x, g, name):
    return _rowwise(_f_rms, [x], [g], (BF16,), tr=TR, name=name)[0]


def _rms_bwd_epi(dh, x, g, gx):
    r = lax.rsqrt(jnp.mean(x * x, axis=-1, keepdims=True) + EPS)
    xr = x * r
    t = dh * g
    dx = r * (t - xr * jnp.mean(t * xr, axis=-1, keepdims=True)) + gx
    return dx, dx, jnp.sum(dh * xr, axis=0, keepdims=True)


def _mm_rms_bwd(a, w, x, g, gx, *, after=None, name, **kw):
    kw.setdefault("tm", 1024)
    return _mm(a, w, tb=True, epi=_rms_bwd_epi, extras=(x, g, gx), out_dtypes=(F32, BF16, F32), n_row_sums=1,
               tn=x.shape[1], after=after, name=name, **kw)


def _rms_bwd(x, g, dh, gx, name, after=None):
    (dx, dxb), (dg,) = _rowwise_vjp(_f_rms, [x], [g], [dh], [(F32, BF16)], adds={0: gx}, after=after, tr=TR,
                                    name=name)
    return dx, dxb, dg


def _grad_cols(M, Nq):
    def imap(tm, tn):
        hp, per = (M // 2) // tm, Nq // tn
        assert hp * tm * 2 == M and per * tn == Nq, (M, Nq, tm, tn)
        return lambda i, j, k: (i // hp, j // per, i % hp, j % per)
    return (2, N_CHIPS, M // 2, Nq), lambda tm, tn: (None, None, tm, tn), imap, None, M // 2, Nq


def _grad_rows(Mq, N):
    def imap(tm, tn):
        po, hp = Mq // tm, (Mq // 2) // tm
        assert hp * tm * 2 == Mq, (Mq, tm)
        return lambda i, j, k: ((i % po) // hp, i // po, (i % po) % hp, j)
    return (2, N_CHIPS, Mq // 2, N), lambda tm, tn: (None, None, tm, tn), imap, None, Mq // 2, N


def _grad_layer_cols(slot, lh, M, Nq, buf):
    def imap(tm, tn):
        per = Nq // tn
        return lambda i, j, k: (j // per, slot, i, j % per)
    return (N_CHIPS, lh, M, Nq), lambda tm, tn: (None, None, tm, tn), imap, buf, M, Nq


def _grad_layer_rows(slot, lh, Mq, N, buf):
    def imap(tm, tn):
        po = Mq // tm
        return lambda i, j, k: (i // po, slot, i % po, j)
    return (N_CHIPS, lh, Mq, N), lambda tm, tn: (None, None, tm, tn), imap, buf, Mq, N


def _add_then_rms(acc, res, g):
    xo = acc + res
    return xo, _f_rms(xo, g)[0]


def _mlp_fwd(x, h2, w_in, w_out, g_next, li):
    w_in_full = getattr(w_in, "plain", w_in)
    r = _mm(h2, w_in_full, out_dtypes=(BF16,), epi=lambda acc: (jnp.maximum(acc, 0.0),), tm=512, tn=w_in.shape[1],
            name=f"mlp_in_{li}")
    tiles = dict(tm=512, tn=x.shape[1], tk=r.shape[1])
    if g_next is None:
        x_out, h_next = _mm(r, w_out, pro_a=lambda t: t * t, epi=lambda acc, res: (acc + res,), extras=(x,),
                            name=f"mlp_out_{li}", **tiles), None
    else:
        x_out, h_next = _mm(r, w_out, pro_a=lambda t: t * t, epi=_add_then_rms, extras=(x, g_next),
                            out_dtypes=(F32, BF16), name=f"mlp_out_{li}", **tiles)
    return x_out, h_next, (h2, r)


def _mlp_bwd(gx, gxb, x, g, w_in, w_out, saved, li, nl, bufs):
    h2, r = saved
    D, F = w_in.shape
    lh = nl // 2
    da = _mm(gxb, w_out, tb=True, out_dtypes=(BF16,),
             epi=lambda acc, rt: (acc * 2.0 * rt.astype(F32),), extras=(r,), tm=512, tn=F, name=f"mlp_dact_{li}")
    buf_in, buf_out = bufs if bufs is not None else (None, None)
    d_w_out = _mm(r, gxb, ta=True, pro_a=lambda t: t * t, tm=1024, tn=1024, tk=4096, out_dtypes=(BF16,),
                  out=_grad_layer_rows(li % lh, lh, F // N_CHIPS, D, buf_out), name=f"mlp_dwout_{li}")
    d_w_in = _mm(h2, da, ta=True, tm=1024, tn=1024, tk=4096, out_dtypes=(BF16,),
                 out=_grad_layer_cols(li % lh, lh, D, F // N_CHIPS, buf_in), name=f"mlp_dwin_{li}")
    gx_mid, gxb_mid, dg = _mm_rms_bwd(da, getattr(w_in, "plain", w_in), x, g, gx, tm=512, tk=F, name=f"mlp_dh_{li}")
    return gx_mid, gxb_mid, dg, (d_w_in, d_w_out)


def _local_step(x3, tgt3, p, layer_weights, token=None, grads_done=lambda group: None):
    B, S, D = x3.shape
    T = B * S
    x = x3.reshape(T, D)
    grads = {}
    row = lambda v: v.reshape(1, -1)
    p = dict(p)
    nl = p["norm_mlp"].shape[0]
    mlp_in, mlp_out = [None] * nl, [None] * nl

    def fetch(li, after):
        wl = dict(layer_weights(li, after))
        mlp_in[li], mlp_out[li] = wl.pop("mlp_w_in"), wl.pop("mlp_w_out")
        p.update(wl)

    g0 = row(p["norm_mix"][0])
    if token is not None:
        g0 = g0 + token[:1, :1]
    h0 = _rms_fwd(x, g0, "rms_mix_0")
    s5_args = (p["ssm_a_re"][0], p["ssm_a_im"][0], p["ssm_b_re"][0], p["ssm_b_im"][0],
               p["ssm_c_re"][0], p["ssm_c_im"][0], p["ssm_log_dt"][0])
    s5_exp, s5_vjp = jax.vjp(_s5_prep, *s5_args)
    abr, abi, bre, bim, cre, cim = s5_exp
    bre_b, bim_b, cre_b, cim_b = (t.astype(BF16) for t in (bre, bim, cre, cim))
    d_skip = p["ssm_d"]
    ypre, yb, sxr, sxi, ser, sei = _s5_fwd(h0, abr, abi, bre_b, bim_b, cre_b, cim_b, d_skip, B=B, L=S5_CHUNK,
                                           name="s5_fwd")
    fetch(0, yb)
    w_glu = p["ssm_w_glu"]
    z0 = _mm(yb, w_glu, tm=2048, name="s5_glu_mm")
    gm = [row(p["norm_mlp"][i]) for i in range(nl)]
    g1, g2, g3 = (row(p["norm_mix"][i]) for i in range(1, nl))
    x_mid0, hm0 = _rowwise(lambda z, xr, g: _add_then_rms(_f_glu(z)[0], xr, g), [z0, x], [gm[0]], (F32, BF16),
                           tr=TR, name="s5_glu")
    x1, h1, mlp_saved0 = _mlp_fwd(x_mid0, hm0, mlp_in[0], mlp_out[0], g1, 0)

    fetch(1, h1)
    z1 = _mm(h1, p["conv_w_pw1"], tm=2048, name="conv_pw1")
    zg = _rowwise(_f_bias_glu, [z1], [p["conv_b_pw1"]], (F32,), tr=TR, name="conv_glu")[0]
    zp = jnp.pad(zg.reshape(B, S, D), ((0, 0), (CONV_HALO, 0), (0, 0)))
    w_dw = jnp.pad(p["conv_w_dw"], ((0, 32 - CONV_WIDTH), (0, 0)))
    yc = _conv_fwd(zp, w_dw, R=256, tc=128, name="conv_dw").reshape(T, D)
    ln_par = [p["conv_b_dw"], p["conv_ln_g"], p["conv_ln_b"]]
    qc = _rowwise(_f_ln_silu, [yc], ln_par, (BF16,), tr=TR, name="conv_ln_silu")[0]
    x_mid1, hm1 = _mm(qc, p["conv_w_pw2"], epi=lambda acc, bias, res, g: _add_then_rms(acc + bias, res, g),
                      extras=(p["conv_b_pw2"], x1, gm[1]), out_dtypes=(F32, BF16), tn=D, name="conv_pw2")
    x2, h2, mlp_saved1 = _mlp_fwd(x_mid1, hm1, mlp_in[1], mlp_out[1], g2, 1)

    fetch(2, h2)
    z2 = _mm(h2, p["gmlp_w_in"], tm=2048, name="gmlp_in")
    gl_par = [p["gmlp_ln_g"], p["gmlp_ln_b"]]
    gu, gvn = _rowwise(_f_gelu_ln, [z2], gl_par, (F32, F32), tr=TR, name="gmlp_gelu_ln")
    causal = jnp.tril(jnp.ones((GMLP_CHUNK, GMLP_CHUNK), dtype=bool))
    ws_b = jnp.where(causal[None], p["gmlp_w_s"][0], 0.0).astype(BF16)
    bcol = jnp.pad(p["gmlp_b_s"][0].T, ((0, 0), (0, 128 - GMLP_HEADS)))
    uv = _gmlp_fwd(gu, gvn, ws_b, bcol, nck=4, name="gmlp_spatial")
    x_mid2, hm2 = _mm(uv, p["gmlp_w_out"], epi=_add_then_rms, extras=(x2, gm[2]), out_dtypes=(F32, BF16), tn=D,
                      name="gmlp_out")
    x3_, h3, mlp_saved2 = _mlp_fwd(x_mid2, hm2, mlp_in[2], mlp_out[2], g3, 2)

    fetch(3, h3)
    ng = len(ATT_DILS)
    att_in, o_tok, l_tok, lses = [], [], [], []
    offs = (0, PAIRS, 2 * PAIRS)
    for gi, dil in enumerate(ATT_DILS):
        w_g = _ColBlocks(p["attn_w_qkv_plain"], gi, ng, 3, ATT_W)
        arr = _mm(h3, w_g, out_dtypes=(BF16,), tm=2048, tn=ATT_W, name=f"attn_qkv_{gi}")
        arr = _deinterleave(arr, B, S, dil)
        att_in.append((arr, offs))
        og, lg = _att_fwd(arr, offs, nb=S // dil // ATT_BLK, nbk=8, name=f"attn_fwd_{gi}")
        lses.append(lg)
        o_tok.append(_interleave(og, B, S, dil))
        l_tok.append(_stats_to_tokens(lg, B, S, dil))
    merged2 = _rowwise(_f_merge, o_tok + l_tok, [], (BF16,), tr=TR, name="attn_merge")[0]
    x_mid3, hm3 = _mm(merged2, p["attn_w_o_plain"], epi=_add_then_rms, extras=(x3_, gm[3]), out_dtypes=(F32, BF16),
                      tn=D, name="attn_out")
    x4, _, mlp_saved3 = _mlp_fwd(x_mid3, hm3, mlp_in[3], mlp_out[3], None, 3)

    loss_part, gx, gxb, dgf = _loss_head(x4, tgt3.reshape(T, D), row(p["norm_final"]), tr=TR, name="loss_head")
    grads["norm_final"] = dgf.reshape(-1)
    d_norm_mix, d_norm_mlp = [None] * 4, [None] * 4
    Dq = D // N_CHIPS

    gx, gxb, d_norm_mlp[3], mlp_hi = _mlp_bwd(
        gx, gxb, x_mid3, row(p["norm_mlp"][3]), mlp_in[3], mlp_out[3], mlp_saved3, 3, nl, None)
    dmerged = _mm(gxb, p["attn_w_o"], tb=True, name="attn_dmerged")
    grads["attn_w_o"] = _mm(merged2, gxb, ta=True, tm=256, tn=256, tk=4096, out_dtypes=(BF16,), out=_grad_cols(ATT_W, Dq), name="attn_dwo")
    dml, _ = _rowwise_vjp(_f_merge, o_tok + l_tok, [], [dmerged], [F32] * 6, tr=TR, name="attn_merge_bwd")
    pieces = [[None] * ng for _ in range(3)]
    for gi, dil in enumerate(ATT_DILS):
        arr, offs = att_in[gi]
        dqkv_g = _att_bwd(arr, offs, _deinterleave(dml[gi], B, S, dil), lses[gi],
                          _stats_from_tokens(dml[ng + gi], B, S, dil), nb=S // dil // ATT_BLK, nbk=8,
                          name=f"attn_bwd_{gi}")
        for i in range(3):
            pieces[i][gi] = _interleave(dqkv_g[i], B, S, dil)
    dqkv = jnp.concatenate([pieces[i][gi] for i in range(3) for gi in range(ng)], axis=1)
    qkv_w = 3 * ng * ATT_W
    grads["attn_w_qkv"] = _mm(h3, dqkv, ta=True, tm=512, tn=1152, tk=4096, out_dtypes=(BF16,), out=_grad_cols(D, qkv_w // N_CHIPS),
                              name="attn_dwqkv")
    tok = grads_done({n: grads[n] for n in ("attn_w_qkv", "attn_w_o")})
    gx, gxb, d_norm_mix[3] = _mm_rms_bwd(dqkv, p["attn_w_qkv_plain"], x3_, g3, gx, tm=512, tk=qkv_w, after=tok, name="attn_dh")

    gx, gxb, d_norm_mlp[2], mlp_hi = _mlp_bwd(
        gx, gxb, x_mid2, row(p["norm_mlp"][2]), mlp_in[2], mlp_out[2], mlp_saved2, 2, nl, mlp_hi)
    tok = grads_done({"mlp_w_in": (1, mlp_hi[0]), "mlp_w_out": (1, mlp_hi[1])})
    duv = _mm(gxb, p["gmlp_w_out"], tb=True, after=tok, name="gmlp_duv")
    grads["gmlp_w_out"] = _mm(uv, gxb, ta=True, tm=128, tn=1024, tk=4096, out_dtypes=(BF16,), out=_grad_rows(Dq, D), name="gmlp_dwout")
    du, dvn, dws, dbcol = _gmlp_bwd(duv, gu, gvn, ws_b, bcol, nck=4, name="gmlp_spatial_bwd")
    grads["gmlp_w_s"] = jnp.where(causal[None], dws, 0.0)[None]
    grads["gmlp_b_s"] = dbcol[:, :GMLP_HEADS].T[None]
    (dz2,), (dlg, dlb_) = _rowwise_vjp(_f_gelu_ln, [z2], gl_par, [du, dvn], [BF16], tr=TR, name="gmlp_gelu_ln_bwd")
    grads["gmlp_ln_g"], grads["gmlp_ln_b"] = dlg, dlb_
    grads["gmlp_w_in"] = _mm(h2, dz2, ta=True, tm=512, tn=512, tk=4096, out_dtypes=(BF16,), out=_grad_cols(D, 2 * Dq), name="gmlp_dwin")
    tok = grads_done({n: grads[n] for n in ("gmlp_w_in", "gmlp_w_out")})
    gx, gxb, d_norm_mix[2] = _mm_rms_bwd(dz2, getattr(p["gmlp_w_in"], "plain", p["gmlp_w_in"]), x2, g2, gx, tm=512, tk=2 * D, after=tok,
                                         name="gmlp_dh")

    gx, gxb, d_norm_mlp[1], mlp_lo = _mlp_bwd(
        gx, gxb, x_mid1, row(p["norm_mlp"][1]), mlp_in[1], mlp_out[1], mlp_saved1, 1, nl, None)
    dqc = _mm(gxb, p["conv_w_pw2"], tb=True, name="conv_dq")
    grads["conv_w_pw2"] = _mm(qc, gxb, ta=True, tm=128, tn=1024, tk=4096, out_dtypes=(BF16,), out=_grad_rows(Dq, D), name="conv_dwpw2")
    _, (db2,) = _rowwise_vjp(lambda t, b: (t + b,), [gx], [p["conv_b_pw2"]], [gx], [None], tr=TR, name="conv_db2")
    grads["conv_b_pw2"] = db2
    (dyc,), (dbdw, dcg, dcb) = _rowwise_vjp(_f_ln_silu, [yc], ln_par, [dqc], [F32], tr=TR, name="conv_ln_silu_bwd")
    grads["conv_b_dw"], grads["conv_ln_g"], grads["conv_ln_b"] = dbdw, dcg, dcb
    dyp = jnp.pad(dyc.reshape(B, S, D), ((0, 0), (0, CONV_HALO), (0, 0)))
    dzg, dwdw = _conv_bwd(zp, dyp, w_dw, R=256, tc=128, name="conv_dw_bwd")
    grads["conv_w_dw"] = dwdw[:CONV_WIDTH][None]
    (dz1,), (db1,) = _rowwise_vjp(_f_bias_glu, [z1], [p["conv_b_pw1"]], [dzg.reshape(T, D)], [BF16], tr=TR,
                                  name="conv_glu_bwd")
    grads["conv_b_pw1"] = db1
    grads["conv_w_pw1"] = _mm(h1, dz1, ta=True, tm=512, tn=512, tk=4096, out_dtypes=(BF16,), out=_grad_cols(D, 2 * Dq), name="conv_dwpw1")
    tok = grads_done({n: grads[n] for n in ("conv_w_pw1", "conv_w_pw2")})
    gx, gxb, d_norm_mix[1] = _mm_rms_bwd(dz1, getattr(p["conv_w_pw1"], "plain", p["conv_w_pw1"]), x1, g1, gx, tm=512, tk=2 * D,
                                         after=tok, name="conv_dh")

    gx, gxb, d_norm_mlp[0], mlp_lo = _mlp_bwd(
        gx, gxb, x_mid0, row(p["norm_mlp"][0]), mlp_in[0], mlp_out[0], mlp_saved0, 0, nl, mlp_lo)
    tok = grads_done({"mlp_w_in": (0, mlp_lo[0]), "mlp_w_out": (0, mlp_lo[1])})
    (dz0,), _ = _rowwise_vjp(_f_glu, [z0], [], [gx], [BF16], after=tok, tr=TR, name="s5_glu_bwd")
    grads["ssm_w_glu"] = _mm(yb, dz0, ta=True, tm=512, tn=512, tk=4096, out_dtypes=(BF16,), out=_grad_cols(D, 2 * Dq), name="s5_dwglu")
    tok = grads_done({"ssm_w_glu": grads["ssm_w_glu"]})
    dypre = _mm(dz0, getattr(w_glu, "plain", w_glu), tb=True, tm=512, tk=2 * D,
                epi=lambda acc, yp: (jax.vjp(lambda t: jax.nn.gelu(t), yp)[1](acc)[0],), extras=(ypre,), name="s5_dypre")
    dh0, dbre, dbim, dcre, dcim, dabr, dabi, dd = _s5_bwd(
        dypre, h0, sxr, sxi, ser, sei, abr, abi, bre_b, bim_b, cre_b, cim_b, d_skip, B=B, L=S5_CHUNK, name="s5_bwd")
    s5_grads = s5_vjp((dabr, dabi, dbre, dbim, dcre, dcim))
    for nm, gv in zip(("ssm_a_re", "ssm_a_im", "ssm_b_re", "ssm_b_im", "ssm_c_re", "ssm_c_im", "ssm_log_dt"), s5_grads):
        grads[nm] = gv[None]
    grads["ssm_d"] = dd
    gx, _, d_norm_mix[0] = _rms_bwd(x, g0, dh0, gx, "rms_mix_bwd_0", after=tok)

    grads["norm_mix"] = jnp.concatenate(d_norm_mix, axis=0)
    grads["norm_mlp"] = jnp.concatenate(d_norm_mlp, axis=0)
    grads["mlp_w_in"], grads["mlp_w_out"] = (mlp_lo[0], mlp_hi[0]), (mlp_lo[1], mlp_hi[1])
    return loss_part, gx.reshape(B, S, D), grads


WEIGHTS = ['norm_mix', 'norm_mlp', 'norm_final', 'ssm_a_re', 'ssm_a_im', 'ssm_b_re', 'ssm_b_im', 'ssm_c_re',
           'ssm_c_im', 'ssm_d', 'ssm_log_dt', 'ssm_w_glu', 'conv_w_pw1', 'conv_b_pw1', 'conv_w_dw', 'conv_b_dw',
           'conv_ln_g', 'conv_ln_b', 'conv_w_pw2', 'conv_b_pw2', 'gmlp_w_in', 'gmlp_ln_g', 'gmlp_ln_b', 'gmlp_w_s',
           'gmlp_b_s', 'gmlp_w_out', 'attn_w_qkv', 'attn_w_o', 'mlp_w_in', 'mlp_w_out']
BIG_AXIS = {'ssm_w_glu': -1, 'conv_w_pw1': -1, 'conv_w_pw2': -2, 'gmlp_w_in': -1, 'gmlp_w_out': -2,
            'attn_w_qkv': -1, 'attn_w_o': -1, 'mlp_w_in': -1, 'mlp_w_out': -2}
BIG = list(BIG_AXIS)
LAYER_MIXER_WEIGHTS = (('ssm_w_glu',), ('conv_w_pw1', 'conv_w_pw2'), ('gmlp_w_in', 'gmlp_w_out'), ('attn_w_qkv', 'attn_w_o'))
SMALL_SHARDED = ['conv_b_pw1', 'conv_w_dw', 'conv_b_dw', 'conv_ln_g', 'conv_ln_b', 'conv_b_pw2', 'gmlp_ln_g', 'gmlp_ln_b']
SMALL_REPL = [n for n in WEIGHTS if n not in BIG_AXIS and n not in SMALL_SHARDED]
SMALL = SMALL_REPL + SMALL_SHARDED
LANES = 128
FLAT_COLS = 1024


def _pack(arrs, cols, row_mult):
    flat = jnp.concatenate([a.reshape(-1) for a in arrs])
    per = cols * row_mult
    n = -(-flat.shape[0] // per) * per
    return jnp.pad(flat, (0, n - flat.shape[0])).reshape(n // cols, cols)


def _unpack(flat2d, shapes):
    flat = flat2d.reshape(-1)
    out, off = [], 0
    for s in shapes:
        n = int(np.prod(s))
        out.append(flat[off:off + n].reshape(s))
        off += n
    return out


def _as_halves(shard):
    if shard.shape[0] == 1:
        shard = shard[0]
    return shard.reshape((2, shard.shape[0] // 2) + shard.shape[1:])


def _stored_weight(name, arr):
    kind = "cols" if BIG_AXIS[name] == -1 else "rows"
    if arr.shape[1] > 1:
        return [_Stored(arr, kind, lead=(li,)) for li in range(arr.shape[1])]
    arr = arr[:, 0]
    if kind == "rows":
        return arr.reshape(-1, arr.shape[-1])
    return _Stored(arr, kind)


def kernel(x, norm_mix, norm_mlp, norm_final, ssm_a_re, ssm_a_im, ssm_b_re, ssm_b_im, ssm_c_re, ssm_c_im, ssm_d, ssm_log_dt, ssm_w_glu, conv_w_pw1, conv_b_pw1, conv_w_dw, conv_b_dw, conv_ln_g, conv_ln_b, conv_w_pw2, conv_b_pw2, gmlp_w_in, gmlp_ln_g, gmlp_ln_b, gmlp_w_s, gmlp_b_s, gmlp_w_out, attn_w_qkv, attn_w_o, mlp_w_in, mlp_w_out, loss_target, m_norm_mix, m_norm_mlp, m_norm_final, m_ssm_a_re, m_ssm_a_im, m_ssm_b_re, m_ssm_b_im, m_ssm_c_re, m_ssm_c_im, m_ssm_d, m_ssm_log_dt, m_ssm_w_glu, m_conv_w_pw1, m_conv_b_pw1, m_conv_w_dw, m_conv_b_dw, m_conv_ln_g, m_conv_ln_b, m_conv_w_pw2, m_conv_b_pw2, m_gmlp_w_in, m_gmlp_ln_g, m_gmlp_ln_b, m_gmlp_w_s, m_gmlp_b_s, m_gmlp_w_out, m_attn_w_qkv, m_attn_w_o, m_mlp_w_in, m_mlp_w_out, v_norm_mix, v_norm_mlp, v_norm_final, v_ssm_a_re, v_ssm_a_im, v_ssm_b_re, v_ssm_b_im, v_ssm_c_re, v_ssm_c_im, v_ssm_d, v_ssm_log_dt, v_ssm_w_glu, v_conv_w_pw1, v_conv_b_pw1, v_conv_w_dw, v_conv_b_dw, v_conv_ln_g, v_conv_ln_b, v_conv_w_pw2, v_conv_b_pw2, v_gmlp_w_in, v_gmlp_ln_g, v_gmlp_ln_b, v_gmlp_w_s, v_gmlp_b_s, v_gmlp_w_out, v_attn_w_qkv, v_attn_w_o, v_mlp_w_in, v_mlp_w_out):
    args = dict(locals())
    w = {n: args[n] for n in WEIGHTS}
    m = {n: args["m_" + n] for n in WEIGHTS}
    v = {n: args["v_" + n] for n in WEIGHTS}
    chip = 2 * lax.axis_index("x") + lax.axis_index("y")
    core = lax.axis_index("c")

    big_shapes = [w[n].shape for n in BIG]
    sm_shapes = [w[n].shape for n in SMALL_SHARDED]
    sflat = _pack([w[n] for n in SMALL_SHARDED], LANES, 8)
    rs = sflat.shape[0]
    sall = _allgather8(sflat, name="gather_small")
    started, token = [], sall
    sall = sall.reshape(8, rs, LANES)
    for li, mixer in enumerate(LAYER_MIXER_WEIGHTS):
        names = list(mixer) + ["mlp_w_in", "mlp_w_out"]
        shards = [w[n][0] for n in mixer] + [w["mlp_w_in"][li], w["mlp_w_out"][li]]
        halves = [s.astype(BF16).reshape((2, s.shape[0] // 2) + s.shape[1:]) for s in shards]
        send_sems, recv_sems, halves, lands, token = _gather_start(halves, token, name=f"gather_start_{li}")
        started.append((names, (send_sems, recv_sems, halves, lands)))

    def layer_weights(li, after):
        names, st = started[li]
        halves, lands = _gather_wait(st, after, name=f"gather_wait_{li}")
        out = {}
        for n, h, arr in zip(names, halves, lands):
            arr = lax.dynamic_update_index_in_dim(arr, h, chip, axis=0)
            arr = arr.reshape((N_CHIPS, arr.shape[1] * arr.shape[2]) + arr.shape[3:])
            if BIG_AXIS[n] == -1:
                out[n] = _Stored(arr, "cols")
                if n in ("attn_w_qkv", "attn_w_o"):
                    out[n + "_plain"] = jnp.swapaxes(arr, 0, 1).reshape(arr.shape[1], -1)
                out[n].plain = jnp.swapaxes(arr, 0, 1).reshape(arr.shape[1], -1)
            else:
                out[n] = arr.reshape(-1, arr.shape[-1])
        return out

    p = {}
    per_chip = [_unpack(sall[2 * k], sm_shapes) for k in range(N_CHIPS)]
    for i, n in enumerate(SMALL_SHARDED):
        p[n] = jnp.concatenate([per_chip[k][i] for k in range(N_CHIPS)], axis=-1)
    for n in SMALL_REPL:
        p[n] = w[n]
    p['conv_w_dw'] = p['conv_w_dw'][0]

    in_flight, arrived, n_rounds = [], {}, [0]

    def finish_round(after):
        k, names, plans, st = in_flight.pop(0)
        srcs, lands = _reduce_wait(st, plans, after, name=f"grads_wait_{k}")
        for n, plan, src, land in zip(names, plans, srcs, lands):
            arrived.setdefault(n, []).append((plan, src, land))

    def grads_done(group):
        names, srcs, plans, lands = [], [], [], []
        for n, v in group.items():
            if isinstance(v, tuple):
                src, plan = v[1].reshape(1, N_CHIPS, -1, v[1].shape[-1]), ((0, v[0]),)
                while any(n in rd[1] for rd in in_flight):
                    finish_round(src)
            else:
                src, plan = v.reshape(2, N_CHIPS, -1, v.shape[-1]), ((0, 0), (1, 1))
            land = arrived[n][-1][2] if n in arrived else lax.empty((2 * N_CHIPS,) + src.shape[2:], BF16)
            names.append(n), srcs.append(src), plans.append(plan), lands.append(land)
        st = _reduce_start(srcs, lands, plans, None, name=f"grads_start_{n_rounds[0]}")
        in_flight.append((n_rounds[0], names, plans, st[:4]))
        n_rounds[0] += 1
        return st[4]

    loss_part, grad_x, g = _local_step(x, loss_target, p, layer_weights, token, grads_done)
    loss = lax.psum(loss_part[0, 0], ("x", "y", "c"))

    my_id = 2 * chip + core
    while in_flight:
        finish_round(grad_x)
    totals = []
    for n in BIG:
        own = None
        for plan, src, land in arrived[n]:
            slab = lax.dynamic_index_in_dim(src, chip, axis=1, keepdims=False)
            if len(plan) == 2:
                own = lax.dynamic_index_in_dim(slab, core, axis=0, keepdims=False)
            else:
                own = slab[0] if own is None else jnp.where(core == plan[0][1], slab[0], own)
        totals.append(_sum8(arrived[n][-1][2], own, my_id, name="owner_sum_" + n))
    shared = _share_halves(totals, name="grads_share_halves")
    big_grads = {}
    for n, arr, t in zip(BIG, shared, totals):
        arr = lax.dynamic_update_index_in_dim(arr, t[None], core, axis=0)
        big_grads[n] = arr.reshape(w[n].shape)

    small_full_shapes = [g[n].shape for n in SMALL]
    gs = _pack([g[n] for n in SMALL], LANES, 8)
    rg = gs.shape[0]
    gs_all = _allgather8(gs, name="gather_small_grads").reshape(8, rg, LANES)
    gs_sum = _rowwise(lambda *a: (functools.reduce(lambda s, t: s + t, a),), [gs_all[k] for k in range(8)], [], (F32,),
                      tr=rg, name="small_grads_sum")[0]
    small_grads = dict(zip(SMALL, _unpack(gs_sum, small_full_shapes)))
    for n in SMALL:
        small_grads[n] = small_grads[n].reshape(p_shape_full(w[n], -1 if n in SMALL_SHARDED else None))
    for n in SMALL_SHARDED:
        width = w[n].shape[-1]
        small_grads[n] = lax.dynamic_slice_in_dim(small_grads[n], chip * width, width, axis=-1)

    grad, delta, new_m, new_v = {}, {}, {}, {}
    for n in BIG:
        shape = w[n].shape
        two_d = lambda t: t.reshape(-1, shape[-1])
        grad[n] = big_grads[n]
        d_, m_, v_ = _adamw(two_d(w[n]), two_d(grad[n]), two_d(m[n]), two_d(v[n]), name="adamw_" + n)
        delta[n], new_m[n], new_v[n] = d_.reshape(shape), m_.reshape(shape), v_.reshape(shape)
    for n in SMALL:
        shape = w[n].shape
        two_d = lambda t: t.reshape(-1, shape[-1])
        grad[n] = small_grads[n]
        d_, m_, v_ = _adamw(two_d(w[n]), two_d(grad[n]), two_d(m[n]), two_d(v[n]), name="adamw_" + n)
        delta[n], new_m[n], new_v[n] = d_.reshape(shape), m_.reshape(shape), v_.reshape(shape)

    return (loss, grad_x, *[grad[n] for n in WEIGHTS], *[delta[n] for n in WEIGHTS],
            *[new_m[n] for n in WEIGHTS], *[new_v[n] for n in WEIGHTS])


def p_shape_full(shard, axis):
    s = list(shard.shape)
    if axis is not None:
        s[axis] *= N_CHIPS
    return tuple(s)
```

```python
import functools
import math

import jax
import jax.numpy as jnp
import numpy as np
from jax import lax
from jax.experimental import pallas as pl
from jax.experimental.pallas import tpu as pltpu

F32 = jnp.float32
BF16 = jnp.bfloat16
MESH = pl.DeviceIdType.MESH

EPS = 1e-6
SSM_GROUP = 16
SSM_STATE = 64
CONV_WIDTH = 31
CONV_HALO = 32
GMLP_CHUNK = 128
GMLP_HEADS = 4
ATT_DILS = (1, 4, 16)
ATT_BLK = 128
ATT_HEADS = 8
HEAD_DIM = 64
ATT_W = ATT_HEADS * HEAD_DIM
N_CHIPS = 4
ADAM_LR, ADAM_B1, ADAM_B2, ADAM_EPS, ADAM_WD, ADAM_STEP = 1e-3, 0.9, 0.999, 1e-8, 0.01, 10

VMEM_BYTES_V7X = 64 * 1024 * 1024
VMEM_LIMIT = VMEM_BYTES_V7X - 8 * 1024 * 1024
MASK_VALUE = -1e30
LANE_TILE = 128


def _cp(sem=None):
    return pltpu.CompilerParams(dimension_semantics=sem, vmem_limit_bytes=VMEM_LIMIT)


def _pick_tile(total, target):
    for cand in range(min(target, total) // LANE_TILE * LANE_TILE, 0, -LANE_TILE):
        if total % cand == 0:
            return cand
    return total


class _Stored:
    def __init__(self, arr, kind="plain", lead=()):
        self.arr, self.kind, self.lead = arr, kind, tuple(lead)
        r, c = arr.shape[-2:]
        self.shape = (r, c * N_CHIPS) if kind == "cols" else (r * N_CHIPS, c) if kind == "rows" else (r, c)

    def spec(self, br, bc, rc_of):
        lead, nl = self.lead, len(self.lead)
        if self.kind == "plain":
            return pl.BlockSpec((None,) * nl + (br, bc), lambda i, j, k: (*lead, *rc_of(i, j, k)))
        if self.kind == "cols":
            per = self.arr.shape[-1] // bc
            assert per * bc == self.arr.shape[-1]

            def imap(i, j, k):
                r, c = rc_of(i, j, k)
                return (c // per, *lead, r, c % per)
        else:
            per = self.arr.shape[-2] // br
            assert per * br == self.arr.shape[-2]

            def imap(i, j, k):
                r, c = rc_of(i, j, k)
                return (r // per, *lead, r % per, c)
        return pl.BlockSpec((None,) * (nl + 1) + (br, bc), imap)


class _ColBlocks:
    kind = "colblocks"

    def __init__(self, arr, first, stride, count, width):
        self.arr, self.first, self.stride, self.width = arr, first, stride, width
        self.shape = (arr.shape[0], count * width)

    def spec(self, br, bc, rc_of):
        per = self.width // bc
        assert per * bc == self.width

        def imap(i, j, k):
            r, c = rc_of(i, j, k)
            return (r, (self.first + (c // per) * self.stride) * per + c % per)
        return pl.BlockSpec((br, bc), imap)


def _mm(a, b, *, ta=False, tb=False, out_dtypes=(F32,), tm=1024, tn=1024, tk=1024,
        pro_a=None, pro_b=None, epi=None, extras=(), n_row_sums=0, out=None, after=None, name):
    if ta:
        K, M = a.shape
    else:
        M, K = a.shape
    if not isinstance(b, (_Stored, _ColBlocks)):
        b = _Stored(b)
    N, Kb = b.shape if tb else b.shape[::-1]
    assert K == Kb, (a.shape, b.shape, ta, tb)
    col_unit = b.width if b.kind == "colblocks" else b.arr.shape[-1] if b.kind == "cols" else b.shape[1]
    row_unit = b.arr.shape[-2] if b.kind == "rows" else b.shape[0]
    n_unit, k_unit = (row_unit, col_unit) if tb else (col_unit, row_unit)
    m_unit = M
    if out is not None:
        m_unit, n_unit = out[4], math.gcd(n_unit, out[5])
    tm, tn, tk = _pick_tile(m_unit, tm), _pick_tile(n_unit, tn), _pick_tile(k_unit, tk)
    assert not n_row_sums or tn == N
    nk = K // tk
    a_spec = (pl.BlockSpec((tk, tm), lambda i, j, k: (k, i)) if ta
              else pl.BlockSpec((tm, tk), lambda i, j, k: (i, k)))
    b_spec = b.spec(tn, tk, lambda i, j, k: (j, k)) if tb else b.spec(tk, tn, lambda i, j, k: (k, j))
    ex_specs = []
    for e in extras:
        if e.shape[0] == 1:
            ex_specs.append(pl.BlockSpec((1, tn), lambda i, j, k: (0, j)))
        else:
            assert e.shape == (M, N), (e.shape, M, N)
            ex_specs.append(pl.BlockSpec((tm, tn), lambda i, j, k: (i, j)))
    dims = (((0 if ta else 1,), (1 if tb else 0,)), ((), ()))
    n_ex, n_out = len(extras), len(out_dtypes)
    direct = epi is None and n_out == 1 and out_dtypes[0] == F32
    use_acc = nk > 1 and not direct
    operands, aliases, alias_specs = [a, b.arr, *extras], {}, []
    if after is not None:
        operands.append(after)
        alias_specs.append(pl.BlockSpec(memory_space=pl.ANY))
    if out is None:
        n_tile_out = n_out - n_row_sums
        out_specs = ([pl.BlockSpec((tm, tn), lambda i, j, k: (i, j))] * n_tile_out
                     + [pl.BlockSpec((1, tn), lambda i, j, k: (0, j))] * n_row_sums)
        out_shape = ([jax.ShapeDtypeStruct((M, N), dt) for dt in out_dtypes[:n_tile_out]]
                     + [jax.ShapeDtypeStruct((1, N), dt) for dt in out_dtypes[n_tile_out:]])
    else:
        shape, block_fn, imap_fn, alias = out[:4]
        assert n_out == 1
        out_specs = [pl.BlockSpec(block_fn(tm, tn), imap_fn(tm, tn))]
        out_shape = [jax.ShapeDtypeStruct(shape, out_dtypes[0])]
        if alias is not None:
            operands.append(alias)
            aliases = {len(operands) - 1: 0}
            alias_specs.append(pl.BlockSpec(memory_space=pl.ANY))
    n_in = len(operands)

    def finish(r, ex, outs, first_row_tile):
        res = epi(r, *[e[...] for e in ex]) if epi is not None else (r,)
        n_tile_out = n_out - n_row_sums
        for o, v in zip(outs[:n_tile_out], res):
            o[...] = v.astype(o.dtype)
        for o, v in zip(outs[n_tile_out:], res[n_tile_out:]):
            @pl.when(first_row_tile)
            def _(o=o):
                o[...] = jnp.zeros_like(o)
            o[...] += v

    def body(*refs):
        a_ref, b_ref = refs[:2]
        ex = refs[2:2 + n_ex]
        outs = refs[n_in:n_in + n_out]
        first_row_tile = pl.program_id(0) == 0
        at, bt = a_ref[...], b_ref[...]
        if pro_a is not None:
            at = pro_a(at)
        if pro_b is not None:
            bt = pro_b(bt)
        part = lax.dot_general(at, bt, dims, preferred_element_type=F32)
        if nk == 1:
            finish(part, ex, outs, first_row_tile)
            return
        acc = refs[-1] if use_acc else outs[0]
        k = pl.program_id(2)

        @pl.when(k == 0)
        def _():
            acc[...] = part

        @pl.when(k > 0)
        def _():
            acc[...] += part

        if use_acc:
            @pl.when(k == nk - 1)
            def _():
                finish(acc[...], ex, outs, first_row_tile)

    res = pl.pallas_call(
        body, grid=(M // tm, N // tn, nk),
        in_specs=[a_spec, b_spec] + ex_specs + alias_specs,
        out_specs=out_specs, out_shape=out_shape,
        scratch_shapes=[pltpu.VMEM((tm, tn), F32)] if use_acc else [],
        input_output_aliases=aliases,
        compiler_params=_cp(("arbitrary" if n_row_sums else "parallel", "parallel", "arbitrary")), name=name,
    )(*operands)
    return res[0] if n_out == 1 else res


def _pick_rows(total, target):
    for cand in range(min(target, total) // 8 * 8, 0, -8):
        if total % cand == 0:
            return cand
    return total


def _rowwise(f, rows, params, out_dtypes, *, tr, name):
    T = rows[0].shape[0]
    tr = _pick_rows(T, tr)
    nr, npar = len(rows), len(params)
    blk = [jax.ShapeDtypeStruct((tr, r.shape[1]), F32) for r in rows]
    blk += [jax.ShapeDtypeStruct(p.shape, F32) for p in params]
    out_avals = jax.eval_shape(f, *blk)

    def body(*refs):
        res = f(*[r[...].astype(F32) for r in refs[:nr + npar]])
        for o, v in zip(refs[nr + npar:], res):
            o[...] = v.astype(o.dtype)

    out = pl.pallas_call(
        body, grid=(T // tr,),
        in_specs=[pl.BlockSpec((tr, r.shape[1]), lambda i: (i, 0)) for r in rows]
        + [pl.BlockSpec(p.shape, lambda i, nd=p.ndim: (0,) * nd) for p in params],
        out_specs=[pl.BlockSpec((tr, o.shape[1]), lambda i: (i, 0)) for o in out_avals],
        out_shape=[jax.ShapeDtypeStruct((T, o.shape[1]), dt) for o, dt in zip(out_avals, out_dtypes)],
        compiler_params=_cp(("parallel",)), name=name,
    )(*rows, *params)
    return out


def _rowwise_vjp(f, rows, params, cots, drow_dtypes, *, adds=None, after=None, tr, name):
    adds = adds or {}
    T = rows[0].shape[0]
    tr = _pick_rows(T, tr)
    nr, npar, nc = len(rows), len(params), len(cots)
    want, want_dt = [], []
    for i, dt in enumerate(drow_dtypes):
        for one in (dt if isinstance(dt, tuple) else (dt,)):
            if one is not None:
                want.append(i)
                want_dt.append(one)
    add_idx = sorted(set(i for i in want if i in adds))
    add_arrays = [adds[i] for i in add_idx]
    na = len(add_arrays)
    extra = [] if after is None else [after]

    def body(*refs):
        ins = [r[...].astype(F32) for r in refs[:nr + npar]]
        cvals = [r[...].astype(F32) for r in refs[nr + npar:nr + npar + nc]]
        avals = refs[nr + npar + nc:nr + npar + nc + na]
        outs = refs[nr + npar + nc + na + len(extra):]
        _, vjp = jax.vjp(f, *ins)
        grads = vjp(tuple(cvals))
        for o, i in zip(outs[:len(want)], want):
            g = grads[i]
            if i in adds:
                g = g + avals[add_idx.index(i)][...].astype(F32)
            o[...] = g.astype(o.dtype)
        step = pl.program_id(0)
        for o, g in zip(outs[len(want):], grads[nr:]):
            @pl.when(step == 0)
            def _(o=o):
                o[...] = jnp.zeros_like(o)
            o[...] += g

    rspec = lambda r: pl.BlockSpec((tr, r.shape[1]), lambda i: (i, 0))
    pspec = lambda p: pl.BlockSpec(p.shape, lambda i, nd=p.ndim: (0,) * nd)
    out = pl.pallas_call(
        body, grid=(T // tr,),
        in_specs=[rspec(r) for r in rows] + [pspec(p) for p in params] + [rspec(c) for c in cots]
        + [rspec(a) for a in add_arrays] + [pl.BlockSpec(memory_space=pl.ANY)] * len(extra),
        out_specs=[rspec(rows[i]) for i in want] + [pspec(p) for p in params],
        out_shape=[jax.ShapeDtypeStruct(rows[i].shape, dt) for i, dt in zip(want, want_dt)]
        + [jax.ShapeDtypeStruct(p.shape, F32) for p in params],
        compiler_params=_cp(("arbitrary",)), name=name,
    )(*rows, *params, *cots, *add_arrays, *extra)
    return out[:len(want)], out[len(want):]


def _f_rms(x, g):
    return (x * lax.rsqrt(jnp.mean(x * x, axis=-1, keepdims=True) + EPS) * g,)


def _ln(x, g, b):
    mu = jnp.mean(x, axis=-1, keepdims=True)
    var = jnp.mean(jnp.square(x - mu), axis=-1, keepdims=True)
    return (x - mu) * lax.rsqrt(var + EPS) * g + b


def _f_glu(z):
    d = z.shape[1] // 2
    return (z[:, :d] * jax.nn.sigmoid(z[:, d:]),)


def _f_bias_glu(z, b):
    return _f_glu(z + b)


def _f_ln_silu(y, b_dw, g, b):
    return (jax.nn.silu(_ln(y + b_dw, g, b)),)


def _f_gelu_ln(z, g, b):
    d = z.shape[1] // 2
    zz = jax.nn.gelu(z)
    return zz[:, :d], _ln(zz[:, d:], g, b)


def _f_merge(o0, o1, o2, l0, l1, l2):
    m = jnp.maximum(jnp.maximum(l0, l1), l2)
    e0, e1, e2 = jnp.exp(l0 - m), jnp.exp(l1 - m), jnp.exp(l2 - m)
    s = e0 + e1 + e2
    pair = 2 * HEAD_DIM
    first_head = lax.broadcasted_iota(jnp.int32, (o0.shape[0], pair), 1) < HEAD_DIM
    cols = []
    for hp in range(o0.shape[1] // pair):
        acc = None
        for o, e in ((o0, e0), (o1, e1), (o2, e2)):
            wgt = e / s
            wp = jnp.where(first_head, wgt[:, 2 * hp:2 * hp + 1], wgt[:, 2 * hp + 1:2 * hp + 2])
            term = wp * o[:, hp * pair:(hp + 1) * pair]
            acc = term if acc is None else acc + term
        cols.append(acc)
    return (jnp.concatenate(cols, axis=1),)


def _loss_head(x, tgt, g, *, tr, name):
    T, D = x.shape
    tr = min(tr, T)

    def f(xv, gv, tv):
        y = _f_rms(xv, gv)[0]
        return 0.5 * jnp.mean(jnp.square(y - tv), axis=-1, keepdims=True)

    def body(x_ref, t_ref, g_ref, loss_ref, dx_ref, dxb_ref, dg_ref):
        tv = t_ref[...]
        l, vjp = jax.vjp(lambda xv, gv: f(xv, gv, tv), x_ref[...], g_ref[...])
        dx, dg = vjp(jnp.ones_like(l))
        dx_ref[...] = dx
        dxb_ref[...] = dx.astype(BF16)

        @pl.when(pl.program_id(0) == 0)
        def _():
            loss_ref[...] = jnp.zeros_like(loss_ref)
            dg_ref[...] = jnp.zeros_like(dg_ref)

        loss_ref[...] += jnp.sum(l)
        dg_ref[...] += dg

    return pl.pallas_call(
        body, grid=(T // tr,),
        in_specs=[pl.BlockSpec((tr, D), lambda i: (i, 0)), pl.BlockSpec((tr, D), lambda i: (i, 0)),
                  pl.BlockSpec((1, D), lambda i: (0, 0))],
        out_specs=[pl.BlockSpec((1, 128), lambda i: (0, 0)), pl.BlockSpec((tr, D), lambda i: (i, 0)),
                   pl.BlockSpec((tr, D), lambda i: (i, 0)), pl.BlockSpec((1, D), lambda i: (0, 0))],
        out_shape=[jax.ShapeDtypeStruct((1, 128), F32), jax.ShapeDtypeStruct((T, D), F32),
                   jax.ShapeDtypeStruct((T, D), BF16), jax.ShapeDtypeStruct((1, D), F32)],
        compiler_params=_cp(("arbitrary",)), name=name,
    )(x, tgt, g)


def _adamw(w, g, m, v, *, name):
    R, C = w.shape
    tr = _pick_rows(R, max(8, 2 * 1024 * 1024 // (4 * C)))
    c1 = 1.0 - ADAM_B1 ** ADAM_STEP
    c2 = 1.0 - ADAM_B2 ** ADAM_STEP

    def body(w_ref, g_ref, m_ref, v_ref, d_ref, nm_ref, nv_ref):
        gv = g_ref[...]
        nm = ADAM_B1 * m_ref[...] + (1.0 - ADAM_B1) * gv
        nv = ADAM_B2 * v_ref[...] + (1.0 - ADAM_B2) * jnp.square(gv)
        nm_ref[...] = nm
        nv_ref[...] = nv
        d_ref[...] = -ADAM_LR * ((nm / c1) / (jnp.sqrt(nv / c2) + ADAM_EPS) + ADAM_WD * w_ref[...])

    spec = pl.BlockSpec((tr, C), lambda i: (i, 0))
    return pl.pallas_call(
        body, grid=(R // tr,), in_specs=[spec] * 4, out_specs=[spec] * 3,
        out_shape=[jax.ShapeDtypeStruct((R, C), F32)] * 3,
        compiler_params=_cp(("parallel",)), name=name,
    )(w, g, m, v)


def _s5_prep(a_re, a_im, b_re, b_im, c_re, c_im, log_dt):
    G, N = a_re.shape
    P = b_re.shape[2]
    gpb = 128 // P
    nblk = G // gpb
    dt = jnp.exp(log_dt)[:, None]
    mag = jnp.exp(a_re * dt)
    abr, abi = mag * jnp.cos(a_im * dt), mag * jnp.sin(a_im * dt)
    den = a_re * a_re + a_im * a_im
    nr, ni = abr - 1.0, abi
    qr, qi = (nr * a_re + ni * a_im) / den, (ni * a_re - nr * a_im) / den
    bbr = qr[..., None] * b_re - qi[..., None] * b_im
    bbi = qr[..., None] * b_im + qi[..., None] * b_re
    eye = jnp.eye(gpb, dtype=F32)

    def expand_b(t):
        t = t.reshape(nblk, gpb, N, P).transpose(0, 1, 3, 2)
        return (t[:, :, :, None, :] * eye[None, :, None, :, None]).reshape(nblk, gpb * P, gpb * N)

    def expand_c(t):
        t = t.reshape(nblk, gpb, P, N).transpose(0, 1, 3, 2)
        return (t[:, :, :, None, :] * eye[None, :, None, :, None]).reshape(nblk, gpb * N, gpb * P)

    return (abr.reshape(1, G * N), abi.reshape(1, G * N), expand_b(bbr), expand_b(bbi),
            expand_c(c_re), expand_c(c_im))


def _s5_fwd(h, abr, abi, bre, bim, cre, cim, d, *, B, L, name):
    T, D = h.shape
    S = T // B
    L = min(L, S)
    nc = S // L
    nblk, cb, sb = bre.shape
    GN = abr.shape[1]

    def body(h_ref, ar_ref, ai_ref, bre_ref, bim_ref, cre_ref, cim_ref, d_ref,
             y_ref, yb_ref, xr_ref, xi_ref, er_ref, ei_ref, sr, si, car, cai):
        ci = pl.program_id(1)

        @pl.when(ci == 0)
        def _():
            car[...] = jnp.zeros_like(car)
            cai[...] = jnp.zeros_like(cai)

        for j in range(nblk):
            u = h_ref[:, j * cb:(j + 1) * cb]
            sr[:, j * sb:(j + 1) * sb] = jnp.dot(u, bre_ref[j], preferred_element_type=F32)
            si[:, j * sb:(j + 1) * sb] = jnp.dot(u, bim_ref[j], preferred_element_type=F32)
        ar, ai = ar_ref[...], ai_ref[...]

        def step(t, carry):
            pr, pi = carry
            nr = ar * pr - ai * pi + sr[pl.ds(t, 1), :]
            ni = ar * pi + ai * pr + si[pl.ds(t, 1), :]
            sr[pl.ds(t, 1), :] = nr
            si[pl.ds(t, 1), :] = ni
            return nr, ni

        pr, pi = lax.fori_loop(0, L, step, (car[...], cai[...]), unroll=4)
        car[...] = pr
        cai[...] = pi
        er_ref[0] = pr
        ei_ref[0] = pi
        for j in range(nblk):
            xr = sr[:, j * sb:(j + 1) * sb].astype(BF16)
            xi = si[:, j * sb:(j + 1) * sb].astype(BF16)
            xr_ref[:, j * sb:(j + 1) * sb] = xr
            xi_ref[:, j * sb:(j + 1) * sb] = xi
            y = (jnp.dot(xr, cre_ref[j], preferred_element_type=F32)
                 - jnp.dot(xi, cim_ref[j], preferred_element_type=F32))
            u = h_ref[:, j * cb:(j + 1) * cb].astype(F32)
            y = y + d_ref[:, j * cb:(j + 1) * cb] * u
            y_ref[:, j * cb:(j + 1) * cb] = y
            yb_ref[:, j * cb:(j + 1) * cb] = jax.nn.gelu(y).astype(BF16)

    tok = lambda w: pl.BlockSpec((L, w), lambda b, c: (b * nc + c, 0))
    whole = lambda p: pl.BlockSpec(p.shape, lambda b, c, nd=p.ndim: (0,) * nd)
    end = pl.BlockSpec((1, 1, GN), lambda b, c: (b * nc + c, 0, 0))
    return pl.pallas_call(
        body, grid=(B, nc),
        in_specs=[tok(D)] + [whole(p) for p in (abr, abi, bre, bim, cre, cim, d)],
        out_specs=[tok(D), tok(D), tok(GN), tok(GN), end, end],
        out_shape=[jax.ShapeDtypeStruct((T, D), F32), jax.ShapeDtypeStruct((T, D), BF16),
                   jax.ShapeDtypeStruct((T, GN), BF16),
                   jax.ShapeDtypeStruct((T, GN), BF16), jax.ShapeDtypeStruct((B * nc, 1, GN), F32),
                   jax.ShapeDtypeStruct((B * nc, 1, GN), F32)],
        scratch_shapes=[pltpu.VMEM((L, GN), F32), pltpu.VMEM((L, GN), F32),
                        pltpu.VMEM((1, GN), F32), pltpu.VMEM((1, GN), F32)],
        compiler_params=_cp(("arbitrary", "arbitrary")), name=name,
    )(h, abr, abi, bre, bim, cre, cim, d)


def _s5_bwd(dy, h, xr, xi, er, ei, abr, abi, bre, bim, cre, cim, d, *, B, L, name):
    T, D = h.shape
    S = T // B
    L = min(L, S)
    nc = S // L
    nblk, cb, sb = bre.shape
    GN = abr.shape[1]
    dims_nt = (((1,), (1,)), ((), ()))
    dims_tn = (((0,), (0,)), ((), ()))

    def body(dy_ref, h_ref, xr_ref, xi_ref, er_ref, ei_ref, ar_ref, ai_ref, bre_ref, bim_ref,
             cre_ref, cim_ref, d_ref,
             dh_ref, dbre_ref, dbim_ref, dcre_ref, dcim_ref, dar_ref, dai_ref, dd_ref,
             lr, li, car, cai):
        b, cstep = pl.program_id(0), pl.program_id(1)
        ci = nc - 1 - cstep

        @pl.when((b == 0) & (cstep == 0))
        def _():
            for r in (dbre_ref, dbim_ref, dcre_ref, dcim_ref, dar_ref, dai_ref, dd_ref):
                r[...] = jnp.zeros_like(r)

        @pl.when(cstep == 0)
        def _():
            car[...] = jnp.zeros_like(car)
            cai[...] = jnp.zeros_like(cai)

        for j in range(nblk):
            dyj = dy_ref[:, j * cb:(j + 1) * cb].astype(BF16)
            lr[:, j * sb:(j + 1) * sb] = lax.dot_general(dyj, cre_ref[j], dims_nt, preferred_element_type=F32)
            li[:, j * sb:(j + 1) * sb] = -lax.dot_general(dyj, cim_ref[j], dims_nt, preferred_element_type=F32)
        ar, ai = ar_ref[...], ai_ref[...]

        def step(s, carry):
            t = L - 1 - s
            pr, pi = carry
            nr = lr[pl.ds(t, 1), :] + ar * pr + ai * pi
            ni = li[pl.ds(t, 1), :] - ai * pr + ar * pi
            lr[pl.ds(t, 1), :] = nr
            li[pl.ds(t, 1), :] = ni
            return nr, ni

        pr, pi = lax.fori_loop(0, L, step, (car[...], cai[...]), unroll=4)
        car[...] = pr
        cai[...] = pi
        has_prev = (ci > 0).astype(F32)
        first_row = lax.broadcasted_iota(jnp.int32, (L, sb), 0) == 0
        for j in range(nblk):
            cs = slice(j * cb, (j + 1) * cb)
            ss = slice(j * sb, (j + 1) * sb)
            lrj, lij = lr[:, ss], li[:, ss]
            xrj, xij = xr_ref[:, ss], xi_ref[:, ss]
            pr_j = jnp.where(first_row, er_ref[0][:, ss] * has_prev, pltpu.roll(xrj.astype(F32), 1, 0))
            pi_j = jnp.where(first_row, ei_ref[0][:, ss] * has_prev, pltpu.roll(xij.astype(F32), 1, 0))
            dar_ref[:, ss] += jnp.sum(lrj * pr_j + lij * pi_j, axis=0, keepdims=True)
            dai_ref[:, ss] += jnp.sum(lij * pr_j - lrj * pi_j, axis=0, keepdims=True)
            lrb, lib = lrj.astype(BF16), lij.astype(BF16)
            hj = h_ref[:, cs]
            dyf = dy_ref[:, cs]
            dyj = dyf.astype(BF16)
            dbre_ref[j] += lax.dot_general(hj, lrb, dims_tn, preferred_element_type=F32)
            dbim_ref[j] += lax.dot_general(hj, lib, dims_tn, preferred_element_type=F32)
            dcre_ref[j] += lax.dot_general(xrj, dyj, dims_tn, preferred_element_type=F32)
            dcim_ref[j] -= lax.dot_general(xij, dyj, dims_tn, preferred_element_type=F32)
            du = (lax.dot_general(lrb, bre_ref[j], dims_nt, preferred_element_type=F32)
                  + lax.dot_general(lib, bim_ref[j], dims_nt, preferred_element_type=F32))
            dh_ref[:, cs] = du + d_ref[:, cs] * dyf
            dd_ref[:, cs] += jnp.sum(dyf * hj.astype(F32), axis=0, keepdims=True)

    tok = lambda w: pl.BlockSpec((L, w), lambda b, c: (b * nc + nc - 1 - c, 0))
    whole = lambda p: pl.BlockSpec(p.shape, lambda b, c, nd=p.ndim: (0,) * nd)
    prev_end = pl.BlockSpec((1, 1, GN), lambda b, c: (b * nc + jnp.maximum(nc - 2 - c, 0), 0, 0))
    params = (abr, abi, bre, bim, cre, cim, d)
    acc_shapes = [bre.shape, bim.shape, cre.shape, cim.shape, abr.shape, abi.shape, d.shape]
    out = pl.pallas_call(
        body, grid=(B, nc),
        in_specs=[tok(D), tok(D), tok(GN), tok(GN), prev_end, prev_end] + [whole(p) for p in params],
        out_specs=[tok(D)] + [pl.BlockSpec(s, lambda b, c, nd=len(s): (0,) * nd) for s in acc_shapes],
        out_shape=[jax.ShapeDtypeStruct((T, D), F32)] + [jax.ShapeDtypeStruct(s, F32) for s in acc_shapes],
        scratch_shapes=[pltpu.VMEM((L, GN), F32), pltpu.VMEM((L, GN), F32),
                        pltpu.VMEM((1, GN), F32), pltpu.VMEM((1, GN), F32)],
        compiler_params=_cp(("arbitrary", "arbitrary")), name=name,
    )(dy, h, xr, xi, er, ei, *params)
    return out


def _conv_fwd(zp, w, *, R, tc, name):
    B, SP, C = zp.shape
    S = SP - CONV_HALO
    R, tc = min(R, S), min(tc, C)

    def body(z_ref, w_ref, y_ref):
        def chunk(ci, _):
            start = pl.multiple_of(ci * R, 8)
            ze = z_ref[pl.ds(start, R + CONV_HALO), :]
            acc = jnp.zeros((R, tc), F32)
            for m in range(CONV_WIDTH):
                k = CONV_WIDTH - 1 - m
                sh = ze if m == 0 else pltpu.roll(ze, m, 0)
                acc = acc + w_ref[k:k + 1, :] * sh[CONV_HALO:, :]
            y_ref[pl.ds(start, R), :] = acc
            return 0

        lax.fori_loop(0, S // R, chunk, 0)

    return pl.pallas_call(
        body, grid=(B, C // tc),
        in_specs=[pl.BlockSpec((None, SP, tc), lambda b, c: (b, 0, c)),
                  pl.BlockSpec((32, tc), lambda b, c: (0, c))],
        out_specs=pl.BlockSpec((None, S, tc), lambda b, c: (b, 0, c)),
        out_shape=jax.ShapeDtypeStruct((B, S, C), F32),
        compiler_params=_cp(("parallel", "parallel")), name=name,
    )(zp, w)


def _conv_bwd(zp, dyp, w, *, R, tc, name):
    B, SP, C = zp.shape
    S = SP - CONV_HALO
    R, tc = min(R, S), min(tc, C)

    def body(z_ref, dy_ref, w_ref, dz_ref, dw_ref):
        @pl.when(pl.program_id(1) == 0)
        def _():
            dw_ref[...] = jnp.zeros_like(dw_ref)

        def chunk(ci, _):
            start = pl.multiple_of(ci * R, 8)
            zc = z_ref[pl.ds(start + CONV_HALO, R), :]
            de = dy_ref[pl.ds(start, R + CONV_HALO), :]
            acc = jnp.zeros((R, tc), F32)
            for m in range(CONV_WIDTH):
                k = CONV_WIDTH - 1 - m
                ds_ = (de if m == 0 else pltpu.roll(de, R + CONV_HALO - m, 0))[:R, :]
                acc = acc + w_ref[k:k + 1, :] * ds_
                dw_ref[k:k + 1, :] += jnp.sum(ds_ * zc, axis=0, keepdims=True)
            dz_ref[pl.ds(start, R), :] = acc
            return 0

        lax.fori_loop(0, S // R, chunk, 0)

    return pl.pallas_call(
        body, grid=(C // tc, B),
        in_specs=[pl.BlockSpec((None, SP, tc), lambda c, b: (b, 0, c)),
                  pl.BlockSpec((None, SP, tc), lambda c, b: (b, 0, c)),
                  pl.BlockSpec((32, tc), lambda c, b: (0, c))],
        out_specs=[pl.BlockSpec((None, S, tc), lambda c, b: (b, 0, c)),
                   pl.BlockSpec((32, tc), lambda c, b: (0, c))],
        out_shape=[jax.ShapeDtypeStruct((B, S, C), F32), jax.ShapeDtypeStruct((32, C), F32)],
        compiler_params=_cp(("parallel", "arbitrary")), name=name,
    )(zp, dyp, w)


def _gmlp_fwd(u, vn, ws, bcol, *, nck, name):
    T, E = u.shape
    H = ws.shape[0]
    he = E // H
    rows = nck * GMLP_CHUNK
    rows = min(rows, T)
    n_in = rows // GMLP_CHUNK

    def body(u_ref, v_ref, ws_ref, b_ref, o_ref):
        for c in range(n_in):
            rs = slice(c * GMLP_CHUNK, (c + 1) * GMLP_CHUNK)
            for hh in range(H):
                cs = slice(hh * he, (hh + 1) * he)
                v2 = jnp.dot(ws_ref[hh], v_ref[rs, cs].astype(BF16), preferred_element_type=F32)
                v2 = v2 + b_ref[:, hh:hh + 1]
                o_ref[rs, cs] = (u_ref[rs, cs] * v2).astype(o_ref.dtype)

    tok = pl.BlockSpec((rows, E), lambda i: (i, 0))
    return pl.pallas_call(
        body, grid=(T // rows,),
        in_specs=[tok, tok, pl.BlockSpec(ws.shape, lambda i: (0, 0, 0)), pl.BlockSpec(bcol.shape, lambda i: (0, 0))],
        out_specs=tok, out_shape=jax.ShapeDtypeStruct((T, E), BF16),
        compiler_params=_cp(("parallel",)), name=name,
    )(u, vn, ws, bcol)


def _gmlp_bwd(duv, u, vn, ws, bcol, *, nck, name):
    T, E = u.shape
    H = ws.shape[0]
    he = E // H
    rows = min(nck * GMLP_CHUNK, T)
    n_in = rows // GMLP_CHUNK
    dims_nt = (((1,), (1,)), ((), ()))
    dims_tn = (((0,), (0,)), ((), ()))

    def body(g_ref, u_ref, v_ref, ws_ref, b_ref, du_ref, dv_ref, dws_ref, db_ref):
        @pl.when(pl.program_id(0) == 0)
        def _():
            dws_ref[...] = jnp.zeros_like(dws_ref)
            db_ref[...] = jnp.zeros_like(db_ref)

        for c in range(n_in):
            rs = slice(c * GMLP_CHUNK, (c + 1) * GMLP_CHUNK)
            for hh in range(H):
                cs = slice(hh * he, (hh + 1) * he)
                vb = v_ref[rs, cs].astype(BF16)
                v2 = jnp.dot(ws_ref[hh], vb, preferred_element_type=F32) + b_ref[:, hh:hh + 1]
                g = g_ref[rs, cs]
                du_ref[rs, cs] = g * v2
                dv2 = g * u_ref[rs, cs]
                dv2b = dv2.astype(BF16)
                dv_ref[rs, cs] = lax.dot_general(ws_ref[hh], dv2b, dims_tn, preferred_element_type=F32)
                dws_ref[hh] += lax.dot_general(dv2b, vb, dims_nt, preferred_element_type=F32)
                db_ref[:, hh:hh + 1] += jnp.sum(dv2, axis=1, keepdims=True)

    tok = pl.BlockSpec((rows, E), lambda i: (i, 0))
    return pl.pallas_call(
        body, grid=(T // rows,),
        in_specs=[tok, tok, tok, pl.BlockSpec(ws.shape, lambda i: (0, 0, 0)), pl.BlockSpec(bcol.shape, lambda i: (0, 0))],
        out_specs=[tok, tok, pl.BlockSpec(ws.shape, lambda i: (0, 0, 0)), pl.BlockSpec(bcol.shape, lambda i: (0, 0))],
        out_shape=[jax.ShapeDtypeStruct((T, E), F32), jax.ShapeDtypeStruct((T, E), F32),
                   jax.ShapeDtypeStruct(ws.shape, F32), jax.ShapeDtypeStruct(bcol.shape, F32)],
        compiler_params=_cp(("arbitrary",)), name=name,
    )(duv, u, vn, ws, bcol)


PAIRS = ATT_HEADS // 2


def _att_consts():
    ji = lax.broadcasted_iota(jnp.int32, (2 * ATT_BLK, ATT_BLK), 0)
    ii = lax.broadcasted_iota(jnp.int32, (2 * ATT_BLK, ATT_BLK), 1)
    dist = ii + ATT_BLK - ji
    band = (dist >= 0) & (dist <= ATT_BLK)
    cur = ji >= ATT_BLK
    first_head = lax.broadcasted_iota(jnp.int32, (ATT_BLK, 2 * HEAD_DIM), 1) < HEAD_DIM
    return band, cur, first_head


def _both_heads(t, first_head):
    zero = jnp.zeros_like(t)
    return jnp.concatenate([jnp.where(first_head, t, zero), jnp.where(first_head, zero, t)], axis=0)


def _att_specs(nbk, offs, nsteps, rev):
    rows = nbk * ATT_BLK
    step = (lambda i: nsteps - 1 - i) if rev else (lambda i: i)
    qoff, koff, voff = offs
    blk = lambda off: pl.BlockSpec((rows, 2 * HEAD_DIM), lambda hp, i: (step(i), off + hp))
    prev = lambda off: pl.BlockSpec((ATT_BLK, 2 * HEAD_DIM), lambda hp, i: (jnp.maximum(step(i) * nbk - 1, 0), off + hp))
    out = pl.BlockSpec((rows, 2 * HEAD_DIM), lambda hp, i: (step(i), hp))
    stat = pl.BlockSpec((2, nbk, ATT_BLK), lambda hp, i: (hp, step(i), 0))
    return [blk(qoff), blk(koff), prev(koff), blk(voff), prev(voff)], out, stat


def _att_fwd(arr, offs, *, nb, nbk, name):
    T = arr.shape[0]
    nbk = min(nbk, T // ATT_BLK)
    nsteps = T // (nbk * ATT_BLK)
    scale = HEAD_DIM ** -0.5
    dims_nt = (((1,), (1,)), ((), ()))
    dims_tn = (((0,), (0,)), ((), ()))

    def body(q_ref, k_ref, kp_ref, v_ref, vp_ref, o_ref, lse_ref):
        i = pl.program_id(1)
        band, cur, first_head = _att_consts()
        for jj in range(nbk):
            rs = slice(jj * ATT_BLK, (jj + 1) * ATT_BLK)
            ps = slice((jj - 1) * ATT_BLK, jj * ATT_BLK)
            has_prev = ((i * nbk + jj) & (nb - 1)) != 0
            valid = band & (cur | has_prev)
            kk = jnp.concatenate([kp_ref[...] if jj == 0 else k_ref[ps, :], k_ref[rs, :]], axis=0)
            vv = jnp.concatenate([vp_ref[...] if jj == 0 else v_ref[ps, :], v_ref[rs, :]], axis=0)
            q2 = _both_heads(q_ref[rs, :], first_head)
            st = lax.dot_general(kk, q2, dims_nt, preferred_element_type=F32) * scale
            st = jnp.where(jnp.concatenate([valid, valid], axis=1), st, MASK_VALUE)
            m = jnp.max(st, axis=0, keepdims=True)
            p = jnp.exp(st - m)
            l = jnp.sum(p, axis=0, keepdims=True)
            lse = m + jnp.log(l)
            lse_ref[0, jj:jj + 1, :] = lse[:, :ATT_BLK]
            lse_ref[1, jj:jj + 1, :] = lse[:, ATT_BLK:]
            pn = (p / l).astype(BF16)
            o2 = lax.dot_general(pn, vv, dims_tn, preferred_element_type=F32)
            o_ref[rs, :] = jnp.where(first_head, o2[:ATT_BLK], o2[ATT_BLK:])

    ins, out, stat = _att_specs(nbk, offs, nsteps, False)
    return pl.pallas_call(
        body, grid=(PAIRS, nsteps), in_specs=ins, out_specs=[out, stat],
        out_shape=[jax.ShapeDtypeStruct((T, ATT_W), F32), jax.ShapeDtypeStruct((ATT_HEADS, T // ATT_BLK, ATT_BLK), F32)],
        compiler_params=_cp(("parallel", "parallel")), name=name,
    )(arr, arr, arr, arr, arr)


def _att_bwd(arr, offs, do, lse, dlse, *, nb, nbk, name):
    T = arr.shape[0]
    nbk = min(nbk, T // ATT_BLK)
    nsteps = T // (nbk * ATT_BLK)
    scale = HEAD_DIM ** -0.5
    dims_nt = (((1,), (1,)), ((), ()))
    dims_tn = (((0,), (0,)), ((), ()))

    def body(q_ref, k_ref, kp_ref, v_ref, vp_ref, do_ref, lse_ref, dlse_ref, dq_ref, dk_ref, dv_ref, ck, cv):
        step = pl.program_id(1)
        i = nsteps - 1 - step
        band, cur, first_head = _att_consts()

        @pl.when(step == 0)
        def _():
            ck[...] = jnp.zeros_like(ck)
            cv[...] = jnp.zeros_like(cv)

        carry_k, carry_v = ck[...], cv[...]
        for jj in reversed(range(nbk)):
            rs = slice(jj * ATT_BLK, (jj + 1) * ATT_BLK)
            ps = slice((jj - 1) * ATT_BLK, jj * ATT_BLK)
            has_prev = ((i * nbk + jj) & (nb - 1)) != 0
            valid = band & (cur | has_prev)
            kk = jnp.concatenate([kp_ref[...] if jj == 0 else k_ref[ps, :], k_ref[rs, :]], axis=0)
            vv = jnp.concatenate([vp_ref[...] if jj == 0 else v_ref[ps, :], v_ref[rs, :]], axis=0)
            q2 = _both_heads(q_ref[rs, :], first_head)
            do2 = _both_heads(do_ref[rs, :].astype(BF16), first_head)
            lse = jnp.concatenate([lse_ref[0, jj:jj + 1, :], lse_ref[1, jj:jj + 1, :]], axis=1)
            dlse = jnp.concatenate([dlse_ref[0, jj:jj + 1, :], dlse_ref[1, jj:jj + 1, :]], axis=1)
            st = lax.dot_general(kk, q2, dims_nt, preferred_element_type=F32) * scale
            st = jnp.where(jnp.concatenate([valid, valid], axis=1), st, MASK_VALUE)
            p = jnp.exp(st - lse)
            dp = lax.dot_general(vv, do2, dims_nt, preferred_element_type=F32)
            delta = jnp.sum(p * dp, axis=0, keepdims=True)
            dsb = (p * (dp - delta + dlse) * scale).astype(BF16)
            dq2 = lax.dot_general(dsb, kk, dims_tn, preferred_element_type=F32)
            dkk = jnp.dot(dsb, q2, preferred_element_type=F32)
            dvv = jnp.dot(p.astype(BF16), do2, preferred_element_type=F32)
            dq_ref[rs, :] = jnp.where(first_head, dq2[:ATT_BLK], dq2[ATT_BLK:]).astype(dq_ref.dtype)
            dk_ref[rs, :] = (dkk[ATT_BLK:] + carry_k).astype(dk_ref.dtype)
            dv_ref[rs, :] = (dvv[ATT_BLK:] + carry_v).astype(dv_ref.dtype)
            carry_k, carry_v = dkk[:ATT_BLK], dvv[:ATT_BLK]
        ck[...] = carry_k
        cv[...] = carry_v

    ins, out, stat = _att_specs(nbk, offs, nsteps, True)
    return pl.pallas_call(
        body, grid=(PAIRS, nsteps), in_specs=ins + [out, stat, stat], out_specs=[out] * 3,
        out_shape=[jax.ShapeDtypeStruct((T, ATT_W), BF16)] * 3,
        scratch_shapes=[pltpu.VMEM((ATT_BLK, 2 * HEAD_DIM), F32), pltpu.VMEM((ATT_BLK, 2 * HEAD_DIM), F32)],
        compiler_params=_cp(("arbitrary", "arbitrary")), name=name,
    )(arr, arr, arr, arr, arr, do, lse, dlse)


def _deinterleave(t, B, S, dil):
    if dil == 1:
        return t
    return t.reshape((B, S // dil, dil) + t.shape[1:]).swapaxes(1, 2).reshape(t.shape)


def _interleave(t, B, S, dil):
    if dil == 1:
        return t
    return t.reshape((B, dil, S // dil) + t.shape[1:]).swapaxes(1, 2).reshape(t.shape)


def _stats_to_tokens(lse, B, S, dil):
    return _interleave(lse.reshape(lse.shape[0], -1).T, B, S, dil)


def _stats_from_tokens(dl, B, S, dil):
    return _deinterleave(dl, B, S, dil).T.reshape(dl.shape[1], -1, ATT_BLK)


def _mesh_pos():
    return lax.axis_index("x"), lax.axis_index("y"), lax.axis_index("c")


def _allgather8(xs, *, name):
    m_per, n = xs.shape

    def body(x_ref, out_ref, send_sems, recv_sems, local_sem):
        x, y, c = _mesh_pos()
        me, sibling = (x, y, c), (x, y, 1 - c)
        chips = [(1 - x, y), (x, 1 - y), (1 - x, 1 - y)]

        def rows(px, py, pc):
            return out_ref.at[pl.ds((4 * px + 2 * py + pc) * m_per, m_per), :]

        def copy(k, block, to, src=None):
            return pltpu.make_async_remote_copy(
                src_ref=rows(*block) if src is None else src, dst_ref=rows(*block),
                send_sem=send_sems.at[k], recv_sem=recv_sems.at[k], device_id=to, device_id_type=MESH)

        mine = pltpu.make_async_copy(x_ref, rows(*me), local_sem)
        mine.start()
        first = [copy(0, me, sibling, src=x_ref)]
        first += [copy(1 + j, me, (*chip, c), src=x_ref) for j, chip in enumerate(chips)]
        for cp in first:
            cp.start()
        passed = [copy(4 + j, (*chip, c), sibling) for j, chip in enumerate(chips)]
        for j, chip in enumerate(chips):
            copy(1 + j, (*chip, c), me).wait_recv()
            passed[j].start()
        copy(0, sibling, me).wait_recv()
        for j, chip in enumerate(chips):
            copy(4 + j, (*chip, 1 - c), me).wait_recv()
        for cp in first + passed:
            cp.wait_send()
        mine.wait()

    return pl.pallas_call(
        body, out_shape=jax.ShapeDtypeStruct((8 * m_per, n), xs.dtype),
        in_specs=[pl.BlockSpec(memory_space=pltpu.VMEM)], out_specs=pl.BlockSpec(memory_space=pltpu.VMEM),
        scratch_shapes=[pltpu.SemaphoreType.DMA((7,)), pltpu.SemaphoreType.DMA((7,)), pltpu.SemaphoreType.DMA],
        compiler_params=pltpu.CompilerParams(vmem_limit_bytes=VMEM_LIMIT), name=name,
    )(xs)


def _hbm_call(body, arrays, out_shapes, n_sems, *, name):
    any_spec = pl.BlockSpec(memory_space=pl.ANY)
    return pl.pallas_call(
        body, out_shape=out_shapes, in_specs=[any_spec] * len(arrays), out_specs=[any_spec] * len(out_shapes),
        scratch_shapes=[pltpu.SemaphoreType.DMA((n_sems,)), pltpu.SemaphoreType.DMA((n_sems,))], name=name,
    )(*arrays)


def _other_chips(x, y):
    return [(1 - x, y), (x, 1 - y), (1 - x, 1 - y)]


def _split_start(srcs, lands, after, issue, n_sems, *, name):
    ns, nl = len(srcs), len(lands)
    hbm, sem = pl.BlockSpec(memory_space=pltpu.HBM), pl.BlockSpec(memory_space=pltpu.SEMAPHORE)
    extra = [] if after is None else [after]

    def body(*refs):
        n_in = ns + nl + len(extra)
        send_sems, recv_sems = refs[n_in], refs[n_in + 1]
        issue(refs[:ns], refs[ns:ns + nl], send_sems, recv_sems)
        refs[-1][...] = jnp.zeros_like(refs[-1])

    arrays = [pltpu.with_memory_space_constraint(a, pltpu.HBM) for a in list(srcs) + list(lands)]
    out = pl.pallas_call(
        body, name=name,
        out_shape=(pltpu.SemaphoreType.DMA((n_sems,)), pltpu.SemaphoreType.DMA((n_sems,)),
                   *[pltpu.HBM(a.shape, a.dtype) for a in arrays], jax.ShapeDtypeStruct((8, 128), F32)),
        in_specs=[hbm] * (ns + nl) + [pl.BlockSpec(memory_space=pl.ANY)] * len(extra),
        out_specs=(sem, sem, *[hbm] * (ns + nl), pl.BlockSpec(memory_space=pltpu.VMEM)),
        input_output_aliases={i: 2 + i for i in range(ns + nl)},
        compiler_params=pltpu.CompilerParams(has_side_effects=pltpu.SideEffectType.DATAFLOW_SIDE_EFFECTING),
    )(*arrays, *extra)
    return out[0], out[1], list(out[2:2 + ns]), list(out[2 + ns:2 + ns + nl]), out[-1]


def _split_wait(send_sems, recv_sems, srcs, lands, after, waits, *, name):
    ns, nl = len(srcs), len(lands)
    hbm, sem = pl.BlockSpec(memory_space=pltpu.HBM), pl.BlockSpec(memory_space=pltpu.SEMAPHORE)

    def body(*refs):
        waits(refs[:ns], refs[ns:ns + nl], refs[ns + nl], refs[ns + nl + 1])

    out = pl.pallas_call(
        body, name=name,
        out_shape=tuple(pltpu.HBM(a.shape, a.dtype) for a in list(srcs) + list(lands)),
        in_specs=[hbm] * (ns + nl) + [sem, sem, pl.BlockSpec(memory_space=pl.ANY)],
        out_specs=tuple([hbm] * (ns + nl)),
        input_output_aliases={i: i for i in range(ns + nl)},
        compiler_params=pltpu.CompilerParams(has_side_effects=pltpu.SideEffectType.DATAFLOW_SIDE_EFFECTING),
    )(*srcs, *lands, send_sems, recv_sems, after)
    return list(out[:ns]), list(out[ns:])


def _gather_start(halves, after, *, name):
    n = len(halves)
    lands = [lax.empty((N_CHIPS,) + h.shape, h.dtype) for h in halves]

    def issue(srcs, dsts, send_sems, recv_sems):
        x, y, c = _mesh_pos()
        me = 2 * x + y
        for a in range(n):
            for j, (px, py) in enumerate(_other_chips(x, y)):
                for cc in range(2):
                    pltpu.make_async_remote_copy(
                        src_ref=srcs[a].at[c], dst_ref=dsts[a].at[me, c],
                        send_sem=send_sems.at[6 * a + 2 * j + cc], recv_sem=recv_sems.at[6 * a + 2 * j + c],
                        device_id=(px, py, cc), device_id_type=MESH).start()

    return _split_start(halves, lands, after, issue, 6 * n, name=name)


def _gather_wait(started, after, *, name):
    send_sems, recv_sems, halves, lands = started
    n = len(halves)

    def waits(srcs, dsts, send_sems, recv_sems):
        x, y, c = _mesh_pos()
        me = 2 * x + y
        for a in range(n):
            for j, (px, py) in enumerate(_other_chips(x, y)):
                for cc in range(2):
                    pltpu.make_async_remote_copy(
                        src_ref=srcs[a].at[cc], dst_ref=dsts[a].at[2 * px + py, cc],
                        send_sem=send_sems.at[6 * a + 2 * j + cc], recv_sem=recv_sems.at[6 * a + 2 * j + cc],
                        device_id=(px, py, cc), device_id_type=MESH).wait_recv()
        for a in range(n):
            for j, (px, py) in enumerate(_other_chips(x, y)):
                for cc in range(2):
                    pltpu.make_async_remote_copy(
                        src_ref=srcs[a].at[c], dst_ref=dsts[a].at[me, c],
                        send_sem=send_sems.at[6 * a + 2 * j + cc], recv_sem=recv_sems.at[6 * a + 2 * j + c],
                        device_id=(px, py, cc), device_id_type=MESH).wait_send()

    return _split_wait(send_sems, recv_sems, halves, lands, after, waits, name=name)


def _reduce_plan_loops(plans, chip, c, fn):
    for a, plan in enumerate(plans):
        for h, cc in plan:
            for k in range(N_CHIPS):
                fn(a, h, k, cc, jnp.logical_or(chip != k, c != cc))


def _reduce_start(srcs, lands, plans, after, *, name):
    def issue(src_refs, land_refs, send_sems, recv_sems):
        x, y, c = _mesh_pos()
        chip = 2 * x + y
        my_id = 2 * chip + c

        def send(a, h, k, cc, is_other):
            @pl.when(is_other)
            def _():
                pltpu.make_async_remote_copy(
                    src_ref=src_refs[a].at[h, k], dst_ref=land_refs[a].at[my_id],
                    send_sem=send_sems.at[8 * a + 2 * k + cc], recv_sem=recv_sems.at[8 * a + my_id],
                    device_id=(k // 2, k % 2, cc), device_id_type=MESH).start()

        _reduce_plan_loops(plans, chip, c, send)

    return _split_start(srcs, lands, after, issue, 8 * len(srcs), name=name)


def _reduce_wait(started, plans, after, *, name):
    send_sems, recv_sems, srcs, lands = started

    def waits(src_refs, land_refs, send_sems, recv_sems):
        x, y, c = _mesh_pos()
        chip = 2 * x + y
        my_id = 2 * chip + c
        for a, plan in enumerate(plans):
            for h, cc in plan:
                for s in range(2 * N_CHIPS):
                    @pl.when(jnp.logical_and(c == cc, my_id != s))
                    def _(a=a, h=h, s=s):
                        pltpu.make_async_remote_copy(
                            src_ref=src_refs[a].at[h, 0], dst_ref=land_refs[a].at[s],
                            send_sem=send_sems.at[8 * a + s], recv_sem=recv_sems.at[8 * a + s],
                            device_id=(s // 4, (s // 2) % 2, s % 2), device_id_type=MESH).wait_recv()

        def sent(a, h, k, cc, is_other):
            @pl.when(is_other)
            def _():
                pltpu.make_async_remote_copy(
                    src_ref=src_refs[a].at[h, k], dst_ref=land_refs[a].at[my_id],
                    send_sem=send_sems.at[8 * a + 2 * k + cc], recv_sem=recv_sems.at[8 * a + my_id],
                    device_id=(k // 2, k % 2, cc), device_id_type=MESH).wait_send()

        _reduce_plan_loops(plans, chip, c, sent)

    return _split_wait(send_sems, recv_sems, srcs, lands, after, waits, name=name)


def _sum8(land, own, my_id, *, name):
    n_src, R, C = land.shape
    tr = _pick_rows(R, max(8, 1024 * 1024 // (2 * C)))

    def body(id_ref, *refs):
        own_ref, o_ref = refs[n_src], refs[n_src + 1]
        me = id_ref[0]
        acc = None
        for s in range(n_src):
            term = jnp.where(me == s, own_ref[...], refs[s][...]).astype(F32)
            acc = term if acc is None else acc + term
        o_ref[...] = acc

    return pl.pallas_call(
        body, out_shape=jax.ShapeDtypeStruct((R, C), F32),
        grid_spec=pltpu.PrefetchScalarGridSpec(
            num_scalar_prefetch=1, grid=(R // tr,),
            in_specs=[pl.BlockSpec((None, tr, C), lambda i, idr, s=s: (s, i, 0)) for s in range(n_src)]
            + [pl.BlockSpec((tr, C), lambda i, idr: (i, 0))],
            out_specs=pl.BlockSpec((tr, C), lambda i, idr: (i, 0))),
        compiler_params=_cp(("parallel",)), name=name,
    )(my_id.reshape(1).astype(jnp.int32), *([land] * n_src), own)


def _share_halves(ts, *, name):
    n = len(ts)

    def body(*refs):
        ins, outs, (send_sems, recv_sems) = refs[:n], refs[n:2 * n], refs[2 * n:]
        x, y, c = _mesh_pos()
        cps = [pltpu.make_async_remote_copy(
            src_ref=ins[a], dst_ref=outs[a].at[c], send_sem=send_sems.at[a], recv_sem=recv_sems.at[a],
            device_id=(x, y, 1 - c), device_id_type=MESH) for a in range(n)]
        for cp in cps:
            cp.start()
        for a in range(n):
            pltpu.make_async_remote_copy(
                src_ref=ins[a], dst_ref=outs[a].at[1 - c], send_sem=send_sems.at[a], recv_sem=recv_sems.at[a],
                device_id=(x, y, 1 - c), device_id_type=MESH).wait_recv()
        for cp in cps:
            cp.wait_send()

    return _hbm_call(body, ts, [jax.ShapeDtypeStruct((2,) + t.shape, t.dtype) for t in ts], n, name=name)


TR = 512
ATT_BLOCKS_PER_STEP = 16
S5_CHUNK = 256


def _rms_fwd(x, g, name):
    return _rowwise(_f_rms, [x], [g], (BF16,), tr=TR, name=name)[0]


def _rms_bwd_epi(dh, x, g, gx):
    r = lax.rsqrt(jnp.mean(x * x, axis=-1, keepdims=True) + EPS)
    xr = x * r
    t = dh * g
    dx = r * (t - xr * jnp.mean(t * xr, axis=-1, keepdims=True)) + gx
    return dx, dx, jnp.sum(dh * xr, axis=0, keepdims=True)


def _mm_rms_bwd(a, w, x, g, gx, *, after=None, name, **kw):
    kw.setdefault("tm", 1024)
    return _mm(a, w, tb=True, epi=_rms_bwd_epi, extras=(x, g, gx), out_dtypes=(F32, BF16, F32), n_row_sums=1,
               tn=x.shape[1], after=after, name=name, **kw)


def _rms_bwd(x, g, dh, gx, name, after=None):
    (dx, dxb), (dg,) = _rowwise_vjp(_f_rms, [x], [g], [dh], [(F32, BF16)], adds={0: gx}, after=after, tr=TR,
                                    name=name)
    return dx, dxb, dg


def _grad_cols(M, Nq):
    def imap(tm, tn):
        hp, per = (M // 2) // tm, Nq // tn
        assert hp * tm * 2 == M and per * tn == Nq, (M, Nq, tm, tn)
        return lambda i, j, k: (i // hp, j // per, i % hp, j % per)
    return (2, N_CHIPS, M // 2, Nq), lambda tm, tn: (None, None, tm, tn), imap, None, M // 2, Nq


def _grad_rows(Mq, N):
    def imap(tm, tn):
        po, hp = Mq // tm, (Mq // 2) // tm
        assert hp * tm * 2 == Mq, (Mq, tm)
        return lambda i, j, k: ((i % po) // hp, i // po, (i % po) % hp, j)
    return (2, N_CHIPS, Mq // 2, N), lambda tm, tn: (None, None, tm, tn), imap, None, Mq // 2, N


def _grad_layer_cols(slot, lh, M, Nq, buf):
    def imap(tm, tn):
        per = Nq // tn
        return lambda i, j, k: (j // per, slot, i, j % per)
    return (N_CHIPS, lh, M, Nq), lambda tm, tn: (None, None, tm, tn), imap, buf, M, Nq


def _grad_layer_rows(slot, lh, Mq, N, buf):
    def imap(tm, tn):
        po = Mq // tm
        return lambda i, j, k: (i // po, slot, i % po, j)
    return (N_CHIPS, lh, Mq, N), lambda tm, tn: (None, None, tm, tn), imap, buf, Mq, N


def _add_then_rms(acc, res, g):
    xo = acc + res
    return xo, _f_rms(xo, g)[0]


def _mlp_fwd(x, h2, w_in, w_out, g_next, li):
    w_in_full = getattr(w_in, "plain", w_in)
    r = _mm(h2, w_in_full, out_dtypes=(BF16,), epi=lambda acc: (jnp.maximum(acc, 0.0),), tm=512, tn=w_in.shape[1],
            name=f"mlp_in_{li}")
    tiles = dict(tm=512, tn=x.shape[1], tk=r.shape[1])
    if g_next is None:
        x_out, h_next = _mm(r, w_out, pro_a=lambda t: t * t, epi=lambda acc, res: (acc + res,), extras=(x,),
                            name=f"mlp_out_{li}", **tiles), None
    else:
        x_out, h_next = _mm(r, w_out, pro_a=lambda t: t * t, epi=_add_then_rms, extras=(x, g_next),
                            out_dtypes=(F32, BF16), name=f"mlp_out_{li}", **tiles)
    return x_out, h_next, (h2, r)


def _mlp_bwd(gx, gxb, x, g, w_in, w_out, saved, li, nl, bufs):
    h2, r = saved
    D, F = w_in.shape
    lh = nl // 2
    da = _mm(gxb, w_out, tb=True, out_dtypes=(BF16,),
             epi=lambda acc, rt: (acc * 2.0 * rt.astype(F32),), extras=(r,), tm=512, tn=F, name=f"mlp_dact_{li}")
    buf_in, buf_out = bufs if bufs is not None else (None, None)
    d_w_out = _mm(r, gxb, ta=True, pro_a=lambda t: t * t, tm=1024, tn=1024, tk=4096, out_dtypes=(BF16,),
                  out=_grad_layer_rows(li % lh, lh, F // N_CHIPS, D, buf_out), name=f"mlp_dwout_{li}")
    d_w_in = _mm(h2, da, ta=True, tm=1024, tn=1024, tk=4096, out_dtypes=(BF16,),
                 out=_grad_layer_cols(li % lh, lh, D, F // N_CHIPS, buf_in), name=f"mlp_dwin_{li}")
    gx_mid, gxb_mid, dg = _mm_rms_bwd(da, getattr(w_in, "plain", w_in), x, g, gx, tm=512, tk=F, name=f"mlp_dh_{li}")
    return gx_mid, gxb_mid, dg, (d_w_in, d_w_out)


def _local_step(x3, tgt3, p, layer_weights, token=None, grads_done=lambda group: None):
    B, S, D = x3.shape
    T = B * S
    x = x3.reshape(T, D)
    grads = {}
    row = lambda v: v.reshape(1, -1)
    p = dict(p)
    nl = p["norm_mlp"].shape[0]
    mlp_in, mlp_out = [None] * nl, [None] * nl

    def fetch(li, after):
        wl = dict(layer_weights(li, after))
        mlp_in[li], mlp_out[li] = wl.pop("mlp_w_in"), wl.pop("mlp_w_out")
        p.update(wl)

    g0 = row(p["norm_mix"][0])
    if token is not None:
        g0 = g0 + token[:1, :1]
    h0 = _rms_fwd(x, g0, "rms_mix_0")
    s5_args = (p["ssm_a_re"][0], p["ssm_a_im"][0], p["ssm_b_re"][0], p["ssm_b_im"][0],
               p["ssm_c_re"][0], p["ssm_c_im"][0], p["ssm_log_dt"][0])
    s5_exp, s5_vjp = jax.vjp(_s5_prep, *s5_args)
    abr, abi, bre, bim, cre, cim = s5_exp
    bre_b, bim_b, cre_b, cim_b = (t.astype(BF16) for t in (bre, bim, cre, cim))
    d_skip = p["ssm_d"]
    ypre, yb, sxr, sxi, ser, sei = _s5_fwd(h0, abr, abi, bre_b, bim_b, cre_b, cim_b, d_skip, B=B, L=S5_CHUNK,
                                           name="s5_fwd")
    fetch(0, yb)
    w_glu = p["ssm_w_glu"]
    z0 = _mm(yb, w_glu, tm=2048, name="s5_glu_mm")
    gm = [row(p["norm_mlp"][i]) for i in range(nl)]
    g1, g2, g3 = (row(p["norm_mix"][i]) for i in range(1, nl))
    x_mid0, hm0 = _rowwise(lambda z, xr, g: _add_then_rms(_f_glu(z)[0], xr, g), [z0, x], [gm[0]], (F32, BF16),
                           tr=TR, name="s5_glu")
    x1, h1, mlp_saved0 = _mlp_fwd(x_mid0, hm0, mlp_in[0], mlp_out[0], g1, 0)

    fetch(1, h1)
    z1 = _mm(h1, p["conv_w_pw1"], tm=2048, name="conv_pw1")
    zg = _rowwise(_f_bias_glu, [z1], [p["conv_b_pw1"]], (F32,), tr=TR, name="conv_glu")[0]
    zp = jnp.pad(zg.reshape(B, S, D), ((0, 0), (CONV_HALO, 0), (0, 0)))
    w_dw = jnp.pad(p["conv_w_dw"], ((0, 32 - CONV_WIDTH), (0, 0)))
    yc = _conv_fwd(zp, w_dw, R=256, tc=128, name="conv_dw").reshape(T, D)
    ln_par = [p["conv_b_dw"], p["conv_ln_g"], p["conv_ln_b"]]
    qc = _rowwise(_f_ln_silu, [yc], ln_par, (BF16,), tr=TR, name="conv_ln_silu")[0]
    x_mid1, hm1 = _mm(qc, p["conv_w_pw2"], epi=lambda acc, bias, res, g: _add_then_rms(acc + bias, res, g),
                      extras=(p["conv_b_pw2"], x1, gm[1]), out_dtypes=(F32, BF16), tn=D, name="conv_pw2")
    x2, h2, mlp_saved1 = _mlp_fwd(x_mid1, hm1, mlp_in[1], mlp_out[1], g2, 1)

    fetch(2, h2)
    z2 = _mm(h2, p["gmlp_w_in"], tm=2048, name="gmlp_in")
    gl_par = [p["gmlp_ln_g"], p["gmlp_ln_b"]]
    gu, gvn = _rowwise(_f_gelu_ln, [z2], gl_par, (F32, F32), tr=TR, name="gmlp_gelu_ln")
    causal = jnp.tril(jnp.ones((GMLP_CHUNK, GMLP_CHUNK), dtype=bool))
    ws_b = jnp.where(causal[None], p["gmlp_w_s"][0], 0.0).astype(BF16)
    bcol = jnp.pad(p["gmlp_b_s"][0].T, ((0, 0), (0, 128 - GMLP_HEADS)))
    uv = _gmlp_fwd(gu, gvn, ws_b, bcol, nck=4, name="gmlp_spatial")
    x_mid2, hm2 = _mm(uv, p["gmlp_w_out"], epi=_add_then_rms, extras=(x2, gm[2]), out_dtypes=(F32, BF16), tn=D,
                      name="gmlp_out")
    x3_, h3, mlp_saved2 = _mlp_fwd(x_mid2, hm2, mlp_in[2], mlp_out[2], g3, 2)

    fetch(3, h3)
    ng = len(ATT_DILS)
    att_in, o_tok, l_tok, lses = [], [], [], []
    offs = (0, PAIRS, 2 * PAIRS)
    for gi, dil in enumerate(ATT_DILS):
        w_g = _ColBlocks(p["attn_w_qkv_plain"], gi, ng, 3, ATT_W)
        arr = _mm(h3, w_g, out_dtypes=(BF16,), tm=2048, tn=ATT_W, name=f"attn_qkv_{gi}")
        arr = _deinterleave(arr, B, S, dil)
        att_in.append((arr, offs))
        og, lg = _att_fwd(arr, offs, nb=S // dil // ATT_BLK, nbk=ATT_BLOCKS_PER_STEP, name=f"attn_fwd_{gi}")
        lses.append(lg)
        o_tok.append(_interleave(og, B, S, dil))
        l_tok.append(_stats_to_tokens(lg, B, S, dil))
    merged2 = _rowwise(_f_merge, o_tok + l_tok, [], (BF16,), tr=TR, name="attn_merge")[0]
    x_mid3, hm3 = _mm(merged2, p["attn_w_o_plain"], epi=_add_then_rms, extras=(x3_, gm[3]), out_dtypes=(F32, BF16),
                      tn=D, name="attn_out")
    x4, _, mlp_saved3 = _mlp_fwd(x_mid3, hm3, mlp_in[3], mlp_out[3], None, 3)

    loss_part, gx, gxb, dgf = _loss_head(x4, tgt3.reshape(T, D), row(p["norm_final"]), tr=TR, name="loss_head")
    grads["norm_final"] = dgf.reshape(-1)
    d_norm_mix, d_norm_mlp = [None] * 4, [None] * 4
    Dq = D // N_CHIPS

    gx, gxb, d_norm_mlp[3], mlp_hi = _mlp_bwd(
        gx, gxb, x_mid3, row(p["norm_mlp"][3]), mlp_in[3], mlp_out[3], mlp_saved3, 3, nl, None)
    dmerged = _mm(gxb, p["attn_w_o"], tb=True, name="attn_dmerged")
    grads["attn_w_o"] = _mm(merged2, gxb, ta=True, tm=256, tn=256, tk=4096, out_dtypes=(BF16,), out=_grad_cols(ATT_W, Dq), name="attn_dwo")
    dml, _ = _rowwise_vjp(_f_merge, o_tok + l_tok, [], [dmerged], [F32] * 6, tr=TR, name="attn_merge_bwd")
    pieces = [[None] * ng for _ in range(3)]
    for gi, dil in enumerate(ATT_DILS):
        arr, offs = att_in[gi]
        dqkv_g = _att_bwd(arr, offs, _deinterleave(dml[gi], B, S, dil), lses[gi],
                          _stats_from_tokens(dml[ng + gi], B, S, dil), nb=S // dil // ATT_BLK,
                          nbk=ATT_BLOCKS_PER_STEP,
                          name=f"attn_bwd_{gi}")
        for i in range(3):
            pieces[i][gi] = _interleave(dqkv_g[i], B, S, dil)
    dqkv = jnp.concatenate([pieces[i][gi] for i in range(3) for gi in range(ng)], axis=1)
    qkv_w = 3 * ng * ATT_W
    grads["attn_w_qkv"] = _mm(h3, dqkv, ta=True, tm=512, tn=1152, tk=4096, out_dtypes=(BF16,), out=_grad_cols(D, qkv_w // N_CHIPS),
                              name="attn_dwqkv")
    tok = grads_done({n: grads[n] for n in ("attn_w_qkv", "attn_w_o")})
    gx, gxb, d_norm_mix[3] = _mm_rms_bwd(dqkv, p["attn_w_qkv_plain"], x3_, g3, gx, tm=512, tk=qkv_w, after=tok, name="attn_dh")

    gx, gxb, d_norm_mlp[2], mlp_hi = _mlp_bwd(
        gx, gxb, x_mid2, row(p["norm_mlp"][2]), mlp_in[2], mlp_out[2], mlp_saved2, 2, nl, mlp_hi)
    tok = grads_done({"mlp_w_in": (1, mlp_hi[0]), "mlp_w_out": (1, mlp_hi[1])})
    duv = _mm(gxb, p["gmlp_w_out"], tb=True, after=tok, name="gmlp_duv")
    grads["gmlp_w_out"] = _mm(uv, gxb, ta=True, tm=128, tn=1024, tk=4096, out_dtypes=(BF16,), out=_grad_rows(Dq, D), name="gmlp_dwout")
    du, dvn, dws, dbcol = _gmlp_bwd(duv, gu, gvn, ws_b, bcol, nck=4, name="gmlp_spatial_bwd")
    grads["gmlp_w_s"] = jnp.where(causal[None], dws, 0.0)[None]
    grads["gmlp_b_s"] = dbcol[:, :GMLP_HEADS].T[None]
    (dz2,), (dlg, dlb_) = _rowwise_vjp(_f_gelu_ln, [z2], gl_par, [du, dvn], [BF16], tr=TR, name="gmlp_gelu_ln_bwd")
    grads["gmlp_ln_g"], grads["gmlp_ln_b"] = dlg, dlb_
    grads["gmlp_w_in"] = _mm(h2, dz2, ta=True, tm=512, tn=512, tk=4096, out_dtypes=(BF16,), out=_grad_cols(D, 2 * Dq), name="gmlp_dwin")
    tok = grads_done({n: grads[n] for n in ("gmlp_w_in", "gmlp_w_out")})
    gx, gxb, d_norm_mix[2] = _mm_rms_bwd(dz2, getattr(p["gmlp_w_in"], "plain", p["gmlp_w_in"]), x2, g2, gx, tm=512, tk=2 * D, after=tok,
                                         name="gmlp_dh")

    gx, gxb, d_norm_mlp[1], mlp_lo = _mlp_bwd(
        gx, gxb, x_mid1, row(p["norm_mlp"][1]), mlp_in[1], mlp_out[1], mlp_saved1, 1, nl, None)
    dqc = _mm(gxb, p["conv_w_pw2"], tb=True, name="conv_dq")
    grads["conv_w_pw2"] = _mm(qc, gxb, ta=True, tm=128, tn=1024, tk=4096, out_dtypes=(BF16,), out=_grad_rows(Dq, D), name="conv_dwpw2")
    _, (db2,) = _rowwise_vjp(lambda t, b: (t + b,), [gx], [p["conv_b_pw2"]], [gx], [None], tr=TR, name="conv_db2")
    grads["conv_b_pw2"] = db2
    (dyc,), (dbdw, dcg, dcb) = _rowwise_vjp(_f_ln_silu, [yc], ln_par, [dqc], [F32], tr=TR, name="conv_ln_silu_bwd")
    grads["conv_b_dw"], grads["conv_ln_g"], grads["conv_ln_b"] = dbdw, dcg, dcb
    dyp = jnp.pad(dyc.reshape(B, S, D), ((0, 0), (0, CONV_HALO), (0, 0)))
    dzg, dwdw = _conv_bwd(zp, dyp, w_dw, R=256, tc=128, name="conv_dw_bwd")
    grads["conv_w_dw"] = dwdw[:CONV_WIDTH][None]
    (dz1,), (db1,) = _rowwise_vjp(_f_bias_glu, [z1], [p["conv_b_pw1"]], [dzg.reshape(T, D)], [BF16], tr=TR,
                                  name="conv_glu_bwd")
    grads["conv_b_pw1"] = db1
    grads["conv_w_pw1"] = _mm(h1, dz1, ta=True, tm=512, tn=512, tk=4096, out_dtypes=(BF16,), out=_grad_cols(D, 2 * Dq), name="conv_dwpw1")
    tok = grads_done({n: grads[n] for n in ("conv_w_pw1", "conv_w_pw2")})
    gx, gxb, d_norm_mix[1] = _mm_rms_bwd(dz1, getattr(p["conv_w_pw1"], "plain", p["conv_w_pw1"]), x1, g1, gx, tm=512, tk=2 * D,
                                         after=tok, name="conv_dh")

    gx, gxb, d_norm_mlp[0], mlp_lo = _mlp_bwd(
        gx, gxb, x_mid0, row(p["norm_mlp"][0]), mlp_in[0], mlp_out[0], mlp_saved0, 0, nl, mlp_lo)
    tok = grads_done({"mlp_w_in": (0, mlp_lo[0]), "mlp_w_out": (0, mlp_lo[1])})
    (dz0,), _ = _rowwise_vjp(_f_glu, [z0], [], [gx], [BF16], after=tok, tr=TR, name="s5_glu_bwd")
    grads["ssm_w_glu"] = _mm(yb, dz0, ta=True, tm=512, tn=512, tk=4096, out_dtypes=(BF16,), out=_grad_cols(D, 2 * Dq), name="s5_dwglu")
    tok = grads_done({"ssm_w_glu": grads["ssm_w_glu"]})
    dypre = _mm(dz0, getattr(w_glu, "plain", w_glu), tb=True, tm=512, tk=2 * D,
                epi=lambda acc, yp: (jax.vjp(lambda t: jax.nn.gelu(t), yp)[1](acc)[0],), extras=(ypre,), name="s5_dypre")
    dh0, dbre, dbim, dcre, dcim, dabr, dabi, dd = _s5_bwd(
        dypre, h0, sxr, sxi, ser, sei, abr, abi, bre_b, bim_b, cre_b, cim_b, d_skip, B=B, L=S5_CHUNK, name="s5_bwd")
    s5_grads = s5_vjp((dabr, dabi, dbre, dbim, dcre, dcim))
    for nm, gv in zip(("ssm_a_re", "ssm_a_im", "ssm_b_re", "ssm_b_im", "ssm_c_re", "ssm_c_im", "ssm_log_dt"), s5_grads):
        grads[nm] = gv[None]
    grads["ssm_d"] = dd
    gx, _, d_norm_mix[0] = _rms_bwd(x, g0, dh0, gx, "rms_mix_bwd_0", after=tok)

    grads["norm_mix"] = jnp.concatenate(d_norm_mix, axis=0)
    grads["norm_mlp"] = jnp.concatenate(d_norm_mlp, axis=0)
    grads["mlp_w_in"], grads["mlp_w_out"] = (mlp_lo[0], mlp_hi[0]), (mlp_lo[1], mlp_hi[1])
    return loss_part, gx.reshape(B, S, D), grads


WEIGHTS = ['norm_mix', 'norm_mlp', 'norm_final', 'ssm_a_re', 'ssm_a_im', 'ssm_b_re', 'ssm_b_im', 'ssm_c_re',
           'ssm_c_im', 'ssm_d', 'ssm_log_dt', 'ssm_w_glu', 'conv_w_pw1', 'conv_b_pw1', 'conv_w_dw', 'conv_b_dw',
           'conv_ln_g', 'conv_ln_b', 'conv_w_pw2', 'conv_b_pw2', 'gmlp_w_in', 'gmlp_ln_g', 'gmlp_ln_b', 'gmlp_w_s',
           'gmlp_b_s', 'gmlp_w_out', 'attn_w_qkv', 'attn_w_o', 'mlp_w_in', 'mlp_w_out']
BIG_AXIS = {'ssm_w_glu': -1, 'conv_w_pw1': -1, 'conv_w_pw2': -2, 'gmlp_w_in': -1, 'gmlp_w_out': -2,
            'attn_w_qkv': -1, 'attn_w_o': -1, 'mlp_w_in': -1, 'mlp_w_out': -2}
BIG = list(BIG_AXIS)
LAYER_MIXER_WEIGHTS = (('ssm_w_glu',), ('conv_w_pw1', 'conv_w_pw2'), ('gmlp_w_in', 'gmlp_w_out'), ('attn_w_qkv', 'attn_w_o'))
SMALL_SHARDED = ['conv_b_pw1', 'conv_w_dw', 'conv_b_dw', 'conv_ln_g', 'conv_ln_b', 'conv_b_pw2', 'gmlp_ln_g', 'gmlp_ln_b']
SMALL_REPL = [n for n in WEIGHTS if n not in BIG_AXIS and n not in SMALL_SHARDED]
SMALL = SMALL_REPL + SMALL_SHARDED
LANES = 128


def _pack(arrs, cols, row_mult):
    flat = jnp.concatenate([a.reshape(-1) for a in arrs])
    per = cols * row_mult
    n = -(-flat.shape[0] // per) * per
    return jnp.pad(flat, (0, n - flat.shape[0])).reshape(n // cols, cols)


def _unpack(flat2d, shapes):
    flat = flat2d.reshape(-1)
    out, off = [], 0
    for s in shapes:
        n = int(np.prod(s))
        out.append(flat[off:off + n].reshape(s))
        off += n
    return out


def kernel(x, norm_mix, norm_mlp, norm_final, ssm_a_re, ssm_a_im, ssm_b_re, ssm_b_im, ssm_c_re, ssm_c_im, ssm_d, ssm_log_dt, ssm_w_glu, conv_w_pw1, conv_b_pw1, conv_w_dw, conv_b_dw, conv_ln_g, conv_ln_b, conv_w_pw2, conv_b_pw2, gmlp_w_in, gmlp_ln_g, gmlp_ln_b, gmlp_w_s, gmlp_b_s, gmlp_w_out, attn_w_qkv, attn_w_o, mlp_w_in, mlp_w_out, loss_target, m_norm_mix, m_norm_mlp, m_norm_final, m_ssm_a_re, m_ssm_a_im, m_ssm_b_re, m_ssm_b_im, m_ssm_c_re, m_ssm_c_im, m_ssm_d, m_ssm_log_dt, m_ssm_w_glu, m_conv_w_pw1, m_conv_b_pw1, m_conv_w_dw, m_conv_b_dw, m_conv_ln_g, m_conv_ln_b, m_conv_w_pw2, m_conv_b_pw2, m_gmlp_w_in, m_gmlp_ln_g, m_gmlp_ln_b, m_gmlp_w_s, m_gmlp_b_s, m_gmlp_w_out, m_attn_w_qkv, m_attn_w_o, m_mlp_w_in, m_mlp_w_out, v_norm_mix, v_norm_mlp, v_norm_final, v_ssm_a_re, v_ssm_a_im, v_ssm_b_re, v_ssm_b_im, v_ssm_c_re, v_ssm_c_im, v_ssm_d, v_ssm_log_dt, v_ssm_w_glu, v_conv_w_pw1, v_conv_b_pw1, v_conv_w_dw, v_conv_b_dw, v_conv_ln_g, v_conv_ln_b, v_conv_w_pw2, v_conv_b_pw2, v_gmlp_w_in, v_gmlp_ln_g, v_gmlp_ln_b, v_gmlp_w_s, v_gmlp_b_s, v_gmlp_w_out, v_attn_w_qkv, v_attn_w_o, v_mlp_w_in, v_mlp_w_out):
    args = dict(locals())
    w = {n: args[n] for n in WEIGHTS}
    m = {n: args["m_" + n] for n in WEIGHTS}
    v = {n: args["v_" + n] for n in WEIGHTS}
    chip = 2 * lax.axis_index("x") + lax.axis_index("y")
    core = lax.axis_index("c")

    sm_shapes = [w[n].shape for n in SMALL_SHARDED]
    sflat = _pack([w[n] for n in SMALL_SHARDED], LANES, 8)
    rs = sflat.shape[0]
    sall = _allgather8(sflat, name="gather_small")
    started, token = [], sall
    sall = sall.reshape(8, rs, LANES)
    for li, mixer in enumerate(LAYER_MIXER_WEIGHTS):
        names = list(mixer) + ["mlp_w_in", "mlp_w_out"]
        shards = [w[n][0] for n in mixer] + [w["mlp_w_in"][li], w["mlp_w_out"][li]]
        halves = [s.astype(BF16).reshape((2, s.shape[0] // 2) + s.shape[1:]) for s in shards]
        send_sems, recv_sems, halves, lands, token = _gather_start(halves, token, name=f"gather_start_{li}")
        started.append((names, (send_sems, recv_sems, halves, lands)))

    def layer_weights(li, after):
        names, st = started[li]
        halves, lands = _gather_wait(st, after, name=f"gather_wait_{li}")
        out = {}
        for n, h, arr in zip(names, halves, lands):
            arr = lax.dynamic_update_index_in_dim(arr, h, chip, axis=0)
            arr = arr.reshape((N_CHIPS, arr.shape[1] * arr.shape[2]) + arr.shape[3:])
            if BIG_AXIS[n] == -1:
                out[n] = _Stored(arr, "cols")
                if n in ("attn_w_qkv", "attn_w_o"):
                    out[n + "_plain"] = jnp.swapaxes(arr, 0, 1).reshape(arr.shape[1], -1)
                out[n].plain = jnp.swapaxes(arr, 0, 1).reshape(arr.shape[1], -1)
            else:
                out[n] = arr.reshape(-1, arr.shape[-1])
        return out

    p = {}
    per_chip = [_unpack(sall[2 * k], sm_shapes) for k in range(N_CHIPS)]
    for i, n in enumerate(SMALL_SHARDED):
        p[n] = jnp.concatenate([per_chip[k][i] for k in range(N_CHIPS)], axis=-1)
    for n in SMALL_REPL:
        p[n] = w[n]
    p['conv_w_dw'] = p['conv_w_dw'][0]

    in_flight, arrived, n_rounds = [], {}, [0]

    def finish_round(after):
        k, names, plans, st = in_flight.pop(0)
        srcs, lands = _reduce_wait(st, plans, after, name=f"grads_wait_{k}")
        for n, plan, src, land in zip(names, plans, srcs, lands):
            arrived.setdefault(n, []).append((plan, src, land))

    def grads_done(group):
        names, srcs, plans, lands = [], [], [], []
        for n, v in group.items():
            if isinstance(v, tuple):
                src, plan = v[1].reshape(1, N_CHIPS, -1, v[1].shape[-1]), ((0, v[0]),)
                while any(n in rd[1] for rd in in_flight):
                    finish_round(src)
            else:
                src, plan = v.reshape(2, N_CHIPS, -1, v.shape[-1]), ((0, 0), (1, 1))
            land = arrived[n][-1][2] if n in arrived else lax.empty((2 * N_CHIPS,) + src.shape[2:], BF16)
            names.append(n), srcs.append(src), plans.append(plan), lands.append(land)
        st = _reduce_start(srcs, lands, plans, None, name=f"grads_start_{n_rounds[0]}")
        in_flight.append((n_rounds[0], names, plans, st[:4]))
        n_rounds[0] += 1
        return st[4]

    loss_part, grad_x, g = _local_step(x, loss_target, p, layer_weights, token, grads_done)
    loss = lax.psum(loss_part[0, 0], ("x", "y", "c"))

    my_id = 2 * chip + core
    while in_flight:
        finish_round(grad_x)
    totals = []
    for n in BIG:
        own = None
        for plan, src, land in arrived[n]:
            slab = lax.dynamic_index_in_dim(src, chip, axis=1, keepdims=False)
            if len(plan) == 2:
                own = lax.dynamic_index_in_dim(slab, core, axis=0, keepdims=False)
            else:
                own = slab[0] if own is None else jnp.where(core == plan[0][1], slab[0], own)
        totals.append(_sum8(arrived[n][-1][2], own, my_id, name="owner_sum_" + n))
    shared = _share_halves(totals, name="grads_share_halves")
    big_grads = {}
    for n, arr, t in zip(BIG, shared, totals):
        arr = lax.dynamic_update_index_in_dim(arr, t[None], core, axis=0)
        big_grads[n] = arr.reshape(w[n].shape)

    small_full_shapes = [g[n].shape for n in SMALL]
    gs = _pack([g[n] for n in SMALL], LANES, 8)
    rg = gs.shape[0]
    gs_all = _allgather8(gs, name="gather_small_grads").reshape(8, rg, LANES)
    gs_sum = _rowwise(lambda *a: (functools.reduce(lambda s, t: s + t, a),), [gs_all[k] for k in range(8)], [], (F32,),
                      tr=rg, name="small_grads_sum")[0]
    small_grads = dict(zip(SMALL, _unpack(gs_sum, small_full_shapes)))
    for n in SMALL:
        small_grads[n] = small_grads[n].reshape(p_shape_full(w[n], -1 if n in SMALL_SHARDED else None))
    for n in SMALL_SHARDED:
        width = w[n].shape[-1]
        small_grads[n] = lax.dynamic_slice_in_dim(small_grads[n], chip * width, width, axis=-1)

    grad, delta, new_m, new_v = {}, {}, {}, {}
    for n in BIG:
        shape = w[n].shape
        two_d = lambda t: t.reshape(-1, shape[-1])
        grad[n] = big_grads[n]
        d_, m_, v_ = _adamw(two_d(w[n]), two_d(grad[n]), two_d(m[n]), two_d(v[n]), name="adamw_" + n)
        delta[n], new_m[n], new_v[n] = d_.reshape(shape), m_.reshape(shape), v_.reshape(shape)
    for n in SMALL:
        shape = w[n].shape
        two_d = lambda t: t.reshape(-1, shape[-1])
        grad[n] = small_grads[n]
        d_, m_, v_ = _adamw(two_d(w[n]), two_d(grad[n]), two_d(m[n]), two_d(v[n]), name="adamw_" + n)
        delta[n], new_m[n], new_v[n] = d_.reshape(shape), m_.reshape(shape), v_.reshape(shape)

    return (loss, grad_x, *[grad[n] for n in WEIGHTS], *[delta[n] for n in WEIGHTS],
            *[new_m[n] for n in WEIGHTS], *[new_v[n] for n in WEIGHTS])


def p_shape_full(shard, axis):
    s = list(shard.shape)
    if axis is not None:
        s[axis] *= N_CHIPS
    return tuple(s)
```

```python
import functools
import math

import jax
import jax.numpy as jnp
import numpy as np
from jax import lax
from jax.experimental import pallas as pl
from jax.experimental.pallas import tpu as pltpu

F32 = jnp.float32
BF16 = jnp.bfloat16
MESH = pl.DeviceIdType.MESH

EPS = 1e-6
S5_UNROLL = 8
SSM_GROUP = 16
SSM_STATE = 64
CONV_WIDTH = 31
CONV_HALO = 32
GMLP_CHUNK = 128
GMLP_HEADS = 4
ATT_DILS = (1, 4, 16)
ATT_BLK = 128
ATT_HEADS = 8
HEAD_DIM = 64
ATT_W = ATT_HEADS * HEAD_DIM
N_CHIPS = 4
ADAM_LR, ADAM_B1, ADAM_B2, ADAM_EPS, ADAM_WD, ADAM_STEP = 1e-3, 0.9, 0.999, 1e-8, 0.01, 10

VMEM_BYTES_V7X = 64 * 1024 * 1024
VMEM_LIMIT = VMEM_BYTES_V7X - 8 * 1024 * 1024
MASK_VALUE = -1e30
LANE_TILE = 128


def _cp(sem=None):
    return pltpu.CompilerParams(dimension_semantics=sem, vmem_limit_bytes=VMEM_LIMIT)


def _pick_tile(total, target):
    for cand in range(min(target, total) // LANE_TILE * LANE_TILE, 0, -LANE_TILE):
        if total % cand == 0:
            return cand
    return total


class _Stored:
    def __init__(self, arr, kind="plain", lead=()):
        self.arr, self.kind, self.lead = arr, kind, tuple(lead)
        r, c = arr.shape[-2:]
        self.shape = (r, c * N_CHIPS) if kind == "cols" else (r * N_CHIPS, c) if kind == "rows" else (r, c)

    def spec(self, br, bc, rc_of):
        lead, nl = self.lead, len(self.lead)
        if self.kind == "plain":
            return pl.BlockSpec((None,) * nl + (br, bc), lambda i, j, k: (*lead, *rc_of(i, j, k)))
        if self.kind == "cols":
            per = self.arr.shape[-1] // bc
            assert per * bc == self.arr.shape[-1]

            def imap(i, j, k):
                r, c = rc_of(i, j, k)
                return (c // per, *lead, r, c % per)
        else:
            per = self.arr.shape[-2] // br
            assert per * br == self.arr.shape[-2]

            def imap(i, j, k):
                r, c = rc_of(i, j, k)
                return (r // per, *lead, r % per, c)
        return pl.BlockSpec((None,) * (nl + 1) + (br, bc), imap)


class _ColBlocks:
    kind = "colblocks"

    def __init__(self, arr, first, stride, count, width):
        self.arr, self.first, self.stride, self.width = arr, first, stride, width
        self.shape = (arr.shape[0], count * width)

    def spec(self, br, bc, rc_of):
        per = self.width // bc
        assert per * bc == self.width

        def imap(i, j, k):
            r, c = rc_of(i, j, k)
            return (r, (self.first + (c // per) * self.stride) * per + c % per)
        return pl.BlockSpec((br, bc), imap)


def _mm(a, b, *, ta=False, tb=False, out_dtypes=(F32,), tm=1024, tn=1024, tk=1024,
        pro_a=None, pro_b=None, epi=None, extras=(), n_row_sums=0, out=None, after=None, name):
    if ta:
        K, M = a.shape
    else:
        M, K = a.shape
    if not isinstance(b, (_Stored, _ColBlocks)):
        b = _Stored(b)
    N, Kb = b.shape if tb else b.shape[::-1]
    assert K == Kb, (a.shape, b.shape, ta, tb)
    col_unit = b.width if b.kind == "colblocks" else b.arr.shape[-1] if b.kind == "cols" else b.shape[1]
    row_unit = b.arr.shape[-2] if b.kind == "rows" else b.shape[0]
    n_unit, k_unit = (row_unit, col_unit) if tb else (col_unit, row_unit)
    m_unit = M
    if out is not None:
        m_unit, n_unit = out[4], math.gcd(n_unit, out[5])
    tm, tn, tk = _pick_tile(m_unit, tm), _pick_tile(n_unit, tn), _pick_tile(k_unit, tk)
    assert not n_row_sums or tn == N
    nk = K // tk
    a_spec = (pl.BlockSpec((tk, tm), lambda i, j, k: (k, i)) if ta
              else pl.BlockSpec((tm, tk), lambda i, j, k: (i, k)))
    b_spec = b.spec(tn, tk, lambda i, j, k: (j, k)) if tb else b.spec(tk, tn, lambda i, j, k: (k, j))
    ex_specs = []
    for e in extras:
        if e.shape[0] == 1:
            ex_specs.append(pl.BlockSpec((1, tn), lambda i, j, k: (0, j)))
        else:
            assert e.shape == (M, N), (e.shape, M, N)
            ex_specs.append(pl.BlockSpec((tm, tn), lambda i, j, k: (i, j)))
    dims = (((0 if ta else 1,), (1 if tb else 0,)), ((), ()))
    n_ex, n_out = len(extras), len(out_dtypes)
    direct = epi is None and n_out == 1 and out_dtypes[0] == F32
    use_acc = nk > 1 and not direct
    operands, aliases, alias_specs = [a, b.arr, *extras], {}, []
    if after is not None:
        operands.append(after)
        alias_specs.append(pl.BlockSpec(memory_space=pl.ANY))
    if out is None:
        n_tile_out = n_out - n_row_sums
        out_specs = ([pl.BlockSpec((tm, tn), lambda i, j, k: (i, j))] * n_tile_out
                     + [pl.BlockSpec((1, tn), lambda i, j, k: (0, j))] * n_row_sums)
        out_shape = ([jax.ShapeDtypeStruct((M, N), dt) for dt in out_dtypes[:n_tile_out]]
                     + [jax.ShapeDtypeStruct((1, N), dt) for dt in out_dtypes[n_tile_out:]])
    else:
        shape, block_fn, imap_fn, alias = out[:4]
        assert n_out == 1
        out_specs = [pl.BlockSpec(block_fn(tm, tn), imap_fn(tm, tn))]
        out_shape = [jax.ShapeDtypeStruct(shape, out_dtypes[0])]
        if alias is not None:
            operands.append(alias)
            aliases = {len(operands) - 1: 0}
            alias_specs.append(pl.BlockSpec(memory_space=pl.ANY))
    n_in = len(operands)

    def finish(r, ex, outs, first_row_tile):
        res = epi(r, *[e[...] for e in ex]) if epi is not None else (r,)
        n_tile_out = n_out - n_row_sums
        for o, v in zip(outs[:n_tile_out], res):
            o[...] = v.astype(o.dtype)
        for o, v in zip(outs[n_tile_out:], res[n_tile_out:]):
            @pl.when(first_row_tile)
            def _(o=o):
                o[...] = jnp.zeros_like(o)
            o[...] += v

    def body(*refs):
        a_ref, b_ref = refs[:2]
        ex = refs[2:2 + n_ex]
        outs = refs[n_in:n_in + n_out]
        first_row_tile = pl.program_id(0) == 0
        at, bt = a_ref[...], b_ref[...]
        if pro_a is not None:
            at = pro_a(at)
        if pro_b is not None:
            bt = pro_b(bt)
        part = lax.dot_general(at, bt, dims, preferred_element_type=F32)
        if nk == 1:
            finish(part, ex, outs, first_row_tile)
            return
        acc = refs[-1] if use_acc else outs[0]
        k = pl.program_id(2)

        @pl.when(k == 0)
        def _():
            acc[...] = part

        @pl.when(k > 0)
        def _():
            acc[...] += part

        if use_acc:
            @pl.when(k == nk - 1)
            def _():
                finish(acc[...], ex, outs, first_row_tile)

    res = pl.pallas_call(
        body, grid=(M // tm, N // tn, nk),
        in_specs=[a_spec, b_spec] + ex_specs + alias_specs,
        out_specs=out_specs, out_shape=out_shape,
        scratch_shapes=[pltpu.VMEM((tm, tn), F32)] if use_acc else [],
        input_output_aliases=aliases,
        compiler_params=_cp(("arbitrary" if n_row_sums else "parallel", "parallel", "arbitrary")), name=name,
    )(*operands)
    return res[0] if n_out == 1 else res


def _pick_rows(total, target):
    for cand in range(min(target, total) // 8 * 8, 0, -8):
        if total % cand == 0:
            return cand
    return total


def _rowwise(f, rows, params, out_dtypes, *, tr, name):
    T = rows[0].shape[0]
    tr = _pick_rows(T, tr)
    nr, npar = len(rows), len(params)
    blk = [jax.ShapeDtypeStruct((tr, r.shape[1]), F32) for r in rows]
    blk += [jax.ShapeDtypeStruct(p.shape, F32) for p in params]
    out_avals = jax.eval_shape(f, *blk)

    def body(*refs):
        res = f(*[r[...].astype(F32) for r in refs[:nr + npar]])
        for o, v in zip(refs[nr + npar:], res):
            o[...] = v.astype(o.dtype)

    out = pl.pallas_call(
        body, grid=(T // tr,),
        in_specs=[pl.BlockSpec((tr, r.shape[1]), lambda i: (i, 0)) for r in rows]
        + [pl.BlockSpec(p.shape, lambda i, nd=p.ndim: (0,) * nd) for p in params],
        out_specs=[pl.BlockSpec((tr, o.shape[1]), lambda i: (i, 0)) for o in out_avals],
        out_shape=[jax.ShapeDtypeStruct((T, o.shape[1]), dt) for o, dt in zip(out_avals, out_dtypes)],
        compiler_params=_cp(("parallel",)), name=name,
    )(*rows, *params)
    return out


def _rowwise_vjp(f, rows, params, cots, drow_dtypes, *, adds=None, after=None, tr, name):
    adds = adds or {}
    T = rows[0].shape[0]
    tr = _pick_rows(T, tr)
    nr, npar, nc = len(rows), len(params), len(cots)
    want, want_dt = [], []
    for i, dt in enumerate(drow_dtypes):
        for one in (dt if isinstance(dt, tuple) else (dt,)):
            if one is not None:
                want.append(i)
                want_dt.append(one)
    add_idx = sorted(set(i for i in want if i in adds))
    add_arrays = [adds[i] for i in add_idx]
    na = len(add_arrays)
    extra = [] if after is None else [after]

    def body(*refs):
        ins = [r[...].astype(F32) for r in refs[:nr + npar]]
        cvals = [r[...].astype(F32) for r in refs[nr + npar:nr + npar + nc]]
        avals = refs[nr + npar + nc:nr + npar + nc + na]
        outs = refs[nr + npar + nc + na + len(extra):]
        _, vjp = jax.vjp(f, *ins)
        grads = vjp(tuple(cvals))
        for o, i in zip(outs[:len(want)], want):
            g = grads[i]
            if i in adds:
                g = g + avals[add_idx.index(i)][...].astype(F32)
            o[...] = g.astype(o.dtype)
        step = pl.program_id(0)
        for o, g in zip(outs[len(want):], grads[nr:]):
            @pl.when(step == 0)
            def _(o=o):
                o[...] = jnp.zeros_like(o)
            o[...] += g

    rspec = lambda r: pl.BlockSpec((tr, r.shape[1]), lambda i: (i, 0))
    pspec = lambda p: pl.BlockSpec(p.shape, lambda i, nd=p.ndim: (0,) * nd)
    out = pl.pallas_call(
        body, grid=(T // tr,),
        in_specs=[rspec(r) for r in rows] + [pspec(p) for p in params] + [rspec(c) for c in cots]
        + [rspec(a) for a in add_arrays] + [pl.BlockSpec(memory_space=pl.ANY)] * len(extra),
        out_specs=[rspec(rows[i]) for i in want] + [pspec(p) for p in params],
        out_shape=[jax.ShapeDtypeStruct(rows[i].shape, dt) for i, dt in zip(want, want_dt)]
        + [jax.ShapeDtypeStruct(p.shape, F32) for p in params],
        compiler_params=_cp(("arbitrary",)), name=name,
    )(*rows, *params, *cots, *add_arrays, *extra)
    return out[:len(want)], out[len(want):]


def _f_rms(x, g):
    return (x * lax.rsqrt(jnp.mean(x * x, axis=-1, keepdims=True) + EPS) * g,)


def _ln(x, g, b):
    mu = jnp.mean(x, axis=-1, keepdims=True)
    var = jnp.mean(jnp.square(x - mu), axis=-1, keepdims=True)
    return (x - mu) * lax.rsqrt(var + EPS) * g + b


def _f_glu(z):
    d = z.shape[1] // 2
    return (z[:, :d] * jax.nn.sigmoid(z[:, d:]),)


def _f_bias_glu(z, b):
    return _f_glu(z + b)


def _f_ln_silu(y, b_dw, g, b):
    return (jax.nn.silu(_ln(y + b_dw, g, b)),)


def _f_gelu_ln(z, g, b):
    d = z.shape[1] // 2
    zz = jax.nn.gelu(z)
    return zz[:, :d], _ln(zz[:, d:], g, b)


def _f_merge(o0, o1, o2, l0, l1, l2):
    m = jnp.maximum(jnp.maximum(l0, l1), l2)
    e0, e1, e2 = jnp.exp(l0 - m), jnp.exp(l1 - m), jnp.exp(l2 - m)
    s = e0 + e1 + e2
    pair = 2 * HEAD_DIM
    first_head = lax.broadcasted_iota(jnp.int32, (o0.shape[0], pair), 1) < HEAD_DIM
    cols = []
    for hp in range(o0.shape[1] // pair):
        acc = None
        for o, e in ((o0, e0), (o1, e1), (o2, e2)):
            wgt = e / s
            wp = jnp.where(first_head, wgt[:, 2 * hp:2 * hp + 1], wgt[:, 2 * hp + 1:2 * hp + 2])
            term = wp * o[:, hp * pair:(hp + 1) * pair]
            acc = term if acc is None else acc + term
        cols.append(acc)
    return (jnp.concatenate(cols, axis=1),)


def _loss_head(x, tgt, g, *, tr, name):
    T, D = x.shape
    tr = min(tr, T)

    def f(xv, gv, tv):
        y = _f_rms(xv, gv)[0]
        return 0.5 * jnp.mean(jnp.square(y - tv), axis=-1, keepdims=True)

    def body(x_ref, t_ref, g_ref, loss_ref, dx_ref, dxb_ref, dg_ref):
        tv = t_ref[...]
        l, vjp = jax.vjp(lambda xv, gv: f(xv, gv, tv), x_ref[...], g_ref[...])
        dx, dg = vjp(jnp.ones_like(l))
        dx_ref[...] = dx
        dxb_ref[...] = dx.astype(BF16)

        @pl.when(pl.program_id(0) == 0)
        def _():
            loss_ref[...] = jnp.zeros_like(loss_ref)
            dg_ref[...] = jnp.zeros_like(dg_ref)

        loss_ref[...] += jnp.sum(l)
        dg_ref[...] += dg

    return pl.pallas_call(
        body, grid=(T // tr,),
        in_specs=[pl.BlockSpec((tr, D), lambda i: (i, 0)), pl.BlockSpec((tr, D), lambda i: (i, 0)),
                  pl.BlockSpec((1, D), lambda i: (0, 0))],
        out_specs=[pl.BlockSpec((1, 128), lambda i: (0, 0)), pl.BlockSpec((tr, D), lambda i: (i, 0)),
                   pl.BlockSpec((tr, D), lambda i: (i, 0)), pl.BlockSpec((1, D), lambda i: (0, 0))],
        out_shape=[jax.ShapeDtypeStruct((1, 128), F32), jax.ShapeDtypeStruct((T, D), F32),
                   jax.ShapeDtypeStruct((T, D), BF16), jax.ShapeDtypeStruct((1, D), F32)],
        compiler_params=_cp(("arbitrary",)), name=name,
    )(x, tgt, g)


def _adamw(w, g, m, v, *, name):
    R, C = w.shape
    tr = _pick_rows(R, max(8, 2 * 1024 * 1024 // (4 * C)))
    c1 = 1.0 - ADAM_B1 ** ADAM_STEP
    c2 = 1.0 - ADAM_B2 ** ADAM_STEP

    def body(w_ref, g_ref, m_ref, v_ref, d_ref, nm_ref, nv_ref):
        gv = g_ref[...]
        nm = ADAM_B1 * m_ref[...] + (1.0 - ADAM_B1) * gv
        nv = ADAM_B2 * v_ref[...] + (1.0 - ADAM_B2) * jnp.square(gv)
        nm_ref[...] = nm
        nv_ref[...] = nv
        d_ref[...] = -ADAM_LR * ((nm / c1) / (jnp.sqrt(nv / c2) + ADAM_EPS) + ADAM_WD * w_ref[...])

    spec = pl.BlockSpec((tr, C), lambda i: (i, 0))
    return pl.pallas_call(
        body, grid=(R // tr,), in_specs=[spec] * 4, out_specs=[spec] * 3,
        out_shape=[jax.ShapeDtypeStruct((R, C), F32)] * 3,
        compiler_params=_cp(("parallel",)), name=name,
    )(w, g, m, v)


def _s5_prep(a_re, a_im, b_re, b_im, c_re, c_im, log_dt):
    G, N = a_re.shape
    P = b_re.shape[2]
    gpb = 128 // P
    nblk = G // gpb
    dt = jnp.exp(log_dt)[:, None]
    mag = jnp.exp(a_re * dt)
    abr, abi = mag * jnp.cos(a_im * dt), mag * jnp.sin(a_im * dt)
    den = a_re * a_re + a_im * a_im
    nr, ni = abr - 1.0, abi
    qr, qi = (nr * a_re + ni * a_im) / den, (ni * a_re - nr * a_im) / den
    bbr = qr[..., None] * b_re - qi[..., None] * b_im
    bbi = qr[..., None] * b_im + qi[..., None] * b_re
    eye = jnp.eye(gpb, dtype=F32)

    def expand_b(t):
        t = t.reshape(nblk, gpb, N, P).transpose(0, 1, 3, 2)
        return (t[:, :, :, None, :] * eye[None, :, None, :, None]).reshape(nblk, gpb * P, gpb * N)

    def expand_c(t):
        t = t.reshape(nblk, gpb, P, N).transpose(0, 1, 3, 2)
        return (t[:, :, :, None, :] * eye[None, :, None, :, None]).reshape(nblk, gpb * N, gpb * P)

    return (abr.reshape(1, G * N), abi.reshape(1, G * N), expand_b(bbr), expand_b(bbi),
            expand_c(c_re), expand_c(c_im))


def _s5_fwd(h, abr, abi, bre, bim, cre, cim, d, *, B, L, name):
    T, D = h.shape
    S = T // B
    L = min(L, S)
    nc = S // L
    nblk, cb, sb = bre.shape
    GN = abr.shape[1]

    def body(h_ref, ar_ref, ai_ref, bre_ref, bim_ref, cre_ref, cim_ref, d_ref,
             y_ref, yb_ref, xr_ref, xi_ref, er_ref, ei_ref, sr, si, car, cai):
        ci = pl.program_id(1)

        @pl.when(ci == 0)
        def _():
            car[...] = jnp.zeros_like(car)
            cai[...] = jnp.zeros_like(cai)

        for j in range(nblk):
            u = h_ref[:, j * cb:(j + 1) * cb]
            sr[:, j * sb:(j + 1) * sb] = jnp.dot(u, bre_ref[j], preferred_element_type=F32)
            si[:, j * sb:(j + 1) * sb] = jnp.dot(u, bim_ref[j], preferred_element_type=F32)
        ar, ai = ar_ref[...], ai_ref[...]

        def step(t, carry):
            pr, pi = carry
            nr = ar * pr - ai * pi + sr[pl.ds(t, 1), :]
            ni = ar * pi + ai * pr + si[pl.ds(t, 1), :]
            sr[pl.ds(t, 1), :] = nr
            si[pl.ds(t, 1), :] = ni
            return nr, ni

        pr, pi = lax.fori_loop(0, L, step, (car[...], cai[...]), unroll=S5_UNROLL)
        car[...] = pr
        cai[...] = pi
        er_ref[0] = pr
        ei_ref[0] = pi
        for j in range(nblk):
            xr = sr[:, j * sb:(j + 1) * sb].astype(BF16)
            xi = si[:, j * sb:(j + 1) * sb].astype(BF16)
            xr_ref[:, j * sb:(j + 1) * sb] = xr
            xi_ref[:, j * sb:(j + 1) * sb] = xi
            y = (jnp.dot(xr, cre_ref[j], preferred_element_type=F32)
                 - jnp.dot(xi, cim_ref[j], preferred_element_type=F32))
            u = h_ref[:, j * cb:(j + 1) * cb].astype(F32)
            y = y + d_ref[:, j * cb:(j + 1) * cb] * u
            y_ref[:, j * cb:(j + 1) * cb] = y
            yb_ref[:, j * cb:(j + 1) * cb] = jax.nn.gelu(y).astype(BF16)

    tok = lambda w: pl.BlockSpec((L, w), lambda b, c: (b * nc + c, 0))
    whole = lambda p: pl.BlockSpec(p.shape, lambda b, c, nd=p.ndim: (0,) * nd)
    end = pl.BlockSpec((1, 1, GN), lambda b, c: (b * nc + c, 0, 0))
    return pl.pallas_call(
        body, grid=(B, nc),
        in_specs=[tok(D)] + [whole(p) for p in (abr, abi, bre, bim, cre, cim, d)],
        out_specs=[tok(D), tok(D), tok(GN), tok(GN), end, end],
        out_shape=[jax.ShapeDtypeStruct((T, D), F32), jax.ShapeDtypeStruct((T, D), BF16),
                   jax.ShapeDtypeStruct((T, GN), BF16),
                   jax.ShapeDtypeStruct((T, GN), BF16), jax.ShapeDtypeStruct((B * nc, 1, GN), F32),
                   jax.ShapeDtypeStruct((B * nc, 1, GN), F32)],
        scratch_shapes=[pltpu.VMEM((L, GN), F32), pltpu.VMEM((L, GN), F32),
                        pltpu.VMEM((1, GN), F32), pltpu.VMEM((1, GN), F32)],
        compiler_params=_cp(("arbitrary", "arbitrary")), name=name,
    )(h, abr, abi, bre, bim, cre, cim, d)


def _s5_bwd(dy, h, xr, xi, er, ei, abr, abi, bre, bim, cre, cim, d, *, B, L, name):
    T, D = h.shape
    S = T // B
    L = min(L, S)
    nc = S // L
    nblk, cb, sb = bre.shape
    GN = abr.shape[1]
    dims_nt = (((1,), (1,)), ((), ()))
    dims_tn = (((0,), (0,)), ((), ()))

    def body(dy_ref, h_ref, xr_ref, xi_ref, er_ref, ei_ref, ar_ref, ai_ref, bre_ref, bim_ref,
             cre_ref, cim_ref, d_ref,
             dh_ref, dbre_ref, dbim_ref, dcre_ref, dcim_ref, dar_ref, dai_ref, dd_ref,
             lr, li, car, cai):
        b, cstep = pl.program_id(0), pl.program_id(1)
        ci = nc - 1 - cstep

        @pl.when((b == 0) & (cstep == 0))
        def _():
            for r in (dbre_ref, dbim_ref, dcre_ref, dcim_ref, dar_ref, dai_ref, dd_ref):
                r[...] = jnp.zeros_like(r)

        @pl.when(cstep == 0)
        def _():
            car[...] = jnp.zeros_like(car)
            cai[...] = jnp.zeros_like(cai)

        for j in range(nblk):
            dyj = dy_ref[:, j * cb:(j + 1) * cb].astype(BF16)
            lr[:, j * sb:(j + 1) * sb] = lax.dot_general(dyj, cre_ref[j], dims_nt, preferred_element_type=F32)
            li[:, j * sb:(j + 1) * sb] = -lax.dot_general(dyj, cim_ref[j], dims_nt, preferred_element_type=F32)
        ar, ai = ar_ref[...], ai_ref[...]

        def step(s, carry):
            t = L - 1 - s
            pr, pi = carry
            nr = lr[pl.ds(t, 1), :] + ar * pr + ai * pi
            ni = li[pl.ds(t, 1), :] - ai * pr + ar * pi
            lr[pl.ds(t, 1), :] = nr
            li[pl.ds(t, 1), :] = ni
            return nr, ni

        pr, pi = lax.fori_loop(0, L, step, (car[...], cai[...]), unroll=S5_UNROLL)
        car[...] = pr
        cai[...] = pi
        has_prev = (ci > 0).astype(F32)
        first_row = lax.broadcasted_iota(jnp.int32, (L, sb), 0) == 0
        for j in range(nblk):
            cs = slice(j * cb, (j + 1) * cb)
            ss = slice(j * sb, (j + 1) * sb)
            lrj, lij = lr[:, ss], li[:, ss]
            xrj, xij = xr_ref[:, ss], xi_ref[:, ss]
            pr_j = jnp.where(first_row, er_ref[0][:, ss] * has_prev, pltpu.roll(xrj.astype(F32), 1, 0))
            pi_j = jnp.where(first_row, ei_ref[0][:, ss] * has_prev, pltpu.roll(xij.astype(F32), 1, 0))
            dar_ref[:, ss] += jnp.sum(lrj * pr_j + lij * pi_j, axis=0, keepdims=True)
            dai_ref[:, ss] += jnp.sum(lij * pr_j - lrj * pi_j, axis=0, keepdims=True)
            lrb, lib = lrj.astype(BF16), lij.astype(BF16)
            hj = h_ref[:, cs]
            dyf = dy_ref[:, cs]
            dyj = dyf.astype(BF16)
            dbre_ref[j] += lax.dot_general(hj, lrb, dims_tn, preferred_element_type=F32)
            dbim_ref[j] += lax.dot_general(hj, lib, dims_tn, preferred_element_type=F32)
            dcre_ref[j] += lax.dot_general(xrj, dyj, dims_tn, preferred_element_type=F32)
            dcim_ref[j] -= lax.dot_general(xij, dyj, dims_tn, preferred_element_type=F32)
            du = (lax.dot_general(lrb, bre_ref[j], dims_nt, preferred_element_type=F32)
                  + lax.dot_general(lib, bim_ref[j], dims_nt, preferred_element_type=F32))
            dh_ref[:, cs] = du + d_ref[:, cs] * dyf
            dd_ref[:, cs] += jnp.sum(dyf * hj.astype(F32), axis=0, keepdims=True)

    tok = lambda w: pl.BlockSpec((L, w), lambda b, c: (b * nc + nc - 1 - c, 0))
    whole = lambda p: pl.BlockSpec(p.shape, lambda b, c, nd=p.ndim: (0,) * nd)
    prev_end = pl.BlockSpec((1, 1, GN), lambda b, c: (b * nc + jnp.maximum(nc - 2 - c, 0), 0, 0))
    params = (abr, abi, bre, bim, cre, cim, d)
    acc_shapes = [bre.shape, bim.shape, cre.shape, cim.shape, abr.shape, abi.shape, d.shape]
    out = pl.pallas_call(
        body, grid=(B, nc),
        in_specs=[tok(D), tok(D), tok(GN), tok(GN), prev_end, prev_end] + [whole(p) for p in params],
        out_specs=[tok(D)] + [pl.BlockSpec(s, lambda b, c, nd=len(s): (0,) * nd) for s in acc_shapes],
        out_shape=[jax.ShapeDtypeStruct((T, D), F32)] + [jax.ShapeDtypeStruct(s, F32) for s in acc_shapes],
        scratch_shapes=[pltpu.VMEM((L, GN), F32), pltpu.VMEM((L, GN), F32),
                        pltpu.VMEM((1, GN), F32), pltpu.VMEM((1, GN), F32)],
        compiler_params=_cp(("arbitrary", "arbitrary")), name=name,
    )(dy, h, xr, xi, er, ei, *params)
    return out


def _conv_fwd(zp, w, *, R, tc, name):
    B, SP, C = zp.shape
    S = SP - CONV_HALO
    R, tc = min(R, S), min(tc, C)

    def body(z_ref, w_ref, y_ref):
        def chunk(ci, _):
            start = pl.multiple_of(ci * R, 8)
            ze = z_ref[pl.ds(start, R + CONV_HALO), :]
            acc = jnp.zeros((R, tc), F32)
            for m in range(CONV_WIDTH):
                k = CONV_WIDTH - 1 - m
                sh = ze if m == 0 else pltpu.roll(ze, m, 0)
                acc = acc + w_ref[k:k + 1, :] * sh[CONV_HALO:, :]
            y_ref[pl.ds(start, R), :] = acc
            return 0

        lax.fori_loop(0, S // R, chunk, 0)

    return pl.pallas_call(
        body, grid=(B, C // tc),
        in_specs=[pl.BlockSpec((None, SP, tc), lambda b, c: (b, 0, c)),
                  pl.BlockSpec((32, tc), lambda b, c: (0, c))],
        out_specs=pl.BlockSpec((None, S, tc), lambda b, c: (b, 0, c)),
        out_shape=jax.ShapeDtypeStruct((B, S, C), F32),
        compiler_params=_cp(("parallel", "parallel")), name=name,
    )(zp, w)


def _conv_bwd(zp, dyp, w, *, R, tc, name):
    B, SP, C = zp.shape
    S = SP - CONV_HALO
    R, tc = min(R, S), min(tc, C)

    def body(z_ref, dy_ref, w_ref, dz_ref, dw_ref):
        @pl.when(pl.program_id(1) == 0)
        def _():
            dw_ref[...] = jnp.zeros_like(dw_ref)

        def chunk(ci, _):
            start = pl.multiple_of(ci * R, 8)
            zc = z_ref[pl.ds(start + CONV_HALO, R), :]
            de = dy_ref[pl.ds(start, R + CONV_HALO), :]
            acc = jnp.zeros((R, tc), F32)
            for m in range(CONV_WIDTH):
                k = CONV_WIDTH - 1 - m
                ds_ = (de if m == 0 else pltpu.roll(de, R + CONV_HALO - m, 0))[:R, :]
                acc = acc + w_ref[k:k + 1, :] * ds_
                dw_ref[k:k + 1, :] += jnp.sum(ds_ * zc, axis=0, keepdims=True)
            dz_ref[pl.ds(start, R), :] = acc
            return 0

        lax.fori_loop(0, S // R, chunk, 0)

    return pl.pallas_call(
        body, grid=(C // tc, B),
        in_specs=[pl.BlockSpec((None, SP, tc), lambda c, b: (b, 0, c)),
                  pl.BlockSpec((None, SP, tc), lambda c, b: (b, 0, c)),
                  pl.BlockSpec((32, tc), lambda c, b: (0, c))],
        out_specs=[pl.BlockSpec((None, S, tc), lambda c, b: (b, 0, c)),
                   pl.BlockSpec((32, tc), lambda c, b: (0, c))],
        out_shape=[jax.ShapeDtypeStruct((B, S, C), F32), jax.ShapeDtypeStruct((32, C), F32)],
        compiler_params=_cp(("parallel", "arbitrary")), name=name,
    )(zp, dyp, w)


def _gmlp_fwd(u, vn, ws, bcol, *, nck, name):
    T, E = u.shape
    H = ws.shape[0]
    he = E // H
    rows = nck * GMLP_CHUNK
    rows = min(rows, T)
    n_in = rows // GMLP_CHUNK

    def body(u_ref, v_ref, ws_ref, b_ref, o_ref):
        for c in range(n_in):
            rs = slice(c * GMLP_CHUNK, (c + 1) * GMLP_CHUNK)
            for hh in range(H):
                cs = slice(hh * he, (hh + 1) * he)
                v2 = jnp.dot(ws_ref[hh], v_ref[rs, cs].astype(BF16), preferred_element_type=F32)
                v2 = v2 + b_ref[:, hh:hh + 1]
                o_ref[rs, cs] = (u_ref[rs, cs] * v2).astype(o_ref.dtype)

    tok = pl.BlockSpec((rows, E), lambda i: (i, 0))
    return pl.pallas_call(
        body, grid=(T // rows,),
        in_specs=[tok, tok, pl.BlockSpec(ws.shape, lambda i: (0, 0, 0)), pl.BlockSpec(bcol.shape, lambda i: (0, 0))],
        out_specs=tok, out_shape=jax.ShapeDtypeStruct((T, E), BF16),
        compiler_params=_cp(("parallel",)), name=name,
    )(u, vn, ws, bcol)


def _gmlp_bwd(duv, u, vn, ws, bcol, *, nck, name):
    T, E = u.shape
    H = ws.shape[0]
    he = E // H
    rows = min(nck * GMLP_CHUNK, T)
    n_in = rows // GMLP_CHUNK
    dims_nt = (((1,), (1,)), ((), ()))
    dims_tn = (((0,), (0,)), ((), ()))

    def body(g_ref, u_ref, v_ref, ws_ref, b_ref, du_ref, dv_ref, dws_ref, db_ref):
        @pl.when(pl.program_id(0) == 0)
        def _():
            dws_ref[...] = jnp.zeros_like(dws_ref)
            db_ref[...] = jnp.zeros_like(db_ref)

        for c in range(n_in):
            rs = slice(c * GMLP_CHUNK, (c + 1) * GMLP_CHUNK)
            for hh in range(H):
                cs = slice(hh * he, (hh + 1) * he)
                vb = v_ref[rs, cs].astype(BF16)
                v2 = jnp.dot(ws_ref[hh], vb, preferred_element_type=F32) + b_ref[:, hh:hh + 1]
                g = g_ref[rs, cs]
                du_ref[rs, cs] = g * v2
                dv2 = g * u_ref[rs, cs]
                dv2b = dv2.astype(BF16)
                dv_ref[rs, cs] = lax.dot_general(ws_ref[hh], dv2b, dims_tn, preferred_element_type=F32)
                dws_ref[hh] += lax.dot_general(dv2b, vb, dims_nt, preferred_element_type=F32)
                db_ref[:, hh:hh + 1] += jnp.sum(dv2, axis=1, keepdims=True)

    tok = pl.BlockSpec((rows, E), lambda i: (i, 0))
    return pl.pallas_call(
        body, grid=(T // rows,),
        in_specs=[tok, tok, tok, pl.BlockSpec(ws.shape, lambda i: (0, 0, 0)), pl.BlockSpec(bcol.shape, lambda i: (0, 0))],
        out_specs=[tok, tok, pl.BlockSpec(ws.shape, lambda i: (0, 0, 0)), pl.BlockSpec(bcol.shape, lambda i: (0, 0))],
        out_shape=[jax.ShapeDtypeStruct((T, E), F32), jax.ShapeDtypeStruct((T, E), F32),
                   jax.ShapeDtypeStruct(ws.shape, F32), jax.ShapeDtypeStruct(bcol.shape, F32)],
        compiler_params=_cp(("arbitrary",)), name=name,
    )(duv, u, vn, ws, bcol)


PAIRS = ATT_HEADS // 2


def _att_consts():
    ji = lax.broadcasted_iota(jnp.int32, (2 * ATT_BLK, ATT_BLK), 0)
    ii = lax.broadcasted_iota(jnp.int32, (2 * ATT_BLK, ATT_BLK), 1)
    dist = ii + ATT_BLK - ji
    band = (dist >= 0) & (dist <= ATT_BLK)
    cur = ji >= ATT_BLK
    first_head = lax.broadcasted_iota(jnp.int32, (ATT_BLK, 2 * HEAD_DIM), 1) < HEAD_DIM
    return band, cur, first_head


def _both_heads(t, first_head):
    zero = jnp.zeros_like(t)
    return jnp.concatenate([jnp.where(first_head, t, zero), jnp.where(first_head, zero, t)], axis=0)


def _att_specs(nbk, offs, nsteps, rev):
    rows = nbk * ATT_BLK
    step = (lambda i: nsteps - 1 - i) if rev else (lambda i: i)
    qoff, koff, voff = offs
    blk = lambda off: pl.BlockSpec((rows, 2 * HEAD_DIM), lambda hp, i: (step(i), off + hp))
    prev = lambda off: pl.BlockSpec((ATT_BLK, 2 * HEAD_DIM), lambda hp, i: (jnp.maximum(step(i) * nbk - 1, 0), off + hp))
    out = pl.BlockSpec((rows, 2 * HEAD_DIM), lambda hp, i: (step(i), hp))
    stat = pl.BlockSpec((2, nbk, ATT_BLK), lambda hp, i: (hp, step(i), 0))
    return [blk(qoff), blk(koff), prev(koff), blk(voff), prev(voff)], out, stat


def _att_fwd(arr, offs, *, nb, nbk, name):
    T = arr.shape[0]
    nbk = min(nbk, T // ATT_BLK)
    nsteps = T // (nbk * ATT_BLK)
    scale = HEAD_DIM ** -0.5
    dims_nt = (((1,), (1,)), ((), ()))
    dims_tn = (((0,), (0,)), ((), ()))

    def body(q_ref, k_ref, kp_ref, v_ref, vp_ref, o_ref, lse_ref):
        i = pl.program_id(1)
        band, cur, first_head = _att_consts()
        for jj in range(nbk):
            rs = slice(jj * ATT_BLK, (jj + 1) * ATT_BLK)
            ps = slice((jj - 1) * ATT_BLK, jj * ATT_BLK)
            has_prev = ((i * nbk + jj) & (nb - 1)) != 0
            valid = band & (cur | has_prev)
            kk = jnp.concatenate([kp_ref[...] if jj == 0 else k_ref[ps, :], k_ref[rs, :]], axis=0)
            vv = jnp.concatenate([vp_ref[...] if jj == 0 else v_ref[ps, :], v_ref[rs, :]], axis=0)
            q2 = _both_heads(q_ref[rs, :], first_head)
            st = lax.dot_general(kk, q2, dims_nt, preferred_element_type=F32) * scale
            st = jnp.where(jnp.concatenate([valid, valid], axis=1), st, MASK_VALUE)
            m = jnp.max(st, axis=0, keepdims=True)
            p = jnp.exp(st - m)
            l = jnp.sum(p, axis=0, keepdims=True)
            lse = m + jnp.log(l)
            lse_ref[0, jj:jj + 1, :] = lse[:, :ATT_BLK]
            lse_ref[1, jj:jj + 1, :] = lse[:, ATT_BLK:]
            pn = (p / l).astype(BF16)
            o2 = lax.dot_general(pn, vv, dims_tn, preferred_element_type=F32)
            o_ref[rs, :] = jnp.where(first_head, o2[:ATT_BLK], o2[ATT_BLK:])

    ins, out, stat = _att_specs(nbk, offs, nsteps, False)
    return pl.pallas_call(
        body, grid=(PAIRS, nsteps), in_specs=ins, out_specs=[out, stat],
        out_shape=[jax.ShapeDtypeStruct((T, ATT_W), F32), jax.ShapeDtypeStruct((ATT_HEADS, T // ATT_BLK, ATT_BLK), F32)],
        compiler_params=_cp(("parallel", "parallel")), name=name,
    )(arr, arr, arr, arr, arr)


def _att_bwd(arr, offs, do, lse, dlse, *, nb, nbk, name):
    T = arr.shape[0]
    nbk = min(nbk, T // ATT_BLK)
    nsteps = T // (nbk * ATT_BLK)
    scale = HEAD_DIM ** -0.5
    dims_nt = (((1,), (1,)), ((), ()))
    dims_tn = (((0,), (0,)), ((), ()))

    def body(q_ref, k_ref, kp_ref, v_ref, vp_ref, do_ref, lse_ref, dlse_ref, dq_ref, dk_ref, dv_ref, ck, cv):
        step = pl.program_id(1)
        i = nsteps - 1 - step
        band, cur, first_head = _att_consts()

        @pl.when(step == 0)
        def _():
            ck[...] = jnp.zeros_like(ck)
            cv[...] = jnp.zeros_like(cv)

        carry_k, carry_v = ck[...], cv[...]
        for jj in reversed(range(nbk)):
            rs = slice(jj * ATT_BLK, (jj + 1) * ATT_BLK)
            ps = slice((jj - 1) * ATT_BLK, jj * ATT_BLK)
            has_prev = ((i * nbk + jj) & (nb - 1)) != 0
            valid = band & (cur | has_prev)
            kk = jnp.concatenate([kp_ref[...] if jj == 0 else k_ref[ps, :], k_ref[rs, :]], axis=0)
            vv = jnp.concatenate([vp_ref[...] if jj == 0 else v_ref[ps, :], v_ref[rs, :]], axis=0)
            q2 = _both_heads(q_ref[rs, :], first_head)
            do2 = _both_heads(do_ref[rs, :].astype(BF16), first_head)
            lse = jnp.concatenate([lse_ref[0, jj:jj + 1, :], lse_ref[1, jj:jj + 1, :]], axis=1)
            dlse = jnp.concatenate([dlse_ref[0, jj:jj + 1, :], dlse_ref[1, jj:jj + 1, :]], axis=1)
            st = lax.dot_general(kk, q2, dims_nt, preferred_element_type=F32) * scale
            st = jnp.where(jnp.concatenate([valid, valid], axis=1), st, MASK_VALUE)
            p = jnp.exp(st - lse)
            dp = lax.dot_general(vv, do2, dims_nt, preferred_element_type=F32)
            delta = jnp.sum(p * dp, axis=0, keepdims=True)
            dsb = (p * (dp - delta + dlse) * scale).astype(BF16)
            dq2 = lax.dot_general(dsb, kk, dims_tn, preferred_element_type=F32)
            dkk = jnp.dot(dsb, q2, preferred_element_type=F32)
            dvv = jnp.dot(p.astype(BF16), do2, preferred_element_type=F32)
            dq_ref[rs, :] = jnp.where(first_head, dq2[:ATT_BLK], dq2[ATT_BLK:]).astype(dq_ref.dtype)
            dk_ref[rs, :] = (dkk[ATT_BLK:] + carry_k).astype(dk_ref.dtype)
            dv_ref[rs, :] = (dvv[ATT_BLK:] + carry_v).astype(dv_ref.dtype)
            carry_k, carry_v = dkk[:ATT_BLK], dvv[:ATT_BLK]
        ck[...] = carry_k
        cv[...] = carry_v

    ins, out, stat = _att_specs(nbk, offs, nsteps, True)
    return pl.pallas_call(
        body, grid=(PAIRS, nsteps), in_specs=ins + [out, stat, stat], out_specs=[out] * 3,
        out_shape=[jax.ShapeDtypeStruct((T, ATT_W), BF16)] * 3,
        scratch_shapes=[pltpu.VMEM((ATT_BLK, 2 * HEAD_DIM), F32), pltpu.VMEM((ATT_BLK, 2 * HEAD_DIM), F32)],
        compiler_params=_cp(("arbitrary", "arbitrary")), name=name,
    )(arr, arr, arr, arr, arr, do, lse, dlse)


def _deinterleave(t, B, S, dil):
    if dil == 1:
        return t
    return t.reshape((B, S // dil, dil) + t.shape[1:]).swapaxes(1, 2).reshape(t.shape)


def _interleave(t, B, S, dil):
    if dil == 1:
        return t
    return t.reshape((B, dil, S // dil) + t.shape[1:]).swapaxes(1, 2).reshape(t.shape)


def _stats_to_tokens(lse, B, S, dil):
    return _interleave(lse.reshape(lse.shape[0], -1).T, B, S, dil)


def _stats_from_tokens(dl, B, S, dil):
    return _deinterleave(dl, B, S, dil).T.reshape(dl.shape[1], -1, ATT_BLK)


def _mesh_pos():
    return lax.axis_index("x"), lax.axis_index("y"), lax.axis_index("c")


def _allgather8(xs, *, name):
    m_per, n = xs.shape

    def body(x_ref, out_ref, send_sems, recv_sems, local_sem):
        x, y, c = _mesh_pos()
        me, sibling = (x, y, c), (x, y, 1 - c)
        chips = [(1 - x, y), (x, 1 - y), (1 - x, 1 - y)]

        def rows(px, py, pc):
            return out_ref.at[pl.ds((4 * px + 2 * py + pc) * m_per, m_per), :]

        def copy(k, block, to, src=None):
            return pltpu.make_async_remote_copy(
                src_ref=rows(*block) if src is None else src, dst_ref=rows(*block),
                send_sem=send_sems.at[k], recv_sem=recv_sems.at[k], device_id=to, device_id_type=MESH)

        mine = pltpu.make_async_copy(x_ref, rows(*me), local_sem)
        mine.start()
        first = [copy(0, me, sibling, src=x_ref)]
        first += [copy(1 + j, me, (*chip, c), src=x_ref) for j, chip in enumerate(chips)]
        for cp in first:
            cp.start()
        passed = [copy(4 + j, (*chip, c), sibling) for j, chip in enumerate(chips)]
        for j, chip in enumerate(chips):
            copy(1 + j, (*chip, c), me).wait_recv()
            passed[j].start()
        copy(0, sibling, me).wait_recv()
        for j, chip in enumerate(chips):
            copy(4 + j, (*chip, 1 - c), me).wait_recv()
        for cp in first + passed:
            cp.wait_send()
        mine.wait()

    return pl.pallas_call(
        body, out_shape=jax.ShapeDtypeStruct((8 * m_per, n), xs.dtype),
        in_specs=[pl.BlockSpec(memory_space=pltpu.VMEM)], out_specs=pl.BlockSpec(memory_space=pltpu.VMEM),
        scratch_shapes=[pltpu.SemaphoreType.DMA((7,)), pltpu.SemaphoreType.DMA((7,)), pltpu.SemaphoreType.DMA],
        compiler_params=pltpu.CompilerParams(vmem_limit_bytes=VMEM_LIMIT), name=name,
    )(xs)


def _hbm_call(body, arrays, out_shapes, n_sems, *, name):
    any_spec = pl.BlockSpec(memory_space=pl.ANY)
    return pl.pallas_call(
        body, out_shape=out_shapes, in_specs=[any_spec] * len(arrays), out_specs=[any_spec] * len(out_shapes),
        scratch_shapes=[pltpu.SemaphoreType.DMA((n_sems,)), pltpu.SemaphoreType.DMA((n_sems,))], name=name,
    )(*arrays)


def _other_chips(x, y):
    return [(1 - x, y), (x, 1 - y), (1 - x, 1 - y)]


def _split_start(srcs, lands, after, issue, n_sems, *, name):
    ns, nl = len(srcs), len(lands)
    hbm, sem = pl.BlockSpec(memory_space=pltpu.HBM), pl.BlockSpec(memory_space=pltpu.SEMAPHORE)
    extra = [] if after is None else [after]

    def body(*refs):
        n_in = ns + nl + len(extra)
        send_sems, recv_sems = refs[n_in], refs[n_in + 1]
        issue(refs[:ns], refs[ns:ns + nl], send_sems, recv_sems)
        refs[-1][...] = jnp.zeros_like(refs[-1])

    arrays = [pltpu.with_memory_space_constraint(a, pltpu.HBM) for a in list(srcs) + list(lands)]
    out = pl.pallas_call(
        body, name=name,
        out_shape=(pltpu.SemaphoreType.DMA((n_sems,)), pltpu.SemaphoreType.DMA((n_sems,)),
                   *[pltpu.HBM(a.shape, a.dtype) for a in arrays], jax.ShapeDtypeStruct((8, 128), F32)),
        in_specs=[hbm] * (ns + nl) + [pl.BlockSpec(memory_space=pl.ANY)] * len(extra),
        out_specs=(sem, sem, *[hbm] * (ns + nl), pl.BlockSpec(memory_space=pltpu.VMEM)),
        input_output_aliases={i: 2 + i for i in range(ns + nl)},
        compiler_params=pltpu.CompilerParams(has_side_effects=pltpu.SideEffectType.DATAFLOW_SIDE_EFFECTING),
    )(*arrays, *extra)
    return out[0], out[1], list(out[2:2 + ns]), list(out[2 + ns:2 + ns + nl]), out[-1]


def _split_wait(send_sems, recv_sems, srcs, lands, after, waits, *, name):
    ns, nl = len(srcs), len(lands)
    hbm, sem = pl.BlockSpec(memory_space=pltpu.HBM), pl.BlockSpec(memory_space=pltpu.SEMAPHORE)

    def body(*refs):
        waits(refs[:ns], refs[ns:ns + nl], refs[ns + nl], refs[ns + nl + 1])

    out = pl.pallas_call(
        body, name=name,
        out_shape=tuple(pltpu.HBM(a.shape, a.dtype) for a in list(srcs) + list(lands)),
        in_specs=[hbm] * (ns + nl) + [sem, sem, pl.BlockSpec(memory_space=pl.ANY)],
        out_specs=tuple([hbm] * (ns + nl)),
        input_output_aliases={i: i for i in range(ns + nl)},
        compiler_params=pltpu.CompilerParams(has_side_effects=pltpu.SideEffectType.DATAFLOW_SIDE_EFFECTING),
    )(*srcs, *lands, send_sems, recv_sems, after)
    return list(out[:ns]), list(out[ns:])


def _gather_start(halves, after, *, name):
    n = len(halves)
    lands = [lax.empty((N_CHIPS,) + h.shape, h.dtype) for h in halves]

    def issue(srcs, dsts, send_sems, recv_sems):
        x, y, c = _mesh_pos()
        me = 2 * x + y
        for a in range(n):
            for j, (px, py) in enumerate(_other_chips(x, y)):
                for cc in range(2):
                    pltpu.make_async_remote_copy(
                        src_ref=srcs[a].at[c], dst_ref=dsts[a].at[me, c],
                        send_sem=send_sems.at[6 * a + 2 * j + cc], recv_sem=recv_sems.at[6 * a + 2 * j + c],
                        device_id=(px, py, cc), device_id_type=MESH).start()

    return _split_start(halves, lands, after, issue, 6 * n, name=name)


def _gather_wait(started, after, *, name):
    send_sems, recv_sems, halves, lands = started
    n = len(halves)

    def waits(srcs, dsts, send_sems, recv_sems):
        x, y, c = _mesh_pos()
        me = 2 * x + y
        for a in range(n):
            for j, (px, py) in enumerate(_other_chips(x, y)):
                for cc in range(2):
                    pltpu.make_async_remote_copy(
                        src_ref=srcs[a].at[cc], dst_ref=dsts[a].at[2 * px + py, cc],
                        send_sem=send_sems.at[6 * a + 2 * j + cc], recv_sem=recv_sems.at[6 * a + 2 * j + cc],
                        device_id=(px, py, cc), device_id_type=MESH).wait_recv()
        for a in range(n):
            for j, (px, py) in enumerate(_other_chips(x, y)):
                for cc in range(2):
                    pltpu.make_async_remote_copy(
                        src_ref=srcs[a].at[c], dst_ref=dsts[a].at[me, c],
                        send_sem=send_sems.at[6 * a + 2 * j + cc], recv_sem=recv_sems.at[6 * a + 2 * j + c],
                        device_id=(px, py, cc), device_id_type=MESH).wait_send()

    return _split_wait(send_sems, recv_sems, halves, lands, after, waits, name=name)


def _reduce_plan_loops(plans, chip, c, fn):
    for a, plan in enumerate(plans):
        for h, cc in plan:
            for k in range(N_CHIPS):
                fn(a, h, k, cc, jnp.logical_or(chip != k, c != cc))


def _reduce_start(srcs, lands, plans, after, *, name):
    def issue(src_refs, land_refs, send_sems, recv_sems):
        x, y, c = _mesh_pos()
        chip = 2 * x + y
        my_id = 2 * chip + c

        def send(a, h, k, cc, is_other):
            @pl.when(is_other)
            def _():
                pltpu.make_async_remote_copy(
                    src_ref=src_refs[a].at[h, k], dst_ref=land_refs[a].at[my_id],
                    send_sem=send_sems.at[8 * a + 2 * k + cc], recv_sem=recv_sems.at[8 * a + my_id],
                    device_id=(k // 2, k % 2, cc), device_id_type=MESH).start()

        _reduce_plan_loops(plans, chip, c, send)

    return _split_start(srcs, lands, after, issue, 8 * len(srcs), name=name)


def _reduce_wait(started, plans, after, *, name):
    send_sems, recv_sems, srcs, lands = started

    def waits(src_refs, land_refs, send_sems, recv_sems):
        x, y, c = _mesh_pos()
        chip = 2 * x + y
        my_id = 2 * chip + c
        for a, plan in enumerate(plans):
            for h, cc in plan:
                for s in range(2 * N_CHIPS):
                    @pl.when(jnp.logical_and(c == cc, my_id != s))
                    def _(a=a, h=h, s=s):
                        pltpu.make_async_remote_copy(
                            src_ref=src_refs[a].at[h, 0], dst_ref=land_refs[a].at[s],
                            send_sem=send_sems.at[8 * a + s], recv_sem=recv_sems.at[8 * a + s],
                            device_id=(s // 4, (s // 2) % 2, s % 2), device_id_type=MESH).wait_recv()

        def sent(a, h, k, cc, is_other):
            @pl.when(is_other)
            def _():
                pltpu.make_async_remote_copy(
                    src_ref=src_refs[a].at[h, k], dst_ref=land_refs[a].at[my_id],
                    send_sem=send_sems.at[8 * a + 2 * k + cc], recv_sem=recv_sems.at[8 * a + my_id],
                    device_id=(k // 2, k % 2, cc), device_id_type=MESH).wait_send()

        _reduce_plan_loops(plans, chip, c, sent)

    return _split_wait(send_sems, recv_sems, srcs, lands, after, waits, name=name)


def _sum8(land, own, my_id, *, name):
    n_src, R, C = land.shape
    tr = _pick_rows(R, max(8, 1024 * 1024 // (2 * C)))

    def body(id_ref, *refs):
        own_ref, o_ref = refs[n_src], refs[n_src + 1]
        me = id_ref[0]
        acc = None
        for s in range(n_src):
            term = jnp.where(me == s, own_ref[...], refs[s][...]).astype(F32)
            acc = term if acc is None else acc + term
        o_ref[...] = acc

    return pl.pallas_call(
        body, out_shape=jax.ShapeDtypeStruct((R, C), F32),
        grid_spec=pltpu.PrefetchScalarGridSpec(
            num_scalar_prefetch=1, grid=(R // tr,),
            in_specs=[pl.BlockSpec((None, tr, C), lambda i, idr, s=s: (s, i, 0)) for s in range(n_src)]
            + [pl.BlockSpec((tr, C), lambda i, idr: (i, 0))],
            out_specs=pl.BlockSpec((tr, C), lambda i, idr: (i, 0))),
        compiler_params=_cp(("parallel",)), name=name,
    )(my_id.reshape(1).astype(jnp.int32), *([land] * n_src), own)


def _share_halves(ts, *, name):
    n = len(ts)

    def body(*refs):
        ins, outs, (send_sems, recv_sems) = refs[:n], refs[n:2 * n], refs[2 * n:]
        x, y, c = _mesh_pos()
        cps = [pltpu.make_async_remote_copy(
            src_ref=ins[a], dst_ref=outs[a].at[c], send_sem=send_sems.at[a], recv_sem=recv_sems.at[a],
            device_id=(x, y, 1 - c), device_id_type=MESH) for a in range(n)]
        for cp in cps:
            cp.start()
        for a in range(n):
            pltpu.make_async_remote_copy(
                src_ref=ins[a], dst_ref=outs[a].at[1 - c], send_sem=send_sems.at[a], recv_sem=recv_sems.at[a],
                device_id=(x, y, 1 - c), device_id_type=MESH).wait_recv()
        for cp in cps:
            cp.wait_send()

    return _hbm_call(body, ts, [jax.ShapeDtypeStruct((2,) + t.shape, t.dtype) for t in ts], n, name=name)


TR = 512
ATT_BLOCKS_PER_STEP = 16
S5_CHUNK = 256


def _rms_fwd(x, g, name):
    return _rowwise(_f_rms, [x], [g], (BF16,), tr=TR, name=name)[0]


def _rms_bwd_epi(dh, x, g, gx):
    r = lax.rsqrt(jnp.mean(x * x, axis=-1, keepdims=True) + EPS)
    xr = x * r
    t = dh * g
    dx = r * (t - xr * jnp.mean(t * xr, axis=-1, keepdims=True)) + gx
    return dx, dx, jnp.sum(dh * xr, axis=0, keepdims=True)


def _mm_rms_bwd(a, w, x, g, gx, *, after=None, name, **kw):
    kw.setdefault("tm", 1024)
    return _mm(a, w, tb=True, epi=_rms_bwd_epi, extras=(x, g, gx), out_dtypes=(F32, BF16, F32), n_row_sums=1,
               tn=x.shape[1], after=after, name=name, **kw)


def _rms_bwd(x, g, dh, gx, name, after=None):
    (dx, dxb), (dg,) = _rowwise_vjp(_f_rms, [x], [g], [dh], [(F32, BF16)], adds={0: gx}, after=after, tr=TR,
                                    name=name)
    return dx, dxb, dg


def _grad_cols(M, Nq):
    def imap(tm, tn):
        hp, per = (M // 2) // tm, Nq // tn
        assert hp * tm * 2 == M and per * tn == Nq, (M, Nq, tm, tn)
        return lambda i, j, k: (i // hp, j // per, i % hp, j % per)
    return (2, N_CHIPS, M // 2, Nq), lambda tm, tn: (None, None, tm, tn), imap, None, M // 2, Nq


def _grad_rows(Mq, N):
    def imap(tm, tn):
        po, hp = Mq // tm, (Mq // 2) // tm
        assert hp * tm * 2 == Mq, (Mq, tm)
        return lambda i, j, k: ((i % po) // hp, i // po, (i % po) % hp, j)
    return (2, N_CHIPS, Mq // 2, N), lambda tm, tn: (None, None, tm, tn), imap, None, Mq // 2, N


def _grad_layer_cols(slot, lh, M, Nq, buf):
    def imap(tm, tn):
        per = Nq // tn
        return lambda i, j, k: (j // per, slot, i, j % per)
    return (N_CHIPS, lh, M, Nq), lambda tm, tn: (None, None, tm, tn), imap, buf, M, Nq


def _grad_layer_rows(slot, lh, Mq, N, buf):
    def imap(tm, tn):
        po = Mq // tm
        return lambda i, j, k: (i // po, slot, i % po, j)
    return (N_CHIPS, lh, Mq, N), lambda tm, tn: (None, None, tm, tn), imap, buf, Mq, N


def _add_then_rms(acc, res, g):
    xo = acc + res
    return xo, _f_rms(xo, g)[0]


def _mlp_fwd(x, h2, w_in, w_out, g_next, li):
    w_in_full = getattr(w_in, "plain", w_in)
    r = _mm(h2, w_in_full, out_dtypes=(BF16,), epi=lambda acc: (jnp.maximum(acc, 0.0),), tm=512, tn=w_in.shape[1],
            name=f"mlp_in_{li}")
    tiles = dict(tm=512, tn=x.shape[1], tk=r.shape[1])
    if g_next is None:
        x_out, h_next = _mm(r, w_out, pro_a=lambda t: t * t, epi=lambda acc, res: (acc + res,), extras=(x,),
                            name=f"mlp_out_{li}", **tiles), None
    else:
        x_out, h_next = _mm(r, w_out, pro_a=lambda t: t * t, epi=_add_then_rms, extras=(x, g_next),
                            out_dtypes=(F32, BF16), name=f"mlp_out_{li}", **tiles)
    return x_out, h_next, (h2, r)


def _mlp_bwd(gx, gxb, x, g, w_in, w_out, saved, li, nl, bufs):
    h2, r = saved
    D, F = w_in.shape
    lh = nl // 2
    da = _mm(gxb, w_out, tb=True, out_dtypes=(BF16,),
             epi=lambda acc, rt: (acc * 2.0 * rt.astype(F32),), extras=(r,), tm=512, tn=F, name=f"mlp_dact_{li}")
    buf_in, buf_out = bufs if bufs is not None else (None, None)
    d_w_out = _mm(r, gxb, ta=True, pro_a=lambda t: t * t, tm=1024, tn=1024, tk=4096, out_dtypes=(BF16,),
                  out=_grad_layer_rows(li % lh, lh, F // N_CHIPS, D, buf_out), name=f"mlp_dwout_{li}")
    d_w_in = _mm(h2, da, ta=True, tm=1024, tn=1024, tk=4096, out_dtypes=(BF16,),
                 out=_grad_layer_cols(li % lh, lh, D, F // N_CHIPS, buf_in), name=f"mlp_dwin_{li}")
    gx_mid, gxb_mid, dg = _mm_rms_bwd(da, getattr(w_in, "plain", w_in), x, g, gx, tm=512, tk=F, name=f"mlp_dh_{li}")
    return gx_mid, gxb_mid, dg, (d_w_in, d_w_out)


def _local_step(x3, tgt3, p, layer_weights, token=None, grads_done=lambda group: None):
    B, S, D = x3.shape
    T = B * S
    x = x3.reshape(T, D)
    grads = {}
    row = lambda v: v.reshape(1, -1)
    p = dict(p)
    nl = p["norm_mlp"].shape[0]
    mlp_in, mlp_out = [None] * nl, [None] * nl

    def fetch(li, after):
        wl = dict(layer_weights(li, after))
        mlp_in[li], mlp_out[li] = wl.pop("mlp_w_in"), wl.pop("mlp_w_out")
        p.update(wl)

    g0 = row(p["norm_mix"][0])
    if token is not None:
        g0 = g0 + token[:1, :1]
    h0 = _rms_fwd(x, g0, "rms_mix_0")
    s5_args = (p["ssm_a_re"][0], p["ssm_a_im"][0], p["ssm_b_re"][0], p["ssm_b_im"][0],
               p["ssm_c_re"][0], p["ssm_c_im"][0], p["ssm_log_dt"][0])
    s5_exp, s5_vjp = jax.vjp(_s5_prep, *s5_args)
    abr, abi, bre, bim, cre, cim = s5_exp
    bre_b, bim_b, cre_b, cim_b = (t.astype(BF16) for t in (bre, bim, cre, cim))
    d_skip = p["ssm_d"]
    ypre, yb, sxr, sxi, ser, sei = _s5_fwd(h0, abr, abi, bre_b, bim_b, cre_b, cim_b, d_skip, B=B, L=S5_CHUNK,
                                           name="s5_fwd")
    fetch(0, yb)
    w_glu = p["ssm_w_glu"]
    z0 = _mm(yb, w_glu, tm=2048, name="s5_glu_mm")
    gm = [row(p["norm_mlp"][i]) for i in range(nl)]
    g1, g2, g3 = (row(p["norm_mix"][i]) for i in range(1, nl))
    x_mid0, hm0 = _rowwise(lambda z, xr, g: _add_then_rms(_f_glu(z)[0], xr, g), [z0, x], [gm[0]], (F32, BF16),
                           tr=TR, name="s5_glu")
    x1, h1, mlp_saved0 = _mlp_fwd(x_mid0, hm0, mlp_in[0], mlp_out[0], g1, 0)

    fetch(1, h1)
    z1 = _mm(h1, p["conv_w_pw1"], tm=2048, name="conv_pw1")
    zg = _rowwise(_f_bias_glu, [z1], [p["conv_b_pw1"]], (F32,), tr=TR, name="conv_glu")[0]
    zp = jnp.pad(zg.reshape(B, S, D), ((0, 0), (CONV_HALO, 0), (0, 0)))
    w_dw = jnp.pad(p["conv_w_dw"], ((0, 32 - CONV_WIDTH), (0, 0)))
    yc = _conv_fwd(zp, w_dw, R=256, tc=128, name="conv_dw").reshape(T, D)
    ln_par = [p["conv_b_dw"], p["conv_ln_g"], p["conv_ln_b"]]
    qc = _rowwise(_f_ln_silu, [yc], ln_par, (BF16,), tr=TR, name="conv_ln_silu")[0]
    x_mid1, hm1 = _mm(qc, p["conv_w_pw2"], epi=lambda acc, bias, res, g: _add_then_rms(acc + bias, res, g),
                      extras=(p["conv_b_pw2"], x1, gm[1]), out_dtypes=(F32, BF16), tn=D, name="conv_pw2")
    x2, h2, mlp_saved1 = _mlp_fwd(x_mid1, hm1, mlp_in[1], mlp_out[1], g2, 1)

    fetch(2, h2)
    z2 = _mm(h2, p["gmlp_w_in"], tm=2048, name="gmlp_in")
    gl_par = [p["gmlp_ln_g"], p["gmlp_ln_b"]]
    gu, gvn = _rowwise(_f_gelu_ln, [z2], gl_par, (F32, F32), tr=TR, name="gmlp_gelu_ln")
    causal = jnp.tril(jnp.ones((GMLP_CHUNK, GMLP_CHUNK), dtype=bool))
    ws_b = jnp.where(causal[None], p["gmlp_w_s"][0], 0.0).astype(BF16)
    bcol = jnp.pad(p["gmlp_b_s"][0].T, ((0, 0), (0, 128 - GMLP_HEADS)))
    uv = _gmlp_fwd(gu, gvn, ws_b, bcol, nck=4, name="gmlp_spatial")
    x_mid2, hm2 = _mm(uv, p["gmlp_w_out"], epi=_add_then_rms, extras=(x2, gm[2]), out_dtypes=(F32, BF16), tn=D,
                      name="gmlp_out")
    x3_, h3, mlp_saved2 = _mlp_fwd(x_mid2, hm2, mlp_in[2], mlp_out[2], g3, 2)

    fetch(3, h3)
    ng = len(ATT_DILS)
    att_in, o_tok, l_tok, lses = [], [], [], []
    offs = (0, PAIRS, 2 * PAIRS)
    for gi, dil in enumerate(ATT_DILS):
        w_g = _ColBlocks(p["attn_w_qkv_plain"], gi, ng, 3, ATT_W)
        arr = _mm(h3, w_g, out_dtypes=(BF16,), tm=2048, tn=ATT_W, name=f"attn_qkv_{gi}")
        arr = _deinterleave(arr, B, S, dil)
        att_in.append((arr, offs))
        og, lg = _att_fwd(arr, offs, nb=S // dil // ATT_BLK, nbk=ATT_BLOCKS_PER_STEP, name=f"attn_fwd_{gi}")
        lses.append(lg)
        o_tok.append(_interleave(og, B, S, dil))
        l_tok.append(_stats_to_tokens(lg, B, S, dil))
    merged2 = _rowwise(_f_merge, o_tok + l_tok, [], (BF16,), tr=TR, name="attn_merge")[0]
    x_mid3, hm3 = _mm(merged2, p["attn_w_o_plain"], epi=_add_then_rms, extras=(x3_, gm[3]), out_dtypes=(F32, BF16),
                      tn=D, name="attn_out")
    x4, _, mlp_saved3 = _mlp_fwd(x_mid3, hm3, mlp_in[3], mlp_out[3], None, 3)

    loss_part, gx, gxb, dgf = _loss_head(x4, tgt3.reshape(T, D), row(p["norm_final"]), tr=TR, name="loss_head")
    grads["norm_final"] = dgf.reshape(-1)
    d_norm_mix, d_norm_mlp = [None] * 4, [None] * 4
    Dq = D // N_CHIPS

    gx, gxb, d_norm_mlp[3], mlp_hi = _mlp_bwd(
        gx, gxb, x_mid3, row(p["norm_mlp"][3]), mlp_in[3], mlp_out[3], mlp_saved3, 3, nl, None)
    dmerged = _mm(gxb, p["attn_w_o"], tb=True, name="attn_dmerged")
    grads["attn_w_o"] = _mm(merged2, gxb, ta=True, tm=256, tn=256, tk=4096, out_dtypes=(BF16,), out=_grad_cols(ATT_W, Dq), name="attn_dwo")
    dml, _ = _rowwise_vjp(_f_merge, o_tok + l_tok, [], [dmerged], [F32] * 6, tr=TR, name="attn_merge_bwd")
    pieces = [[None] * ng for _ in range(3)]
    for gi, dil in enumerate(ATT_DILS):
        arr, offs = att_in[gi]
        dqkv_g = _att_bwd(arr, offs, _deinterleave(dml[gi], B, S, dil), lses[gi],
                          _stats_from_tokens(dml[ng + gi], B, S, dil), nb=S // dil // ATT_BLK,
                          nbk=ATT_BLOCKS_PER_STEP,
                          name=f"attn_bwd_{gi}")
        for i in range(3):
            pieces[i][gi] = _interleave(dqkv_g[i], B, S, dil)
    dqkv = jnp.concatenate([pieces[i][gi] for i in range(3) for gi in range(ng)], axis=1)
    qkv_w = 3 * ng * ATT_W
    grads["attn_w_qkv"] = _mm(h3, dqkv, ta=True, tm=512, tn=1152, tk=4096, out_dtypes=(BF16,), out=_grad_cols(D, qkv_w // N_CHIPS),
                              name="attn_dwqkv")
    tok = grads_done({n: grads[n] for n in ("attn_w_qkv", "attn_w_o")})
    gx, gxb, d_norm_mix[3] = _mm_rms_bwd(dqkv, p["attn_w_qkv_plain"], x3_, g3, gx, tm=512, tk=qkv_w, after=tok, name="attn_dh")

    gx, gxb, d_norm_mlp[2], mlp_hi = _mlp_bwd(
        gx, gxb, x_mid2, row(p["norm_mlp"][2]), mlp_in[2], mlp_out[2], mlp_saved2, 2, nl, mlp_hi)
    tok = grads_done({"mlp_w_in": (1, mlp_hi[0]), "mlp_w_out": (1, mlp_hi[1])})
    duv = _mm(gxb, p["gmlp_w_out"], tb=True, after=tok, name="gmlp_duv")
    grads["gmlp_w_out"] = _mm(uv, gxb, ta=True, tm=128, tn=1024, tk=4096, out_dtypes=(BF16,), out=_grad_rows(Dq, D), name="gmlp_dwout")
    du, dvn, dws, dbcol = _gmlp_bwd(duv, gu, gvn, ws_b, bcol, nck=4, name="gmlp_spatial_bwd")
    grads["gmlp_w_s"] = jnp.where(causal[None], dws, 0.0)[None]
    grads["gmlp_b_s"] = dbcol[:, :GMLP_HEADS].T[None]
    (dz2,), (dlg, dlb_) = _rowwise_vjp(_f_gelu_ln, [z2], gl_par, [du, dvn], [BF16], tr=TR, name="gmlp_gelu_ln_bwd")
    grads["gmlp_ln_g"], grads["gmlp_ln_b"] = dlg, dlb_
    grads["gmlp_w_in"] = _mm(h2, dz2, ta=True, tm=512, tn=512, tk=4096, out_dtypes=(BF16,), out=_grad_cols(D, 2 * Dq), name="gmlp_dwin")
    tok = grads_done({n: grads[n] for n in ("gmlp_w_in", "gmlp_w_out")})
    gx, gxb, d_norm_mix[2] = _mm_rms_bwd(dz2, getattr(p["gmlp_w_in"], "plain", p["gmlp_w_in"]), x2, g2, gx, tm=512, tk=2 * D, after=tok,
                                         name="gmlp_dh")

    gx, gxb, d_norm_mlp[1], mlp_lo = _mlp_bwd(
        gx, gxb, x_mid1, row(p["norm_mlp"][1]), mlp_in[1], mlp_out[1], mlp_saved1, 1, nl, None)
    dqc = _mm(gxb, p["conv_w_pw2"], tb=True, name="conv_dq")
    grads["conv_w_pw2"] = _mm(qc, gxb, ta=True, tm=128, tn=1024, tk=4096, out_dtypes=(BF16,), out=_grad_rows(Dq, D), name="conv_dwpw2")
    _, (db2,) = _rowwise_vjp(lambda t, b: (t + b,), [gx], [p["conv_b_pw2"]], [gx], [None], tr=TR, name="conv_db2")
    grads["conv_b_pw2"] = db2
    (dyc,), (dbdw, dcg, dcb) = _rowwise_vjp(_f_ln_silu, [yc], ln_par, [dqc], [F32], tr=TR, name="conv_ln_silu_bwd")
    grads["conv_b_dw"], grads["conv_ln_g"], grads["conv_ln_b"] = dbdw, dcg, dcb
    dyp = jnp.pad(dyc.reshape(B, S, D), ((0, 0), (0, CONV_HALO), (0, 0)))
    dzg, dwdw = _conv_bwd(zp, dyp, w_dw, R=256, tc=128, name="conv_dw_bwd")
    grads["conv_w_dw"] = dwdw[:CONV_WIDTH][None]
    (dz1,), (db1,) = _rowwise_vjp(_f_bias_glu, [z1], [p["conv_b_pw1"]], [dzg.reshape(T, D)], [BF16], tr=TR,
                                  name="conv_glu_bwd")
    grads["conv_b_pw1"] = db1
    grads["conv_w_pw1"] = _mm(h1, dz1, ta=True, tm=512, tn=512, tk=4096, out_dtypes=(BF16,), out=_grad_cols(D, 2 * Dq), name="conv_dwpw1")
    tok = grads_done({n: grads[n] for n in ("conv_w_pw1", "conv_w_pw2")})
    gx, gxb, d_norm_mix[1] = _mm_rms_bwd(dz1, getattr(p["conv_w_pw1"], "plain", p["conv_w_pw1"]), x1, g1, gx, tm=512, tk=2 * D,
                                         after=tok, name="conv_dh")

    gx, gxb, d_norm_mlp[0], mlp_lo = _mlp_bwd(
        gx, gxb, x_mid0, row(p["norm_mlp"][0]), mlp_in[0], mlp_out[0], mlp_saved0, 0, nl, mlp_lo)
    tok = grads_done({"mlp_w_in": (0, mlp_lo[0]), "mlp_w_out": (0, mlp_lo[1])})
    (dz0,), _ = _rowwise_vjp(_f_glu, [z0], [], [gx], [BF16], after=tok, tr=TR, name="s5_glu_bwd")
    grads["ssm_w_glu"] = _mm(yb, dz0, ta=True, tm=512, tn=512, tk=4096, out_dtypes=(BF16,), out=_grad_cols(D, 2 * Dq), name="s5_dwglu")
    tok = grads_done({"ssm_w_glu": grads["ssm_w_glu"]})
    dypre = _mm(dz0, getattr(w_glu, "plain", w_glu), tb=True, tm=512, tk=2 * D,
                epi=lambda acc, yp: (jax.vjp(lambda t: jax.nn.gelu(t), yp)[1](acc)[0],), extras=(ypre,), name="s5_dypre")
    dh0, dbre, dbim, dcre, dcim, dabr, dabi, dd = _s5_bwd(
        dypre, h0, sxr, sxi, ser, sei, abr, abi, bre_b, bim_b, cre_b, cim_b, d_skip, B=B, L=S5_CHUNK, name="s5_bwd")
    s5_grads = s5_vjp((dabr, dabi, dbre, dbim, dcre, dcim))
    for nm, gv in zip(("ssm_a_re", "ssm_a_im", "ssm_b_re", "ssm_b_im", "ssm_c_re", "ssm_c_im", "ssm_log_dt"), s5_grads):
        grads[nm] = gv[None]
    grads["ssm_d"] = dd
    gx, _, d_norm_mix[0] = _rms_bwd(x, g0, dh0, gx, "rms_mix_bwd_0", after=tok)

    grads["norm_mix"] = jnp.concatenate(d_norm_mix, axis=0)
    grads["norm_mlp"] = jnp.concatenate(d_norm_mlp, axis=0)
    grads["mlp_w_in"], grads["mlp_w_out"] = (mlp_lo[0], mlp_hi[0]), (mlp_lo[1], mlp_hi[1])
    return loss_part, gx.reshape(B, S, D), grads


WEIGHTS = ['norm_mix', 'norm_mlp', 'norm_final', 'ssm_a_re', 'ssm_a_im', 'ssm_b_re', 'ssm_b_im', 'ssm_c_re',
           'ssm_c_im', 'ssm_d', 'ssm_log_dt', 'ssm_w_glu', 'conv_w_pw1', 'conv_b_pw1', 'conv_w_dw', 'conv_b_dw',
           'conv_ln_g', 'conv_ln_b', 'conv_w_pw2', 'conv_b_pw2', 'gmlp_w_in', 'gmlp_ln_g', 'gmlp_ln_b', 'gmlp_w_s',
           'gmlp_b_s', 'gmlp_w_out', 'attn_w_qkv', 'attn_w_o', 'mlp_w_in', 'mlp_w_out']
BIG_AXIS = {'ssm_w_glu': -1, 'conv_w_pw1': -1, 'conv_w_pw2': -2, 'gmlp_w_in': -1, 'gmlp_w_out': -2,
            'attn_w_qkv': -1, 'attn_w_o': -1, 'mlp_w_in': -1, 'mlp_w_out': -2}
BIG = list(BIG_AXIS)
LAYER_MIXER_WEIGHTS = (('ssm_w_glu',), ('conv_w_pw1', 'conv_w_pw2'), ('gmlp_w_in', 'gmlp_w_out'), ('attn_w_qkv', 'attn_w_o'))
SMALL_SHARDED = ['conv_b_pw1', 'conv_w_dw', 'conv_b_dw', 'conv_ln_g', 'conv_ln_b', 'conv_b_pw2', 'gmlp_ln_g', 'gmlp_ln_b']
SMALL_REPL = [n for n in WEIGHTS if n not in BIG_AXIS and n not in SMALL_SHARDED]
SMALL = SMALL_REPL + SMALL_SHARDED
LANES = 128


def _pack(arrs, cols, row_mult):
    flat = jnp.concatenate([a.reshape(-1) for a in arrs])
    per = cols * row_mult
    n = -(-flat.shape[0] // per) * per
    return jnp.pad(flat, (0, n - flat.shape[0])).reshape(n // cols, cols)


def _unpack(flat2d, shapes):
    flat = flat2d.reshape(-1)
    out, off = [], 0
    for s in shapes:
        n = int(np.prod(s))
        out.append(flat[off:off + n].reshape(s))
        off += n
    return out


def kernel(x, norm_mix, norm_mlp, norm_final, ssm_a_re, ssm_a_im, ssm_b_re, ssm_b_im, ssm_c_re, ssm_c_im, ssm_d, ssm_log_dt, ssm_w_glu, conv_w_pw1, conv_b_pw1, conv_w_dw, conv_b_dw, conv_ln_g, conv_ln_b, conv_w_pw2, conv_b_pw2, gmlp_w_in, gmlp_ln_g, gmlp_ln_b, gmlp_w_s, gmlp_b_s, gmlp_w_out, attn_w_qkv, attn_w_o, mlp_w_in, mlp_w_out, loss_target, m_norm_mix, m_norm_mlp, m_norm_final, m_ssm_a_re, m_ssm_a_im, m_ssm_b_re, m_ssm_b_im, m_ssm_c_re, m_ssm_c_im, m_ssm_d, m_ssm_log_dt, m_ssm_w_glu, m_conv_w_pw1, m_conv_b_pw1, m_conv_w_dw, m_conv_b_dw, m_conv_ln_g, m_conv_ln_b, m_conv_w_pw2, m_conv_b_pw2, m_gmlp_w_in, m_gmlp_ln_g, m_gmlp_ln_b, m_gmlp_w_s, m_gmlp_b_s, m_gmlp_w_out, m_attn_w_qkv, m_attn_w_o, m_mlp_w_in, m_mlp_w_out, v_norm_mix, v_norm_mlp, v_norm_final, v_ssm_a_re, v_ssm_a_im, v_ssm_b_re, v_ssm_b_im, v_ssm_c_re, v_ssm_c_im, v_ssm_d, v_ssm_log_dt, v_ssm_w_glu, v_conv_w_pw1, v_conv_b_pw1, v_conv_w_dw, v_conv_b_dw, v_conv_ln_g, v_conv_ln_b, v_conv_w_pw2, v_conv_b_pw2, v_gmlp_w_in, v_gmlp_ln_g, v_gmlp_ln_b, v_gmlp_w_s, v_gmlp_b_s, v_gmlp_w_out, v_attn_w_qkv, v_attn_w_o, v_mlp_w_in, v_mlp_w_out):
    args = dict(locals())
    w = {n: args[n] for n in WEIGHTS}
    m = {n: args["m_" + n] for n in WEIGHTS}
    v = {n: args["v_" + n] for n in WEIGHTS}
    chip = 2 * lax.axis_index("x") + lax.axis_index("y")
    core = lax.axis_index("c")

    sm_shapes = [w[n].shape for n in SMALL_SHARDED]
    sflat = _pack([w[n] for n in SMALL_SHARDED], LANES, 8)
    rs = sflat.shape[0]
    sall = _allgather8(sflat, name="gather_small")
    started, token = [], sall
    sall = sall.reshape(8, rs, LANES)
    for li, mixer in enumerate(LAYER_MIXER_WEIGHTS):
        names = list(mixer) + ["mlp_w_in", "mlp_w_out"]
        shards = [w[n][0] for n in mixer] + [w["mlp_w_in"][li], w["mlp_w_out"][li]]
        halves = [s.astype(BF16).reshape((2, s.shape[0] // 2) + s.shape[1:]) for s in shards]
        send_sems, recv_sems, halves, lands, token = _gather_start(halves, token, name=f"gather_start_{li}")
        started.append((names, (send_sems, recv_sems, halves, lands)))

    def layer_weights(li, after):
        names, st = started[li]
        halves, lands = _gather_wait(st, after, name=f"gather_wait_{li}")
        out = {}
        for n, h, arr in zip(names, halves, lands):
            arr = lax.dynamic_update_index_in_dim(arr, h, chip, axis=0)
            arr = arr.reshape((N_CHIPS, arr.shape[1] * arr.shape[2]) + arr.shape[3:])
            if BIG_AXIS[n] == -1:
                out[n] = _Stored(arr, "cols")
                if n in ("attn_w_qkv", "attn_w_o"):
                    out[n + "_plain"] = jnp.swapaxes(arr, 0, 1).reshape(arr.shape[1], -1)
                out[n].plain = jnp.swapaxes(arr, 0, 1).reshape(arr.shape[1], -1)
            else:
                out[n] = arr.reshape(-1, arr.shape[-1])
        return out

    p = {}
    per_chip = [_unpack(sall[2 * k], sm_shapes) for k in range(N_CHIPS)]
    for i, n in enumerate(SMALL_SHARDED):
        p[n] = jnp.concatenate([per_chip[k][i] for k in range(N_CHIPS)], axis=-1)
    for n in SMALL_REPL:
        p[n] = w[n]
    p['conv_w_dw'] = p['conv_w_dw'][0]

    in_flight, arrived, n_rounds = [], {}, [0]

    def finish_round(after):
        k, names, plans, st = in_flight.pop(0)
        srcs, lands = _reduce_wait(st, plans, after, name=f"grads_wait_{k}")
        for n, plan, src, land in zip(names, plans, srcs, lands):
            arrived.setdefault(n, []).append((plan, src, land))

    def grads_done(group):
        names, srcs, plans, lands = [], [], [], []
        for n, v in group.items():
            if isinstance(v, tuple):
                src, plan = v[1].reshape(1, N_CHIPS, -1, v[1].shape[-1]), ((0, v[0]),)
                while any(n in rd[1] for rd in in_flight):
                    finish_round(src)
            else:
                src, plan = v.reshape(2, N_CHIPS, -1, v.shape[-1]), ((0, 0), (1, 1))
            land = arrived[n][-1][2] if n in arrived else lax.empty((2 * N_CHIPS,) + src.shape[2:], BF16)
            names.append(n), srcs.append(src), plans.append(plan), lands.append(land)
        st = _reduce_start(srcs, lands, plans, None, name=f"grads_start_{n_rounds[0]}")
        in_flight.append((n_rounds[0], names, plans, st[:4]))
        n_rounds[0] += 1
        return st[4]

    loss_part, grad_x, g = _local_step(x, loss_target, p, layer_weights, token, grads_done)
    loss = lax.psum(loss_part[0, 0], ("x", "y", "c"))

    my_id = 2 * chip + core
    while in_flight:
        finish_round(grad_x)
    totals = []
    for n in BIG:
        own = None
        for plan, src, land in arrived[n]:
            slab = lax.dynamic_index_in_dim(src, chip, axis=1, keepdims=False)
            if len(plan) == 2:
                own = lax.dynamic_index_in_dim(slab, core, axis=0, keepdims=False)
            else:
                own = slab[0] if own is None else jnp.where(core == plan[0][1], slab[0], own)
        totals.append(_sum8(arrived[n][-1][2], own, my_id, name="owner_sum_" + n))
    shared = _share_halves(totals, name="grads_share_halves")
    big_grads = {}
    for n, arr, t in zip(BIG, shared, totals):
        arr = lax.dynamic_update_index_in_dim(arr, t[None], core, axis=0)
        big_grads[n] = arr.reshape(w[n].shape)

    small_full_shapes = [g[n].shape for n in SMALL]
    gs = _pack([g[n] for n in SMALL], LANES, 8)
    rg = gs.shape[0]
    gs_all = _allgather8(gs, name="gather_small_grads").reshape(8, rg, LANES)
    gs_sum = _rowwise(lambda *a: (functools.reduce(lambda s, t: s + t, a),), [gs_all[k] for k in range(8)], [], (F32,),
                      tr=rg, name="small_grads_sum")[0]
    small_grads = dict(zip(SMALL, _unpack(gs_sum, small_full_shapes)))
    for n in SMALL:
        small_grads[n] = small_grads[n].reshape(p_shape_full(w[n], -1 if n in SMALL_SHARDED else None))
    for n in SMALL_SHARDED:
        width = w[n].shape[-1]
        small_grads[n] = lax.dynamic_slice_in_dim(small_grads[n], chip * width, width, axis=-1)

    grad, delta, new_m, new_v = {}, {}, {}, {}
    for n in BIG:
        shape = w[n].shape
        two_d = lambda t: t.reshape(-1, shape[-1])
        grad[n] = big_grads[n]
        d_, m_, v_ = _adamw(two_d(w[n]), two_d(grad[n]), two_d(m[n]), two_d(v[n]), name="adamw_" + n)
        delta[n], new_m[n], new_v[n] = d_.reshape(shape), m_.reshape(shape), v_.reshape(shape)
    for n in SMALL:
        shape = w[n].shape
        two_d = lambda t: t.reshape(-1, shape[-1])
        grad[n] = small_grads[n]
        d_, m_, v_ = _adamw(two_d(w[n]), two_d(grad[n]), two_d(m[n]), two_d(v[n]), name="adamw_" + n)
        delta[n], new_m[n], new_v[n] = d_.reshape(shape), m_.reshape(shape), v_.reshape(shape)

    return (loss, grad_x, *[grad[n] for n in WEIGHTS], *[delta[n] for n in WEIGHTS],
            *[new_m[n] for n in WEIGHTS], *[new_v[n] for n in WEIGHTS])


def p_shape_full(shard, axis):
    s = list(shard.shape)
    if axis is not None:
        s[axis] *= N_CHIPS
    return tuple(s)
```

```python
import functools
import math

import jax
import jax.numpy as jnp
import numpy as np
from jax import lax
from jax.experimental import pallas as pl
from jax.experimental.pallas import tpu as pltpu

F32 = jnp.float32
BF16 = jnp.bfloat16
MESH = pl.DeviceIdType.MESH

EPS = 1e-6
SSM_GROUP = 16
SSM_STATE = 64
CONV_WIDTH = 31
CONV_HALO = 32
GMLP_CHUNK = 128
GMLP_HEADS = 4
ATT_DILS = (1, 4, 16)
ATT_BLK = 128
ATT_HEADS = 8
HEAD_DIM = 64
ATT_W = ATT_HEADS * HEAD_DIM
N_CHIPS = 4
ADAM_LR, ADAM_B1, ADAM_B2, ADAM_EPS, ADAM_WD, ADAM_STEP = 1e-3, 0.9, 0.999, 1e-8, 0.01, 10

VMEM_BYTES_V7X = 64 * 1024 * 1024
VMEM_LIMIT = VMEM_BYTES_V7X - 8 * 1024 * 1024
MASK_VALUE = -1e30
LANE_TILE = 128


def _cp(sem=None):
    return pltpu.CompilerParams(dimension_semantics=sem, vmem_limit_bytes=VMEM_LIMIT)


def _pick_tile(total, target):
    for cand in range(min(target, total) // LANE_TILE * LANE_TILE, 0, -LANE_TILE):
        if total % cand == 0:
            return cand
    return total


class _Stored:
    def __init__(self, arr, kind="plain", lead=()):
        self.arr, self.kind, self.lead = arr, kind, tuple(lead)
        r, c = arr.shape[-2:]
        self.shape = (r, c * N_CHIPS) if kind == "cols" else (r * N_CHIPS, c) if kind == "rows" else (r, c)

    def spec(self, br, bc, rc_of):
        lead, nl = self.lead, len(self.lead)
        if self.kind == "plain":
            return pl.BlockSpec((None,) * nl + (br, bc), lambda i, j, k: (*lead, *rc_of(i, j, k)))
        if self.kind == "cols":
            per = self.arr.shape[-1] // bc
            assert per * bc == self.arr.shape[-1]

            def imap(i, j, k):
                r, c = rc_of(i, j, k)
                return (c // per, *lead, r, c % per)
        else:
            per = self.arr.shape[-2] // br
            assert per * br == self.arr.shape[-2]

            def imap(i, j, k):
                r, c = rc_of(i, j, k)
                return (r // per, *lead, r % per, c)
        return pl.BlockSpec((None,) * (nl + 1) + (br, bc), imap)


class _ColBlocks:
    kind = "colblocks"

    def __init__(self, arr, first, stride, count, width):
        self.arr, self.first, self.stride, self.width = arr, first, stride, width
        self.shape = (arr.shape[0], count * width)

    def spec(self, br, bc, rc_of):
        per = self.width // bc
        assert per * bc == self.width

        def imap(i, j, k):
            r, c = rc_of(i, j, k)
            return (r, (self.first + (c // per) * self.stride) * per + c % per)
        return pl.BlockSpec((br, bc), imap)


def _mm(a, b, *, ta=False, tb=False, out_dtypes=(F32,), tm=1024, tn=1024, tk=1024,
        pro_a=None, pro_b=None, epi=None, extras=(), n_row_sums=0, out=None, after=None, name):
    if ta:
        K, M = a.shape
    else:
        M, K = a.shape
    if not isinstance(b, (_Stored, _ColBlocks)):
        b = _Stored(b)
    N, Kb = b.shape if tb else b.shape[::-1]
    assert K == Kb, (a.shape, b.shape, ta, tb)
    col_unit = b.width if b.kind == "colblocks" else b.arr.shape[-1] if b.kind == "cols" else b.shape[1]
    row_unit = b.arr.shape[-2] if b.kind == "rows" else b.shape[0]
    n_unit, k_unit = (row_unit, col_unit) if tb else (col_unit, row_unit)
    m_unit = M
    if out is not None:
        m_unit, n_unit = out[4], math.gcd(n_unit, out[5])
    tm, tn, tk = _pick_tile(m_unit, tm), _pick_tile(n_unit, tn), _pick_tile(k_unit, tk)
    assert not n_row_sums or tn == N
    nk = K // tk
    a_spec = (pl.BlockSpec((tk, tm), lambda i, j, k: (k, i)) if ta
              else pl.BlockSpec((tm, tk), lambda i, j, k: (i, k)))
    b_spec = b.spec(tn, tk, lambda i, j, k: (j, k)) if tb else b.spec(tk, tn, lambda i, j, k: (k, j))
    ex_specs = []
    for e in extras:
        if e.shape[0] == 1:
            ex_specs.append(pl.BlockSpec((1, tn), lambda i, j, k: (0, j)))
        else:
            assert e.shape == (M, N), (e.shape, M, N)
            ex_specs.append(pl.BlockSpec((tm, tn), lambda i, j, k: (i, j)))
    dims = (((0 if ta else 1,), (1 if tb else 0,)), ((), ()))
    n_ex, n_out = len(extras), len(out_dtypes)
    direct = epi is None and n_out == 1 and out_dtypes[0] == F32
    use_acc = nk > 1 and not direct
    operands, aliases, alias_specs = [a, b.arr, *extras], {}, []
    if after is not None:
        operands.append(after)
        alias_specs.append(pl.BlockSpec(memory_space=pl.ANY))
    if out is None:
        n_tile_out = n_out - n_row_sums
        out_specs = ([pl.BlockSpec((tm, tn), lambda i, j, k: (i, j))] * n_tile_out
                     + [pl.BlockSpec((1, tn), lambda i, j, k: (0, j))] * n_row_sums)
        out_shape = ([jax.ShapeDtypeStruct((M, N), dt) for dt in out_dtypes[:n_tile_out]]
                     + [jax.ShapeDtypeStruct((1, N), dt) for dt in out_dtypes[n_tile_out:]])
    else:
        shape, block_fn, imap_fn, alias = out[:4]
        assert n_out == 1
        out_specs = [pl.BlockSpec(block_fn(tm, tn), imap_fn(tm, tn))]
        out_shape = [jax.ShapeDtypeStruct(shape, out_dtypes[0])]
        if alias is not None:
            operands.append(alias)
            aliases = {len(operands) - 1: 0}
            alias_specs.append(pl.BlockSpec(memory_space=pl.ANY))
    n_in = len(operands)

    def finish(r, ex, outs, first_row_tile):
        res = epi(r, *[e[...] for e in ex]) if epi is not None else (r,)
        n_tile_out = n_out - n_row_sums
        for o, v in zip(outs[:n_tile_out], res):
            o[...] = v.astype(o.dtype)
        for o, v in zip(outs[n_tile_out:], res[n_tile_out:]):
            @pl.when(first_row_tile)
            def _(o=o):
                o[...] = jnp.zeros_like(o)
            o[...] += v

    def body(*refs):
        a_ref, b_ref = refs[:2]
        ex = refs[2:2 + n_ex]
        outs = refs[n_in:n_in + n_out]
        first_row_tile = pl.program_id(0) == 0
        at, bt = a_ref[...], b_ref[...]
        if pro_a is not None:
            at = pro_a(at)
        if pro_b is not None:
            bt = pro_b(bt)
        part = lax.dot_general(at, bt, dims, preferred_element_type=F32)
        if nk == 1:
            finish(part, ex, outs, first_row_tile)
            return
        acc = refs[-1] if use_acc else outs[0]
        k = pl.program_id(2)

        @pl.when(k == 0)
        def _():
            acc[...] = part

        @pl.when(k > 0)
        def _():
            acc[...] += part

        if use_acc:
            @pl.when(k == nk - 1)
            def _():
                finish(acc[...], ex, outs, first_row_tile)

    res = pl.pallas_call(
        body, grid=(M // tm, N // tn, nk),
        in_specs=[a_spec, b_spec] + ex_specs + alias_specs,
        out_specs=out_specs, out_shape=out_shape,
        scratch_shapes=[pltpu.VMEM((tm, tn), F32)] if use_acc else [],
        input_output_aliases=aliases,
        compiler_params=_cp(("arbitrary" if n_row_sums else "parallel", "parallel", "arbitrary")), name=name,
    )(*operands)
    return res[0] if n_out == 1 else res


def _pick_rows(total, target):
    for cand in range(min(target, total) // 8 * 8, 0, -8):
        if total % cand == 0:
            return cand
    return total


def _rowwise(f, rows, params, out_dtypes, *, tr, name):
    T = rows[0].shape[0]
    tr = _pick_rows(T, tr)
    nr, npar = len(rows), len(params)
    blk = [jax.ShapeDtypeStruct((tr, r.shape[1]), F32) for r in rows]
    blk += [jax.ShapeDtypeStruct(p.shape, F32) for p in params]
    out_avals = jax.eval_shape(f, *blk)

    def body(*refs):
        res = f(*[r[...].astype(F32) for r in refs[:nr + npar]])
        for o, v in zip(refs[nr + npar:], res):
            o[...] = v.astype(o.dtype)

    out = pl.pallas_call(
        body, grid=(T // tr,),
        in_specs=[pl.BlockSpec((tr, r.shape[1]), lambda i: (i, 0)) for r in rows]
        + [pl.BlockSpec(p.shape, lambda i, nd=p.ndim: (0,) * nd) for p in params],
        out_specs=[pl.BlockSpec((tr, o.shape[1]), lambda i: (i, 0)) for o in out_avals],
        out_shape=[jax.ShapeDtypeStruct((T, o.shape[1]), dt) for o, dt in zip(out_avals, out_dtypes)],
        compiler_params=_cp(("parallel",)), name=name,
    )(*rows, *params)
    return out


def _rowwise_vjp(f, rows, params, cots, drow_dtypes, *, adds=None, after=None, tr, name):
    adds = adds or {}
    T = rows[0].shape[0]
    tr = _pick_rows(T, tr)
    nr, npar, nc = len(rows), len(params), len(cots)
    want, want_dt = [], []
    for i, dt in enumerate(drow_dtypes):
        for one in (dt if isinstance(dt, tuple) else (dt,)):
            if one is not None:
                want.append(i)
                want_dt.append(one)
    add_idx = sorted(set(i for i in want if i in adds))
    add_arrays = [adds[i] for i in add_idx]
    na = len(add_arrays)
    extra = [] if after is None else [after]

    def body(*refs):
        ins = [r[...].astype(F32) for r in refs[:nr + npar]]
        cvals = [r[...].astype(F32) for r in refs[nr + npar:nr + npar + nc]]
        avals = refs[nr + npar + nc:nr + npar + nc + na]
        outs = refs[nr + npar + nc + na + len(extra):]
        _, vjp = jax.vjp(f, *ins)
        grads = vjp(tuple(cvals))
        for o, i in zip(outs[:len(want)], want):
            g = grads[i]
            if i in adds:
                g = g + avals[add_idx.index(i)][...].astype(F32)
            o[...] = g.astype(o.dtype)
        step = pl.program_id(0)
        for o, g in zip(outs[len(want):], grads[nr:]):
            @pl.when(step == 0)
            def _(o=o):
                o[...] = jnp.zeros_like(o)
            o[...] += g

    rspec = lambda r: pl.BlockSpec((tr, r.shape[1]), lambda i: (i, 0))
    pspec = lambda p: pl.BlockSpec(p.shape, lambda i, nd=p.ndim: (0,) * nd)
    out = pl.pallas_call(
        body, grid=(T // tr,),
        in_specs=[rspec(r) for r in rows] + [pspec(p) for p in params] + [rspec(c) for c in cots]
        + [rspec(a) for a in add_arrays] + [pl.BlockSpec(memory_space=pl.ANY)] * len(extra),
        out_specs=[rspec(rows[i]) for i in want] + [pspec(p) for p in params],
        out_shape=[jax.ShapeDtypeStruct(rows[i].shape, dt) for i, dt in zip(want, want_dt)]
        + [jax.ShapeDtypeStruct(p.shape, F32) for p in params],
        compiler_params=_cp(("arbitrary",)), name=name,
    )(*rows, *params, *cots, *add_arrays, *extra)
    return out[:len(want)], out[len(want):]


def _f_rms(x, g):
    return (x * lax.rsqrt(jnp.mean(x * x, axis=-1, keepdims=True) + EPS) * g,)


def _ln(x, g, b):
    mu = jnp.mean(x, axis=-1, keepdims=True)
    var = jnp.mean(jnp.square(x - mu), axis=-1, keepdims=True)
    return (x - mu) * lax.rsqrt(var + EPS) * g + b


def _f_glu(z):
    d = z.shape[1] // 2
    return (z[:, :d] * jax.nn.sigmoid(z[:, d:]),)


def _f_bias_glu(z, b):
    return _f_glu(z + b)


def _f_ln_silu(y, b_dw, g, b):
    return (jax.nn.silu(_ln(y + b_dw, g, b)),)


def _f_gelu_ln(z, g, b):
    d = z.shape[1] // 2
    zz = jax.nn.gelu(z)
    return zz[:, :d], _ln(zz[:, d:], g, b)


def _f_merge(o0, o1, o2, l0, l1, l2):
    m = jnp.maximum(jnp.maximum(l0, l1), l2)
    e0, e1, e2 = jnp.exp(l0 - m), jnp.exp(l1 - m), jnp.exp(l2 - m)
    s = e0 + e1 + e2
    pair = 2 * HEAD_DIM
    first_head = lax.broadcasted_iota(jnp.int32, (o0.shape[0], pair), 1) < HEAD_DIM
    cols = []
    for hp in range(o0.shape[1] // pair):
        acc = None
        for o, e in ((o0, e0), (o1, e1), (o2, e2)):
            wgt = e / s
            wp = jnp.where(first_head, wgt[:, 2 * hp:2 * hp + 1], wgt[:, 2 * hp + 1:2 * hp + 2])
            term = wp * o[:, hp * pair:(hp + 1) * pair]
            acc = term if acc is None else acc + term
        cols.append(acc)
    return (jnp.concatenate(cols, axis=1),)


def _loss_head(x, tgt, g, *, tr, name):
    T, D = x.shape
    tr = min(tr, T)

    def f(xv, gv, tv):
        y = _f_rms(xv, gv)[0]
        return 0.5 * jnp.mean(jnp.square(y - tv), axis=-1, keepdims=True)

    def body(x_ref, t_ref, g_ref, loss_ref, dx_ref, dxb_ref, dg_ref):
        tv = t_ref[...]
        l, vjp = jax.vjp(lambda xv, gv: f(xv, gv, tv), x_ref[...], g_ref[...])
        dx, dg = vjp(jnp.ones_like(l))
        dx_ref[...] = dx
        dxb_ref[...] = dx.astype(BF16)

        @pl.when(pl.program_id(0) == 0)
        def _():
            loss_ref[...] = jnp.zeros_like(loss_ref)
            dg_ref[...] = jnp.zeros_like(dg_ref)

        loss_ref[...] += jnp.sum(l)
        dg_ref[...] += dg

    return pl.pallas_call(
        body, grid=(T // tr,),
        in_specs=[pl.BlockSpec((tr, D), lambda i: (i, 0)), pl.BlockSpec((tr, D), lambda i: (i, 0)),
                  pl.BlockSpec((1, D), lambda i: (0, 0))],
        out_specs=[pl.BlockSpec((1, 128), lambda i: (0, 0)), pl.BlockSpec((tr, D), lambda i: (i, 0)),
                   pl.BlockSpec((tr, D), lambda i: (i, 0)), pl.BlockSpec((1, D), lambda i: (0, 0))],
        out_shape=[jax.ShapeDtypeStruct((1, 128), F32), jax.ShapeDtypeStruct((T, D), F32),
                   jax.ShapeDtypeStruct((T, D), BF16), jax.ShapeDtypeStruct((1, D), F32)],
        compiler_params=_cp(("arbitrary",)), name=name,
    )(x, tgt, g)


def _adamw(w, g, m, v, *, name):
    R, C = w.shape
    tr = _pick_rows(R, max(8, 2 * 1024 * 1024 // (4 * C)))
    c1 = 1.0 - ADAM_B1 ** ADAM_STEP
    c2 = 1.0 - ADAM_B2 ** ADAM_STEP

    def body(w_ref, g_ref, m_ref, v_ref, d_ref, nm_ref, nv_ref):
        gv = g_ref[...]
        nm = ADAM_B1 * m_ref[...] + (1.0 - ADAM_B1) * gv
        nv = ADAM_B2 * v_ref[...] + (1.0 - ADAM_B2) * jnp.square(gv)
        nm_ref[...] = nm
        nv_ref[...] = nv
        d_ref[...] = -ADAM_LR * ((nm / c1) / (jnp.sqrt(nv / c2) + ADAM_EPS) + ADAM_WD * w_ref[...])

    spec = pl.BlockSpec((tr, C), lambda i: (i, 0))
    return pl.pallas_call(
        body, grid=(R // tr,), in_specs=[spec] * 4, out_specs=[spec] * 3,
        out_shape=[jax.ShapeDtypeStruct((R, C), F32)] * 3,
        compiler_params=_cp(("parallel",)), name=name,
    )(w, g, m, v)


def _s5_prep(a_re, a_im, b_re, b_im, c_re, c_im, log_dt):
    G, N = a_re.shape
    P = b_re.shape[2]
    gpb = 128 // P
    nblk = G // gpb
    dt = jnp.exp(log_dt)[:, None]
    mag = jnp.exp(a_re * dt)
    abr, abi = mag * jnp.cos(a_im * dt), mag * jnp.sin(a_im * dt)
    den = a_re * a_re + a_im * a_im
    nr, ni = abr - 1.0, abi
    qr, qi = (nr * a_re + ni * a_im) / den, (ni * a_re - nr * a_im) / den
    bbr = qr[..., None] * b_re - qi[..., None] * b_im
    bbi = qr[..., None] * b_im + qi[..., None] * b_re
    eye = jnp.eye(gpb, dtype=F32)

    def expand_b(t):
        t = t.reshape(nblk, gpb, N, P).transpose(0, 1, 3, 2)
        return (t[:, :, :, None, :] * eye[None, :, None, :, None]).reshape(nblk, gpb * P, gpb * N)

    def expand_c(t):
        t = t.reshape(nblk, gpb, P, N).transpose(0, 1, 3, 2)
        return (t[:, :, :, None, :] * eye[None, :, None, :, None]).reshape(nblk, gpb * N, gpb * P)

    return (abr.reshape(1, G * N), abi.reshape(1, G * N), expand_b(bbr), expand_b(bbi),
            expand_c(c_re), expand_c(c_im))


def _s5_fwd(h, abr, abi, bre, bim, cre, cim, d, *, B, L, name):
    T, D = h.shape
    S = T // B
    L = min(L, S)
    nc = S // L
    nblk, cb, sb = bre.shape
    GN = abr.shape[1]

    def body(h_ref, ar_ref, ai_ref, bre_ref, bim_ref, cre_ref, cim_ref, d_ref,
             y_ref, yb_ref, xr_ref, xi_ref, er_ref, ei_ref, sr, si, car, cai):
        ci = pl.program_id(1)

        @pl.when(ci == 0)
        def _():
            car[...] = jnp.zeros_like(car)
            cai[...] = jnp.zeros_like(cai)

        for j in range(nblk):
            u = h_ref[:, j * cb:(j + 1) * cb]
            sr[:, j * sb:(j + 1) * sb] = jnp.dot(u, bre_ref[j], preferred_element_type=F32)
            si[:, j * sb:(j + 1) * sb] = jnp.dot(u, bim_ref[j], preferred_element_type=F32)
        ar, ai = ar_ref[...], ai_ref[...]

        def step(t, carry):
            pr, pi = carry
            nr = ar * pr - ai * pi + sr[pl.ds(t, 1), :]
            ni = ar * pi + ai * pr + si[pl.ds(t, 1), :]
            sr[pl.ds(t, 1), :] = nr
            si[pl.ds(t, 1), :] = ni
            return nr, ni

        pr, pi = lax.fori_loop(0, L, step, (car[...], cai[...]), unroll=4)
        car[...] = pr
        cai[...] = pi
        er_ref[0] = pr
        ei_ref[0] = pi
        for j in range(nblk):
            xr = sr[:, j * sb:(j + 1) * sb].astype(BF16)
            xi = si[:, j * sb:(j + 1) * sb].astype(BF16)
            xr_ref[:, j * sb:(j + 1) * sb] = xr
            xi_ref[:, j * sb:(j + 1) * sb] = xi
            y = (jnp.dot(xr, cre_ref[j], preferred_element_type=F32)
                 - jnp.dot(xi, cim_ref[j], preferred_element_type=F32))
            u = h_ref[:, j * cb:(j + 1) * cb].astype(F32)
            y = y + d_ref[:, j * cb:(j + 1) * cb] * u
            y_ref[:, j * cb:(j + 1) * cb] = y
            yb_ref[:, j * cb:(j + 1) * cb] = jax.nn.gelu(y).astype(BF16)

    tok = lambda w: pl.BlockSpec((L, w), lambda b, c: (b * nc + c, 0))
    whole = lambda p: pl.BlockSpec(p.shape, lambda b, c, nd=p.ndim: (0,) * nd)
    end = pl.BlockSpec((1, 1, GN), lambda b, c: (b * nc + c, 0, 0))
    return pl.pallas_call(
        body, grid=(B, nc),
        in_specs=[tok(D)] + [whole(p) for p in (abr, abi, bre, bim, cre, cim, d)],
        out_specs=[tok(D), tok(D), tok(GN), tok(GN), end, end],
        out_shape=[jax.ShapeDtypeStruct((T, D), F32), jax.ShapeDtypeStruct((T, D), BF16),
                   jax.ShapeDtypeStruct((T, GN), BF16),
                   jax.ShapeDtypeStruct((T, GN), BF16), jax.ShapeDtypeStruct((B * nc, 1, GN), F32),
                   jax.ShapeDtypeStruct((B * nc, 1, GN), F32)],
        scratch_shapes=[pltpu.VMEM((L, GN), F32), pltpu.VMEM((L, GN), F32),
                        pltpu.VMEM((1, GN), F32), pltpu.VMEM((1, GN), F32)],
        compiler_params=_cp(("arbitrary", "arbitrary")), name=name,
    )(h, abr, abi, bre, bim, cre, cim, d)


def _s5_bwd(dy, h, xr, xi, er, ei, abr, abi, bre, bim, cre, cim, d, *, B, L, name):
    T, D = h.shape
    S = T // B
    L = min(L, S)
    nc = S // L
    nblk, cb, sb = bre.shape
    GN = abr.shape[1]
    dims_nt = (((1,), (1,)), ((), ()))
    dims_tn = (((0,), (0,)), ((), ()))

    def body(dy_ref, h_ref, xr_ref, xi_ref, er_ref, ei_ref, ar_ref, ai_ref, bre_ref, bim_ref,
             cre_ref, cim_ref, d_ref,
             dh_ref, dbre_ref, dbim_ref, dcre_ref, dcim_ref, dar_ref, dai_ref, dd_ref,
             lr, li, car, cai):
        b, cstep = pl.program_id(0), pl.program_id(1)
        ci = nc - 1 - cstep

        @pl.when((b == 0) & (cstep == 0))
        def _():
            for r in (dbre_ref, dbim_ref, dcre_ref, dcim_ref, dar_ref, dai_ref, dd_ref):
                r[...] = jnp.zeros_like(r)

        @pl.when(cstep == 0)
        def _():
            car[...] = jnp.zeros_like(car)
            cai[...] = jnp.zeros_like(cai)

        for j in range(nblk):
            dyj = dy_ref[:, j * cb:(j + 1) * cb].astype(BF16)
            lr[:, j * sb:(j + 1) * sb] = lax.dot_general(dyj, cre_ref[j], dims_nt, preferred_element_type=F32)
            li[:, j * sb:(j + 1) * sb] = -lax.dot_general(dyj, cim_ref[j], dims_nt, preferred_element_type=F32)
        ar, ai = ar_ref[...], ai_ref[...]

        def step(s, carry):
            t = L - 1 - s
            pr, pi = carry
            nr = lr[pl.ds(t, 1), :] + ar * pr + ai * pi
            ni = li[pl.ds(t, 1), :] - ai * pr + ar * pi
            lr[pl.ds(t, 1), :] = nr
            li[pl.ds(t, 1), :] = ni
            return nr, ni

        pr, pi = lax.fori_loop(0, L, step, (car[...], cai[...]), unroll=4)
        car[...] = pr
        cai[...] = pi
        has_prev = (ci > 0).astype(F32)
        first_row = lax.broadcasted_iota(jnp.int32, (L, sb), 0) == 0
        for j in range(nblk):
            cs = slice(j * cb, (j + 1) * cb)
            ss = slice(j * sb, (j + 1) * sb)
            lrj, lij = lr[:, ss], li[:, ss]
            xrj, xij = xr_ref[:, ss], xi_ref[:, ss]
            pr_j = jnp.where(first_row, er_ref[0][:, ss] * has_prev, pltpu.roll(xrj.astype(F32), 1, 0))
            pi_j = jnp.where(first_row, ei_ref[0][:, ss] * has_prev, pltpu.roll(xij.astype(F32), 1, 0))
            dar_ref[:, ss] += jnp.sum(lrj * pr_j + lij * pi_j, axis=0, keepdims=True)
            dai_ref[:, ss] += jnp.sum(lij * pr_j - lrj * pi_j, axis=0, keepdims=True)
            lrb, lib = lrj.astype(BF16), lij.astype(BF16)
            hj = h_ref[:, cs]
            dyf = dy_ref[:, cs]
            dyj = dyf.astype(BF16)
            dbre_ref[j] += lax.dot_general(hj, lrb, dims_tn, preferred_element_type=F32)
            dbim_ref[j] += lax.dot_general(hj, lib, dims_tn, preferred_element_type=F32)
            dcre_ref[j] += lax.dot_general(xrj, dyj, dims_tn, preferred_element_type=F32)
            dcim_ref[j] -= lax.dot_general(xij, dyj, dims_tn, preferred_element_type=F32)
            du = (lax.dot_general(lrb, bre_ref[j], dims_nt, preferred_element_type=F32)
                  + lax.dot_general(lib, bim_ref[j], dims_nt, preferred_element_type=F32))
            dh_ref[:, cs] = du + d_ref[:, cs] * dyf
            dd_ref[:, cs] += jnp.sum(dyf * hj.astype(F32), axis=0, keepdims=True)

    tok = lambda w: pl.BlockSpec((L, w), lambda b, c: (b * nc + nc - 1 - c, 0))
    whole = lambda p: pl.BlockSpec(p.shape, lambda b, c, nd=p.ndim: (0,) * nd)
    prev_end = pl.BlockSpec((1, 1, GN), lambda b, c: (b * nc + jnp.maximum(nc - 2 - c, 0), 0, 0))
    params = (abr, abi, bre, bim, cre, cim, d)
    acc_shapes = [bre.shape, bim.shape, cre.shape, cim.shape, abr.shape, abi.shape, d.shape]
    out = pl.pallas_call(
        body, grid=(B, nc),
        in_specs=[tok(D), tok(D), tok(GN), tok(GN), prev_end, prev_end] + [whole(p) for p in params],
        out_specs=[tok(D)] + [pl.BlockSpec(s, lambda b, c, nd=len(s): (0,) * nd) for s in acc_shapes],
        out_shape=[jax.ShapeDtypeStruct((T, D), F32)] + [jax.ShapeDtypeStruct(s, F32) for s in acc_shapes],
        scratch_shapes=[pltpu.VMEM((L, GN), F32), pltpu.VMEM((L, GN), F32),
                        pltpu.VMEM((1, GN), F32), pltpu.VMEM((1, GN), F32)],
        compiler_params=_cp(("arbitrary", "arbitrary")), name=name,
    )(dy, h, xr, xi, er, ei, *params)
    return out


def _conv_fwd(zp, w, *, R, tc, name):
    B, SP, C = zp.shape
    S = SP - CONV_HALO
    R, tc = min(R, S), min(tc, C)

    def body(z_ref, w_ref, y_ref):
        def chunk(ci, _):
            start = pl.multiple_of(ci * R, 8)
            ze = z_ref[pl.ds(start, R + CONV_HALO), :]
            acc = jnp.zeros((R, tc), F32)
            for m in range(CONV_WIDTH):
                k = CONV_WIDTH - 1 - m
                sh = ze if m == 0 else pltpu.roll(ze, m, 0)
                acc = acc + w_ref[k:k + 1, :] * sh[CONV_HALO:, :]
            y_ref[pl.ds(start, R), :] = acc
            return 0

        lax.fori_loop(0, S // R, chunk, 0)

    return pl.pallas_call(
        body, grid=(B, C // tc),
        in_specs=[pl.BlockSpec((None, SP, tc), lambda b, c: (b, 0, c)),
                  pl.BlockSpec((32, tc), lambda b, c: (0, c))],
        out_specs=pl.BlockSpec((None, S, tc), lambda b, c: (b, 0, c)),
        out_shape=jax.ShapeDtypeStruct((B, S, C), F32),
        compiler_params=_cp(("parallel", "parallel")), name=name,
    )(zp, w)


def _conv_bwd(zp, dyp, w, *, R, tc, name):
    B, SP, C = zp.shape
    S = SP - CONV_HALO
    R, tc = min(R, S), min(tc, C)

    def body(z_ref, dy_ref, w_ref, dz_ref, dw_ref):
        @pl.when(pl.program_id(1) == 0)
        def _():
            dw_ref[...] = jnp.zeros_like(dw_ref)

        def chunk(ci, _):
            start = pl.multiple_of(ci * R, 8)
            zc = z_ref[pl.ds(start + CONV_HALO, R), :]
            de = dy_ref[pl.ds(start, R + CONV_HALO), :]
            acc = jnp.zeros((R, tc), F32)
            for m in range(CONV_WIDTH):
                k = CONV_WIDTH - 1 - m
                ds_ = (de if m == 0 else pltpu.roll(de, R + CONV_HALO - m, 0))[:R, :]
                acc = acc + w_ref[k:k + 1, :] * ds_
                dw_ref[k:k + 1, :] += jnp.sum(ds_ * zc, axis=0, keepdims=True)
            dz_ref[pl.ds(start, R), :] = acc
            return 0

        lax.fori_loop(0, S // R, chunk, 0)

    return pl.pallas_call(
        body, grid=(C // tc, B),
        in_specs=[pl.BlockSpec((None, SP, tc), lambda c, b: (b, 0, c)),
                  pl.BlockSpec((None, SP, tc), lambda c, b: (b, 0, c)),
                  pl.BlockSpec((32, tc), lambda c, b: (0, c))],
        out_specs=[pl.BlockSpec((None, S, tc), lambda c, b: (b, 0, c)),
                   pl.BlockSpec((32, tc), lambda c, b: (0, c))],
        out_shape=[jax.ShapeDtypeStruct((B, S, C), F32), jax.ShapeDtypeStruct((32, C), F32)],
        compiler_params=_cp(("parallel", "arbitrary")), name=name,
    )(zp, dyp, w)


def _gmlp_fwd(u, vn, ws, bcol, *, nck, name):
    T, E = u.shape
    H = ws.shape[0]
    he = E // H
    rows = nck * GMLP_CHUNK
    rows = min(rows, T)
    n_in = rows // GMLP_CHUNK

    def body(u_ref, v_ref, ws_ref, b_ref, o_ref):
        for c in range(n_in):
            rs = slice(c * GMLP_CHUNK, (c + 1) * GMLP_CHUNK)
            for hh in range(H):
                cs = slice(hh * he, (hh + 1) * he)
                v2 = jnp.dot(ws_ref[hh], v_ref[rs, cs].astype(BF16), preferred_element_type=F32)
                v2 = v2 + b_ref[:, hh:hh + 1]
                o_ref[rs, cs] = (u_ref[rs, cs] * v2).astype(o_ref.dtype)

    tok = pl.BlockSpec((rows, E), lambda i: (i, 0))
    return pl.pallas_call(
        body, grid=(T // rows,),
        in_specs=[tok, tok, pl.BlockSpec(ws.shape, lambda i: (0, 0, 0)), pl.BlockSpec(bcol.shape, lambda i: (0, 0))],
        out_specs=tok, out_shape=jax.ShapeDtypeStruct((T, E), BF16),
        compiler_params=_cp(("parallel",)), name=name,
    )(u, vn, ws, bcol)


def _gmlp_bwd(duv, u, vn, ws, bcol, *, nck, name):
    T, E = u.shape
    H = ws.shape[0]
    he = E // H
    rows = min(nck * GMLP_CHUNK, T)
    n_in = rows // GMLP_CHUNK
    dims_nt = (((1,), (1,)), ((), ()))
    dims_tn = (((0,), (0,)), ((), ()))

    def body(g_ref, u_ref, v_ref, ws_ref, b_ref, du_ref, dv_ref, dws_ref, db_ref):
        @pl.when(pl.program_id(0) == 0)
        def _():
            dws_ref[...] = jnp.zeros_like(dws_ref)
            db_ref[...] = jnp.zeros_like(db_ref)

        for c in range(n_in):
            rs = slice(c * GMLP_CHUNK, (c + 1) * GMLP_CHUNK)
            for hh in range(H):
                cs = slice(hh * he, (hh + 1) * he)
                vb = v_ref[rs, cs].astype(BF16)
                v2 = jnp.dot(ws_ref[hh], vb, preferred_element_type=F32) + b_ref[:, hh:hh + 1]
                g = g_ref[rs, cs]
                du_ref[rs, cs] = g * v2
                dv2 = g * u_ref[rs, cs]
                dv2b = dv2.astype(BF16)
                dv_ref[rs, cs] = lax.dot_general(ws_ref[hh], dv2b, dims_tn, preferred_element_type=F32)
                dws_ref[hh] += lax.dot_general(dv2b, vb, dims_nt, preferred_element_type=F32)
                db_ref[:, hh:hh + 1] += jnp.sum(dv2, axis=1, keepdims=True)

    tok = pl.BlockSpec((rows, E), lambda i: (i, 0))
    return pl.pallas_call(
        body, grid=(T // rows,),
        in_specs=[tok, tok, tok, pl.BlockSpec(ws.shape, lambda i: (0, 0, 0)), pl.BlockSpec(bcol.shape, lambda i: (0, 0))],
        out_specs=[tok, tok, pl.BlockSpec(ws.shape, lambda i: (0, 0, 0)), pl.BlockSpec(bcol.shape, lambda i: (0, 0))],
        out_shape=[jax.ShapeDtypeStruct((T, E), F32), jax.ShapeDtypeStruct((T, E), F32),
                   jax.ShapeDtypeStruct(ws.shape, F32), jax.ShapeDtypeStruct(bcol.shape, F32)],
        compiler_params=_cp(("arbitrary",)), name=name,
    )(duv, u, vn, ws, bcol)


PAIRS = ATT_HEADS // 2


def _att_consts():
    ji = lax.broadcasted_iota(jnp.int32, (2 * ATT_BLK, ATT_BLK), 0)
    ii = lax.broadcasted_iota(jnp.int32, (2 * ATT_BLK, ATT_BLK), 1)
    dist = ii + ATT_BLK - ji
    band = (dist >= 0) & (dist <= ATT_BLK)
    cur = ji >= ATT_BLK
    first_head = lax.broadcasted_iota(jnp.int32, (ATT_BLK, 2 * HEAD_DIM), 1) < HEAD_DIM
    return band, cur, first_head


def _both_heads(t, first_head):
    zero = jnp.zeros_like(t)
    return jnp.concatenate([jnp.where(first_head, t, zero), jnp.where(first_head, zero, t)], axis=0)


def _att_specs(nbk, offs, nsteps, rev):
    rows = nbk * ATT_BLK
    step = (lambda i: nsteps - 1 - i) if rev else (lambda i: i)
    qoff, koff, voff = offs
    blk = lambda off: pl.BlockSpec((rows, 2 * HEAD_DIM), lambda hp, i: (step(i), off + hp))
    prev = lambda off: pl.BlockSpec((ATT_BLK, 2 * HEAD_DIM), lambda hp, i: (jnp.maximum(step(i) * nbk - 1, 0), off + hp))
    out = pl.BlockSpec((rows, 2 * HEAD_DIM), lambda hp, i: (step(i), hp))
    stat = pl.BlockSpec((2, nbk, ATT_BLK), lambda hp, i: (hp, step(i), 0))
    return [blk(qoff), blk(koff), prev(koff), blk(voff), prev(voff)], out, stat


def _att_fwd(arr, offs, *, nb, nbk, name):
    T = arr.shape[0]
    nbk = min(nbk, T // ATT_BLK)
    nsteps = T // (nbk * ATT_BLK)
    scale = HEAD_DIM ** -0.5
    dims_nt = (((1,), (1,)), ((), ()))
    dims_tn = (((0,), (0,)), ((), ()))

    def body(q_ref, k_ref, kp_ref, v_ref, vp_ref, o_ref, lse_ref):
        i = pl.program_id(1)
        band, cur, first_head = _att_consts()
        for jj in range(nbk):
            rs = slice(jj * ATT_BLK, (jj + 1) * ATT_BLK)
            ps = slice((jj - 1) * ATT_BLK, jj * ATT_BLK)
            has_prev = ((i * nbk + jj) & (nb - 1)) != 0
            valid = band & (cur | has_prev)
            kk = jnp.concatenate([kp_ref[...] if jj == 0 else k_ref[ps, :], k_ref[rs, :]], axis=0)
            vv = jnp.concatenate([vp_ref[...] if jj == 0 else v_ref[ps, :], v_ref[rs, :]], axis=0)
            q2 = _both_heads(q_ref[rs, :], first_head)
            st = lax.dot_general(kk, q2, dims_nt, preferred_element_type=F32) * scale
            st = jnp.where(jnp.concatenate([valid, valid], axis=1), st, MASK_VALUE)
            m = jnp.max(st, axis=0, keepdims=True)
            p = jnp.exp(st - m)
            l = jnp.sum(p, axis=0, keepdims=True)
            lse = m + jnp.log(l)
            lse_ref[0, jj:jj + 1, :] = lse[:, :ATT_BLK]
            lse_ref[1, jj:jj + 1, :] = lse[:, ATT_BLK:]
            pn = (p / l).astype(BF16)
            o2 = lax.dot_general(pn, vv, dims_tn, preferred_element_type=F32)
            o_ref[rs, :] = jnp.where(first_head, o2[:ATT_BLK], o2[ATT_BLK:])

    ins, out, stat = _att_specs(nbk, offs, nsteps, False)
    return pl.pallas_call(
        body, grid=(PAIRS, nsteps), in_specs=ins, out_specs=[out, stat],
        out_shape=[jax.ShapeDtypeStruct((T, ATT_W), F32), jax.ShapeDtypeStruct((ATT_HEADS, T // ATT_BLK, ATT_BLK), F32)],
        compiler_params=_cp(("parallel", "parallel")), name=name,
    )(arr, arr, arr, arr, arr)


def _att_bwd(arr, offs, do, lse, dlse, *, nb, nbk, name):
    T = arr.shape[0]
    nbk = min(nbk, T // ATT_BLK)
    nsteps = T // (nbk * ATT_BLK)
    scale = HEAD_DIM ** -0.5
    dims_nt = (((1,), (1,)), ((), ()))
    dims_tn = (((0,), (0,)), ((), ()))

    def body(q_ref, k_ref, kp_ref, v_ref, vp_ref, do_ref, lse_ref, dlse_ref, dq_ref, dk_ref, dv_ref, ck, cv):
        step = pl.program_id(1)
        i = nsteps - 1 - step
        band, cur, first_head = _att_consts()

        @pl.when(step == 0)
        def _():
            ck[...] = jnp.zeros_like(ck)
            cv[...] = jnp.zeros_like(cv)

        carry_k, carry_v = ck[...], cv[...]
        for jj in reversed(range(nbk)):
            rs = slice(jj * ATT_BLK, (jj + 1) * ATT_BLK)
            ps = slice((jj - 1) * ATT_BLK, jj * ATT_BLK)
            has_prev = ((i * nbk + jj) & (nb - 1)) != 0
            valid = band & (cur | has_prev)
            kk = jnp.concatenate([kp_ref[...] if jj == 0 else k_ref[ps, :], k_ref[rs, :]], axis=0)
            vv = jnp.concatenate([vp_ref[...] if jj == 0 else v_ref[ps, :], v_ref[rs, :]], axis=0)
            q2 = _both_heads(q_ref[rs, :], first_head)
            do2 = _both_heads(do_ref[rs, :].astype(BF16), first_head)
            lse = jnp.concatenate([lse_ref[0, jj:jj + 1, :], lse_ref[1, jj:jj + 1, :]], axis=1)
            dlse = jnp.concatenate([dlse_ref[0, jj:jj + 1, :], dlse_ref[1, jj:jj + 1, :]], axis=1)
            st = lax.dot_general(kk, q2, dims_nt, preferred_element_type=F32) * scale
            st = jnp.where(jnp.concatenate([valid, valid], axis=1), st, MASK_VALUE)
            p = jnp.exp(st - lse)
            dp = lax.dot_general(vv, do2, dims_nt, preferred_element_type=F32)
            delta = jnp.sum(p * dp, axis=0, keepdims=True)
            dsb = (p * (dp - delta + dlse) * scale).astype(BF16)
            dq2 = lax.dot_general(dsb, kk, dims_tn, preferred_element_type=F32)
            dkk = jnp.dot(dsb, q2, preferred_element_type=F32)
            dvv = jnp.dot(p.astype(BF16), do2, preferred_element_type=F32)
            dq_ref[rs, :] = jnp.where(first_head, dq2[:ATT_BLK], dq2[ATT_BLK:]).astype(dq_ref.dtype)
            dk_ref[rs, :] = (dkk[ATT_BLK:] + carry_k).astype(dk_ref.dtype)
            dv_ref[rs, :] = (dvv[ATT_BLK:] + carry_v).astype(dv_ref.dtype)
            carry_k, carry_v = dkk[:ATT_BLK], dvv[:ATT_BLK]
        ck[...] = carry_k
        cv[...] = carry_v

    ins, out, stat = _att_specs(nbk, offs, nsteps, True)
    return pl.pallas_call(
        body, grid=(PAIRS, nsteps), in_specs=ins + [out, stat, stat], out_specs=[out] * 3,
        out_shape=[jax.ShapeDtypeStruct((T, ATT_W), BF16)] * 3,
        scratch_shapes=[pltpu.VMEM((ATT_BLK, 2 * HEAD_DIM), F32), pltpu.VMEM((ATT_BLK, 2 * HEAD_DIM), F32)],
        compiler_params=_cp(("arbitrary", "arbitrary")), name=name,
    )(arr, arr, arr, arr, arr, do, lse, dlse)


def _deinterleave(t, B, S, dil):
    if dil == 1:
        return t
    return t.reshape((B, S // dil, dil) + t.shape[1:]).swapaxes(1, 2).reshape(t.shape)


def _interleave(t, B, S, dil):
    if dil == 1:
        return t
    return t.reshape((B, dil, S // dil) + t.shape[1:]).swapaxes(1, 2).reshape(t.shape)


def _stats_to_tokens(lse, B, S, dil):
    return _interleave(lse.reshape(lse.shape[0], -1).T, B, S, dil)


def _stats_from_tokens(dl, B, S, dil):
    return _deinterleave(dl, B, S, dil).T.reshape(dl.shape[1], -1, ATT_BLK)


def _mesh_pos():
    return lax.axis_index("x"), lax.axis_index("y"), lax.axis_index("c")


def _allgather8(xs, *, name):
    m_per, n = xs.shape

    def body(x_ref, out_ref, send_sems, recv_sems, local_sem):
        x, y, c = _mesh_pos()
        me, sibling = (x, y, c), (x, y, 1 - c)
        chips = [(1 - x, y), (x, 1 - y), (1 - x, 1 - y)]

        def rows(px, py, pc):
            return out_ref.at[pl.ds((4 * px + 2 * py + pc) * m_per, m_per), :]

        def copy(k, block, to, src=None):
            return pltpu.make_async_remote_copy(
                src_ref=rows(*block) if src is None else src, dst_ref=rows(*block),
                send_sem=send_sems.at[k], recv_sem=recv_sems.at[k], device_id=to, device_id_type=MESH)

        mine = pltpu.make_async_copy(x_ref, rows(*me), local_sem)
        mine.start()
        first = [copy(0, me, sibling, src=x_ref)]
        first += [copy(1 + j, me, (*chip, c), src=x_ref) for j, chip in enumerate(chips)]
        for cp in first:
            cp.start()
        passed = [copy(4 + j, (*chip, c), sibling) for j, chip in enumerate(chips)]
        for j, chip in enumerate(chips):
            copy(1 + j, (*chip, c), me).wait_recv()
            passed[j].start()
        copy(0, sibling, me).wait_recv()
        for j, chip in enumerate(chips):
            copy(4 + j, (*chip, 1 - c), me).wait_recv()
        for cp in first + passed:
            cp.wait_send()
        mine.wait()

    return pl.pallas_call(
        body, out_shape=jax.ShapeDtypeStruct((8 * m_per, n), xs.dtype),
        in_specs=[pl.BlockSpec(memory_space=pltpu.VMEM)], out_specs=pl.BlockSpec(memory_space=pltpu.VMEM),
        scratch_shapes=[pltpu.SemaphoreType.DMA((7,)), pltpu.SemaphoreType.DMA((7,)), pltpu.SemaphoreType.DMA],
        compiler_params=pltpu.CompilerParams(vmem_limit_bytes=VMEM_LIMIT), name=name,
    )(xs)


def _hbm_call(body, arrays, out_shapes, n_sems, *, name):
    any_spec = pl.BlockSpec(memory_space=pl.ANY)
    return pl.pallas_call(
        body, out_shape=out_shapes, in_specs=[any_spec] * len(arrays), out_specs=[any_spec] * len(out_shapes),
        scratch_shapes=[pltpu.SemaphoreType.DMA((n_sems,)), pltpu.SemaphoreType.DMA((n_sems,))], name=name,
    )(*arrays)


def _other_chips(x, y):
    return [(1 - x, y), (x, 1 - y), (1 - x, 1 - y)]


def _split_start(srcs, lands, after, issue, n_sems, *, name):
    ns, nl = len(srcs), len(lands)
    hbm, sem = pl.BlockSpec(memory_space=pltpu.HBM), pl.BlockSpec(memory_space=pltpu.SEMAPHORE)
    extra = [] if after is None else [after]

    def body(*refs):
        n_in = ns + nl + len(extra)
        send_sems, recv_sems = refs[n_in], refs[n_in + 1]
        issue(refs[:ns], refs[ns:ns + nl], send_sems, recv_sems)
        refs[-1][...] = jnp.zeros_like(refs[-1])

    arrays = [pltpu.with_memory_space_constraint(a, pltpu.HBM) for a in list(srcs) + list(lands)]
    out = pl.pallas_call(
        body, name=name,
        out_shape=(pltpu.SemaphoreType.DMA((n_sems,)), pltpu.SemaphoreType.DMA((n_sems,)),
                   *[pltpu.HBM(a.shape, a.dtype) for a in arrays], jax.ShapeDtypeStruct((8, 128), F32)),
        in_specs=[hbm] * (ns + nl) + [pl.BlockSpec(memory_space=pl.ANY)] * len(extra),
        out_specs=(sem, sem, *[hbm] * (ns + nl), pl.BlockSpec(memory_space=pltpu.VMEM)),
        input_output_aliases={i: 2 + i for i in range(ns + nl)},
        compiler_params=pltpu.CompilerParams(has_side_effects=pltpu.SideEffectType.DATAFLOW_SIDE_EFFECTING),
    )(*arrays, *extra)
    return out[0], out[1], list(out[2:2 + ns]), list(out[2 + ns:2 + ns + nl]), out[-1]


def _split_wait(send_sems, recv_sems, srcs, lands, after, waits, *, name):
    ns, nl = len(srcs), len(lands)
    hbm, sem = pl.BlockSpec(memory_space=pltpu.HBM), pl.BlockSpec(memory_space=pltpu.SEMAPHORE)

    def body(*refs):
        waits(refs[:ns], refs[ns:ns + nl], refs[ns + nl], refs[ns + nl + 1])

    out = pl.pallas_call(
        body, name=name,
        out_shape=tuple(pltpu.HBM(a.shape, a.dtype) for a in list(srcs) + list(lands)),
        in_specs=[hbm] * (ns + nl) + [sem, sem, pl.BlockSpec(memory_space=pl.ANY)],
        out_specs=tuple([hbm] * (ns + nl)),
        input_output_aliases={i: i for i in range(ns + nl)},
        compiler_params=pltpu.CompilerParams(has_side_effects=pltpu.SideEffectType.DATAFLOW_SIDE_EFFECTING),
    )(*srcs, *lands, send_sems, recv_sems, after)
    return list(out[:ns]), list(out[ns:])


def _gather_start(halves, after, *, name):
    n = len(halves)
    lands = [lax.empty((N_CHIPS,) + h.shape, h.dtype) for h in halves]

    def issue(srcs, dsts, send_sems, recv_sems):
        x, y, c = _mesh_pos()
        me = 2 * x + y
        for a in range(n):
            for j, (px, py) in enumerate(_other_chips(x, y)):
                for cc in range(2):
                    pltpu.make_async_remote_copy(
                        src_ref=srcs[a].at[c], dst_ref=dsts[a].at[me, c],
                        send_sem=send_sems.at[6 * a + 2 * j + cc], recv_sem=recv_sems.at[6 * a + 2 * j + c],
                        device_id=(px, py, cc), device_id_type=MESH).start()

    return _split_start(halves, lands, after, issue, 6 * n, name=name)


def _gather_wait(started, after, *, name):
    send_sems, recv_sems, halves, lands = started
    n = len(halves)

    def waits(srcs, dsts, send_sems, recv_sems):
        x, y, c = _mesh_pos()
        me = 2 * x + y
        for a in range(n):
            for j, (px, py) in enumerate(_other_chips(x, y)):
                for cc in range(2):
                    pltpu.make_async_remote_copy(
                        src_ref=srcs[a].at[cc], dst_ref=dsts[a].at[2 * px + py, cc],
                        send_sem=send_sems.at[6 * a + 2 * j + cc], recv_sem=recv_sems.at[6 * a + 2 * j + cc],
                        device_id=(px, py, cc), device_id_type=MESH).wait_recv()
        for a in range(n):
            for j, (px, py) in enumerate(_other_chips(x, y)):
                for cc in range(2):
                    pltpu.make_async_remote_copy(
                        src_ref=srcs[a].at[c], dst_ref=dsts[a].at[me, c],
                        send_sem=send_sems.at[6 * a + 2 * j + cc], recv_sem=recv_sems.at[6 * a + 2 * j + c],
                        device_id=(px, py, cc), device_id_type=MESH).wait_send()

    return _split_wait(send_sems, recv_sems, halves, lands, after, waits, name=name)


def _reduce_plan_loops(plans, chip, c, fn):
    for a, plan in enumerate(plans):
        for h, cc in plan:
            for k in range(N_CHIPS):
                fn(a, h, k, cc, jnp.logical_or(chip != k, c != cc))


def _reduce_start(srcs, lands, plans, after, *, name):
    def issue(src_refs, land_refs, send_sems, recv_sems):
        x, y, c = _mesh_pos()
        chip = 2 * x + y
        my_id = 2 * chip + c

        def send(a, h, k, cc, is_other):
            @pl.when(is_other)
            def _():
                pltpu.make_async_remote_copy(
                    src_ref=src_refs[a].at[h, k], dst_ref=land_refs[a].at[my_id],
                    send_sem=send_sems.at[8 * a + 2 * k + cc], recv_sem=recv_sems.at[8 * a + my_id],
                    device_id=(k // 2, k % 2, cc), device_id_type=MESH).start()

        _reduce_plan_loops(plans, chip, c, send)

    return _split_start(srcs, lands, after, issue, 8 * len(srcs), name=name)


def _reduce_wait(started, plans, after, *, name):
    send_sems, recv_sems, srcs, lands = started

    def waits(src_refs, land_refs, send_sems, recv_sems):
        x, y, c = _mesh_pos()
        chip = 2 * x + y
        my_id = 2 * chip + c
        for a, plan in enumerate(plans):
            for h, cc in plan:
                for s in range(2 * N_CHIPS):
                    @pl.when(jnp.logical_and(c == cc, my_id != s))
                    def _(a=a, h=h, s=s):
                        pltpu.make_async_remote_copy(
                            src_ref=src_refs[a].at[h, 0], dst_ref=land_refs[a].at[s],
                            send_sem=send_sems.at[8 * a + s], recv_sem=recv_sems.at[8 * a + s],
                            device_id=(s // 4, (s // 2) % 2, s % 2), device_id_type=MESH).wait_recv()

        def sent(a, h, k, cc, is_other):
            @pl.when(is_other)
            def _():
                pltpu.make_async_remote_copy(
                    src_ref=src_refs[a].at[h, k], dst_ref=land_refs[a].at[my_id],
                    send_sem=send_sems.at[8 * a + 2 * k + cc], recv_sem=recv_sems.at[8 * a + my_id],
                    device_id=(k // 2, k % 2, cc), device_id_type=MESH).wait_send()

        _reduce_plan_loops(plans, chip, c, sent)

    return _split_wait(send_sems, recv_sems, srcs, lands, after, waits, name=name)


def _sum8(land, own, my_id, *, name):
    n_src, R, C = land.shape
    tr = _pick_rows(R, max(8, 1024 * 1024 // (2 * C)))

    def body(id_ref, *refs):
        own_ref, o_ref = refs[n_src], refs[n_src + 1]
        me = id_ref[0]
        acc = None
        for s in range(n_src):
            term = jnp.where(me == s, own_ref[...], refs[s][...]).astype(F32)
            acc = term if acc is None else acc + term
        o_ref[...] = acc

    return pl.pallas_call(
        body, out_shape=jax.ShapeDtypeStruct((R, C), F32),
        grid_spec=pltpu.PrefetchScalarGridSpec(
            num_scalar_prefetch=1, grid=(R // tr,),
            in_specs=[pl.BlockSpec((None, tr, C), lambda i, idr, s=s: (s, i, 0)) for s in range(n_src)]
            + [pl.BlockSpec((tr, C), lambda i, idr: (i, 0))],
            out_specs=pl.BlockSpec((tr, C), lambda i, idr: (i, 0))),
        compiler_params=_cp(("parallel",)), name=name,
    )(my_id.reshape(1).astype(jnp.int32), *([land] * n_src), own)


def _share_halves(ts, *, name):
    n = len(ts)

    def body(*refs):
        ins, outs, (send_sems, recv_sems) = refs[:n], refs[n:2 * n], refs[2 * n:]
        x, y, c = _mesh_pos()
        cps = [pltpu.make_async_remote_copy(
            src_ref=ins[a], dst_ref=outs[a].at[c], send_sem=send_sems.at[a], recv_sem=recv_sems.at[a],
            device_id=(x, y, 1 - c), device_id_type=MESH) for a in range(n)]
        for cp in cps:
            cp.start()
        for a in range(n):
            pltpu.make_async_remote_copy(
                src_ref=ins[a], dst_ref=outs[a].at[1 - c], send_sem=send_sems.at[a], recv_sem=recv_sems.at[a],
                device_id=(x, y, 1 - c), device_id_type=MESH).wait_recv()
        for cp in cps:
            cp.wait_send()

    return _hbm_call(body, ts, [jax.ShapeDtypeStruct((2,) + t.shape, t.dtype) for t in ts], n, name=name)


TR = 512
ATT_BLOCKS_PER_STEP = 32
S5_CHUNK = 256


def _rms_fwd(x, g, name):
    return _rowwise(_f_rms, [x], [g], (BF16,), tr=TR, name=name)[0]


def _rms_bwd_epi(dh, x, g, gx):
    r = lax.rsqrt(jnp.mean(x * x, axis=-1, keepdims=True) + EPS)
    xr = x * r
    t = dh * g
    dx = r * (t - xr * jnp.mean(t * xr, axis=-1, keepdims=True)) + gx
    return dx, dx, jnp.sum(dh * xr, axis=0, keepdims=True)


def _mm_rms_bwd(a, w, x, g, gx, *, after=None, name, **kw):
    kw.setdefault("tm", 1024)
    return _mm(a, w, tb=True, epi=_rms_bwd_epi, extras=(x, g, gx), out_dtypes=(F32, BF16, F32), n_row_sums=1,
               tn=x.shape[1], after=after, name=name, **kw)


def _rms_bwd(x, g, dh, gx, name, after=None):
    (dx, dxb), (dg,) = _rowwise_vjp(_f_rms, [x], [g], [dh], [(F32, BF16)], adds={0: gx}, after=after, tr=TR,
                                    name=name)
    return dx, dxb, dg


def _grad_cols(M, Nq):
    def imap(tm, tn):
        hp, per = (M // 2) // tm, Nq // tn
        assert hp * tm * 2 == M and per * tn == Nq, (M, Nq, tm, tn)
        return lambda i, j, k: (i // hp, j // per, i % hp, j % per)
    return (2, N_CHIPS, M // 2, Nq), lambda tm, tn: (None, None, tm, tn), imap, None, M // 2, Nq


def _grad_rows(Mq, N):
    def imap(tm, tn):
        po, hp = Mq // tm, (Mq // 2) // tm
        assert hp * tm * 2 == Mq, (Mq, tm)
        return lambda i, j, k: ((i % po) // hp, i // po, (i % po) % hp, j)
    return (2, N_CHIPS, Mq // 2, N), lambda tm, tn: (None, None, tm, tn), imap, None, Mq // 2, N


def _grad_layer_cols(slot, lh, M, Nq, buf):
    def imap(tm, tn):
        per = Nq // tn
        return lambda i, j, k: (j // per, slot, i, j % per)
    return (N_CHIPS, lh, M, Nq), lambda tm, tn: (None, None, tm, tn), imap, buf, M, Nq


def _grad_layer_rows(slot, lh, Mq, N, buf):
    def imap(tm, tn):
        po = Mq // tm
        return lambda i, j, k: (i // po, slot, i % po, j)
    return (N_CHIPS, lh, Mq, N), lambda tm, tn: (None, None, tm, tn), imap, buf, Mq, N


def _add_then_rms(acc, res, g):
    xo = acc + res
    return xo, _f_rms(xo, g)[0]


def _mlp_fwd(x, h2, w_in, w_out, g_next, li):
    w_in_full = getattr(w_in, "plain", w_in)
    r = _mm(h2, w_in_full, out_dtypes=(BF16,), epi=lambda acc: (jnp.maximum(acc, 0.0),), tm=512, tn=w_in.shape[1],
            name=f"mlp_in_{li}")
    tiles = dict(tm=512, tn=x.shape[1], tk=r.shape[1])
    if g_next is None:
        x_out, h_next = _mm(r, w_out, pro_a=lambda t: t * t, epi=lambda acc, res: (acc + res,), extras=(x,),
                            name=f"mlp_out_{li}", **tiles), None
    else:
        x_out, h_next = _mm(r, w_out, pro_a=lambda t: t * t, epi=_add_then_rms, extras=(x, g_next),
                            out_dtypes=(F32, BF16), name=f"mlp_out_{li}", **tiles)
    return x_out, h_next, (h2, r)


def _mlp_bwd(gx, gxb, x, g, w_in, w_out, saved, li, nl, bufs):
    h2, r = saved
    D, F = w_in.shape
    lh = nl // 2
    da = _mm(gxb, w_out, tb=True, out_dtypes=(BF16,),
             epi=lambda acc, rt: (acc * 2.0 * rt.astype(F32),), extras=(r,), tm=512, tn=F, name=f"mlp_dact_{li}")
    buf_in, buf_out = bufs if bufs is not None else (None, None)
    d_w_out = _mm(r, gxb, ta=True, pro_a=lambda t: t * t, tm=1024, tn=1024, tk=4096, out_dtypes=(BF16,),
                  out=_grad_layer_rows(li % lh, lh, F // N_CHIPS, D, buf_out), name=f"mlp_dwout_{li}")
    d_w_in = _mm(h2, da, ta=True, tm=1024, tn=1024, tk=4096, out_dtypes=(BF16,),
                 out=_grad_layer_cols(li % lh, lh, D, F // N_CHIPS, buf_in), name=f"mlp_dwin_{li}")
    gx_mid, gxb_mid, dg = _mm_rms_bwd(da, getattr(w_in, "plain", w_in), x, g, gx, tm=512, tk=F, name=f"mlp_dh_{li}")
    return gx_mid, gxb_mid, dg, (d_w_in, d_w_out)


def _local_step(x3, tgt3, p, layer_weights, token=None, grads_done=lambda group: None):
    B, S, D = x3.shape
    T = B * S
    x = x3.reshape(T, D)
    grads = {}
    row = lambda v: v.reshape(1, -1)
    p = dict(p)
    nl = p["norm_mlp"].shape[0]
    mlp_in, mlp_out = [None] * nl, [None] * nl

    def fetch(li, after):
        wl = dict(layer_weights(li, after))
        mlp_in[li], mlp_out[li] = wl.pop("mlp_w_in"), wl.pop("mlp_w_out")
        p.update(wl)

    g0 = row(p["norm_mix"][0])
    if token is not None:
        g0 = g0 + token[:1, :1]
    h0 = _rms_fwd(x, g0, "rms_mix_0")
    s5_args = (p["ssm_a_re"][0], p["ssm_a_im"][0], p["ssm_b_re"][0], p["ssm_b_im"][0],
               p["ssm_c_re"][0], p["ssm_c_im"][0], p["ssm_log_dt"][0])
    s5_exp, s5_vjp = jax.vjp(_s5_prep, *s5_args)
    abr, abi, bre, bim, cre, cim = s5_exp
    bre_b, bim_b, cre_b, cim_b = (t.astype(BF16) for t in (bre, bim, cre, cim))
    d_skip = p["ssm_d"]
    ypre, yb, sxr, sxi, ser, sei = _s5_fwd(h0, abr, abi, bre_b, bim_b, cre_b, cim_b, d_skip, B=B, L=S5_CHUNK,
                                           name="s5_fwd")
    fetch(0, yb)
    w_glu = p["ssm_w_glu"]
    z0 = _mm(yb, w_glu, tm=2048, name="s5_glu_mm")
    gm = [row(p["norm_mlp"][i]) for i in range(nl)]
    g1, g2, g3 = (row(p["norm_mix"][i]) for i in range(1, nl))
    x_mid0, hm0 = _rowwise(lambda z, xr, g: _add_then_rms(_f_glu(z)[0], xr, g), [z0, x], [gm[0]], (F32, BF16),
                           tr=TR, name="s5_glu")
    x1, h1, mlp_saved0 = _mlp_fwd(x_mid0, hm0, mlp_in[0], mlp_out[0], g1, 0)

    fetch(1, h1)
    z1 = _mm(h1, p["conv_w_pw1"], tm=2048, name="conv_pw1")
    zg = _rowwise(_f_bias_glu, [z1], [p["conv_b_pw1"]], (F32,), tr=TR, name="conv_glu")[0]
    zp = jnp.pad(zg.reshape(B, S, D), ((0, 0), (CONV_HALO, 0), (0, 0)))
    w_dw = jnp.pad(p["conv_w_dw"], ((0, 32 - CONV_WIDTH), (0, 0)))
    yc = _conv_fwd(zp, w_dw, R=256, tc=128, name="conv_dw").reshape(T, D)
    ln_par = [p["conv_b_dw"], p["conv_ln_g"], p["conv_ln_b"]]
    qc = _rowwise(_f_ln_silu, [yc], ln_par, (BF16,), tr=TR, name="conv_ln_silu")[0]
    x_mid1, hm1 = _mm(qc, p["conv_w_pw2"], epi=lambda acc, bias, res, g: _add_then_rms(acc + bias, res, g),
                      extras=(p["conv_b_pw2"], x1, gm[1]), out_dtypes=(F32, BF16), tn=D, name="conv_pw2")
    x2, h2, mlp_saved1 = _mlp_fwd(x_mid1, hm1, mlp_in[1], mlp_out[1], g2, 1)

    fetch(2, h2)
    z2 = _mm(h2, p["gmlp_w_in"], tm=2048, name="gmlp_in")
    gl_par = [p["gmlp_ln_g"], p["gmlp_ln_b"]]
    gu, gvn = _rowwise(_f_gelu_ln, [z2], gl_par, (F32, F32), tr=TR, name="gmlp_gelu_ln")
    causal = jnp.tril(jnp.ones((GMLP_CHUNK, GMLP_CHUNK), dtype=bool))
    ws_b = jnp.where(causal[None], p["gmlp_w_s"][0], 0.0).astype(BF16)
    bcol = jnp.pad(p["gmlp_b_s"][0].T, ((0, 0), (0, 128 - GMLP_HEADS)))
    uv = _gmlp_fwd(gu, gvn, ws_b, bcol, nck=4, name="gmlp_spatial")
    x_mid2, hm2 = _mm(uv, p["gmlp_w_out"], epi=_add_then_rms, extras=(x2, gm[2]), out_dtypes=(F32, BF16), tn=D,
                      name="gmlp_out")
    x3_, h3, mlp_saved2 = _mlp_fwd(x_mid2, hm2, mlp_in[2], mlp_out[2], g3, 2)

    fetch(3, h3)
    ng = len(ATT_DILS)
    att_in, o_tok, l_tok, lses = [], [], [], []
    offs = (0, PAIRS, 2 * PAIRS)
    for gi, dil in enumerate(ATT_DILS):
        w_g = _ColBlocks(p["attn_w_qkv_plain"], gi, ng, 3, ATT_W)
        arr = _mm(h3, w_g, out_dtypes=(BF16,), tm=4096, tn=ATT_W, name=f"attn_qkv_{gi}")
        arr = _deinterleave(arr, B, S, dil)
        att_in.append((arr, offs))
        og, lg = _att_fwd(arr, offs, nb=S // dil // ATT_BLK, nbk=ATT_BLOCKS_PER_STEP, name=f"attn_fwd_{gi}")
        lses.append(lg)
        o_tok.append(_interleave(og, B, S, dil))
        l_tok.append(_stats_to_tokens(lg, B, S, dil))
    merged2 = _rowwise(_f_merge, o_tok + l_tok, [], (BF16,), tr=TR, name="attn_merge")[0]
    x_mid3, hm3 = _mm(merged2, p["attn_w_o_plain"], epi=_add_then_rms, extras=(x3_, gm[3]), out_dtypes=(F32, BF16),
                      tn=D, name="attn_out")
    x4, _, mlp_saved3 = _mlp_fwd(x_mid3, hm3, mlp_in[3], mlp_out[3], None, 3)

    loss_part, gx, gxb, dgf = _loss_head(x4, tgt3.reshape(T, D), row(p["norm_final"]), tr=TR, name="loss_head")
    grads["norm_final"] = dgf.reshape(-1)
    d_norm_mix, d_norm_mlp = [None] * 4, [None] * 4
    Dq = D // N_CHIPS

    gx, gxb, d_norm_mlp[3], mlp_hi = _mlp_bwd(
        gx, gxb, x_mid3, row(p["norm_mlp"][3]), mlp_in[3], mlp_out[3], mlp_saved3, 3, nl, None)
    dmerged = _mm(gxb, p["attn_w_o"], tb=True, name="attn_dmerged")
    grads["attn_w_o"] = _mm(merged2, gxb, ta=True, tm=256, tn=256, tk=4096, out_dtypes=(BF16,), out=_grad_cols(ATT_W, Dq), name="attn_dwo")
    dml, _ = _rowwise_vjp(_f_merge, o_tok + l_tok, [], [dmerged], [F32] * 6, tr=TR, name="attn_merge_bwd")
    pieces = [[None] * ng for _ in range(3)]
    for gi, dil in enumerate(ATT_DILS):
        arr, offs = att_in[gi]
        dqkv_g = _att_bwd(arr, offs, _deinterleave(dml[gi], B, S, dil), lses[gi],
                          _stats_from_tokens(dml[ng + gi], B, S, dil), nb=S // dil // ATT_BLK,
                          nbk=ATT_BLOCKS_PER_STEP,
                          name=f"attn_bwd_{gi}")
        for i in range(3):
            pieces[i][gi] = _interleave(dqkv_g[i], B, S, dil)
    dqkv = jnp.concatenate([pieces[i][gi] for i in range(3) for gi in range(ng)], axis=1)
    qkv_w = 3 * ng * ATT_W
    grads["attn_w_qkv"] = _mm(h3, dqkv, ta=True, tm=512, tn=1152, tk=4096, out_dtypes=(BF16,), out=_grad_cols(D, qkv_w // N_CHIPS),
                              name="attn_dwqkv")
    tok = grads_done({n: grads[n] for n in ("attn_w_qkv", "attn_w_o")})
    gx, gxb, d_norm_mix[3] = _mm_rms_bwd(dqkv, p["attn_w_qkv_plain"], x3_, g3, gx, tm=512, tk=qkv_w, after=tok, name="attn_dh")

    gx, gxb, d_norm_mlp[2], mlp_hi = _mlp_bwd(
        gx, gxb, x_mid2, row(p["norm_mlp"][2]), mlp_in[2], mlp_out[2], mlp_saved2, 2, nl, mlp_hi)
    tok = grads_done({"mlp_w_in": (1, mlp_hi[0]), "mlp_w_out": (1, mlp_hi[1])})
    duv = _mm(gxb, p["gmlp_w_out"], tb=True, after=tok, name="gmlp_duv")
    grads["gmlp_w_out"] = _mm(uv, gxb, ta=True, tm=128, tn=1024, tk=4096, out_dtypes=(BF16,), out=_grad_rows(Dq, D), name="gmlp_dwout")
    du, dvn, dws, dbcol = _gmlp_bwd(duv, gu, gvn, ws_b, bcol, nck=4, name="gmlp_spatial_bwd")
    grads["gmlp_w_s"] = jnp.where(causal[None], dws, 0.0)[None]
    grads["gmlp_b_s"] = dbcol[:, :GMLP_HEADS].T[None]
    (dz2,), (dlg, dlb_) = _rowwise_vjp(_f_gelu_ln, [z2], gl_par, [du, dvn], [BF16], tr=TR, name="gmlp_gelu_ln_bwd")
    grads["gmlp_ln_g"], grads["gmlp_ln_b"] = dlg, dlb_
    grads["gmlp_w_in"] = _mm(h2, dz2, ta=True, tm=512, tn=512, tk=4096, out_dtypes=(BF16,), out=_grad_cols(D, 2 * Dq), name="gmlp_dwin")
    tok = grads_done({n: grads[n] for n in ("gmlp_w_in", "gmlp_w_out")})
    gx, gxb, d_norm_mix[2] = _mm_rms_bwd(dz2, getattr(p["gmlp_w_in"], "plain", p["gmlp_w_in"]), x2, g2, gx, tm=512, tk=2 * D, after=tok,
                                         name="gmlp_dh")

    gx, gxb, d_norm_mlp[1], mlp_lo = _mlp_bwd(
        gx, gxb, x_mid1, row(p["norm_mlp"][1]), mlp_in[1], mlp_out[1], mlp_saved1, 1, nl, None)
    dqc = _mm(gxb, p["conv_w_pw2"], tb=True, name="conv_dq")
    grads["conv_w_pw2"] = _mm(qc, gxb, ta=True, tm=128, tn=1024, tk=4096, out_dtypes=(BF16,), out=_grad_rows(Dq, D), name="conv_dwpw2")
    _, (db2,) = _rowwise_vjp(lambda t, b: (t + b,), [gx], [p["conv_b_pw2"]], [gx], [None], tr=TR, name="conv_db2")
    grads["conv_b_pw2"] = db2
    (dyc,), (dbdw, dcg, dcb) = _rowwise_vjp(_f_ln_silu, [yc], ln_par, [dqc], [F32], tr=TR, name="conv_ln_silu_bwd")
    grads["conv_b_dw"], grads["conv_ln_g"], grads["conv_ln_b"] = dbdw, dcg, dcb
    dyp = jnp.pad(dyc.reshape(B, S, D), ((0, 0), (0, CONV_HALO), (0, 0)))
    dzg, dwdw = _conv_bwd(zp, dyp, w_dw, R=256, tc=128, name="conv_dw_bwd")
    grads["conv_w_dw"] = dwdw[:CONV_WIDTH][None]
    (dz1,), (db1,) = _rowwise_vjp(_f_bias_glu, [z1], [p["conv_b_pw1"]], [dzg.reshape(T, D)], [BF16], tr=TR,
                                  name="conv_glu_bwd")
    grads["conv_b_pw1"] = db1
    grads["conv_w_pw1"] = _mm(h1, dz1, ta=True, tm=512, tn=512, tk=4096, out_dtypes=(BF16,), out=_grad_cols(D, 2 * Dq), name="conv_dwpw1")
    tok = grads_done({n: grads[n] for n in ("conv_w_pw1", "conv_w_pw2")})
    gx, gxb, d_norm_mix[1] = _mm_rms_bwd(dz1, getattr(p["conv_w_pw1"], "plain", p["conv_w_pw1"]), x1, g1, gx, tm=512, tk=2 * D,
                                         after=tok, name="conv_dh")

    gx, gxb, d_norm_mlp[0], mlp_lo = _mlp_bwd(
        gx, gxb, x_mid0, row(p["norm_mlp"][0]), mlp_in[0], mlp_out[0], mlp_saved0, 0, nl, mlp_lo)
    tok = grads_done({"mlp_w_in": (0, mlp_lo[0]), "mlp_w_out": (0, mlp_lo[1])})
    (dz0,), _ = _rowwise_vjp(_f_glu, [z0], [], [gx], [BF16], after=tok, tr=TR, name="s5_glu_bwd")
    grads["ssm_w_glu"] = _mm(yb, dz0, ta=True, tm=512, tn=512, tk=4096, out_dtypes=(BF16,), out=_grad_cols(D, 2 * Dq), name="s5_dwglu")
    tok = grads_done({"ssm_w_glu": grads["ssm_w_glu"]})
    dypre = _mm(dz0, getattr(w_glu, "plain", w_glu), tb=True, tm=512, tk=2 * D,
                epi=lambda acc, yp: (jax.vjp(lambda t: jax.nn.gelu(t), yp)[1](acc)[0],), extras=(ypre,), name="s5_dypre")
    dh0, dbre, dbim, dcre, dcim, dabr, dabi, dd = _s5_bwd(
        dypre, h0, sxr, sxi, ser, sei, abr, abi, bre_b, bim_b, cre_b, cim_b, d_skip, B=B, L=S5_CHUNK, name="s5_bwd")
    s5_grads = s5_vjp((dabr, dabi, dbre, dbim, dcre, dcim))
    for nm, gv in zip(("ssm_a_re", "ssm_a_im", "ssm_b_re", "ssm_b_im", "ssm_c_re", "ssm_c_im", "ssm_log_dt"), s5_grads):
        grads[nm] = gv[None]
    grads["ssm_d"] = dd
    gx, _, d_norm_mix[0] = _rms_bwd(x, g0, dh0, gx, "rms_mix_bwd_0", after=tok)

    grads["norm_mix"] = jnp.concatenate(d_norm_mix, axis=0)
    grads["norm_mlp"] = jnp.concatenate(d_norm_mlp, axis=0)
    grads["mlp_w_in"], grads["mlp_w_out"] = (mlp_lo[0], mlp_hi[0]), (mlp_lo[1], mlp_hi[1])
    return loss_part, gx.reshape(B, S, D), grads


WEIGHTS = ['norm_mix', 'norm_mlp', 'norm_final', 'ssm_a_re', 'ssm_a_im', 'ssm_b_re', 'ssm_b_im', 'ssm_c_re',
           'ssm_c_im', 'ssm_d', 'ssm_log_dt', 'ssm_w_glu', 'conv_w_pw1', 'conv_b_pw1', 'conv_w_dw', 'conv_b_dw',
           'conv_ln_g', 'conv_ln_b', 'conv_w_pw2', 'conv_b_pw2', 'gmlp_w_in', 'gmlp_ln_g', 'gmlp_ln_b', 'gmlp_w_s',
           'gmlp_b_s', 'gmlp_w_out', 'attn_w_qkv', 'attn_w_o', 'mlp_w_in', 'mlp_w_out']
BIG_AXIS = {'ssm_w_glu': -1, 'conv_w_pw1': -1, 'conv_w_pw2': -2, 'gmlp_w_in': -1, 'gmlp_w_out': -2,
            'attn_w_qkv': -1, 'attn_w_o': -1, 'mlp_w_in': -1, 'mlp_w_out': -2}
BIG = list(BIG_AXIS)
LAYER_MIXER_WEIGHTS = (('ssm_w_glu',), ('conv_w_pw1', 'conv_w_pw2'), ('gmlp_w_in', 'gmlp_w_out'), ('attn_w_qkv', 'attn_w_o'))
SMALL_SHARDED = ['conv_b_pw1', 'conv_w_dw', 'conv_b_dw', 'conv_ln_g', 'conv_ln_b', 'conv_b_pw2', 'gmlp_ln_g', 'gmlp_ln_b']
SMALL_REPL = [n for n in WEIGHTS if n not in BIG_AXIS and n not in SMALL_SHARDED]
SMALL = SMALL_REPL + SMALL_SHARDED
LANES = 128


def _pack(arrs, cols, row_mult):
    flat = jnp.concatenate([a.reshape(-1) for a in arrs])
    per = cols * row_mult
    n = -(-flat.shape[0] // per) * per
    return jnp.pad(flat, (0, n - flat.shape[0])).reshape(n // cols, cols)


def _unpack(flat2d, shapes):
    flat = flat2d.reshape(-1)
    out, off = [], 0
    for s in shapes:
        n = int(np.prod(s))
        out.append(flat[off:off + n].reshape(s))
        off += n
    return out


def kernel(x, norm_mix, norm_mlp, norm_final, ssm_a_re, ssm_a_im, ssm_b_re, ssm_b_im, ssm_c_re, ssm_c_im, ssm_d, ssm_log_dt, ssm_w_glu, conv_w_pw1, conv_b_pw1, conv_w_dw, conv_b_dw, conv_ln_g, conv_ln_b, conv_w_pw2, conv_b_pw2, gmlp_w_in, gmlp_ln_g, gmlp_ln_b, gmlp_w_s, gmlp_b_s, gmlp_w_out, attn_w_qkv, attn_w_o, mlp_w_in, mlp_w_out, loss_target, m_norm_mix, m_norm_mlp, m_norm_final, m_ssm_a_re, m_ssm_a_im, m_ssm_b_re, m_ssm_b_im, m_ssm_c_re, m_ssm_c_im, m_ssm_d, m_ssm_log_dt, m_ssm_w_glu, m_conv_w_pw1, m_conv_b_pw1, m_conv_w_dw, m_conv_b_dw, m_conv_ln_g, m_conv_ln_b, m_conv_w_pw2, m_conv_b_pw2, m_gmlp_w_in, m_gmlp_ln_g, m_gmlp_ln_b, m_gmlp_w_s, m_gmlp_b_s, m_gmlp_w_out, m_attn_w_qkv, m_attn_w_o, m_mlp_w_in, m_mlp_w_out, v_norm_mix, v_norm_mlp, v_norm_final, v_ssm_a_re, v_ssm_a_im, v_ssm_b_re, v_ssm_b_im, v_ssm_c_re, v_ssm_c_im, v_ssm_d, v_ssm_log_dt, v_ssm_w_glu, v_conv_w_pw1, v_conv_b_pw1, v_conv_w_dw, v_conv_b_dw, v_conv_ln_g, v_conv_ln_b, v_conv_w_pw2, v_conv_b_pw2, v_gmlp_w_in, v_gmlp_ln_g, v_gmlp_ln_b, v_gmlp_w_s, v_gmlp_b_s, v_gmlp_w_out, v_attn_w_qkv, v_attn_w_o, v_mlp_w_in, v_mlp_w_out):
    args = dict(locals())
    w = {n: args[n] for n in WEIGHTS}
    m = {n: args["m_" + n] for n in WEIGHTS}
    v = {n: args["v_" + n] for n in WEIGHTS}
    chip = 2 * lax.axis_index("x") + lax.axis_index("y")
    core = lax.axis_index("c")

    sm_shapes = [w[n].shape for n in SMALL_SHARDED]
    sflat = _pack([w[n] for n in SMALL_SHARDED], LANES, 8)
    rs = sflat.shape[0]
    sall = _allgather8(sflat, name="gather_small")
    started, token = [], sall
    sall = sall.reshape(8, rs, LANES)
    for li, mixer in enumerate(LAYER_MIXER_WEIGHTS):
        names = list(mixer) + ["mlp_w_in", "mlp_w_out"]
        shards = [w[n][0] for n in mixer] + [w["mlp_w_in"][li], w["mlp_w_out"][li]]
        halves = [s.astype(BF16).reshape((2, s.shape[0] // 2) + s.shape[1:]) for s in shards]
        send_sems, recv_sems, halves, lands, token = _gather_start(halves, token, name=f"gather_start_{li}")
        started.append((names, (send_sems, recv_sems, halves, lands)))

    def layer_weights(li, after):
        names, st = started[li]
        halves, lands = _gather_wait(st, after, name=f"gather_wait_{li}")
        out = {}
        for n, h, arr in zip(names, halves, lands):
            arr = lax.dynamic_update_index_in_dim(arr, h, chip, axis=0)
            arr = arr.reshape((N_CHIPS, arr.shape[1] * arr.shape[2]) + arr.shape[3:])
            if BIG_AXIS[n] == -1:
                out[n] = _Stored(arr, "cols")
                if n in ("attn_w_qkv", "attn_w_o"):
                    out[n + "_plain"] = jnp.swapaxes(arr, 0, 1).reshape(arr.shape[1], -1)
                out[n].plain = jnp.swapaxes(arr, 0, 1).reshape(arr.shape[1], -1)
            else:
                out[n] = arr.reshape(-1, arr.shape[-1])
        return out

    p = {}
    per_chip = [_unpack(sall[2 * k], sm_shapes) for k in range(N_CHIPS)]
    for i, n in enumerate(SMALL_SHARDED):
        p[n] = jnp.concatenate([per_chip[k][i] for k in range(N_CHIPS)], axis=-1)
    for n in SMALL_REPL:
        p[n] = w[n]
    p['conv_w_dw'] = p['conv_w_dw'][0]

    in_flight, arrived, n_rounds = [], {}, [0]

    def finish_round(after):
        k, names, plans, st = in_flight.pop(0)
        srcs, lands = _reduce_wait(st, plans, after, name=f"grads_wait_{k}")
        for n, plan, src, land in zip(names, plans, srcs, lands):
            arrived.setdefault(n, []).append((plan, src, land))

    def grads_done(group):
        names, srcs, plans, lands = [], [], [], []
        for n, v in group.items():
            if isinstance(v, tuple):
                src, plan = v[1].reshape(1, N_CHIPS, -1, v[1].shape[-1]), ((0, v[0]),)
                while any(n in rd[1] for rd in in_flight):
                    finish_round(src)
            else:
                src, plan = v.reshape(2, N_CHIPS, -1, v.shape[-1]), ((0, 0), (1, 1))
            land = arrived[n][-1][2] if n in arrived else lax.empty((2 * N_CHIPS,) + src.shape[2:], BF16)
            names.append(n), srcs.append(src), plans.append(plan), lands.append(land)
        st = _reduce_start(srcs, lands, plans, None, name=f"grads_start_{n_rounds[0]}")
        in_flight.append((n_rounds[0], names, plans, st[:4]))
        n_rounds[0] += 1
        return st[4]

    loss_part, grad_x, g = _local_step(x, loss_target, p, layer_weights, token, grads_done)
    loss = lax.psum(loss_part[0, 0], ("x", "y", "c"))

    my_id = 2 * chip + core
    while in_flight:
        finish_round(grad_x)
    totals = []
    for n in BIG:
        own = None
        for plan, src, land in arrived[n]:
            slab = lax.dynamic_index_in_dim(src, chip, axis=1, keepdims=False)
            if len(plan) == 2:
                own = lax.dynamic_index_in_dim(slab, core, axis=0, keepdims=False)
            else:
                own = slab[0] if own is None else jnp.where(core == plan[0][1], slab[0], own)
        totals.append(_sum8(arrived[n][-1][2], own, my_id, name="owner_sum_" + n))
    shared = _share_halves(totals, name="grads_share_halves")
    big_grads = {}
    for n, arr, t in zip(BIG, shared, totals):
        arr = lax.dynamic_update_index_in_dim(arr, t[None], core, axis=0)
        big_grads[n] = arr.reshape(w[n].shape)

    small_full_shapes = [g[n].shape for n in SMALL]
    gs = _pack([g[n] for n in SMALL], LANES, 8)
    rg = gs.shape[0]
    gs_all = _allgather8(gs, name="gather_small_grads").reshape(8, rg, LANES)
    gs_sum = _rowwise(lambda *a: (functools.reduce(lambda s, t: s + t, a),), [gs_all[k] for k in range(8)], [], (F32,),
                      tr=rg, name="small_grads_sum")[0]
    small_grads = dict(zip(SMALL, _unpack(gs_sum, small_full_shapes)))
    for n in SMALL:
        small_grads[n] = small_grads[n].reshape(p_shape_full(w[n], -1 if n in SMALL_SHARDED else None))
    for n in SMALL_SHARDED:
        width = w[n].shape[-1]
        small_grads[n] = lax.dynamic_slice_in_dim(small_grads[n], chip * width, width, axis=-1)

    grad, delta, new_m, new_v = {}, {}, {}, {}
    for n in BIG:
        shape = w[n].shape
        two_d = lambda t: t.reshape(-1, shape[-1])
        grad[n] = big_grads[n]
        d_, m_, v_ = _adamw(two_d(w[n]), two_d(grad[n]), two_d(m[n]), two_d(v[n]), name="adamw_" + n)
        delta[n], new_m[n], new_v[n] = d_.reshape(shape), m_.reshape(shape), v_.reshape(shape)
    for n in SMALL:
        shape = w[n].shape
        two_d = lambda t: t.reshape(-1, shape[-1])
        grad[n] = small_grads[n]
        d_, m_, v_ = _adamw(two_d(w[n]), two_d(grad[n]), two_d(m[n]), two_d(v[n]), name="adamw_" + n)
        delta[n], new_m[n], new_v[n] = d_.reshape(shape), m_.reshape(shape), v_.reshape(shape)

    return (loss, grad_x, *[grad[n] for n in WEIGHTS], *[delta[n] for n in WEIGHTS],
            *[new_m[n] for n in WEIGHTS], *[new_v[n] for n in WEIGHTS])


def p_shape_full(shard, axis):
    s = list(shard.shape)
    if axis is not None:
        s[axis] *= N_CHIPS
    return tuple(s)
```

```python
import functools
import math

import jax
import jax.numpy as jnp
import numpy as np
from jax import lax
from jax.experimental import pallas as pl
from jax.experimental.pallas import tpu as pltpu

F32 = jnp.float32
BF16 = jnp.bfloat16
MESH = pl.DeviceIdType.MESH

EPS = 1e-6
SSM_GROUP = 16
SSM_STATE = 64
CONV_WIDTH = 31
CONV_HALO = 32
GMLP_CHUNK = 128
GMLP_HEADS = 4
ATT_DILS = (1, 4, 16)
ATT_BLK = 128
ATT_HEADS = 8
HEAD_DIM = 64
ATT_W = ATT_HEADS * HEAD_DIM
N_CHIPS = 4
ADAM_LR, ADAM_B1, ADAM_B2, ADAM_EPS, ADAM_WD, ADAM_STEP = 1e-3, 0.9, 0.999, 1e-8, 0.01, 10

VMEM_BYTES_V7X = 64 * 1024 * 1024
VMEM_LIMIT = VMEM_BYTES_V7X - 8 * 1024 * 1024
MASK_VALUE = -1e30
LANE_TILE = 128


def _cp(sem=None):
    return pltpu.CompilerParams(dimension_semantics=sem, vmem_limit_bytes=VMEM_LIMIT)


def _pick_tile(total, target):
    for cand in range(min(target, total) // LANE_TILE * LANE_TILE, 0, -LANE_TILE):
        if total % cand == 0:
            return cand
    return total


class _Stored:
    def __init__(self, arr, kind="plain", lead=()):
        self.arr, self.kind, self.lead = arr, kind, tuple(lead)
        r, c = arr.shape[-2:]
        self.shape = (r, c * N_CHIPS) if kind == "cols" else (r * N_CHIPS, c) if kind == "rows" else (r, c)

    def spec(self, br, bc, rc_of):
        lead, nl = self.lead, len(self.lead)
        if self.kind == "plain":
            return pl.BlockSpec((None,) * nl + (br, bc), lambda i, j, k: (*lead, *rc_of(i, j, k)))
        if self.kind == "cols":
            per = self.arr.shape[-1] // bc
            assert per * bc == self.arr.shape[-1]

            def imap(i, j, k):
                r, c = rc_of(i, j, k)
                return (c // per, *lead, r, c % per)
        else:
            per = self.arr.shape[-2] // br
            assert per * br == self.arr.shape[-2]

            def imap(i, j, k):
                r, c = rc_of(i, j, k)
                return (r // per, *lead, r % per, c)
        return pl.BlockSpec((None,) * (nl + 1) + (br, bc), imap)


class _ColBlocks:
    kind = "colblocks"

    def __init__(self, arr, first, stride, count, width):
        self.arr, self.first, self.stride, self.width = arr, first, stride, width
        self.shape = (arr.shape[0], count * width)

    def spec(self, br, bc, rc_of):
        per = self.width // bc
        assert per * bc == self.width

        def imap(i, j, k):
            r, c = rc_of(i, j, k)
            return (r, (self.first + (c // per) * self.stride) * per + c % per)
        return pl.BlockSpec((br, bc), imap)


def _mm(a, b, *, ta=False, tb=False, out_dtypes=(F32,), tm=1024, tn=1024, tk=1024,
        pro_a=None, pro_b=None, epi=None, extras=(), n_row_sums=0, out=None, after=None, name):
    if ta:
        K, M = a.shape
    else:
        M, K = a.shape
    if not isinstance(b, (_Stored, _ColBlocks)):
        b = _Stored(b)
    N, Kb = b.shape if tb else b.shape[::-1]
    assert K == Kb, (a.shape, b.shape, ta, tb)
    col_unit = b.width if b.kind == "colblocks" else b.arr.shape[-1] if b.kind == "cols" else b.shape[1]
    row_unit = b.arr.shape[-2] if b.kind == "rows" else b.shape[0]
    n_unit, k_unit = (row_unit, col_unit) if tb else (col_unit, row_unit)
    m_unit = M
    if out is not None:
        m_unit, n_unit = out[4], math.gcd(n_unit, out[5])
    tm, tn, tk = _pick_tile(m_unit, tm), _pick_tile(n_unit, tn), _pick_tile(k_unit, tk)
    assert not n_row_sums or tn == N
    nk = K // tk
    a_spec = (pl.BlockSpec((tk, tm), lambda i, j, k: (k, i)) if ta
              else pl.BlockSpec((tm, tk), lambda i, j, k: (i, k)))
    b_spec = b.spec(tn, tk, lambda i, j, k: (j, k)) if tb else b.spec(tk, tn, lambda i, j, k: (k, j))
    ex_specs = []
    for e in extras:
        if e.shape[0] == 1:
            ex_specs.append(pl.BlockSpec((1, tn), lambda i, j, k: (0, j)))
        else:
            assert e.shape == (M, N), (e.shape, M, N)
            ex_specs.append(pl.BlockSpec((tm, tn), lambda i, j, k: (i, j)))
    dims = (((0 if ta else 1,), (1 if tb else 0,)), ((), ()))
    n_ex, n_out = len(extras), len(out_dtypes)
    direct = epi is None and n_out == 1 and out_dtypes[0] == F32
    use_acc = nk > 1 and not direct
    operands, aliases, alias_specs = [a, b.arr, *extras], {}, []
    if after is not None:
        operands.append(after)
        alias_specs.append(pl.BlockSpec(memory_space=pl.ANY))
    if out is None:
        n_tile_out = n_out - n_row_sums
        out_specs = ([pl.BlockSpec((tm, tn), lambda i, j, k: (i, j))] * n_tile_out
                     + [pl.BlockSpec((1, tn), lambda i, j, k: (0, j))] * n_row_sums)
        out_shape = ([jax.ShapeDtypeStruct((M, N), dt) for dt in out_dtypes[:n_tile_out]]
                     + [jax.ShapeDtypeStruct((1, N), dt) for dt in out_dtypes[n_tile_out:]])
    else:
        shape, block_fn, imap_fn, alias = out[:4]
        assert n_out == 1
        out_specs = [pl.BlockSpec(block_fn(tm, tn), imap_fn(tm, tn))]
        out_shape = [jax.ShapeDtypeStruct(shape, out_dtypes[0])]
        if alias is not None:
            operands.append(alias)
            aliases = {len(operands) - 1: 0}
            alias_specs.append(pl.BlockSpec(memory_space=pl.ANY))
    n_in = len(operands)

    def finish(r, ex, outs, first_row_tile):
        res = epi(r, *[e[...] for e in ex]) if epi is not None else (r,)
        n_tile_out = n_out - n_row_sums
        for o, v in zip(outs[:n_tile_out], res):
            o[...] = v.astype(o.dtype)
        for o, v in zip(outs[n_tile_out:], res[n_tile_out:]):
            @pl.when(first_row_tile)
            def _(o=o):
                o[...] = jnp.zeros_like(o)
            o[...] += v

    def body(*refs):
        a_ref, b_ref = refs[:2]
        ex = refs[2:2 + n_ex]
        outs = refs[n_in:n_in + n_out]
        first_row_tile = pl.program_id(0) == 0
        at, bt = a_ref[...], b_ref[...]
        if pro_a is not None:
            at = pro_a(at)
        if pro_b is not None:
            bt = pro_b(bt)
        part = lax.dot_general(at, bt, dims, preferred_element_type=F32)
        if nk == 1:
            finish(part, ex, outs, first_row_tile)
            return
        acc = refs[-1] if use_acc else outs[0]
        k = pl.program_id(2)

        @pl.when(k == 0)
        def _():
            acc[...] = part

        @pl.when(k > 0)
        def _():
            acc[...] += part

        if use_acc:
            @pl.when(k == nk - 1)
            def _():
                finish(acc[...], ex, outs, first_row_tile)

    res = pl.pallas_call(
        body, grid=(M // tm, N // tn, nk),
        in_specs=[a_spec, b_spec] + ex_specs + alias_specs,
        out_specs=out_specs, out_shape=out_shape,
        scratch_shapes=[pltpu.VMEM((tm, tn), F32)] if use_acc else [],
        input_output_aliases=aliases,
        compiler_params=_cp(("arbitrary" if n_row_sums else "parallel", "parallel", "arbitrary")), name=name,
    )(*operands)
    return res[0] if n_out == 1 else res


def _pick_rows(total, target):
    for cand in range(min(target, total) // 8 * 8, 0, -8):
        if total % cand == 0:
            return cand
    return total


def _rowwise(f, rows, params, out_dtypes, *, tr, name):
    T = rows[0].shape[0]
    tr = _pick_rows(T, tr)
    nr, npar = len(rows), len(params)
    blk = [jax.ShapeDtypeStruct((tr, r.shape[1]), F32) for r in rows]
    blk += [jax.ShapeDtypeStruct(p.shape, F32) for p in params]
    out_avals = jax.eval_shape(f, *blk)

    def body(*refs):
        res = f(*[r[...].astype(F32) for r in refs[:nr + npar]])
        for o, v in zip(refs[nr + npar:], res):
            o[...] = v.astype(o.dtype)

    out = pl.pallas_call(
        body, grid=(T // tr,),
        in_specs=[pl.BlockSpec((tr, r.shape[1]), lambda i: (i, 0)) for r in rows]
        + [pl.BlockSpec(p.shape, lambda i, nd=p.ndim: (0,) * nd) for p in params],
        out_specs=[pl.BlockSpec((tr, o.shape[1]), lambda i: (i, 0)) for o in out_avals],
        out_shape=[jax.ShapeDtypeStruct((T, o.shape[1]), dt) for o, dt in zip(out_avals, out_dtypes)],
        compiler_params=_cp(("parallel",)), name=name,
    )(*rows, *params)
    return out


def _rowwise_vjp(f, rows, params, cots, drow_dtypes, *, adds=None, after=None, tr, name):
    adds = adds or {}
    T = rows[0].shape[0]
    tr = _pick_rows(T, tr)
    nr, npar, nc = len(rows), len(params), len(cots)
    want, want_dt = [], []
    for i, dt in enumerate(drow_dtypes):
        for one in (dt if isinstance(dt, tuple) else (dt,)):
            if one is not None:
                want.append(i)
                want_dt.append(one)
    add_idx = sorted(set(i for i in want if i in adds))
    add_arrays = [adds[i] for i in add_idx]
    na = len(add_arrays)
    extra = [] if after is None else [after]

    def body(*refs):
        ins = [r[...].astype(F32) for r in refs[:nr + npar]]
        cvals = [r[...].astype(F32) for r in refs[nr + npar:nr + npar + nc]]
        avals = refs[nr + npar + nc:nr + npar + nc + na]
        outs = refs[nr + npar + nc + na + len(extra):]
        _, vjp = jax.vjp(f, *ins)
        grads = vjp(tuple(cvals))
        for o, i in zip(outs[:len(want)], want):
            g = grads[i]
            if i in adds:
                g = g + avals[add_idx.index(i)][...].astype(F32)
            o[...] = g.astype(o.dtype)
        step = pl.program_id(0)
        for o, g in zip(outs[len(want):], grads[nr:]):
            @pl.when(step == 0)
            def _(o=o):
                o[...] = jnp.zeros_like(o)
            o[...] += g

    rspec = lambda r: pl.BlockSpec((tr, r.shape[1]), lambda i: (i, 0))
    pspec = lambda p: pl.BlockSpec(p.shape, lambda i, nd=p.ndim: (0,) * nd)
    out = pl.pallas_call(
        body, grid=(T // tr,),
        in_specs=[rspec(r) for r in rows] + [pspec(p) for p in params] + [rspec(c) for c in cots]
        + [rspec(a) for a in add_arrays] + [pl.BlockSpec(memory_space=pl.ANY)] * len(extra),
        out_specs=[rspec(rows[i]) for i in want] + [pspec(p) for p in params],
        out_shape=[jax.ShapeDtypeStruct(rows[i].shape, dt) for i, dt in zip(want, want_dt)]
        + [jax.ShapeDtypeStruct(p.shape, F32) for p in params],
        compiler_params=_cp(("arbitrary",)), name=name,
    )(*rows, *params, *cots, *add_arrays, *extra)
    return out[:len(want)], out[len(want):]


def _f_rms(x, g):
    return (x * lax.rsqrt(jnp.mean(x * x, axis=-1, keepdims=True) + EPS) * g,)


def _ln(x, g, b):
    mu = jnp.mean(x, axis=-1, keepdims=True)
    var = jnp.mean(jnp.square(x - mu), axis=-1, keepdims=True)
    return (x - mu) * lax.rsqrt(var + EPS) * g + b


def _f_glu(z):
    d = z.shape[1] // 2
    return (z[:, :d] * jax.nn.sigmoid(z[:, d:]),)


def _f_bias_glu(z, b):
    return _f_glu(z + b)


def _f_ln_silu(y, b_dw, g, b):
    return (jax.nn.silu(_ln(y + b_dw, g, b)),)


def _f_gelu_ln(z, g, b):
    d = z.shape[1] // 2
    zz = jax.nn.gelu(z)
    return zz[:, :d], _ln(zz[:, d:], g, b)


def _f_merge(o0, o1, o2, l0, l1, l2):
    m = jnp.maximum(jnp.maximum(l0, l1), l2)
    e0, e1, e2 = jnp.exp(l0 - m), jnp.exp(l1 - m), jnp.exp(l2 - m)
    s = e0 + e1 + e2
    pair = 2 * HEAD_DIM
    first_head = lax.broadcasted_iota(jnp.int32, (o0.shape[0], pair), 1) < HEAD_DIM
    cols = []
    for hp in range(o0.shape[1] // pair):
        acc = None
        for o, e in ((o0, e0), (o1, e1), (o2, e2)):
            wgt = e / s
            wp = jnp.where(first_head, wgt[:, 2 * hp:2 * hp + 1], wgt[:, 2 * hp + 1:2 * hp + 2])
            term = wp * o[:, hp * pair:(hp + 1) * pair]
            acc = term if acc is None else acc + term
        cols.append(acc)
    return (jnp.concatenate(cols, axis=1),)


def _loss_head(x, tgt, g, *, tr, name):
    T, D = x.shape
    tr = min(tr, T)

    def f(xv, gv, tv):
        y = _f_rms(xv, gv)[0]
        return 0.5 * jnp.mean(jnp.square(y - tv), axis=-1, keepdims=True)

    def body(x_ref, t_ref, g_ref, loss_ref, dx_ref, dxb_ref, dg_ref):
        tv = t_ref[...]
        l, vjp = jax.vjp(lambda xv, gv: f(xv, gv, tv), x_ref[...], g_ref[...])
        dx, dg = vjp(jnp.ones_like(l))
        dx_ref[...] = dx
        dxb_ref[...] = dx.astype(BF16)

        @pl.when(pl.program_id(0) == 0)
        def _():
            loss_ref[...] = jnp.zeros_like(loss_ref)
            dg_ref[...] = jnp.zeros_like(dg_ref)

        loss_ref[...] += jnp.sum(l)
        dg_ref[...] += dg

    return pl.pallas_call(
        body, grid=(T // tr,),
        in_specs=[pl.BlockSpec((tr, D), lambda i: (i, 0)), pl.BlockSpec((tr, D), lambda i: (i, 0)),
                  pl.BlockSpec((1, D), lambda i: (0, 0))],
        out_specs=[pl.BlockSpec((1, 128), lambda i: (0, 0)), pl.BlockSpec((tr, D), lambda i: (i, 0)),
                   pl.BlockSpec((tr, D), lambda i: (i, 0)), pl.BlockSpec((1, D), lambda i: (0, 0))],
        out_shape=[jax.ShapeDtypeStruct((1, 128), F32), jax.ShapeDtypeStruct((T, D), F32),
                   jax.ShapeDtypeStruct((T, D), BF16), jax.ShapeDtypeStruct((1, D), F32)],
        compiler_params=_cp(("arbitrary",)), name=name,
    )(x, tgt, g)


def _adamw(w, g, m, v, *, name):
    R, C = w.shape
    tr = _pick_rows(R, max(8, 2 * 1024 * 1024 // (4 * C)))
    c1 = 1.0 - ADAM_B1 ** ADAM_STEP
    c2 = 1.0 - ADAM_B2 ** ADAM_STEP

    def body(w_ref, g_ref, m_ref, v_ref, d_ref, nm_ref, nv_ref):
        gv = g_ref[...]
        nm = ADAM_B1 * m_ref[...] + (1.0 - ADAM_B1) * gv
        nv = ADAM_B2 * v_ref[...] + (1.0 - ADAM_B2) * jnp.square(gv)
        nm_ref[...] = nm
        nv_ref[...] = nv
        d_ref[...] = -ADAM_LR * ((nm / c1) / (jnp.sqrt(nv / c2) + ADAM_EPS) + ADAM_WD * w_ref[...])

    spec = pl.BlockSpec((tr, C), lambda i: (i, 0))
    return pl.pallas_call(
        body, grid=(R // tr,), in_specs=[spec] * 4, out_specs=[spec] * 3,
        out_shape=[jax.ShapeDtypeStruct((R, C), F32)] * 3,
        compiler_params=_cp(("parallel",)), name=name,
    )(w, g, m, v)


def _s5_prep(a_re, a_im, b_re, b_im, c_re, c_im, log_dt):
    G, N = a_re.shape
    P = b_re.shape[2]
    gpb = 128 // P
    nblk = G // gpb
    dt = jnp.exp(log_dt)[:, None]
    mag = jnp.exp(a_re * dt)
    abr, abi = mag * jnp.cos(a_im * dt), mag * jnp.sin(a_im * dt)
    den = a_re * a_re + a_im * a_im
    nr, ni = abr - 1.0, abi
    qr, qi = (nr * a_re + ni * a_im) / den, (ni * a_re - nr * a_im) / den
    bbr = qr[..., None] * b_re - qi[..., None] * b_im
    bbi = qr[..., None] * b_im + qi[..., None] * b_re
    eye = jnp.eye(gpb, dtype=F32)

    def expand_b(t):
        t = t.reshape(nblk, gpb, N, P).transpose(0, 1, 3, 2)
        return (t[:, :, :, None, :] * eye[None, :, None, :, None]).reshape(nblk, gpb * P, gpb * N)

    def expand_c(t):
        t = t.reshape(nblk, gpb, P, N).transpose(0, 1, 3, 2)
        return (t[:, :, :, None, :] * eye[None, :, None, :, None]).reshape(nblk, gpb * N, gpb * P)

    return (abr.reshape(1, G * N), abi.reshape(1, G * N), expand_b(bbr), expand_b(bbi),
            expand_c(c_re), expand_c(c_im))


def _s5_fwd(h, abr, abi, bre, bim, cre, cim, d, *, B, L, name):
    T, D = h.shape
    S = T // B
    L = min(L, S)
    nc = S // L
    nblk, cb, sb = bre.shape
    GN = abr.shape[1]

    def body(h_ref, ar_ref, ai_ref, bre_ref, bim_ref, cre_ref, cim_ref, d_ref,
             y_ref, yb_ref, xr_ref, xi_ref, er_ref, ei_ref, sr, si, car, cai):
        ci = pl.program_id(1)

        @pl.when(ci == 0)
        def _():
            car[...] = jnp.zeros_like(car)
            cai[...] = jnp.zeros_like(cai)

        for j in range(nblk):
            u = h_ref[:, j * cb:(j + 1) * cb]
            sr[:, j * sb:(j + 1) * sb] = jnp.dot(u, bre_ref[j], preferred_element_type=F32)
            si[:, j * sb:(j + 1) * sb] = jnp.dot(u, bim_ref[j], preferred_element_type=F32)
        ar, ai = ar_ref[...], ai_ref[...]

        def step(t, carry):
            pr, pi = carry
            nr = ar * pr - ai * pi + sr[pl.ds(t, 1), :]
            ni = ar * pi + ai * pr + si[pl.ds(t, 1), :]
            sr[pl.ds(t, 1), :] = nr
            si[pl.ds(t, 1), :] = ni
            return nr, ni

        pr, pi = lax.fori_loop(0, L, step, (car[...], cai[...]), unroll=4)
        car[...] = pr
        cai[...] = pi
        er_ref[0] = pr
        ei_ref[0] = pi
        for j in range(nblk):
            xr = sr[:, j * sb:(j + 1) * sb].astype(BF16)
            xi = si[:, j * sb:(j + 1) * sb].astype(BF16)
            xr_ref[:, j * sb:(j + 1) * sb] = xr
            xi_ref[:, j * sb:(j + 1) * sb] = xi
            y = (jnp.dot(xr, cre_ref[j], preferred_element_type=F32)
                 - jnp.dot(xi, cim_ref[j], preferred_element_type=F32))
            u = h_ref[:, j * cb:(j + 1) * cb].astype(F32)
            y = y + d_ref[:, j * cb:(j + 1) * cb] * u
            y_ref[:, j * cb:(j + 1) * cb] = y
            yb_ref[:, j * cb:(j + 1) * cb] = jax.nn.gelu(y).astype(BF16)

    tok = lambda w: pl.BlockSpec((L, w), lambda b, c: (b * nc + c, 0))
    whole = lambda p: pl.BlockSpec(p.shape, lambda b, c, nd=p.ndim: (0,) * nd)
    end = pl.BlockSpec((1, 1, GN), lambda b, c: (b * nc + c, 0, 0))
    return pl.pallas_call(
        body, grid=(B, nc),
        in_specs=[tok(D)] + [whole(p) for p in (abr, abi, bre, bim, cre, cim, d)],
        out_specs=[tok(D), tok(D), tok(GN), tok(GN), end, end],
        out_shape=[jax.ShapeDtypeStruct((T, D), F32), jax.ShapeDtypeStruct((T, D), BF16),
                   jax.ShapeDtypeStruct((T, GN), BF16),
                   jax.ShapeDtypeStruct((T, GN), BF16), jax.ShapeDtypeStruct((B * nc, 1, GN), F32),
                   jax.ShapeDtypeStruct((B * nc, 1, GN), F32)],
        scratch_shapes=[pltpu.VMEM((L, GN), F32), pltpu.VMEM((L, GN), F32),
                        pltpu.VMEM((1, GN), F32), pltpu.VMEM((1, GN), F32)],
        compiler_params=_cp(("arbitrary", "arbitrary")), name=name,
    )(h, abr, abi, bre, bim, cre, cim, d)


def _s5_bwd(dy, h, xr, xi, er, ei, abr, abi, bre, bim, cre, cim, d, *, B, L, name):
    T, D = h.shape
    S = T // B
    L = min(L, S)
    nc = S // L
    nblk, cb, sb = bre.shape
    GN = abr.shape[1]
    dims_nt = (((1,), (1,)), ((), ()))
    dims_tn = (((0,), (0,)), ((), ()))

    def body(dy_ref, h_ref, xr_ref, xi_ref, er_ref, ei_ref, ar_ref, ai_ref, bre_ref, bim_ref,
             cre_ref, cim_ref, d_ref,
             dh_ref, dbre_ref, dbim_ref, dcre_ref, dcim_ref, dar_ref, dai_ref, dd_ref,
             lr, li, car, cai):
        b, cstep = pl.program_id(0), pl.program_id(1)
        ci = nc - 1 - cstep

        @pl.when((b == 0) & (cstep == 0))
        def _():
            for r in (dbre_ref, dbim_ref, dcre_ref, dcim_ref, dar_ref, dai_ref, dd_ref):
                r[...] = jnp.zeros_like(r)

        @pl.when(cstep == 0)
        def _():
            car[...] = jnp.zeros_like(car)
            cai[...] = jnp.zeros_like(cai)

        for j in range(nblk):
            dyj = dy_ref[:, j * cb:(j + 1) * cb].astype(BF16)
            lr[:, j * sb:(j + 1) * sb] = lax.dot_general(dyj, cre_ref[j], dims_nt, preferred_element_type=F32)
            li[:, j * sb:(j + 1) * sb] = -lax.dot_general(dyj, cim_ref[j], dims_nt, preferred_element_type=F32)
        ar, ai = ar_ref[...], ai_ref[...]

        def step(s, carry):
            t = L - 1 - s
            pr, pi = carry
            nr = lr[pl.ds(t, 1), :] + ar * pr + ai * pi
            ni = li[pl.ds(t, 1), :] - ai * pr + ar * pi
            lr[pl.ds(t, 1), :] = nr
            li[pl.ds(t, 1), :] = ni
            return nr, ni

        pr, pi = lax.fori_loop(0, L, step, (car[...], cai[...]), unroll=4)
        car[...] = pr
        cai[...] = pi
        has_prev = (ci > 0).astype(F32)
        first_row = lax.broadcasted_iota(jnp.int32, (L, sb), 0) == 0
        for j in range(nblk):
            cs = slice(j * cb, (j + 1) * cb)
            ss = slice(j * sb, (j + 1) * sb)
            lrj, lij = lr[:, ss], li[:, ss]
            xrj, xij = xr_ref[:, ss], xi_ref[:, ss]
            pr_j = jnp.where(first_row, er_ref[0][:, ss] * has_prev, pltpu.roll(xrj.astype(F32), 1, 0))
            pi_j = jnp.where(first_row, ei_ref[0][:, ss] * has_prev, pltpu.roll(xij.astype(F32), 1, 0))
            dar_ref[:, ss] += jnp.sum(lrj * pr_j + lij * pi_j, axis=0, keepdims=True)
            dai_ref[:, ss] += jnp.sum(lij * pr_j - lrj * pi_j, axis=0, keepdims=True)
            lrb, lib = lrj.astype(BF16), lij.astype(BF16)
            hj = h_ref[:, cs]
            dyf = dy_ref[:, cs]
            dyj = dyf.astype(BF16)
            dbre_ref[j] += lax.dot_general(hj, lrb, dims_tn, preferred_element_type=F32)
            dbim_ref[j] += lax.dot_general(hj, lib, dims_tn, preferred_element_type=F32)
            dcre_ref[j] += lax.dot_general(xrj, dyj, dims_tn, preferred_element_type=F32)
            dcim_ref[j] -= lax.dot_general(xij, dyj, dims_tn, preferred_element_type=F32)
            du = (lax.dot_general(lrb, bre_ref[j], dims_nt, preferred_element_type=F32)
                  + lax.dot_general(lib, bim_ref[j], dims_nt, preferred_element_type=F32))
            dh_ref[:, cs] = du + d_ref[:, cs] * dyf
            dd_ref[:, cs] += jnp.sum(dyf * hj.astype(F32), axis=0, keepdims=True)

    tok = lambda w: pl.BlockSpec((L, w), lambda b, c: (b * nc + nc - 1 - c, 0))
    whole = lambda p: pl.BlockSpec(p.shape, lambda b, c, nd=p.ndim: (0,) * nd)
    prev_end = pl.BlockSpec((1, 1, GN), lambda b, c: (b * nc + jnp.maximum(nc - 2 - c, 0), 0, 0))
    params = (abr, abi, bre, bim, cre, cim, d)
    acc_shapes = [bre.shape, bim.shape, cre.shape, cim.shape, abr.shape, abi.shape, d.shape]
    out = pl.pallas_call(
        body, grid=(B, nc),
        in_specs=[tok(D), tok(D), tok(GN), tok(GN), prev_end, prev_end] + [whole(p) for p in params],
        out_specs=[tok(D)] + [pl.BlockSpec(s, lambda b, c, nd=len(s): (0,) * nd) for s in acc_shapes],
        out_shape=[jax.ShapeDtypeStruct((T, D), F32)] + [jax.ShapeDtypeStruct(s, F32) for s in acc_shapes],
        scratch_shapes=[pltpu.VMEM((L, GN), F32), pltpu.VMEM((L, GN), F32),
                        pltpu.VMEM((1, GN), F32), pltpu.VMEM((1, GN), F32)],
        compiler_params=_cp(("arbitrary", "arbitrary")), name=name,
    )(dy, h, xr, xi, er, ei, *params)
    return out


def _conv_fwd(zp, w, *, R, tc, name):
    B, SP, C = zp.shape
    S = SP - CONV_HALO
    R, tc = min(R, S), min(tc, C)

    def body(z_ref, w_ref, y_ref):
        def chunk(ci, _):
            start = pl.multiple_of(ci * R, 8)
            ze = z_ref[pl.ds(start, R + CONV_HALO), :]
            acc = jnp.zeros((R, tc), F32)
            for m in range(CONV_WIDTH):
                k = CONV_WIDTH - 1 - m
                sh = ze if m == 0 else pltpu.roll(ze, m, 0)
                acc = acc + w_ref[k:k + 1, :] * sh[CONV_HALO:, :]
            y_ref[pl.ds(start, R), :] = acc
            return 0

        lax.fori_loop(0, S // R, chunk, 0)

    return pl.pallas_call(
        body, grid=(B, C // tc),
        in_specs=[pl.BlockSpec((None, SP, tc), lambda b, c: (b, 0, c)),
                  pl.BlockSpec((32, tc), lambda b, c: (0, c))],
        out_specs=pl.BlockSpec((None, S, tc), lambda b, c: (b, 0, c)),
        out_shape=jax.ShapeDtypeStruct((B, S, C), F32),
        compiler_params=_cp(("parallel", "parallel")), name=name,
    )(zp, w)


def _conv_bwd(zp, dyp, w, *, R, tc, name):
    B, SP, C = zp.shape
    S = SP - CONV_HALO
    R, tc = min(R, S), min(tc, C)

    def body(z_ref, dy_ref, w_ref, dz_ref, dw_ref):
        @pl.when(pl.program_id(1) == 0)
        def _():
            dw_ref[...] = jnp.zeros_like(dw_ref)

        def chunk(ci, _):
            start = pl.multiple_of(ci * R, 8)
            zc = z_ref[pl.ds(start + CONV_HALO, R), :]
            de = dy_ref[pl.ds(start, R + CONV_HALO), :]
            acc = jnp.zeros((R, tc), F32)
            for m in range(CONV_WIDTH):
                k = CONV_WIDTH - 1 - m
                ds_ = (de if m == 0 else pltpu.roll(de, R + CONV_HALO - m, 0))[:R, :]
                acc = acc + w_ref[k:k + 1, :] * ds_
                dw_ref[k:k + 1, :] += jnp.sum(ds_ * zc, axis=0, keepdims=True)
            dz_ref[pl.ds(start, R), :] = acc
            return 0

        lax.fori_loop(0, S // R, chunk, 0)

    return pl.pallas_call(
        body, grid=(C // tc, B),
        in_specs=[pl.BlockSpec((None, SP, tc), lambda c, b: (b, 0, c)),
                  pl.BlockSpec((None, SP, tc), lambda c, b: (b, 0, c)),
                  pl.BlockSpec((32, tc), lambda c, b: (0, c))],
        out_specs=[pl.BlockSpec((None, S, tc), lambda c, b: (b, 0, c)),
                   pl.BlockSpec((32, tc), lambda c, b: (0, c))],
        out_shape=[jax.ShapeDtypeStruct((B, S, C), F32), jax.ShapeDtypeStruct((32, C), F32)],
        compiler_params=_cp(("parallel", "arbitrary")), name=name,
    )(zp, dyp, w)


def _gmlp_fwd(u, vn, ws, bcol, *, nck, name):
    T, E = u.shape
    H = ws.shape[0]
    he = E // H
    rows = nck * GMLP_CHUNK
    rows = min(rows, T)
    n_in = rows // GMLP_CHUNK

    def body(u_ref, v_ref, ws_ref, b_ref, o_ref):
        for c in range(n_in):
            rs = slice(c * GMLP_CHUNK, (c + 1) * GMLP_CHUNK)
            for hh in range(H):
                cs = slice(hh * he, (hh + 1) * he)
                v2 = jnp.dot(ws_ref[hh], v_ref[rs, cs].astype(BF16), preferred_element_type=F32)
                v2 = v2 + b_ref[:, hh:hh + 1]
                o_ref[rs, cs] = (u_ref[rs, cs] * v2).astype(o_ref.dtype)

    tok = pl.BlockSpec((rows, E), lambda i: (i, 0))
    return pl.pallas_call(
        body, grid=(T // rows,),
        in_specs=[tok, tok, pl.BlockSpec(ws.shape, lambda i: (0, 0, 0)), pl.BlockSpec(bcol.shape, lambda i: (0, 0))],
        out_specs=tok, out_shape=jax.ShapeDtypeStruct((T, E), BF16),
        compiler_params=_cp(("parallel",)), name=name,
    )(u, vn, ws, bcol)


def _gmlp_bwd(duv, u, vn, ws, bcol, *, nck, name):
    T, E = u.shape
    H = ws.shape[0]
    he = E // H
    rows = min(nck * GMLP_CHUNK, T)
    n_in = rows // GMLP_CHUNK
    dims_nt = (((1,), (1,)), ((), ()))
    dims_tn = (((0,), (0,)), ((), ()))

    def body(g_ref, u_ref, v_ref, ws_ref, b_ref, du_ref, dv_ref, dws_ref, db_ref):
        @pl.when(pl.program_id(0) == 0)
        def _():
            dws_ref[...] = jnp.zeros_like(dws_ref)
            db_ref[...] = jnp.zeros_like(db_ref)

        for c in range(n_in):
            rs = slice(c * GMLP_CHUNK, (c + 1) * GMLP_CHUNK)
            for hh in range(H):
                cs = slice(hh * he, (hh + 1) * he)
                vb = v_ref[rs, cs].astype(BF16)
                v2 = jnp.dot(ws_ref[hh], vb, preferred_element_type=F32) + b_ref[:, hh:hh + 1]
                g = g_ref[rs, cs]
                du_ref[rs, cs] = g * v2
                dv2 = g * u_ref[rs, cs]
                dv2b = dv2.astype(BF16)
                dv_ref[rs, cs] = lax.dot_general(ws_ref[hh], dv2b, dims_tn, preferred_element_type=F32)
                dws_ref[hh] += lax.dot_general(dv2b, vb, dims_nt, preferred_element_type=F32)
                db_ref[:, hh:hh + 1] += jnp.sum(dv2, axis=1, keepdims=True)

    tok = pl.BlockSpec((rows, E), lambda i: (i, 0))
    return pl.pallas_call(
        body, grid=(T // rows,),
        in_specs=[tok, tok, tok, pl.BlockSpec(ws.shape, lambda i: (0, 0, 0)), pl.BlockSpec(bcol.shape, lambda i: (0, 0))],
        out_specs=[tok, tok, pl.BlockSpec(ws.shape, lambda i: (0, 0, 0)), pl.BlockSpec(bcol.shape, lambda i: (0, 0))],
        out_shape=[jax.ShapeDtypeStruct((T, E), F32), jax.ShapeDtypeStruct((T, E), F32),
                   jax.ShapeDtypeStruct(ws.shape, F32), jax.ShapeDtypeStruct(bcol.shape, F32)],
        compiler_params=_cp(("arbitrary",)), name=name,
    )(duv, u, vn, ws, bcol)


PAIRS = ATT_HEADS // 2


def _att_consts():
    ji = lax.broadcasted_iota(jnp.int32, (2 * ATT_BLK, ATT_BLK), 0)
    ii = lax.broadcasted_iota(jnp.int32, (2 * ATT_BLK, ATT_BLK), 1)
    dist = ii + ATT_BLK - ji
    band = (dist >= 0) & (dist <= ATT_BLK)
    cur = ji >= ATT_BLK
    first_head = lax.broadcasted_iota(jnp.int32, (ATT_BLK, 2 * HEAD_DIM), 1) < HEAD_DIM
    return band, cur, first_head


def _both_heads(t, first_head):
    zero = jnp.zeros_like(t)
    return jnp.concatenate([jnp.where(first_head, t, zero), jnp.where(first_head, zero, t)], axis=0)


def _att_specs(nbk, offs, nsteps, rev):
    rows = nbk * ATT_BLK
    step = (lambda i: nsteps - 1 - i) if rev else (lambda i: i)
    qoff, koff, voff = offs
    blk = lambda off: pl.BlockSpec((rows, 2 * HEAD_DIM), lambda hp, i: (step(i), off + hp))
    prev = lambda off: pl.BlockSpec((ATT_BLK, 2 * HEAD_DIM), lambda hp, i: (jnp.maximum(step(i) * nbk - 1, 0), off + hp))
    out = pl.BlockSpec((rows, 2 * HEAD_DIM), lambda hp, i: (step(i), hp))
    stat = pl.BlockSpec((2, nbk, ATT_BLK), lambda hp, i: (hp, step(i), 0))
    return [blk(qoff), blk(koff), prev(koff), blk(voff), prev(voff)], out, stat


def _att_fwd(arr, offs, *, nb, nbk, name):
    T = arr.shape[0]
    nbk = min(nbk, T // ATT_BLK)
    nsteps = T // (nbk * ATT_BLK)
    scale = HEAD_DIM ** -0.5
    dims_nt = (((1,), (1,)), ((), ()))
    dims_tn = (((0,), (0,)), ((), ()))

    def body(q_ref, k_ref, kp_ref, v_ref, vp_ref, o_ref, lse_ref):
        i = pl.program_id(1)
        band, cur, first_head = _att_consts()
        for jj in range(nbk):
            rs = slice(jj * ATT_BLK, (jj + 1) * ATT_BLK)
            ps = slice((jj - 1) * ATT_BLK, jj * ATT_BLK)
            has_prev = ((i * nbk + jj) & (nb - 1)) != 0
            valid = band & (cur | has_prev)
            kk = jnp.concatenate([kp_ref[...] if jj == 0 else k_ref[ps, :], k_ref[rs, :]], axis=0)
            vv = jnp.concatenate([vp_ref[...] if jj == 0 else v_ref[ps, :], v_ref[rs, :]], axis=0)
            q2 = _both_heads(q_ref[rs, :], first_head)
            st = lax.dot_general(kk, q2, dims_nt, preferred_element_type=F32) * scale
            st = jnp.where(jnp.concatenate([valid, valid], axis=1), st, MASK_VALUE)
            m = jnp.max(st, axis=0, keepdims=True)
            p = jnp.exp(st - m)
            l = jnp.sum(p, axis=0, keepdims=True)
            lse = m + jnp.log(l)
            lse_ref[0, jj:jj + 1, :] = lse[:, :ATT_BLK]
            lse_ref[1, jj:jj + 1, :] = lse[:, ATT_BLK:]
            pn = (p / l).astype(BF16)
            o2 = lax.dot_general(pn, vv, dims_tn, preferred_element_type=F32)
            o_ref[rs, :] = jnp.where(first_head, o2[:ATT_BLK], o2[ATT_BLK:])

    ins, out, stat = _att_specs(nbk, offs, nsteps, False)
    return pl.pallas_call(
        body, grid=(PAIRS, nsteps), in_specs=ins, out_specs=[out, stat],
        out_shape=[jax.ShapeDtypeStruct((T, ATT_W), F32), jax.ShapeDtypeStruct((ATT_HEADS, T // ATT_BLK, ATT_BLK), F32)],
        compiler_params=_cp(("parallel", "parallel")), name=name,
    )(arr, arr, arr, arr, arr)


def _att_bwd(arr, offs, do, lse, dlse, *, nb, nbk, name):
    T = arr.shape[0]
    nbk = min(nbk, T // ATT_BLK)
    nsteps = T // (nbk * ATT_BLK)
    scale = HEAD_DIM ** -0.5
    dims_nt = (((1,), (1,)), ((), ()))
    dims_tn = (((0,), (0,)), ((), ()))

    def body(q_ref, k_ref, kp_ref, v_ref, vp_ref, do_ref, lse_ref, dlse_ref, dq_ref, dk_ref, dv_ref, ck, cv):
        step = pl.program_id(1)
        i = nsteps - 1 - step
        band, cur, first_head = _att_consts()

        @pl.when(step == 0)
        def _():
            ck[...] = jnp.zeros_like(ck)
            cv[...] = jnp.zeros_like(cv)

        carry_k, carry_v = ck[...], cv[...]
        for jj in reversed(range(nbk)):
            rs = slice(jj * ATT_BLK, (jj + 1) * ATT_BLK)
            ps = slice((jj - 1) * ATT_BLK, jj * ATT_BLK)
            has_prev = ((i * nbk + jj) & (nb - 1)) != 0
            valid = band & (cur | has_prev)
            kk = jnp.concatenate([kp_ref[...] if jj == 0 else k_ref[ps, :], k_ref[rs, :]], axis=0)
            vv = jnp.concatenate([vp_ref[...] if jj == 0 else v_ref[ps, :], v_ref[rs, :]], axis=0)
            q2 = _both_heads(q_ref[rs, :], first_head)
            do2 = _both_heads(do_ref[rs, :].astype(BF16), first_head)
            lse = jnp.concatenate([lse_ref[0, jj:jj + 1, :], lse_ref[1, jj:jj + 1, :]], axis=1)
            dlse = jnp.concatenate([dlse_ref[0, jj:jj + 1, :], dlse_ref[1, jj:jj + 1, :]], axis=1)
            st = lax.dot_general(kk, q2, dims_nt, preferred_element_type=F32) * scale
            st = jnp.where(jnp.concatenate([valid, valid], axis=1), st, MASK_VALUE)
            p = jnp.exp(st - lse)
            dp = lax.dot_general(vv, do2, dims_nt, preferred_element_type=F32)
            delta = jnp.sum(p * dp, axis=0, keepdims=True)
            dsb = (p * (dp - delta + dlse) * scale).astype(BF16)
            dq2 = lax.dot_general(dsb, kk, dims_tn, preferred_element_type=F32)
            dkk = jnp.dot(dsb, q2, preferred_element_type=F32)
            dvv = jnp.dot(p.astype(BF16), do2, preferred_element_type=F32)
            dq_ref[rs, :] = jnp.where(first_head, dq2[:ATT_BLK], dq2[ATT_BLK:]).astype(dq_ref.dtype)
            dk_ref[rs, :] = (dkk[ATT_BLK:] + carry_k).astype(dk_ref.dtype)
            dv_ref[rs, :] = (dvv[ATT_BLK:] + carry_v).astype(dv_ref.dtype)
            carry_k, carry_v = dkk[:ATT_BLK], dvv[:ATT_BLK]
        ck[...] = carry_k
        cv[...] = carry_v

    ins, out, stat = _att_specs(nbk, offs, nsteps, True)
    return pl.pallas_call(
        body, grid=(PAIRS, nsteps), in_specs=ins + [out, stat, stat], out_specs=[out] * 3,
        out_shape=[jax.ShapeDtypeStruct((T, ATT_W), BF16)] * 3,
        scratch_shapes=[pltpu.VMEM((ATT_BLK, 2 * HEAD_DIM), F32), pltpu.VMEM((ATT_BLK, 2 * HEAD_DIM), F32)],
        compiler_params=_cp(("arbitrary", "arbitrary")), name=name,
    )(arr, arr, arr, arr, arr, do, lse, dlse)


def _deinterleave(t, B, S, dil):
    if dil == 1:
        return t
    return t.reshape((B, S // dil, dil) + t.shape[1:]).swapaxes(1, 2).reshape(t.shape)


def _interleave(t, B, S, dil):
    if dil == 1:
        return t
    return t.reshape((B, dil, S // dil) + t.shape[1:]).swapaxes(1, 2).reshape(t.shape)


def _stats_to_tokens(lse, B, S, dil):
    return _interleave(lse.reshape(lse.shape[0], -1).T, B, S, dil)


def _stats_from_tokens(dl, B, S, dil):
    return _deinterleave(dl, B, S, dil).T.reshape(dl.shape[1], -1, ATT_BLK)


def _mesh_pos():
    return lax.axis_index("x"), lax.axis_index("y"), lax.axis_index("c")


def _allgather8(xs, *, name):
    m_per, n = xs.shape

    def body(x_ref, out_ref, send_sems, recv_sems, local_sem):
        x, y, c = _mesh_pos()
        me, sibling = (x, y, c), (x, y, 1 - c)
        chips = [(1 - x, y), (x, 1 - y), (1 - x, 1 - y)]

        def rows(px, py, pc):
            return out_ref.at[pl.ds((4 * px + 2 * py + pc) * m_per, m_per), :]

        def copy(k, block, to, src=None):
            return pltpu.make_async_remote_copy(
                src_ref=rows(*block) if src is None else src, dst_ref=rows(*block),
                send_sem=send_sems.at[k], recv_sem=recv_sems.at[k], device_id=to, device_id_type=MESH)

        mine = pltpu.make_async_copy(x_ref, rows(*me), local_sem)
        mine.start()
        first = [copy(0, me, sibling, src=x_ref)]
        first += [copy(1 + j, me, (*chip, c), src=x_ref) for j, chip in enumerate(chips)]
        for cp in first:
            cp.start()
        passed = [copy(4 + j, (*chip, c), sibling) for j, chip in enumerate(chips)]
        for j, chip in enumerate(chips):
            copy(1 + j, (*chip, c), me).wait_recv()
            passed[j].start()
        copy(0, sibling, me).wait_recv()
        for j, chip in enumerate(chips):
            copy(4 + j, (*chip, 1 - c), me).wait_recv()
        for cp in first + passed:
            cp.wait_send()
        mine.wait()

    return pl.pallas_call(
        body, out_shape=jax.ShapeDtypeStruct((8 * m_per, n), xs.dtype),
        in_specs=[pl.BlockSpec(memory_space=pltpu.VMEM)], out_specs=pl.BlockSpec(memory_space=pltpu.VMEM),
        scratch_shapes=[pltpu.SemaphoreType.DMA((7,)), pltpu.SemaphoreType.DMA((7,)), pltpu.SemaphoreType.DMA],
        compiler_params=pltpu.CompilerParams(vmem_limit_bytes=VMEM_LIMIT), name=name,
    )(xs)


def _hbm_call(body, arrays, out_shapes, n_sems, *, name):
    any_spec = pl.BlockSpec(memory_space=pl.ANY)
    return pl.pallas_call(
        body, out_shape=out_shapes, in_specs=[any_spec] * len(arrays), out_specs=[any_spec] * len(out_shapes),
        scratch_shapes=[pltpu.SemaphoreType.DMA((n_sems,)), pltpu.SemaphoreType.DMA((n_sems,))], name=name,
    )(*arrays)


def _other_chips(x, y):
    return [(1 - x, y), (x, 1 - y), (1 - x, 1 - y)]


def _split_start(srcs, lands, after, issue, n_sems, *, name):
    ns, nl = len(srcs), len(lands)
    hbm, sem = pl.BlockSpec(memory_space=pltpu.HBM), pl.BlockSpec(memory_space=pltpu.SEMAPHORE)
    extra = [] if after is None else [after]

    def body(*refs):
        n_in = ns + nl + len(extra)
        send_sems, recv_sems = refs[n_in], refs[n_in + 1]
        issue(refs[:ns], refs[ns:ns + nl], send_sems, recv_sems)
        refs[-1][...] = jnp.zeros_like(refs[-1])

    arrays = [pltpu.with_memory_space_constraint(a, pltpu.HBM) for a in list(srcs) + list(lands)]
    out = pl.pallas_call(
        body, name=name,
        out_shape=(pltpu.SemaphoreType.DMA((n_sems,)), pltpu.SemaphoreType.DMA((n_sems,)),
                   *[pltpu.HBM(a.shape, a.dtype) for a in arrays], jax.ShapeDtypeStruct((8, 128), F32)),
        in_specs=[hbm] * (ns + nl) + [pl.BlockSpec(memory_space=pl.ANY)] * len(extra),
        out_specs=(sem, sem, *[hbm] * (ns + nl), pl.BlockSpec(memory_space=pltpu.VMEM)),
        input_output_aliases={i: 2 + i for i in range(ns + nl)},
        compiler_params=pltpu.CompilerParams(has_side_effects=pltpu.SideEffectType.DATAFLOW_SIDE_EFFECTING),
    )(*arrays, *extra)
    return out[0], out[1], list(out[2:2 + ns]), list(out[2 + ns:2 + ns + nl]), out[-1]


def _split_wait(send_sems, recv_sems, srcs, lands, after, waits, *, name):
    ns, nl = len(srcs), len(lands)
    hbm, sem = pl.BlockSpec(memory_space=pltpu.HBM), pl.BlockSpec(memory_space=pltpu.SEMAPHORE)

    def body(*refs):
        waits(refs[:ns], refs[ns:ns + nl], refs[ns + nl], refs[ns + nl + 1])

    out = pl.pallas_call(
        body, name=name,
        out_shape=tuple(pltpu.HBM(a.shape, a.dtype) for a in list(srcs) + list(lands)),
        in_specs=[hbm] * (ns + nl) + [sem, sem, pl.BlockSpec(memory_space=pl.ANY)],
        out_specs=tuple([hbm] * (ns + nl)),
        input_output_aliases={i: i for i in range(ns + nl)},
        compiler_params=pltpu.CompilerParams(has_side_effects=pltpu.SideEffectType.DATAFLOW_SIDE_EFFECTING),
    )(*srcs, *lands, send_sems, recv_sems, after)
    return list(out[:ns]), list(out[ns:])


def _gather_start(halves, after, *, name):
    n = len(halves)
    lands = [lax.empty((N_CHIPS,) + h.shape, h.dtype) for h in halves]

    def issue(srcs, dsts, send_sems, recv_sems):
        x, y, c = _mesh_pos()
        me = 2 * x + y
        for a in range(n):
            for j, (px, py) in enumerate(_other_chips(x, y)):
                for cc in range(2):
                    pltpu.make_async_remote_copy(
                        src_ref=srcs[a].at[c], dst_ref=dsts[a].at[me, c],
                        send_sem=send_sems.at[6 * a + 2 * j + cc], recv_sem=recv_sems.at[6 * a + 2 * j + c],
                        device_id=(px, py, cc), device_id_type=MESH).start()

    return _split_start(halves, lands, after, issue, 6 * n, name=name)


def _gather_wait(started, after, *, name):
    send_sems, recv_sems, halves, lands = started
    n = len(halves)

    def waits(srcs, dsts, send_sems, recv_sems):
        x, y, c = _mesh_pos()
        me = 2 * x + y
        for a in range(n):
            for j, (px, py) in enumerate(_other_chips(x, y)):
                for cc in range(2):
                    pltpu.make_async_remote_copy(
                        src_ref=srcs[a].at[cc], dst_ref=dsts[a].at[2 * px + py, cc],
                        send_sem=send_sems.at[6 * a + 2 * j + cc], recv_sem=recv_sems.at[6 * a + 2 * j + cc],
                        device_id=(px, py, cc), device_id_type=MESH).wait_recv()
        for a in range(n):
            for j, (px, py) in enumerate(_other_chips(x, y)):
                for cc in range(2):
                    pltpu.make_async_remote_copy(
                        src_ref=srcs[a].at[c], dst_ref=dsts[a].at[me, c],
                        send_sem=send_sems.at[6 * a + 2 * j + cc], recv_sem=recv_sems.at[6 * a + 2 * j + c],
                        device_id=(px, py, cc), device_id_type=MESH).wait_send()

    return _split_wait(send_sems, recv_sems, halves, lands, after, waits, name=name)


def _reduce_plan_loops(plans, chip, c, fn):
    for a, plan in enumerate(plans):
        for h, cc in plan:
            for k in range(N_CHIPS):
                fn(a, h, k, cc, jnp.logical_or(chip != k, c != cc))


def _reduce_start(srcs, lands, plans, after, *, name):
    def issue(src_refs, land_refs, send_sems, recv_sems):
        x, y, c = _mesh_pos()
        chip = 2 * x + y
        my_id = 2 * chip + c

        def send(a, h, k, cc, is_other):
            @pl.when(is_other)
            def _():
                pltpu.make_async_remote_copy(
                    src_ref=src_refs[a].at[h, k], dst_ref=land_refs[a].at[my_id],
                    send_sem=send_sems.at[8 * a + 2 * k + cc], recv_sem=recv_sems.at[8 * a + my_id],
                    device_id=(k // 2, k % 2, cc), device_id_type=MESH).start()

        _reduce_plan_loops(plans, chip, c, send)

    return _split_start(srcs, lands, after, issue, 8 * len(srcs), name=name)


def _reduce_wait(started, plans, after, *, name):
    send_sems, recv_sems, srcs, lands = started

    def waits(src_refs, land_refs, send_sems, recv_sems):
        x, y, c = _mesh_pos()
        chip = 2 * x + y
        my_id = 2 * chip + c
        for a, plan in enumerate(plans):
            for h, cc in plan:
                for s in range(2 * N_CHIPS):
                    @pl.when(jnp.logical_and(c == cc, my_id != s))
                    def _(a=a, h=h, s=s):
                        pltpu.make_async_remote_copy(
                            src_ref=src_refs[a].at[h, 0], dst_ref=land_refs[a].at[s],
                            send_sem=send_sems.at[8 * a + s], recv_sem=recv_sems.at[8 * a + s],
                            device_id=(s // 4, (s // 2) % 2, s % 2), device_id_type=MESH).wait_recv()

        def sent(a, h, k, cc, is_other):
            @pl.when(is_other)
            def _():
                pltpu.make_async_remote_copy(
                    src_ref=src_refs[a].at[h, k], dst_ref=land_refs[a].at[my_id],
                    send_sem=send_sems.at[8 * a + 2 * k + cc], recv_sem=recv_sems.at[8 * a + my_id],
                    device_id=(k // 2, k % 2, cc), device_id_type=MESH).wait_send()

        _reduce_plan_loops(plans, chip, c, sent)

    return _split_wait(send_sems, recv_sems, srcs, lands, after, waits, name=name)


def _sum8(land, own, my_id, *, name):
    n_src, R, C = land.shape
    tr = _pick_rows(R, max(8, 1024 * 1024 // (2 * C)))

    def body(id_ref, *refs):
        own_ref, o_ref = refs[n_src], refs[n_src + 1]
        me = id_ref[0]
        acc = None
        for s in range(n_src):
            term = jnp.where(me == s, own_ref[...], refs[s][...]).astype(F32)
            acc = term if acc is None else acc + term
        o_ref[...] = acc

    return pl.pallas_call(
        body, out_shape=jax.ShapeDtypeStruct((R, C), F32),
        grid_spec=pltpu.PrefetchScalarGridSpec(
            num_scalar_prefetch=1, grid=(R // tr,),
            in_specs=[pl.BlockSpec((None, tr, C), lambda i, idr, s=s: (s, i, 0)) for s in range(n_src)]
            + [pl.BlockSpec((tr, C), lambda i, idr: (i, 0))],
            out_specs=pl.BlockSpec((tr, C), lambda i, idr: (i, 0))),
        compiler_params=_cp(("parallel",)), name=name,
    )(my_id.reshape(1).astype(jnp.int32), *([land] * n_src), own)


def _share_halves(ts, *, name):
    n = len(ts)

    def body(*refs):
        ins, outs, (send_sems, recv_sems) = refs[:n], refs[n:2 * n], refs[2 * n:]
        x, y, c = _mesh_pos()
        cps = [pltpu.make_async_remote_copy(
            src_ref=ins[a], dst_ref=outs[a].at[c], send_sem=send_sems.at[a], recv_sem=recv_sems.at[a],
            device_id=(x, y, 1 - c), device_id_type=MESH) for a in range(n)]
        for cp in cps:
            cp.start()
        for a in range(n):
            pltpu.make_async_remote_copy(
                src_ref=ins[a], dst_ref=outs[a].at[1 - c], send_sem=send_sems.at[a], recv_sem=recv_sems.at[a],
                device_id=(x, y, 1 - c), device_id_type=MESH).wait_recv()
        for cp in cps:
            cp.wait_send()

    return _hbm_call(body, ts, [jax.ShapeDtypeStruct((2,) + t.shape, t.dtype) for t in ts], n, name=name)


TR = 512
ATT_BLOCKS_PER_STEP = 64
S5_CHUNK = 256


def _rms_fwd(x, g, name):
    return _rowwise(_f_rms, [x], [g], (BF16,), tr=TR, name=name)[0]


def _rms_bwd_epi(dh, x, g, gx):
    r = lax.rsqrt(jnp.mean(x * x, axis=-1, keepdims=True) + EPS)
    xr = x * r
    t = dh * g
    dx = r * (t - xr * jnp.mean(t * xr, axis=-1, keepdims=True)) + gx
    return dx, dx, jnp.sum(dh * xr, axis=0, keepdims=True)


def _mm_rms_bwd(a, w, x, g, gx, *, after=None, name, **kw):
    kw.setdefault("tm", 1024)
    return _mm(a, w, tb=True, epi=_rms_bwd_epi, extras=(x, g, gx), out_dtypes=(F32, BF16, F32), n_row_sums=1,
               tn=x.shape[1], after=after, name=name, **kw)


def _rms_bwd(x, g, dh, gx, name, after=None):
    (dx, dxb), (dg,) = _rowwise_vjp(_f_rms, [x], [g], [dh], [(F32, BF16)], adds={0: gx}, after=after, tr=TR,
                                    name=name)
    return dx, dxb, dg


def _grad_cols(M, Nq):
    def imap(tm, tn):
        hp, per = (M // 2) // tm, Nq // tn
        assert hp * tm * 2 == M and per * tn == Nq, (M, Nq, tm, tn)
        return lambda i, j, k: (i // hp, j // per, i % hp, j % per)
    return (2, N_CHIPS, M // 2, Nq), lambda tm, tn: (None, None, tm, tn), imap, None, M // 2, Nq


def _grad_rows(Mq, N):
    def imap(tm, tn):
        po, hp = Mq // tm, (Mq // 2) // tm
        assert hp * tm * 2 == Mq, (Mq, tm)
        return lambda i, j, k: ((i % po) // hp, i // po, (i % po) % hp, j)
    return (2, N_CHIPS, Mq // 2, N), lambda tm, tn: (None, None, tm, tn), imap, None, Mq // 2, N


def _grad_layer_cols(slot, lh, M, Nq, buf):
    def imap(tm, tn):
        per = Nq // tn
        return lambda i, j, k: (j // per, slot, i, j % per)
    return (N_CHIPS, lh, M, Nq), lambda tm, tn: (None, None, tm, tn), imap, buf, M, Nq


def _grad_layer_rows(slot, lh, Mq, N, buf):
    def imap(tm, tn):
        po = Mq // tm
        return lambda i, j, k: (i // po, slot, i % po, j)
    return (N_CHIPS, lh, Mq, N), lambda tm, tn: (None, None, tm, tn), imap, buf, Mq, N


def _add_then_rms(acc, res, g):
    xo = acc + res
    return xo, _f_rms(xo, g)[0]


def _mlp_fwd(x, h2, w_in, w_out, g_next, li):
    w_in_full = getattr(w_in, "plain", w_in)
    r = _mm(h2, w_in_full, out_dtypes=(BF16,), epi=lambda acc: (jnp.maximum(acc, 0.0),), tm=512, tn=w_in.shape[1],
            name=f"mlp_in_{li}")
    tiles = dict(tm=512, tn=x.shape[1], tk=r.shape[1])
    if g_next is None:
        x_out, h_next = _mm(r, w_out, pro_a=lambda t: t * t, epi=lambda acc, res: (acc + res,), extras=(x,),
                            name=f"mlp_out_{li}", **tiles), None
    else:
        x_out, h_next = _mm(r, w_out, pro_a=lambda t: t * t, epi=_add_then_rms, extras=(x, g_next),
                            out_dtypes=(F32, BF16), name=f"mlp_out_{li}", **tiles)
    return x_out, h_next, (h2, r)


def _mlp_bwd(gx, gxb, x, g, w_in, w_out, saved, li, nl, bufs):
    h2, r = saved
    D, F = w_in.shape
    lh = nl // 2
    da = _mm(gxb, w_out, tb=True, out_dtypes=(BF16,),
             epi=lambda acc, rt: (acc * 2.0 * rt.astype(F32),), extras=(r,), tm=512, tn=F, name=f"mlp_dact_{li}")
    buf_in, buf_out = bufs if bufs is not None else (None, None)
    d_w_out = _mm(r, gxb, ta=True, pro_a=lambda t: t * t, tm=1024, tn=1024, tk=4096, out_dtypes=(BF16,),
                  out=_grad_layer_rows(li % lh, lh, F // N_CHIPS, D, buf_out), name=f"mlp_dwout_{li}")
    d_w_in = _mm(h2, da, ta=True, tm=1024, tn=1024, tk=4096, out_dtypes=(BF16,),
                 out=_grad_layer_cols(li % lh, lh, D, F // N_CHIPS, buf_in), name=f"mlp_dwin_{li}")
    gx_mid, gxb_mid, dg = _mm_rms_bwd(da, getattr(w_in, "plain", w_in), x, g, gx, tm=512, tk=F, name=f"mlp_dh_{li}")
    return gx_mid, gxb_mid, dg, (d_w_in, d_w_out)


def _local_step(x3, tgt3, p, layer_weights, token=None, grads_done=lambda group: None):
    B, S, D = x3.shape
    T = B * S
    x = x3.reshape(T, D)
    grads = {}
    row = lambda v: v.reshape(1, -1)
    p = dict(p)
    nl = p["norm_mlp"].shape[0]
    mlp_in, mlp_out = [None] * nl, [None] * nl

    def fetch(li, after):
        wl = dict(layer_weights(li, after))
        mlp_in[li], mlp_out[li] = wl.pop("mlp_w_in"), wl.pop("mlp_w_out")
        p.update(wl)

    g0 = row(p["norm_mix"][0])
    if token is not None:
        g0 = g0 + token[:1, :1]
    h0 = _rms_fwd(x, g0, "rms_mix_0")
    s5_args = (p["ssm_a_re"][0], p["ssm_a_im"][0], p["ssm_b_re"][0], p["ssm_b_im"][0],
               p["ssm_c_re"][0], p["ssm_c_im"][0], p["ssm_log_dt"][0])
    s5_exp, s5_vjp = jax.vjp(_s5_prep, *s5_args)
    abr, abi, bre, bim, cre, cim = s5_exp
    bre_b, bim_b, cre_b, cim_b = (t.astype(BF16) for t in (bre, bim, cre, cim))
    d_skip = p["ssm_d"]
    ypre, yb, sxr, sxi, ser, sei = _s5_fwd(h0, abr, abi, bre_b, bim_b, cre_b, cim_b, d_skip, B=B, L=S5_CHUNK,
                                           name="s5_fwd")
    fetch(0, yb)
    w_glu = p["ssm_w_glu"]
    z0 = _mm(yb, w_glu, tm=4096, name="s5_glu_mm")
    gm = [row(p["norm_mlp"][i]) for i in range(nl)]
    g1, g2, g3 = (row(p["norm_mix"][i]) for i in range(1, nl))
    x_mid0, hm0 = _rowwise(lambda z, xr, g: _add_then_rms(_f_glu(z)[0], xr, g), [z0, x], [gm[0]], (F32, BF16),
                           tr=TR, name="s5_glu")
    x1, h1, mlp_saved0 = _mlp_fwd(x_mid0, hm0, mlp_in[0], mlp_out[0], g1, 0)

    fetch(1, h1)
    z1 = _mm(h1, p["conv_w_pw1"], tm=4096, name="conv_pw1")
    zg = _rowwise(_f_bias_glu, [z1], [p["conv_b_pw1"]], (F32,), tr=TR, name="conv_glu")[0]
    zp = jnp.pad(zg.reshape(B, S, D), ((0, 0), (CONV_HALO, 0), (0, 0)))
    w_dw = jnp.pad(p["conv_w_dw"], ((0, 32 - CONV_WIDTH), (0, 0)))
    yc = _conv_fwd(zp, w_dw, R=256, tc=128, name="conv_dw").reshape(T, D)
    ln_par = [p["conv_b_dw"], p["conv_ln_g"], p["conv_ln_b"]]
    qc = _rowwise(_f_ln_silu, [yc], ln_par, (BF16,), tr=TR, name="conv_ln_silu")[0]
    x_mid1, hm1 = _mm(qc, p["conv_w_pw2"], epi=lambda acc, bias, res, g: _add_then_rms(acc + bias, res, g),
                      extras=(p["conv_b_pw2"], x1, gm[1]), out_dtypes=(F32, BF16), tn=D, name="conv_pw2")
    x2, h2, mlp_saved1 = _mlp_fwd(x_mid1, hm1, mlp_in[1], mlp_out[1], g2, 1)

    fetch(2, h2)
    z2 = _mm(h2, p["gmlp_w_in"], tm=4096, name="gmlp_in")
    gl_par = [p["gmlp_ln_g"], p["gmlp_ln_b"]]
    gu, gvn = _rowwise(_f_gelu_ln, [z2], gl_par, (F32, F32), tr=TR, name="gmlp_gelu_ln")
    causal = jnp.tril(jnp.ones((GMLP_CHUNK, GMLP_CHUNK), dtype=bool))
    ws_b = jnp.where(causal[None], p["gmlp_w_s"][0], 0.0).astype(BF16)
    bcol = jnp.pad(p["gmlp_b_s"][0].T, ((0, 0), (0, 128 - GMLP_HEADS)))
    uv = _gmlp_fwd(gu, gvn, ws_b, bcol, nck=4, name="gmlp_spatial")
    x_mid2, hm2 = _mm(uv, p["gmlp_w_out"], epi=_add_then_rms, extras=(x2, gm[2]), out_dtypes=(F32, BF16), tn=D,
                      name="gmlp_out")
    x3_, h3, mlp_saved2 = _mlp_fwd(x_mid2, hm2, mlp_in[2], mlp_out[2], g3, 2)

    fetch(3, h3)
    ng = len(ATT_DILS)
    att_in, o_tok, l_tok, lses = [], [], [], []
    offs = (0, PAIRS, 2 * PAIRS)
    for gi, dil in enumerate(ATT_DILS):
        w_g = _ColBlocks(p["attn_w_qkv_plain"], gi, ng, 3, ATT_W)
        arr = _mm(h3, w_g, out_dtypes=(BF16,), tm=4096, tn=ATT_W, name=f"attn_qkv_{gi}")
        arr = _deinterleave(arr, B, S, dil)
        att_in.append((arr, offs))
        og, lg = _att_fwd(arr, offs, nb=S // dil // ATT_BLK, nbk=ATT_BLOCKS_PER_STEP, name=f"attn_fwd_{gi}")
        lses.append(lg)
        o_tok.append(_interleave(og, B, S, dil))
        l_tok.append(_stats_to_tokens(lg, B, S, dil))
    merged2 = _rowwise(_f_merge, o_tok + l_tok, [], (BF16,), tr=TR, name="attn_merge")[0]
    x_mid3, hm3 = _mm(merged2, p["attn_w_o_plain"], epi=_add_then_rms, extras=(x3_, gm[3]), out_dtypes=(F32, BF16),
                      tn=D, name="attn_out")
    x4, _, mlp_saved3 = _mlp_fwd(x_mid3, hm3, mlp_in[3], mlp_out[3], None, 3)

    loss_part, gx, gxb, dgf = _loss_head(x4, tgt3.reshape(T, D), row(p["norm_final"]), tr=TR, name="loss_head")
    grads["norm_final"] = dgf.reshape(-1)
    d_norm_mix, d_norm_mlp = [None] * 4, [None] * 4
    Dq = D // N_CHIPS

    gx, gxb, d_norm_mlp[3], mlp_hi = _mlp_bwd(
        gx, gxb, x_mid3, row(p["norm_mlp"][3]), mlp_in[3], mlp_out[3], mlp_saved3, 3, nl, None)
    dmerged = _mm(gxb, p["attn_w_o"], tb=True, name="attn_dmerged")
    grads["attn_w_o"] = _mm(merged2, gxb, ta=True, tm=256, tn=256, tk=4096, out_dtypes=(BF16,), out=_grad_cols(ATT_W, Dq), name="attn_dwo")
    dml, _ = _rowwise_vjp(_f_merge, o_tok + l_tok, [], [dmerged], [F32] * 6, tr=TR, name="attn_merge_bwd")
    pieces = [[None] * ng for _ in range(3)]
    for gi, dil in enumerate(ATT_DILS):
        arr, offs = att_in[gi]
        dqkv_g = _att_bwd(arr, offs, _deinterleave(dml[gi], B, S, dil), lses[gi],
                          _stats_from_tokens(dml[ng + gi], B, S, dil), nb=S // dil // ATT_BLK,
                          nbk=ATT_BLOCKS_PER_STEP,
                          name=f"attn_bwd_{gi}")
        for i in range(3):
            pieces[i][gi] = _interleave(dqkv_g[i], B, S, dil)
    dqkv = jnp.concatenate([pieces[i][gi] for i in range(3) for gi in range(ng)], axis=1)
    qkv_w = 3 * ng * ATT_W
    grads["attn_w_qkv"] = _mm(h3, dqkv, ta=True, tm=512, tn=1152, tk=4096, out_dtypes=(BF16,), out=_grad_cols(D, qkv_w // N_CHIPS),
                              name="attn_dwqkv")
    tok = grads_done({n: grads[n] for n in ("attn_w_qkv", "attn_w_o")})
    gx, gxb, d_norm_mix[3] = _mm_rms_bwd(dqkv, p["attn_w_qkv_plain"], x3_, g3, gx, tm=512, tk=qkv_w, after=tok, name="attn_dh")

    gx, gxb, d_norm_mlp[2], mlp_hi = _mlp_bwd(
        gx, gxb, x_mid2, row(p["norm_mlp"][2]), mlp_in[2], mlp_out[2], mlp_saved2, 2, nl, mlp_hi)
    tok = grads_done({"mlp_w_in": (1, mlp_hi[0]), "mlp_w_out": (1, mlp_hi[1])})
    duv = _mm(gxb, p["gmlp_w_out"], tb=True, after=tok, name="gmlp_duv")
    grads["gmlp_w_out"] = _mm(uv, gxb, ta=True, tm=128, tn=1024, tk=4096, out_dtypes=(BF16,), out=_grad_rows(Dq, D), name="gmlp_dwout")
    du, dvn, dws, dbcol = _gmlp_bwd(duv, gu, gvn, ws_b, bcol, nck=4, name="gmlp_spatial_bwd")
    grads["gmlp_w_s"] = jnp.where(causal[None], dws, 0.0)[None]
    grads["gmlp_b_s"] = dbcol[:, :GMLP_HEADS].T[None]
    (dz2,), (dlg, dlb_) = _rowwise_vjp(_f_gelu_ln, [z2], gl_par, [du, dvn], [BF16], tr=TR, name="gmlp_gelu_ln_bwd")
    grads["gmlp_ln_g"], grads["gmlp_ln_b"] = dlg, dlb_
    grads["gmlp_w_in"] = _mm(h2, dz2, ta=True, tm=512, tn=512, tk=4096, out_dtypes=(BF16,), out=_grad_cols(D, 2 * Dq), name="gmlp_dwin")
    tok = grads_done({n: grads[n] for n in ("gmlp_w_in", "gmlp_w_out")})
    gx, gxb, d_norm_mix[2] = _mm_rms_bwd(dz2, getattr(p["gmlp_w_in"], "plain", p["gmlp_w_in"]), x2, g2, gx, tm=512, tk=2 * D, after=tok,
                                         name="gmlp_dh")

    gx, gxb, d_norm_mlp[1], mlp_lo = _mlp_bwd(
        gx, gxb, x_mid1, row(p["norm_mlp"][1]), mlp_in[1], mlp_out[1], mlp_saved1, 1, nl, None)
    dqc = _mm(gxb, p["conv_w_pw2"], tb=True, name="conv_dq")
    grads["conv_w_pw2"] = _mm(qc, gxb, ta=True, tm=128, tn=1024, tk=4096, out_dtypes=(BF16,), out=_grad_rows(Dq, D), name="conv_dwpw2")
    _, (db2,) = _rowwise_vjp(lambda t, b: (t + b,), [gx], [p["conv_b_pw2"]], [gx], [None], tr=TR, name="conv_db2")
    grads["conv_b_pw2"] = db2
    (dyc,), (dbdw, dcg, dcb) = _rowwise_vjp(_f_ln_silu, [yc], ln_par, [dqc], [F32], tr=TR, name="conv_ln_silu_bwd")
    grads["conv_b_dw"], grads["conv_ln_g"], grads["conv_ln_b"] = dbdw, dcg, dcb
    dyp = jnp.pad(dyc.reshape(B, S, D), ((0, 0), (0, CONV_HALO), (0, 0)))
    dzg, dwdw = _conv_bwd(zp, dyp, w_dw, R=256, tc=128, name="conv_dw_bwd")
    grads["conv_w_dw"] = dwdw[:CONV_WIDTH][None]
    (dz1,), (db1,) = _rowwise_vjp(_f_bias_glu, [z1], [p["conv_b_pw1"]], [dzg.reshape(T, D)], [BF16], tr=TR,
                                  name="conv_glu_bwd")
    grads["conv_b_pw1"] = db1
    grads["conv_w_pw1"] = _mm(h1, dz1, ta=True, tm=512, tn=512, tk=4096, out_dtypes=(BF16,), out=_grad_cols(D, 2 * Dq), name="conv_dwpw1")
    tok = grads_done({n: grads[n] for n in ("conv_w_pw1", "conv_w_pw2")})
    gx, gxb, d_norm_mix[1] = _mm_rms_bwd(dz1, getattr(p["conv_w_pw1"], "plain", p["conv_w_pw1"]), x1, g1, gx, tm=512, tk=2 * D,
                                         after=tok, name="conv_dh")

    gx, gxb, d_norm_mlp[0], mlp_lo = _mlp_bwd(
        gx, gxb, x_mid0, row(p["norm_mlp"][0]), mlp_in[0], mlp_out[0], mlp_saved0, 0, nl, mlp_lo)
    tok = grads_done({"mlp_w_in": (0, mlp_lo[0]), "mlp_w_out": (0, mlp_lo[1])})
    (dz0,), _ = _rowwise_vjp(_f_glu, [z0], [], [gx], [BF16], after=tok, tr=TR, name="s5_glu_bwd")
    grads["ssm_w_glu"] = _mm(yb, dz0, ta=True, tm=512, tn=512, tk=4096, out_dtypes=(BF16,), out=_grad_cols(D, 2 * Dq), name="s5_dwglu")
    tok = grads_done({"ssm_w_glu": grads["ssm_w_glu"]})
    dypre = _mm(dz0, getattr(w_glu, "plain", w_glu), tb=True, tm=512, tk=2 * D,
                epi=lambda acc, yp: (jax.vjp(lambda t: jax.nn.gelu(t), yp)[1](acc)[0],), extras=(ypre,), name="s5_dypre")
    dh0, dbre, dbim, dcre, dcim, dabr, dabi, dd = _s5_bwd(
        dypre, h0, sxr, sxi, ser, sei, abr, abi, bre_b, bim_b, cre_b, cim_b, d_skip, B=B, L=S5_CHUNK, name="s5_bwd")
    s5_grads = s5_vjp((dabr, dabi, dbre, dbim, dcre, dcim))
    for nm, gv in zip(("ssm_a_re", "ssm_a_im", "ssm_b_re", "ssm_b_im", "ssm_c_re", "ssm_c_im", "ssm_log_dt"), s5_grads):
        grads[nm] = gv[None]
    grads["ssm_d"] = dd
    gx, _, d_norm_mix[0] = _rms_bwd(x, g0, dh0, gx, "rms_mix_bwd_0", after=tok)

    grads["norm_mix"] = jnp.concatenate(d_norm_mix, axis=0)
    grads["norm_mlp"] = jnp.concatenate(d_norm_mlp, axis=0)
    grads["mlp_w_in"], grads["mlp_w_out"] = (mlp_lo[0], mlp_hi[0]), (mlp_lo[1], mlp_hi[1])
    return loss_part, gx.reshape(B, S, D), grads


WEIGHTS = ['norm_mix', 'norm_mlp', 'norm_final', 'ssm_a_re', 'ssm_a_im', 'ssm_b_re', 'ssm_b_im', 'ssm_c_re',
           'ssm_c_im', 'ssm_d', 'ssm_log_dt', 'ssm_w_glu', 'conv_w_pw1', 'conv_b_pw1', 'conv_w_dw', 'conv_b_dw',
           'conv_ln_g', 'conv_ln_b', 'conv_w_pw2', 'conv_b_pw2', 'gmlp_w_in', 'gmlp_ln_g', 'gmlp_ln_b', 'gmlp_w_s',
           'gmlp_b_s', 'gmlp_w_out', 'attn_w_qkv', 'attn_w_o', 'mlp_w_in', 'mlp_w_out']
BIG_AXIS = {'ssm_w_glu': -1, 'conv_w_pw1': -1, 'conv_w_pw2': -2, 'gmlp_w_in': -1, 'gmlp_w_out': -2,
            'attn_w_qkv': -1, 'attn_w_o': -1, 'mlp_w_in': -1, 'mlp_w_out': -2}
BIG = list(BIG_AXIS)
LAYER_MIXER_WEIGHTS = (('ssm_w_glu',), ('conv_w_pw1', 'conv_w_pw2'), ('gmlp_w_in', 'gmlp_w_out'), ('attn_w_qkv', 'attn_w_o'))
SMALL_SHARDED = ['conv_b_pw1', 'conv_w_dw', 'conv_b_dw', 'conv_ln_g', 'conv_ln_b', 'conv_b_pw2', 'gmlp_ln_g', 'gmlp_ln_b']
SMALL_REPL = [n for n in WEIGHTS if n not in BIG_AXIS and n not in SMALL_SHARDED]
SMALL = SMALL_REPL + SMALL_SHARDED
LANES = 128


def _pack(arrs, cols, row_mult):
    flat = jnp.concatenate([a.reshape(-1) for a in arrs])
    per = cols * row_mult
    n = -(-flat.shape[0] // per) * per
    return jnp.pad(flat, (0, n - flat.shape[0])).reshape(n // cols, cols)


def _unpack(flat2d, shapes):
    flat = flat2d.reshape(-1)
    out, off = [], 0
    for s in shapes:
        n = int(np.prod(s))
        out.append(flat[off:off + n].reshape(s))
        off += n
    return out


def kernel(x, norm_mix, norm_mlp, norm_final, ssm_a_re, ssm_a_im, ssm_b_re, ssm_b_im, ssm_c_re, ssm_c_im, ssm_d, ssm_log_dt, ssm_w_glu, conv_w_pw1, conv_b_pw1, conv_w_dw, conv_b_dw, conv_ln_g, conv_ln_b, conv_w_pw2, conv_b_pw2, gmlp_w_in, gmlp_ln_g, gmlp_ln_b, gmlp_w_s, gmlp_b_s, gmlp_w_out, attn_w_qkv, attn_w_o, mlp_w_in, mlp_w_out, loss_target, m_norm_mix, m_norm_mlp, m_norm_final, m_ssm_a_re, m_ssm_a_im, m_ssm_b_re, m_ssm_b_im, m_ssm_c_re, m_ssm_c_im, m_ssm_d, m_ssm_log_dt, m_ssm_w_glu, m_conv_w_pw1, m_conv_b_pw1, m_conv_w_dw, m_conv_b_dw, m_conv_ln_g, m_conv_ln_b, m_conv_w_pw2, m_conv_b_pw2, m_gmlp_w_in, m_gmlp_ln_g, m_gmlp_ln_b, m_gmlp_w_s, m_gmlp_b_s, m_gmlp_w_out, m_attn_w_qkv, m_attn_w_o, m_mlp_w_in, m_mlp_w_out, v_norm_mix, v_norm_mlp, v_norm_final, v_ssm_a_re, v_ssm_a_im, v_ssm_b_re, v_ssm_b_im, v_ssm_c_re, v_ssm_c_im, v_ssm_d, v_ssm_log_dt, v_ssm_w_glu, v_conv_w_pw1, v_conv_b_pw1, v_conv_w_dw, v_conv_b_dw, v_conv_ln_g, v_conv_ln_b, v_conv_w_pw2, v_conv_b_pw2, v_gmlp_w_in, v_gmlp_ln_g, v_gmlp_ln_b, v_gmlp_w_s, v_gmlp_b_s, v_gmlp_w_out, v_attn_w_qkv, v_attn_w_o, v_mlp_w_in, v_mlp_w_out):
    args = dict(locals())
    w = {n: args[n] for n in WEIGHTS}
    m = {n: args["m_" + n] for n in WEIGHTS}
    v = {n: args["v_" + n] for n in WEIGHTS}
    chip = 2 * lax.axis_index("x") + lax.axis_index("y")
    core = lax.axis_index("c")

    sm_shapes = [w[n].shape for n in SMALL_SHARDED]
    sflat = _pack([w[n] for n in SMALL_SHARDED], LANES, 8)
    rs = sflat.shape[0]
    sall = _allgather8(sflat, name="gather_small")
    started, token = [], sall
    sall = sall.reshape(8, rs, LANES)
    for li, mixer in enumerate(LAYER_MIXER_WEIGHTS):
        names = list(mixer) + ["mlp_w_in", "mlp_w_out"]
        shards = [w[n][0] for n in mixer] + [w["mlp_w_in"][li], w["mlp_w_out"][li]]
        halves = [s.astype(BF16).reshape((2, s.shape[0] // 2) + s.shape[1:]) for s in shards]
        send_sems, recv_sems, halves, lands, token = _gather_start(halves, token, name=f"gather_start_{li}")
        started.append((names, (send_sems, recv_sems, halves, lands)))

    def layer_weights(li, after):
        names, st = started[li]
        halves, lands = _gather_wait(st, after, name=f"gather_wait_{li}")
        out = {}
        for n, h, arr in zip(names, halves, lands):
            arr = lax.dynamic_update_index_in_dim(arr, h, chip, axis=0)
            arr = arr.reshape((N_CHIPS, arr.shape[1] * arr.shape[2]) + arr.shape[3:])
            if BIG_AXIS[n] == -1:
                out[n] = _Stored(arr, "cols")
                if n in ("attn_w_qkv", "attn_w_o"):
                    out[n + "_plain"] = jnp.swapaxes(arr, 0, 1).reshape(arr.shape[1], -1)
                out[n].plain = jnp.swapaxes(arr, 0, 1).reshape(arr.shape[1], -1)
            else:
                out[n] = arr.reshape(-1, arr.shape[-1])
        return out

    p = {}
    per_chip = [_unpack(sall[2 * k], sm_shapes) for k in range(N_CHIPS)]
    for i, n in enumerate(SMALL_SHARDED):
        p[n] = jnp.concatenate([per_chip[k][i] for k in range(N_CHIPS)], axis=-1)
    for n in SMALL_REPL:
        p[n] = w[n]
    p['conv_w_dw'] = p['conv_w_dw'][0]

    in_flight, arrived, n_rounds = [], {}, [0]

    def finish_round(after):
        k, names, plans, st = in_flight.pop(0)
        srcs, lands = _reduce_wait(st, plans, after, name=f"grads_wait_{k}")
        for n, plan, src, land in zip(names, plans, srcs, lands):
            arrived.setdefault(n, []).append((plan, src, land))

    def grads_done(group):
        names, srcs, plans, lands = [], [], [], []
        for n, v in group.items():
            if isinstance(v, tuple):
                src, plan = v[1].reshape(1, N_CHIPS, -1, v[1].shape[-1]), ((0, v[0]),)
                while any(n in rd[1] for rd in in_flight):
                    finish_round(src)
            else:
                src, plan = v.reshape(2, N_CHIPS, -1, v.shape[-1]), ((0, 0), (1, 1))
            land = arrived[n][-1][2] if n in arrived else lax.empty((2 * N_CHIPS,) + src.shape[2:], BF16)
            names.append(n), srcs.append(src), plans.append(plan), lands.append(land)
        st = _reduce_start(srcs, lands, plans, None, name=f"grads_start_{n_rounds[0]}")
        in_flight.append((n_rounds[0], names, plans, st[:4]))
        n_rounds[0] += 1
        return st[4]

    loss_part, grad_x, g = _local_step(x, loss_target, p, layer_weights, token, grads_done)
    loss = lax.psum(loss_part[0, 0], ("x", "y", "c"))

    my_id = 2 * chip + core
    while in_flight:
        finish_round(grad_x)
    totals = []
    for n in BIG:
        own = None
        for plan, src, land in arrived[n]:
            slab = lax.dynamic_index_in_dim(src, chip, axis=1, keepdims=False)
            if len(plan) == 2:
                own = lax.dynamic_index_in_dim(slab, core, axis=0, keepdims=False)
            else:
                own = slab[0] if own is None else jnp.where(core == plan[0][1], slab[0], own)
        totals.append(_sum8(arrived[n][-1][2], own, my_id, name="owner_sum_" + n))
    shared = _share_halves(totals, name="grads_share_halves")
    big_grads = {}
    for n, arr, t in zip(BIG, shared, totals):
        arr = lax.dynamic_update_index_in_dim(arr, t[None], core, axis=0)
        big_grads[n] = arr.reshape(w[n].shape)

    small_full_shapes = [g[n].shape for n in SMALL]
    gs = _pack([g[n] for n in SMALL], LANES, 8)
    rg = gs.shape[0]
    gs_all = _allgather8(gs, name="gather_small_grads").reshape(8, rg, LANES)
    gs_sum = _rowwise(lambda *a: (functools.reduce(lambda s, t: s + t, a),), [gs_all[k] for k in range(8)], [], (F32,),
                      tr=rg, name="small_grads_sum")[0]
    small_grads = dict(zip(SMALL, _unpack(gs_sum, small_full_shapes)))
    for n in SMALL:
        small_grads[n] = small_grads[n].reshape(p_shape_full(w[n], -1 if n in SMALL_SHARDED else None))
    for n in SMALL_SHARDED:
        width = w[n].shape[-1]
        small_grads[n] = lax.dynamic_slice_in_dim(small_grads[n], chip * width, width, axis=-1)

    grad, delta, new_m, new_v = {}, {}, {}, {}
    for n in BIG:
        shape = w[n].shape
        two_d = lambda t: t.reshape(-1, shape[-1])
        grad[n] = big_grads[n]
        d_, m_, v_ = _adamw(two_d(w[n]), two_d(grad[n]), two_d(m[n]), two_d(v[n]), name="adamw_" + n)
        delta[n], new_m[n], new_v[n] = d_.reshape(shape), m_.reshape(shape), v_.reshape(shape)
    for n in SMALL:
        shape = w[n].shape
        two_d = lambda t: t.reshape(-1, shape[-1])
        grad[n] = small_grads[n]
        d_, m_, v_ = _adamw(two_d(w[n]), two_d(grad[n]), two_d(m[n]), two_d(v[n]), name="adamw_" + n)
        delta[n], new_m[n], new_v[n] = d_.reshape(shape), m_.reshape(shape), v_.reshape(shape)

    return (loss, grad_x, *[grad[n] for n in WEIGHTS], *[delta[n] for n in WEIGHTS],
            *[new_m[n] for n in WEIGHTS], *[new_v[n] for n in WEIGHTS])


def p_shape_full(shard, axis):
    s = list(shard.shape)
    if axis is not None:
        s[axis] *= N_CHIPS
    return tuple(s)
```
